```python
import jax, jax.numpy as jnp
from jax import lax
import numpy as np

D_MODEL = 1024
BATCH = 8
SEQ = 2048
DEPTH = 4

EXPAND = 2
E_INNER = EXPAND * D_MODEL
HEAD_DIM = 128
E_A = E_INNER // 2
E_B = E_INNER - E_A
H_A = E_A // HEAD_DIM
H_B = E_B // HEAD_DIM
CONV_WIDTH = 3
CHUNK = 128
AB_SPLITS = (E_A, E_A, E_A, E_A, E_B, E_B, E_B)
AB_IN = sum(AB_SPLITS)
E_C = E_INNER
POOL_WINDOWS = (2, 4, 8, 16)
N_POOL_GROUPS = len(POOL_WINDOWS)
G_C = E_C // N_POOL_GROUPS
N_EVEN = (DEPTH + 1) // 2
N_ODD = DEPTH // 2
EPS = 1e-6

kernel_name = "hybrid_shortconv_sgu_pool_adaln_trunk"


def rmsnorm(x, g):
    x32 = x.astype(jnp.float32)
    y = x32 * lax.rsqrt(jnp.mean(x32 * x32, axis=-1, keepdims=True) + EPS)
    return (y * g.astype(jnp.float32)).astype(x.dtype)


def modulate(h, shift, scale):
    return h * (1 + scale[:, None, :]) + shift[:, None, :]


def causal_short_conv(x, w):
    S = x.shape[1]
    xp = jnp.pad(x, ((0, 0), (CONV_WIDTH - 1, 0), (0, 0)))
    y = xp[:, 0:S] * w[0]
    for k in range(1, CONV_WIDTH):
        y = y + xp[:, k:k + S] * w[k]
    return y


def chunked_sgu(u, v, ln_g, ln_b, w_s, b_s):
    Bn, S, _ = v.shape
    n_chunks = S // CHUNK
    v32 = v.astype(jnp.float32).reshape(Bn, S, H_B, HEAD_DIM)
    mu = jnp.mean(v32, axis=-1, keepdims=True)
    var = jnp.mean(jnp.square(v32 - mu), axis=-1, keepdims=True)
    vn = ((v32 - mu) * lax.rsqrt(var + EPS)).reshape(Bn, S, E_B)
    vn = (vn * ln_g.astype(jnp.float32) + ln_b.astype(jnp.float32)).astype(v.dtype)
    vn = vn.reshape(Bn, n_chunks, CHUNK, H_B, HEAD_DIM)
    causal = jnp.tril(jnp.ones((CHUNK, CHUNK), dtype=bool))
    w_masked = jnp.where(causal[None], w_s, jnp.zeros_like(w_s))
    mixed = jnp.einsum('hts,bnshd->bnthd', w_masked, vn) + b_s.T[None, None, :, :, None]
    return u * mixed.reshape(Bn, S, E_B)


def multiscale_pool(p):
    S = p.shape[1]
    p32 = p.astype(jnp.float32)
    cs = jnp.cumsum(p32, axis=1)
    outs = []
    for gi, win in enumerate(POOL_WINDOWS):
        sl = slice(gi * G_C, (gi + 1) * G_C)
        csg = cs[..., sl]
        prev = jnp.pad(csg, ((0, 0), (win, 0), (0, 0)))[:, :S]
        cnt = jnp.minimum(jnp.arange(1, S + 1), win).astype(jnp.float32)[None, :, None]
        outs.append((csg - prev) / cnt - p32[..., sl])
    return jnp.stack(outs, axis=2).astype(p.dtype)


def even_mixer(h, w_in, conv_w, ln_g, ln_b, w_s, b_s, w_out):
    proj = h @ w_in
    idx = [int(i) for i in np.cumsum(AB_SPLITS)[:-1]]
    a_h, a_b, a_c, a_z, b_u, b_v, b_z = jnp.split(proj, idx, axis=-1)
    y_a = a_b * causal_short_conv(a_c * a_h, conv_w)
    y_a = y_a * jax.nn.silu(a_z)
    y_b = chunked_sgu(b_u, b_v, ln_g, ln_b, w_s, b_s)
    y_b = y_b * jax.nn.silu(b_z)
    return jnp.concatenate([y_a, y_b], axis=-1) @ w_out


def odd_mixer(h, w_in, pool_w, pool_scale, w_out):
    Bn, S, _ = h.shape
    proj = h @ w_in
    p, z = jnp.split(proj, 2, axis=-1)
    pooled = multiscale_pool(p)
    y = jnp.einsum('bsgi,gio->bsgo', pooled, pool_w).reshape(Bn, S, E_C)
    y = y * pool_scale * jax.nn.silu(z)
    return y @ w_out


def _fwd_setup_inputs(seed: int = 0) -> dict:
    key = jax.random.key(seed)
    ks = jax.random.split(key, 20)
    nrm = jax.random.normal
    f32 = jnp.float32
    return {
        "x": nrm(ks[0], (BATCH, SEQ, D_MODEL), f32),
        "c": nrm(ks[1], (BATCH, D_MODEL), f32),
        "norm_g": 1.0 + 0.1 * nrm(ks[2], (DEPTH, D_MODEL), f32),
        "ada_w": nrm(ks[3], (DEPTH, D_MODEL, 3 * D_MODEL), f32) * D_MODEL ** -0.5,
        "ada_b": 0.01 * nrm(ks[4], (DEPTH, 3 * D_MODEL), f32),
        "ab_w_in": nrm(ks[5], (N_EVEN, D_MODEL, AB_IN), f32) * D_MODEL ** -0.5,
        "ab_conv_w": nrm(ks[6], (N_EVEN, CONV_WIDTH, E_A), f32) * CONV_WIDTH ** -0.5,
        "ab_ln_g": 1.0 + 0.1 * nrm(ks[7], (N_EVEN, E_B), f32),
        "ab_ln_b": 0.02 * nrm(ks[8], (N_EVEN, E_B), f32),
        "ab_sgu_w": nrm(ks[9], (N_EVEN, H_B, CHUNK, CHUNK), f32) * CHUNK ** -0.5,
        "ab_sgu_b": 1.0 + 0.1 * nrm(ks[10], (N_EVEN, H_B, CHUNK), f32),
        "ab_w_out": nrm(ks[11], (N_EVEN, E_A + E_B, D_MODEL), f32) * (E_A + E_B) ** -0.5,
        "c_w_in": nrm(ks[12], (N_ODD, D_MODEL, 2 * E_C), f32) * D_MODEL ** -0.5,
        "c_pool_w": nrm(ks[13], (N_ODD, N_POOL_GROUPS, G_C, G_C), f32) * G_C ** -0.5,
        "c_pool_scale": 1.0 + 0.1 * nrm(ks[14], (N_ODD, E_C), f32),
        "c_w_out": nrm(ks[15], (N_ODD, E_C, D_MODEL), f32) * E_C ** -0.5,
        "final_g": 1.0 + 0.1 * nrm(ks[16], (D_MODEL,), f32),
    }


def _fwd_reference(x, c, norm_g, ada_w, ada_b, ab_w_in, ab_conv_w, ab_ln_g, ab_ln_b,
              ab_sgu_w, ab_sgu_b, ab_w_out, c_w_in, c_pool_w, c_pool_scale,
              c_w_out, final_g):
    c_act = jax.nn.silu(c)
    for i in range(DEPTH):
        mod = c_act @ ada_w[i] + ada_b[i]
        shift, scale, gate = jnp.split(mod, 3, axis=-1)
        h = modulate(rmsnorm(x, norm_g[i]), shift, scale)
        j = i // 2
        if i % 2 == 0:
            out = even_mixer(h, ab_w_in[j], ab_conv_w[j], ab_ln_g[j], ab_ln_b[j],
                             ab_sgu_w[j], ab_sgu_b[j], ab_w_out[j])
        else:
            out = odd_mixer(h, c_w_in[j], c_pool_w[j], c_pool_scale[j], c_w_out[j])
        x = x + gate[:, None, :] * out
    return rmsnorm(x, final_g)


import jax as _jax
import jax.numpy as _jnp

TWIN_FORMAT = 'train_step'
FWD_PARAMS = ['x', 'c', 'norm_g', 'ada_w', 'ada_b', 'ab_w_in', 'ab_conv_w', 'ab_ln_g', 'ab_ln_b', 'ab_sgu_w', 'ab_sgu_b', 'ab_w_out', 'c_w_in', 'c_pool_w', 'c_pool_scale', 'c_w_out', 'final_g']
TWIN_WEIGHTS = ['norm_g', 'ada_w', 'ada_b', 'ab_w_in', 'ab_conv_w', 'ab_ln_g', 'ab_ln_b', 'ab_sgu_w', 'ab_sgu_b', 'ab_w_out', 'c_w_in', 'c_pool_w', 'c_pool_scale', 'c_w_out', 'final_g']
TWIN_DIFF_INPUT = 'x'
TWIN_INPUTS = ['x', 'c', 'norm_g', 'ada_w', 'ada_b', 'ab_w_in', 'ab_conv_w', 'ab_ln_g', 'ab_ln_b', 'ab_sgu_w', 'ab_sgu_b', 'ab_w_out', 'c_w_in', 'c_pool_w', 'c_pool_scale', 'c_w_out', 'final_g', 'loss_target', 'm_norm_g', 'm_ada_w', 'm_ada_b', 'm_ab_w_in', 'm_ab_conv_w', 'm_ab_ln_g', 'm_ab_ln_b', 'm_ab_sgu_w', 'm_ab_sgu_b', 'm_ab_w_out', 'm_c_w_in', 'm_c_pool_w', 'm_c_pool_scale', 'm_c_w_out', 'm_final_g', 'v_norm_g', 'v_ada_w', 'v_ada_b', 'v_ab_w_in', 'v_ab_conv_w', 'v_ab_ln_g', 'v_ab_ln_b', 'v_ab_sgu_w', 'v_ab_sgu_b', 'v_ab_w_out', 'v_c_w_in', 'v_c_pool_w', 'v_c_pool_scale', 'v_c_w_out', 'v_final_g']
TWIN_OUTPUTS = ['loss', 'grad_x', 'grad_norm_g', 'grad_ada_w', 'grad_ada_b', 'grad_ab_w_in', 'grad_ab_conv_w', 'grad_ab_ln_g', 'grad_ab_ln_b', 'grad_ab_sgu_w', 'grad_ab_sgu_b', 'grad_ab_w_out', 'grad_c_w_in', 'grad_c_pool_w', 'grad_c_pool_scale', 'grad_c_w_out', 'grad_final_g', 'delta_norm_g', 'delta_ada_w', 'delta_ada_b', 'delta_ab_w_in', 'delta_ab_conv_w', 'delta_ab_ln_g', 'delta_ab_ln_b', 'delta_ab_sgu_w', 'delta_ab_sgu_b', 'delta_ab_w_out', 'delta_c_w_in', 'delta_c_pool_w', 'delta_c_pool_scale', 'delta_c_w_out', 'delta_final_g', 'new_m_norm_g', 'new_m_ada_w', 'new_m_ada_b', 'new_m_ab_w_in', 'new_m_ab_conv_w', 'new_m_ab_ln_g', 'new_m_ab_ln_b', 'new_m_ab_sgu_w', 'new_m_ab_sgu_b', 'new_m_ab_w_out', 'new_m_c_w_in', 'new_m_c_pool_w', 'new_m_c_pool_scale', 'new_m_c_w_out', 'new_m_final_g', 'new_v_norm_g', 'new_v_ada_w', 'new_v_ada_b', 'new_v_ab_w_in', 'new_v_ab_conv_w', 'new_v_ab_ln_g', 'new_v_ab_ln_b', 'new_v_ab_sgu_w', 'new_v_ab_sgu_b', 'new_v_ab_w_out', 'new_v_c_w_in', 'new_v_c_pool_w', 'new_v_c_pool_scale', 'new_v_c_w_out', 'new_v_final_g']
TWIN_LEAF_KINDS = {'loss': 'loss', 'grad_x': 'grad_x', 'grad_norm_g': 'grad_w', 'grad_ada_w': 'grad_w', 'grad_ada_b': 'grad_w', 'grad_ab_w_in': 'grad_w', 'grad_ab_conv_w': 'grad_w', 'grad_ab_ln_g': 'grad_w', 'grad_ab_ln_b': 'grad_w', 'grad_ab_sgu_w': 'grad_w', 'grad_ab_sgu_b': 'grad_w', 'grad_ab_w_out': 'grad_w', 'grad_c_w_in': 'grad_w', 'grad_c_pool_w': 'grad_w', 'grad_c_pool_scale': 'grad_w', 'grad_c_w_out': 'grad_w', 'grad_final_g': 'grad_w', 'delta_norm_g': 'delta_w', 'delta_ada_w': 'delta_w', 'delta_ada_b': 'delta_w', 'delta_ab_w_in': 'delta_w', 'delta_ab_conv_w': 'delta_w', 'delta_ab_ln_g': 'delta_w', 'delta_ab_ln_b': 'delta_w', 'delta_ab_sgu_w': 'delta_w', 'delta_ab_sgu_b': 'delta_w', 'delta_ab_w_out': 'delta_w', 'delta_c_w_in': 'delta_w', 'delta_c_pool_w': 'delta_w', 'delta_c_pool_scale': 'delta_w', 'delta_c_w_out': 'delta_w', 'delta_final_g': 'delta_w', 'new_m_norm_g': 'new_m', 'new_m_ada_w': 'new_m', 'new_m_ada_b': 'new_m', 'new_m_ab_w_in': 'new_m', 'new_m_ab_conv_w': 'new_m', 'new_m_ab_ln_g': 'new_m', 'new_m_ab_ln_b': 'new_m', 'new_m_ab_sgu_w': 'new_m', 'new_m_ab_sgu_b': 'new_m', 'new_m_ab_w_out': 'new_m', 'new_m_c_w_in': 'new_m', 'new_m_c_pool_w': 'new_m', 'new_m_c_pool_scale': 'new_m', 'new_m_c_w_out': 'new_m', 'new_m_final_g': 'new_m', 'new_v_norm_g': 'new_v', 'new_v_ada_w': 'new_v', 'new_v_ada_b': 'new_v', 'new_v_ab_w_in': 'new_v', 'new_v_ab_conv_w': 'new_v', 'new_v_ab_ln_g': 'new_v', 'new_v_ab_ln_b': 'new_v', 'new_v_ab_sgu_w': 'new_v', 'new_v_ab_sgu_b': 'new_v', 'new_v_ab_w_out': 'new_v', 'new_v_c_w_in': 'new_v', 'new_v_c_pool_w': 'new_v', 'new_v_c_pool_scale': 'new_v', 'new_v_c_w_out': 'new_v', 'new_v_final_g': 'new_v'}


def _forward(args):
    return _fwd_reference(*[args[k] for k in FWD_PARAMS])


def _output_shape():
    out = _jax.eval_shape(lambda: _forward(_fwd_setup_inputs(0)))
    return out.shape, out.dtype

N_MICROBATCH = 1
ADAM_LR = 0.001
ADAM_B1 = 0.9
ADAM_B2 = 0.999
ADAM_EPS = 1e-08
ADAM_WD = 0.01
ADAM_STEP = 10
PER_EXAMPLE_BATCH_AXIS = {'x': 0, 'c': 0, 'loss_target': 0}
SHARED_INPUTS = []
_WEIGHT_DTYPES = {'norm_g': _jnp.float32, 'ada_w': _jnp.float32, 'ada_b': _jnp.float32, 'ab_w_in': _jnp.float32, 'ab_conv_w': _jnp.float32, 'ab_ln_g': _jnp.float32, 'ab_ln_b': _jnp.float32, 'ab_sgu_w': _jnp.float32, 'ab_sgu_b': _jnp.float32, 'ab_w_out': _jnp.float32, 'c_w_in': _jnp.float32, 'c_pool_w': _jnp.float32, 'c_pool_scale': _jnp.float32, 'c_w_out': _jnp.float32, 'final_g': _jnp.float32}
MOMENT_SCALE = {'norm_g': 1.318681e-01, 'ada_w': 1.713417e-01, 'ada_b': 3.076071e-01, 'ab_w_in': 7.428219e-02, 'ab_conv_w': 8.461273e-02, 'ab_ln_g': 3.387663e-02, 'ab_ln_b': 3.426154e-02, 'ab_sgu_w': 3.391855e-02, 'ab_sgu_b': 4.873149e-02, 'ab_w_out': 1.071268e-01, 'c_w_in': 3.485294e-02, 'c_pool_w': 3.435359e-02, 'c_pool_scale': 3.448956e-02, 'c_w_out': 4.877239e-02, 'final_g': 1.675542e+01}


def _to_microbatches(a, axis):
    t = _jnp.moveaxis(a, axis, 0)
    t = t.reshape((N_MICROBATCH, t.shape[0] // N_MICROBATCH) + t.shape[1:])
    return _jnp.moveaxis(t, 1, axis + 1)


def setup_inputs(seed: int = 0) -> dict:
    inp = _fwd_setup_inputs(seed)
    key = _jax.random.fold_in(_jax.random.key(seed), 7919)
    shape, _ = _output_shape()
    out = dict(inp)
    out["loss_target"] = _jax.random.normal(_jax.random.fold_in(key, 0), shape, _jnp.float32)
    for i, name in enumerate(TWIN_WEIGHTS):
        w = inp[name].astype(_jnp.float32)
        if MOMENT_SCALE is None:
            s = _jnp.sqrt(_jnp.mean(_jnp.square(w)) + 1e-30)
        else:
            s = MOMENT_SCALE[name]
        km, kv = _jax.random.split(_jax.random.fold_in(key, i + 1))
        out[name] = w
        out["m_" + name] = s * _jax.random.normal(km, w.shape, _jnp.float32)
        out["v_" + name] = (s * s) * _jax.random.uniform(kv, w.shape, _jnp.float32, 0.5, 1.5)
    if N_MICROBATCH > 1:
        for name, axis in PER_EXAMPLE_BATCH_AXIS.items():
            out[name] = _to_microbatches(out[name], axis)
    return {'x': out['x'], 'c': out['c'], 'norm_g': out['norm_g'], 'ada_w': out['ada_w'], 'ada_b': out['ada_b'], 'ab_w_in': out['ab_w_in'], 'ab_conv_w': out['ab_conv_w'], 'ab_ln_g': out['ab_ln_g'], 'ab_ln_b': out['ab_ln_b'], 'ab_sgu_w': out['ab_sgu_w'], 'ab_sgu_b': out['ab_sgu_b'], 'ab_w_out': out['ab_w_out'], 'c_w_in': out['c_w_in'], 'c_pool_w': out['c_pool_w'], 'c_pool_scale': out['c_pool_scale'], 'c_w_out': out['c_w_out'], 'final_g': out['final_g'], 'loss_target': out['loss_target'], 'm_norm_g': out['m_norm_g'], 'm_ada_w': out['m_ada_w'], 'm_ada_b': out['m_ada_b'], 'm_ab_w_in': out['m_ab_w_in'], 'm_ab_conv_w': out['m_ab_conv_w'], 'm_ab_ln_g': out['m_ab_ln_g'], 'm_ab_ln_b': out['m_ab_ln_b'], 'm_ab_sgu_w': out['m_ab_sgu_w'], 'm_ab_sgu_b': out['m_ab_sgu_b'], 'm_ab_w_out': out['m_ab_w_out'], 'm_c_w_in': out['m_c_w_in'], 'm_c_pool_w': out['m_c_pool_w'], 'm_c_pool_scale': out['m_c_pool_scale'], 'm_c_w_out': out['m_c_w_out'], 'm_final_g': out['m_final_g'], 'v_norm_g': out['v_norm_g'], 'v_ada_w': out['v_ada_w'], 'v_ada_b': out['v_ada_b'], 'v_ab_w_in': out['v_ab_w_in'], 'v_ab_conv_w': out['v_ab_conv_w'], 'v_ab_ln_g': out['v_ab_ln_g'], 'v_ab_ln_b': out['v_ab_ln_b'], 'v_ab_sgu_w': out['v_ab_sgu_w'], 'v_ab_sgu_b': out['v_ab_sgu_b'], 'v_ab_w_out': out['v_ab_w_out'], 'v_c_w_in': out['v_c_w_in'], 'v_c_pool_w': out['v_c_pool_w'], 'v_c_pool_scale': out['v_c_pool_scale'], 'v_c_w_out': out['v_c_w_out'], 'v_final_g': out['v_final_g']}


def _loss(weights, diff, rest, loss_target):
    with _jax.named_scope("forward"):
        args = {**rest, TWIN_DIFF_INPUT: diff, **{k: w.astype(_WEIGHT_DTYPES[k]) for k, w in weights.items()}}
        y = _forward(args)
    with _jax.named_scope("loss_head"):
        err = _jnp.square(y.astype(_jnp.float32) - loss_target)
        return 0.5 * _jnp.sum(_jnp.mean(err, axis=-1)) if err.ndim else 0.5 * err


def _adamw(w, g, m, v):
    m = ADAM_B1 * m + (1.0 - ADAM_B1) * g
    v = ADAM_B2 * v + (1.0 - ADAM_B2) * _jnp.square(g)
    m_hat = m / (1.0 - ADAM_B1 ** ADAM_STEP)
    v_hat = v / (1.0 - ADAM_B2 ** ADAM_STEP)
    delta = -ADAM_LR * (m_hat / (_jnp.sqrt(v_hat) + ADAM_EPS) + ADAM_WD * w)
    return delta, m, v


def reference(x, c, norm_g, ada_w, ada_b, ab_w_in, ab_conv_w, ab_ln_g, ab_ln_b, ab_sgu_w, ab_sgu_b, ab_w_out, c_w_in, c_pool_w, c_pool_scale, c_w_out, final_g, loss_target, m_norm_g, m_ada_w, m_ada_b, m_ab_w_in, m_ab_conv_w, m_ab_ln_g, m_ab_ln_b, m_ab_sgu_w, m_ab_sgu_b, m_ab_w_out, m_c_w_in, m_c_pool_w, m_c_pool_scale, m_c_w_out, m_final_g, v_norm_g, v_ada_w, v_ada_b, v_ab_w_in, v_ab_conv_w, v_ab_ln_g, v_ab_ln_b, v_ab_sgu_w, v_ab_sgu_b, v_ab_w_out, v_c_w_in, v_c_pool_w, v_c_pool_scale, v_c_w_out, v_final_g):
    given = dict(x=x, c=c, norm_g=norm_g, ada_w=ada_w, ada_b=ada_b, ab_w_in=ab_w_in, ab_conv_w=ab_conv_w, ab_ln_g=ab_ln_g, ab_ln_b=ab_ln_b, ab_sgu_w=ab_sgu_w, ab_sgu_b=ab_sgu_b, ab_w_out=ab_w_out, c_w_in=c_w_in, c_pool_w=c_pool_w, c_pool_scale=c_pool_scale, c_w_out=c_w_out, final_g=final_g, loss_target=loss_target, m_norm_g=m_norm_g, m_ada_w=m_ada_w, m_ada_b=m_ada_b, m_ab_w_in=m_ab_w_in, m_ab_conv_w=m_ab_conv_w, m_ab_ln_g=m_ab_ln_g, m_ab_ln_b=m_ab_ln_b, m_ab_sgu_w=m_ab_sgu_w, m_ab_sgu_b=m_ab_sgu_b, m_ab_w_out=m_ab_w_out, m_c_w_in=m_c_w_in, m_c_pool_w=m_c_pool_w, m_c_pool_scale=m_c_pool_scale, m_c_w_out=m_c_w_out, m_final_g=m_final_g, v_norm_g=v_norm_g, v_ada_w=v_ada_w, v_ada_b=v_ada_b, v_ab_w_in=v_ab_w_in, v_ab_conv_w=v_ab_conv_w, v_ab_ln_g=v_ab_ln_g, v_ab_ln_b=v_ab_ln_b, v_ab_sgu_w=v_ab_sgu_w, v_ab_sgu_b=v_ab_sgu_b, v_ab_w_out=v_ab_w_out, v_c_w_in=v_c_w_in, v_c_pool_w=v_c_pool_w, v_c_pool_scale=v_c_pool_scale, v_c_w_out=v_c_w_out, v_final_g=v_final_g)
    weights = {n: given[n] for n in TWIN_WEIGHTS}
    shared = {n: given[n] for n in SHARED_INPUTS}
    per_example = {n: given[n] for n in ['x', 'c']}
    grad_fn = _jax.value_and_grad(_loss, argnums=(0, 1))

    def one_microbatch(ex, loss_target):
        ex = dict(ex)
        diff = ex.pop(TWIN_DIFF_INPUT)
        return grad_fn(weights, diff, {**shared, **ex}, loss_target)

    if N_MICROBATCH == 1:
        loss, (grad_w, grad_x) = one_microbatch(per_example, given["loss_target"])
    else:
        def body(carry, xs):
            loss_sum, grad_sum = carry
            l_k, (gw_k, gx_k) = one_microbatch(xs[0], xs[1])
            with _jax.named_scope("update"):
                return (loss_sum + l_k, _jax.tree.map(_jnp.add, grad_sum, gw_k)), gx_k

        init = (_jnp.zeros((), _jnp.float32), _jax.tree.map(_jnp.zeros_like, weights))
        (loss, grad_w), grad_x = _jax.lax.scan(body, init, (per_example, given["loss_target"]))
    with _jax.named_scope("update"):
        delta_w, new_m, new_v = {}, {}, {}
        for n in TWIN_WEIGHTS:
            delta_w[n], new_m[n], new_v[n] = _adamw(weights[n], grad_w[n], given["m_" + n], given["v_" + n])
    return (loss, grad_x, *[grad_w[n] for n in TWIN_WEIGHTS], *[delta_w[n] for n in TWIN_WEIGHTS],
            *[new_m[n] for n in TWIN_WEIGHTS], *[new_v[n] for n in TWIN_WEIGHTS])
```

```python
import jax
import jax.numpy as jnp
from jax import lax
from jax.experimental import pallas as pl
from jax.experimental.pallas import tpu as pltpu

F32, BF16 = jnp.float32, jnp.bfloat16
S, D = 2048, 1024
NDEV = 8
DEPTH = 4
EPS = 1e-6
E_A = 1024
HEAD = 128
CHUNK = 128
POOL_WINDOWS = (2, 4, 8, 16)
G_C = 512
HALO = 16
ADA_NC = 384
MIB = 1024 * 1024
LANE = 128

ADAM_LR, ADAM_B1, ADAM_B2, ADAM_EPS, ADAM_WD, ADAM_STEP = 0.001, 0.9, 0.999, 1e-08, 0.01, 10

ANY = pl.BlockSpec(memory_space=pl.ANY)
VMEM_FULL = pl.BlockSpec(memory_space=pltpu.VMEM)


def _params(vmem_mib, semantics=None):
    return pltpu.CompilerParams(dimension_semantics=semantics, vmem_limit_bytes=vmem_mib * MIB)


def _silu(z):
    return z * jax.nn.sigmoid(z)


def _silu_and_grad(z):
    sig = jax.nn.sigmoid(z)
    return z * sig, sig * (1.0 + z * (1.0 - sig))


def _position():
    return lax.axis_index("x"), lax.axis_index("y"), lax.axis_index("c")


def _index(pos):
    return 4 * pos[0] + 2 * pos[1] + pos[2]


def _peer(pos, k):
    flipped = tuple(1 - p if (k >> (2 - b)) & 1 else p for b, p in enumerate(pos))
    return flipped, _index(flipped)


def _exchange(arrays, scatter, name):
    n = len(arrays)
    out_shape = [jax.ShapeDtypeStruct(a.shape if scatter else (NDEV,) + a.shape, a.dtype) for a in arrays]

    def body(*refs):
        ins, outs = refs[:n], refs[n:2 * n]
        send_sems, recv_sems, own_sems = refs[2 * n:]
        pos = _position()
        me = _index(pos)
        copies = []
        for j in range(n):
            own = pltpu.make_async_copy(ins[j].at[me] if scatter else ins[j], outs[j].at[me], own_sems.at[j])
            own.start()
            copies.append(own)
            for k in range(1, NDEV):
                peer, peer_index = _peer(pos, k)
                copy = pltpu.make_async_remote_copy(
                    src_ref=ins[j].at[peer_index] if scatter else ins[j],
                    dst_ref=outs[j].at[me],
                    send_sem=send_sems.at[j, k - 1],
                    recv_sem=recv_sems.at[j, k - 1],
                    device_id=peer,
                    device_id_type=pl.DeviceIdType.MESH,
                )
                copy.start()
                copies.append(copy)
        for copy in copies:
            copy.wait()

    return pl.pallas_call(
        body, name=name, out_shape=out_shape, in_specs=[ANY] * n, out_specs=[ANY] * n,
        scratch_shapes=[pltpu.SemaphoreType.DMA((n, NDEV - 1)), pltpu.SemaphoreType.DMA((n, NDEV - 1)),
                        pltpu.SemaphoreType.DMA((n,))],
    )(*arrays)


def _ada_forward(c, ada_w, ada_b):
    def body(c_ref, w_ref, b_ref, cact_ref, mod_ref, gbuf, modrow, send_sems, recv_sems):
        pos = _position()
        me = _index(pos)

        def to_all(ref, round_):
            copies = []
            for k in range(1, NDEV):
                peer, _ = _peer(pos, k)
                copy = pltpu.make_async_remote_copy(
                    src_ref=ref.at[me], dst_ref=ref.at[me], send_sem=send_sems.at[round_, k - 1],
                    recv_sem=recv_sems.at[round_, k - 1], device_id=peer, device_id_type=pl.DeviceIdType.MESH)
                copy.start()
                copies.append(copy)
            for copy in copies:
                copy.wait()

        cact_ref[me] = _silu(c_ref[...])
        to_all(cact_ref, 0)
        rows = lax.broadcasted_iota(jnp.int32, (NDEV, D), 0)
        cact = jnp.zeros((NDEV, D), F32)
        for e in range(NDEV):
            cact = jnp.where(rows == e, cact_ref[e], cact)
        cact = cact.astype(BF16)
        for l in range(DEPTH):
            gbuf[me, l] = jnp.dot(cact, w_ref[l].astype(BF16), preferred_element_type=F32)
        to_all(gbuf, 1)
        mine = lax.broadcasted_iota(jnp.int32, (NDEV, ADA_NC), 0) == me
        for l in range(DEPTH):
            for d in range(NDEV):
                modrow[:, d * ADA_NC:(d + 1) * ADA_NC] = jnp.sum(jnp.where(mine, gbuf[d, l], 0.0), axis=0, keepdims=True)
            full = modrow[...] + b_ref[l:l + 1, :]
            for w in range(3):
                mod_ref[l, w] = full[:, w * D:(w + 1) * D]

    return pl.pallas_call(
        body, name="ada_forward",
        out_shape=[jax.ShapeDtypeStruct((NDEV, 1, D), F32), jax.ShapeDtypeStruct((DEPTH, 3, 1, D), F32)],
        in_specs=[VMEM_FULL] * 3, out_specs=[VMEM_FULL] * 2,
        scratch_shapes=[pltpu.VMEM((NDEV, DEPTH, NDEV, ADA_NC), F32), pltpu.VMEM((1, 3 * D), F32),
                        pltpu.SemaphoreType.DMA((2, NDEV - 1)), pltpu.SemaphoreType.DMA((2, NDEV - 1))],
        compiler_params=_params(40),
    )(c, ada_w, ada_b)


def _cast_bf16(a, name):
    rows, cols = a.shape
    tr = 256

    def body(a_ref, o_ref):
        o_ref[...] = a_ref[...].astype(BF16)

    spec = pl.BlockSpec((tr, cols), lambda i: (i, 0))
    return pl.pallas_call(body, name=name, grid=(rows // tr,), out_shape=jax.ShapeDtypeStruct(a.shape, BF16),
                          in_specs=[spec], out_specs=spec, compiler_params=_params(32, ("arbitrary",)))(a)


def _mod_spec(layer, which, ngrid):
    index = {1: lambda i: (layer, which, 0, 0), 2: lambda i, j: (layer, which, 0, 0)}[ngrid]
    return pl.BlockSpec((None, None, 1, D), index)


def _norm_proj(x, mod, norm_g3, wg, layer, wl, name):
    nb = wg.shape[-1]
    tm = 512

    def body(x_ref, g_ref, shift_ref, scale_ref, w_ref, h_ref, p_ref):
        @pl.when(pl.program_id(1) == 0)
        def _():
            xv = x_ref[...]
            r = lax.rsqrt(jnp.mean(xv * xv, axis=-1, keepdims=True) + EPS)
            hn = xv * r * g_ref[...]
            h_ref[...] = (hn * (1.0 + scale_ref[...]) + shift_ref[...]).astype(BF16)

        p_ref[...] = jnp.dot(h_ref[...], w_ref[...], preferred_element_type=F32).astype(BF16)

    return pl.pallas_call(
        body, name=name, grid=(S // tm, NDEV),
        out_shape=[jax.ShapeDtypeStruct((S, D), BF16), jax.ShapeDtypeStruct((S, NDEV * nb), BF16)],
        in_specs=[pl.BlockSpec((tm, D), lambda i, d: (i, 0)),
                  pl.BlockSpec((None, 1, D), lambda i, d: (layer, 0, 0)),
                  _mod_spec(layer, 0, 2), _mod_spec(layer, 1, 2),
                  pl.BlockSpec((None, None, D, nb), lambda i, d: (d, wl, 0, 0))],
        out_specs=[pl.BlockSpec((tm, D), lambda i, d: (i, 0)), pl.BlockSpec((tm, nb), lambda i, d: (i, d))],
        compiler_params=_params(48, ("arbitrary", "arbitrary")),
    )(x, norm_g3, mod, mod, wg)


def _out_proj(ycat, wg_out, wl, x, mod, layer, name):
    tm = 512
    rb = wg_out.shape[2]

    def body(y_ref, w_ref, x_ref, gate_ref, xn_ref, o_ref):
        acc = jnp.zeros((tm, D), F32)
        for d in range(NDEV):
            acc = acc + jnp.dot(y_ref[:, d * rb:(d + 1) * rb], w_ref[d], preferred_element_type=F32)
        o_ref[...] = acc.astype(BF16)
        xn_ref[...] = x_ref[...] + gate_ref[...] * acc

    return pl.pallas_call(
        body, name=name, grid=(S // tm,),
        out_shape=[jax.ShapeDtypeStruct((S, D), F32), jax.ShapeDtypeStruct((S, D), BF16)],
        in_specs=[pl.BlockSpec((tm, NDEV * rb), lambda i: (i, 0)),
                  pl.BlockSpec((NDEV, None, rb, D), lambda i: (0, wl, 0, 0)),
                  pl.BlockSpec((tm, D), lambda i: (i, 0)), _mod_spec(layer, 2, 1)],
        out_specs=[pl.BlockSpec((tm, D), lambda i: (i, 0))] * 2,
        compiler_params=_params(48, ("arbitrary",)),
    )(ycat, wg_out, x, mod)


def _final_loss(x, target, final_g2):
    tm = 256

    def body(x_ref, t_ref, g_ref, dx_ref, loss_ref, dg_ref):
        @pl.when(pl.program_id(0) == 0)
        def _():
            loss_ref[...] = jnp.zeros_like(loss_ref)
            dg_ref[...] = jnp.zeros_like(dg_ref)

        xv, g = x_ref[...], g_ref[...]
        r = lax.rsqrt(jnp.mean(xv * xv, axis=-1, keepdims=True) + EPS)
        xn = xv * r
        err = xn * g - t_ref[...]
        loss_ref[...] += 0.5 * jnp.sum(jnp.mean(err * err, axis=-1, keepdims=True), axis=0, keepdims=True)
        dy = err * (1.0 / D)
        dg_ref[...] += jnp.sum(dy * xn, axis=0, keepdims=True)
        u = dy * g
        dx_ref[...] = r * (u - xn * jnp.mean(xn * u, axis=-1, keepdims=True))

    tile = pl.BlockSpec((tm, D), lambda i: (i, 0))
    row = pl.BlockSpec((1, D), lambda i: (0, 0))
    return pl.pallas_call(
        body, name="final_loss", grid=(S // tm,),
        out_shape=[jax.ShapeDtypeStruct((S, D), F32), jax.ShapeDtypeStruct((1, LANE), F32), jax.ShapeDtypeStruct((1, D), F32)],
        in_specs=[tile, tile, row], out_specs=[tile, pl.BlockSpec((1, LANE), lambda i: (0, 0)), row],
        compiler_params=_params(32, ("arbitrary",)),
    )(x, target, final_g2)


def _dout(dx, out, mod, layer, name):
    tm = 256

    def body(dx_ref, o_ref, gate_ref, do_ref, dgate_ref):
        @pl.when(pl.program_id(0) == 0)
        def _():
            dgate_ref[...] = jnp.zeros_like(dgate_ref)

        dxv = dx_ref[...]
        do_ref[...] = (gate_ref[...] * dxv).astype(BF16)
        dgate_ref[...] += jnp.sum(dxv * o_ref[...].astype(F32), axis=0, keepdims=True)

    tile = pl.BlockSpec((tm, D), lambda i: (i, 0))
    return pl.pallas_call(
        body, name=name, grid=(S // tm,),
        out_shape=[jax.ShapeDtypeStruct((S, D), BF16), jax.ShapeDtypeStruct((1, D), F32)],
        in_specs=[tile, tile, _mod_spec(layer, 2, 1)], out_specs=[tile, pl.BlockSpec((1, D), lambda i: (0, 0))],
        compiler_params=_params(32, ("arbitrary",)),
    )(dx, out, mod)


def _dycat(d_out, wg_out, wl, name):
    tm = 512
    rb = wg_out.shape[2]

    def body(do_ref, w_ref, dy_ref):
        dov = do_ref[...]
        for d in range(NDEV):
            dy_ref[:, d * rb:(d + 1) * rb] = lax.dot_general(
                dov, w_ref[d], (((1,), (1,)), ((), ())), preferred_element_type=F32).astype(BF16)

    return pl.pallas_call(
        body, name=name, grid=(S // tm,), out_shape=jax.ShapeDtypeStruct((S, NDEV * rb), BF16),
        in_specs=[pl.BlockSpec((tm, D), lambda i: (i, 0)), pl.BlockSpec((NDEV, None, rb, D), lambda i: (0, wl, 0, 0))],
        out_specs=pl.BlockSpec((tm, NDEV * rb), lambda i: (i, 0)),
        compiler_params=_params(48, ("arbitrary",)),
    )(d_out, wg_out)


def _weight_grad(a, b, nlayers, wl, prev, name):
    a_blocks = a.shape[1] != D
    ka = a.shape[1] // NDEV if a_blocks else a.shape[1]
    nb = b.shape[1] if a_blocks else b.shape[1] // NDEV

    def body(a_ref, b_ref, *rest):
        o_ref = rest[-1]
        o_ref[...] = lax.dot_general(a_ref[...], b_ref[...], (((0,), (0,)), ((), ())),
                                     preferred_element_type=F32).astype(BF16)

    in_specs = [pl.BlockSpec((S, ka), (lambda d: (0, d)) if a_blocks else (lambda d: (0, 0))),
                pl.BlockSpec((S, nb), (lambda d: (0, 0)) if a_blocks else (lambda d: (0, d)))]
    args = [a, b]
    aliases = {}
    if prev is not None:
        in_specs.append(ANY)
        args.append(prev)
        aliases = {2: 0}
    return pl.pallas_call(
        body, name=name, grid=(NDEV,), out_shape=jax.ShapeDtypeStruct((NDEV, nlayers, ka, nb), BF16),
        in_specs=in_specs, out_specs=pl.BlockSpec((None, None, ka, nb), lambda d: (d, wl, 0, 0)),
        input_output_aliases=aliases, compiler_params=_params(56, ("arbitrary",)),
    )(*args)


def _dh(d_proj, wg, wl, name):
    nb = wg.shape[-1]
    tm = 512

    def body(dp_ref, w_ref, dh_ref):
        part = lax.dot_general(dp_ref[...], w_ref[...], (((1,), (1,)), ((), ())), preferred_element_type=F32)

        @pl.when(pl.program_id(1) == 0)
        def _():
            dh_ref[...] = part

        @pl.when(pl.program_id(1) != 0)
        def _():
            dh_ref[...] += part

    return pl.pallas_call(
        body, name=name, grid=(S // tm, NDEV), out_shape=jax.ShapeDtypeStruct((S, D), F32),
        in_specs=[pl.BlockSpec((tm, nb), lambda i, d: (i, d)), pl.BlockSpec((None, None, D, nb), lambda i, d: (d, wl, 0, 0))],
        out_specs=pl.BlockSpec((tm, D), lambda i, d: (i, 0)),
        compiler_params=_params(48, ("arbitrary", "arbitrary")),
    )(d_proj, wg)


def _norm_bwd(d_h, x, dx, mod, norm_g3, layer, name):
    tm = 256

    def body(dh_ref, x_ref, dx_ref, g_ref, scale_ref, dxi_ref, dshift_ref, dscale_ref, dg_ref):
        @pl.when(pl.program_id(0) == 0)
        def _():
            dshift_ref[...] = jnp.zeros_like(dshift_ref)
            dscale_ref[...] = jnp.zeros_like(dscale_ref)
            dg_ref[...] = jnp.zeros_like(dg_ref)

        xv, dhv, g = x_ref[...], dh_ref[...], g_ref[...]
        r = lax.rsqrt(jnp.mean(xv * xv, axis=-1, keepdims=True) + EPS)
        xn = xv * r
        dshift_ref[...] += jnp.sum(dhv, axis=0, keepdims=True)
        dscale_ref[...] += jnp.sum(dhv * (xn * g), axis=0, keepdims=True)
        dhn = dhv * (1.0 + scale_ref[...])
        dg_ref[...] += jnp.sum(dhn * xn, axis=0, keepdims=True)
        u = dhn * g
        dxi_ref[...] = dx_ref[...] + r * (u - xn * jnp.mean(xn * u, axis=-1, keepdims=True))

    tile = pl.BlockSpec((tm, D), lambda i: (i, 0))
    row = pl.BlockSpec((1, D), lambda i: (0, 0))
    return pl.pallas_call(
        body, name=name, grid=(S // tm,),
        out_shape=[jax.ShapeDtypeStruct((S, D), F32)] + [jax.ShapeDtypeStruct((1, D), F32)] * 3,
        in_specs=[tile, tile, tile, pl.BlockSpec((None, 1, D), lambda i: (layer, 0, 0)), _mod_spec(layer, 1, 1)],
        out_specs=[tile, row, row, row], compiler_params=_params(40, ("arbitrary",)),
    )(d_h, x, dx, norm_g3, mod)


TS = 256
NCH = TS // CHUNK
HALO_BLOCKS = TS // HALO


def _halo_before(width, col_block):
    return pl.BlockSpec((HALO, width), lambda i: (jnp.maximum(i * HALO_BLOCKS - 1, 0), col_block))


def _halo_after(width, col_block):
    return pl.BlockSpec((HALO, width), lambda i: (jnp.minimum((i + 1) * HALO_BLOCKS, S // HALO - 1), col_block))


def _shift_down(ext, k):
    return pltpu.roll(ext, k, 0)[HALO:]


def _shift_up(ext, k):
    return pltpu.roll(ext, ext.shape[0] - k, 0)[:ext.shape[0] - HALO]


def _layer_norm_head(v, lg, lb):
    mu = jnp.mean(v, axis=-1, keepdims=True)
    vc = v - mu
    rstd = lax.rsqrt(jnp.mean(vc * vc, axis=-1, keepdims=True) + EPS)
    vhat = vc * rstd
    return vhat, rstd, vhat * lg + lb


def _causal_mask():
    return lax.broadcasted_iota(jnp.int32, (CHUNK, CHUNK), 0) >= lax.broadcasted_iota(jnp.int32, (CHUNK, CHUNK), 1)


def _even_mix_fwd(proj, convw, ln_g3, ln_b3, sgu_w, sgu_bcol, wl, name):
    def body(pj_ref, hh_ref, hc_ref, cw_ref, lg_ref, lb_ref, sw_ref, sb_ref, y_ref):
        live = (pl.program_id(0) > 0).astype(F32)
        causal = _causal_mask()
        for j in range(E_A // HEAD):
            cols = slice(j * HEAD, (j + 1) * HEAD)
            w0, w1, w2 = cw_ref[0:1, cols], cw_ref[1:2, cols], cw_ref[2:3, cols]
            lg, lb = lg_ref[:, cols], lb_ref[:, cols]
            wm = jnp.where(causal, sw_ref[j], 0.0).astype(BF16)
            bias = sb_ref[j]

            def split(s, rows, cols=cols):
                return pj_ref[rows, s * E_A + cols.start:s * E_A + cols.stop].astype(F32)

            prev_tail = hc_ref[:, cols].astype(F32) * hh_ref[:, cols].astype(F32) * live
            for n in range(NCH):
                rows = slice(n * CHUNK, (n + 1) * CHUNK)
                p = split(2, rows) * split(0, rows)
                ext = jnp.concatenate([prev_tail, p], axis=0)
                prev_tail = p[CHUNK - HALO:]
                cv = w2 * p + w1 * _shift_down(ext, 1) + w0 * _shift_down(ext, 2)
                y_ref[rows, cols] = (split(1, rows) * cv * _silu(split(3, rows))).astype(BF16)
                _, _, vn = _layer_norm_head(split(5, rows), lg, lb)
                mixed = jnp.dot(wm, vn.astype(BF16), preferred_element_type=F32) + bias
                y_ref[rows, E_A + cols.start:E_A + cols.stop] = (split(4, rows) * mixed * _silu(split(6, rows))).astype(BF16)

    const3 = lambda i: (wl, 0, 0)
    const4 = lambda i: (wl, 0, 0, 0)
    return pl.pallas_call(
        body, name=name, grid=(S // TS,), out_shape=jax.ShapeDtypeStruct((S, 2 * E_A), BF16),
        in_specs=[pl.BlockSpec((TS, 7 * E_A), lambda i: (i, 0)), _halo_before(E_A, 0), _halo_before(E_A, 2),
                  pl.BlockSpec((None, 3, E_A), const3), pl.BlockSpec((None, 1, E_A), const3),
                  pl.BlockSpec((None, 1, E_A), const3), pl.BlockSpec((None, NDEV, CHUNK, CHUNK), const4),
                  pl.BlockSpec((None, NDEV, CHUNK, 1), const4)],
        out_specs=pl.BlockSpec((TS, 2 * E_A), lambda i: (i, 0)),
        compiler_params=_params(48, ("arbitrary",)),
    )(proj, proj, proj, convw, ln_g3, ln_b3, sgu_w, sgu_bcol)


def _even_mix_bwd(proj, d_ycat, convw, ln_g3, ln_b3, sgu_w, sgu_bcol, wl, name):
    nsteps = S // TS

    def body(pj_ref, hh_ref, hc_ref, hb_ref, hz_ref, dy_ref, hdy_ref, cw_ref, lg_ref, lb_ref, sw_ref, sb_ref,
             dp_ref, dcw_ref, dlg_ref, dlb_ref, dsw_ref, dsb_ref):
        step = pl.program_id(0)

        @pl.when(step == 0)
        def _():
            for ref in (dcw_ref, dlg_ref, dlb_ref, dsw_ref, dsb_ref):
                ref[...] = jnp.zeros_like(ref)

        live_before = (step > 0).astype(F32)
        live_after = (step < nsteps - 1).astype(F32)
        causal = _causal_mask()
        for j in range(E_A // HEAD):
            cols = slice(j * HEAD, (j + 1) * HEAD)
            w0, w1, w2 = cw_ref[0:1, cols], cw_ref[1:2, cols], cw_ref[2:3, cols]
            lg, lb = lg_ref[:, cols], lb_ref[:, cols]
            wmf = jnp.where(causal, sw_ref[j], 0.0)
            wm, wmt = wmf.astype(BF16), wmf.T.astype(BF16)
            bias = sb_ref[j]

            def split(s, rows, cols=cols):
                return pj_ref[rows, s * E_A + cols.start:s * E_A + cols.stop].astype(F32)

            def put(s, rows, val, cols=cols):
                dp_ref[rows, s * E_A + cols.start:s * E_A + cols.stop] = val.astype(BF16)

            ps = [split(2, slice(n * CHUNK, (n + 1) * CHUNK)) * split(0, slice(n * CHUNK, (n + 1) * CHUNK)) for n in range(NCH)]
            next_head = (hdy_ref[:, cols].astype(F32) * hb_ref[:, cols].astype(F32) * _silu(hz_ref[:, cols].astype(F32))
                         * live_after)
            acc_w = [jnp.zeros((1, HEAD), F32) for _ in range(3)]
            for n in reversed(range(NCH)):
                rows = slice(n * CHUNK, (n + 1) * CHUNK)
                p = ps[n]
                tail = ps[n - 1][CHUNK - HALO:] if n > 0 else hc_ref[:, cols].astype(F32) * hh_ref[:, cols].astype(F32) * live_before
                ext = jnp.concatenate([tail, p], axis=0)
                p1, p2 = _shift_down(ext, 1), _shift_down(ext, 2)
                cv = w2 * p + w1 * p1 + w0 * p2
                a_b, a_z = split(1, rows), split(3, rows)
                sz, dsz = _silu_and_grad(a_z)
                dya = dy_ref[rows, cols].astype(F32)
                put(1, rows, dya * cv * sz)
                put(3, rows, dya * a_b * cv * dsz)
                gcv = dya * a_b * sz
                acc_w[0] += jnp.sum(gcv * p2, axis=0, keepdims=True)
                acc_w[1] += jnp.sum(gcv * p1, axis=0, keepdims=True)
                acc_w[2] += jnp.sum(gcv * p, axis=0, keepdims=True)
                gext = jnp.concatenate([gcv, next_head], axis=0)
                next_head = gcv[:HALO]
                dpv = w2 * gcv + w1 * _shift_up(gext, 1) + w0 * _shift_up(gext, 2)
                put(2, rows, dpv * split(0, rows))
                put(0, rows, dpv * split(2, rows))
            for k in range(3):
                dcw_ref[k:k + 1, cols] += acc_w[k]

            acc_lg, acc_lb = jnp.zeros((1, HEAD), F32), jnp.zeros((1, HEAD), F32)
            acc_sw, acc_sb = jnp.zeros((CHUNK, CHUNK), F32), jnp.zeros((CHUNK, 1), F32)
            for n in range(NCH):
                rows = slice(n * CHUNK, (n + 1) * CHUNK)
                u, z = split(4, rows), split(6, rows)
                vhat, rstd, vn = _layer_norm_head(split(5, rows), lg, lb)
                vn16 = vn.astype(BF16)
                mixed = jnp.dot(wm, vn16, preferred_element_type=F32) + bias
                sz, dsz = _silu_and_grad(z)
                dyb = dy_ref[rows, E_A + cols.start:E_A + cols.stop].astype(F32)
                put(4, rows, dyb * mixed * sz)
                put(6, rows, dyb * u * mixed * dsz)
                dmix = dyb * u * sz
                dmix16 = dmix.astype(BF16)
                acc_sb += jnp.sum(dmix, axis=1, keepdims=True)
                acc_sw += lax.dot_general(dmix16, vn16, (((1,), (1,)), ((), ())), preferred_element_type=F32)
                dvn = jnp.dot(wmt, dmix16, preferred_element_type=F32)
                acc_lg += jnp.sum(dvn * vhat, axis=0, keepdims=True)
                acc_lb += jnp.sum(dvn, axis=0, keepdims=True)
                dvh = dvn * lg
                put(5, rows, rstd * (dvh - jnp.mean(dvh, axis=-1, keepdims=True)
                                     - vhat * jnp.mean(dvh * vhat, axis=-1, keepdims=True)))
            dlg_ref[:, cols] += acc_lg
            dlb_ref[:, cols] += acc_lb
            dsw_ref[j] += jnp.where(causal, acc_sw, 0.0)
            dsb_ref[j] += acc_sb

    const3 = lambda i: (wl, 0, 0)
    const4 = lambda i: (wl, 0, 0, 0)
    fixed2 = lambda i: (0, 0)
    fixed3 = lambda i: (0, 0, 0)
    return pl.pallas_call(
        body, name=name, grid=(nsteps,),
        out_shape=[jax.ShapeDtypeStruct((S, 7 * E_A), BF16), jax.ShapeDtypeStruct((3, E_A), F32),
                   jax.ShapeDtypeStruct((1, E_A), F32), jax.ShapeDtypeStruct((1, E_A), F32),
                   jax.ShapeDtypeStruct((NDEV, CHUNK, CHUNK), F32), jax.ShapeDtypeStruct((NDEV, CHUNK, 1), F32)],
        in_specs=[pl.BlockSpec((TS, 7 * E_A), lambda i: (i, 0)), _halo_before(E_A, 0), _halo_before(E_A, 2),
                  _halo_after(E_A, 1), _halo_after(E_A, 3),
                  pl.BlockSpec((TS, 2 * E_A), lambda i: (i, 0)), _halo_after(E_A, 0),
                  pl.BlockSpec((None, 3, E_A), const3), pl.BlockSpec((None, 1, E_A), const3),
                  pl.BlockSpec((None, 1, E_A), const3), pl.BlockSpec((None, NDEV, CHUNK, CHUNK), const4),
                  pl.BlockSpec((None, NDEV, CHUNK, 1), const4)],
        out_specs=[pl.BlockSpec((TS, 7 * E_A), lambda i: (i, 0)), pl.BlockSpec((3, E_A), fixed2),
                   pl.BlockSpec((1, E_A), fixed2), pl.BlockSpec((1, E_A), fixed2),
                   pl.BlockSpec((NDEV, CHUNK, CHUNK), fixed3), pl.BlockSpec((NDEV, CHUNK, 1), fixed3)],
        compiler_params=_params(56, ("arbitrary",)),
    )(proj, proj, proj, proj, proj, d_ycat, d_ycat, convw, ln_g3, ln_b3, sgu_w, sgu_bcol)


def _window_count(step, n, win, ext_before):
    rows = CHUNK if ext_before else CHUNK + HALO
    t = step * TS + n * CHUNK + lax.broadcasted_iota(jnp.int32, (rows, 1), 0)
    return jnp.minimum(t + 1, win).astype(F32)


def _pool_weight(wp_ref, g):
    return jnp.concatenate([wp_ref[d, g] for d in range(NDEV)], axis=0)


def _pooled_chunk(p, tail, win, count):
    sums = jnp.concatenate([tail, p], axis=0)
    shift = 1
    while shift < win:
        sums = sums + pltpu.roll(sums, shift, 0)
        shift *= 2
    return sums[HALO:] / count - p


def _pool_mix_fwd(proj, wpool, pscale4, wl, name):
    e_c = 4 * G_C

    def body(pj_ref, hp_ref, wp_ref, ps_ref, y_ref, pooled_scr, yraw_scr):
        step = pl.program_id(0)
        live = (step > 0).astype(F32)
        for g, win in enumerate(POOL_WINDOWS):
            for q in range(G_C // LANE):
                cols = slice(g * G_C + q * LANE, g * G_C + (q + 1) * LANE)
                tail = hp_ref[:, cols].astype(F32) * live
                for n in range(NCH):
                    rows = slice(n * CHUNK, (n + 1) * CHUNK)
                    p = pj_ref[rows, cols].astype(F32)
                    pooled_scr[rows, q * LANE:(q + 1) * LANE] = _pooled_chunk(
                        p, tail, win, _window_count(step, n, win, True)).astype(BF16)
                    tail = p[CHUNK - HALO:]
            yraw_scr[...] = jnp.dot(pooled_scr[...], _pool_weight(wp_ref, g), preferred_element_type=F32)
            for q in range(G_C // LANE):
                cols = slice(g * G_C + q * LANE, g * G_C + (q + 1) * LANE)
                for n in range(NCH):
                    rows = slice(n * CHUNK, (n + 1) * CHUNK)
                    z = pj_ref[rows, e_c + cols.start:e_c + cols.stop].astype(F32)
                    y_ref[rows, cols] = (yraw_scr[rows, q * LANE:(q + 1) * LANE] * ps_ref[:, cols] * _silu(z)).astype(BF16)

    return pl.pallas_call(
        body, name=name, grid=(S // TS,), out_shape=jax.ShapeDtypeStruct((S, e_c), BF16),
        in_specs=[pl.BlockSpec((TS, 2 * e_c), lambda i: (i, 0)), _halo_before(e_c, 0),
                  pl.BlockSpec((NDEV, None, 4, G_C // NDEV, G_C), lambda i: (0, wl, 0, 0, 0)),
                  pl.BlockSpec((None, 1, e_c), lambda i: (wl, 0, 0))],
        out_specs=pl.BlockSpec((TS, e_c), lambda i: (i, 0)),
        scratch_shapes=[pltpu.VMEM((TS, G_C), BF16), pltpu.VMEM((TS, G_C), F32)],
        compiler_params=_params(48, ("arbitrary",)),
    )(proj, proj, wpool, pscale4)


def _pool_mix_bwd(proj, d_ycat, wpool, pscale4, wl, prev, name):
    e_c = 4 * G_C
    nsteps = S // TS
    rb = G_C // NDEV

    def body(pj_ref, hp_ref, hz_ref, dy_ref, hdy_ref, wp_ref, ps_ref, *rest):
        dp_ref, dps_ref, dwp_ref, pooled_scr, yraw_scr, dyraw_scr, dpool_scr, acc_w = rest[-8:]
        step = pl.program_id(0)

        @pl.when(step == 0)
        def _():
            dps_ref[...] = jnp.zeros_like(dps_ref)
            acc_w[...] = jnp.zeros_like(acc_w)

        live_before = (step > 0).astype(F32)
        live_after = (step < nsteps - 1).astype(F32)
        for g, win in enumerate(POOL_WINDOWS):
            weight = _pool_weight(wp_ref, g)
            for q in range(G_C // LANE):
                cols = slice(g * G_C + q * LANE, g * G_C + (q + 1) * LANE)
                tail = hp_ref[:, cols].astype(F32) * live_before
                for n in range(NCH):
                    rows = slice(n * CHUNK, (n + 1) * CHUNK)
                    p = pj_ref[rows, cols].astype(F32)
                    pooled_scr[rows, q * LANE:(q + 1) * LANE] = _pooled_chunk(
                        p, tail, win, _window_count(step, n, win, True)).astype(BF16)
                    tail = p[CHUNK - HALO:]
            yraw_scr[...] = jnp.dot(pooled_scr[...], weight, preferred_element_type=F32)
            for q in range(G_C // LANE):
                cols = slice(g * G_C + q * LANE, g * G_C + (q + 1) * LANE)
                local = slice(q * LANE, (q + 1) * LANE)
                scale = ps_ref[:, cols]
                acc_ps = jnp.zeros((1, LANE), F32)
                for n in range(NCH):
                    rows = slice(n * CHUNK, (n + 1) * CHUNK)
                    sz, dsz = _silu_and_grad(pj_ref[rows, e_c + cols.start:e_c + cols.stop].astype(F32))
                    dyv = dy_ref[rows, cols].astype(F32)
                    yraw = yraw_scr[rows, local]
                    dyraw_scr[rows, local] = (dyv * scale * sz).astype(BF16)
                    acc_ps += jnp.sum(dyv * yraw * sz, axis=0, keepdims=True)
                    dp_ref[rows, e_c + cols.start:e_c + cols.stop] = (dyv * yraw * scale * dsz).astype(BF16)
                dps_ref[:, cols] += acc_ps
                dyraw_scr[TS:, local] = (hdy_ref[:, cols].astype(F32) * scale * _silu(hz_ref[:, cols].astype(F32))
                                         * live_after).astype(BF16)
            dpool_scr[...] = lax.dot_general(dyraw_scr[...], weight, (((1,), (1,)), ((), ())), preferred_element_type=F32)
            acc_w[g] += lax.dot_general(pooled_scr[...], dyraw_scr[:TS, :], (((0,), (0,)), ((), ())),
                                        preferred_element_type=F32)
            for q in range(G_C // LANE):
                cols = slice(g * G_C + q * LANE, g * G_C + (q + 1) * LANE)
                local = slice(q * LANE, (q + 1) * LANE)
                for n in range(NCH):
                    rows = slice(n * CHUNK, (n + 1) * CHUNK)
                    ext = dpool_scr[n * CHUNK:(n + 1) * CHUNK + HALO, local]
                    sums = ext / _window_count(step, n, win, False)
                    shift = 1
                    while shift < win:
                        sums = sums + pltpu.roll(sums, CHUNK + HALO - shift, 0)
                        shift *= 2
                    dp_ref[rows, cols] = (sums[:CHUNK] - ext[:CHUNK]).astype(BF16)

        @pl.when(step == nsteps - 1)
        def _():
            for g in range(4):
                for d in range(NDEV):
                    dwp_ref[d, g] = acc_w[g, d * rb:(d + 1) * rb, :].astype(BF16)

    in_specs = [pl.BlockSpec((TS, 2 * e_c), lambda i: (i, 0)), _halo_before(e_c, 0), _halo_after(e_c, 1),
                pl.BlockSpec((TS, e_c), lambda i: (i, 0)), _halo_after(e_c, 0),
                pl.BlockSpec((NDEV, None, 4, rb, G_C), lambda i: (0, wl, 0, 0, 0)),
                pl.BlockSpec((None, 1, e_c), lambda i: (wl, 0, 0))]
    args = [proj, proj, proj, d_ycat, d_ycat, wpool, pscale4]
    aliases = {}
    if prev is not None:
        in_specs.append(ANY)
        args.append(prev)
        aliases = {7: 2}
    return pl.pallas_call(
        body, name=name, grid=(nsteps,),
        out_shape=[jax.ShapeDtypeStruct((S, 2 * e_c), BF16), jax.ShapeDtypeStruct((1, e_c), F32),
                   jax.ShapeDtypeStruct(wpool.shape, BF16)],
        in_specs=in_specs,
        out_specs=[pl.BlockSpec((TS, 2 * e_c), lambda i: (i, 0)), pl.BlockSpec((1, e_c), lambda i: (0, 0)),
                   pl.BlockSpec((NDEV, None, 4, rb, G_C), lambda i: (0, wl, 0, 0, 0))],
        scratch_shapes=[pltpu.VMEM((TS, G_C), BF16), pltpu.VMEM((TS, G_C), F32), pltpu.VMEM((TS + HALO, G_C), BF16),
                        pltpu.VMEM((TS + HALO, G_C), F32), pltpu.VMEM((4, G_C, G_C), F32)],
        input_output_aliases=aliases, compiler_params=_params(56, ("arbitrary",)),
    )(*args)


def _adamw(w, g, m, v):
    m = ADAM_B1 * m + (1.0 - ADAM_B1) * g
    v = ADAM_B2 * v + (1.0 - ADAM_B2) * jnp.square(g)
    m_hat = m / (1.0 - ADAM_B1 ** ADAM_STEP)
    v_hat = v / (1.0 - ADAM_B2 ** ADAM_STEP)
    delta = -ADAM_LR * (m_hat / (jnp.sqrt(v_hat) + ADAM_EPS) + ADAM_WD * w)
    return delta, m, v


def _adam_sharded(w, m, v, recv, name):
    nl, nr, ncol = w.shape
    tr = 128

    def body(w_ref, m_ref, v_ref, r_ref, g_ref, d_ref, nm_ref, nv_ref):
        g = r_ref[0].astype(F32)
        for s in range(1, NDEV):
            g = g + r_ref[s].astype(F32)
        g_ref[...] = g
        d_ref[...], nm_ref[...], nv_ref[...] = _adamw(w_ref[...], g, m_ref[...], v_ref[...])

    spec = pl.BlockSpec((None, tr, ncol), lambda l, i: (l, i, 0))
    return pl.pallas_call(
        body, name=name, grid=(nl, nr // tr), out_shape=[jax.ShapeDtypeStruct(w.shape, F32)] * 4,
        in_specs=[spec, spec, spec, pl.BlockSpec((NDEV, None, tr, ncol), lambda l, i: (0, l, i, 0))],
        out_specs=[spec] * 4, compiler_params=_params(40, ("arbitrary", "arbitrary")),
    )(w, m, v, recv)


def _adam_small(w, g, m, v, name):
    def body(w_ref, g_ref, m_ref, v_ref, d_ref, nm_ref, nv_ref):
        d_ref[...], nm_ref[...], nv_ref[...] = _adamw(w_ref[...], g_ref[...], m_ref[...], v_ref[...])

    return pl.pallas_call(body, name=name, out_shape=[jax.ShapeDtypeStruct(w.shape, F32)] * 3,
                          in_specs=[VMEM_FULL] * 4, out_specs=[VMEM_FULL] * 3, compiler_params=_params(32))(w, g, m, v)


def _sum_devices(gathered, name):
    _, nr, ncol = gathered.shape

    def body(g_ref, o_ref):
        acc = g_ref[0]
        for s in range(1, NDEV):
            acc = acc + g_ref[s]
        o_ref[...] = acc

    return pl.pallas_call(body, name=name, grid=(1,), out_shape=jax.ShapeDtypeStruct((nr, ncol), F32),
                          in_specs=[pl.BlockSpec((NDEV, nr, ncol), lambda i: (0, 0, 0))],
                          out_specs=pl.BlockSpec((nr, ncol), lambda i: (0, 0)),
                          compiler_params=_params(48, ("arbitrary",)))(gathered)


def _ada_weight_adam(cact_t, dmod_mine, w, m, v):
    def body(ct_ref, dm_ref, w_ref, m_ref, v_ref, g_ref, d_ref, nm_ref, nv_ref):
        ct, dm = ct_ref[...], dm_ref[...]
        g = ct[:, 0:1] * dm[0:1, :]
        for e in range(1, NDEV):
            g = g + ct[:, e:e + 1] * dm[e:e + 1, :]
        g_ref[...] = g
        d_ref[...], nm_ref[...], nv_ref[...] = _adamw(w_ref[...], g, m_ref[...], v_ref[...])

    spec = pl.BlockSpec((None, D, ADA_NC), lambda l: (l, 0, 0))
    return pl.pallas_call(
        body, name="ada_weight_adam", grid=(DEPTH,), out_shape=[jax.ShapeDtypeStruct(w.shape, F32)] * 4,
        in_specs=[pl.BlockSpec((D, NDEV), lambda l: (0, 0)), pl.BlockSpec((None, NDEV, ADA_NC), lambda l: (l, 0, 0)),
                  spec, spec, spec],
        out_specs=[spec] * 4, compiler_params=_params(40, ("arbitrary",)),
    )(cact_t, dmod_mine, w, m, v)


def _pad_rows(a, rows):
    a = a.reshape(-1, D)
    return jnp.pad(a, ((0, rows - a.shape[0]), (0, 0)))


def kernel(x, c, norm_g, ada_w, ada_b, ab_w_in, ab_conv_w, ab_ln_g, ab_ln_b, ab_sgu_w, ab_sgu_b, ab_w_out, c_w_in, c_pool_w, c_pool_scale, c_w_out, final_g, loss_target, m_norm_g, m_ada_w, m_ada_b, m_ab_w_in, m_ab_conv_w, m_ab_ln_g, m_ab_ln_b, m_ab_sgu_w, m_ab_sgu_b, m_ab_w_out, m_c_w_in, m_c_pool_w, m_c_pool_scale, m_c_w_out, m_final_g, v_norm_g, v_ada_w, v_ada_b, v_ab_w_in, v_ab_conv_w, v_ab_ln_g, v_ab_ln_b, v_ab_sgu_w, v_ab_sgu_b, v_ab_w_out, v_c_w_in, v_c_pool_w, v_c_pool_scale, v_c_w_out, v_final_g):
    me = _index(_position())
    x0 = x.reshape(S, D)
    target = loss_target.reshape(S, D)
    norm_g3 = norm_g.reshape(DEPTH, 1, D)
    ln_g3, ln_b3 = ab_ln_g.reshape(2, 1, E_A), ab_ln_b.reshape(2, 1, E_A)
    sgu_bcol = ab_sgu_b.reshape(2, NDEV, CHUNK, 1)
    rb = G_C // NDEV

    cact_all, mod = _ada_forward(c, ada_w, ada_b)
    big = [_cast_bf16(ab_w_in.reshape(2 * D, -1), "cast_ab_w_in").reshape(ab_w_in.shape),
           _cast_bf16(ab_w_out.reshape(-1, D), "cast_ab_w_out").reshape(ab_w_out.shape),
           _cast_bf16(c_w_in.reshape(2 * D, -1), "cast_c_w_in").reshape(c_w_in.shape),
           _cast_bf16(c_pool_w.reshape(-1, G_C), "cast_c_pool_w").reshape(c_pool_w.shape),
           _cast_bf16(c_w_out.reshape(-1, D), "cast_c_w_out").reshape(c_w_out.shape)]
    wg_ab_in, wg_ab_out, wg_c_in, wg_pool, wg_c_out = _exchange(big, False, "gather_weights")
    convw_all, pscale_all = _exchange([ab_conv_w, c_pool_scale], False, "gather_small_weights")
    convw = jnp.transpose(convw_all, (1, 2, 0, 3)).reshape(2, 3, E_A)
    pscale4 = jnp.transpose(pscale_all, (1, 0, 2)).reshape(2, 1, 4 * G_C)

    xs, hs, projs, ycats, outs = [x0], [], [], [], []
    for layer in range(DEPTH):
        wl = layer // 2
        even = layer % 2 == 0
        h, proj = _norm_proj(xs[-1], mod, norm_g3, wg_ab_in if even else wg_c_in, layer, wl, f"norm_proj_{layer}")
        if even:
            ycat = _even_mix_fwd(proj, convw, ln_g3, ln_b3, ab_sgu_w, sgu_bcol, wl, f"even_mix_fwd_{layer}")
        else:
            ycat = _pool_mix_fwd(proj, wg_pool, pscale4, wl, f"pool_mix_fwd_{layer}")
        x_new, out = _out_proj(ycat, wg_ab_out if even else wg_c_out, wl, xs[-1], mod, layer, f"out_proj_{layer}")
        xs.append(x_new)
        hs.append(h)
        projs.append(proj)
        ycats.append(ycat)
        outs.append(out)

    dx, loss_part, d_final_g = _final_loss(xs[DEPTH], target, final_g.reshape(1, D))
    loss = lax.psum(loss_part[0, 0], ("x", "y", "c"))

    part = {"ab_w_in": None, "ab_w_out": None, "c_w_in": None, "c_pool_w": None, "c_w_out": None}
    d_mod, d_norm_g = [None] * DEPTH, [None] * DEPTH
    small = {}
    for layer in reversed(range(DEPTH)):
        wl = layer // 2
        even = layer % 2 == 0
        wg_in, wg_out = (wg_ab_in, wg_ab_out) if even else (wg_c_in, wg_c_out)
        key_in, key_out = ("ab_w_in", "ab_w_out") if even else ("c_w_in", "c_w_out")
        d_out, d_gate = _dout(dx, outs[layer], mod, layer, f"dout_{layer}")
        d_ycat = _dycat(d_out, wg_out, wl, f"dycat_{layer}")
        part[key_out] = _weight_grad(ycats[layer], d_out, 2, wl, part[key_out], f"grad_w_out_{layer}")
        if even:
            d_proj, d_cw, d_lg, d_lb, d_sw, d_sb = _even_mix_bwd(
                projs[layer], d_ycat, convw, ln_g3, ln_b3, ab_sgu_w, sgu_bcol, wl, f"even_mix_bwd_{layer}")
            small[layer] = (d_cw, d_lg, d_lb, d_sw, d_sb)
        else:
            d_proj, d_ps, part["c_pool_w"] = _pool_mix_bwd(
                projs[layer], d_ycat, wg_pool, pscale4, wl, part["c_pool_w"], f"pool_mix_bwd_{layer}")
            small[layer] = (d_ps,)
        d_h = _dh(d_proj, wg_in, wl, f"dh_{layer}")
        part[key_in] = _weight_grad(hs[layer], d_proj, 2, wl, part[key_in], f"grad_w_in_{layer}")
        dx, d_shift, d_scale, d_norm_g[layer] = _norm_bwd(d_h, xs[layer], dx, mod, norm_g3, layer, f"norm_bwd_{layer}")
        d_mod[layer] = jnp.concatenate([d_shift, d_scale, d_gate], axis=0)
    grad_x = dx.reshape(x.shape)

    keys = ["ab_w_in", "ab_w_out", "c_w_in", "c_pool_w", "c_w_out"]
    recv = dict(zip(keys, _exchange([part[k] for k in keys], True, "scatter_grads")))
    big_w = {"ab_w_in": (ab_w_in, m_ab_w_in, v_ab_w_in), "ab_w_out": (ab_w_out, m_ab_w_out, v_ab_w_out),
             "c_w_in": (c_w_in, m_c_w_in, v_c_w_in), "c_pool_w": (c_pool_w, m_c_pool_w, v_c_pool_w),
             "c_w_out": (c_w_out, m_c_w_out, v_c_w_out)}
    res = {}
    for k in keys:
        w, m, v = big_w[k]
        flat = (2, w.size // (2 * w.shape[-1]), w.shape[-1])
        outs4 = _adam_sharded(w.reshape(flat), m.reshape(flat), v.reshape(flat), recv[k].reshape((NDEV,) + flat), "adam_" + k)
        res[k] = [o.reshape(w.shape) for o in outs4]

    sections = [("norm_g", jnp.concatenate(d_norm_g, axis=0), 8),
                ("d_mod", jnp.concatenate(d_mod, axis=0), 16),
                ("ab_ln_g", jnp.concatenate([small[0][1], small[2][1]], axis=0), 8),
                ("ab_ln_b", jnp.concatenate([small[0][2], small[2][2]], axis=0), 8),
                ("ab_sgu_b", jnp.stack([small[0][4], small[2][4]]), 8),
                ("final_g", d_final_g, 8),
                ("ab_conv_w", jnp.stack([small[0][0], small[2][0]]), 8),
                ("c_pool_scale", jnp.concatenate([small[1][0], small[3][0]], axis=0), 8),
                ("ab_sgu_w", jnp.stack([small[0][3], small[2][3]]), 256)]
    offsets, at = {}, 0
    for name, _, rows in sections:
        offsets[name] = (at, rows)
        at += rows
    packed = jnp.concatenate([_pad_rows(a, rows) for _, a, rows in sections], axis=0)
    gathered = _exchange([packed], False, "gather_small_grads")[0]
    summed = _sum_devices(gathered, "sum_small_grads")

    def section(name, nrows, src=summed):
        start = offsets[name][0]
        return src[..., start:start + nrows, :]

    grads = {
        "norm_g": section("norm_g", DEPTH),
        "ada_b": section("d_mod", 3 * DEPTH).reshape(DEPTH, 3 * D),
        "ab_ln_g": section("ab_ln_g", 2), "ab_ln_b": section("ab_ln_b", 2),
        "ab_sgu_b": section("ab_sgu_b", 2).reshape(ab_sgu_b.shape),
        "final_g": section("final_g", 1),
        "ab_sgu_w": section("ab_sgu_w", 256).reshape(ab_sgu_w.shape),
        "ab_conv_w": lax.dynamic_slice_in_dim(section("ab_conv_w", 6).reshape(2, 3, E_A), me * HEAD, HEAD, axis=2),
        "c_pool_scale": lax.dynamic_slice_in_dim(section("c_pool_scale", 4).reshape(2, 4 * G_C), me * 256, 256, axis=1),
    }
    small_w = {"norm_g": (norm_g, m_norm_g, v_norm_g), "ada_b": (ada_b, m_ada_b, v_ada_b),
               "ab_ln_g": (ab_ln_g, m_ab_ln_g, v_ab_ln_g), "ab_ln_b": (ab_ln_b, m_ab_ln_b, v_ab_ln_b),
               "ab_sgu_b": (ab_sgu_b, m_ab_sgu_b, v_ab_sgu_b),
               "final_g": (final_g.reshape(1, D), m_final_g.reshape(1, D), v_final_g.reshape(1, D)),
               "ab_sgu_w": (ab_sgu_w, m_ab_sgu_w, v_ab_sgu_w), "ab_conv_w": (ab_conv_w, m_ab_conv_w, v_ab_conv_w),
               "c_pool_scale": (c_pool_scale, m_c_pool_scale, v_c_pool_scale)}
    for k, (w, m, v) in small_w.items():
        res[k] = [grads[k]] + list(_adam_small(w, grads[k], m, v, "adam_" + k))
    res["final_g"] = [a.reshape(D) for a in res["final_g"]]

    dmod_all = section("d_mod", 3 * DEPTH, gathered).reshape(NDEV, DEPTH, 3 * D)
    dmod_mine = jnp.transpose(lax.dynamic_slice_in_dim(dmod_all, me * ADA_NC, ADA_NC, axis=2), (1, 0, 2))
    res["ada_w"] = _ada_weight_adam(jnp.transpose(cact_all.reshape(NDEV, D)), dmod_mine, ada_w, m_ada_w, v_ada_w)

    order = ["norm_g", "ada_w", "ada_b", "ab_w_in", "ab_conv_w", "ab_ln_g", "ab_ln_b", "ab_sgu_w", "ab_sgu_b",
             "ab_w_out", "c_w_in", "c_pool_w", "c_pool_scale", "c_w_out", "final_g"]
    return (loss, grad_x, *[res[k][0] for k in order], *[res[k][1] for k in order],
            *[res[k][2] for k in order], *[res[k][3] for k in order])
```

```python
import jax
import jax.numpy as jnp
from jax import lax
from jax.experimental import pallas as pl
from jax.experimental.pallas import tpu as pltpu

F32, BF16 = jnp.float32, jnp.bfloat16
S, D = 2048, 1024
NDEV = 8
DEPTH = 4
EPS = 1e-6
E_A = 1024
HEAD = 128
CHUNK = 128
POOL_WINDOWS = (2, 4, 8, 16)
G_C = 512
HALO = 16
ADA_NC = 384
MIB = 1024 * 1024
LANE = 128

ADAM_LR, ADAM_B1, ADAM_B2, ADAM_EPS, ADAM_WD, ADAM_STEP = 0.001, 0.9, 0.999, 1e-08, 0.01, 10

ANY = pl.BlockSpec(memory_space=pl.ANY)
VMEM_FULL = pl.BlockSpec(memory_space=pltpu.VMEM)


def _params(vmem_mib, semantics=None):
    return pltpu.CompilerParams(dimension_semantics=semantics, vmem_limit_bytes=vmem_mib * MIB)


def _silu(z):
    return z * jax.nn.sigmoid(z)


def _silu_and_grad(z):
    sig = jax.nn.sigmoid(z)
    return z * sig, sig * (1.0 + z * (1.0 - sig))


def _position():
    return lax.axis_index("x"), lax.axis_index("y"), lax.axis_index("c")


def _index(pos):
    return 4 * pos[0] + 2 * pos[1] + pos[2]


def _peer(pos, k):
    flipped = tuple(1 - p if (k >> (2 - b)) & 1 else p for b, p in enumerate(pos))
    return flipped, _index(flipped)


def _remote(src, dst, send_sem, recv_sem, device):
    return pltpu.make_async_remote_copy(src_ref=src, dst_ref=dst, send_sem=send_sem, recv_sem=recv_sem,
                                        device_id=device, device_id_type=pl.DeviceIdType.MESH)


def _gather(arrays, name):
    n = len(arrays)
    out_shape = [jax.ShapeDtypeStruct((NDEV,) + a.shape, a.dtype) for a in arrays]

    def body(*refs):
        ins, outs = refs[:n], refs[n:2 * n]
        send_sems, recv_sems, own_sems = refs[2 * n:]
        x, y, c = _position()
        me = _index((x, y, c))
        sibling = (x, y, 1 - c)
        chips = [(1 - x, y), (x, 1 - y), (1 - x, 1 - y)]

        def block_copy(j, k, owner, to, src=None):
            rows = outs[j].at[_index(owner)]
            return _remote(rows if src is None else src, rows, send_sems.at[j, k], recv_sems.at[j, k], to)

        own, first, passed = [], [], []
        for j in range(n):
            own.append(pltpu.make_async_copy(ins[j], outs[j].at[me], own_sems.at[j]))
            first.append(block_copy(j, 0, (x, y, c), sibling, src=ins[j]))
            first += [block_copy(j, 1 + q, (x, y, c), (*chip, c), src=ins[j]) for q, chip in enumerate(chips)]
        for copy in own + first:
            copy.start()
        for q, chip in enumerate(chips):
            for j in range(n):
                block_copy(j, 1 + q, (*chip, c), (x, y, c)).wait_recv()
                forward = block_copy(j, 4 + q, (*chip, c), sibling)
                forward.start()
                passed.append(forward)
        for j in range(n):
            block_copy(j, 0, sibling, (x, y, c)).wait_recv()
            for q, chip in enumerate(chips):
                block_copy(j, 4 + q, (*chip, 1 - c), (x, y, c)).wait_recv()
        for copy in first + passed:
            copy.wait_send()
        for copy in own:
            copy.wait()

    return pl.pallas_call(
        body, name=name, out_shape=out_shape, in_specs=[ANY] * n, out_specs=[ANY] * n,
        scratch_shapes=[pltpu.SemaphoreType.DMA((n, NDEV - 1)), pltpu.SemaphoreType.DMA((n, NDEV - 1)),
                        pltpu.SemaphoreType.DMA((n,))],
    )(*arrays)


def _pair_exchange(parts, name):
    n = len(parts)

    def body(*refs):
        ins, outs = refs[:n], refs[n:2 * n]
        send_sems, recv_sems = refs[2 * n:]
        x, y, c = _position()
        copies = [_remote(ins[j].at[1 - c], outs[j], send_sems.at[j], recv_sems.at[j], (x, y, 1 - c)) for j in range(n)]
        for copy in copies:
            copy.start()
        for copy in copies:
            copy.wait()

    return pl.pallas_call(
        body, name=name, out_shape=[jax.ShapeDtypeStruct(p.shape[1:], p.dtype) for p in parts],
        in_specs=[ANY] * n, out_specs=[ANY] * n,
        scratch_shapes=[pltpu.SemaphoreType.DMA((n,)), pltpu.SemaphoreType.DMA((n,))],
    )(*parts)


def _pair_sum(part, from_sibling, core, name):
    ncol = part.shape[-1]
    p3 = part.reshape(2, -1, ncol)
    q2 = from_sibling.reshape(-1, ncol)
    nrows = q2.shape[0]
    tr = 512

    def body(core_ref, p_ref, q_ref, o_ref):
        o_ref[...] = (p_ref[...].astype(F32) + q_ref[...].astype(F32)).astype(BF16)

    grid_spec = pltpu.PrefetchScalarGridSpec(
        num_scalar_prefetch=1, grid=(nrows // tr,),
        in_specs=[pl.BlockSpec((None, tr, ncol), lambda i, core_ref: (core_ref[0], i, 0)),
                  pl.BlockSpec((tr, ncol), lambda i, core_ref: (i, 0))],
        out_specs=pl.BlockSpec((tr, ncol), lambda i, core_ref: (i, 0)))
    out = pl.pallas_call(body, name=name, grid_spec=grid_spec, out_shape=jax.ShapeDtypeStruct(q2.shape, BF16),
                         compiler_params=_params(32, ("arbitrary",)))(core, p3, q2)
    return out.reshape(from_sibling.shape)


def _chip_scatter(parts, name):
    n = len(parts)

    def body(*refs):
        ins, outs = refs[:n], refs[n:2 * n]
        send_sems, recv_sems, own_sems = refs[2 * n:]
        x, y, c = _position()
        my_chip = 2 * x + y
        chips = [(1 - x, y), (x, 1 - y), (1 - x, 1 - y)]
        copies = []
        for j in range(n):
            copies.append(pltpu.make_async_copy(ins[j].at[my_chip], outs[j].at[my_chip], own_sems.at[j]))
            copies += [_remote(ins[j].at[2 * px + py], outs[j].at[my_chip], send_sems.at[j, q], recv_sems.at[j, q], (px, py, c))
                       for q, (px, py) in enumerate(chips)]
        for copy in copies:
            copy.start()
        for copy in copies:
            copy.wait()

    return pl.pallas_call(
        body, name=name, out_shape=[jax.ShapeDtypeStruct(p.shape, p.dtype) for p in parts],
        in_specs=[ANY] * n, out_specs=[ANY] * n,
        scratch_shapes=[pltpu.SemaphoreType.DMA((n, 3)), pltpu.SemaphoreType.DMA((n, 3)), pltpu.SemaphoreType.DMA((n,))],
    )(*parts)


def _ada_forward(c, ada_w, ada_b):
    def body(c_ref, w_ref, b_ref, cact_ref, mod_ref, gbuf, modrow, send_sems, recv_sems):
        pos = _position()
        me = _index(pos)

        def to_all(ref, round_):
            copies = []
            for k in range(1, NDEV):
                peer, _ = _peer(pos, k)
                copy = pltpu.make_async_remote_copy(
                    src_ref=ref.at[me], dst_ref=ref.at[me], send_sem=send_sems.at[round_, k - 1],
                    recv_sem=recv_sems.at[round_, k - 1], device_id=peer, device_id_type=pl.DeviceIdType.MESH)
                copy.start()
                copies.append(copy)
            for copy in copies:
                copy.wait()

        cact_ref[me] = _silu(c_ref[...])
        to_all(cact_ref, 0)
        rows = lax.broadcasted_iota(jnp.int32, (NDEV, D), 0)
        cact = jnp.zeros((NDEV, D), F32)
        for e in range(NDEV):
            cact = jnp.where(rows == e, cact_ref[e], cact)
        cact = cact.astype(BF16)
        for l in range(DEPTH):
            gbuf[me, l] = jnp.dot(cact, w_ref[l].astype(BF16), preferred_element_type=F32)
        to_all(gbuf, 1)
        mine = lax.broadcasted_iota(jnp.int32, (NDEV, ADA_NC), 0) == me
        for l in range(DEPTH):
            for d in range(NDEV):
                modrow[:, d * ADA_NC:(d + 1) * ADA_NC] = jnp.sum(jnp.where(mine, gbuf[d, l], 0.0), axis=0, keepdims=True)
            full = modrow[...] + b_ref[l:l + 1, :]
            for w in range(3):
                mod_ref[l, w] = full[:, w * D:(w + 1) * D]

    return pl.pallas_call(
        body, name="ada_forward",
        out_shape=[jax.ShapeDtypeStruct((NDEV, 1, D), F32), jax.ShapeDtypeStruct((DEPTH, 3, 1, D), F32)],
        in_specs=[VMEM_FULL] * 3, out_specs=[VMEM_FULL] * 2,
        scratch_shapes=[pltpu.VMEM((NDEV, DEPTH, NDEV, ADA_NC), F32), pltpu.VMEM((1, 3 * D), F32),
                        pltpu.SemaphoreType.DMA((2, NDEV - 1)), pltpu.SemaphoreType.DMA((2, NDEV - 1))],
        compiler_params=_params(40),
    )(c, ada_w, ada_b)


def _cast_bf16(a, name):
    rows, cols = a.shape
    tr = 256

    def body(a_ref, o_ref):
        o_ref[...] = a_ref[...].astype(BF16)

    spec = pl.BlockSpec((tr, cols), lambda i: (i, 0))
    return pl.pallas_call(body, name=name, grid=(rows // tr,), out_shape=jax.ShapeDtypeStruct(a.shape, BF16),
                          in_specs=[spec], out_specs=spec, compiler_params=_params(32, ("arbitrary",)))(a)


def _mod_spec(layer, which, ngrid):
    index = {1: lambda i: (layer, which, 0, 0), 2: lambda i, j: (layer, which, 0, 0)}[ngrid]
    return pl.BlockSpec((None, None, 1, D), index)


def _norm_proj(x, mod, norm_g3, wg, layer, wl, name):
    nb = wg.shape[-1]
    tm = 512

    def body(x_ref, g_ref, shift_ref, scale_ref, w_ref, h_ref, p_ref):
        @pl.when(pl.program_id(1) == 0)
        def _():
            xv = x_ref[...]
            r = lax.rsqrt(jnp.mean(xv * xv, axis=-1, keepdims=True) + EPS)
            hn = xv * r * g_ref[...]
            h_ref[...] = (hn * (1.0 + scale_ref[...]) + shift_ref[...]).astype(BF16)

        p_ref[...] = jnp.dot(h_ref[...], w_ref[...], preferred_element_type=F32).astype(BF16)

    return pl.pallas_call(
        body, name=name, grid=(S // tm, NDEV),
        out_shape=[jax.ShapeDtypeStruct((S, D), BF16), jax.ShapeDtypeStruct((S, NDEV * nb), BF16)],
        in_specs=[pl.BlockSpec((tm, D), lambda i, d: (i, 0)),
                  pl.BlockSpec((None, 1, D), lambda i, d: (layer, 0, 0)),
                  _mod_spec(layer, 0, 2), _mod_spec(layer, 1, 2),
                  pl.BlockSpec((None, None, D, nb), lambda i, d: (d, wl, 0, 0))],
        out_specs=[pl.BlockSpec((tm, D), lambda i, d: (i, 0)), pl.BlockSpec((tm, nb), lambda i, d: (i, d))],
        compiler_params=_params(48, ("arbitrary", "arbitrary")),
    )(x, norm_g3, mod, mod, wg)


def _out_proj(ycat, wg_out, wl, x, mod, layer, name):
    tm = 512
    rb = wg_out.shape[2]

    def body(y_ref, w_ref, x_ref, gate_ref, xn_ref, o_ref):
        acc = jnp.zeros((tm, D), F32)
        for d in range(NDEV):
            acc = acc + jnp.dot(y_ref[:, d * rb:(d + 1) * rb], w_ref[d], preferred_element_type=F32)
        o_ref[...] = acc.astype(BF16)
        xn_ref[...] = x_ref[...] + gate_ref[...] * acc

    return pl.pallas_call(
        body, name=name, grid=(S // tm,),
        out_shape=[jax.ShapeDtypeStruct((S, D), F32), jax.ShapeDtypeStruct((S, D), BF16)],
        in_specs=[pl.BlockSpec((tm, NDEV * rb), lambda i: (i, 0)),
                  pl.BlockSpec((NDEV, None, rb, D), lambda i: (0, wl, 0, 0)),
                  pl.BlockSpec((tm, D), lambda i: (i, 0)), _mod_spec(layer, 2, 1)],
        out_specs=[pl.BlockSpec((tm, D), lambda i: (i, 0))] * 2,
        compiler_params=_params(48, ("arbitrary",)),
    )(ycat, wg_out, x, mod)


def _final_loss(x, target, final_g2):
    tm = 256

    def body(x_ref, t_ref, g_ref, dx_ref, loss_ref, dg_ref):
        @pl.when(pl.program_id(0) == 0)
        def _():
            loss_ref[...] = jnp.zeros_like(loss_ref)
            dg_ref[...] = jnp.zeros_like(dg_ref)

        xv, g = x_ref[...], g_ref[...]
        r = lax.rsqrt(jnp.mean(xv * xv, axis=-1, keepdims=True) + EPS)
        xn = xv * r
        err = xn * g - t_ref[...]
        loss_ref[...] += 0.5 * jnp.sum(jnp.mean(err * err, axis=-1, keepdims=True), axis=0, keepdims=True)
        dy = err * (1.0 / D)
        dg_ref[...] += jnp.sum(dy * xn, axis=0, keepdims=True)
        u = dy * g
        dx_ref[...] = r * (u - xn * jnp.mean(xn * u, axis=-1, keepdims=True))

    tile = pl.BlockSpec((tm, D), lambda i: (i, 0))
    row = pl.BlockSpec((1, D), lambda i: (0, 0))
    return pl.pallas_call(
        body, name="final_loss", grid=(S // tm,),
        out_shape=[jax.ShapeDtypeStruct((S, D), F32), jax.ShapeDtypeStruct((1, LANE), F32), jax.ShapeDtypeStruct((1, D), F32)],
        in_specs=[tile, tile, row], out_specs=[tile, pl.BlockSpec((1, LANE), lambda i: (0, 0)), row],
        compiler_params=_params(32, ("arbitrary",)),
    )(x, target, final_g2)


def _dout(dx, out, mod, layer, name):
    tm = 256

    def body(dx_ref, o_ref, gate_ref, do_ref, dgate_ref):
        @pl.when(pl.program_id(0) == 0)
        def _():
            dgate_ref[...] = jnp.zeros_like(dgate_ref)

        dxv = dx_ref[...]
        do_ref[...] = (gate_ref[...] * dxv).astype(BF16)
        dgate_ref[...] += jnp.sum(dxv * o_ref[...].astype(F32), axis=0, keepdims=True)

    tile = pl.BlockSpec((tm, D), lambda i: (i, 0))
    return pl.pallas_call(
        body, name=name, grid=(S // tm,),
        out_shape=[jax.ShapeDtypeStruct((S, D), BF16), jax.ShapeDtypeStruct((1, D), F32)],
        in_specs=[tile, tile, _mod_spec(layer, 2, 1)], out_specs=[tile, pl.BlockSpec((1, D), lambda i: (0, 0))],
        compiler_params=_params(32, ("arbitrary",)),
    )(dx, out, mod)


def _dycat(d_out, wg_out, wl, name):
    tm = 512
    rb = wg_out.shape[2]

    def body(do_ref, w_ref, dy_ref):
        dov = do_ref[...]
        for d in range(NDEV):
            dy_ref[:, d * rb:(d + 1) * rb] = lax.dot_general(
                dov, w_ref[d], (((1,), (1,)), ((), ())), preferred_element_type=F32).astype(BF16)

    return pl.pallas_call(
        body, name=name, grid=(S // tm,), out_shape=jax.ShapeDtypeStruct((S, NDEV * rb), BF16),
        in_specs=[pl.BlockSpec((tm, D), lambda i: (i, 0)), pl.BlockSpec((NDEV, None, rb, D), lambda i: (0, wl, 0, 0))],
        out_specs=pl.BlockSpec((tm, NDEV * rb), lambda i: (i, 0)),
        compiler_params=_params(48, ("arbitrary",)),
    )(d_out, wg_out)


def _weight_grad(a, b, nlayers, wl, prev, name):
    a_blocks = a.shape[1] != D
    ka = a.shape[1] // NDEV if a_blocks else a.shape[1]
    nb = b.shape[1] if a_blocks else b.shape[1] // NDEV

    def body(a_ref, b_ref, *rest):
        o_ref = rest[-1]
        o_ref[...] = lax.dot_general(a_ref[...], b_ref[...], (((0,), (0,)), ((), ())),
                                     preferred_element_type=F32).astype(BF16)

    in_specs = [pl.BlockSpec((S, ka), (lambda d: (0, d)) if a_blocks else (lambda d: (0, 0))),
                pl.BlockSpec((S, nb), (lambda d: (0, 0)) if a_blocks else (lambda d: (0, d)))]
    args = [a, b]
    aliases = {}
    if prev is not None:
        in_specs.append(ANY)
        args.append(prev)
        aliases = {2: 0}
    return pl.pallas_call(
        body, name=name, grid=(NDEV,), out_shape=jax.ShapeDtypeStruct((2, NDEV // 2, nlayers, ka, nb), BF16),
        in_specs=in_specs, out_specs=pl.BlockSpec((None, None, None, ka, nb), lambda d: (d % 2, d // 2, wl, 0, 0)),
        input_output_aliases=aliases, compiler_params=_params(56, ("arbitrary",)),
    )(*args)


def _dh(d_proj, wg, wl, name):
    nb = wg.shape[-1]
    tm = 512

    def body(dp_ref, w_ref, dh_ref):
        part = lax.dot_general(dp_ref[...], w_ref[...], (((1,), (1,)), ((), ())), preferred_element_type=F32)

        @pl.when(pl.program_id(1) == 0)
        def _():
            dh_ref[...] = part

        @pl.when(pl.program_id(1) != 0)
        def _():
            dh_ref[...] += part

    return pl.pallas_call(
        body, name=name, grid=(S // tm, NDEV), out_shape=jax.ShapeDtypeStruct((S, D), F32),
        in_specs=[pl.BlockSpec((tm, nb), lambda i, d: (i, d)), pl.BlockSpec((None, None, D, nb), lambda i, d: (d, wl, 0, 0))],
        out_specs=pl.BlockSpec((tm, D), lambda i, d: (i, 0)),
        compiler_params=_params(48, ("arbitrary", "arbitrary")),
    )(d_proj, wg)


def _norm_bwd(d_h, x, dx, mod, norm_g3, layer, name):
    tm = 256

    def body(dh_ref, x_ref, dx_ref, g_ref, scale_ref, dxi_ref, dshift_ref, dscale_ref, dg_ref):
        @pl.when(pl.program_id(0) == 0)
        def _():
            dshift_ref[...] = jnp.zeros_like(dshift_ref)
            dscale_ref[...] = jnp.zeros_like(dscale_ref)
            dg_ref[...] = jnp.zeros_like(dg_ref)

        xv, dhv, g = x_ref[...], dh_ref[...], g_ref[...]
        r = lax.rsqrt(jnp.mean(xv * xv, axis=-1, keepdims=True) + EPS)
        xn = xv * r
        dshift_ref[...] += jnp.sum(dhv, axis=0, keepdims=True)
        dscale_ref[...] += jnp.sum(dhv * (xn * g), axis=0, keepdims=True)
        dhn = dhv * (1.0 + scale_ref[...])
        dg_ref[...] += jnp.sum(dhn * xn, axis=0, keepdims=True)
        u = dhn * g
        dxi_ref[...] = dx_ref[...] + r * (u - xn * jnp.mean(xn * u, axis=-1, keepdims=True))

    tile = pl.BlockSpec((tm, D), lambda i: (i, 0))
    row = pl.BlockSpec((1, D), lambda i: (0, 0))
    return pl.pallas_call(
        body, name=name, grid=(S // tm,),
        out_shape=[jax.ShapeDtypeStruct((S, D), F32)] + [jax.ShapeDtypeStruct((1, D), F32)] * 3,
        in_specs=[tile, tile, tile, pl.BlockSpec((None, 1, D), lambda i: (layer, 0, 0)), _mod_spec(layer, 1, 1)],
        out_specs=[tile, row, row, row], compiler_params=_params(40, ("arbitrary",)),
    )(d_h, x, dx, norm_g3, mod)


TS = 256
NCH = TS // CHUNK
HALO_BLOCKS = TS // HALO


def _halo_before(width, col_block):
    return pl.BlockSpec((HALO, width), lambda i: (jnp.maximum(i * HALO_BLOCKS - 1, 0), col_block))


def _halo_after(width, col_block):
    return pl.BlockSpec((HALO, width), lambda i: (jnp.minimum((i + 1) * HALO_BLOCKS, S // HALO - 1), col_block))


def _shift_down(ext, k):
    return pltpu.roll(ext, k, 0)[HALO:]


def _shift_up(ext, k):
    return pltpu.roll(ext, ext.shape[0] - k, 0)[:ext.shape[0] - HALO]


def _layer_norm_head(v, lg, lb):
    mu = jnp.mean(v, axis=-1, keepdims=True)
    vc = v - mu
    rstd = lax.rsqrt(jnp.mean(vc * vc, axis=-1, keepdims=True) + EPS)
    vhat = vc * rstd
    return vhat, rstd, vhat * lg + lb


def _causal_mask():
    return lax.broadcasted_iota(jnp.int32, (CHUNK, CHUNK), 0) >= lax.broadcasted_iota(jnp.int32, (CHUNK, CHUNK), 1)


def _even_mix_fwd(proj, convw, ln_g3, ln_b3, sgu_w, sgu_bcol, wl, name):
    def body(pj_ref, hh_ref, hc_ref, cw_ref, lg_ref, lb_ref, sw_ref, sb_ref, y_ref):
        live = (pl.program_id(0) > 0).astype(F32)
        causal = _causal_mask()
        for j in range(E_A // HEAD):
            cols = slice(j * HEAD, (j + 1) * HEAD)
            w0, w1, w2 = cw_ref[0:1, cols], cw_ref[1:2, cols], cw_ref[2:3, cols]
            lg, lb = lg_ref[:, cols], lb_ref[:, cols]
            wm = jnp.where(causal, sw_ref[j], 0.0).astype(BF16)
            bias = sb_ref[j]

            def split(s, rows, cols=cols):
                return pj_ref[rows, s * E_A + cols.start:s * E_A + cols.stop].astype(F32)

            prev_tail = hc_ref[:, cols].astype(F32) * hh_ref[:, cols].astype(F32) * live
            for n in range(NCH):
                rows = slice(n * CHUNK, (n + 1) * CHUNK)
                p = split(2, rows) * split(0, rows)
                ext = jnp.concatenate([prev_tail, p], axis=0)
                prev_tail = p[CHUNK - HALO:]
                cv = w2 * p + w1 * _shift_down(ext, 1) + w0 * _shift_down(ext, 2)
                y_ref[rows, cols] = (split(1, rows) * cv * _silu(split(3, rows))).astype(BF16)
                _, _, vn = _layer_norm_head(split(5, rows), lg, lb)
                mixed = jnp.dot(wm, vn.astype(BF16), preferred_element_type=F32) + bias
                y_ref[rows, E_A + cols.start:E_A + cols.stop] = (split(4, rows) * mixed * _silu(split(6, rows))).astype(BF16)

    const3 = lambda i: (wl, 0, 0)
    const4 = lambda i: (wl, 0, 0, 0)
    return pl.pallas_call(
        body, name=name, grid=(S // TS,), out_shape=jax.ShapeDtypeStruct((S, 2 * E_A), BF16),
        in_specs=[pl.BlockSpec((TS, 7 * E_A), lambda i: (i, 0)), _halo_before(E_A, 0), _halo_before(E_A, 2),
                  pl.BlockSpec((None, 3, E_A), const3), pl.BlockSpec((None, 1, E_A), const3),
                  pl.BlockSpec((None, 1, E_A), const3), pl.BlockSpec((None, NDEV, CHUNK, CHUNK), const4),
                  pl.BlockSpec((None, NDEV, CHUNK, 1), const4)],
        out_specs=pl.BlockSpec((TS, 2 * E_A), lambda i: (i, 0)),
        compiler_params=_params(48, ("arbitrary",)),
    )(proj, proj, proj, convw, ln_g3, ln_b3, sgu_w, sgu_bcol)


def _even_mix_bwd(proj, d_ycat, convw, ln_g3, ln_b3, sgu_w, sgu_bcol, wl, name):
    nsteps = S // TS

    def body(pj_ref, hh_ref, hc_ref, hb_ref, hz_ref, dy_ref, hdy_ref, cw_ref, lg_ref, lb_ref, sw_ref, sb_ref,
             dp_ref, dcw_ref, dlg_ref, dlb_ref, dsw_ref, dsb_ref):
        step = pl.program_id(0)

        @pl.when(step == 0)
        def _():
            for ref in (dcw_ref, dlg_ref, dlb_ref, dsw_ref, dsb_ref):
                ref[...] = jnp.zeros_like(ref)

        live_before = (step > 0).astype(F32)
        live_after = (step < nsteps - 1).astype(F32)
        causal = _causal_mask()
        for j in range(E_A // HEAD):
            cols = slice(j * HEAD, (j + 1) * HEAD)
            w0, w1, w2 = cw_ref[0:1, cols], cw_ref[1:2, cols], cw_ref[2:3, cols]
            lg, lb = lg_ref[:, cols], lb_ref[:, cols]
            wmf = jnp.where(causal, sw_ref[j], 0.0)
            wm, wmt = wmf.astype(BF16), wmf.T.astype(BF16)
            bias = sb_ref[j]

            def split(s, rows, cols=cols):
                return pj_ref[rows, s * E_A + cols.start:s * E_A + cols.stop].astype(F32)

            def put(s, rows, val, cols=cols):
                dp_ref[rows, s * E_A + cols.start:s * E_A + cols.stop] = val.astype(BF16)

            ps = [split(2, slice(n * CHUNK, (n + 1) * CHUNK)) * split(0, slice(n * CHUNK, (n + 1) * CHUNK)) for n in range(NCH)]
            next_head = (hdy_ref[:, cols].astype(F32) * hb_ref[:, cols].astype(F32) * _silu(hz_ref[:, cols].astype(F32))
                         * live_after)
            acc_w = [jnp.zeros((1, HEAD), F32) for _ in range(3)]
            for n in reversed(range(NCH)):
                rows = slice(n * CHUNK, (n + 1) * CHUNK)
                p = ps[n]
                tail = ps[n - 1][CHUNK - HALO:] if n > 0 else hc_ref[:, cols].astype(F32) * hh_ref[:, cols].astype(F32) * live_before
                ext = jnp.concatenate([tail, p], axis=0)
                p1, p2 = _shift_down(ext, 1), _shift_down(ext, 2)
                cv = w2 * p + w1 * p1 + w0 * p2
                a_b, a_z = split(1, rows), split(3, rows)
                sz, dsz = _silu_and_grad(a_z)
                dya = dy_ref[rows, cols].astype(F32)
                put(1, rows, dya * cv * sz)
                put(3, rows, dya * a_b * cv * dsz)
                gcv = dya * a_b * sz
                acc_w[0] += jnp.sum(gcv * p2, axis=0, keepdims=True)
                acc_w[1] += jnp.sum(gcv * p1, axis=0, keepdims=True)
                acc_w[2] += jnp.sum(gcv * p, axis=0, keepdims=True)
                gext = jnp.concatenate([gcv, next_head], axis=0)
                next_head = gcv[:HALO]
                dpv = w2 * gcv + w1 * _shift_up(gext, 1) + w0 * _shift_up(gext, 2)
                put(2, rows, dpv * split(0, rows))
                put(0, rows, dpv * split(2, rows))
            for k in range(3):
                dcw_ref[k:k + 1, cols] += acc_w[k]

            acc_lg, acc_lb = jnp.zeros((1, HEAD), F32), jnp.zeros((1, HEAD), F32)
            acc_sw, acc_sb = jnp.zeros((CHUNK, CHUNK), F32), jnp.zeros((CHUNK, 1), F32)
            for n in range(NCH):
                rows = slice(n * CHUNK, (n + 1) * CHUNK)
                u, z = split(4, rows), split(6, rows)
                vhat, rstd, vn = _layer_norm_head(split(5, rows), lg, lb)
                vn16 = vn.astype(BF16)
                mixed = jnp.dot(wm, vn16, preferred_element_type=F32) + bias
                sz, dsz = _silu_and_grad(z)
                dyb = dy_ref[rows, E_A + cols.start:E_A + cols.stop].astype(F32)
                put(4, rows, dyb * mixed * sz)
                put(6, rows, dyb * u * mixed * dsz)
                dmix = dyb * u * sz
                dmix16 = dmix.astype(BF16)
                acc_sb += jnp.sum(dmix, axis=1, keepdims=True)
                acc_sw += lax.dot_general(dmix16, vn16, (((1,), (1,)), ((), ())), preferred_element_type=F32)
                dvn = jnp.dot(wmt, dmix16, preferred_element_type=F32)
                acc_lg += jnp.sum(dvn * vhat, axis=0, keepdims=True)
                acc_lb += jnp.sum(dvn, axis=0, keepdims=True)
                dvh = dvn * lg
                put(5, rows, rstd * (dvh - jnp.mean(dvh, axis=-1, keepdims=True)
                                     - vhat * jnp.mean(dvh * vhat, axis=-1, keepdims=True)))
            dlg_ref[:, cols] += acc_lg
            dlb_ref[:, cols] += acc_lb
            dsw_ref[j] += jnp.where(causal, acc_sw, 0.0)
            dsb_ref[j] += acc_sb

    const3 = lambda i: (wl, 0, 0)
    const4 = lambda i: (wl, 0, 0, 0)
    fixed2 = lambda i: (0, 0)
    fixed3 = lambda i: (0, 0, 0)
    return pl.pallas_call(
        body, name=name, grid=(nsteps,),
        out_shape=[jax.ShapeDtypeStruct((S, 7 * E_A), BF16), jax.ShapeDtypeStruct((3, E_A), F32),
                   jax.ShapeDtypeStruct((1, E_A), F32), jax.ShapeDtypeStruct((1, E_A), F32),
                   jax.ShapeDtypeStruct((NDEV, CHUNK, CHUNK), F32), jax.ShapeDtypeStruct((NDEV, CHUNK, 1), F32)],
        in_specs=[pl.BlockSpec((TS, 7 * E_A), lambda i: (i, 0)), _halo_before(E_A, 0), _halo_before(E_A, 2),
                  _halo_after(E_A, 1), _halo_after(E_A, 3),
                  pl.BlockSpec((TS, 2 * E_A), lambda i: (i, 0)), _halo_after(E_A, 0),
                  pl.BlockSpec((None, 3, E_A), const3), pl.BlockSpec((None, 1, E_A), const3),
                  pl.BlockSpec((None, 1, E_A), const3), pl.BlockSpec((None, NDEV, CHUNK, CHUNK), const4),
                  pl.BlockSpec((None, NDEV, CHUNK, 1), const4)],
        out_specs=[pl.BlockSpec((TS, 7 * E_A), lambda i: (i, 0)), pl.BlockSpec((3, E_A), fixed2),
                   pl.BlockSpec((1, E_A), fixed2), pl.BlockSpec((1, E_A), fixed2),
                   pl.BlockSpec((NDEV, CHUNK, CHUNK), fixed3), pl.BlockSpec((NDEV, CHUNK, 1), fixed3)],
        compiler_params=_params(56, ("arbitrary",)),
    )(proj, proj, proj, proj, proj, d_ycat, d_ycat, convw, ln_g3, ln_b3, sgu_w, sgu_bcol)


def _window_count(step, n, win, ext_before):
    rows = CHUNK if ext_before else CHUNK + HALO
    t = step * TS + n * CHUNK + lax.broadcasted_iota(jnp.int32, (rows, 1), 0)
    return jnp.minimum(t + 1, win).astype(F32)


def _pool_weight(wp_ref, g):
    return jnp.concatenate([wp_ref[d, g] for d in range(NDEV)], axis=0)


def _pooled_chunk(p, tail, win, count):
    sums = jnp.concatenate([tail, p], axis=0)
    shift = 1
    while shift < win:
        sums = sums + pltpu.roll(sums, shift, 0)
        shift *= 2
    return sums[HALO:] / count - p


def _pool_mix_fwd(proj, wpool, pscale4, wl, name):
    e_c = 4 * G_C

    def body(pj_ref, hp_ref, wp_ref, ps_ref, y_ref, pooled_scr, yraw_scr):
        step = pl.program_id(0)
        live = (step > 0).astype(F32)
        for g, win in enumerate(POOL_WINDOWS):
            for q in range(G_C // LANE):
                cols = slice(g * G_C + q * LANE, g * G_C + (q + 1) * LANE)
                tail = hp_ref[:, cols].astype(F32) * live
                for n in range(NCH):
                    rows = slice(n * CHUNK, (n + 1) * CHUNK)
                    p = pj_ref[rows, cols].astype(F32)
                    pooled_scr[rows, q * LANE:(q + 1) * LANE] = _pooled_chunk(
                        p, tail, win, _window_count(step, n, win, True)).astype(BF16)
                    tail = p[CHUNK - HALO:]
            yraw_scr[...] = jnp.dot(pooled_scr[...], _pool_weight(wp_ref, g), preferred_element_type=F32)
            for q in range(G_C // LANE):
                cols = slice(g * G_C + q * LANE, g * G_C + (q + 1) * LANE)
                for n in range(NCH):
                    rows = slice(n * CHUNK, (n + 1) * CHUNK)
                    z = pj_ref[rows, e_c + cols.start:e_c + cols.stop].astype(F32)
                    y_ref[rows, cols] = (yraw_scr[rows, q * LANE:(q + 1) * LANE] * ps_ref[:, cols] * _silu(z)).astype(BF16)

    return pl.pallas_call(
        body, name=name, grid=(S // TS,), out_shape=jax.ShapeDtypeStruct((S, e_c), BF16),
        in_specs=[pl.BlockSpec((TS, 2 * e_c), lambda i: (i, 0)), _halo_before(e_c, 0),
                  pl.BlockSpec((NDEV, None, 4, G_C // NDEV, G_C), lambda i: (0, wl, 0, 0, 0)),
                  pl.BlockSpec((None, 1, e_c), lambda i: (wl, 0, 0))],
        out_specs=pl.BlockSpec((TS, e_c), lambda i: (i, 0)),
        scratch_shapes=[pltpu.VMEM((TS, G_C), BF16), pltpu.VMEM((TS, G_C), F32)],
        compiler_params=_params(48, ("arbitrary",)),
    )(proj, proj, wpool, pscale4)


def _pool_mix_bwd(proj, d_ycat, wpool, pscale4, wl, prev, name):
    e_c = 4 * G_C
    nsteps = S // TS
    rb = G_C // NDEV

    def body(pj_ref, hp_ref, hz_ref, dy_ref, hdy_ref, wp_ref, ps_ref, *rest):
        dp_ref, dps_ref, dwp_ref, pooled_scr, yraw_scr, dyraw_scr, dpool_scr, acc_w = rest[-8:]
        step = pl.program_id(0)

        @pl.when(step == 0)
        def _():
            dps_ref[...] = jnp.zeros_like(dps_ref)
            acc_w[...] = jnp.zeros_like(acc_w)

        live_before = (step > 0).astype(F32)
        live_after = (step < nsteps - 1).astype(F32)
        for g, win in enumerate(POOL_WINDOWS):
            weight = _pool_weight(wp_ref, g)
            for q in range(G_C // LANE):
                cols = slice(g * G_C + q * LANE, g * G_C + (q + 1) * LANE)
                tail = hp_ref[:, cols].astype(F32) * live_before
                for n in range(NCH):
                    rows = slice(n * CHUNK, (n + 1) * CHUNK)
                    p = pj_ref[rows, cols].astype(F32)
                    pooled_scr[rows, q * LANE:(q + 1) * LANE] = _pooled_chunk(
                        p, tail, win, _window_count(step, n, win, True)).astype(BF16)
                    tail = p[CHUNK - HALO:]
            yraw_scr[...] = jnp.dot(pooled_scr[...], weight, preferred_element_type=F32)
            for q in range(G_C // LANE):
                cols = slice(g * G_C + q * LANE, g * G_C + (q + 1) * LANE)
                local = slice(q * LANE, (q + 1) * LANE)
                scale = ps_ref[:, cols]
                acc_ps = jnp.zeros((1, LANE), F32)
                for n in range(NCH):
                    rows = slice(n * CHUNK, (n + 1) * CHUNK)
                    sz, dsz = _silu_and_grad(pj_ref[rows, e_c + cols.start:e_c + cols.stop].astype(F32))
                    dyv = dy_ref[rows, cols].astype(F32)
                    yraw = yraw_scr[rows, local]
                    dyraw_scr[rows, local] = (dyv * scale * sz).astype(BF16)
                    acc_ps += jnp.sum(dyv * yraw * sz, axis=0, keepdims=True)
                    dp_ref[rows, e_c + cols.start:e_c + cols.stop] = (dyv * yraw * scale * dsz).astype(BF16)
                dps_ref[:, cols] += acc_ps
                dyraw_scr[TS:, local] = (hdy_ref[:, cols].astype(F32) * scale * _silu(hz_ref[:, cols].astype(F32))
                                         * live_after).astype(BF16)
            dpool_scr[...] = lax.dot_general(dyraw_scr[...], weight, (((1,), (1,)), ((), ())), preferred_element_type=F32)
            acc_w[g] += lax.dot_general(pooled_scr[...], dyraw_scr[:TS, :], (((0,), (0,)), ((), ())),
                                        preferred_element_type=F32)
            for q in range(G_C // LANE):
                cols = slice(g * G_C + q * LANE, g * G_C + (q + 1) * LANE)
                local = slice(q * LANE, (q + 1) * LANE)
                for n in range(NCH):
                    rows = slice(n * CHUNK, (n + 1) * CHUNK)
                    ext = dpool_scr[n * CHUNK:(n + 1) * CHUNK + HALO, local]
                    sums = ext / _window_count(step, n, win, False)
                    shift = 1
                    while shift < win:
                        sums = sums + pltpu.roll(sums, CHUNK + HALO - shift, 0)
                        shift *= 2
                    dp_ref[rows, cols] = (sums[:CHUNK] - ext[:CHUNK]).astype(BF16)

        @pl.when(step == nsteps - 1)
        def _():
            for g in range(4):
                for d in range(NDEV):
                    dwp_ref[d % 2, d // 2, g] = acc_w[g, d * rb:(d + 1) * rb, :].astype(BF16)

    in_specs = [pl.BlockSpec((TS, 2 * e_c), lambda i: (i, 0)), _halo_before(e_c, 0), _halo_after(e_c, 1),
                pl.BlockSpec((TS, e_c), lambda i: (i, 0)), _halo_after(e_c, 0),
                pl.BlockSpec((NDEV, None, 4, rb, G_C), lambda i: (0, wl, 0, 0, 0)),
                pl.BlockSpec((None, 1, e_c), lambda i: (wl, 0, 0))]
    args = [proj, proj, proj, d_ycat, d_ycat, wpool, pscale4]
    aliases = {}
    if prev is not None:
        in_specs.append(ANY)
        args.append(prev)
        aliases = {7: 2}
    return pl.pallas_call(
        body, name=name, grid=(nsteps,),
        out_shape=[jax.ShapeDtypeStruct((S, 2 * e_c), BF16), jax.ShapeDtypeStruct((1, e_c), F32),
                   jax.ShapeDtypeStruct((2, NDEV // 2) + wpool.shape[1:], BF16)],
        in_specs=in_specs,
        out_specs=[pl.BlockSpec((TS, 2 * e_c), lambda i: (i, 0)), pl.BlockSpec((1, e_c), lambda i: (0, 0)),
                   pl.BlockSpec((2, NDEV // 2, None, 4, rb, G_C), lambda i: (0, 0, wl, 0, 0, 0))],
        scratch_shapes=[pltpu.VMEM((TS, G_C), BF16), pltpu.VMEM((TS, G_C), F32), pltpu.VMEM((TS + HALO, G_C), BF16),
                        pltpu.VMEM((TS + HALO, G_C), F32), pltpu.VMEM((4, G_C, G_C), F32)],
        input_output_aliases=aliases, compiler_params=_params(56, ("arbitrary",)),
    )(*args)


def _adamw(w, g, m, v):
    m = ADAM_B1 * m + (1.0 - ADAM_B1) * g
    v = ADAM_B2 * v + (1.0 - ADAM_B2) * jnp.square(g)
    m_hat = m / (1.0 - ADAM_B1 ** ADAM_STEP)
    v_hat = v / (1.0 - ADAM_B2 ** ADAM_STEP)
    delta = -ADAM_LR * (m_hat / (jnp.sqrt(v_hat) + ADAM_EPS) + ADAM_WD * w)
    return delta, m, v


def _adam_sharded(w, m, v, recv, name):
    nl, nr, ncol = w.shape
    tr = 128

    def body(w_ref, m_ref, v_ref, r_ref, g_ref, d_ref, nm_ref, nv_ref):
        g = r_ref[0].astype(F32)
        for s in range(1, NDEV // 2):
            g = g + r_ref[s].astype(F32)
        g_ref[...] = g
        d_ref[...], nm_ref[...], nv_ref[...] = _adamw(w_ref[...], g, m_ref[...], v_ref[...])

    spec = pl.BlockSpec((None, tr, ncol), lambda l, i: (l, i, 0))
    return pl.pallas_call(
        body, name=name, grid=(nl, nr // tr), out_shape=[jax.ShapeDtypeStruct(w.shape, F32)] * 4,
        in_specs=[spec, spec, spec, pl.BlockSpec((NDEV // 2, None, tr, ncol), lambda l, i: (0, l, i, 0))],
        out_specs=[spec] * 4, compiler_params=_params(40, ("arbitrary", "arbitrary")),
    )(w, m, v, recv)


def _adam_small(w, g, m, v, name):
    def body(w_ref, g_ref, m_ref, v_ref, d_ref, nm_ref, nv_ref):
        d_ref[...], nm_ref[...], nv_ref[...] = _adamw(w_ref[...], g_ref[...], m_ref[...], v_ref[...])

    return pl.pallas_call(body, name=name, out_shape=[jax.ShapeDtypeStruct(w.shape, F32)] * 3,
                          in_specs=[VMEM_FULL] * 4, out_specs=[VMEM_FULL] * 3, compiler_params=_params(32))(w, g, m, v)


def _sum_devices(gathered, name):
    _, nr, ncol = gathered.shape

    def body(g_ref, o_ref):
        acc = g_ref[0]
        for s in range(1, NDEV):
            acc = acc + g_ref[s]
        o_ref[...] = acc

    return pl.pallas_call(body, name=name, grid=(1,), out_shape=jax.ShapeDtypeStruct((nr, ncol), F32),
                          in_specs=[pl.BlockSpec((NDEV, nr, ncol), lambda i: (0, 0, 0))],
                          out_specs=pl.BlockSpec((nr, ncol), lambda i: (0, 0)),
                          compiler_params=_params(48, ("arbitrary",)))(gathered)


def _ada_weight_adam(cact_t, dmod_mine, w, m, v):
    def body(ct_ref, dm_ref, w_ref, m_ref, v_ref, g_ref, d_ref, nm_ref, nv_ref):
        ct, dm = ct_ref[...], dm_ref[...]
        g = ct[:, 0:1] * dm[0:1, :]
        for e in range(1, NDEV):
            g = g + ct[:, e:e + 1] * dm[e:e + 1, :]
        g_ref[...] = g
        d_ref[...], nm_ref[...], nv_ref[...] = _adamw(w_ref[...], g, m_ref[...], v_ref[...])

    spec = pl.BlockSpec((None, D, ADA_NC), lambda l: (l, 0, 0))
    return pl.pallas_call(
        body, name="ada_weight_adam", grid=(DEPTH,), out_shape=[jax.ShapeDtypeStruct(w.shape, F32)] * 4,
        in_specs=[pl.BlockSpec((D, NDEV), lambda l: (0, 0)), pl.BlockSpec((None, NDEV, ADA_NC), lambda l: (l, 0, 0)),
                  spec, spec, spec],
        out_specs=[spec] * 4, compiler_params=_params(40, ("arbitrary",)),
    )(cact_t, dmod_mine, w, m, v)


def _pad_rows(a, rows):
    a = a.reshape(-1, D)
    return jnp.pad(a, ((0, rows - a.shape[0]), (0, 0)))


def kernel(x, c, norm_g, ada_w, ada_b, ab_w_in, ab_conv_w, ab_ln_g, ab_ln_b, ab_sgu_w, ab_sgu_b, ab_w_out, c_w_in, c_pool_w, c_pool_scale, c_w_out, final_g, loss_target, m_norm_g, m_ada_w, m_ada_b, m_ab_w_in, m_ab_conv_w, m_ab_ln_g, m_ab_ln_b, m_ab_sgu_w, m_ab_sgu_b, m_ab_w_out, m_c_w_in, m_c_pool_w, m_c_pool_scale, m_c_w_out, m_final_g, v_norm_g, v_ada_w, v_ada_b, v_ab_w_in, v_ab_conv_w, v_ab_ln_g, v_ab_ln_b, v_ab_sgu_w, v_ab_sgu_b, v_ab_w_out, v_c_w_in, v_c_pool_w, v_c_pool_scale, v_c_w_out, v_final_g):
    me = _index(_position())
    x0 = x.reshape(S, D)
    target = loss_target.reshape(S, D)
    norm_g3 = norm_g.reshape(DEPTH, 1, D)
    ln_g3, ln_b3 = ab_ln_g.reshape(2, 1, E_A), ab_ln_b.reshape(2, 1, E_A)
    sgu_bcol = ab_sgu_b.reshape(2, NDEV, CHUNK, 1)
    rb = G_C // NDEV

    cact_all, mod = _ada_forward(c, ada_w, ada_b)
    big = [_cast_bf16(ab_w_in.reshape(2 * D, -1), "cast_ab_w_in").reshape(ab_w_in.shape),
           _cast_bf16(ab_w_out.reshape(-1, D), "cast_ab_w_out").reshape(ab_w_out.shape),
           _cast_bf16(c_w_in.reshape(2 * D, -1), "cast_c_w_in").reshape(c_w_in.shape),
           _cast_bf16(c_pool_w.reshape(-1, G_C), "cast_c_pool_w").reshape(c_pool_w.shape),
           _cast_bf16(c_w_out.reshape(-1, D), "cast_c_w_out").reshape(c_w_out.shape)]
    wg_ab_in, wg_ab_out, wg_c_in, wg_pool, wg_c_out = _gather(big, "gather_weights")
    convw_all, pscale_all = _gather([ab_conv_w, c_pool_scale], "gather_small_weights")
    convw = jnp.transpose(convw_all, (1, 2, 0, 3)).reshape(2, 3, E_A)
    pscale4 = jnp.transpose(pscale_all, (1, 0, 2)).reshape(2, 1, 4 * G_C)

    xs, hs, projs, ycats, outs = [x0], [], [], [], []
    for layer in range(DEPTH):
        wl = layer // 2
        even = layer % 2 == 0
        h, proj = _norm_proj(xs[-1], mod, norm_g3, wg_ab_in if even else wg_c_in, layer, wl, f"norm_proj_{layer}")
        if even:
            ycat = _even_mix_fwd(proj, convw, ln_g3, ln_b3, ab_sgu_w, sgu_bcol, wl, f"even_mix_fwd_{layer}")
        else:
            ycat = _pool_mix_fwd(proj, wg_pool, pscale4, wl, f"pool_mix_fwd_{layer}")
        x_new, out = _out_proj(ycat, wg_ab_out if even else wg_c_out, wl, xs[-1], mod, layer, f"out_proj_{layer}")
        xs.append(x_new)
        hs.append(h)
        projs.append(proj)
        ycats.append(ycat)
        outs.append(out)

    dx, loss_part, d_final_g = _final_loss(xs[DEPTH], target, final_g.reshape(1, D))
    loss = lax.psum(loss_part[0, 0], ("x", "y", "c"))

    part = {"ab_w_in": None, "ab_w_out": None, "c_w_in": None, "c_pool_w": None, "c_w_out": None}
    d_mod, d_norm_g = [None] * DEPTH, [None] * DEPTH
    small = {}
    for layer in reversed(range(DEPTH)):
        wl = layer // 2
        even = layer % 2 == 0
        wg_in, wg_out = (wg_ab_in, wg_ab_out) if even else (wg_c_in, wg_c_out)
        key_in, key_out = ("ab_w_in", "ab_w_out") if even else ("c_w_in", "c_w_out")
        d_out, d_gate = _dout(dx, outs[layer], mod, layer, f"dout_{layer}")
        d_ycat = _dycat(d_out, wg_out, wl, f"dycat_{layer}")
        part[key_out] = _weight_grad(ycats[layer], d_out, 2, wl, part[key_out], f"grad_w_out_{layer}")
        if even:
            d_proj, d_cw, d_lg, d_lb, d_sw, d_sb = _even_mix_bwd(
                projs[layer], d_ycat, convw, ln_g3, ln_b3, ab_sgu_w, sgu_bcol, wl, f"even_mix_bwd_{layer}")
            small[layer] = (d_cw, d_lg, d_lb, d_sw, d_sb)
        else:
            d_proj, d_ps, part["c_pool_w"] = _pool_mix_bwd(
                projs[layer], d_ycat, wg_pool, pscale4, wl, part["c_pool_w"], f"pool_mix_bwd_{layer}")
            small[layer] = (d_ps,)
        d_h = _dh(d_proj, wg_in, wl, f"dh_{layer}")
        part[key_in] = _weight_grad(hs[layer], d_proj, 2, wl, part[key_in], f"grad_w_in_{layer}")
        dx, d_shift, d_scale, d_norm_g[layer] = _norm_bwd(d_h, xs[layer], dx, mod, norm_g3, layer, f"norm_bwd_{layer}")
        d_mod[layer] = jnp.concatenate([d_shift, d_scale, d_gate], axis=0)
    grad_x = dx.reshape(x.shape)

    keys = ["ab_w_in", "ab_w_out", "c_w_in", "c_pool_w", "c_w_out"]
    core = lax.axis_index("c").astype(jnp.int32).reshape(1)
    from_sibling = _pair_exchange([part[k] for k in keys], "pair_exchange_grads")
    chip_part = [_pair_sum(part[k], q, core, "pair_sum_" + k) for k, q in zip(keys, from_sibling)]
    recv = dict(zip(keys, _chip_scatter(chip_part, "scatter_grads")))
    big_w = {"ab_w_in": (ab_w_in, m_ab_w_in, v_ab_w_in), "ab_w_out": (ab_w_out, m_ab_w_out, v_ab_w_out),
             "c_w_in": (c_w_in, m_c_w_in, v_c_w_in), "c_pool_w": (c_pool_w, m_c_pool_w, v_c_pool_w),
             "c_w_out": (c_w_out, m_c_w_out, v_c_w_out)}
    res = {}
    for k in keys:
        w, m, v = big_w[k]
        flat = (2, w.size // (2 * w.shape[-1]), w.shape[-1])
        outs4 = _adam_sharded(w.reshape(flat), m.reshape(flat), v.reshape(flat), recv[k].reshape((NDEV // 2,) + flat), "adam_" + k)
        res[k] = [o.reshape(w.shape) for o in outs4]

    sections = [("norm_g", jnp.concatenate(d_norm_g, axis=0), 8),
                ("d_mod", jnp.concatenate(d_mod, axis=0), 16),
                ("ab_ln_g", jnp.concatenate([small[0][1], small[2][1]], axis=0), 8),
                ("ab_ln_b", jnp.concatenate([small[0][2], small[2][2]], axis=0), 8),
                ("ab_sgu_b", jnp.stack([small[0][4], small[2][4]]), 8),
                ("final_g", d_final_g, 8),
                ("ab_conv_w", jnp.stack([small[0][0], small[2][0]]), 8),
                ("c_pool_scale", jnp.concatenate([small[1][0], small[3][0]], axis=0), 8),
                ("ab_sgu_w", jnp.stack([small[0][3], small[2][3]]), 256)]
    offsets, at = {}, 0
    for name, _, rows in sections:
        offsets[name] = (at, rows)
        at += rows
    packed = jnp.concatenate([_pad_rows(a, rows) for _, a, rows in sections], axis=0)
    gathered = _gather([packed], "gather_small_grads")[0]
    summed = _sum_devices(gathered, "sum_small_grads")

    def section(name, nrows, src=summed):
        start = offsets[name][0]
        return src[..., start:start + nrows, :]

    grads = {
        "norm_g": section("norm_g", DEPTH),
        "ada_b": section("d_mod", 3 * DEPTH).reshape(DEPTH, 3 * D),
        "ab_ln_g": section("ab_ln_g", 2), "ab_ln_b": section("ab_ln_b", 2),
        "ab_sgu_b": section("ab_sgu_b", 2).reshape(ab_sgu_b.shape),
        "final_g": section("final_g", 1),
        "ab_sgu_w": section("ab_sgu_w", 256).reshape(ab_sgu_w.shape),
        "ab_conv_w": lax.dynamic_slice_in_dim(section("ab_conv_w", 6).reshape(2, 3, E_A), me * HEAD, HEAD, axis=2),
        "c_pool_scale": lax.dynamic_slice_in_dim(section("c_pool_scale", 4).reshape(2, 4 * G_C), me * 256, 256, axis=1),
    }
    small_w = {"norm_g": (norm_g, m_norm_g, v_norm_g), "ada_b": (ada_b, m_ada_b, v_ada_b),
               "ab_ln_g": (ab_ln_g, m_ab_ln_g, v_ab_ln_g), "ab_ln_b": (ab_ln_b, m_ab_ln_b, v_ab_ln_b),
               "ab_sgu_b": (ab_sgu_b, m_ab_sgu_b, v_ab_sgu_b),
               "final_g": (final_g.reshape(1, D), m_final_g.reshape(1, D), v_final_g.reshape(1, D)),
               "ab_sgu_w": (ab_sgu_w, m_ab_sgu_w, v_ab_sgu_w), "ab_conv_w": (ab_conv_w, m_ab_conv_w, v_ab_conv_w),
               "c_pool_scale": (c_pool_scale, m_c_pool_scale, v_c_pool_scale)}
    for k, (w, m, v) in small_w.items():
        res[k] = [grads[k]] + list(_adam_small(w, grads[k], m, v, "adam_" + k))
    res["final_g"] = [a.reshape(D) for a in res["final_g"]]

    dmod_all = section("d_mod", 3 * DEPTH, gathered).reshape(NDEV, DEPTH, 3 * D)
    dmod_mine = jnp.transpose(lax.dynamic_slice_in_dim(dmod_all, me * ADA_NC, ADA_NC, axis=2), (1, 0, 2))
    res["ada_w"] = _ada_weight_adam(jnp.transpose(cact_all.reshape(NDEV, D)), dmod_mine, ada_w, m_ada_w, v_ada_w)

    order = ["norm_g", "ada_w", "ada_b", "ab_w_in", "ab_conv_w", "ab_ln_g", "ab_ln_b", "ab_sgu_w", "ab_sgu_b",
             "ab_w_out", "c_w_in", "c_pool_w", "c_pool_scale", "c_w_out", "final_g"]
    return (loss, grad_x, *[res[k][0] for k in order], *[res[k][1] for k in order],
            *[res[k][2] for k in order], *[res[k][3] for k in order])
```

```python
import jax
import jax.numpy as jnp
from jax import lax
from jax.experimental import pallas as pl
from jax.experimental.pallas import tpu as pltpu

F32, BF16 = jnp.float32, jnp.bfloat16
S, D = 2048, 1024
NDEV = 8
DEPTH = 4
EPS = 1e-6
E_A = 1024
HEAD = 128
CHUNK = 128
POOL_WINDOWS = (2, 4, 8, 16)
G_C = 512
HALO = 16
ADA_NC = 384
MIB = 1024 * 1024
LANE = 128

ADAM_LR, ADAM_B1, ADAM_B2, ADAM_EPS, ADAM_WD, ADAM_STEP = 0.001, 0.9, 0.999, 1e-08, 0.01, 10

ANY = pl.BlockSpec(memory_space=pl.ANY)
VMEM_FULL = pl.BlockSpec(memory_space=pltpu.VMEM)
IN_HBM = pl.BlockSpec(memory_space=pltpu.HBM)
SEMAPHORES = pl.BlockSpec(memory_space=pltpu.SEMAPHORE)
IN_FLIGHT = pltpu.SideEffectType.DATAFLOW_SIDE_EFFECTING


def _params(vmem_mib, semantics=None):
    return pltpu.CompilerParams(dimension_semantics=semantics, vmem_limit_bytes=vmem_mib * MIB)


def _silu(z):
    return z * jax.nn.sigmoid(z)


def _silu_and_grad(z):
    sig = jax.nn.sigmoid(z)
    return z * sig, sig * (1.0 + z * (1.0 - sig))


def _position():
    return lax.axis_index("x"), lax.axis_index("y"), lax.axis_index("c")


def _index(pos):
    return 4 * pos[0] + 2 * pos[1] + pos[2]


def _peer(pos, k):
    flipped = tuple(1 - p if (k >> (2 - b)) & 1 else p for b, p in enumerate(pos))
    return flipped, _index(flipped)


def _remote(src, dst, send_sem, recv_sem, device):
    return pltpu.make_async_remote_copy(src_ref=src, dst_ref=dst, send_sem=send_sem, recv_sem=recv_sem,
                                        device_id=device, device_id_type=pl.DeviceIdType.MESH)


def _gather(arrays, name):
    n = len(arrays)
    out_shape = [jax.ShapeDtypeStruct((NDEV,) + a.shape, a.dtype) for a in arrays]

    def body(*refs):
        ins, outs = refs[:n], refs[n:2 * n]
        send_sems, recv_sems, own_sems = refs[2 * n:]
        x, y, c = _position()
        me = _index((x, y, c))
        sibling = (x, y, 1 - c)
        chips = [(1 - x, y), (x, 1 - y), (1 - x, 1 - y)]

        def block_copy(j, k, owner, to, src=None):
            rows = outs[j].at[_index(owner)]
            return _remote(rows if src is None else src, rows, send_sems.at[j, k], recv_sems.at[j, k], to)

        own, first, passed = [], [], []
        for j in range(n):
            own.append(pltpu.make_async_copy(ins[j], outs[j].at[me], own_sems.at[j]))
            first.append(block_copy(j, 0, (x, y, c), sibling, src=ins[j]))
            first += [block_copy(j, 1 + q, (x, y, c), (*chip, c), src=ins[j]) for q, chip in enumerate(chips)]
        for copy in own + first:
            copy.start()
        for q, chip in enumerate(chips):
            for j in range(n):
                block_copy(j, 1 + q, (*chip, c), (x, y, c)).wait_recv()
                forward = block_copy(j, 4 + q, (*chip, c), sibling)
                forward.start()
                passed.append(forward)
        for j in range(n):
            block_copy(j, 0, sibling, (x, y, c)).wait_recv()
            for q, chip in enumerate(chips):
                block_copy(j, 4 + q, (*chip, 1 - c), (x, y, c)).wait_recv()
        for copy in first + passed:
            copy.wait_send()
        for copy in own:
            copy.wait()

    return pl.pallas_call(
        body, name=name, out_shape=out_shape, in_specs=[ANY] * n, out_specs=[ANY] * n,
        scratch_shapes=[pltpu.SemaphoreType.DMA((n, NDEV - 1)), pltpu.SemaphoreType.DMA((n, NDEV - 1)),
                        pltpu.SemaphoreType.DMA((n,))],
    )(*arrays)


def _pair_exchange(parts, name):
    n = len(parts)

    def body(*refs):
        ins, outs = refs[:n], refs[n:2 * n]
        send_sems, recv_sems = refs[2 * n:]
        x, y, c = _position()
        copies = [_remote(ins[j].at[1 - c], outs[j], send_sems.at[j], recv_sems.at[j], (x, y, 1 - c)) for j in range(n)]
        for copy in copies:
            copy.start()
        for copy in copies:
            copy.wait()

    return pl.pallas_call(
        body, name=name, out_shape=[jax.ShapeDtypeStruct(p.shape[1:], p.dtype) for p in parts],
        in_specs=[ANY] * n, out_specs=[ANY] * n,
        scratch_shapes=[pltpu.SemaphoreType.DMA((n,)), pltpu.SemaphoreType.DMA((n,))],
    )(*parts)


def _pair_sum(part, from_sibling, core, name):
    ncol = part.shape[-1]
    p3 = part.reshape(2, -1, ncol)
    q2 = from_sibling.reshape(-1, ncol)
    nrows = q2.shape[0]
    tr = 512

    def body(core_ref, p_ref, q_ref, o_ref):
        o_ref[...] = (p_ref[...].astype(F32) + q_ref[...].astype(F32)).astype(BF16)

    grid_spec = pltpu.PrefetchScalarGridSpec(
        num_scalar_prefetch=1, grid=(nrows // tr,),
        in_specs=[pl.BlockSpec((None, tr, ncol), lambda i, core_ref: (core_ref[0], i, 0)),
                  pl.BlockSpec((tr, ncol), lambda i, core_ref: (i, 0))],
        out_specs=pl.BlockSpec((tr, ncol), lambda i, core_ref: (i, 0)))
    out = pl.pallas_call(body, name=name, grid_spec=grid_spec, out_shape=jax.ShapeDtypeStruct(q2.shape, BF16),
                         compiler_params=_params(32, ("arbitrary",)))(core, p3, q2)
    return out.reshape(from_sibling.shape)


def _in_hbm(a):
    return pltpu.with_memory_space_constraint(a, pltpu.HBM)


def _chips(x, y):
    return [(1 - x, y), (x, 1 - y), (1 - x, 1 - y)]


def _gather_start(arrays, after, name):
    n = len(arrays)
    lands = [_in_hbm(lax.empty((NDEV,) + a.shape, a.dtype)) for a in arrays]
    extra = [] if after is None else [after]

    def body(*refs):
        ins, zones = refs[:n], refs[n:2 * n]
        send_sems, recv_sems = refs[2 * n + len(extra):2 * n + len(extra) + 2]
        token, own_sems = refs[-2], refs[-1]
        x, y, c = _position()
        me = _index((x, y, c))
        own = [pltpu.make_async_copy(ins[j], zones[j].at[me], own_sems.at[j]) for j in range(n)]
        for copy in own:
            copy.start()
        for j in range(n):
            _remote(ins[j], zones[j].at[me], send_sems.at[4 * j], recv_sems.at[4 * j], (x, y, 1 - c)).start()
            for q, chip in enumerate(_chips(x, y)):
                _remote(ins[j], zones[j].at[me], send_sems.at[4 * j + 1 + q], recv_sems.at[4 * j + 1 + q], (*chip, c)).start()
        for copy in own:
            copy.wait()
        token[...] = jnp.zeros_like(token)

    outs = pl.pallas_call(
        body, name=name,
        out_shape=(pltpu.SemaphoreType.DMA((4 * n,)), pltpu.SemaphoreType.DMA((4 * n,)),
                   *[pltpu.HBM(a.shape, a.dtype) for a in arrays], *[pltpu.HBM(z.shape, z.dtype) for z in lands],
                   jax.ShapeDtypeStruct((8, LANE), F32)),
        in_specs=[IN_HBM] * (2 * n) + [ANY] * len(extra),
        out_specs=(SEMAPHORES, SEMAPHORES, *[IN_HBM] * (2 * n), VMEM_FULL),
        input_output_aliases={j: 2 + j for j in range(2 * n)},
        scratch_shapes=[pltpu.SemaphoreType.DMA((n,))],
        compiler_params=pltpu.CompilerParams(has_side_effects=IN_FLIGHT),
    )(*[_in_hbm(a) for a in arrays], *lands, *extra)
    return outs[0], outs[1], list(outs[2:2 + n]), list(outs[2 + n:2 + 2 * n]), outs[-1]


def _gather_mid(zones, recv_sems, after, name):
    n = len(zones)

    def body(*refs):
        zs, first_recv = refs[:n], refs[n]
        send_sems, recv_sems2 = refs[n + 2:n + 4]
        x, y, c = _position()
        for q, chip in enumerate(_chips(x, y)):
            for j in range(n):
                rows = zs[j].at[_index((*chip, c))]
                _remote(rows, rows, send_sems.at[3 * j + q], first_recv.at[4 * j + 1 + q], (x, y, c)).wait_recv()
                _remote(rows, rows, send_sems.at[3 * j + q], recv_sems2.at[3 * j + q], (x, y, 1 - c)).start()

    outs = pl.pallas_call(
        body, name=name,
        out_shape=(pltpu.SemaphoreType.DMA((3 * n,)), pltpu.SemaphoreType.DMA((3 * n,)),
                   *[pltpu.HBM(z.shape, z.dtype) for z in zones]),
        in_specs=[IN_HBM] * n + [SEMAPHORES, ANY], out_specs=(SEMAPHORES, SEMAPHORES, *[IN_HBM] * n),
        input_output_aliases={j: 2 + j for j in range(n)},
        compiler_params=pltpu.CompilerParams(has_side_effects=IN_FLIGHT),
    )(*zones, recv_sems, after)
    return outs[0], outs[1], list(outs[2:])


def _gather_end(arrays, zones, send1, recv1, send2, recv2, name):
    n = len(zones)

    def body(*refs):
        ins, zs = refs[:n], refs[n:2 * n]
        s1, r1, s2, r2 = refs[2 * n:2 * n + 4]
        x, y, c = _position()
        me = (x, y, c)
        for j in range(n):
            rows = zs[j].at[_index((x, y, 1 - c))]
            _remote(rows, rows, s1.at[4 * j], r1.at[4 * j], me).wait_recv()
            for q, chip in enumerate(_chips(x, y)):
                rows = zs[j].at[_index((*chip, 1 - c))]
                _remote(rows, rows, s2.at[3 * j + q], r2.at[3 * j + q], me).wait_recv()
        for j in range(n):
            for k in range(4):
                _remote(ins[j], zs[j].at[0], s1.at[4 * j + k], r1.at[4 * j + k], me).wait_send()
            for q in range(3):
                _remote(zs[j].at[0], zs[j].at[0], s2.at[3 * j + q], r2.at[3 * j + q], me).wait_send()

    outs = pl.pallas_call(
        body, name=name, out_shape=tuple(pltpu.HBM(z.shape, z.dtype) for z in zones),
        in_specs=[IN_HBM] * (2 * n) + [SEMAPHORES] * 4, out_specs=tuple([IN_HBM] * n),
        input_output_aliases={n + j: j for j in range(n)},
        compiler_params=pltpu.CompilerParams(has_side_effects=IN_FLIGHT),
    )(*arrays, *zones, send1, recv1, send2, recv2)
    return list(outs)


def _scatter_start(parts, name):
    n = len(parts)
    lands = [_in_hbm(lax.empty((3,) + p.shape[1:], p.dtype)) for p in parts]

    def body(*refs):
        ins, zones = refs[:n], refs[n:2 * n]
        send_sems, recv_sems = refs[2 * n:2 * n + 2]
        x, y, c = _position()
        for j in range(n):
            for q, (px, py) in enumerate(_chips(x, y)):
                _remote(ins[j].at[2 * px + py], zones[j].at[q], send_sems.at[3 * j + q], recv_sems.at[3 * j + q],
                        (px, py, c)).start()

    outs = pl.pallas_call(
        body, name=name,
        out_shape=(pltpu.SemaphoreType.DMA((3 * n,)), pltpu.SemaphoreType.DMA((3 * n,)),
                   *[pltpu.HBM(p.shape, p.dtype) for p in parts], *[pltpu.HBM(z.shape, z.dtype) for z in lands]),
        in_specs=[IN_HBM] * (2 * n), out_specs=(SEMAPHORES, SEMAPHORES, *[IN_HBM] * (2 * n)),
        input_output_aliases={j: 2 + j for j in range(2 * n)},
        compiler_params=pltpu.CompilerParams(has_side_effects=IN_FLIGHT),
    )(*[_in_hbm(p) for p in parts], *lands)
    return outs[0], outs[1], list(outs[2:2 + n]), list(outs[2 + n:])


def _scatter_end(parts, zones, send_sems, recv_sems, after, name):
    n = len(parts)

    def body(*refs):
        ins, zs = refs[:n], refs[n:2 * n]
        s, r = refs[2 * n:2 * n + 2]
        me = _position()
        for j in range(n):
            for q in range(3):
                copy = _remote(ins[j].at[0], zs[j].at[q], s.at[3 * j + q], r.at[3 * j + q], me)
                copy.wait_send()
                copy.wait_recv()

    outs = pl.pallas_call(
        body, name=name,
        out_shape=(*[pltpu.HBM(p.shape, p.dtype) for p in parts], *[pltpu.HBM(z.shape, z.dtype) for z in zones]),
        in_specs=[IN_HBM] * (2 * n) + [SEMAPHORES, SEMAPHORES, ANY], out_specs=tuple([IN_HBM] * (2 * n)),
        input_output_aliases={j: j for j in range(2 * n)},
        compiler_params=pltpu.CompilerParams(has_side_effects=IN_FLIGHT),
    )(*parts, *zones, send_sems, recv_sems, after)
    return list(outs[:n]), list(outs[n:])


def _ada_forward(c, ada_w, ada_b):
    def body(c_ref, w_ref, b_ref, cact_ref, mod_ref, gbuf, modrow, send_sems, recv_sems):
        pos = _position()
        me = _index(pos)

        def to_all(ref, round_):
            copies = []
            for k in range(1, NDEV):
                peer, _ = _peer(pos, k)
                copy = pltpu.make_async_remote_copy(
                    src_ref=ref.at[me], dst_ref=ref.at[me], send_sem=send_sems.at[round_, k - 1],
                    recv_sem=recv_sems.at[round_, k - 1], device_id=peer, device_id_type=pl.DeviceIdType.MESH)
                copy.start()
                copies.append(copy)
            for copy in copies:
                copy.wait()

        cact_ref[me] = _silu(c_ref[...])
        to_all(cact_ref, 0)
        rows = lax.broadcasted_iota(jnp.int32, (NDEV, D), 0)
        cact = jnp.zeros((NDEV, D), F32)
        for e in range(NDEV):
            cact = jnp.where(rows == e, cact_ref[e], cact)
        cact = cact.astype(BF16)
        for l in range(DEPTH):
            gbuf[me, l] = jnp.dot(cact, w_ref[l].astype(BF16), preferred_element_type=F32)
        to_all(gbuf, 1)
        mine = lax.broadcasted_iota(jnp.int32, (NDEV, ADA_NC), 0) == me
        for l in range(DEPTH):
            for d in range(NDEV):
                modrow[:, d * ADA_NC:(d + 1) * ADA_NC] = jnp.sum(jnp.where(mine, gbuf[d, l], 0.0), axis=0, keepdims=True)
            full = modrow[...] + b_ref[l:l + 1, :]
            for w in range(3):
                mod_ref[l, w] = full[:, w * D:(w + 1) * D]

    return pl.pallas_call(
        body, name="ada_forward",
        out_shape=[jax.ShapeDtypeStruct((NDEV, 1, D), F32), jax.ShapeDtypeStruct((DEPTH, 3, 1, D), F32)],
        in_specs=[VMEM_FULL] * 3, out_specs=[VMEM_FULL] * 2,
        scratch_shapes=[pltpu.VMEM((NDEV, DEPTH, NDEV, ADA_NC), F32), pltpu.VMEM((1, 3 * D), F32),
                        pltpu.SemaphoreType.DMA((2, NDEV - 1)), pltpu.SemaphoreType.DMA((2, NDEV - 1))],
        compiler_params=_params(40),
    )(c, ada_w, ada_b)


def _cast_bf16(a, wl, name):
    _, rows, cols = a.shape
    tr = 256

    def body(a_ref, o_ref):
        o_ref[...] = a_ref[...].astype(BF16)

    return pl.pallas_call(body, name=name, grid=(rows // tr,), out_shape=jax.ShapeDtypeStruct((rows, cols), BF16),
                          in_specs=[pl.BlockSpec((None, tr, cols), lambda i: (wl, i, 0))],
                          out_specs=pl.BlockSpec((tr, cols), lambda i: (i, 0)),
                          compiler_params=_params(32, ("arbitrary",)))(a)


def _mod_spec(layer, which, ngrid):
    index = {1: lambda i: (layer, which, 0, 0), 2: lambda i, j: (layer, which, 0, 0)}[ngrid]
    return pl.BlockSpec((None, None, 1, D), index)


def _norm_proj(x, mod, norm_g3, wg, layer, name):
    nb = wg.shape[-1]
    tm = 512

    def body(x_ref, g_ref, shift_ref, scale_ref, w_ref, h_ref, p_ref):
        @pl.when(pl.program_id(1) == 0)
        def _():
            xv = x_ref[...]
            r = lax.rsqrt(jnp.mean(xv * xv, axis=-1, keepdims=True) + EPS)
            hn = xv * r * g_ref[...]
            h_ref[...] = (hn * (1.0 + scale_ref[...]) + shift_ref[...]).astype(BF16)

        p_ref[...] = jnp.dot(h_ref[...], w_ref[...], preferred_element_type=F32).astype(BF16)

    return pl.pallas_call(
        body, name=name, grid=(S // tm, NDEV),
        out_shape=[jax.ShapeDtypeStruct((S, D), BF16), jax.ShapeDtypeStruct((S, NDEV * nb), BF16)],
        in_specs=[pl.BlockSpec((tm, D), lambda i, d: (i, 0)),
                  pl.BlockSpec((None, 1, D), lambda i, d: (layer, 0, 0)),
                  _mod_spec(layer, 0, 2), _mod_spec(layer, 1, 2),
                  pl.BlockSpec((None, D, nb), lambda i, d: (d, 0, 0))],
        out_specs=[pl.BlockSpec((tm, D), lambda i, d: (i, 0)), pl.BlockSpec((tm, nb), lambda i, d: (i, d))],
        compiler_params=_params(48, ("arbitrary", "arbitrary")),
    )(x, norm_g3, mod, mod, wg)


def _out_proj(ycat, wg_out, x, mod, layer, name):
    tm = 512
    rb = wg_out.shape[1]

    def body(y_ref, w_ref, x_ref, gate_ref, xn_ref, o_ref):
        acc = jnp.zeros((tm, D), F32)
        for d in range(NDEV):
            acc = acc + jnp.dot(y_ref[:, d * rb:(d + 1) * rb], w_ref[d], preferred_element_type=F32)
        o_ref[...] = acc.astype(BF16)
        xn_ref[...] = x_ref[...] + gate_ref[...] * acc

    return pl.pallas_call(
        body, name=name, grid=(S // tm,),
        out_shape=[jax.ShapeDtypeStruct((S, D), F32), jax.ShapeDtypeStruct((S, D), BF16)],
        in_specs=[pl.BlockSpec((tm, NDEV * rb), lambda i: (i, 0)),
                  pl.BlockSpec((NDEV, rb, D), lambda i: (0, 0, 0)),
                  pl.BlockSpec((tm, D), lambda i: (i, 0)), _mod_spec(layer, 2, 1)],
        out_specs=[pl.BlockSpec((tm, D), lambda i: (i, 0))] * 2,
        compiler_params=_params(48, ("arbitrary",)),
    )(ycat, wg_out, x, mod)


def _final_loss(x, target, final_g2):
    tm = 256

    def body(x_ref, t_ref, g_ref, dx_ref, loss_ref, dg_ref):
        @pl.when(pl.program_id(0) == 0)
        def _():
            loss_ref[...] = jnp.zeros_like(loss_ref)
            dg_ref[...] = jnp.zeros_like(dg_ref)

        xv, g = x_ref[...], g_ref[...]
        r = lax.rsqrt(jnp.mean(xv * xv, axis=-1, keepdims=True) + EPS)
        xn = xv * r
        err = xn * g - t_ref[...]
        loss_ref[...] += 0.5 * jnp.sum(jnp.mean(err * err, axis=-1, keepdims=True), axis=0, keepdims=True)
        dy = err * (1.0 / D)
        dg_ref[...] += jnp.sum(dy * xn, axis=0, keepdims=True)
        u = dy * g
        dx_ref[...] = r * (u - xn * jnp.mean(xn * u, axis=-1, keepdims=True))

    tile = pl.BlockSpec((tm, D), lambda i: (i, 0))
    row = pl.BlockSpec((1, D), lambda i: (0, 0))
    return pl.pallas_call(
        body, name="final_loss", grid=(S // tm,),
        out_shape=[jax.ShapeDtypeStruct((S, D), F32), jax.ShapeDtypeStruct((1, LANE), F32), jax.ShapeDtypeStruct((1, D), F32)],
        in_specs=[tile, tile, row], out_specs=[tile, pl.BlockSpec((1, LANE), lambda i: (0, 0)), row],
        compiler_params=_params(32, ("arbitrary",)),
    )(x, target, final_g2)


def _dout(dx, out, mod, layer, name):
    tm = 256

    def body(dx_ref, o_ref, gate_ref, do_ref, dgate_ref):
        @pl.when(pl.program_id(0) == 0)
        def _():
            dgate_ref[...] = jnp.zeros_like(dgate_ref)

        dxv = dx_ref[...]
        do_ref[...] = (gate_ref[...] * dxv).astype(BF16)
        dgate_ref[...] += jnp.sum(dxv * o_ref[...].astype(F32), axis=0, keepdims=True)

    tile = pl.BlockSpec((tm, D), lambda i: (i, 0))
    return pl.pallas_call(
        body, name=name, grid=(S // tm,),
        out_shape=[jax.ShapeDtypeStruct((S, D), BF16), jax.ShapeDtypeStruct((1, D), F32)],
        in_specs=[tile, tile, _mod_spec(layer, 2, 1)], out_specs=[tile, pl.BlockSpec((1, D), lambda i: (0, 0))],
        compiler_params=_params(32, ("arbitrary",)),
    )(dx, out, mod)


def _dycat(d_out, wg_out, name):
    tm = 512
    rb = wg_out.shape[1]

    def body(do_ref, w_ref, dy_ref):
        dov = do_ref[...]
        for d in range(NDEV):
            dy_ref[:, d * rb:(d + 1) * rb] = lax.dot_general(
                dov, w_ref[d], (((1,), (1,)), ((), ())), preferred_element_type=F32).astype(BF16)

    return pl.pallas_call(
        body, name=name, grid=(S // tm,), out_shape=jax.ShapeDtypeStruct((S, NDEV * rb), BF16),
        in_specs=[pl.BlockSpec((tm, D), lambda i: (i, 0)), pl.BlockSpec((NDEV, rb, D), lambda i: (0, 0, 0))],
        out_specs=pl.BlockSpec((tm, NDEV * rb), lambda i: (i, 0)),
        compiler_params=_params(48, ("arbitrary",)),
    )(d_out, wg_out)


def _weight_grad(a, b, name):
    a_blocks = a.shape[1] != D
    ka = a.shape[1] // NDEV if a_blocks else a.shape[1]
    nb = b.shape[1] if a_blocks else b.shape[1] // NDEV

    def body(a_ref, b_ref, o_ref):
        o_ref[...] = lax.dot_general(a_ref[...], b_ref[...], (((0,), (0,)), ((), ())),
                                     preferred_element_type=F32).astype(BF16)

    return pl.pallas_call(
        body, name=name, grid=(NDEV,), out_shape=jax.ShapeDtypeStruct((2, NDEV // 2, ka, nb), BF16),
        in_specs=[pl.BlockSpec((S, ka), (lambda d: (0, d)) if a_blocks else (lambda d: (0, 0))),
                  pl.BlockSpec((S, nb), (lambda d: (0, 0)) if a_blocks else (lambda d: (0, d)))],
        out_specs=pl.BlockSpec((None, None, ka, nb), lambda d: (d % 2, d // 2, 0, 0)),
        compiler_params=_params(56, ("arbitrary",)),
    )(a, b)


def _dh(d_proj, wg, name):
    nb = wg.shape[-1]
    tm = 512

    def body(dp_ref, w_ref, dh_ref):
        part = lax.dot_general(dp_ref[...], w_ref[...], (((1,), (1,)), ((), ())), preferred_element_type=F32)

        @pl.when(pl.program_id(1) == 0)
        def _():
            dh_ref[...] = part

        @pl.when(pl.program_id(1) != 0)
        def _():
            dh_ref[...] += part

    return pl.pallas_call(
        body, name=name, grid=(S // tm, NDEV), out_shape=jax.ShapeDtypeStruct((S, D), F32),
        in_specs=[pl.BlockSpec((tm, nb), lambda i, d: (i, d)), pl.BlockSpec((None, D, nb), lambda i, d: (d, 0, 0))],
        out_specs=pl.BlockSpec((tm, D), lambda i, d: (i, 0)),
        compiler_params=_params(48, ("arbitrary", "arbitrary")),
    )(d_proj, wg)


def _norm_bwd(d_h, x, dx, mod, norm_g3, layer, name):
    tm = 256

    def body(dh_ref, x_ref, dx_ref, g_ref, scale_ref, dxi_ref, dshift_ref, dscale_ref, dg_ref):
        @pl.when(pl.program_id(0) == 0)
        def _():
            dshift_ref[...] = jnp.zeros_like(dshift_ref)
            dscale_ref[...] = jnp.zeros_like(dscale_ref)
            dg_ref[...] = jnp.zeros_like(dg_ref)

        xv, dhv, g = x_ref[...], dh_ref[...], g_ref[...]
        r = lax.rsqrt(jnp.mean(xv * xv, axis=-1, keepdims=True) + EPS)
        xn = xv * r
        dshift_ref[...] += jnp.sum(dhv, axis=0, keepdims=True)
        dscale_ref[...] += jnp.sum(dhv * (xn * g), axis=0, keepdims=True)
        dhn = dhv * (1.0 + scale_ref[...])
        dg_ref[...] += jnp.sum(dhn * xn, axis=0, keepdims=True)
        u = dhn * g
        dxi_ref[...] = dx_ref[...] + r * (u - xn * jnp.mean(xn * u, axis=-1, keepdims=True))

    tile = pl.BlockSpec((tm, D), lambda i: (i, 0))
    row = pl.BlockSpec((1, D), lambda i: (0, 0))
    return pl.pallas_call(
        body, name=name, grid=(S // tm,),
        out_shape=[jax.ShapeDtypeStruct((S, D), F32)] + [jax.ShapeDtypeStruct((1, D), F32)] * 3,
        in_specs=[tile, tile, tile, pl.BlockSpec((None, 1, D), lambda i: (layer, 0, 0)), _mod_spec(layer, 1, 1)],
        out_specs=[tile, row, row, row], compiler_params=_params(40, ("arbitrary",)),
    )(d_h, x, dx, norm_g3, mod)


TS = 256
NCH = TS // CHUNK
HALO_BLOCKS = TS // HALO


def _halo_before(width, col_block):
    return pl.BlockSpec((HALO, width), lambda i: (jnp.maximum(i * HALO_BLOCKS - 1, 0), col_block))


def _halo_after(width, col_block):
    return pl.BlockSpec((HALO, width), lambda i: (jnp.minimum((i + 1) * HALO_BLOCKS, S // HALO - 1), col_block))


def _shift_down(ext, k):
    return pltpu.roll(ext, k, 0)[HALO:]


def _shift_up(ext, k):
    return pltpu.roll(ext, ext.shape[0] - k, 0)[:ext.shape[0] - HALO]


def _layer_norm_head(v, lg, lb):
    mu = jnp.mean(v, axis=-1, keepdims=True)
    vc = v - mu
    rstd = lax.rsqrt(jnp.mean(vc * vc, axis=-1, keepdims=True) + EPS)
    vhat = vc * rstd
    return vhat, rstd, vhat * lg + lb


def _causal_mask():
    return lax.broadcasted_iota(jnp.int32, (CHUNK, CHUNK), 0) >= lax.broadcasted_iota(jnp.int32, (CHUNK, CHUNK), 1)


def _even_mix_fwd(proj, convw, ln_g3, ln_b3, sgu_w, sgu_bcol, wl, name):
    def body(pj_ref, hh_ref, hc_ref, cw_ref, lg_ref, lb_ref, sw_ref, sb_ref, y_ref):
        live = (pl.program_id(0) > 0).astype(F32)
        causal = _causal_mask()
        for j in range(E_A // HEAD):
            cols = slice(j * HEAD, (j + 1) * HEAD)
            w0, w1, w2 = cw_ref[0:1, cols], cw_ref[1:2, cols], cw_ref[2:3, cols]
            lg, lb = lg_ref[:, cols], lb_ref[:, cols]
            wm = jnp.where(causal, sw_ref[j], 0.0).astype(BF16)
            bias = sb_ref[j]

            def split(s, rows, cols=cols):
                return pj_ref[rows, s * E_A + cols.start:s * E_A + cols.stop].astype(F32)

            prev_tail = hc_ref[:, cols].astype(F32) * hh_ref[:, cols].astype(F32) * live
            for n in range(NCH):
                rows = slice(n * CHUNK, (n + 1) * CHUNK)
                p = split(2, rows) * split(0, rows)
                ext = jnp.concatenate([prev_tail, p], axis=0)
                prev_tail = p[CHUNK - HALO:]
                cv = w2 * p + w1 * _shift_down(ext, 1) + w0 * _shift_down(ext, 2)
                y_ref[rows, cols] = (split(1, rows) * cv * _silu(split(3, rows))).astype(BF16)
                _, _, vn = _layer_norm_head(split(5, rows), lg, lb)
                mixed = jnp.dot(wm, vn.astype(BF16), preferred_element_type=F32) + bias
                y_ref[rows, E_A + cols.start:E_A + cols.stop] = (split(4, rows) * mixed * _silu(split(6, rows))).astype(BF16)

    const3 = lambda i: (wl, 0, 0)
    const4 = lambda i: (wl, 0, 0, 0)
    return pl.pallas_call(
        body, name=name, grid=(S // TS,), out_shape=jax.ShapeDtypeStruct((S, 2 * E_A), BF16),
        in_specs=[pl.BlockSpec((TS, 7 * E_A), lambda i: (i, 0)), _halo_before(E_A, 0), _halo_before(E_A, 2),
                  pl.BlockSpec((None, 3, E_A), const3), pl.BlockSpec((None, 1, E_A), const3),
                  pl.BlockSpec((None, 1, E_A), const3), pl.BlockSpec((None, NDEV, CHUNK, CHUNK), const4),
                  pl.BlockSpec((None, NDEV, CHUNK, 1), const4)],
        out_specs=pl.BlockSpec((TS, 2 * E_A), lambda i: (i, 0)),
        compiler_params=_params(48, ("arbitrary",)),
    )(proj, proj, proj, convw, ln_g3, ln_b3, sgu_w, sgu_bcol)


def _even_mix_bwd(proj, d_ycat, convw, ln_g3, ln_b3, sgu_w, sgu_bcol, wl, name):
    nsteps = S // TS

    def body(pj_ref, hh_ref, hc_ref, hb_ref, hz_ref, dy_ref, hdy_ref, cw_ref, lg_ref, lb_ref, sw_ref, sb_ref,
             dp_ref, dcw_ref, dlg_ref, dlb_ref, dsw_ref, dsb_ref):
        step = pl.program_id(0)

        @pl.when(step == 0)
        def _():
            for ref in (dcw_ref, dlg_ref, dlb_ref, dsw_ref, dsb_ref):
                ref[...] = jnp.zeros_like(ref)

        live_before = (step > 0).astype(F32)
        live_after = (step < nsteps - 1).astype(F32)
        causal = _causal_mask()
        for j in range(E_A // HEAD):
            cols = slice(j * HEAD, (j + 1) * HEAD)
            w0, w1, w2 = cw_ref[0:1, cols], cw_ref[1:2, cols], cw_ref[2:3, cols]
            lg, lb = lg_ref[:, cols], lb_ref[:, cols]
            wmf = jnp.where(causal, sw_ref[j], 0.0)
            wm, wmt = wmf.astype(BF16), wmf.T.astype(BF16)
            bias = sb_ref[j]

            def split(s, rows, cols=cols):
                return pj_ref[rows, s * E_A + cols.start:s * E_A + cols.stop].astype(F32)

            def put(s, rows, val, cols=cols):
                dp_ref[rows, s * E_A + cols.start:s * E_A + cols.stop] = val.astype(BF16)

            ps = [split(2, slice(n * CHUNK, (n + 1) * CHUNK)) * split(0, slice(n * CHUNK, (n + 1) * CHUNK)) for n in range(NCH)]
            next_head = (hdy_ref[:, cols].astype(F32) * hb_ref[:, cols].astype(F32) * _silu(hz_ref[:, cols].astype(F32))
                         * live_after)
            acc_w = [jnp.zeros((1, HEAD), F32) for _ in range(3)]
            for n in reversed(range(NCH)):
                rows = slice(n * CHUNK, (n + 1) * CHUNK)
                p = ps[n]
                tail = ps[n - 1][CHUNK - HALO:] if n > 0 else hc_ref[:, cols].astype(F32) * hh_ref[:, cols].astype(F32) * live_before
                ext = jnp.concatenate([tail, p], axis=0)
                p1, p2 = _shift_down(ext, 1), _shift_down(ext, 2)
                cv = w2 * p + w1 * p1 + w0 * p2
                a_b, a_z = split(1, rows), split(3, rows)
                sz, dsz = _silu_and_grad(a_z)
                dya = dy_ref[rows, cols].astype(F32)
                put(1, rows, dya * cv * sz)
                put(3, rows, dya * a_b * cv * dsz)
                gcv = dya * a_b * sz
                acc_w[0] += jnp.sum(gcv * p2, axis=0, keepdims=True)
                acc_w[1] += jnp.sum(gcv * p1, axis=0, keepdims=True)
                acc_w[2] += jnp.sum(gcv * p, axis=0, keepdims=True)
                gext = jnp.concatenate([gcv, next_head], axis=0)
                next_head = gcv[:HALO]
                dpv = w2 * gcv + w1 * _shift_up(gext, 1) + w0 * _shift_up(gext, 2)
                put(2, rows, dpv * split(0, rows))
                put(0, rows, dpv * split(2, rows))
            for k in range(3):
                dcw_ref[k:k + 1, cols] += acc_w[k]

            acc_lg, acc_lb = jnp.zeros((1, HEAD), F32), jnp.zeros((1, HEAD), F32)
            acc_sw, acc_sb = jnp.zeros((CHUNK, CHUNK), F32), jnp.zeros((CHUNK, 1), F32)
            for n in range(NCH):
                rows = slice(n * CHUNK, (n + 1) * CHUNK)
                u, z = split(4, rows), split(6, rows)
                vhat, rstd, vn = _layer_norm_head(split(5, rows), lg, lb)
                vn16 = vn.astype(BF16)
                mixed = jnp.dot(wm, vn16, preferred_element_type=F32) + bias
                sz, dsz = _silu_and_grad(z)
                dyb = dy_ref[rows, E_A + cols.start:E_A + cols.stop].astype(F32)
                put(4, rows, dyb * mixed * sz)
                put(6, rows, dyb * u * mixed * dsz)
                dmix = dyb * u * sz
                dmix16 = dmix.astype(BF16)
                acc_sb += jnp.sum(dmix, axis=1, keepdims=True)
                acc_sw += lax.dot_general(dmix16, vn16, (((1,), (1,)), ((), ())), preferred_element_type=F32)
                dvn = jnp.dot(wmt, dmix16, preferred_element_type=F32)
                acc_lg += jnp.sum(dvn * vhat, axis=0, keepdims=True)
                acc_lb += jnp.sum(dvn, axis=0, keepdims=True)
                dvh = dvn * lg
                put(5, rows, rstd * (dvh - jnp.mean(dvh, axis=-1, keepdims=True)
                                     - vhat * jnp.mean(dvh * vhat, axis=-1, keepdims=True)))
            dlg_ref[:, cols] += acc_lg
            dlb_ref[:, cols] += acc_lb
            dsw_ref[j] += jnp.where(causal, acc_sw, 0.0)
            dsb_ref[j] += acc_sb

    const3 = lambda i: (wl, 0, 0)
    const4 = lambda i: (wl, 0, 0, 0)
    fixed2 = lambda i: (0, 0)
    fixed3 = lambda i: (0, 0, 0)
    return pl.pallas_call(
        body, name=name, grid=(nsteps,),
        out_shape=[jax.ShapeDtypeStruct((S, 7 * E_A), BF16), jax.ShapeDtypeStruct((3, E_A), F32),
                   jax.ShapeDtypeStruct((1, E_A), F32), jax.ShapeDtypeStruct((1, E_A), F32),
                   jax.ShapeDtypeStruct((NDEV, CHUNK, CHUNK), F32), jax.ShapeDtypeStruct((NDEV, CHUNK, 1), F32)],
        in_specs=[pl.BlockSpec((TS, 7 * E_A), lambda i: (i, 0)), _halo_before(E_A, 0), _halo_before(E_A, 2),
                  _halo_after(E_A, 1), _halo_after(E_A, 3),
                  pl.BlockSpec((TS, 2 * E_A), lambda i: (i, 0)), _halo_after(E_A, 0),
                  pl.BlockSpec((None, 3, E_A), const3), pl.BlockSpec((None, 1, E_A), const3),
                  pl.BlockSpec((None, 1, E_A), const3), pl.BlockSpec((None, NDEV, CHUNK, CHUNK), const4),
                  pl.BlockSpec((None, NDEV, CHUNK, 1), const4)],
        out_specs=[pl.BlockSpec((TS, 7 * E_A), lambda i: (i, 0)), pl.BlockSpec((3, E_A), fixed2),
                   pl.BlockSpec((1, E_A), fixed2), pl.BlockSpec((1, E_A), fixed2),
                   pl.BlockSpec((NDEV, CHUNK, CHUNK), fixed3), pl.BlockSpec((NDEV, CHUNK, 1), fixed3)],
        compiler_params=_params(56, ("arbitrary",)),
    )(proj, proj, proj, proj, proj, d_ycat, d_ycat, convw, ln_g3, ln_b3, sgu_w, sgu_bcol)


def _window_count(step, n, win, ext_before):
    rows = CHUNK if ext_before else CHUNK + HALO
    t = step * TS + n * CHUNK + lax.broadcasted_iota(jnp.int32, (rows, 1), 0)
    return jnp.minimum(t + 1, win).astype(F32)


def _pool_weight(wp_ref, g):
    return jnp.concatenate([wp_ref[d, g] for d in range(NDEV)], axis=0)


def _pooled_chunk(p, tail, win, count):
    sums = jnp.concatenate([tail, p], axis=0)
    shift = 1
    while shift < win:
        sums = sums + pltpu.roll(sums, shift, 0)
        shift *= 2
    return sums[HALO:] / count - p


def _pool_mix_fwd(proj, wpool, pscale4, wl, name):
    e_c = 4 * G_C

    def body(pj_ref, hp_ref, wp_ref, ps_ref, y_ref, pooled_scr, yraw_scr):
        step = pl.program_id(0)
        live = (step > 0).astype(F32)
        for g, win in enumerate(POOL_WINDOWS):
            for q in range(G_C // LANE):
                cols = slice(g * G_C + q * LANE, g * G_C + (q + 1) * LANE)
                tail = hp_ref[:, cols].astype(F32) * live
                for n in range(NCH):
                    rows = slice(n * CHUNK, (n + 1) * CHUNK)
                    p = pj_ref[rows, cols].astype(F32)
                    pooled_scr[rows, q * LANE:(q + 1) * LANE] = _pooled_chunk(
                        p, tail, win, _window_count(step, n, win, True)).astype(BF16)
                    tail = p[CHUNK - HALO:]
            yraw_scr[...] = jnp.dot(pooled_scr[...], _pool_weight(wp_ref, g), preferred_element_type=F32)
            for q in range(G_C // LANE):
                cols = slice(g * G_C + q * LANE, g * G_C + (q + 1) * LANE)
                for n in range(NCH):
                    rows = slice(n * CHUNK, (n + 1) * CHUNK)
                    z = pj_ref[rows, e_c + cols.start:e_c + cols.stop].astype(F32)
                    y_ref[rows, cols] = (yraw_scr[rows, q * LANE:(q + 1) * LANE] * ps_ref[:, cols] * _silu(z)).astype(BF16)

    return pl.pallas_call(
        body, name=name, grid=(S // TS,), out_shape=jax.ShapeDtypeStruct((S, e_c), BF16),
        in_specs=[pl.BlockSpec((TS, 2 * e_c), lambda i: (i, 0)), _halo_before(e_c, 0),
                  pl.BlockSpec((NDEV, 4, G_C // NDEV, G_C), lambda i: (0, 0, 0, 0)),
                  pl.BlockSpec((None, 1, e_c), lambda i: (wl, 0, 0))],
        out_specs=pl.BlockSpec((TS, e_c), lambda i: (i, 0)),
        scratch_shapes=[pltpu.VMEM((TS, G_C), BF16), pltpu.VMEM((TS, G_C), F32)],
        compiler_params=_params(48, ("arbitrary",)),
    )(proj, proj, wpool, pscale4)


def _pool_mix_bwd(proj, d_ycat, wpool, pscale4, wl, name):
    e_c = 4 * G_C
    nsteps = S // TS
    rb = G_C // NDEV

    def body(pj_ref, hp_ref, hz_ref, dy_ref, hdy_ref, wp_ref, ps_ref,
             dp_ref, dps_ref, dwp_ref, pooled_scr, yraw_scr, dyraw_scr, dpool_scr, acc_w):
        step = pl.program_id(0)

        @pl.when(step == 0)
        def _():
            dps_ref[...] = jnp.zeros_like(dps_ref)
            acc_w[...] = jnp.zeros_like(acc_w)

        live_before = (step > 0).astype(F32)
        live_after = (step < nsteps - 1).astype(F32)
        for g, win in enumerate(POOL_WINDOWS):
            weight = _pool_weight(wp_ref, g)
            for q in range(G_C // LANE):
                cols = slice(g * G_C + q * LANE, g * G_C + (q + 1) * LANE)
                tail = hp_ref[:, cols].astype(F32) * live_before
                for n in range(NCH):
                    rows = slice(n * CHUNK, (n + 1) * CHUNK)
                    p = pj_ref[rows, cols].astype(F32)
                    pooled_scr[rows, q * LANE:(q + 1) * LANE] = _pooled_chunk(
                        p, tail, win, _window_count(step, n, win, True)).astype(BF16)
                    tail = p[CHUNK - HALO:]
            yraw_scr[...] = jnp.dot(pooled_scr[...], weight, preferred_element_type=F32)
            for q in range(G_C // LANE):
                cols = slice(g * G_C + q * LANE, g * G_C + (q + 1) * LANE)
                local = slice(q * LANE, (q + 1) * LANE)
                scale = ps_ref[:, cols]
                acc_ps = jnp.zeros((1, LANE), F32)
                for n in range(NCH):
                    rows = slice(n * CHUNK, (n + 1) * CHUNK)
                    sz, dsz = _silu_and_grad(pj_ref[rows, e_c + cols.start:e_c + cols.stop].astype(F32))
                    dyv = dy_ref[rows, cols].astype(F32)
                    yraw = yraw_scr[rows, local]
                    dyraw_scr[rows, local] = (dyv * scale * sz).astype(BF16)
                    acc_ps += jnp.sum(dyv * yraw * sz, axis=0, keepdims=True)
                    dp_ref[rows, e_c + cols.start:e_c + cols.stop] = (dyv * yraw * scale * dsz).astype(BF16)
                dps_ref[:, cols] += acc_ps
                dyraw_scr[TS:, local] = (hdy_ref[:, cols].astype(F32) * scale * _silu(hz_ref[:, cols].astype(F32))
                                         * live_after).astype(BF16)
            dpool_scr[...] = lax.dot_general(dyraw_scr[...], weight, (((1,), (1,)), ((), ())), preferred_element_type=F32)
            acc_w[g] += lax.dot_general(pooled_scr[...], dyraw_scr[:TS, :], (((0,), (0,)), ((), ())),
                                        preferred_element_type=F32)
            for q in range(G_C // LANE):
                cols = slice(g * G_C + q * LANE, g * G_C + (q + 1) * LANE)
                local = slice(q * LANE, (q + 1) * LANE)
                for n in range(NCH):
                    rows = slice(n * CHUNK, (n + 1) * CHUNK)
                    ext = dpool_scr[n * CHUNK:(n + 1) * CHUNK + HALO, local]
                    sums = ext / _window_count(step, n, win, False)
                    shift = 1
                    while shift < win:
                        sums = sums + pltpu.roll(sums, CHUNK + HALO - shift, 0)
                        shift *= 2
                    dp_ref[rows, cols] = (sums[:CHUNK] - ext[:CHUNK]).astype(BF16)

        @pl.when(step == nsteps - 1)
        def _():
            for g in range(4):
                for d in range(NDEV):
                    dwp_ref[d % 2, d // 2, g] = acc_w[g, d * rb:(d + 1) * rb, :].astype(BF16)

    in_specs = [pl.BlockSpec((TS, 2 * e_c), lambda i: (i, 0)), _halo_before(e_c, 0), _halo_after(e_c, 1),
                pl.BlockSpec((TS, e_c), lambda i: (i, 0)), _halo_after(e_c, 0),
                pl.BlockSpec((NDEV, 4, rb, G_C), lambda i: (0, 0, 0, 0)),
                pl.BlockSpec((None, 1, e_c), lambda i: (wl, 0, 0))]
    args = [proj, proj, proj, d_ycat, d_ycat, wpool, pscale4]
    return pl.pallas_call(
        body, name=name, grid=(nsteps,),
        out_shape=[jax.ShapeDtypeStruct((S, 2 * e_c), BF16), jax.ShapeDtypeStruct((1, e_c), F32),
                   jax.ShapeDtypeStruct((2, NDEV // 2) + wpool.shape[1:], BF16)],
        in_specs=in_specs,
        out_specs=[pl.BlockSpec((TS, 2 * e_c), lambda i: (i, 0)), pl.BlockSpec((1, e_c), lambda i: (0, 0)),
                   pl.BlockSpec((2, NDEV // 2, 4, rb, G_C), lambda i: (0, 0, 0, 0, 0))],
        scratch_shapes=[pltpu.VMEM((TS, G_C), BF16), pltpu.VMEM((TS, G_C), F32), pltpu.VMEM((TS + HALO, G_C), BF16),
                        pltpu.VMEM((TS + HALO, G_C), F32), pltpu.VMEM((4, G_C, G_C), F32)],
        compiler_params=_params(56, ("arbitrary",)),
    )(*args)


def _adamw(w, g, m, v):
    m = ADAM_B1 * m + (1.0 - ADAM_B1) * g
    v = ADAM_B2 * v + (1.0 - ADAM_B2) * jnp.square(g)
    m_hat = m / (1.0 - ADAM_B1 ** ADAM_STEP)
    v_hat = v / (1.0 - ADAM_B2 ** ADAM_STEP)
    delta = -ADAM_LR * (m_hat / (jnp.sqrt(v_hat) + ADAM_EPS) + ADAM_WD * w)
    return delta, m, v


def _adam_sharded(w, m, v, chip_parts, landed, my_chip, name):
    nl, nr, ncol = w.shape
    tr = 128
    steps = nr // tr

    def body(chip_ref, w_ref, m_ref, v_ref, *rest):
        parts, zones = rest[:nl], rest[nl:2 * nl]
        g_ref, d_ref, nm_ref, nv_ref = rest[2 * nl:]
        layer = pl.program_id(0)
        g = jnp.zeros((tr, ncol), F32)
        for l in range(nl):
            gl = parts[l][...].astype(F32)
            for q in range(3):
                gl = gl + zones[l][q].astype(F32)
            g = jnp.where(layer == l, gl, g)
        g_ref[...] = g
        d_ref[...], nm_ref[...], nv_ref[...] = _adamw(w_ref[...], g, m_ref[...], v_ref[...])

    def rows_of(l):
        return lambda layer, i, chip_ref: jnp.where(layer == l, i, jnp.where(layer < l, 0, steps - 1))

    spec = pl.BlockSpec((None, tr, ncol), lambda layer, i, chip_ref: (layer, i, 0))
    part_specs = [pl.BlockSpec((None, tr, ncol), lambda layer, i, chip_ref, l=l: (chip_ref[0], rows_of(l)(layer, i, chip_ref), 0))
                  for l in range(nl)]
    zone_specs = [pl.BlockSpec((3, tr, ncol), lambda layer, i, chip_ref, l=l: (0, rows_of(l)(layer, i, chip_ref), 0))
                  for l in range(nl)]
    grid_spec = pltpu.PrefetchScalarGridSpec(
        num_scalar_prefetch=1, grid=(nl, steps), in_specs=[spec, spec, spec] + part_specs + zone_specs, out_specs=[spec] * 4)
    return pl.pallas_call(
        body, name=name, grid_spec=grid_spec, out_shape=[jax.ShapeDtypeStruct(w.shape, F32)] * 4,
        compiler_params=_params(40, ("arbitrary", "arbitrary")),
    )(my_chip, w, m, v, *chip_parts, *landed)


def _adam_small(w, g, m, v, name):
    def body(w_ref, g_ref, m_ref, v_ref, d_ref, nm_ref, nv_ref):
        d_ref[...], nm_ref[...], nv_ref[...] = _adamw(w_ref[...], g_ref[...], m_ref[...], v_ref[...])

    return pl.pallas_call(body, name=name, out_shape=[jax.ShapeDtypeStruct(w.shape, F32)] * 3,
                          in_specs=[VMEM_FULL] * 4, out_specs=[VMEM_FULL] * 3, compiler_params=_params(32))(w, g, m, v)


def _sum_devices(gathered, name):
    _, nr, ncol = gathered.shape

    def body(g_ref, o_ref):
        acc = g_ref[0]
        for s in range(1, NDEV):
            acc = acc + g_ref[s]
        o_ref[...] = acc

    return pl.pallas_call(body, name=name, grid=(1,), out_shape=jax.ShapeDtypeStruct((nr, ncol), F32),
                          in_specs=[pl.BlockSpec((NDEV, nr, ncol), lambda i: (0, 0, 0))],
                          out_specs=pl.BlockSpec((nr, ncol), lambda i: (0, 0)),
                          compiler_params=_params(48, ("arbitrary",)))(gathered)


def _ada_weight_adam(cact_t, dmod_mine, w, m, v):
    def body(ct_ref, dm_ref, w_ref, m_ref, v_ref, g_ref, d_ref, nm_ref, nv_ref):
        ct, dm = ct_ref[...], dm_ref[...]
        g = ct[:, 0:1] * dm[0:1, :]
        for e in range(1, NDEV):
            g = g + ct[:, e:e + 1] * dm[e:e + 1, :]
        g_ref[...] = g
        d_ref[...], nm_ref[...], nv_ref[...] = _adamw(w_ref[...], g, m_ref[...], v_ref[...])

    spec = pl.BlockSpec((None, D, ADA_NC), lambda l: (l, 0, 0))
    return pl.pallas_call(
        body, name="ada_weight_adam", grid=(DEPTH,), out_shape=[jax.ShapeDtypeStruct(w.shape, F32)] * 4,
        in_specs=[pl.BlockSpec((D, NDEV), lambda l: (0, 0)), pl.BlockSpec((None, NDEV, ADA_NC), lambda l: (l, 0, 0)),
                  spec, spec, spec],
        out_specs=[spec] * 4, compiler_params=_params(40, ("arbitrary",)),
    )(cact_t, dmod_mine, w, m, v)


def _pad_rows(a, rows):
    a = a.reshape(-1, D)
    return jnp.pad(a, ((0, rows - a.shape[0]), (0, 0)))


def kernel(x, c, norm_g, ada_w, ada_b, ab_w_in, ab_conv_w, ab_ln_g, ab_ln_b, ab_sgu_w, ab_sgu_b, ab_w_out, c_w_in, c_pool_w, c_pool_scale, c_w_out, final_g, loss_target, m_norm_g, m_ada_w, m_ada_b, m_ab_w_in, m_ab_conv_w, m_ab_ln_g, m_ab_ln_b, m_ab_sgu_w, m_ab_sgu_b, m_ab_w_out, m_c_w_in, m_c_pool_w, m_c_pool_scale, m_c_w_out, m_final_g, v_norm_g, v_ada_w, v_ada_b, v_ab_w_in, v_ab_conv_w, v_ab_ln_g, v_ab_ln_b, v_ab_sgu_w, v_ab_sgu_b, v_ab_w_out, v_c_w_in, v_c_pool_w, v_c_pool_scale, v_c_w_out, v_final_g):
    x_pos, y_pos, c_pos = _position()
    me = _index((x_pos, y_pos, c_pos))
    core = c_pos.astype(jnp.int32).reshape(1)
    my_chip = (2 * x_pos + y_pos).astype(jnp.int32).reshape(1)
    x0 = x.reshape(S, D)
    target = loss_target.reshape(S, D)
    norm_g3 = norm_g.reshape(DEPTH, 1, D)
    ln_g3, ln_b3 = ab_ln_g.reshape(2, 1, E_A), ab_ln_b.reshape(2, 1, E_A)
    sgu_bcol = ab_sgu_b.reshape(2, NDEV, CHUNK, 1)
    rb = G_C // NDEV
    pool_w3, m_pool_w3, v_pool_w3 = (a.reshape(2, 4 * rb, G_C) for a in (c_pool_w, m_c_pool_w, v_c_pool_w))

    cact_all, mod = _ada_forward(c, ada_w, ada_b)
    flights, token = [], None
    for layer in range(DEPTH):
        wl = layer // 2
        if layer % 2 == 0:
            shards = [_cast_bf16(ab_w_in, wl, f"cast_w_in_{layer}"), _cast_bf16(ab_w_out, wl, f"cast_w_out_{layer}")]
        else:
            shards = [_cast_bf16(c_w_in, wl, f"cast_w_in_{layer}"), _cast_bf16(c_w_out, wl, f"cast_w_out_{layer}"),
                      _cast_bf16(pool_w3, wl, f"cast_pool_w_{layer}").reshape(4, rb, G_C)]
        send1, recv1, shards, zones, token = _gather_start(shards, token, f"gather_start_{layer}")
        flights.append((send1, recv1, shards, zones))
    convw_all, pscale_all = _gather([ab_conv_w, c_pool_scale], "gather_small_weights")
    convw = jnp.transpose(convw_all, (1, 2, 0, 3)).reshape(2, 3, E_A)
    pscale4 = jnp.transpose(pscale_all, (1, 0, 2)).reshape(2, 1, 4 * G_C)

    xs, hs, projs, ycats, outs, gathered_w = [x0], [], [], [], [], []
    after = token
    for layer in range(DEPTH):
        wl = layer // 2
        even = layer % 2 == 0
        send1, recv1, shards, zones = flights[layer]
        send2, recv2, zones = _gather_mid(zones, recv1, after, f"gather_mid_{layer}")
        wg = _gather_end(shards, zones, send1, recv1, send2, recv2, f"gather_end_{layer}")
        gathered_w.append(wg)
        h, proj = _norm_proj(xs[-1], mod, norm_g3, wg[0], layer, f"norm_proj_{layer}")
        if even:
            ycat = _even_mix_fwd(proj, convw, ln_g3, ln_b3, ab_sgu_w, sgu_bcol, wl, f"even_mix_fwd_{layer}")
        else:
            ycat = _pool_mix_fwd(proj, wg[2], pscale4, wl, f"pool_mix_fwd_{layer}")
        x_new, out = _out_proj(ycat, wg[1], xs[-1], mod, layer, f"out_proj_{layer}")
        after = x_new
        xs.append(x_new)
        hs.append(h)
        projs.append(proj)
        ycats.append(ycat)
        outs.append(out)

    dx, loss_part, d_final_g = _final_loss(xs[DEPTH], target, final_g.reshape(1, D))
    loss = lax.psum(loss_part[0, 0], ("x", "y", "c"))

    d_mod, d_norm_g = [None] * DEPTH, [None] * DEPTH
    small, scatters = {}, {}
    for layer in reversed(range(DEPTH)):
        wl = layer // 2
        even = layer % 2 == 0
        wg = gathered_w[layer]
        d_out, d_gate = _dout(dx, outs[layer], mod, layer, f"dout_{layer}")
        d_ycat = _dycat(d_out, wg[1], f"dycat_{layer}")
        parts = [None, _weight_grad(ycats[layer], d_out, f"grad_w_out_{layer}")]
        if even:
            d_proj, d_cw, d_lg, d_lb, d_sw, d_sb = _even_mix_bwd(
                projs[layer], d_ycat, convw, ln_g3, ln_b3, ab_sgu_w, sgu_bcol, wl, f"even_mix_bwd_{layer}")
            small[layer] = (d_cw, d_lg, d_lb, d_sw, d_sb)
        else:
            d_proj, d_ps, d_pool = _pool_mix_bwd(projs[layer], d_ycat, wg[2], pscale4, wl, f"pool_mix_bwd_{layer}")
            small[layer] = (d_ps,)
            parts.append(d_pool)
        d_h = _dh(d_proj, wg[0], f"dh_{layer}")
        parts[0] = _weight_grad(hs[layer], d_proj, f"grad_w_in_{layer}")
        dx, d_shift, d_scale, d_norm_g[layer] = _norm_bwd(d_h, xs[layer], dx, mod, norm_g3, layer, f"norm_bwd_{layer}")
        d_mod[layer] = jnp.concatenate([d_shift, d_scale, d_gate], axis=0)
        from_sibling = _pair_exchange(parts, f"pair_exchange_{layer}")
        chip_parts = [_pair_sum(p, q, core, f"pair_sum_{layer}_{j}") for j, (p, q) in enumerate(zip(parts, from_sibling))]
        scatters[layer] = _scatter_start(chip_parts, f"scatter_start_{layer}")
    grad_x = dx.reshape(x.shape)

    landed = {}
    for layer in range(DEPTH):
        send_sems, recv_sems, chip_parts, zones = scatters[layer]
        landed[layer] = _scatter_end(chip_parts, zones, send_sems, recv_sems, dx, f"scatter_end_{layer}")

    def flat(a):
        return a.reshape(a.shape[0], -1, a.shape[-1])

    res = {}
    for k, j, layers, (w, m, v) in [
            ("ab_w_in", 0, (0, 2), (ab_w_in, m_ab_w_in, v_ab_w_in)), ("ab_w_out", 1, (0, 2), (ab_w_out, m_ab_w_out, v_ab_w_out)),
            ("c_w_in", 0, (1, 3), (c_w_in, m_c_w_in, v_c_w_in)), ("c_w_out", 1, (1, 3), (c_w_out, m_c_w_out, v_c_w_out)),
            ("c_pool_w", 2, (1, 3), (pool_w3, m_pool_w3, v_pool_w3))]:
        outs4 = _adam_sharded(w, m, v, [flat(landed[l][0][j]) for l in layers], [flat(landed[l][1][j]) for l in layers],
                              my_chip, "adam_" + k)
        res[k] = [o.reshape(c_pool_w.shape) if k == "c_pool_w" else o for o in outs4]

    sections = [("norm_g", jnp.concatenate(d_norm_g, axis=0), 8),
                ("d_mod", jnp.concatenate(d_mod, axis=0), 16),
                ("ab_ln_g", jnp.concatenate([small[0][1], small[2][1]], axis=0), 8),
                ("ab_ln_b", jnp.concatenate([small[0][2], small[2][2]], axis=0), 8),
                ("ab_sgu_b", jnp.stack([small[0][4], small[2][4]]), 8),
                ("final_g", d_final_g, 8),
                ("ab_conv_w", jnp.stack([small[0][0], small[2][0]]), 8),
                ("c_pool_scale", jnp.concatenate([small[1][0], small[3][0]], axis=0), 8),
                ("ab_sgu_w", jnp.stack([small[0][3], small[2][3]]), 256)]
    offsets, at = {}, 0
    for name, _, rows in sections:
        offsets[name] = (at, rows)
        at += rows
    packed = jnp.concatenate([_pad_rows(a, rows) for _, a, rows in sections], axis=0)
    gathered = _gather([packed], "gather_small_grads")[0]
    summed = _sum_devices(gathered, "sum_small_grads")

    def section(name, nrows, src=summed):
        start = offsets[name][0]
        return src[..., start:start + nrows, :]

    grads = {
        "norm_g": section("norm_g", DEPTH),
        "ada_b": section("d_mod", 3 * DEPTH).reshape(DEPTH, 3 * D),
        "ab_ln_g": section("ab_ln_g", 2), "ab_ln_b": section("ab_ln_b", 2),
        "ab_sgu_b": section("ab_sgu_b", 2).reshape(ab_sgu_b.shape),
        "final_g": section("final_g", 1),
        "ab_sgu_w": section("ab_sgu_w", 256).reshape(ab_sgu_w.shape),
        "ab_conv_w": lax.dynamic_slice_in_dim(section("ab_conv_w", 6).reshape(2, 3, E_A), me * HEAD, HEAD, axis=2),
        "c_pool_scale": lax.dynamic_slice_in_dim(section("c_pool_scale", 4).reshape(2, 4 * G_C), me * 256, 256, axis=1),
    }
    small_w = {"norm_g": (norm_g, m_norm_g, v_norm_g), "ada_b": (ada_b, m_ada_b, v_ada_b),
               "ab_ln_g": (ab_ln_g, m_ab_ln_g, v_ab_ln_g), "ab_ln_b": (ab_ln_b, m_ab_ln_b, v_ab_ln_b),
               "ab_sgu_b": (ab_sgu_b, m_ab_sgu_b, v_ab_sgu_b),
               "final_g": (final_g.reshape(1, D), m_final_g.reshape(1, D), v_final_g.reshape(1, D)),
               "ab_sgu_w": (ab_sgu_w, m_ab_sgu_w, v_ab_sgu_w), "ab_conv_w": (ab_conv_w, m_ab_conv_w, v_ab_conv_w),
               "c_pool_scale": (c_pool_scale, m_c_pool_scale, v_c_pool_scale)}
    for k, (w, m, v) in small_w.items():
        res[k] = [grads[k]] + list(_adam_small(w, grads[k], m, v, "adam_" + k))
    res["final_g"] = [a.reshape(D) for a in res["final_g"]]

    dmod_all = section("d_mod", 3 * DEPTH, gathered).reshape(NDEV, DEPTH, 3 * D)
    dmod_mine = jnp.transpose(lax.dynamic_slice_in_dim(dmod_all, me * ADA_NC, ADA_NC, axis=2), (1, 0, 2))
    res["ada_w"] = _ada_weight_adam(jnp.transpose(cact_all.reshape(NDEV, D)), dmod_mine, ada_w, m_ada_w, v_ada_w)

    order = ["norm_g", "ada_w", "ada_b", "ab_w_in", "ab_conv_w", "ab_ln_g", "ab_ln_b", "ab_sgu_w", "ab_sgu_b",
             "ab_w_out", "c_w_in", "c_pool_w", "c_pool_scale", "c_w_out", "final_g"]
    return (loss, grad_x, *[res[k][0] for k in order], *[res[k][1] for k in order],
            *[res[k][2] for k in order], *[res[k][3] for k in order])
```

```python
import jax
import jax.numpy as jnp
from jax import lax
from jax.experimental import pallas as pl
from jax.experimental.pallas import tpu as pltpu

F32, BF16 = jnp.float32, jnp.bfloat16
S, D = 2048, 1024
NDEV = 8
DEPTH = 4
EPS = 1e-6
E_A = 1024
HEAD = 128
CHUNK = 128
POOL_WINDOWS = (2, 4, 8, 16)
G_C = 512
HALO = 16
ADA_NC = 384
MIB = 1024 * 1024
LANE = 128

ADAM_LR, ADAM_B1, ADAM_B2, ADAM_EPS, ADAM_WD, ADAM_STEP = 0.001, 0.9, 0.999, 1e-08, 0.01, 10

ANY = pl.BlockSpec(memory_space=pl.ANY)
VMEM_FULL = pl.BlockSpec(memory_space=pltpu.VMEM)
IN_HBM = pl.BlockSpec(memory_space=pltpu.HBM)
SEMAPHORES = pl.BlockSpec(memory_space=pltpu.SEMAPHORE)
IN_FLIGHT = pltpu.SideEffectType.DATAFLOW_SIDE_EFFECTING


def _params(vmem_mib, semantics=None):
    return pltpu.CompilerParams(dimension_semantics=semantics, vmem_limit_bytes=vmem_mib * MIB)


def _silu(z):
    return z * jax.nn.sigmoid(z)


def _silu_and_grad(z):
    sig = jax.nn.sigmoid(z)
    return z * sig, sig * (1.0 + z * (1.0 - sig))


def _position():
    return lax.axis_index("x"), lax.axis_index("y"), lax.axis_index("c")


def _index(pos):
    return 4 * pos[0] + 2 * pos[1] + pos[2]


def _peer(pos, k):
    flipped = tuple(1 - p if (k >> (2 - b)) & 1 else p for b, p in enumerate(pos))
    return flipped, _index(flipped)


def _remote(src, dst, send_sem, recv_sem, device):
    return pltpu.make_async_remote_copy(src_ref=src, dst_ref=dst, send_sem=send_sem, recv_sem=recv_sem,
                                        device_id=device, device_id_type=pl.DeviceIdType.MESH)


def _gather(arrays, name):
    n = len(arrays)
    out_shape = [jax.ShapeDtypeStruct((NDEV,) + a.shape, a.dtype) for a in arrays]

    def body(*refs):
        ins, outs = refs[:n], refs[n:2 * n]
        send_sems, recv_sems, own_sems = refs[2 * n:]
        x, y, c = _position()
        me = _index((x, y, c))
        sibling = (x, y, 1 - c)
        chips = [(1 - x, y), (x, 1 - y), (1 - x, 1 - y)]

        def block_copy(j, k, owner, to, src=None):
            rows = outs[j].at[_index(owner)]
            return _remote(rows if src is None else src, rows, send_sems.at[j, k], recv_sems.at[j, k], to)

        own, first, passed = [], [], []
        for j in range(n):
            own.append(pltpu.make_async_copy(ins[j], outs[j].at[me], own_sems.at[j]))
            first.append(block_copy(j, 0, (x, y, c), sibling, src=ins[j]))
            first += [block_copy(j, 1 + q, (x, y, c), (*chip, c), src=ins[j]) for q, chip in enumerate(chips)]
        for copy in own + first:
            copy.start()
        for q, chip in enumerate(chips):
            for j in range(n):
                block_copy(j, 1 + q, (*chip, c), (x, y, c)).wait_recv()
                forward = block_copy(j, 4 + q, (*chip, c), sibling)
                forward.start()
                passed.append(forward)
        for j in range(n):
            block_copy(j, 0, sibling, (x, y, c)).wait_recv()
            for q, chip in enumerate(chips):
                block_copy(j, 4 + q, (*chip, 1 - c), (x, y, c)).wait_recv()
        for copy in first + passed:
            copy.wait_send()
        for copy in own:
            copy.wait()

    return pl.pallas_call(
        body, name=name, out_shape=out_shape, in_specs=[ANY] * n, out_specs=[ANY] * n,
        scratch_shapes=[pltpu.SemaphoreType.DMA((n, NDEV - 1)), pltpu.SemaphoreType.DMA((n, NDEV - 1)),
                        pltpu.SemaphoreType.DMA((n,))],
    )(*arrays)


def _pair_exchange(parts, name):
    n = len(parts)

    def body(*refs):
        ins, outs = refs[:n], refs[n:2 * n]
        send_sems, recv_sems = refs[2 * n:]
        x, y, c = _position()
        copies = [_remote(ins[j].at[1 - c], outs[j], send_sems.at[j], recv_sems.at[j], (x, y, 1 - c)) for j in range(n)]
        for copy in copies:
            copy.start()
        for copy in copies:
            copy.wait()

    return pl.pallas_call(
        body, name=name, out_shape=[jax.ShapeDtypeStruct(p.shape[1:], p.dtype) for p in parts],
        in_specs=[ANY] * n, out_specs=[ANY] * n,
        scratch_shapes=[pltpu.SemaphoreType.DMA((n,)), pltpu.SemaphoreType.DMA((n,))],
    )(*parts)


def _pair_sum(part, from_sibling, core, name):
    ncol = part.shape[-1]
    p3 = part.reshape(2, -1, ncol)
    q2 = from_sibling.reshape(-1, ncol)
    nrows = q2.shape[0]
    tr = 512

    def body(core_ref, p_ref, q_ref, o_ref):
        o_ref[...] = (p_ref[...].astype(F32) + q_ref[...].astype(F32)).astype(BF16)

    grid_spec = pltpu.PrefetchScalarGridSpec(
        num_scalar_prefetch=1, grid=(nrows // tr,),
        in_specs=[pl.BlockSpec((None, tr, ncol), lambda i, core_ref: (core_ref[0], i, 0)),
                  pl.BlockSpec((tr, ncol), lambda i, core_ref: (i, 0))],
        out_specs=pl.BlockSpec((tr, ncol), lambda i, core_ref: (i, 0)))
    out = pl.pallas_call(body, name=name, grid_spec=grid_spec, out_shape=jax.ShapeDtypeStruct(q2.shape, BF16),
                         compiler_params=_params(32, ("arbitrary",)))(core, p3, q2)
    return out.reshape(from_sibling.shape)


def _in_hbm(a):
    return pltpu.with_memory_space_constraint(a, pltpu.HBM)


def _chips(x, y):
    return [(1 - x, y), (x, 1 - y), (1 - x, 1 - y)]


def _gather_start(arrays, after, name):
    n = len(arrays)
    lands = [_in_hbm(lax.empty((NDEV,) + a.shape, a.dtype)) for a in arrays]
    extra = list(after)

    def body(*refs):
        ins, zones = refs[:n], refs[n:2 * n]
        send_sems, recv_sems = refs[2 * n + len(extra):2 * n + len(extra) + 2]
        own_sems = refs[-1]
        x, y, c = _position()
        me = _index((x, y, c))
        own = [pltpu.make_async_copy(ins[j], zones[j].at[me], own_sems.at[j]) for j in range(n)]
        for copy in own:
            copy.start()
        for j in range(n):
            _remote(ins[j], zones[j].at[me], send_sems.at[4 * j], recv_sems.at[4 * j], (x, y, 1 - c)).start()
            for q, chip in enumerate(_chips(x, y)):
                _remote(ins[j], zones[j].at[me], send_sems.at[4 * j + 1 + q], recv_sems.at[4 * j + 1 + q], (*chip, c)).start()
        for copy in own:
            copy.wait()

    outs = pl.pallas_call(
        body, name=name,
        out_shape=(pltpu.SemaphoreType.DMA((4 * n,)), pltpu.SemaphoreType.DMA((4 * n,)),
                   *[pltpu.HBM(a.shape, a.dtype) for a in arrays], *[pltpu.HBM(z.shape, z.dtype) for z in lands]),
        in_specs=[IN_HBM] * (2 * n) + [ANY] * len(extra),
        out_specs=(SEMAPHORES, SEMAPHORES, *[IN_HBM] * (2 * n)),
        input_output_aliases={j: 2 + j for j in range(2 * n)},
        scratch_shapes=[pltpu.SemaphoreType.DMA((n,))],
        compiler_params=pltpu.CompilerParams(has_side_effects=IN_FLIGHT),
    )(*[_in_hbm(a) for a in arrays], *lands, *extra)
    return outs[0], outs[1], list(outs[2:2 + n]), list(outs[2 + n:2 + 2 * n])


def _gather_mid(zones, recv_sems, after, name):
    n = len(zones)

    def body(*refs):
        zs, first_recv = refs[:n], refs[n]
        send_sems, recv_sems2 = refs[n + 2:n + 4]
        x, y, c = _position()
        for q, chip in enumerate(_chips(x, y)):
            for j in range(n):
                rows = zs[j].at[_index((*chip, c))]
                _remote(rows, rows, send_sems.at[3 * j + q], first_recv.at[4 * j + 1 + q], (x, y, c)).wait_recv()
                _remote(rows, rows, send_sems.at[3 * j + q], recv_sems2.at[3 * j + q], (x, y, 1 - c)).start()

    outs = pl.pallas_call(
        body, name=name,
        out_shape=(pltpu.SemaphoreType.DMA((3 * n,)), pltpu.SemaphoreType.DMA((3 * n,)),
                   *[pltpu.HBM(z.shape, z.dtype) for z in zones]),
        in_specs=[IN_HBM] * n + [SEMAPHORES, ANY], out_specs=(SEMAPHORES, SEMAPHORES, *[IN_HBM] * n),
        input_output_aliases={j: 2 + j for j in range(n)},
        compiler_params=pltpu.CompilerParams(has_side_effects=IN_FLIGHT),
    )(*zones, recv_sems, after)
    return outs[0], outs[1], list(outs[2:])


def _gather_end(arrays, zones, send1, recv1, send2, recv2, name):
    n = len(zones)

    def body(*refs):
        ins, zs = refs[:n], refs[n:2 * n]
        s1, r1, s2, r2 = refs[2 * n:2 * n + 4]
        x, y, c = _position()
        me = (x, y, c)
        for j in range(n):
            rows = zs[j].at[_index((x, y, 1 - c))]
            _remote(rows, rows, s1.at[4 * j], r1.at[4 * j], me).wait_recv()
            for q, chip in enumerate(_chips(x, y)):
                rows = zs[j].at[_index((*chip, 1 - c))]
                _remote(rows, rows, s2.at[3 * j + q], r2.at[3 * j + q], me).wait_recv()
        for j in range(n):
            for k in range(4):
                _remote(ins[j], zs[j].at[0], s1.at[4 * j + k], r1.at[4 * j + k], me).wait_send()
            for q in range(3):
                _remote(zs[j].at[0], zs[j].at[0], s2.at[3 * j + q], r2.at[3 * j + q], me).wait_send()

    outs = pl.pallas_call(
        body, name=name, out_shape=tuple(pltpu.HBM(z.shape, z.dtype) for z in zones),
        in_specs=[IN_HBM] * (2 * n) + [SEMAPHORES] * 4, out_specs=tuple([IN_HBM] * n),
        input_output_aliases={n + j: j for j in range(n)},
        compiler_params=pltpu.CompilerParams(has_side_effects=IN_FLIGHT),
    )(*arrays, *zones, send1, recv1, send2, recv2)
    return list(outs)


def _scatter_start(parts, name):
    n = len(parts)
    lands = [_in_hbm(lax.empty((3,) + p.shape[1:], p.dtype)) for p in parts]

    def body(*refs):
        ins, zones = refs[:n], refs[n:2 * n]
        send_sems, recv_sems = refs[2 * n:2 * n + 2]
        x, y, c = _position()
        for j in range(n):
            for q, (px, py) in enumerate(_chips(x, y)):
                _remote(ins[j].at[2 * px + py], zones[j].at[q], send_sems.at[3 * j + q], recv_sems.at[3 * j + q],
                        (px, py, c)).start()

    outs = pl.pallas_call(
        body, name=name,
        out_shape=(pltpu.SemaphoreType.DMA((3 * n,)), pltpu.SemaphoreType.DMA((3 * n,)),
                   *[pltpu.HBM(p.shape, p.dtype) for p in parts], *[pltpu.HBM(z.shape, z.dtype) for z in lands]),
        in_specs=[IN_HBM] * (2 * n), out_specs=(SEMAPHORES, SEMAPHORES, *[IN_HBM] * (2 * n)),
        input_output_aliases={j: 2 + j for j in range(2 * n)},
        compiler_params=pltpu.CompilerParams(has_side_effects=IN_FLIGHT),
    )(*[_in_hbm(p) for p in parts], *lands)
    return outs[0], outs[1], list(outs[2:2 + n]), list(outs[2 + n:])


def _scatter_end(parts, zones, send_sems, recv_sems, after, name):
    n = len(parts)

    def body(*refs):
        ins, zs = refs[:n], refs[n:2 * n]
        s, r = refs[2 * n:2 * n + 2]
        me = _position()
        for j in range(n):
            for q in range(3):
                copy = _remote(ins[j].at[0], zs[j].at[q], s.at[3 * j + q], r.at[3 * j + q], me)
                copy.wait_send()
                copy.wait_recv()

    outs = pl.pallas_call(
        body, name=name,
        out_shape=(*[pltpu.HBM(p.shape, p.dtype) for p in parts], *[pltpu.HBM(z.shape, z.dtype) for z in zones]),
        in_specs=[IN_HBM] * (2 * n) + [SEMAPHORES, SEMAPHORES, ANY], out_specs=tuple([IN_HBM] * (2 * n)),
        input_output_aliases={j: j for j in range(2 * n)},
        compiler_params=pltpu.CompilerParams(has_side_effects=IN_FLIGHT),
    )(*parts, *zones, send_sems, recv_sems, after)
    return list(outs[:n]), list(outs[n:])


def _ada_forward(c, ada_w, ada_b):
    def body(c_ref, w_ref, b_ref, cact_ref, mod_ref, gbuf, modrow, send_sems, recv_sems):
        pos = _position()
        me = _index(pos)

        def to_all(ref, round_):
            copies = []
            for k in range(1, NDEV):
                peer, _ = _peer(pos, k)
                copy = pltpu.make_async_remote_copy(
                    src_ref=ref.at[me], dst_ref=ref.at[me], send_sem=send_sems.at[round_, k - 1],
                    recv_sem=recv_sems.at[round_, k - 1], device_id=peer, device_id_type=pl.DeviceIdType.MESH)
                copy.start()
                copies.append(copy)
            for copy in copies:
                copy.wait()

        cact_ref[me] = _silu(c_ref[...])
        to_all(cact_ref, 0)
        rows = lax.broadcasted_iota(jnp.int32, (NDEV, D), 0)
        cact = jnp.zeros((NDEV, D), F32)
        for e in range(NDEV):
            cact = jnp.where(rows == e, cact_ref[e], cact)
        cact = cact.astype(BF16)
        for l in range(DEPTH):
            gbuf[me, l] = jnp.dot(cact, w_ref[l].astype(BF16), preferred_element_type=F32)
        to_all(gbuf, 1)
        mine = lax.broadcasted_iota(jnp.int32, (NDEV, ADA_NC), 0) == me
        for l in range(DEPTH):
            for d in range(NDEV):
                modrow[:, d * ADA_NC:(d + 1) * ADA_NC] = jnp.sum(jnp.where(mine, gbuf[d, l], 0.0), axis=0, keepdims=True)
            full = modrow[...] + b_ref[l:l + 1, :]
            for w in range(3):
                mod_ref[l, w] = full[:, w * D:(w + 1) * D]

    return pl.pallas_call(
        body, name="ada_forward",
        out_shape=[jax.ShapeDtypeStruct((NDEV, 1, D), F32), jax.ShapeDtypeStruct((DEPTH, 3, 1, D), F32)],
        in_specs=[VMEM_FULL] * 3, out_specs=[VMEM_FULL] * 2,
        scratch_shapes=[pltpu.VMEM((NDEV, DEPTH, NDEV, ADA_NC), F32), pltpu.VMEM((1, 3 * D), F32),
                        pltpu.SemaphoreType.DMA((2, NDEV - 1)), pltpu.SemaphoreType.DMA((2, NDEV - 1))],
        compiler_params=_params(40),
    )(c, ada_w, ada_b)


def _cast_bf16(a, wl, name):
    _, rows, cols = a.shape
    tr = 256

    def body(a_ref, o_ref):
        o_ref[...] = a_ref[...].astype(BF16)

    return pl.pallas_call(body, name=name, grid=(rows // tr,), out_shape=jax.ShapeDtypeStruct((rows, cols), BF16),
                          in_specs=[pl.BlockSpec((None, tr, cols), lambda i: (wl, i, 0))],
                          out_specs=pl.BlockSpec((tr, cols), lambda i: (i, 0)),
                          compiler_params=_params(32, ("arbitrary",)))(a)


def _mod_spec(layer, which, ngrid):
    index = {1: lambda i: (layer, which, 0, 0), 2: lambda i, j: (layer, which, 0, 0)}[ngrid]
    return pl.BlockSpec((None, None, 1, D), index)


def _norm_proj(x, mod, norm_g3, wg, layer, name):
    nb = wg.shape[-1]
    tm = 512

    def body(x_ref, g_ref, shift_ref, scale_ref, w_ref, h_ref, p_ref):
        @pl.when(pl.program_id(1) == 0)
        def _():
            xv = x_ref[...]
            r = lax.rsqrt(jnp.mean(xv * xv, axis=-1, keepdims=True) + EPS)
            hn = xv * r * g_ref[...]
            h_ref[...] = (hn * (1.0 + scale_ref[...]) + shift_ref[...]).astype(BF16)

        p_ref[...] = jnp.dot(h_ref[...], w_ref[...], preferred_element_type=F32).astype(BF16)

    return pl.pallas_call(
        body, name=name, grid=(S // tm, NDEV),
        out_shape=[jax.ShapeDtypeStruct((S, D), BF16), jax.ShapeDtypeStruct((S, NDEV * nb), BF16)],
        in_specs=[pl.BlockSpec((tm, D), lambda i, d: (i, 0)),
                  pl.BlockSpec((None, 1, D), lambda i, d: (layer, 0, 0)),
                  _mod_spec(layer, 0, 2), _mod_spec(layer, 1, 2),
                  pl.BlockSpec((None, D, nb), lambda i, d: (d, 0, 0))],
        out_specs=[pl.BlockSpec((tm, D), lambda i, d: (i, 0)), pl.BlockSpec((tm, nb), lambda i, d: (i, d))],
        compiler_params=_params(48, ("arbitrary", "arbitrary")),
    )(x, norm_g3, mod, mod, wg)


def _out_proj(ycat, wg_out, x, mod, layer, name):
    tm = 512
    rb = wg_out.shape[1]

    def body(y_ref, w_ref, x_ref, gate_ref, xn_ref, o_ref):
        acc = jnp.zeros((tm, D), F32)
        for d in range(NDEV):
            acc = acc + jnp.dot(y_ref[:, d * rb:(d + 1) * rb], w_ref[d], preferred_element_type=F32)
        o_ref[...] = acc.astype(BF16)
        xn_ref[...] = x_ref[...] + gate_ref[...] * acc

    return pl.pallas_call(
        body, name=name, grid=(S // tm,),
        out_shape=[jax.ShapeDtypeStruct((S, D), F32), jax.ShapeDtypeStruct((S, D), BF16)],
        in_specs=[pl.BlockSpec((tm, NDEV * rb), lambda i: (i, 0)),
                  pl.BlockSpec((NDEV, rb, D), lambda i: (0, 0, 0)),
                  pl.BlockSpec((tm, D), lambda i: (i, 0)), _mod_spec(layer, 2, 1)],
        out_specs=[pl.BlockSpec((tm, D), lambda i: (i, 0))] * 2,
        compiler_params=_params(48, ("arbitrary",)),
    )(ycat, wg_out, x, mod)


def _final_loss(x, target, final_g2):
    tm = 256

    def body(x_ref, t_ref, g_ref, dx_ref, loss_ref, dg_ref):
        @pl.when(pl.program_id(0) == 0)
        def _():
            loss_ref[...] = jnp.zeros_like(loss_ref)
            dg_ref[...] = jnp.zeros_like(dg_ref)

        xv, g = x_ref[...], g_ref[...]
        r = lax.rsqrt(jnp.mean(xv * xv, axis=-1, keepdims=True) + EPS)
        xn = xv * r
        err = xn * g - t_ref[...]
        loss_ref[...] += 0.5 * jnp.sum(jnp.mean(err * err, axis=-1, keepdims=True), axis=0, keepdims=True)
        dy = err * (1.0 / D)
        dg_ref[...] += jnp.sum(dy * xn, axis=0, keepdims=True)
        u = dy * g
        dx_ref[...] = r * (u - xn * jnp.mean(xn * u, axis=-1, keepdims=True))

    tile = pl.BlockSpec((tm, D), lambda i: (i, 0))
    row = pl.BlockSpec((1, D), lambda i: (0, 0))
    return pl.pallas_call(
        body, name="final_loss", grid=(S // tm,),
        out_shape=[jax.ShapeDtypeStruct((S, D), F32), jax.ShapeDtypeStruct((1, LANE), F32), jax.ShapeDtypeStruct((1, D), F32)],
        in_specs=[tile, tile, row], out_specs=[tile, pl.BlockSpec((1, LANE), lambda i: (0, 0)), row],
        compiler_params=_params(32, ("arbitrary",)),
    )(x, target, final_g2)


def _dout(dx, out, mod, layer, name):
    tm = 256

    def body(dx_ref, o_ref, gate_ref, do_ref, dgate_ref):
        @pl.when(pl.program_id(0) == 0)
        def _():
            dgate_ref[...] = jnp.zeros_like(dgate_ref)

        dxv = dx_ref[...]
        do_ref[...] = (gate_ref[...] * dxv).astype(BF16)
        dgate_ref[...] += jnp.sum(dxv * o_ref[...].astype(F32), axis=0, keepdims=True)

    tile = pl.BlockSpec((tm, D), lambda i: (i, 0))
    return pl.pallas_call(
        body, name=name, grid=(S // tm,),
        out_shape=[jax.ShapeDtypeStruct((S, D), BF16), jax.ShapeDtypeStruct((1, D), F32)],
        in_specs=[tile, tile, _mod_spec(layer, 2, 1)], out_specs=[tile, pl.BlockSpec((1, D), lambda i: (0, 0))],
        compiler_params=_params(32, ("arbitrary",)),
    )(dx, out, mod)


def _dycat(d_out, wg_out, name):
    tm = 512
    rb = wg_out.shape[1]

    def body(do_ref, w_ref, dy_ref):
        dov = do_ref[...]
        for d in range(NDEV):
            dy_ref[:, d * rb:(d + 1) * rb] = lax.dot_general(
                dov, w_ref[d], (((1,), (1,)), ((), ())), preferred_element_type=F32).astype(BF16)

    return pl.pallas_call(
        body, name=name, grid=(S // tm,), out_shape=jax.ShapeDtypeStruct((S, NDEV * rb), BF16),
        in_specs=[pl.BlockSpec((tm, D), lambda i: (i, 0)), pl.BlockSpec((NDEV, rb, D), lambda i: (0, 0, 0))],
        out_specs=pl.BlockSpec((tm, NDEV * rb), lambda i: (i, 0)),
        compiler_params=_params(48, ("arbitrary",)),
    )(d_out, wg_out)


def _weight_grad(a, b, name):
    a_blocks = a.shape[1] != D
    ka = a.shape[1] // NDEV if a_blocks else a.shape[1]
    nb = b.shape[1] if a_blocks else b.shape[1] // NDEV

    def body(a_ref, b_ref, o_ref):
        o_ref[...] = lax.dot_general(a_ref[...], b_ref[...], (((0,), (0,)), ((), ())),
                                     preferred_element_type=F32).astype(BF16)

    return pl.pallas_call(
        body, name=name, grid=(NDEV,), out_shape=jax.ShapeDtypeStruct((2, NDEV // 2, ka, nb), BF16),
        in_specs=[pl.BlockSpec((S, ka), (lambda d: (0, d)) if a_blocks else (lambda d: (0, 0))),
                  pl.BlockSpec((S, nb), (lambda d: (0, 0)) if a_blocks else (lambda d: (0, d)))],
        out_specs=pl.BlockSpec((None, None, ka, nb), lambda d: (d % 2, d // 2, 0, 0)),
        compiler_params=_params(56, ("arbitrary",)),
    )(a, b)


def _dh(d_proj, wg, carried, name):
    nb = wg.shape[-1]
    tm = 512

    def body(dp_ref, w_ref, carried_ref, dh_ref, carried_out):
        part = lax.dot_general(dp_ref[...], w_ref[...], (((1,), (1,)), ((), ())), preferred_element_type=F32)

        @pl.when(pl.program_id(1) == 0)
        def _():
            dh_ref[...] = part

        @pl.when(pl.program_id(1) != 0)
        def _():
            dh_ref[...] += part

    return pl.pallas_call(
        body, name=name, grid=(S // tm, NDEV),
        out_shape=[jax.ShapeDtypeStruct((S, D), F32), jax.ShapeDtypeStruct(carried.shape, carried.dtype)],
        in_specs=[pl.BlockSpec((tm, nb), lambda i, d: (i, d)), pl.BlockSpec((None, D, nb), lambda i, d: (d, 0, 0)), ANY],
        out_specs=[pl.BlockSpec((tm, D), lambda i, d: (i, 0)), ANY],
        input_output_aliases={2: 1},
        compiler_params=_params(48, ("arbitrary", "arbitrary")),
    )(d_proj, wg, carried)


def _norm_bwd(d_h, x, dx, mod, norm_g3, layer, name):
    tm = 256

    def body(dh_ref, x_ref, dx_ref, g_ref, scale_ref, dxi_ref, dshift_ref, dscale_ref, dg_ref):
        @pl.when(pl.program_id(0) == 0)
        def _():
            dshift_ref[...] = jnp.zeros_like(dshift_ref)
            dscale_ref[...] = jnp.zeros_like(dscale_ref)
            dg_ref[...] = jnp.zeros_like(dg_ref)

        xv, dhv, g = x_ref[...], dh_ref[...], g_ref[...]
        r = lax.rsqrt(jnp.mean(xv * xv, axis=-1, keepdims=True) + EPS)
        xn = xv * r
        dshift_ref[...] += jnp.sum(dhv, axis=0, keepdims=True)
        dscale_ref[...] += jnp.sum(dhv * (xn * g), axis=0, keepdims=True)
        dhn = dhv * (1.0 + scale_ref[...])
        dg_ref[...] += jnp.sum(dhn * xn, axis=0, keepdims=True)
        u = dhn * g
        dxi_ref[...] = dx_ref[...] + r * (u - xn * jnp.mean(xn * u, axis=-1, keepdims=True))

    tile = pl.BlockSpec((tm, D), lambda i: (i, 0))
    row = pl.BlockSpec((1, D), lambda i: (0, 0))
    return pl.pallas_call(
        body, name=name, grid=(S // tm,),
        out_shape=[jax.ShapeDtypeStruct((S, D), F32)] + [jax.ShapeDtypeStruct((1, D), F32)] * 3,
        in_specs=[tile, tile, tile, pl.BlockSpec((None, 1, D), lambda i: (layer, 0, 0)), _mod_spec(layer, 1, 1)],
        out_specs=[tile, row, row, row], compiler_params=_params(40, ("arbitrary",)),
    )(d_h, x, dx, norm_g3, mod)


TS = 256
NCH = TS // CHUNK
HALO_BLOCKS = TS // HALO


def _halo_before(width, col_block):
    return pl.BlockSpec((HALO, width), lambda i: (jnp.maximum(i * HALO_BLOCKS - 1, 0), col_block))


def _halo_after(width, col_block):
    return pl.BlockSpec((HALO, width), lambda i: (jnp.minimum((i + 1) * HALO_BLOCKS, S // HALO - 1), col_block))


def _shift_down(ext, k):
    return pltpu.roll(ext, k, 0)[HALO:]


def _shift_up(ext, k):
    return pltpu.roll(ext, ext.shape[0] - k, 0)[:ext.shape[0] - HALO]


def _layer_norm_head(v, lg, lb):
    mu = jnp.mean(v, axis=-1, keepdims=True)
    vc = v - mu
    rstd = lax.rsqrt(jnp.mean(vc * vc, axis=-1, keepdims=True) + EPS)
    vhat = vc * rstd
    return vhat, rstd, vhat * lg + lb


def _causal_mask():
    return lax.broadcasted_iota(jnp.int32, (CHUNK, CHUNK), 0) >= lax.broadcasted_iota(jnp.int32, (CHUNK, CHUNK), 1)


def _even_mix_fwd(proj, convw, ln_g3, ln_b3, sgu_w, sgu_bcol, wl, name):
    def body(pj_ref, hh_ref, hc_ref, cw_ref, lg_ref, lb_ref, sw_ref, sb_ref, y_ref):
        live = (pl.program_id(0) > 0).astype(F32)
        causal = _causal_mask()
        for j in range(E_A // HEAD):
            cols = slice(j * HEAD, (j + 1) * HEAD)
            w0, w1, w2 = cw_ref[0:1, cols], cw_ref[1:2, cols], cw_ref[2:3, cols]
            lg, lb = lg_ref[:, cols], lb_ref[:, cols]
            wm = jnp.where(causal, sw_ref[j], 0.0).astype(BF16)
            bias = sb_ref[j]

            def split(s, rows, cols=cols):
                return pj_ref[rows, s * E_A + cols.start:s * E_A + cols.stop].astype(F32)

            prev_tail = hc_ref[:, cols].astype(F32) * hh_ref[:, cols].astype(F32) * live
            for n in range(NCH):
                rows = slice(n * CHUNK, (n + 1) * CHUNK)
                p = split(2, rows) * split(0, rows)
                ext = jnp.concatenate([prev_tail, p], axis=0)
                prev_tail = p[CHUNK - HALO:]
                cv = w2 * p + w1 * _shift_down(ext, 1) + w0 * _shift_down(ext, 2)
                y_ref[rows, cols] = (split(1, rows) * cv * _silu(split(3, rows))).astype(BF16)
                _, _, vn = _layer_norm_head(split(5, rows), lg, lb)
                mixed = jnp.dot(wm, vn.astype(BF16), preferred_element_type=F32) + bias
                y_ref[rows, E_A + cols.start:E_A + cols.stop] = (split(4, rows) * mixed * _silu(split(6, rows))).astype(BF16)

    const3 = lambda i: (wl, 0, 0)
    const4 = lambda i: (wl, 0, 0, 0)
    return pl.pallas_call(
        body, name=name, grid=(S // TS,), out_shape=jax.ShapeDtypeStruct((S, 2 * E_A), BF16),
        in_specs=[pl.BlockSpec((TS, 7 * E_A), lambda i: (i, 0)), _halo_before(E_A, 0), _halo_before(E_A, 2),
                  pl.BlockSpec((None, 3, E_A), const3), pl.BlockSpec((None, 1, E_A), const3),
                  pl.BlockSpec((None, 1, E_A), const3), pl.BlockSpec((None, NDEV, CHUNK, CHUNK), const4),
                  pl.BlockSpec((None, NDEV, CHUNK, 1), const4)],
        out_specs=pl.BlockSpec((TS, 2 * E_A), lambda i: (i, 0)),
        compiler_params=_params(48, ("arbitrary",)),
    )(proj, proj, proj, convw, ln_g3, ln_b3, sgu_w, sgu_bcol)


def _even_mix_bwd(proj, d_ycat, convw, ln_g3, ln_b3, sgu_w, sgu_bcol, wl, name):
    nsteps = S // TS

    def body(pj_ref, hh_ref, hc_ref, hb_ref, hz_ref, dy_ref, hdy_ref, cw_ref, lg_ref, lb_ref, sw_ref, sb_ref,
             dp_ref, dcw_ref, dlg_ref, dlb_ref, dsw_ref, dsb_ref):
        step = pl.program_id(0)

        @pl.when(step == 0)
        def _():
            for ref in (dcw_ref, dlg_ref, dlb_ref, dsw_ref, dsb_ref):
                ref[...] = jnp.zeros_like(ref)

        live_before = (step > 0).astype(F32)
        live_after = (step < nsteps - 1).astype(F32)
        causal = _causal_mask()
        for j in range(E_A // HEAD):
            cols = slice(j * HEAD, (j + 1) * HEAD)
            w0, w1, w2 = cw_ref[0:1, cols], cw_ref[1:2, cols], cw_ref[2:3, cols]
            lg, lb = lg_ref[:, cols], lb_ref[:, cols]
            wmf = jnp.where(causal, sw_ref[j], 0.0)
            wm, wmt = wmf.astype(BF16), wmf.T.astype(BF16)
            bias = sb_ref[j]

            def split(s, rows, cols=cols):
                return pj_ref[rows, s * E_A + cols.start:s * E_A + cols.stop].astype(F32)

            def put(s, rows, val, cols=cols):
                dp_ref[rows, s * E_A + cols.start:s * E_A + cols.stop] = val.astype(BF16)

            ps = [split(2, slice(n * CHUNK, (n + 1) * CHUNK)) * split(0, slice(n * CHUNK, (n + 1) * CHUNK)) for n in range(NCH)]
            next_head = (hdy_ref[:, cols].astype(F32) * hb_ref[:, cols].astype(F32) * _silu(hz_ref[:, cols].astype(F32))
                         * live_after)
            acc_w = [jnp.zeros((1, HEAD), F32) for _ in range(3)]
            for n in reversed(range(NCH)):
                rows = slice(n * CHUNK, (n + 1) * CHUNK)
                p = ps[n]
                tail = ps[n - 1][CHUNK - HALO:] if n > 0 else hc_ref[:, cols].astype(F32) * hh_ref[:, cols].astype(F32) * live_before
                ext = jnp.concatenate([tail, p], axis=0)
                p1, p2 = _shift_down(ext, 1), _shift_down(ext, 2)
                cv = w2 * p + w1 * p1 + w0 * p2
                a_b, a_z = split(1, rows), split(3, rows)
                sz, dsz = _silu_and_grad(a_z)
                dya = dy_ref[rows, cols].astype(F32)
                put(1, rows, dya * cv * sz)
                put(3, rows, dya * a_b * cv * dsz)
                gcv = dya * a_b * sz
                acc_w[0] += jnp.sum(gcv * p2, axis=0, keepdims=True)
                acc_w[1] += jnp.sum(gcv * p1, axis=0, keepdims=True)
                acc_w[2] += jnp.sum(gcv * p, axis=0, keepdims=True)
                gext = jnp.concatenate([gcv, next_head], axis=0)
                next_head = gcv[:HALO]
                dpv = w2 * gcv + w1 * _shift_up(gext, 1) + w0 * _shift_up(gext, 2)
                put(2, rows, dpv * split(0, rows))
                put(0, rows, dpv * split(2, rows))
            for k in range(3):
                dcw_ref[k:k + 1, cols] += acc_w[k]

            acc_lg, acc_lb = jnp.zeros((1, HEAD), F32), jnp.zeros((1, HEAD), F32)
            acc_sw, acc_sb = jnp.zeros((CHUNK, CHUNK), F32), jnp.zeros((CHUNK, 1), F32)
            for n in range(NCH):
                rows = slice(n * CHUNK, (n + 1) * CHUNK)
                u, z = split(4, rows), split(6, rows)
                vhat, rstd, vn = _layer_norm_head(split(5, rows), lg, lb)
                vn16 = vn.astype(BF16)
                mixed = jnp.dot(wm, vn16, preferred_element_type=F32) + bias
                sz, dsz = _silu_and_grad(z)
                dyb = dy_ref[rows, E_A + cols.start:E_A + cols.stop].astype(F32)
                put(4, rows, dyb * mixed * sz)
                put(6, rows, dyb * u * mixed * dsz)
                dmix = dyb * u * sz
                dmix16 = dmix.astype(BF16)
                acc_sb += jnp.sum(dmix, axis=1, keepdims=True)
                acc_sw += lax.dot_general(dmix16, vn16, (((1,), (1,)), ((), ())), preferred_element_type=F32)
                dvn = jnp.dot(wmt, dmix16, preferred_element_type=F32)
                acc_lg += jnp.sum(dvn * vhat, axis=0, keepdims=True)
                acc_lb += jnp.sum(dvn, axis=0, keepdims=True)
                dvh = dvn * lg
                put(5, rows, rstd * (dvh - jnp.mean(dvh, axis=-1, keepdims=True)
                                     - vhat * jnp.mean(dvh * vhat, axis=-1, keepdims=True)))
            dlg_ref[:, cols] += acc_lg
            dlb_ref[:, cols] += acc_lb
            dsw_ref[j] += jnp.where(causal, acc_sw, 0.0)
            dsb_ref[j] += acc_sb

    const3 = lambda i: (wl, 0, 0)
    const4 = lambda i: (wl, 0, 0, 0)
    fixed2 = lambda i: (0, 0)
    fixed3 = lambda i: (0, 0, 0)
    return pl.pallas_call(
        body, name=name, grid=(nsteps,),
        out_shape=[jax.ShapeDtypeStruct((S, 7 * E_A), BF16), jax.ShapeDtypeStruct((3, E_A), F32),
                   jax.ShapeDtypeStruct((1, E_A), F32), jax.ShapeDtypeStruct((1, E_A), F32),
                   jax.ShapeDtypeStruct((NDEV, CHUNK, CHUNK), F32), jax.ShapeDtypeStruct((NDEV, CHUNK, 1), F32)],
        in_specs=[pl.BlockSpec((TS, 7 * E_A), lambda i: (i, 0)), _halo_before(E_A, 0), _halo_before(E_A, 2),
                  _halo_after(E_A, 1), _halo_after(E_A, 3),
                  pl.BlockSpec((TS, 2 * E_A), lambda i: (i, 0)), _halo_after(E_A, 0),
                  pl.BlockSpec((None, 3, E_A), const3), pl.BlockSpec((None, 1, E_A), const3),
                  pl.BlockSpec((None, 1, E_A), const3), pl.BlockSpec((None, NDEV, CHUNK, CHUNK), const4),
                  pl.BlockSpec((None, NDEV, CHUNK, 1), const4)],
        out_specs=[pl.BlockSpec((TS, 7 * E_A), lambda i: (i, 0)), pl.BlockSpec((3, E_A), fixed2),
                   pl.BlockSpec((1, E_A), fixed2), pl.BlockSpec((1, E_A), fixed2),
                   pl.BlockSpec((NDEV, CHUNK, CHUNK), fixed3), pl.BlockSpec((NDEV, CHUNK, 1), fixed3)],
        compiler_params=_params(56, ("arbitrary",)),
    )(proj, proj, proj, proj, proj, d_ycat, d_ycat, convw, ln_g3, ln_b3, sgu_w, sgu_bcol)


def _window_count(step, n, win, ext_before):
    rows = CHUNK if ext_before else CHUNK + HALO
    t = step * TS + n * CHUNK + lax.broadcasted_iota(jnp.int32, (rows, 1), 0)
    return jnp.minimum(t + 1, win).astype(F32)


def _pool_weight(wp_ref, g):
    return jnp.concatenate([wp_ref[d, g] for d in range(NDEV)], axis=0)


def _pooled_chunk(p, tail, win, count):
    sums = jnp.concatenate([tail, p], axis=0)
    shift = 1
    while shift < win:
        sums = sums + pltpu.roll(sums, shift, 0)
        shift *= 2
    return sums[HALO:] / count - p


def _pool_mix_fwd(proj, wpool, pscale4, wl, name):
    e_c = 4 * G_C

    def body(pj_ref, hp_ref, wp_ref, ps_ref, y_ref, pooled_scr, yraw_scr):
        step = pl.program_id(0)
        live = (step > 0).astype(F32)
        for g, win in enumerate(POOL_WINDOWS):
            for q in range(G_C // LANE):
                cols = slice(g * G_C + q * LANE, g * G_C + (q + 1) * LANE)
                tail = hp_ref[:, cols].astype(F32) * live
                for n in range(NCH):
                    rows = slice(n * CHUNK, (n + 1) * CHUNK)
                    p = pj_ref[rows, cols].astype(F32)
                    pooled_scr[rows, q * LANE:(q + 1) * LANE] = _pooled_chunk(
                        p, tail, win, _window_count(step, n, win, True)).astype(BF16)
                    tail = p[CHUNK - HALO:]
            yraw_scr[...] = jnp.dot(pooled_scr[...], _pool_weight(wp_ref, g), preferred_element_type=F32)
            for q in range(G_C // LANE):
                cols = slice(g * G_C + q * LANE, g * G_C + (q + 1) * LANE)
                for n in range(NCH):
                    rows = slice(n * CHUNK, (n + 1) * CHUNK)
                    z = pj_ref[rows, e_c + cols.start:e_c + cols.stop].astype(F32)
                    y_ref[rows, cols] = (yraw_scr[rows, q * LANE:(q + 1) * LANE] * ps_ref[:, cols] * _silu(z)).astype(BF16)

    return pl.pallas_call(
        body, name=name, grid=(S // TS,), out_shape=jax.ShapeDtypeStruct((S, e_c), BF16),
        in_specs=[pl.BlockSpec((TS, 2 * e_c), lambda i: (i, 0)), _halo_before(e_c, 0),
                  pl.BlockSpec((NDEV, 4, G_C // NDEV, G_C), lambda i: (0, 0, 0, 0)),
                  pl.BlockSpec((None, 1, e_c), lambda i: (wl, 0, 0))],
        out_specs=pl.BlockSpec((TS, e_c), lambda i: (i, 0)),
        scratch_shapes=[pltpu.VMEM((TS, G_C), BF16), pltpu.VMEM((TS, G_C), F32)],
        compiler_params=_params(48, ("arbitrary",)),
    )(proj, proj, wpool, pscale4)


def _pool_mix_bwd(proj, d_ycat, wpool, pscale4, wl, name):
    e_c = 4 * G_C
    nsteps = S // TS
    rb = G_C // NDEV

    def body(pj_ref, hp_ref, hz_ref, dy_ref, hdy_ref, wp_ref, ps_ref,
             dp_ref, dps_ref, dwp_ref, pooled_scr, yraw_scr, dyraw_scr, dpool_scr, acc_w):
        step = pl.program_id(0)

        @pl.when(step == 0)
        def _():
            dps_ref[...] = jnp.zeros_like(dps_ref)
            acc_w[...] = jnp.zeros_like(acc_w)

        live_before = (step > 0).astype(F32)
        live_after = (step < nsteps - 1).astype(F32)
        for g, win in enumerate(POOL_WINDOWS):
            weight = _pool_weight(wp_ref, g)
            for q in range(G_C // LANE):
                cols = slice(g * G_C + q * LANE, g * G_C + (q + 1) * LANE)
                tail = hp_ref[:, cols].astype(F32) * live_before
                for n in range(NCH):
                    rows = slice(n * CHUNK, (n + 1) * CHUNK)
                    p = pj_ref[rows, cols].astype(F32)
                    pooled_scr[rows, q * LANE:(q + 1) * LANE] = _pooled_chunk(
                        p, tail, win, _window_count(step, n, win, True)).astype(BF16)
                    tail = p[CHUNK - HALO:]
            yraw_scr[...] = jnp.dot(pooled_scr[...], weight, preferred_element_type=F32)
            for q in range(G_C // LANE):
                cols = slice(g * G_C + q * LANE, g * G_C + (q + 1) * LANE)
                local = slice(q * LANE, (q + 1) * LANE)
                scale = ps_ref[:, cols]
                acc_ps = jnp.zeros((1, LANE), F32)
                for n in range(NCH):
                    rows = slice(n * CHUNK, (n + 1) * CHUNK)
                    sz, dsz = _silu_and_grad(pj_ref[rows, e_c + cols.start:e_c + cols.stop].astype(F32))
                    dyv = dy_ref[rows, cols].astype(F32)
                    yraw = yraw_scr[rows, local]
                    dyraw_scr[rows, local] = (dyv * scale * sz).astype(BF16)
                    acc_ps += jnp.sum(dyv * yraw * sz, axis=0, keepdims=True)
                    dp_ref[rows, e_c + cols.start:e_c + cols.stop] = (dyv * yraw * scale * dsz).astype(BF16)
                dps_ref[:, cols] += acc_ps
                dyraw_scr[TS:, local] = (hdy_ref[:, cols].astype(F32) * scale * _silu(hz_ref[:, cols].astype(F32))
                                         * live_after).astype(BF16)
            dpool_scr[...] = lax.dot_general(dyraw_scr[...], weight, (((1,), (1,)), ((), ())), preferred_element_type=F32)
            acc_w[g] += lax.dot_general(pooled_scr[...], dyraw_scr[:TS, :], (((0,), (0,)), ((), ())),
                                        preferred_element_type=F32)
            for q in range(G_C // LANE):
                cols = slice(g * G_C + q * LANE, g * G_C + (q + 1) * LANE)
                local = slice(q * LANE, (q + 1) * LANE)
                for n in range(NCH):
                    rows = slice(n * CHUNK, (n + 1) * CHUNK)
                    ext = dpool_scr[n * CHUNK:(n + 1) * CHUNK + HALO, local]
                    sums = ext / _window_count(step, n, win, False)
                    shift = 1
                    while shift < win:
                        sums = sums + pltpu.roll(sums, CHUNK + HALO - shift, 0)
                        shift *= 2
                    dp_ref[rows, cols] = (sums[:CHUNK] - ext[:CHUNK]).astype(BF16)

        @pl.when(step == nsteps - 1)
        def _():
            for g in range(4):
                for d in range(NDEV):
                    dwp_ref[d % 2, d // 2, g] = acc_w[g, d * rb:(d + 1) * rb, :].astype(BF16)

    in_specs = [pl.BlockSpec((TS, 2 * e_c), lambda i: (i, 0)), _halo_before(e_c, 0), _halo_after(e_c, 1),
                pl.BlockSpec((TS, e_c), lambda i: (i, 0)), _halo_after(e_c, 0),
                pl.BlockSpec((NDEV, 4, rb, G_C), lambda i: (0, 0, 0, 0)),
                pl.BlockSpec((None, 1, e_c), lambda i: (wl, 0, 0))]
    args = [proj, proj, proj, d_ycat, d_ycat, wpool, pscale4]
    return pl.pallas_call(
        body, name=name, grid=(nsteps,),
        out_shape=[jax.ShapeDtypeStruct((S, 2 * e_c), BF16), jax.ShapeDtypeStruct((1, e_c), F32),
                   jax.ShapeDtypeStruct((2, NDEV // 2) + wpool.shape[1:], BF16)],
        in_specs=in_specs,
        out_specs=[pl.BlockSpec((TS, 2 * e_c), lambda i: (i, 0)), pl.BlockSpec((1, e_c), lambda i: (0, 0)),
                   pl.BlockSpec((2, NDEV // 2, 4, rb, G_C), lambda i: (0, 0, 0, 0, 0))],
        scratch_shapes=[pltpu.VMEM((TS, G_C), BF16), pltpu.VMEM((TS, G_C), F32), pltpu.VMEM((TS + HALO, G_C), BF16),
                        pltpu.VMEM((TS + HALO, G_C), F32), pltpu.VMEM((4, G_C, G_C), F32)],
        compiler_params=_params(56, ("arbitrary",)),
    )(*args)


def _adamw(w, g, m, v):
    m = ADAM_B1 * m + (1.0 - ADAM_B1) * g
    v = ADAM_B2 * v + (1.0 - ADAM_B2) * jnp.square(g)
    m_hat = m / (1.0 - ADAM_B1 ** ADAM_STEP)
    v_hat = v / (1.0 - ADAM_B2 ** ADAM_STEP)
    delta = -ADAM_LR * (m_hat / (jnp.sqrt(v_hat) + ADAM_EPS) + ADAM_WD * w)
    return delta, m, v


def _adam_sharded(w, m, v, chip_parts, landed, my_chip, name):
    nl, nr, ncol = w.shape
    tr = 128
    steps = nr // tr

    def body(chip_ref, w_ref, m_ref, v_ref, *rest):
        parts, zones = rest[:nl], rest[nl:2 * nl]
        g_ref, d_ref, nm_ref, nv_ref = rest[2 * nl:]
        layer = pl.program_id(0)
        g = jnp.zeros((tr, ncol), F32)
        for l in range(nl):
            gl = parts[l][...].astype(F32)
            for q in range(3):
                gl = gl + zones[l][q].astype(F32)
            g = jnp.where(layer == l, gl, g)
        g_ref[...] = g
        d_ref[...], nm_ref[...], nv_ref[...] = _adamw(w_ref[...], g, m_ref[...], v_ref[...])

    def rows_of(l):
        return lambda layer, i, chip_ref: jnp.where(layer == l, i, jnp.where(layer < l, 0, steps - 1))

    spec = pl.BlockSpec((None, tr, ncol), lambda layer, i, chip_ref: (layer, i, 0))
    part_specs = [pl.BlockSpec((None, tr, ncol), lambda layer, i, chip_ref, l=l: (chip_ref[0], rows_of(l)(layer, i, chip_ref), 0))
                  for l in range(nl)]
    zone_specs = [pl.BlockSpec((3, tr, ncol), lambda layer, i, chip_ref, l=l: (0, rows_of(l)(layer, i, chip_ref), 0))
                  for l in range(nl)]
    grid_spec = pltpu.PrefetchScalarGridSpec(
        num_scalar_prefetch=1, grid=(nl, steps), in_specs=[spec, spec, spec] + part_specs + zone_specs, out_specs=[spec] * 4)
    return pl.pallas_call(
        body, name=name, grid_spec=grid_spec, out_shape=[jax.ShapeDtypeStruct(w.shape, F32)] * 4,
        compiler_params=_params(40, ("arbitrary", "arbitrary")),
    )(my_chip, w, m, v, *chip_parts, *landed)


def _adam_small(w, g, m, v, name):
    def body(w_ref, g_ref, m_ref, v_ref, d_ref, nm_ref, nv_ref):
        d_ref[...], nm_ref[...], nv_ref[...] = _adamw(w_ref[...], g_ref[...], m_ref[...], v_ref[...])

    return pl.pallas_call(body, name=name, out_shape=[jax.ShapeDtypeStruct(w.shape, F32)] * 3,
                          in_specs=[VMEM_FULL] * 4, out_specs=[VMEM_FULL] * 3, compiler_params=_params(32))(w, g, m, v)


def _sum_devices(gathered, name):
    _, nr, ncol = gathered.shape

    def body(g_ref, o_ref):
        acc = g_ref[0]
        for s in range(1, NDEV):
            acc = acc + g_ref[s]
        o_ref[...] = acc

    return pl.pallas_call(body, name=name, grid=(1,), out_shape=jax.ShapeDtypeStruct((nr, ncol), F32),
                          in_specs=[pl.BlockSpec((NDEV, nr, ncol), lambda i: (0, 0, 0))],
                          out_specs=pl.BlockSpec((nr, ncol), lambda i: (0, 0)),
                          compiler_params=_params(48, ("arbitrary",)))(gathered)


def _ada_weight_adam(cact_t, dmod_mine, w, m, v):
    def body(ct_ref, dm_ref, w_ref, m_ref, v_ref, g_ref, d_ref, nm_ref, nv_ref):
        ct, dm = ct_ref[...], dm_ref[...]
        g = ct[:, 0:1] * dm[0:1, :]
        for e in range(1, NDEV):
            g = g + ct[:, e:e + 1] * dm[e:e + 1, :]
        g_ref[...] = g
        d_ref[...], nm_ref[...], nv_ref[...] = _adamw(w_ref[...], g, m_ref[...], v_ref[...])

    spec = pl.BlockSpec((None, D, ADA_NC), lambda l: (l, 0, 0))
    return pl.pallas_call(
        body, name="ada_weight_adam", grid=(DEPTH,), out_shape=[jax.ShapeDtypeStruct(w.shape, F32)] * 4,
        in_specs=[pl.BlockSpec((D, NDEV), lambda l: (0, 0)), pl.BlockSpec((None, NDEV, ADA_NC), lambda l: (l, 0, 0)),
                  spec, spec, spec],
        out_specs=[spec] * 4, compiler_params=_params(40, ("arbitrary",)),
    )(cact_t, dmod_mine, w, m, v)


def _pad_rows(a, rows):
    a = a.reshape(-1, D)
    return jnp.pad(a, ((0, rows - a.shape[0]), (0, 0)))


def kernel(x, c, norm_g, ada_w, ada_b, ab_w_in, ab_conv_w, ab_ln_g, ab_ln_b, ab_sgu_w, ab_sgu_b, ab_w_out, c_w_in, c_pool_w, c_pool_scale, c_w_out, final_g, loss_target, m_norm_g, m_ada_w, m_ada_b, m_ab_w_in, m_ab_conv_w, m_ab_ln_g, m_ab_ln_b, m_ab_sgu_w, m_ab_sgu_b, m_ab_w_out, m_c_w_in, m_c_pool_w, m_c_pool_scale, m_c_w_out, m_final_g, v_norm_g, v_ada_w, v_ada_b, v_ab_w_in, v_ab_conv_w, v_ab_ln_g, v_ab_ln_b, v_ab_sgu_w, v_ab_sgu_b, v_ab_w_out, v_c_w_in, v_c_pool_w, v_c_pool_scale, v_c_w_out, v_final_g):
    x_pos, y_pos, c_pos = _position()
    me = _index((x_pos, y_pos, c_pos))
    core = c_pos.astype(jnp.int32).reshape(1)
    my_chip = (2 * x_pos + y_pos).astype(jnp.int32).reshape(1)
    x0 = x.reshape(S, D)
    target = loss_target.reshape(S, D)
    norm_g3 = norm_g.reshape(DEPTH, 1, D)
    ln_g3, ln_b3 = ab_ln_g.reshape(2, 1, E_A), ab_ln_b.reshape(2, 1, E_A)
    sgu_bcol = ab_sgu_b.reshape(2, NDEV, CHUNK, 1)
    rb = G_C // NDEV
    pool_w3, m_pool_w3, v_pool_w3 = (a.reshape(2, 4 * rb, G_C) for a in (c_pool_w, m_c_pool_w, v_c_pool_w))

    cact_all, mod = _ada_forward(c, ada_w, ada_b)
    convw_all, pscale_all = _gather([ab_conv_w, c_pool_scale], "gather_small_weights")
    convw = jnp.transpose(convw_all, (1, 2, 0, 3)).reshape(2, 3, E_A)
    pscale4 = jnp.transpose(pscale_all, (1, 0, 2)).reshape(2, 1, 4 * G_C)
    flights, after = [], [mod, convw_all]
    for layer in range(DEPTH):
        wl = layer // 2
        if layer % 2 == 0:
            shards = [_cast_bf16(ab_w_in, wl, f"cast_w_in_{layer}"), _cast_bf16(ab_w_out, wl, f"cast_w_out_{layer}")]
        else:
            shards = [_cast_bf16(c_w_in, wl, f"cast_w_in_{layer}"), _cast_bf16(c_w_out, wl, f"cast_w_out_{layer}"),
                      _cast_bf16(pool_w3, wl, f"cast_pool_w_{layer}").reshape(4, rb, G_C)]
        send1, recv1, shards, zones = _gather_start(shards, after, f"gather_start_{layer}")
        flights.append((send1, recv1, shards, zones))
        after = [shards[0]]

    xs, hs, projs, ycats, outs, gathered_w = [x0], [], [], [], [], []
    after = after[0]
    for layer in range(DEPTH):
        wl = layer // 2
        even = layer % 2 == 0
        send1, recv1, shards, zones = flights[layer]
        send2, recv2, zones = _gather_mid(zones, recv1, after, f"gather_mid_{layer}")
        wg = _gather_end(shards, zones, send1, recv1, send2, recv2, f"gather_end_{layer}")
        gathered_w.append(wg)
        h, proj = _norm_proj(xs[-1], mod, norm_g3, wg[0], layer, f"norm_proj_{layer}")
        if even:
            ycat = _even_mix_fwd(proj, convw, ln_g3, ln_b3, ab_sgu_w, sgu_bcol, wl, f"even_mix_fwd_{layer}")
        else:
            ycat = _pool_mix_fwd(proj, wg[2], pscale4, wl, f"pool_mix_fwd_{layer}")
        x_new, out = _out_proj(ycat, wg[1], xs[-1], mod, layer, f"out_proj_{layer}")
        after = x_new
        xs.append(x_new)
        hs.append(h)
        projs.append(proj)
        ycats.append(ycat)
        outs.append(out)

    dx, loss_part, d_final_g = _final_loss(xs[DEPTH], target, final_g.reshape(1, D))
    loss = lax.psum(loss_part[0, 0], ("x", "y", "c"))

    d_mod, d_norm_g = [None] * DEPTH, [None] * DEPTH
    small, scatters = {}, {}
    for layer in reversed(range(DEPTH)):
        wl = layer // 2
        even = layer % 2 == 0
        wg = gathered_w[layer]
        d_out, d_gate = _dout(dx, outs[layer], mod, layer, f"dout_{layer}")
        d_ycat = _dycat(d_out, wg[1], f"dycat_{layer}")
        parts = [None, _weight_grad(ycats[layer], d_out, f"grad_w_out_{layer}")]
        if even:
            d_proj, d_cw, d_lg, d_lb, d_sw, d_sb = _even_mix_bwd(
                projs[layer], d_ycat, convw, ln_g3, ln_b3, ab_sgu_w, sgu_bcol, wl, f"even_mix_bwd_{layer}")
            small[layer] = (d_cw, d_lg, d_lb, d_sw, d_sb)
        else:
            d_proj, d_ps, d_pool = _pool_mix_bwd(projs[layer], d_ycat, wg[2], pscale4, wl, f"pool_mix_bwd_{layer}")
            small[layer] = (d_ps,)
            parts.append(d_pool)
        parts[0] = _weight_grad(hs[layer], d_proj, f"grad_w_in_{layer}")
        from_sibling = _pair_exchange(parts, f"pair_exchange_{layer}")
        chip_parts = [_pair_sum(p, q, core, f"pair_sum_{layer}_{j}") for j, (p, q) in enumerate(zip(parts, from_sibling))]
        send_sems, recv_sems, chip_parts, zones = _scatter_start(chip_parts, f"scatter_start_{layer}")
        d_h, chip_parts[0] = _dh(d_proj, wg[0], chip_parts[0], f"dh_{layer}")
        scatters[layer] = (send_sems, recv_sems, chip_parts, zones)
        dx, d_shift, d_scale, d_norm_g[layer] = _norm_bwd(d_h, xs[layer], dx, mod, norm_g3, layer, f"norm_bwd_{layer}")
        d_mod[layer] = jnp.concatenate([d_shift, d_scale, d_gate], axis=0)
    grad_x = dx.reshape(x.shape)

    landed = {}
    for layer in range(DEPTH):
        send_sems, recv_sems, chip_parts, zones = scatters[layer]
        landed[layer] = _scatter_end(chip_parts, zones, send_sems, recv_sems, dx, f"scatter_end_{layer}")

    def flat(a):
        return a.reshape(a.shape[0], -1, a.shape[-1])

    res = {}
    for k, j, layers, (w, m, v) in [
            ("ab_w_in", 0, (0, 2), (ab_w_in, m_ab_w_in, v_ab_w_in)), ("ab_w_out", 1, (0, 2), (ab_w_out, m_ab_w_out, v_ab_w_out)),
            ("c_w_in", 0, (1, 3), (c_w_in, m_c_w_in, v_c_w_in)), ("c_w_out", 1, (1, 3), (c_w_out, m_c_w_out, v_c_w_out)),
            ("c_pool_w", 2, (1, 3), (pool_w3, m_pool_w3, v_pool_w3))]:
        outs4 = _adam_sharded(w, m, v, [flat(landed[l][0][j]) for l in layers], [flat(landed[l][1][j]) for l in layers],
                              my_chip, "adam_" + k)
        res[k] = [o.reshape(c_pool_w.shape) if k == "c_pool_w" else o for o in outs4]

    sections = [("norm_g", jnp.concatenate(d_norm_g, axis=0), 8),
                ("d_mod", jnp.concatenate(d_mod, axis=0), 16),
                ("ab_ln_g", jnp.concatenate([small[0][1], small[2][1]], axis=0), 8),
                ("ab_ln_b", jnp.concatenate([small[0][2], small[2][2]], axis=0), 8),
                ("ab_sgu_b", jnp.stack([small[0][4], small[2][4]]), 8),
                ("final_g", d_final_g, 8),
                ("ab_conv_w", jnp.stack([small[0][0], small[2][0]]), 8),
                ("c_pool_scale", jnp.concatenate([small[1][0], small[3][0]], axis=0), 8),
                ("ab_sgu_w", jnp.stack([small[0][3], small[2][3]]), 256)]
    offsets, at = {}, 0
    for name, _, rows in sections:
        offsets[name] = (at, rows)
        at += rows
    packed = jnp.concatenate([_pad_rows(a, rows) for _, a, rows in sections], axis=0)
    gathered = _gather([packed], "gather_small_grads")[0]
    summed = _sum_devices(gathered, "sum_small_grads")

    def section(name, nrows, src=summed):
        start = offsets[name][0]
        return src[..., start:start + nrows, :]

    grads = {
        "norm_g": section("norm_g", DEPTH),
        "ada_b": section("d_mod", 3 * DEPTH).reshape(DEPTH, 3 * D),
        "ab_ln_g": section("ab_ln_g", 2), "ab_ln_b": section("ab_ln_b", 2),
        "ab_sgu_b": section("ab_sgu_b", 2).reshape(ab_sgu_b.shape),
        "final_g": section("final_g", 1),
        "ab_sgu_w": section("ab_sgu_w", 256).reshape(ab_sgu_w.shape),
        "ab_conv_w": lax.dynamic_slice_in_dim(section("ab_conv_w", 6).reshape(2, 3, E_A), me * HEAD, HEAD, axis=2),
        "c_pool_scale": lax.dynamic_slice_in_dim(section("c_pool_scale", 4).reshape(2, 4 * G_C), me * 256, 256, axis=1),
    }
    small_w = {"norm_g": (norm_g, m_norm_g, v_norm_g), "ada_b": (ada_b, m_ada_b, v_ada_b),
               "ab_ln_g": (ab_ln_g, m_ab_ln_g, v_ab_ln_g), "ab_ln_b": (ab_ln_b, m_ab_ln_b, v_ab_ln_b),
               "ab_sgu_b": (ab_sgu_b, m_ab_sgu_b, v_ab_sgu_b),
               "final_g": (final_g.reshape(1, D), m_final_g.reshape(1, D), v_final_g.reshape(1, D)),
               "ab_sgu_w": (ab_sgu_w, m_ab_sgu_w, v_ab_sgu_w), "ab_conv_w": (ab_conv_w, m_ab_conv_w, v_ab_conv_w),
               "c_pool_scale": (c_pool_scale, m_c_pool_scale, v_c_pool_scale)}
    for k, (w, m, v) in small_w.items():
        res[k] = [grads[k]] + list(_adam_small(w, grads[k], m, v, "adam_" + k))
    res["final_g"] = [a.reshape(D) for a in res["final_g"]]

    dmod_all = section("d_mod", 3 * DEPTH, gathered).reshape(NDEV, DEPTH, 3 * D)
    dmod_mine = jnp.transpose(lax.dynamic_slice_in_dim(dmod_all, me * ADA_NC, ADA_NC, axis=2), (1, 0, 2))
    res["ada_w"] = _ada_weight_adam(jnp.transpose(cact_all.reshape(NDEV, D)), dmod_mine, ada_w, m_ada_w, v_ada_w)

    order = ["norm_g", "ada_w", "ada_b", "ab_w_in", "ab_conv_w", "ab_ln_g", "ab_ln_b", "ab_sgu_w", "ab_sgu_b",
             "ab_w_out", "c_w_in", "c_pool_w", "c_pool_scale", "c_w_out", "final_g"]
    return (loss, grad_x, *[res[k][0] for k in order], *[res[k][1] for k in order],
            *[res[k][2] for k in order], *[res[k][3] for k in order])
```

```python
import jax
import jax.numpy as jnp
from jax import lax
from jax.experimental import pallas as pl
from jax.experimental.pallas import tpu as pltpu

F32, BF16 = jnp.float32, jnp.bfloat16
S, D = 2048, 1024
NDEV = 8
DEPTH = 4
EPS = 1e-6
E_A = 1024
HEAD = 128
CHUNK = 128
POOL_WINDOWS = (2, 4, 8, 16)
G_C = 512
HALO = 16
ADA_NC = 384
MIB = 1024 * 1024
LANE = 128

ADAM_LR, ADAM_B1, ADAM_B2, ADAM_EPS, ADAM_WD, ADAM_STEP = 0.001, 0.9, 0.999, 1e-08, 0.01, 10

ANY = pl.BlockSpec(memory_space=pl.ANY)
VMEM_FULL = pl.BlockSpec(memory_space=pltpu.VMEM)
IN_HBM = pl.BlockSpec(memory_space=pltpu.HBM)
SEMAPHORES = pl.BlockSpec(memory_space=pltpu.SEMAPHORE)
IN_FLIGHT = pltpu.SideEffectType.DATAFLOW_SIDE_EFFECTING


V7X_VMEM_MIB = 64
VMEM_LIMIT_MIB = V7X_VMEM_MIB - 4


def _params(semantics=None):
    return pltpu.CompilerParams(dimension_semantics=semantics, vmem_limit_bytes=VMEM_LIMIT_MIB * MIB)


def _silu(z):
    return z * jax.nn.sigmoid(z)


def _silu_and_grad(z):
    sig = jax.nn.sigmoid(z)
    return z * sig, sig * (1.0 + z * (1.0 - sig))


def _position():
    return lax.axis_index("x"), lax.axis_index("y"), lax.axis_index("c")


def _index(pos):
    return 4 * pos[0] + 2 * pos[1] + pos[2]


def _peer(pos, k):
    flipped = tuple(1 - p if (k >> (2 - b)) & 1 else p for b, p in enumerate(pos))
    return flipped, _index(flipped)


def _remote(src, dst, send_sem, recv_sem, device):
    return pltpu.make_async_remote_copy(src_ref=src, dst_ref=dst, send_sem=send_sem, recv_sem=recv_sem,
                                        device_id=device, device_id_type=pl.DeviceIdType.MESH)


def _gather(arrays, name):
    n = len(arrays)
    out_shape = [jax.ShapeDtypeStruct((NDEV,) + a.shape, a.dtype) for a in arrays]

    def body(*refs):
        ins, outs = refs[:n], refs[n:2 * n]
        send_sems, recv_sems, own_sems = refs[2 * n:]
        x, y, c = _position()
        me = _index((x, y, c))
        sibling = (x, y, 1 - c)
        chips = [(1 - x, y), (x, 1 - y), (1 - x, 1 - y)]

        def block_copy(j, k, owner, to, src=None):
            rows = outs[j].at[_index(owner)]
            return _remote(rows if src is None else src, rows, send_sems.at[j, k], recv_sems.at[j, k], to)

        own, first, passed = [], [], []
        for j in range(n):
            own.append(pltpu.make_async_copy(ins[j], outs[j].at[me], own_sems.at[j]))
            first.append(block_copy(j, 0, (x, y, c), sibling, src=ins[j]))
            first += [block_copy(j, 1 + q, (x, y, c), (*chip, c), src=ins[j]) for q, chip in enumerate(chips)]
        for copy in own + first:
            copy.start()
        for q, chip in enumerate(chips):
            for j in range(n):
                block_copy(j, 1 + q, (*chip, c), (x, y, c)).wait_recv()
                forward = block_copy(j, 4 + q, (*chip, c), sibling)
                forward.start()
                passed.append(forward)
        for j in range(n):
            block_copy(j, 0, sibling, (x, y, c)).wait_recv()
            for q, chip in enumerate(chips):
                block_copy(j, 4 + q, (*chip, 1 - c), (x, y, c)).wait_recv()
        for copy in first + passed:
            copy.wait_send()
        for copy in own:
            copy.wait()

    return pl.pallas_call(
        body, name=name, out_shape=out_shape, in_specs=[ANY] * n, out_specs=[ANY] * n,
        scratch_shapes=[pltpu.SemaphoreType.DMA((n, NDEV - 1)), pltpu.SemaphoreType.DMA((n, NDEV - 1)),
                        pltpu.SemaphoreType.DMA((n,))],
    )(*arrays)


def _pair_exchange(parts, name):
    n = len(parts)

    def body(*refs):
        ins, outs = refs[:n], refs[n:2 * n]
        send_sems, recv_sems = refs[2 * n:]
        x, y, c = _position()
        copies = [_remote(ins[j].at[1 - c], outs[j], send_sems.at[j], recv_sems.at[j], (x, y, 1 - c)) for j in range(n)]
        for copy in copies:
            copy.start()
        for copy in copies:
            copy.wait()

    return pl.pallas_call(
        body, name=name, out_shape=[jax.ShapeDtypeStruct(p.shape[1:], p.dtype) for p in parts],
        in_specs=[ANY] * n, out_specs=[ANY] * n,
        scratch_shapes=[pltpu.SemaphoreType.DMA((n,)), pltpu.SemaphoreType.DMA((n,))],
    )(*parts)


def _pair_sum(part, from_sibling, core, name):
    ncol = part.shape[-1]
    p3 = part.reshape(2, -1, ncol)
    q2 = from_sibling.reshape(-1, ncol)
    nrows = q2.shape[0]
    tr = 512

    def body(core_ref, p_ref, q_ref, o_ref):
        o_ref[...] = (p_ref[...].astype(F32) + q_ref[...].astype(F32)).astype(BF16)

    grid_spec = pltpu.PrefetchScalarGridSpec(
        num_scalar_prefetch=1, grid=(nrows // tr,),
        in_specs=[pl.BlockSpec((None, tr, ncol), lambda i, core_ref: (core_ref[0], i, 0)),
                  pl.BlockSpec((tr, ncol), lambda i, core_ref: (i, 0))],
        out_specs=pl.BlockSpec((tr, ncol), lambda i, core_ref: (i, 0)))
    out = pl.pallas_call(body, name=name, grid_spec=grid_spec, out_shape=jax.ShapeDtypeStruct(q2.shape, BF16),
                         compiler_params=_params(("arbitrary",)))(core, p3, q2)
    return out.reshape(from_sibling.shape)


def _in_hbm(a):
    return pltpu.with_memory_space_constraint(a, pltpu.HBM)


def _chips(x, y):
    return [(1 - x, y), (x, 1 - y), (1 - x, 1 - y)]


def _to_zone(a, wl, me, dtype, name):
    _, rows, cols = a.shape
    tr = 256 if rows % 256 == 0 else rows

    def body(me_ref, a_ref, o_ref):
        o_ref[...] = a_ref[...].astype(dtype)

    grid_spec = pltpu.PrefetchScalarGridSpec(
        num_scalar_prefetch=1, grid=(rows // tr,),
        in_specs=[pl.BlockSpec((None, tr, cols), lambda i, me_ref: (wl, i, 0))],
        out_specs=pl.BlockSpec((None, tr, cols), lambda i, me_ref: (me_ref[0], i, 0)))
    return pl.pallas_call(body, name=name, grid_spec=grid_spec, out_shape=jax.ShapeDtypeStruct((NDEV, rows, cols), dtype),
                          compiler_params=_params(("arbitrary",)))(me, a)


def _gather_start(zones, after, carried, name):
    n, m = len(zones), len(carried)

    def body(*refs):
        zs = refs[:n]
        send_sems, recv_sems = refs[n + m + len(after):n + m + len(after) + 2]
        x, y, c = _position()
        me = _index((x, y, c))
        for j in range(n):
            mine = zs[j].at[me]
            _remote(mine, mine, send_sems.at[4 * j], recv_sems.at[4 * j], (x, y, 1 - c)).start()
            for q, chip in enumerate(_chips(x, y)):
                _remote(mine, mine, send_sems.at[4 * j + 1 + q], recv_sems.at[4 * j + 1 + q], (*chip, c)).start()

    outs = pl.pallas_call(
        body, name=name,
        out_shape=(pltpu.SemaphoreType.DMA((4 * n,)), pltpu.SemaphoreType.DMA((4 * n,)),
                   *[pltpu.HBM(z.shape, z.dtype) for z in zones], *[jax.ShapeDtypeStruct(a.shape, a.dtype) for a in carried]),
        in_specs=[IN_HBM] * n + [ANY] * (m + len(after)),
        out_specs=(SEMAPHORES, SEMAPHORES, *[IN_HBM] * n, *[ANY] * m),
        input_output_aliases={j: 2 + j for j in range(n + m)},
        compiler_params=pltpu.CompilerParams(has_side_effects=IN_FLIGHT),
    )(*[_in_hbm(z) for z in zones], *carried, *after)
    return outs[0], outs[1], list(outs[2:2 + n]), list(outs[2 + n:])


def _gather_mid(zones, recv_sems, after, name):
    n = len(zones)

    def body(*refs):
        zs, first_recv = refs[:n], refs[n]
        send_sems, recv_sems2 = refs[n + 2:n + 4]
        x, y, c = _position()
        for q, chip in enumerate(_chips(x, y)):
            for j in range(n):
                rows = zs[j].at[_index((*chip, c))]
                _remote(rows, rows, send_sems.at[3 * j + q], first_recv.at[4 * j + 1 + q], (x, y, c)).wait_recv()
                _remote(rows, rows, send_sems.at[3 * j + q], recv_sems2.at[3 * j + q], (x, y, 1 - c)).start()

    outs = pl.pallas_call(
        body, name=name,
        out_shape=(pltpu.SemaphoreType.DMA((3 * n,)), pltpu.SemaphoreType.DMA((3 * n,)),
                   *[pltpu.HBM(z.shape, z.dtype) for z in zones]),
        in_specs=[IN_HBM] * n + [SEMAPHORES, ANY], out_specs=(SEMAPHORES, SEMAPHORES, *[IN_HBM] * n),
        input_output_aliases={j: 2 + j for j in range(n)},
        compiler_params=pltpu.CompilerParams(has_side_effects=IN_FLIGHT),
    )(*zones, recv_sems, after)
    return outs[0], outs[1], list(outs[2:])


def _gather_end(zones, send1, recv1, send2, recv2, after, name):
    n = len(zones)

    def body(*refs):
        zs = refs[:n]
        s1, r1, s2, r2 = refs[n:n + 4]
        x, y, c = _position()
        me = (x, y, c)
        for j in range(n):
            rows = zs[j].at[_index((x, y, 1 - c))]
            _remote(rows, rows, s1.at[4 * j], r1.at[4 * j], me).wait_recv()
            for q, chip in enumerate(_chips(x, y)):
                rows = zs[j].at[_index((*chip, 1 - c))]
                _remote(rows, rows, s2.at[3 * j + q], r2.at[3 * j + q], me).wait_recv()
        for j in range(n):
            block = zs[j].at[0]
            for k in range(4):
                _remote(block, block, s1.at[4 * j + k], r1.at[4 * j + k], me).wait_send()
            for q in range(3):
                _remote(block, block, s2.at[3 * j + q], r2.at[3 * j + q], me).wait_send()

    outs = pl.pallas_call(
        body, name=name, out_shape=tuple(pltpu.HBM(z.shape, z.dtype) for z in zones),
        in_specs=[IN_HBM] * n + [SEMAPHORES] * 4 + [ANY], out_specs=tuple([IN_HBM] * n),
        input_output_aliases={j: j for j in range(n)},
        compiler_params=pltpu.CompilerParams(has_side_effects=IN_FLIGHT),
    )(*zones, send1, recv1, send2, recv2, after)
    return list(outs)


def _scatter_start(parts, name):
    n = len(parts)
    lands = [_in_hbm(lax.empty((3,) + p.shape[1:], p.dtype)) for p in parts]

    def body(*refs):
        ins, zones = refs[:n], refs[n:2 * n]
        send_sems, recv_sems = refs[2 * n:2 * n + 2]
        x, y, c = _position()
        for j in range(n):
            for q, (px, py) in enumerate(_chips(x, y)):
                _remote(ins[j].at[2 * px + py], zones[j].at[q], send_sems.at[3 * j + q], recv_sems.at[3 * j + q],
                        (px, py, c)).start()

    outs = pl.pallas_call(
        body, name=name,
        out_shape=(pltpu.SemaphoreType.DMA((3 * n,)), pltpu.SemaphoreType.DMA((3 * n,)),
                   *[pltpu.HBM(p.shape, p.dtype) for p in parts], *[pltpu.HBM(z.shape, z.dtype) for z in lands]),
        in_specs=[IN_HBM] * (2 * n), out_specs=(SEMAPHORES, SEMAPHORES, *[IN_HBM] * (2 * n)),
        input_output_aliases={j: 2 + j for j in range(2 * n)},
        compiler_params=pltpu.CompilerParams(has_side_effects=IN_FLIGHT),
    )(*[_in_hbm(p) for p in parts], *lands)
    return outs[0], outs[1], list(outs[2:2 + n]), list(outs[2 + n:])


def _scatter_end(parts, zones, send_sems, recv_sems, after, name):
    n = len(parts)

    def body(*refs):
        ins, zs = refs[:n], refs[n:2 * n]
        s, r = refs[2 * n:2 * n + 2]
        me = _position()
        for j in range(n):
            for q in range(3):
                copy = _remote(ins[j].at[0], zs[j].at[q], s.at[3 * j + q], r.at[3 * j + q], me)
                copy.wait_send()
                copy.wait_recv()

    outs = pl.pallas_call(
        body, name=name,
        out_shape=(*[pltpu.HBM(p.shape, p.dtype) for p in parts], *[pltpu.HBM(z.shape, z.dtype) for z in zones]),
        in_specs=[IN_HBM] * (2 * n) + [SEMAPHORES, SEMAPHORES, ANY], out_specs=tuple([IN_HBM] * (2 * n)),
        input_output_aliases={j: j for j in range(2 * n)},
        compiler_params=pltpu.CompilerParams(has_side_effects=IN_FLIGHT),
    )(*parts, *zones, send_sems, recv_sems, after)
    return list(outs[:n]), list(outs[n:])


def _ada_forward(c, ada_w, ada_b):
    def body(c_ref, w_ref, b_ref, cact_ref, mod_ref, gbuf, modrow, send_sems, recv_sems):
        pos = _position()
        me = _index(pos)

        def to_all(ref, round_):
            copies = []
            for k in range(1, NDEV):
                peer, _ = _peer(pos, k)
                copy = pltpu.make_async_remote_copy(
                    src_ref=ref.at[me], dst_ref=ref.at[me], send_sem=send_sems.at[round_, k - 1],
                    recv_sem=recv_sems.at[round_, k - 1], device_id=peer, device_id_type=pl.DeviceIdType.MESH)
                copy.start()
                copies.append(copy)
            for copy in copies:
                copy.wait()

        cact_ref[me] = _silu(c_ref[...])
        to_all(cact_ref, 0)
        rows = lax.broadcasted_iota(jnp.int32, (NDEV, D), 0)
        cact = jnp.zeros((NDEV, D), F32)
        for e in range(NDEV):
            cact = jnp.where(rows == e, cact_ref[e], cact)
        cact = cact.astype(BF16)
        for l in range(DEPTH):
            gbuf[me, l] = jnp.dot(cact, w_ref[l].astype(BF16), preferred_element_type=F32)
        to_all(gbuf, 1)
        mine = lax.broadcasted_iota(jnp.int32, (NDEV, ADA_NC), 0) == me
        for l in range(DEPTH):
            for d in range(NDEV):
                modrow[:, d * ADA_NC:(d + 1) * ADA_NC] = jnp.sum(jnp.where(mine, gbuf[d, l], 0.0), axis=0, keepdims=True)
            full = modrow[...] + b_ref[l:l + 1, :]
            for w in range(3):
                mod_ref[l, w] = full[:, w * D:(w + 1) * D]

    return pl.pallas_call(
        body, name="ada_forward",
        out_shape=[jax.ShapeDtypeStruct((NDEV, 1, D), F32), jax.ShapeDtypeStruct((DEPTH, 3, 1, D), F32)],
        in_specs=[VMEM_FULL] * 3, out_specs=[VMEM_FULL] * 2,
        scratch_shapes=[pltpu.VMEM((NDEV, DEPTH, NDEV, ADA_NC), F32), pltpu.VMEM((1, 3 * D), F32),
                        pltpu.SemaphoreType.DMA((2, NDEV - 1)), pltpu.SemaphoreType.DMA((2, NDEV - 1))],
        compiler_params=_params(),
    )(c, ada_w, ada_b)


def _mod_spec(layer, which, ngrid):
    index = {1: lambda i: (layer, which, 0, 0), 2: lambda i, j: (layer, which, 0, 0)}[ngrid]
    return pl.BlockSpec((None, None, 1, D), index)


def _norm_proj(x, mod, norm_g3, wg, layer, name):
    nb = wg.shape[-1]
    tm = 512

    def body(x_ref, g_ref, shift_ref, scale_ref, w_ref, h_ref, p_ref):
        @pl.when(pl.program_id(1) == 0)
        def _():
            xv = x_ref[...]
            r = lax.rsqrt(jnp.mean(xv * xv, axis=-1, keepdims=True) + EPS)
            hn = xv * r * g_ref[...]
            h_ref[...] = (hn * (1.0 + scale_ref[...]) + shift_ref[...]).astype(BF16)

        p_ref[...] = jnp.dot(h_ref[...], w_ref[...], preferred_element_type=F32).astype(BF16)

    return pl.pallas_call(
        body, name=name, grid=(S // tm, NDEV),
        out_shape=[jax.ShapeDtypeStruct((S, D), BF16), jax.ShapeDtypeStruct((S, NDEV * nb), BF16)],
        in_specs=[pl.BlockSpec((tm, D), lambda i, d: (i, 0)),
                  pl.BlockSpec((None, 1, D), lambda i, d: (layer, 0, 0)),
                  _mod_spec(layer, 0, 2), _mod_spec(layer, 1, 2),
                  pl.BlockSpec((None, D, nb), lambda i, d: (d, 0, 0))],
        out_specs=[pl.BlockSpec((tm, D), lambda i, d: (i, 0)), pl.BlockSpec((tm, nb), lambda i, d: (i, d))],
        compiler_params=_params(("arbitrary", "arbitrary")),
    )(x, norm_g3, mod, mod, wg)


def _out_proj(ycat, wg_out, x, mod, layer, name):
    tm = 512
    rb = wg_out.shape[1]

    def body(y_ref, w_ref, x_ref, gate_ref, xn_ref, o_ref):
        acc = jnp.zeros((tm, D), F32)
        for d in range(NDEV):
            acc = acc + jnp.dot(y_ref[:, d * rb:(d + 1) * rb], w_ref[d], preferred_element_type=F32)
        o_ref[...] = acc.astype(BF16)
        xn_ref[...] = x_ref[...] + gate_ref[...] * acc

    return pl.pallas_call(
        body, name=name, grid=(S // tm,),
        out_shape=[jax.ShapeDtypeStruct((S, D), F32), jax.ShapeDtypeStruct((S, D), BF16)],
        in_specs=[pl.BlockSpec((tm, NDEV * rb), lambda i: (i, 0)),
                  pl.BlockSpec((NDEV, rb, D), lambda i: (0, 0, 0)),
                  pl.BlockSpec((tm, D), lambda i: (i, 0)), _mod_spec(layer, 2, 1)],
        out_specs=[pl.BlockSpec((tm, D), lambda i: (i, 0))] * 2,
        compiler_params=_params(("arbitrary",)),
    )(ycat, wg_out, x, mod)


def _final_loss(x, target, final_g2):
    tm = 256

    def body(x_ref, t_ref, g_ref, dx_ref, loss_ref, dg_ref):
        @pl.when(pl.program_id(0) == 0)
        def _():
            loss_ref[...] = jnp.zeros_like(loss_ref)
            dg_ref[...] = jnp.zeros_like(dg_ref)

        xv, g = x_ref[...], g_ref[...]
        r = lax.rsqrt(jnp.mean(xv * xv, axis=-1, keepdims=True) + EPS)
        xn = xv * r
        err = xn * g - t_ref[...]
        loss_ref[...] += 0.5 * jnp.sum(jnp.mean(err * err, axis=-1, keepdims=True), axis=0, keepdims=True)
        dy = err * (1.0 / D)
        dg_ref[...] += jnp.sum(dy * xn, axis=0, keepdims=True)
        u = dy * g
        dx_ref[...] = r * (u - xn * jnp.mean(xn * u, axis=-1, keepdims=True))

    tile = pl.BlockSpec((tm, D), lambda i: (i, 0))
    row = pl.BlockSpec((1, D), lambda i: (0, 0))
    return pl.pallas_call(
        body, name="final_loss", grid=(S // tm,),
        out_shape=[jax.ShapeDtypeStruct((S, D), F32), jax.ShapeDtypeStruct((1, LANE), F32), jax.ShapeDtypeStruct((1, D), F32)],
        in_specs=[tile, tile, row], out_specs=[tile, pl.BlockSpec((1, LANE), lambda i: (0, 0)), row],
        compiler_params=_params(("arbitrary",)),
    )(x, target, final_g2)


def _dout(dx, out, mod, layer, name):
    tm = 256

    def body(dx_ref, o_ref, gate_ref, do_ref, dgate_ref):
        @pl.when(pl.program_id(0) == 0)
        def _():
            dgate_ref[...] = jnp.zeros_like(dgate_ref)

        dxv = dx_ref[...]
        do_ref[...] = (gate_ref[...] * dxv).astype(BF16)
        dgate_ref[...] += jnp.sum(dxv * o_ref[...].astype(F32), axis=0, keepdims=True)

    tile = pl.BlockSpec((tm, D), lambda i: (i, 0))
    return pl.pallas_call(
        body, name=name, grid=(S // tm,),
        out_shape=[jax.ShapeDtypeStruct((S, D), BF16), jax.ShapeDtypeStruct((1, D), F32)],
        in_specs=[tile, tile, _mod_spec(layer, 2, 1)], out_specs=[tile, pl.BlockSpec((1, D), lambda i: (0, 0))],
        compiler_params=_params(("arbitrary",)),
    )(dx, out, mod)


def _dycat(d_out, wg_out, name):
    tm = 512
    rb = wg_out.shape[1]

    def body(do_ref, w_ref, dy_ref):
        dov = do_ref[...]
        for d in range(NDEV):
            dy_ref[:, d * rb:(d + 1) * rb] = lax.dot_general(
                dov, w_ref[d], (((1,), (1,)), ((), ())), preferred_element_type=F32).astype(BF16)

    return pl.pallas_call(
        body, name=name, grid=(S // tm,), out_shape=jax.ShapeDtypeStruct((S, NDEV * rb), BF16),
        in_specs=[pl.BlockSpec((tm, D), lambda i: (i, 0)), pl.BlockSpec((NDEV, rb, D), lambda i: (0, 0, 0))],
        out_specs=pl.BlockSpec((tm, NDEV * rb), lambda i: (i, 0)),
        compiler_params=_params(("arbitrary",)),
    )(d_out, wg_out)


def _weight_grad(a, b, name):
    a_blocks = a.shape[1] != D
    ka = a.shape[1] // NDEV if a_blocks else a.shape[1]
    nb = b.shape[1] if a_blocks else b.shape[1] // NDEV

    def body(a_ref, b_ref, o_ref):
        o_ref[...] = lax.dot_general(a_ref[...], b_ref[...], (((0,), (0,)), ((), ())),
                                     preferred_element_type=F32).astype(BF16)

    return pl.pallas_call(
        body, name=name, grid=(NDEV,), out_shape=jax.ShapeDtypeStruct((2, NDEV // 2, ka, nb), BF16),
        in_specs=[pl.BlockSpec((S, ka), (lambda d: (0, d)) if a_blocks else (lambda d: (0, 0))),
                  pl.BlockSpec((S, nb), (lambda d: (0, 0)) if a_blocks else (lambda d: (0, d)))],
        out_specs=pl.BlockSpec((None, None, ka, nb), lambda d: (d % 2, d // 2, 0, 0)),
        compiler_params=_params(("arbitrary",)),
    )(a, b)


def _dh(d_proj, wg, carried, name):
    nb = wg.shape[-1]
    tm = 512

    def body(dp_ref, w_ref, carried_ref, dh_ref, carried_out):
        part = lax.dot_general(dp_ref[...], w_ref[...], (((1,), (1,)), ((), ())), preferred_element_type=F32)

        @pl.when(pl.program_id(1) == 0)
        def _():
            dh_ref[...] = part

        @pl.when(pl.program_id(1) != 0)
        def _():
            dh_ref[...] += part

    return pl.pallas_call(
        body, name=name, grid=(S // tm, NDEV),
        out_shape=[jax.ShapeDtypeStruct((S, D), F32), jax.ShapeDtypeStruct(carried.shape, carried.dtype)],
        in_specs=[pl.BlockSpec((tm, nb), lambda i, d: (i, d)), pl.BlockSpec((None, D, nb), lambda i, d: (d, 0, 0)), ANY],
        out_specs=[pl.BlockSpec((tm, D), lambda i, d: (i, 0)), ANY],
        input_output_aliases={2: 1},
        compiler_params=_params(("arbitrary", "arbitrary")),
    )(d_proj, wg, carried)


def _norm_bwd(d_h, x, dx, mod, norm_g3, layer, name):
    tm = 256

    def body(dh_ref, x_ref, dx_ref, g_ref, scale_ref, dxi_ref, dshift_ref, dscale_ref, dg_ref):
        @pl.when(pl.program_id(0) == 0)
        def _():
            dshift_ref[...] = jnp.zeros_like(dshift_ref)
            dscale_ref[...] = jnp.zeros_like(dscale_ref)
            dg_ref[...] = jnp.zeros_like(dg_ref)

        xv, dhv, g = x_ref[...], dh_ref[...], g_ref[...]
        r = lax.rsqrt(jnp.mean(xv * xv, axis=-1, keepdims=True) + EPS)
        xn = xv * r
        dshift_ref[...] += jnp.sum(dhv, axis=0, keepdims=True)
        dscale_ref[...] += jnp.sum(dhv * (xn * g), axis=0, keepdims=True)
        dhn = dhv * (1.0 + scale_ref[...])
        dg_ref[...] += jnp.sum(dhn * xn, axis=0, keepdims=True)
        u = dhn * g
        dxi_ref[...] = dx_ref[...] + r * (u - xn * jnp.mean(xn * u, axis=-1, keepdims=True))

    tile = pl.BlockSpec((tm, D), lambda i: (i, 0))
    row = pl.BlockSpec((1, D), lambda i: (0, 0))
    return pl.pallas_call(
        body, name=name, grid=(S // tm,),
        out_shape=[jax.ShapeDtypeStruct((S, D), F32)] + [jax.ShapeDtypeStruct((1, D), F32)] * 3,
        in_specs=[tile, tile, tile, pl.BlockSpec((None, 1, D), lambda i: (layer, 0, 0)), _mod_spec(layer, 1, 1)],
        out_specs=[tile, row, row, row], compiler_params=_params(("arbitrary",)),
    )(d_h, x, dx, norm_g3, mod)


TS = 256
NCH = TS // CHUNK
HALO_BLOCKS = TS // HALO


def _halo_before(width, col_block):
    return pl.BlockSpec((HALO, width), lambda i: (jnp.maximum(i * HALO_BLOCKS - 1, 0), col_block))


def _halo_after(width, col_block):
    return pl.BlockSpec((HALO, width), lambda i: (jnp.minimum((i + 1) * HALO_BLOCKS, S // HALO - 1), col_block))


def _shift_down(ext, k):
    return pltpu.roll(ext, k, 0)[HALO:]


def _shift_up(ext, k):
    return pltpu.roll(ext, ext.shape[0] - k, 0)[:ext.shape[0] - HALO]


def _layer_norm_head(v, lg, lb):
    mu = jnp.mean(v, axis=-1, keepdims=True)
    vc = v - mu
    rstd = lax.rsqrt(jnp.mean(vc * vc, axis=-1, keepdims=True) + EPS)
    vhat = vc * rstd
    return vhat, rstd, vhat * lg + lb


def _causal_mask():
    return lax.broadcasted_iota(jnp.int32, (CHUNK, CHUNK), 0) >= lax.broadcasted_iota(jnp.int32, (CHUNK, CHUNK), 1)


def _even_mix_fwd(proj, convw, ln_g3, ln_b3, sgu_w, sgu_bcol, wl, name):
    def body(pj_ref, hh_ref, hc_ref, cw_ref, lg_ref, lb_ref, sw_ref, sb_ref, y_ref):
        live = (pl.program_id(0) > 0).astype(F32)
        causal = _causal_mask()
        for j in range(E_A // HEAD):
            cols = slice(j * HEAD, (j + 1) * HEAD)
            w0, w1, w2 = cw_ref[0:1, cols], cw_ref[1:2, cols], cw_ref[2:3, cols]
            lg, lb = lg_ref[:, cols], lb_ref[:, cols]
            wm = jnp.where(causal, sw_ref[j], 0.0).astype(BF16)
            bias = sb_ref[j]

            def split(s, rows, cols=cols):
                return pj_ref[rows, s * E_A + cols.start:s * E_A + cols.stop].astype(F32)

            prev_tail = hc_ref[:, cols].astype(F32) * hh_ref[:, cols].astype(F32) * live
            for n in range(NCH):
                rows = slice(n * CHUNK, (n + 1) * CHUNK)
                p = split(2, rows) * split(0, rows)
                ext = jnp.concatenate([prev_tail, p], axis=0)
                prev_tail = p[CHUNK - HALO:]
                cv = w2 * p + w1 * _shift_down(ext, 1) + w0 * _shift_down(ext, 2)
                y_ref[rows, cols] = (split(1, rows) * cv * _silu(split(3, rows))).astype(BF16)
                _, _, vn = _layer_norm_head(split(5, rows), lg, lb)
                mixed = jnp.dot(wm, vn.astype(BF16), preferred_element_type=F32) + bias
                y_ref[rows, E_A + cols.start:E_A + cols.stop] = (split(4, rows) * mixed * _silu(split(6, rows))).astype(BF16)

    const3 = lambda i: (wl, 0, 0)
    const4 = lambda i: (wl, 0, 0, 0)
    return pl.pallas_call(
        body, name=name, grid=(S // TS,), out_shape=jax.ShapeDtypeStruct((S, 2 * E_A), BF16),
        in_specs=[pl.BlockSpec((TS, 7 * E_A), lambda i: (i, 0)), _halo_before(E_A, 0), _halo_before(E_A, 2),
                  pl.BlockSpec((None, 3, E_A), const3), pl.BlockSpec((None, 1, E_A), const3),
                  pl.BlockSpec((None, 1, E_A), const3), pl.BlockSpec((None, NDEV, CHUNK, CHUNK), const4),
                  pl.BlockSpec((None, NDEV, CHUNK, 1), const4)],
        out_specs=pl.BlockSpec((TS, 2 * E_A), lambda i: (i, 0)),
        compiler_params=_params(("arbitrary",)),
    )(proj, proj, proj, convw, ln_g3, ln_b3, sgu_w, sgu_bcol)


def _even_mix_bwd(proj, d_ycat, convw, ln_g3, ln_b3, sgu_w, sgu_bcol, wl, name):
    nsteps = S // TS

    def body(pj_ref, hh_ref, hc_ref, hb_ref, hz_ref, dy_ref, hdy_ref, cw_ref, lg_ref, lb_ref, sw_ref, sb_ref,
             dp_ref, dcw_ref, dlg_ref, dlb_ref, dsw_ref, dsb_ref):
        step = pl.program_id(0)

        @pl.when(step == 0)
        def _():
            for ref in (dcw_ref, dlg_ref, dlb_ref, dsw_ref, dsb_ref):
                ref[...] = jnp.zeros_like(ref)

        live_before = (step > 0).astype(F32)
        live_after = (step < nsteps - 1).astype(F32)
        causal = _causal_mask()
        for j in range(E_A // HEAD):
            cols = slice(j * HEAD, (j + 1) * HEAD)
            w0, w1, w2 = cw_ref[0:1, cols], cw_ref[1:2, cols], cw_ref[2:3, cols]
            lg, lb = lg_ref[:, cols], lb_ref[:, cols]
            wmf = jnp.where(causal, sw_ref[j], 0.0)
            wm, wmt = wmf.astype(BF16), wmf.T.astype(BF16)
            bias = sb_ref[j]

            def split(s, rows, cols=cols):
                return pj_ref[rows, s * E_A + cols.start:s * E_A + cols.stop].astype(F32)

            def put(s, rows, val, cols=cols):
                dp_ref[rows, s * E_A + cols.start:s * E_A + cols.stop] = val.astype(BF16)

            ps = [split(2, slice(n * CHUNK, (n + 1) * CHUNK)) * split(0, slice(n * CHUNK, (n + 1) * CHUNK)) for n in range(NCH)]
            next_head = (hdy_ref[:, cols].astype(F32) * hb_ref[:, cols].astype(F32) * _silu(hz_ref[:, cols].astype(F32))
                         * live_after)
            acc_w = [jnp.zeros((1, HEAD), F32) for _ in range(3)]
            for n in reversed(range(NCH)):
                rows = slice(n * CHUNK, (n + 1) * CHUNK)
                p = ps[n]
                tail = ps[n - 1][CHUNK - HALO:] if n > 0 else hc_ref[:, cols].astype(F32) * hh_ref[:, cols].astype(F32) * live_before
                ext = jnp.concatenate([tail, p], axis=0)
                p1, p2 = _shift_down(ext, 1), _shift_down(ext, 2)
                cv = w2 * p + w1 * p1 + w0 * p2
                a_b, a_z = split(1, rows), split(3, rows)
                sz, dsz = _silu_and_grad(a_z)
                dya = dy_ref[rows, cols].astype(F32)
                put(1, rows, dya * cv * sz)
                put(3, rows, dya * a_b * cv * dsz)
                gcv = dya * a_b * sz
                acc_w[0] += jnp.sum(gcv * p2, axis=0, keepdims=True)
                acc_w[1] += jnp.sum(gcv * p1, axis=0, keepdims=True)
                acc_w[2] += jnp.sum(gcv * p, axis=0, keepdims=True)
                gext = jnp.concatenate([gcv, next_head], axis=0)
                next_head = gcv[:HALO]
                dpv = w2 * gcv + w1 * _shift_up(gext, 1) + w0 * _shift_up(gext, 2)
                put(2, rows, dpv * split(0, rows))
                put(0, rows, dpv * split(2, rows))
            for k in range(3):
                dcw_ref[k:k + 1, cols] += acc_w[k]

            acc_lg, acc_lb = jnp.zeros((1, HEAD), F32), jnp.zeros((1, HEAD), F32)
            acc_sw, acc_sb = jnp.zeros((CHUNK, CHUNK), F32), jnp.zeros((CHUNK, 1), F32)
            for n in range(NCH):
                rows = slice(n * CHUNK, (n + 1) * CHUNK)
                u, z = split(4, rows), split(6, rows)
                vhat, rstd, vn = _layer_norm_head(split(5, rows), lg, lb)
                vn16 = vn.astype(BF16)
                mixed = jnp.dot(wm, vn16, preferred_element_type=F32) + bias
                sz, dsz = _silu_and_grad(z)
                dyb = dy_ref[rows, E_A + cols.start:E_A + cols.stop].astype(F32)
                put(4, rows, dyb * mixed * sz)
                put(6, rows, dyb * u * mixed * dsz)
                dmix = dyb * u * sz
                dmix16 = dmix.astype(BF16)
                acc_sb += jnp.sum(dmix, axis=1, keepdims=True)
                acc_sw += lax.dot_general(dmix16, vn16, (((1,), (1,)), ((), ())), preferred_element_type=F32)
                dvn = jnp.dot(wmt, dmix16, preferred_element_type=F32)
                acc_lg += jnp.sum(dvn * vhat, axis=0, keepdims=True)
                acc_lb += jnp.sum(dvn, axis=0, keepdims=True)
                dvh = dvn * lg
                put(5, rows, rstd * (dvh - jnp.mean(dvh, axis=-1, keepdims=True)
                                     - vhat * jnp.mean(dvh * vhat, axis=-1, keepdims=True)))
            dlg_ref[:, cols] += acc_lg
            dlb_ref[:, cols] += acc_lb
            dsw_ref[j] += jnp.where(causal, acc_sw, 0.0)
            dsb_ref[j] += acc_sb

    const3 = lambda i: (wl, 0, 0)
    const4 = lambda i: (wl, 0, 0, 0)
    fixed2 = lambda i: (0, 0)
    fixed3 = lambda i: (0, 0, 0)
    return pl.pallas_call(
        body, name=name, grid=(nsteps,),
        out_shape=[jax.ShapeDtypeStruct((S, 7 * E_A), BF16), jax.ShapeDtypeStruct((3, E_A), F32),
                   jax.ShapeDtypeStruct((1, E_A), F32), jax.ShapeDtypeStruct((1, E_A), F32),
                   jax.ShapeDtypeStruct((NDEV, CHUNK, CHUNK), F32), jax.ShapeDtypeStruct((NDEV, CHUNK, 1), F32)],
        in_specs=[pl.BlockSpec((TS, 7 * E_A), lambda i: (i, 0)), _halo_before(E_A, 0), _halo_before(E_A, 2),
                  _halo_after(E_A, 1), _halo_after(E_A, 3),
                  pl.BlockSpec((TS, 2 * E_A), lambda i: (i, 0)), _halo_after(E_A, 0),
                  pl.BlockSpec((None, 3, E_A), const3), pl.BlockSpec((None, 1, E_A), const3),
                  pl.BlockSpec((None, 1, E_A), const3), pl.BlockSpec((None, NDEV, CHUNK, CHUNK), const4),
                  pl.BlockSpec((None, NDEV, CHUNK, 1), const4)],
        out_specs=[pl.BlockSpec((TS, 7 * E_A), lambda i: (i, 0)), pl.BlockSpec((3, E_A), fixed2),
                   pl.BlockSpec((1, E_A), fixed2), pl.BlockSpec((1, E_A), fixed2),
                   pl.BlockSpec((NDEV, CHUNK, CHUNK), fixed3), pl.BlockSpec((NDEV, CHUNK, 1), fixed3)],
        compiler_params=_params(("arbitrary",)),
    )(proj, proj, proj, proj, proj, d_ycat, d_ycat, convw, ln_g3, ln_b3, sgu_w, sgu_bcol)


def _window_count(step, n, win, ext_before):
    rows = CHUNK if ext_before else CHUNK + HALO
    t = step * TS + n * CHUNK + lax.broadcasted_iota(jnp.int32, (rows, 1), 0)
    return jnp.minimum(t + 1, win).astype(F32)


def _pool_weight(wp_ref, g):
    return jnp.concatenate([wp_ref[d, g] for d in range(NDEV)], axis=0)


def _pooled_chunk(p, tail, win, count):
    sums = jnp.concatenate([tail, p], axis=0)
    shift = 1
    while shift < win:
        sums = sums + pltpu.roll(sums, shift, 0)
        shift *= 2
    return sums[HALO:] / count - p


def _pool_mix_fwd(proj, wpool, pscale4, wl, name):
    e_c = 4 * G_C

    def body(pj_ref, hp_ref, wp_ref, ps_ref, y_ref, pooled_scr, yraw_scr):
        step = pl.program_id(0)
        live = (step > 0).astype(F32)
        for g, win in enumerate(POOL_WINDOWS):
            for q in range(G_C // LANE):
                cols = slice(g * G_C + q * LANE, g * G_C + (q + 1) * LANE)
                tail = hp_ref[:, cols].astype(F32) * live
                for n in range(NCH):
                    rows = slice(n * CHUNK, (n + 1) * CHUNK)
                    p = pj_ref[rows, cols].astype(F32)
                    pooled_scr[rows, q * LANE:(q + 1) * LANE] = _pooled_chunk(
                        p, tail, win, _window_count(step, n, win, True)).astype(BF16)
                    tail = p[CHUNK - HALO:]
            yraw_scr[...] = jnp.dot(pooled_scr[...], _pool_weight(wp_ref, g), preferred_element_type=F32)
            for q in range(G_C // LANE):
                cols = slice(g * G_C + q * LANE, g * G_C + (q + 1) * LANE)
                for n in range(NCH):
                    rows = slice(n * CHUNK, (n + 1) * CHUNK)
                    z = pj_ref[rows, e_c + cols.start:e_c + cols.stop].astype(F32)
                    y_ref[rows, cols] = (yraw_scr[rows, q * LANE:(q + 1) * LANE] * ps_ref[:, cols] * _silu(z)).astype(BF16)

    return pl.pallas_call(
        body, name=name, grid=(S // TS,), out_shape=jax.ShapeDtypeStruct((S, e_c), BF16),
        in_specs=[pl.BlockSpec((TS, 2 * e_c), lambda i: (i, 0)), _halo_before(e_c, 0),
                  pl.BlockSpec((NDEV, 4, G_C // NDEV, G_C), lambda i: (0, 0, 0, 0)),
                  pl.BlockSpec((None, 1, e_c), lambda i: (wl, 0, 0))],
        out_specs=pl.BlockSpec((TS, e_c), lambda i: (i, 0)),
        scratch_shapes=[pltpu.VMEM((TS, G_C), BF16), pltpu.VMEM((TS, G_C), F32)],
        compiler_params=_params(("arbitrary",)),
    )(proj, proj, wpool, pscale4)


def _pool_mix_bwd(proj, d_ycat, wpool, pscale4, wl, name):
    e_c = 4 * G_C
    nsteps = S // TS
    rb = G_C // NDEV

    def body(pj_ref, hp_ref, hz_ref, dy_ref, hdy_ref, wp_ref, ps_ref,
             dp_ref, dps_ref, dwp_ref, pooled_scr, yraw_scr, dyraw_scr, dpool_scr, acc_w):
        step = pl.program_id(0)

        @pl.when(step == 0)
        def _():
            dps_ref[...] = jnp.zeros_like(dps_ref)
            acc_w[...] = jnp.zeros_like(acc_w)

        live_before = (step > 0).astype(F32)
        live_after = (step < nsteps - 1).astype(F32)
        for g, win in enumerate(POOL_WINDOWS):
            weight = _pool_weight(wp_ref, g)
            for q in range(G_C // LANE):
                cols = slice(g * G_C + q * LANE, g * G_C + (q + 1) * LANE)
                tail = hp_ref[:, cols].astype(F32) * live_before
                for n in range(NCH):
                    rows = slice(n * CHUNK, (n + 1) * CHUNK)
                    p = pj_ref[rows, cols].astype(F32)
                    pooled_scr[rows, q * LANE:(q + 1) * LANE] = _pooled_chunk(
                        p, tail, win, _window_count(step, n, win, True)).astype(BF16)
                    tail = p[CHUNK - HALO:]
            yraw_scr[...] = jnp.dot(pooled_scr[...], weight, preferred_element_type=F32)
            for q in range(G_C // LANE):
                cols = slice(g * G_C + q * LANE, g * G_C + (q + 1) * LANE)
                local = slice(q * LANE, (q + 1) * LANE)
                scale = ps_ref[:, cols]
                acc_ps = jnp.zeros((1, LANE), F32)
                for n in range(NCH):
                    rows = slice(n * CHUNK, (n + 1) * CHUNK)
                    sz, dsz = _silu_and_grad(pj_ref[rows, e_c + cols.start:e_c + cols.stop].astype(F32))
                    dyv = dy_ref[rows, cols].astype(F32)
                    yraw = yraw_scr[rows, local]
                    dyraw_scr[rows, local] = (dyv * scale * sz).astype(BF16)
                    acc_ps += jnp.sum(dyv * yraw * sz, axis=0, keepdims=True)
                    dp_ref[rows, e_c + cols.start:e_c + cols.stop] = (dyv * yraw * scale * dsz).astype(BF16)
                dps_ref[:, cols] += acc_ps
                dyraw_scr[TS:, local] = (hdy_ref[:, cols].astype(F32) * scale * _silu(hz_ref[:, cols].astype(F32))
                                         * live_after).astype(BF16)
            dpool_scr[...] = lax.dot_general(dyraw_scr[...], weight, (((1,), (1,)), ((), ())), preferred_element_type=F32)
            acc_w[g] += lax.dot_general(pooled_scr[...], dyraw_scr[:TS, :], (((0,), (0,)), ((), ())),
                                        preferred_element_type=F32)
            for q in range(G_C // LANE):
                cols = slice(g * G_C + q * LANE, g * G_C + (q + 1) * LANE)
                local = slice(q * LANE, (q + 1) * LANE)
                for n in range(NCH):
                    rows = slice(n * CHUNK, (n + 1) * CHUNK)
                    ext = dpool_scr[n * CHUNK:(n + 1) * CHUNK + HALO, local]
                    sums = ext / _window_count(step, n, win, False)
                    shift = 1
                    while shift < win:
                        sums = sums + pltpu.roll(sums, CHUNK + HALO - shift, 0)
                        shift *= 2
                    dp_ref[rows, cols] = (sums[:CHUNK] - ext[:CHUNK]).astype(BF16)

        @pl.when(step == nsteps - 1)
        def _():
            for g in range(4):
                for d in range(NDEV):
                    dwp_ref[d % 2, d // 2, g] = acc_w[g, d * rb:(d + 1) * rb, :].astype(BF16)

    in_specs = [pl.BlockSpec((TS, 2 * e_c), lambda i: (i, 0)), _halo_before(e_c, 0), _halo_after(e_c, 1),
                pl.BlockSpec((TS, e_c), lambda i: (i, 0)), _halo_after(e_c, 0),
                pl.BlockSpec((NDEV, 4, rb, G_C), lambda i: (0, 0, 0, 0)),
                pl.BlockSpec((None, 1, e_c), lambda i: (wl, 0, 0))]
    args = [proj, proj, proj, d_ycat, d_ycat, wpool, pscale4]
    return pl.pallas_call(
        body, name=name, grid=(nsteps,),
        out_shape=[jax.ShapeDtypeStruct((S, 2 * e_c), BF16), jax.ShapeDtypeStruct((1, e_c), F32),
                   jax.ShapeDtypeStruct((2, NDEV // 2) + wpool.shape[1:], BF16)],
        in_specs=in_specs,
        out_specs=[pl.BlockSpec((TS, 2 * e_c), lambda i: (i, 0)), pl.BlockSpec((1, e_c), lambda i: (0, 0)),
                   pl.BlockSpec((2, NDEV // 2, 4, rb, G_C), lambda i: (0, 0, 0, 0, 0))],
        scratch_shapes=[pltpu.VMEM((TS, G_C), BF16), pltpu.VMEM((TS, G_C), F32), pltpu.VMEM((TS + HALO, G_C), BF16),
                        pltpu.VMEM((TS + HALO, G_C), F32), pltpu.VMEM((4, G_C, G_C), F32)],
        compiler_params=_params(("arbitrary",)),
    )(*args)


def _adamw(w, g, m, v):
    m = ADAM_B1 * m + (1.0 - ADAM_B1) * g
    v = ADAM_B2 * v + (1.0 - ADAM_B2) * jnp.square(g)
    m_hat = m / (1.0 - ADAM_B1 ** ADAM_STEP)
    v_hat = v / (1.0 - ADAM_B2 ** ADAM_STEP)
    delta = -ADAM_LR * (m_hat / (jnp.sqrt(v_hat) + ADAM_EPS) + ADAM_WD * w)
    return delta, m, v


def _adam_sharded(w, m, v, chip_parts, landed, my_chip, name):
    nl, nr, ncol = w.shape
    tr = 128
    steps = nr // tr

    def body(chip_ref, w_ref, m_ref, v_ref, *rest):
        parts, zones = rest[:nl], rest[nl:2 * nl]
        g_ref, d_ref, nm_ref, nv_ref = rest[2 * nl:]
        layer = pl.program_id(0)
        g = jnp.zeros((tr, ncol), F32)
        for l in range(nl):
            gl = parts[l][...].astype(F32)
            for q in range(3):
                gl = gl + zones[l][q].astype(F32)
            g = jnp.where(layer == l, gl, g)
        g_ref[...] = g
        d_ref[...], nm_ref[...], nv_ref[...] = _adamw(w_ref[...], g, m_ref[...], v_ref[...])

    def rows_of(l):
        return lambda layer, i, chip_ref: jnp.where(layer == l, i, jnp.where(layer < l, 0, steps - 1))

    spec = pl.BlockSpec((None, tr, ncol), lambda layer, i, chip_ref: (layer, i, 0))
    part_specs = [pl.BlockSpec((None, tr, ncol), lambda layer, i, chip_ref, l=l: (chip_ref[0], rows_of(l)(layer, i, chip_ref), 0))
                  for l in range(nl)]
    zone_specs = [pl.BlockSpec((3, tr, ncol), lambda layer, i, chip_ref, l=l: (0, rows_of(l)(layer, i, chip_ref), 0))
                  for l in range(nl)]
    grid_spec = pltpu.PrefetchScalarGridSpec(
        num_scalar_prefetch=1, grid=(nl, steps), in_specs=[spec, spec, spec] + part_specs + zone_specs, out_specs=[spec] * 4)
    return pl.pallas_call(
        body, name=name, grid_spec=grid_spec, out_shape=[jax.ShapeDtypeStruct(w.shape, F32)] * 4,
        compiler_params=_params(("arbitrary", "arbitrary")),
    )(my_chip, w, m, v, *chip_parts, *landed)


def _adam_small(w, g, m, v, name):
    def body(w_ref, g_ref, m_ref, v_ref, d_ref, nm_ref, nv_ref):
        d_ref[...], nm_ref[...], nv_ref[...] = _adamw(w_ref[...], g_ref[...], m_ref[...], v_ref[...])

    return pl.pallas_call(body, name=name, out_shape=[jax.ShapeDtypeStruct(w.shape, F32)] * 3,
                          in_specs=[VMEM_FULL] * 4, out_specs=[VMEM_FULL] * 3, compiler_params=_params())(w, g, m, v)


def _sum_devices(gathered, name):
    _, nr, ncol = gathered.shape

    def body(g_ref, o_ref):
        acc = g_ref[0]
        for s in range(1, NDEV):
            acc = acc + g_ref[s]
        o_ref[...] = acc

    return pl.pallas_call(body, name=name, grid=(1,), out_shape=jax.ShapeDtypeStruct((nr, ncol), F32),
                          in_specs=[pl.BlockSpec((NDEV, nr, ncol), lambda i: (0, 0, 0))],
                          out_specs=pl.BlockSpec((nr, ncol), lambda i: (0, 0)),
                          compiler_params=_params(("arbitrary",)))(gathered)


def _ada_weight_adam(cact_t, dmod_mine, w, m, v):
    def body(ct_ref, dm_ref, w_ref, m_ref, v_ref, g_ref, d_ref, nm_ref, nv_ref):
        ct, dm = ct_ref[...], dm_ref[...]
        g = ct[:, 0:1] * dm[0:1, :]
        for e in range(1, NDEV):
            g = g + ct[:, e:e + 1] * dm[e:e + 1, :]
        g_ref[...] = g
        d_ref[...], nm_ref[...], nv_ref[...] = _adamw(w_ref[...], g, m_ref[...], v_ref[...])

    spec = pl.BlockSpec((None, D, ADA_NC), lambda l: (l, 0, 0))
    return pl.pallas_call(
        body, name="ada_weight_adam", grid=(DEPTH,), out_shape=[jax.ShapeDtypeStruct(w.shape, F32)] * 4,
        in_specs=[pl.BlockSpec((D, NDEV), lambda l: (0, 0)), pl.BlockSpec((None, NDEV, ADA_NC), lambda l: (l, 0, 0)),
                  spec, spec, spec],
        out_specs=[spec] * 4, compiler_params=_params(("arbitrary",)),
    )(cact_t, dmod_mine, w, m, v)


def _pad_rows(a, rows):
    a = a.reshape(-1, D)
    return jnp.pad(a, ((0, rows - a.shape[0]), (0, 0)))


def kernel(x, c, norm_g, ada_w, ada_b, ab_w_in, ab_conv_w, ab_ln_g, ab_ln_b, ab_sgu_w, ab_sgu_b, ab_w_out, c_w_in, c_pool_w, c_pool_scale, c_w_out, final_g, loss_target, m_norm_g, m_ada_w, m_ada_b, m_ab_w_in, m_ab_conv_w, m_ab_ln_g, m_ab_ln_b, m_ab_sgu_w, m_ab_sgu_b, m_ab_w_out, m_c_w_in, m_c_pool_w, m_c_pool_scale, m_c_w_out, m_final_g, v_norm_g, v_ada_w, v_ada_b, v_ab_w_in, v_ab_conv_w, v_ab_ln_g, v_ab_ln_b, v_ab_sgu_w, v_ab_sgu_b, v_ab_w_out, v_c_w_in, v_c_pool_w, v_c_pool_scale, v_c_w_out, v_final_g):
    x_pos, y_pos, c_pos = _position()
    me = _index((x_pos, y_pos, c_pos))
    core = c_pos.astype(jnp.int32).reshape(1)
    my_chip = (2 * x_pos + y_pos).astype(jnp.int32).reshape(1)
    me1 = me.astype(jnp.int32).reshape(1)
    x0 = x.reshape(S, D)
    target = loss_target.reshape(S, D)
    norm_g3 = norm_g.reshape(DEPTH, 1, D)
    ln_g3, ln_b3 = ab_ln_g.reshape(2, 1, E_A), ab_ln_b.reshape(2, 1, E_A)
    sgu_bcol = ab_sgu_b.reshape(2, NDEV, CHUNK, 1)
    rb = G_C // NDEV
    pool_w3, m_pool_w3, v_pool_w3 = (a.reshape(2, 4 * rb, G_C) for a in (c_pool_w, m_c_pool_w, v_c_pool_w))

    cact_all, mod = _ada_forward(c, ada_w, ada_b)
    convw_all, pscale_all = _gather([ab_conv_w, c_pool_scale], "gather_small_weights")
    convw = jnp.transpose(convw_all, (1, 2, 0, 3)).reshape(2, 3, E_A)
    pscale4 = jnp.transpose(pscale_all, (1, 0, 2)).reshape(2, 1, 4 * G_C)
    flights, after = [], [convw_all]
    for layer in range(DEPTH):
        wl = layer // 2
        if layer % 2 == 0:
            zones = [_to_zone(ab_w_in, wl, me1, BF16, f"cast_w_in_{layer}"), _to_zone(ab_w_out, wl, me1, BF16, f"cast_w_out_{layer}")]
        else:
            zones = [_to_zone(c_w_in, wl, me1, BF16, f"cast_w_in_{layer}"), _to_zone(c_w_out, wl, me1, BF16, f"cast_w_out_{layer}"),
                     _to_zone(pool_w3, wl, me1, BF16, f"cast_pool_w_{layer}")]
        send1, recv1, zones, (mod,) = _gather_start(zones, after, [mod], f"gather_start_{layer}")
        flights.append((send1, recv1, zones))
        after = []

    def finish_gather(layer, passed, after):
        send1, recv1, _ = flights[layer]
        send2, recv2, zones = passed
        wg = _gather_end(zones, send1, recv1, send2, recv2, after, f"gather_end_{layer}")
        return wg[:2] + [w.reshape(NDEV, 4, rb, G_C) for w in wg[2:]]

    xs, hs, projs, ycats, outs, gathered_w = [x0], [], [], [], [], []
    gathered_w.append(finish_gather(0, _gather_mid(flights[0][2], flights[0][1], mod, "gather_mid_0"), mod))
    for layer in range(DEPTH):
        wl = layer // 2
        even = layer % 2 == 0
        wg = gathered_w[layer]
        h, proj = _norm_proj(xs[-1], mod, norm_g3, wg[0], layer, f"norm_proj_{layer}")
        if layer + 1 < DEPTH:
            passed = _gather_mid(flights[layer + 1][2], flights[layer + 1][1], h, f"gather_mid_{layer + 1}")
        if even:
            ycat = _even_mix_fwd(proj, convw, ln_g3, ln_b3, ab_sgu_w, sgu_bcol, wl, f"even_mix_fwd_{layer}")
        else:
            ycat = _pool_mix_fwd(proj, wg[2], pscale4, wl, f"pool_mix_fwd_{layer}")
        x_new, out = _out_proj(ycat, wg[1], xs[-1], mod, layer, f"out_proj_{layer}")
        if layer + 1 < DEPTH:
            gathered_w.append(finish_gather(layer + 1, passed, x_new))
        xs.append(x_new)
        hs.append(h)
        projs.append(proj)
        ycats.append(ycat)
        outs.append(out)

    dx, loss_part, d_final_g = _final_loss(xs[DEPTH], target, final_g.reshape(1, D))
    loss = lax.psum(loss_part[0, 0], ("x", "y", "c"))

    d_mod, d_norm_g = [None] * DEPTH, [None] * DEPTH
    small, scatters = {}, {}
    for layer in reversed(range(DEPTH)):
        wl = layer // 2
        even = layer % 2 == 0
        wg = gathered_w[layer]
        d_out, d_gate = _dout(dx, outs[layer], mod, layer, f"dout_{layer}")
        d_ycat = _dycat(d_out, wg[1], f"dycat_{layer}")
        parts = [None, _weight_grad(ycats[layer], d_out, f"grad_w_out_{layer}")]
        if even:
            d_proj, d_cw, d_lg, d_lb, d_sw, d_sb = _even_mix_bwd(
                projs[layer], d_ycat, convw, ln_g3, ln_b3, ab_sgu_w, sgu_bcol, wl, f"even_mix_bwd_{layer}")
            small[layer] = (d_cw, d_lg, d_lb, d_sw, d_sb)
        else:
            d_proj, d_ps, d_pool = _pool_mix_bwd(projs[layer], d_ycat, wg[2], pscale4, wl, f"pool_mix_bwd_{layer}")
            small[layer] = (d_ps,)
            parts.append(d_pool)
        parts[0] = _weight_grad(hs[layer], d_proj, f"grad_w_in_{layer}")
        from_sibling = _pair_exchange(parts, f"pair_exchange_{layer}")
        chip_parts = [_pair_sum(p, q, core, f"pair_sum_{layer}_{j}") for j, (p, q) in enumerate(zip(parts, from_sibling))]
        send_sems, recv_sems, chip_parts, zones = _scatter_start(chip_parts, f"scatter_start_{layer}")
        d_h, chip_parts[0] = _dh(d_proj, wg[0], chip_parts[0], f"dh_{layer}")
        scatters[layer] = (send_sems, recv_sems, chip_parts, zones)
        dx, d_shift, d_scale, d_norm_g[layer] = _norm_bwd(d_h, xs[layer], dx, mod, norm_g3, layer, f"norm_bwd_{layer}")
        d_mod[layer] = jnp.concatenate([d_shift, d_scale, d_gate], axis=0)

    sections = [("norm_g", jnp.concatenate(d_norm_g, axis=0), 8),
                ("d_mod", jnp.concatenate(d_mod, axis=0), 16),
                ("ab_ln_g", jnp.concatenate([small[0][1], small[2][1]], axis=0), 8),
                ("ab_ln_b", jnp.concatenate([small[0][2], small[2][2]], axis=0), 8),
                ("ab_sgu_b", jnp.stack([small[0][4], small[2][4]]), 8),
                ("final_g", d_final_g, 8),
                ("ab_conv_w", jnp.stack([small[0][0], small[2][0]]), 8),
                ("c_pool_scale", jnp.concatenate([small[1][0], small[3][0]], axis=0), 8),
                ("ab_sgu_w", jnp.stack([small[0][3], small[2][3]]), 256)]
    offsets, at = {}, 0
    for name, _, rows in sections:
        offsets[name] = (at, rows)
        at += rows
    packed = jnp.concatenate([_pad_rows(a, rows) for _, a, rows in sections], axis=0)
    small_zone = _to_zone(packed[None], 0, me1, F32, "place_small_grads")
    small_send1, small_recv1, small_zone, (dx,) = _gather_start([small_zone], [], [dx], "gather_small_start")
    grad_x = dx.reshape(x.shape)

    landed = {}
    for layer in range(DEPTH):
        send_sems, recv_sems, chip_parts, zones = scatters[layer]
        landed[layer] = _scatter_end(chip_parts, zones, send_sems, recv_sems, dx, f"scatter_end_{layer}")

    def flat(a):
        return a.reshape(a.shape[0], -1, a.shape[-1])

    res = {}
    for k, j, layers, (w, m, v) in [
            ("ab_w_in", 0, (0, 2), (ab_w_in, m_ab_w_in, v_ab_w_in)), ("ab_w_out", 1, (0, 2), (ab_w_out, m_ab_w_out, v_ab_w_out)),
            ("c_w_in", 0, (1, 3), (c_w_in, m_c_w_in, v_c_w_in)), ("c_w_out", 1, (1, 3), (c_w_out, m_c_w_out, v_c_w_out)),
            ("c_pool_w", 2, (1, 3), (pool_w3, m_pool_w3, v_pool_w3))]:
        outs4 = _adam_sharded(w, m, v, [flat(landed[l][0][j]) for l in layers], [flat(landed[l][1][j]) for l in layers],
                              my_chip, "adam_" + k)
        res[k] = [o.reshape(c_pool_w.shape) if k == "c_pool_w" else o for o in outs4]

    last = res["c_pool_w"][0]
    small_send2, small_recv2, small_zone = _gather_mid(small_zone, small_recv1, last, "gather_small_mid")
    gathered = _gather_end(small_zone, small_send1, small_recv1, small_send2, small_recv2, last, "gather_small_end")[0]
    summed = _sum_devices(gathered, "sum_small_grads")

    def section(name, nrows, src=summed):
        start = offsets[name][0]
        return src[..., start:start + nrows, :]

    grads = {
        "norm_g": section("norm_g", DEPTH),
        "ada_b": section("d_mod", 3 * DEPTH).reshape(DEPTH, 3 * D),
        "ab_ln_g": section("ab_ln_g", 2), "ab_ln_b": section("ab_ln_b", 2),
        "ab_sgu_b": section("ab_sgu_b", 2).reshape(ab_sgu_b.shape),
        "final_g": section("final_g", 1),
        "ab_sgu_w": section("ab_sgu_w", 256).reshape(ab_sgu_w.shape),
        "ab_conv_w": lax.dynamic_slice_in_dim(section("ab_conv_w", 6).reshape(2, 3, E_A), me * HEAD, HEAD, axis=2),
        "c_pool_scale": lax.dynamic_slice_in_dim(section("c_pool_scale", 4).reshape(2, 4 * G_C), me * 256, 256, axis=1),
    }
    small_w = {"norm_g": (norm_g, m_norm_g, v_norm_g), "ada_b": (ada_b, m_ada_b, v_ada_b),
               "ab_ln_g": (ab_ln_g, m_ab_ln_g, v_ab_ln_g), "ab_ln_b": (ab_ln_b, m_ab_ln_b, v_ab_ln_b),
               "ab_sgu_b": (ab_sgu_b, m_ab_sgu_b, v_ab_sgu_b),
               "final_g": (final_g.reshape(1, D), m_final_g.reshape(1, D), v_final_g.reshape(1, D)),
               "ab_sgu_w": (ab_sgu_w, m_ab_sgu_w, v_ab_sgu_w), "ab_conv_w": (ab_conv_w, m_ab_conv_w, v_ab_conv_w),
               "c_pool_scale": (c_pool_scale, m_c_pool_scale, v_c_pool_scale)}
    for k, (w, m, v) in small_w.items():
        res[k] = [grads[k]] + list(_adam_small(w, grads[k], m, v, "adam_" + k))
    res["final_g"] = [a.reshape(D) for a in res["final_g"]]

    dmod_all = section("d_mod", 3 * DEPTH, gathered).reshape(NDEV, DEPTH, 3 * D)
    dmod_mine = jnp.transpose(lax.dynamic_slice_in_dim(dmod_all, me * ADA_NC, ADA_NC, axis=2), (1, 0, 2))
    res["ada_w"] = _ada_weight_adam(jnp.transpose(cact_all.reshape(NDEV, D)), dmod_mine, ada_w, m_ada_w, v_ada_w)

    order = ["norm_g", "ada_w", "ada_b", "ab_w_in", "ab_conv_w", "ab_ln_g", "ab_ln_b", "ab_sgu_w", "ab_sgu_b",
             "ab_w_out", "c_w_in", "c_pool_w", "c_pool_scale", "c_w_out", "final_g"]
    return (loss, grad_x, *[res[k][0] for k in order], *[res[k][1] for k in order],
            *[res[k][2] for k in order], *[res[k][3] for k in order])
```

```python
import jax
import jax.numpy as jnp
from jax import lax
from jax.experimental import pallas as pl
from jax.experimental.pallas import tpu as pltpu

F32, BF16 = jnp.float32, jnp.bfloat16
S, D = 2048, 1024
NDEV = 8
DEPTH = 4
EPS = 1e-6
E_A = 1024
HEAD = 128
CHUNK = 128
POOL_WINDOWS = (2, 4, 8, 16)
G_C = 512
HALO = 16
ADA_NC = 384
MIB = 1024 * 1024
LANE = 128

ADAM_LR, ADAM_B1, ADAM_B2, ADAM_EPS, ADAM_WD, ADAM_STEP = 0.001, 0.9, 0.999, 1e-08, 0.01, 10

ANY = pl.BlockSpec(memory_space=pl.ANY)
VMEM_FULL = pl.BlockSpec(memory_space=pltpu.VMEM)
IN_HBM = pl.BlockSpec(memory_space=pltpu.HBM)
SEMAPHORES = pl.BlockSpec(memory_space=pltpu.SEMAPHORE)
IN_FLIGHT = pltpu.SideEffectType.DATAFLOW_SIDE_EFFECTING


V7X_VMEM_MIB = 64
VMEM_LIMIT_MIB = V7X_VMEM_MIB - 4


def _params(semantics=None):
    return pltpu.CompilerParams(dimension_semantics=semantics, vmem_limit_bytes=VMEM_LIMIT_MIB * MIB)


def _silu(z):
    return z * jax.nn.sigmoid(z)


def _silu_and_grad(z):
    sig = jax.nn.sigmoid(z)
    return z * sig, sig * (1.0 + z * (1.0 - sig))


def _position():
    return lax.axis_index("x"), lax.axis_index("y"), lax.axis_index("c")


def _index(pos):
    return 4 * pos[0] + 2 * pos[1] + pos[2]


def _peer(pos, k):
    flipped = tuple(1 - p if (k >> (2 - b)) & 1 else p for b, p in enumerate(pos))
    return flipped, _index(flipped)


def _remote(src, dst, send_sem, recv_sem, device):
    return pltpu.make_async_remote_copy(src_ref=src, dst_ref=dst, send_sem=send_sem, recv_sem=recv_sem,
                                        device_id=device, device_id_type=pl.DeviceIdType.MESH)


def _gather(arrays, name):
    n = len(arrays)
    out_shape = [jax.ShapeDtypeStruct((NDEV,) + a.shape, a.dtype) for a in arrays]

    def body(*refs):
        ins, outs = refs[:n], refs[n:2 * n]
        send_sems, recv_sems, own_sems = refs[2 * n:]
        x, y, c = _position()
        me = _index((x, y, c))
        sibling = (x, y, 1 - c)
        chips = [(1 - x, y), (x, 1 - y), (1 - x, 1 - y)]

        def block_copy(j, k, owner, to, src=None):
            rows = outs[j].at[_index(owner)]
            return _remote(rows if src is None else src, rows, send_sems.at[j, k], recv_sems.at[j, k], to)

        own, first, passed = [], [], []
        for j in range(n):
            own.append(pltpu.make_async_copy(ins[j], outs[j].at[me], own_sems.at[j]))
            first.append(block_copy(j, 0, (x, y, c), sibling, src=ins[j]))
            first += [block_copy(j, 1 + q, (x, y, c), (*chip, c), src=ins[j]) for q, chip in enumerate(chips)]
        for copy in own + first:
            copy.start()
        for q, chip in enumerate(chips):
            for j in range(n):
                block_copy(j, 1 + q, (*chip, c), (x, y, c)).wait_recv()
                forward = block_copy(j, 4 + q, (*chip, c), sibling)
                forward.start()
                passed.append(forward)
        for j in range(n):
            block_copy(j, 0, sibling, (x, y, c)).wait_recv()
            for q, chip in enumerate(chips):
                block_copy(j, 4 + q, (*chip, 1 - c), (x, y, c)).wait_recv()
        for copy in first + passed:
            copy.wait_send()
        for copy in own:
            copy.wait()

    return pl.pallas_call(
        body, name=name, out_shape=out_shape, in_specs=[ANY] * n, out_specs=[ANY] * n,
        scratch_shapes=[pltpu.SemaphoreType.DMA((n, NDEV - 1)), pltpu.SemaphoreType.DMA((n, NDEV - 1)),
                        pltpu.SemaphoreType.DMA((n,))],
    )(*arrays)


def _pair_start(parts, name):
    n = len(parts)
    lands = [_in_hbm(lax.empty(p.shape[1:], p.dtype)) for p in parts]

    def body(*refs):
        ins, zones = refs[:n], refs[n:2 * n]
        send_sems, recv_sems = refs[2 * n:2 * n + 2]
        x, y, c = _position()
        for j in range(n):
            _remote(ins[j].at[1 - c], zones[j], send_sems.at[j], recv_sems.at[j], (x, y, 1 - c)).start()

    outs = pl.pallas_call(
        body, name=name,
        out_shape=(pltpu.SemaphoreType.DMA((n,)), pltpu.SemaphoreType.DMA((n,)),
                   *[pltpu.HBM(p.shape, p.dtype) for p in parts], *[pltpu.HBM(z.shape, z.dtype) for z in lands]),
        in_specs=[IN_HBM] * (2 * n), out_specs=(SEMAPHORES, SEMAPHORES, *[IN_HBM] * (2 * n)),
        input_output_aliases={j: 2 + j for j in range(2 * n)},
        compiler_params=pltpu.CompilerParams(has_side_effects=IN_FLIGHT),
    )(*[_in_hbm(p) for p in parts], *lands)
    return outs[0], outs[1], list(outs[2:2 + n]), list(outs[2 + n:])


def _pair_end(parts, zones, send_sems, recv_sems, after, name):
    n = len(parts)

    def body(*refs):
        ins, zs = refs[:n], refs[n:2 * n]
        s, r = refs[2 * n:2 * n + 2]
        me = _position()
        for j in range(n):
            copy = _remote(ins[j].at[0], zs[j], s.at[j], r.at[j], me)
            copy.wait_send()
            copy.wait_recv()

    outs = pl.pallas_call(
        body, name=name,
        out_shape=(*[pltpu.HBM(p.shape, p.dtype) for p in parts], *[pltpu.HBM(z.shape, z.dtype) for z in zones]),
        in_specs=[IN_HBM] * (2 * n) + [SEMAPHORES, SEMAPHORES, ANY], out_specs=tuple([IN_HBM] * (2 * n)),
        input_output_aliases={j: j for j in range(2 * n)},
        compiler_params=pltpu.CompilerParams(has_side_effects=IN_FLIGHT),
    )(*parts, *zones, send_sems, recv_sems, after)
    return list(outs[:n]), list(outs[n:])


def _pair_sum(part, from_sibling, core, name):
    ncol = part.shape[-1]
    p3 = part.reshape(2, -1, ncol)
    q2 = from_sibling.reshape(-1, ncol)
    nrows = q2.shape[0]
    tr = 512

    def body(core_ref, p_ref, q_ref, o_ref):
        o_ref[...] = (p_ref[...].astype(F32) + q_ref[...].astype(F32)).astype(BF16)

    grid_spec = pltpu.PrefetchScalarGridSpec(
        num_scalar_prefetch=1, grid=(nrows // tr,),
        in_specs=[pl.BlockSpec((None, tr, ncol), lambda i, core_ref: (core_ref[0], i, 0)),
                  pl.BlockSpec((tr, ncol), lambda i, core_ref: (i, 0))],
        out_specs=pl.BlockSpec((tr, ncol), lambda i, core_ref: (i, 0)))
    out = pl.pallas_call(body, name=name, grid_spec=grid_spec, out_shape=jax.ShapeDtypeStruct(q2.shape, BF16),
                         compiler_params=_params(("arbitrary",)))(core, p3, q2)
    return out.reshape(from_sibling.shape)


def _in_hbm(a):
    return pltpu.with_memory_space_constraint(a, pltpu.HBM)


def _chips(x, y):
    return [(1 - x, y), (x, 1 - y), (1 - x, 1 - y)]


def _to_zone(a, wl, me, dtype, name):
    _, rows, cols = a.shape
    tr = 256 if rows % 256 == 0 else rows

    def body(me_ref, a_ref, o_ref):
        o_ref[...] = a_ref[...].astype(dtype)

    grid_spec = pltpu.PrefetchScalarGridSpec(
        num_scalar_prefetch=1, grid=(rows // tr,),
        in_specs=[pl.BlockSpec((None, tr, cols), lambda i, me_ref: (wl, i, 0))],
        out_specs=pl.BlockSpec((None, tr, cols), lambda i, me_ref: (me_ref[0], i, 0)))
    return pl.pallas_call(body, name=name, grid_spec=grid_spec, out_shape=jax.ShapeDtypeStruct((NDEV, rows, cols), dtype),
                          compiler_params=_params(("arbitrary",)))(me, a)


def _gather_start(zones, after, carried, name):
    n, m = len(zones), len(carried)

    def body(*refs):
        zs = refs[:n]
        send_sems, recv_sems = refs[n + m + len(after):n + m + len(after) + 2]
        x, y, c = _position()
        me = _index((x, y, c))
        for j in range(n):
            mine = zs[j].at[me]
            _remote(mine, mine, send_sems.at[4 * j], recv_sems.at[4 * j], (x, y, 1 - c)).start()
            for q, chip in enumerate(_chips(x, y)):
                _remote(mine, mine, send_sems.at[4 * j + 1 + q], recv_sems.at[4 * j + 1 + q], (*chip, c)).start()

    outs = pl.pallas_call(
        body, name=name,
        out_shape=(pltpu.SemaphoreType.DMA((4 * n,)), pltpu.SemaphoreType.DMA((4 * n,)),
                   *[pltpu.HBM(z.shape, z.dtype) for z in zones], *[jax.ShapeDtypeStruct(a.shape, a.dtype) for a in carried]),
        in_specs=[IN_HBM] * n + [ANY] * (m + len(after)),
        out_specs=(SEMAPHORES, SEMAPHORES, *[IN_HBM] * n, *[ANY] * m),
        input_output_aliases={j: 2 + j for j in range(n + m)},
        compiler_params=pltpu.CompilerParams(has_side_effects=IN_FLIGHT),
    )(*[_in_hbm(z) for z in zones], *carried, *after)
    return outs[0], outs[1], list(outs[2:2 + n]), list(outs[2 + n:])


def _gather_mid(zones, recv_sems, after, name):
    n = len(zones)

    def body(*refs):
        zs, first_recv = refs[:n], refs[n]
        send_sems, recv_sems2 = refs[n + 2:n + 4]
        x, y, c = _position()
        for q, chip in enumerate(_chips(x, y)):
            for j in range(n):
                rows = zs[j].at[_index((*chip, c))]
                _remote(rows, rows, send_sems.at[3 * j + q], first_recv.at[4 * j + 1 + q], (x, y, c)).wait_recv()
                _remote(rows, rows, send_sems.at[3 * j + q], recv_sems2.at[3 * j + q], (x, y, 1 - c)).start()

    outs = pl.pallas_call(
        body, name=name,
        out_shape=(pltpu.SemaphoreType.DMA((3 * n,)), pltpu.SemaphoreType.DMA((3 * n,)),
                   *[pltpu.HBM(z.shape, z.dtype) for z in zones]),
        in_specs=[IN_HBM] * n + [SEMAPHORES, ANY], out_specs=(SEMAPHORES, SEMAPHORES, *[IN_HBM] * n),
        input_output_aliases={j: 2 + j for j in range(n)},
        compiler_params=pltpu.CompilerParams(has_side_effects=IN_FLIGHT),
    )(*zones, recv_sems, after)
    return outs[0], outs[1], list(outs[2:])


def _gather_end(zones, send1, recv1, send2, recv2, after, name):
    n = len(zones)

    def body(*refs):
        zs = refs[:n]
        s1, r1, s2, r2 = refs[n:n + 4]
        x, y, c = _position()
        me = (x, y, c)
        for j in range(n):
            rows = zs[j].at[_index((x, y, 1 - c))]
            _remote(rows, rows, s1.at[4 * j], r1.at[4 * j], me).wait_recv()
            for q, chip in enumerate(_chips(x, y)):
                rows = zs[j].at[_index((*chip, 1 - c))]
                _remote(rows, rows, s2.at[3 * j + q], r2.at[3 * j + q], me).wait_recv()
        for j in range(n):
            block = zs[j].at[0]
            for k in range(4):
                _remote(block, block, s1.at[4 * j + k], r1.at[4 * j + k], me).wait_send()
            for q in range(3):
                _remote(block, block, s2.at[3 * j + q], r2.at[3 * j + q], me).wait_send()

    outs = pl.pallas_call(
        body, name=name, out_shape=tuple(pltpu.HBM(z.shape, z.dtype) for z in zones),
        in_specs=[IN_HBM] * n + [SEMAPHORES] * 4 + [ANY], out_specs=tuple([IN_HBM] * n),
        input_output_aliases={j: j for j in range(n)},
        compiler_params=pltpu.CompilerParams(has_side_effects=IN_FLIGHT),
    )(*zones, send1, recv1, send2, recv2, after)
    return list(outs)


def _scatter_start(parts, name):
    n = len(parts)
    lands = [_in_hbm(lax.empty((3,) + p.shape[1:], p.dtype)) for p in parts]

    def body(*refs):
        ins, zones = refs[:n], refs[n:2 * n]
        send_sems, recv_sems = refs[2 * n:2 * n + 2]
        x, y, c = _position()
        for j in range(n):
            for q, (px, py) in enumerate(_chips(x, y)):
                _remote(ins[j].at[2 * px + py], zones[j].at[q], send_sems.at[3 * j + q], recv_sems.at[3 * j + q],
                        (px, py, c)).start()

    outs = pl.pallas_call(
        body, name=name,
        out_shape=(pltpu.SemaphoreType.DMA((3 * n,)), pltpu.SemaphoreType.DMA((3 * n,)),
                   *[pltpu.HBM(p.shape, p.dtype) for p in parts], *[pltpu.HBM(z.shape, z.dtype) for z in lands]),
        in_specs=[IN_HBM] * (2 * n), out_specs=(SEMAPHORES, SEMAPHORES, *[IN_HBM] * (2 * n)),
        input_output_aliases={j: 2 + j for j in range(2 * n)},
        compiler_params=pltpu.CompilerParams(has_side_effects=IN_FLIGHT),
    )(*[_in_hbm(p) for p in parts], *lands)
    return outs[0], outs[1], list(outs[2:2 + n]), list(outs[2 + n:])


def _scatter_end(parts, zones, send_sems, recv_sems, after, name):
    n = len(parts)

    def body(*refs):
        ins, zs = refs[:n], refs[n:2 * n]
        s, r = refs[2 * n:2 * n + 2]
        me = _position()
        for j in range(n):
            for q in range(3):
                copy = _remote(ins[j].at[0], zs[j].at[q], s.at[3 * j + q], r.at[3 * j + q], me)
                copy.wait_send()
                copy.wait_recv()

    outs = pl.pallas_call(
        body, name=name,
        out_shape=(*[pltpu.HBM(p.shape, p.dtype) for p in parts], *[pltpu.HBM(z.shape, z.dtype) for z in zones]),
        in_specs=[IN_HBM] * (2 * n) + [SEMAPHORES, SEMAPHORES, ANY], out_specs=tuple([IN_HBM] * (2 * n)),
        input_output_aliases={j: j for j in range(2 * n)},
        compiler_params=pltpu.CompilerParams(has_side_effects=IN_FLIGHT),
    )(*parts, *zones, send_sems, recv_sems, after)
    return list(outs[:n]), list(outs[n:])


def _ada_forward(c, ada_w, ada_b):
    def body(c_ref, w_ref, b_ref, cact_ref, mod_ref, gbuf, modrow, send_sems, recv_sems):
        pos = _position()
        me = _index(pos)

        def to_all(ref, round_):
            copies = []
            for k in range(1, NDEV):
                peer, _ = _peer(pos, k)
                copy = pltpu.make_async_remote_copy(
                    src_ref=ref.at[me], dst_ref=ref.at[me], send_sem=send_sems.at[round_, k - 1],
                    recv_sem=recv_sems.at[round_, k - 1], device_id=peer, device_id_type=pl.DeviceIdType.MESH)
                copy.start()
                copies.append(copy)
            for copy in copies:
                copy.wait()

        cact_ref[me] = _silu(c_ref[...])
        to_all(cact_ref, 0)
        rows = lax.broadcasted_iota(jnp.int32, (NDEV, D), 0)
        cact = jnp.zeros((NDEV, D), F32)
        for e in range(NDEV):
            cact = jnp.where(rows == e, cact_ref[e], cact)
        cact = cact.astype(BF16)
        for l in range(DEPTH):
            gbuf[me, l] = jnp.dot(cact, w_ref[l].astype(BF16), preferred_element_type=F32)
        to_all(gbuf, 1)
        mine = lax.broadcasted_iota(jnp.int32, (NDEV, ADA_NC), 0) == me
        for l in range(DEPTH):
            for d in range(NDEV):
                modrow[:, d * ADA_NC:(d + 1) * ADA_NC] = jnp.sum(jnp.where(mine, gbuf[d, l], 0.0), axis=0, keepdims=True)
            full = modrow[...] + b_ref[l:l + 1, :]
            for w in range(3):
                mod_ref[l, w] = full[:, w * D:(w + 1) * D]

    return pl.pallas_call(
        body, name="ada_forward",
        out_shape=[jax.ShapeDtypeStruct((NDEV, 1, D), F32), jax.ShapeDtypeStruct((DEPTH, 3, 1, D), F32)],
        in_specs=[VMEM_FULL] * 3, out_specs=[VMEM_FULL] * 2,
        scratch_shapes=[pltpu.VMEM((NDEV, DEPTH, NDEV, ADA_NC), F32), pltpu.VMEM((1, 3 * D), F32),
                        pltpu.SemaphoreType.DMA((2, NDEV - 1)), pltpu.SemaphoreType.DMA((2, NDEV - 1))],
        compiler_params=_params(),
    )(c, ada_w, ada_b)


def _mod_spec(layer, which, ngrid):
    index = {1: lambda i: (layer, which, 0, 0), 2: lambda i, j: (layer, which, 0, 0)}[ngrid]
    return pl.BlockSpec((None, None, 1, D), index)


def _norm_proj(x, mod, norm_g3, wg, layer, name):
    nb = wg.shape[-1]
    tm = 1024

    def body(x_ref, g_ref, shift_ref, scale_ref, w_ref, h_ref, p_ref):
        @pl.when(pl.program_id(1) == 0)
        def _():
            xv = x_ref[...]
            r = lax.rsqrt(jnp.mean(xv * xv, axis=-1, keepdims=True) + EPS)
            hn = xv * r * g_ref[...]
            h_ref[...] = (hn * (1.0 + scale_ref[...]) + shift_ref[...]).astype(BF16)

        p_ref[...] = jnp.dot(h_ref[...], w_ref[...], preferred_element_type=F32).astype(BF16)

    return pl.pallas_call(
        body, name=name, grid=(S // tm, NDEV),
        out_shape=[jax.ShapeDtypeStruct((S, D), BF16), jax.ShapeDtypeStruct((S, NDEV * nb), BF16)],
        in_specs=[pl.BlockSpec((tm, D), lambda i, d: (i, 0)),
                  pl.BlockSpec((None, 1, D), lambda i, d: (layer, 0, 0)),
                  _mod_spec(layer, 0, 2), _mod_spec(layer, 1, 2),
                  pl.BlockSpec((None, D, nb), lambda i, d: (d, 0, 0))],
        out_specs=[pl.BlockSpec((tm, D), lambda i, d: (i, 0)), pl.BlockSpec((tm, nb), lambda i, d: (i, d))],
        compiler_params=_params(("arbitrary", "arbitrary")),
    )(x, norm_g3, mod, mod, wg)


def _out_proj(ycat, wg_out, x, mod, layer, name):
    tm = 512
    rb = wg_out.shape[1]

    def body(y_ref, w_ref, x_ref, gate_ref, xn_ref, o_ref):
        acc = jnp.zeros((tm, D), F32)
        for d in range(NDEV):
            acc = acc + jnp.dot(y_ref[:, d * rb:(d + 1) * rb], w_ref[d], preferred_element_type=F32)
        o_ref[...] = acc.astype(BF16)
        xn_ref[...] = x_ref[...] + gate_ref[...] * acc

    return pl.pallas_call(
        body, name=name, grid=(S // tm,),
        out_shape=[jax.ShapeDtypeStruct((S, D), F32), jax.ShapeDtypeStruct((S, D), BF16)],
        in_specs=[pl.BlockSpec((tm, NDEV * rb), lambda i: (i, 0)),
                  pl.BlockSpec((NDEV, rb, D), lambda i: (0, 0, 0)),
                  pl.BlockSpec((tm, D), lambda i: (i, 0)), _mod_spec(layer, 2, 1)],
        out_specs=[pl.BlockSpec((tm, D), lambda i: (i, 0))] * 2,
        compiler_params=_params(("arbitrary",)),
    )(ycat, wg_out, x, mod)


def _final_loss(x, target, final_g2):
    tm = 256

    def body(x_ref, t_ref, g_ref, dx_ref, loss_ref, dg_ref):
        @pl.when(pl.program_id(0) == 0)
        def _():
            loss_ref[...] = jnp.zeros_like(loss_ref)
            dg_ref[...] = jnp.zeros_like(dg_ref)

        xv, g = x_ref[...], g_ref[...]
        r = lax.rsqrt(jnp.mean(xv * xv, axis=-1, keepdims=True) + EPS)
        xn = xv * r
        err = xn * g - t_ref[...]
        loss_ref[...] += 0.5 * jnp.sum(jnp.mean(err * err, axis=-1, keepdims=True), axis=0, keepdims=True)
        dy = err * (1.0 / D)
        dg_ref[...] += jnp.sum(dy * xn, axis=0, keepdims=True)
        u = dy * g
        dx_ref[...] = r * (u - xn * jnp.mean(xn * u, axis=-1, keepdims=True))

    tile = pl.BlockSpec((tm, D), lambda i: (i, 0))
    row = pl.BlockSpec((1, D), lambda i: (0, 0))
    return pl.pallas_call(
        body, name="final_loss", grid=(S // tm,),
        out_shape=[jax.ShapeDtypeStruct((S, D), F32), jax.ShapeDtypeStruct((1, LANE), F32), jax.ShapeDtypeStruct((1, D), F32)],
        in_specs=[tile, tile, row], out_specs=[tile, pl.BlockSpec((1, LANE), lambda i: (0, 0)), row],
        compiler_params=_params(("arbitrary",)),
    )(x, target, final_g2)


def _dout(dx, out, mod, layer, name):
    tm = 256

    def body(dx_ref, o_ref, gate_ref, do_ref, dgate_ref):
        @pl.when(pl.program_id(0) == 0)
        def _():
            dgate_ref[...] = jnp.zeros_like(dgate_ref)

        dxv = dx_ref[...]
        do_ref[...] = (gate_ref[...] * dxv).astype(BF16)
        dgate_ref[...] += jnp.sum(dxv * o_ref[...].astype(F32), axis=0, keepdims=True)

    tile = pl.BlockSpec((tm, D), lambda i: (i, 0))
    return pl.pallas_call(
        body, name=name, grid=(S // tm,),
        out_shape=[jax.ShapeDtypeStruct((S, D), BF16), jax.ShapeDtypeStruct((1, D), F32)],
        in_specs=[tile, tile, _mod_spec(layer, 2, 1)], out_specs=[tile, pl.BlockSpec((1, D), lambda i: (0, 0))],
        compiler_params=_params(("arbitrary",)),
    )(dx, out, mod)


def _dycat(d_out, wg_out, name):
    tm = 512
    rb = wg_out.shape[1]

    def body(do_ref, w_ref, dy_ref):
        dov = do_ref[...]
        for d in range(NDEV):
            dy_ref[:, d * rb:(d + 1) * rb] = lax.dot_general(
                dov, w_ref[d], (((1,), (1,)), ((), ())), preferred_element_type=F32).astype(BF16)

    return pl.pallas_call(
        body, name=name, grid=(S // tm,), out_shape=jax.ShapeDtypeStruct((S, NDEV * rb), BF16),
        in_specs=[pl.BlockSpec((tm, D), lambda i: (i, 0)), pl.BlockSpec((NDEV, rb, D), lambda i: (0, 0, 0))],
        out_specs=pl.BlockSpec((tm, NDEV * rb), lambda i: (i, 0)),
        compiler_params=_params(("arbitrary",)),
    )(d_out, wg_out)


def _weight_grad(a, b, name):
    a_blocks = a.shape[1] != D
    ka = a.shape[1] // NDEV if a_blocks else a.shape[1]
    nb = b.shape[1] if a_blocks else b.shape[1] // NDEV

    def body(a_ref, b_ref, o_ref):
        o_ref[...] = lax.dot_general(a_ref[...], b_ref[...], (((0,), (0,)), ((), ())),
                                     preferred_element_type=F32).astype(BF16)

    return pl.pallas_call(
        body, name=name, grid=(NDEV,), out_shape=jax.ShapeDtypeStruct((2, NDEV // 2, ka, nb), BF16),
        in_specs=[pl.BlockSpec((S, ka), (lambda d: (0, d)) if a_blocks else (lambda d: (0, 0))),
                  pl.BlockSpec((S, nb), (lambda d: (0, 0)) if a_blocks else (lambda d: (0, d)))],
        out_specs=pl.BlockSpec((None, None, ka, nb), lambda d: (d % 2, d // 2, 0, 0)),
        compiler_params=_params(("arbitrary",)),
    )(a, b)


def _dh(d_proj, wg, carried, name):
    nb = wg.shape[-1]
    tm = 1024

    def body(dp_ref, w_ref, carried_ref, dh_ref, carried_out):
        part = lax.dot_general(dp_ref[...], w_ref[...], (((1,), (1,)), ((), ())), preferred_element_type=F32)

        @pl.when(pl.program_id(1) == 0)
        def _():
            dh_ref[...] = part

        @pl.when(pl.program_id(1) != 0)
        def _():
            dh_ref[...] += part

    return pl.pallas_call(
        body, name=name, grid=(S // tm, NDEV),
        out_shape=[jax.ShapeDtypeStruct((S, D), F32), jax.ShapeDtypeStruct(carried.shape, carried.dtype)],
        in_specs=[pl.BlockSpec((tm, nb), lambda i, d: (i, d)), pl.BlockSpec((None, D, nb), lambda i, d: (d, 0, 0)), ANY],
        out_specs=[pl.BlockSpec((tm, D), lambda i, d: (i, 0)), ANY],
        input_output_aliases={2: 1},
        compiler_params=_params(("arbitrary", "arbitrary")),
    )(d_proj, wg, carried)


def _norm_bwd(d_h, x, dx, mod, norm_g3, layer, carried, name):
    tm = 256

    def body(dh_ref, x_ref, dx_ref, g_ref, scale_ref, carried_ref, dxi_ref, dshift_ref, dscale_ref, dg_ref, carried_out):
        @pl.when(pl.program_id(0) == 0)
        def _():
            dshift_ref[...] = jnp.zeros_like(dshift_ref)
            dscale_ref[...] = jnp.zeros_like(dscale_ref)
            dg_ref[...] = jnp.zeros_like(dg_ref)

        xv, dhv, g = x_ref[...], dh_ref[...], g_ref[...]
        r = lax.rsqrt(jnp.mean(xv * xv, axis=-1, keepdims=True) + EPS)
        xn = xv * r
        dshift_ref[...] += jnp.sum(dhv, axis=0, keepdims=True)
        dscale_ref[...] += jnp.sum(dhv * (xn * g), axis=0, keepdims=True)
        dhn = dhv * (1.0 + scale_ref[...])
        dg_ref[...] += jnp.sum(dhn * xn, axis=0, keepdims=True)
        u = dhn * g
        dxi_ref[...] = dx_ref[...] + r * (u - xn * jnp.mean(xn * u, axis=-1, keepdims=True))

    tile = pl.BlockSpec((tm, D), lambda i: (i, 0))
    row = pl.BlockSpec((1, D), lambda i: (0, 0))
    return pl.pallas_call(
        body, name=name, grid=(S // tm,),
        out_shape=[jax.ShapeDtypeStruct((S, D), F32)] + [jax.ShapeDtypeStruct((1, D), F32)] * 3
        + [jax.ShapeDtypeStruct(carried.shape, carried.dtype)],
        in_specs=[tile, tile, tile, pl.BlockSpec((None, 1, D), lambda i: (layer, 0, 0)), _mod_spec(layer, 1, 1), ANY],
        out_specs=[tile, row, row, row, ANY], input_output_aliases={5: 4}, compiler_params=_params(("arbitrary",)),
    )(d_h, x, dx, norm_g3, mod, carried)


TS = 256
NCH = TS // CHUNK
HALO_BLOCKS = TS // HALO


def _halo_before(width, col_block):
    return pl.BlockSpec((HALO, width), lambda i: (jnp.maximum(i * HALO_BLOCKS - 1, 0), col_block))


def _halo_after(width, col_block):
    return pl.BlockSpec((HALO, width), lambda i: (jnp.minimum((i + 1) * HALO_BLOCKS, S // HALO - 1), col_block))


def _shift_down(ext, k):
    return pltpu.roll(ext, k, 0)[HALO:]


def _shift_up(ext, k):
    return pltpu.roll(ext, ext.shape[0] - k, 0)[:ext.shape[0] - HALO]


def _layer_norm_head(v, lg, lb):
    mu = jnp.mean(v, axis=-1, keepdims=True)
    vc = v - mu
    rstd = lax.rsqrt(jnp.mean(vc * vc, axis=-1, keepdims=True) + EPS)
    vhat = vc * rstd
    return vhat, rstd, vhat * lg + lb


def _causal_mask():
    return lax.broadcasted_iota(jnp.int32, (CHUNK, CHUNK), 0) >= lax.broadcasted_iota(jnp.int32, (CHUNK, CHUNK), 1)


def _even_mix_fwd(proj, convw, ln_g3, ln_b3, sgu_w, sgu_bcol, wl, name):
    def body(pj_ref, hh_ref, hc_ref, cw_ref, lg_ref, lb_ref, sw_ref, sb_ref, y_ref):
        live = (pl.program_id(0) > 0).astype(F32)
        causal = _causal_mask()
        for j in range(E_A // HEAD):
            cols = slice(j * HEAD, (j + 1) * HEAD)
            w0, w1, w2 = cw_ref[0:1, cols], cw_ref[1:2, cols], cw_ref[2:3, cols]
            lg, lb = lg_ref[:, cols], lb_ref[:, cols]
            wm = jnp.where(causal, sw_ref[j], 0.0).astype(BF16)
            bias = sb_ref[j]

            def split(s, rows, cols=cols):
                return pj_ref[rows, s * E_A + cols.start:s * E_A + cols.stop].astype(F32)

            prev_tail = hc_ref[:, cols].astype(F32) * hh_ref[:, cols].astype(F32) * live
            for n in range(NCH):
                rows = slice(n * CHUNK, (n + 1) * CHUNK)
                p = split(2, rows) * split(0, rows)
                ext = jnp.concatenate([prev_tail, p], axis=0)
                prev_tail = p[CHUNK - HALO:]
                cv = w2 * p + w1 * _shift_down(ext, 1) + w0 * _shift_down(ext, 2)
                y_ref[rows, cols] = (split(1, rows) * cv * _silu(split(3, rows))).astype(BF16)
                _, _, vn = _layer_norm_head(split(5, rows), lg, lb)
                mixed = jnp.dot(wm, vn.astype(BF16), preferred_element_type=F32) + bias
                y_ref[rows, E_A + cols.start:E_A + cols.stop] = (split(4, rows) * mixed * _silu(split(6, rows))).astype(BF16)

    const3 = lambda i: (wl, 0, 0)
    const4 = lambda i: (wl, 0, 0, 0)
    return pl.pallas_call(
        body, name=name, grid=(S // TS,), out_shape=jax.ShapeDtypeStruct((S, 2 * E_A), BF16),
        in_specs=[pl.BlockSpec((TS, 7 * E_A), lambda i: (i, 0)), _halo_before(E_A, 0), _halo_before(E_A, 2),
                  pl.BlockSpec((None, 3, E_A), const3), pl.BlockSpec((None, 1, E_A), const3),
                  pl.BlockSpec((None, 1, E_A), const3), pl.BlockSpec((None, NDEV, CHUNK, CHUNK), const4),
                  pl.BlockSpec((None, NDEV, CHUNK, 1), const4)],
        out_specs=pl.BlockSpec((TS, 2 * E_A), lambda i: (i, 0)),
        compiler_params=_params(("arbitrary",)),
    )(proj, proj, proj, convw, ln_g3, ln_b3, sgu_w, sgu_bcol)


def _even_mix_bwd(proj, d_ycat, convw, ln_g3, ln_b3, sgu_w, sgu_bcol, wl, name):
    nsteps = S // TS

    def body(pj_ref, hh_ref, hc_ref, hb_ref, hz_ref, dy_ref, hdy_ref, cw_ref, lg_ref, lb_ref, sw_ref, sb_ref,
             dp_ref, dcw_ref, dlg_ref, dlb_ref, dsw_ref, dsb_ref):
        step = pl.program_id(0)

        @pl.when(step == 0)
        def _():
            for ref in (dcw_ref, dlg_ref, dlb_ref, dsw_ref, dsb_ref):
                ref[...] = jnp.zeros_like(ref)

        live_before = (step > 0).astype(F32)
        live_after = (step < nsteps - 1).astype(F32)
        causal = _causal_mask()
        for j in range(E_A // HEAD):
            cols = slice(j * HEAD, (j + 1) * HEAD)
            w0, w1, w2 = cw_ref[0:1, cols], cw_ref[1:2, cols], cw_ref[2:3, cols]
            lg, lb = lg_ref[:, cols], lb_ref[:, cols]
            wmf = jnp.where(causal, sw_ref[j], 0.0)
            wm, wmt = wmf.astype(BF16), wmf.T.astype(BF16)
            bias = sb_ref[j]

            def split(s, rows, cols=cols):
                return pj_ref[rows, s * E_A + cols.start:s * E_A + cols.stop].astype(F32)

            def put(s, rows, val, cols=cols):
                dp_ref[rows, s * E_A + cols.start:s * E_A + cols.stop] = val.astype(BF16)

            ps = [split(2, slice(n * CHUNK, (n + 1) * CHUNK)) * split(0, slice(n * CHUNK, (n + 1) * CHUNK)) for n in range(NCH)]
            next_head = (hdy_ref[:, cols].astype(F32) * hb_ref[:, cols].astype(F32) * _silu(hz_ref[:, cols].astype(F32))
                         * live_after)
            acc_w = [jnp.zeros((1, HEAD), F32) for _ in range(3)]
            for n in reversed(range(NCH)):
                rows = slice(n * CHUNK, (n + 1) * CHUNK)
                p = ps[n]
                tail = ps[n - 1][CHUNK - HALO:] if n > 0 else hc_ref[:, cols].astype(F32) * hh_ref[:, cols].astype(F32) * live_before
                ext = jnp.concatenate([tail, p], axis=0)
                p1, p2 = _shift_down(ext, 1), _shift_down(ext, 2)
                cv = w2 * p + w1 * p1 + w0 * p2
                a_b, a_z = split(1, rows), split(3, rows)
                sz, dsz = _silu_and_grad(a_z)
                dya = dy_ref[rows, cols].astype(F32)
                put(1, rows, dya * cv * sz)
                put(3, rows, dya * a_b * cv * dsz)
                gcv = dya * a_b * sz
                acc_w[0] += jnp.sum(gcv * p2, axis=0, keepdims=True)
                acc_w[1] += jnp.sum(gcv * p1, axis=0, keepdims=True)
                acc_w[2] += jnp.sum(gcv * p, axis=0, keepdims=True)
                gext = jnp.concatenate([gcv, next_head], axis=0)
                next_head = gcv[:HALO]
                dpv = w2 * gcv + w1 * _shift_up(gext, 1) + w0 * _shift_up(gext, 2)
                put(2, rows, dpv * split(0, rows))
                put(0, rows, dpv * split(2, rows))
            for k in range(3):
                dcw_ref[k:k + 1, cols] += acc_w[k]

            acc_lg, acc_lb = jnp.zeros((1, HEAD), F32), jnp.zeros((1, HEAD), F32)
            acc_sw, acc_sb = jnp.zeros((CHUNK, CHUNK), F32), jnp.zeros((CHUNK, 1), F32)
            for n in range(NCH):
                rows = slice(n * CHUNK, (n + 1) * CHUNK)
                u, z = split(4, rows), split(6, rows)
                vhat, rstd, vn = _layer_norm_head(split(5, rows), lg, lb)
                vn16 = vn.astype(BF16)
                mixed = jnp.dot(wm, vn16, preferred_element_type=F32) + bias
                sz, dsz = _silu_and_grad(z)
                dyb = dy_ref[rows, E_A + cols.start:E_A + cols.stop].astype(F32)
                put(4, rows, dyb * mixed * sz)
                put(6, rows, dyb * u * mixed * dsz)
                dmix = dyb * u * sz
                dmix16 = dmix.astype(BF16)
                acc_sb += jnp.sum(dmix, axis=1, keepdims=True)
                acc_sw += lax.dot_general(dmix16, vn16, (((1,), (1,)), ((), ())), preferred_element_type=F32)
                dvn = jnp.dot(wmt, dmix16, preferred_element_type=F32)
                acc_lg += jnp.sum(dvn * vhat, axis=0, keepdims=True)
                acc_lb += jnp.sum(dvn, axis=0, keepdims=True)
                dvh = dvn * lg
                put(5, rows, rstd * (dvh - jnp.mean(dvh, axis=-1, keepdims=True)
                                     - vhat * jnp.mean(dvh * vhat, axis=-1, keepdims=True)))
            dlg_ref[:, cols] += acc_lg
            dlb_ref[:, cols] += acc_lb
            dsw_ref[j] += jnp.where(causal, acc_sw, 0.0)
            dsb_ref[j] += acc_sb

    const3 = lambda i: (wl, 0, 0)
    const4 = lambda i: (wl, 0, 0, 0)
    fixed2 = lambda i: (0, 0)
    fixed3 = lambda i: (0, 0, 0)
    return pl.pallas_call(
        body, name=name, grid=(nsteps,),
        out_shape=[jax.ShapeDtypeStruct((S, 7 * E_A), BF16), jax.ShapeDtypeStruct((3, E_A), F32),
                   jax.ShapeDtypeStruct((1, E_A), F32), jax.ShapeDtypeStruct((1, E_A), F32),
                   jax.ShapeDtypeStruct((NDEV, CHUNK, CHUNK), F32), jax.ShapeDtypeStruct((NDEV, CHUNK, 1), F32)],
        in_specs=[pl.BlockSpec((TS, 7 * E_A), lambda i: (i, 0)), _halo_before(E_A, 0), _halo_before(E_A, 2),
                  _halo_after(E_A, 1), _halo_after(E_A, 3),
                  pl.BlockSpec((TS, 2 * E_A), lambda i: (i, 0)), _halo_after(E_A, 0),
                  pl.BlockSpec((None, 3, E_A), const3), pl.BlockSpec((None, 1, E_A), const3),
                  pl.BlockSpec((None, 1, E_A), const3), pl.BlockSpec((None, NDEV, CHUNK, CHUNK), const4),
                  pl.BlockSpec((None, NDEV, CHUNK, 1), const4)],
        out_specs=[pl.BlockSpec((TS, 7 * E_A), lambda i: (i, 0)), pl.BlockSpec((3, E_A), fixed2),
                   pl.BlockSpec((1, E_A), fixed2), pl.BlockSpec((1, E_A), fixed2),
                   pl.BlockSpec((NDEV, CHUNK, CHUNK), fixed3), pl.BlockSpec((NDEV, CHUNK, 1), fixed3)],
        compiler_params=_params(("arbitrary",)),
    )(proj, proj, proj, proj, proj, d_ycat, d_ycat, convw, ln_g3, ln_b3, sgu_w, sgu_bcol)


def _window_count(step, n, win, ext_before):
    rows = CHUNK if ext_before else CHUNK + HALO
    t = step * TS + n * CHUNK + lax.broadcasted_iota(jnp.int32, (rows, 1), 0)
    return jnp.minimum(t + 1, win).astype(F32)


def _pool_weight(wp_ref, g):
    return jnp.concatenate([wp_ref[d, g] for d in range(NDEV)], axis=0)


def _pooled_chunk(p, tail, win, count):
    sums = jnp.concatenate([tail, p], axis=0)
    shift = 1
    while shift < win:
        sums = sums + pltpu.roll(sums, shift, 0)
        shift *= 2
    return sums[HALO:] / count - p


def _pool_mix_fwd(proj, wpool, pscale4, wl, name):
    e_c = 4 * G_C

    def body(pj_ref, hp_ref, wp_ref, ps_ref, y_ref, pooled_scr, yraw_scr):
        step = pl.program_id(0)
        live = (step > 0).astype(F32)
        for g, win in enumerate(POOL_WINDOWS):
            for q in range(G_C // LANE):
                cols = slice(g * G_C + q * LANE, g * G_C + (q + 1) * LANE)
                tail = hp_ref[:, cols].astype(F32) * live
                for n in range(NCH):
                    rows = slice(n * CHUNK, (n + 1) * CHUNK)
                    p = pj_ref[rows, cols].astype(F32)
                    pooled_scr[rows, q * LANE:(q + 1) * LANE] = _pooled_chunk(
                        p, tail, win, _window_count(step, n, win, True)).astype(BF16)
                    tail = p[CHUNK - HALO:]
            yraw_scr[...] = jnp.dot(pooled_scr[...], _pool_weight(wp_ref, g), preferred_element_type=F32)
            for q in range(G_C // LANE):
                cols = slice(g * G_C + q * LANE, g * G_C + (q + 1) * LANE)
                for n in range(NCH):
                    rows = slice(n * CHUNK, (n + 1) * CHUNK)
                    z = pj_ref[rows, e_c + cols.start:e_c + cols.stop].astype(F32)
                    y_ref[rows, cols] = (yraw_scr[rows, q * LANE:(q + 1) * LANE] * ps_ref[:, cols] * _silu(z)).astype(BF16)

    return pl.pallas_call(
        body, name=name, grid=(S // TS,), out_shape=jax.ShapeDtypeStruct((S, e_c), BF16),
        in_specs=[pl.BlockSpec((TS, 2 * e_c), lambda i: (i, 0)), _halo_before(e_c, 0),
                  pl.BlockSpec((NDEV, 4, G_C // NDEV, G_C), lambda i: (0, 0, 0, 0)),
                  pl.BlockSpec((None, 1, e_c), lambda i: (wl, 0, 0))],
        out_specs=pl.BlockSpec((TS, e_c), lambda i: (i, 0)),
        scratch_shapes=[pltpu.VMEM((TS, G_C), BF16), pltpu.VMEM((TS, G_C), F32)],
        compiler_params=_params(("arbitrary",)),
    )(proj, proj, wpool, pscale4)


def _pool_mix_bwd(proj, d_ycat, wpool, pscale4, wl, name):
    e_c = 4 * G_C
    nsteps = S // TS
    rb = G_C // NDEV

    def body(pj_ref, hp_ref, hz_ref, dy_ref, hdy_ref, wp_ref, ps_ref,
             dp_ref, dps_ref, dwp_ref, pooled_scr, yraw_scr, dyraw_scr, dpool_scr, acc_w):
        step = pl.program_id(0)

        @pl.when(step == 0)
        def _():
            dps_ref[...] = jnp.zeros_like(dps_ref)
            acc_w[...] = jnp.zeros_like(acc_w)

        live_before = (step > 0).astype(F32)
        live_after = (step < nsteps - 1).astype(F32)
        for g, win in enumerate(POOL_WINDOWS):
            weight = _pool_weight(wp_ref, g)
            for q in range(G_C // LANE):
                cols = slice(g * G_C + q * LANE, g * G_C + (q + 1) * LANE)
                tail = hp_ref[:, cols].astype(F32) * live_before
                for n in range(NCH):
                    rows = slice(n * CHUNK, (n + 1) * CHUNK)
                    p = pj_ref[rows, cols].astype(F32)
                    pooled_scr[rows, q * LANE:(q + 1) * LANE] = _pooled_chunk(
                        p, tail, win, _window_count(step, n, win, True)).astype(BF16)
                    tail = p[CHUNK - HALO:]
            yraw_scr[...] = jnp.dot(pooled_scr[...], weight, preferred_element_type=F32)
            for q in range(G_C // LANE):
                cols = slice(g * G_C + q * LANE, g * G_C + (q + 1) * LANE)
                local = slice(q * LANE, (q + 1) * LANE)
                scale = ps_ref[:, cols]
                acc_ps = jnp.zeros((1, LANE), F32)
                for n in range(NCH):
                    rows = slice(n * CHUNK, (n + 1) * CHUNK)
                    sz, dsz = _silu_and_grad(pj_ref[rows, e_c + cols.start:e_c + cols.stop].astype(F32))
                    dyv = dy_ref[rows, cols].astype(F32)
                    yraw = yraw_scr[rows, local]
                    dyraw_scr[rows, local] = (dyv * scale * sz).astype(BF16)
                    acc_ps += jnp.sum(dyv * yraw * sz, axis=0, keepdims=True)
                    dp_ref[rows, e_c + cols.start:e_c + cols.stop] = (dyv * yraw * scale * dsz).astype(BF16)
                dps_ref[:, cols] += acc_ps
                dyraw_scr[TS:, local] = (hdy_ref[:, cols].astype(F32) * scale * _silu(hz_ref[:, cols].astype(F32))
                                         * live_after).astype(BF16)
            dpool_scr[...] = lax.dot_general(dyraw_scr[...], weight, (((1,), (1,)), ((), ())), preferred_element_type=F32)
            acc_w[g] += lax.dot_general(pooled_scr[...], dyraw_scr[:TS, :], (((0,), (0,)), ((), ())),
                                        preferred_element_type=F32)
            for q in range(G_C // LANE):
                cols = slice(g * G_C + q * LANE, g * G_C + (q + 1) * LANE)
                local = slice(q * LANE, (q + 1) * LANE)
                for n in range(NCH):
                    rows = slice(n * CHUNK, (n + 1) * CHUNK)
                    ext = dpool_scr[n * CHUNK:(n + 1) * CHUNK + HALO, local]
                    sums = ext / _window_count(step, n, win, False)
                    shift = 1
                    while shift < win:
                        sums = sums + pltpu.roll(sums, CHUNK + HALO - shift, 0)
                        shift *= 2
                    dp_ref[rows, cols] = (sums[:CHUNK] - ext[:CHUNK]).astype(BF16)

        @pl.when(step == nsteps - 1)
        def _():
            for g in range(4):
                for d in range(NDEV):
                    dwp_ref[d % 2, d // 2, g] = acc_w[g, d * rb:(d + 1) * rb, :].astype(BF16)

    in_specs = [pl.BlockSpec((TS, 2 * e_c), lambda i: (i, 0)), _halo_before(e_c, 0), _halo_after(e_c, 1),
                pl.BlockSpec((TS, e_c), lambda i: (i, 0)), _halo_after(e_c, 0),
                pl.BlockSpec((NDEV, 4, rb, G_C), lambda i: (0, 0, 0, 0)),
                pl.BlockSpec((None, 1, e_c), lambda i: (wl, 0, 0))]
    args = [proj, proj, proj, d_ycat, d_ycat, wpool, pscale4]
    return pl.pallas_call(
        body, name=name, grid=(nsteps,),
        out_shape=[jax.ShapeDtypeStruct((S, 2 * e_c), BF16), jax.ShapeDtypeStruct((1, e_c), F32),
                   jax.ShapeDtypeStruct((2, NDEV // 2) + wpool.shape[1:], BF16)],
        in_specs=in_specs,
        out_specs=[pl.BlockSpec((TS, 2 * e_c), lambda i: (i, 0)), pl.BlockSpec((1, e_c), lambda i: (0, 0)),
                   pl.BlockSpec((2, NDEV // 2, 4, rb, G_C), lambda i: (0, 0, 0, 0, 0))],
        scratch_shapes=[pltpu.VMEM((TS, G_C), BF16), pltpu.VMEM((TS, G_C), F32), pltpu.VMEM((TS + HALO, G_C), BF16),
                        pltpu.VMEM((TS + HALO, G_C), F32), pltpu.VMEM((4, G_C, G_C), F32)],
        compiler_params=_params(("arbitrary",)),
    )(*args)


def _adamw(w, g, m, v):
    m = ADAM_B1 * m + (1.0 - ADAM_B1) * g
    v = ADAM_B2 * v + (1.0 - ADAM_B2) * jnp.square(g)
    m_hat = m / (1.0 - ADAM_B1 ** ADAM_STEP)
    v_hat = v / (1.0 - ADAM_B2 ** ADAM_STEP)
    delta = -ADAM_LR * (m_hat / (jnp.sqrt(v_hat) + ADAM_EPS) + ADAM_WD * w)
    return delta, m, v


def _adam_sharded(w, m, v, chip_parts, landed, my_chip, name):
    nl, nr, ncol = w.shape
    tr = 128
    steps = nr // tr

    def body(chip_ref, w_ref, m_ref, v_ref, *rest):
        parts, zones = rest[:nl], rest[nl:2 * nl]
        g_ref, d_ref, nm_ref, nv_ref = rest[2 * nl:]
        layer = pl.program_id(0)
        g = jnp.zeros((tr, ncol), F32)
        for l in range(nl):
            gl = parts[l][...].astype(F32)
            for q in range(3):
                gl = gl + zones[l][q].astype(F32)
            g = jnp.where(layer == l, gl, g)
        g_ref[...] = g
        d_ref[...], nm_ref[...], nv_ref[...] = _adamw(w_ref[...], g, m_ref[...], v_ref[...])

    def rows_of(l):
        return lambda layer, i, chip_ref: jnp.where(layer == l, i, jnp.where(layer < l, 0, steps - 1))

    spec = pl.BlockSpec((None, tr, ncol), lambda layer, i, chip_ref: (layer, i, 0))
    part_specs = [pl.BlockSpec((None, tr, ncol), lambda layer, i, chip_ref, l=l: (chip_ref[0], rows_of(l)(layer, i, chip_ref), 0))
                  for l in range(nl)]
    zone_specs = [pl.BlockSpec((3, tr, ncol), lambda layer, i, chip_ref, l=l: (0, rows_of(l)(layer, i, chip_ref), 0))
                  for l in range(nl)]
    grid_spec = pltpu.PrefetchScalarGridSpec(
        num_scalar_prefetch=1, grid=(nl, steps), in_specs=[spec, spec, spec] + part_specs + zone_specs, out_specs=[spec] * 4)
    return pl.pallas_call(
        body, name=name, grid_spec=grid_spec, out_shape=[jax.ShapeDtypeStruct(w.shape, F32)] * 4,
        compiler_params=_params(("arbitrary", "arbitrary")),
    )(my_chip, w, m, v, *chip_parts, *landed)


def _adam_small(w, g, m, v, name):
    def body(w_ref, g_ref, m_ref, v_ref, d_ref, nm_ref, nv_ref):
        d_ref[...], nm_ref[...], nv_ref[...] = _adamw(w_ref[...], g_ref[...], m_ref[...], v_ref[...])

    return pl.pallas_call(body, name=name, out_shape=[jax.ShapeDtypeStruct(w.shape, F32)] * 3,
                          in_specs=[VMEM_FULL] * 4, out_specs=[VMEM_FULL] * 3, compiler_params=_params())(w, g, m, v)


def _sum_devices(gathered, name):
    _, nr, ncol = gathered.shape

    def body(g_ref, o_ref):
        acc = g_ref[0]
        for s in range(1, NDEV):
            acc = acc + g_ref[s]
        o_ref[...] = acc

    return pl.pallas_call(body, name=name, grid=(1,), out_shape=jax.ShapeDtypeStruct((nr, ncol), F32),
                          in_specs=[pl.BlockSpec((NDEV, nr, ncol), lambda i: (0, 0, 0))],
                          out_specs=pl.BlockSpec((nr, ncol), lambda i: (0, 0)),
                          compiler_params=_params(("arbitrary",)))(gathered)


def _ada_weight_adam(cact_t, dmod_mine, w, m, v):
    def body(ct_ref, dm_ref, w_ref, m_ref, v_ref, g_ref, d_ref, nm_ref, nv_ref):
        ct, dm = ct_ref[...], dm_ref[...]
        g = ct[:, 0:1] * dm[0:1, :]
        for e in range(1, NDEV):
            g = g + ct[:, e:e + 1] * dm[e:e + 1, :]
        g_ref[...] = g
        d_ref[...], nm_ref[...], nv_ref[...] = _adamw(w_ref[...], g, m_ref[...], v_ref[...])

    spec = pl.BlockSpec((None, D, ADA_NC), lambda l: (l, 0, 0))
    return pl.pallas_call(
        body, name="ada_weight_adam", grid=(DEPTH,), out_shape=[jax.ShapeDtypeStruct(w.shape, F32)] * 4,
        in_specs=[pl.BlockSpec((D, NDEV), lambda l: (0, 0)), pl.BlockSpec((None, NDEV, ADA_NC), lambda l: (l, 0, 0)),
                  spec, spec, spec],
        out_specs=[spec] * 4, compiler_params=_params(("arbitrary",)),
    )(cact_t, dmod_mine, w, m, v)


def _pad_rows(a, rows):
    a = a.reshape(-1, D)
    return jnp.pad(a, ((0, rows - a.shape[0]), (0, 0)))


def kernel(x, c, norm_g, ada_w, ada_b, ab_w_in, ab_conv_w, ab_ln_g, ab_ln_b, ab_sgu_w, ab_sgu_b, ab_w_out, c_w_in, c_pool_w, c_pool_scale, c_w_out, final_g, loss_target, m_norm_g, m_ada_w, m_ada_b, m_ab_w_in, m_ab_conv_w, m_ab_ln_g, m_ab_ln_b, m_ab_sgu_w, m_ab_sgu_b, m_ab_w_out, m_c_w_in, m_c_pool_w, m_c_pool_scale, m_c_w_out, m_final_g, v_norm_g, v_ada_w, v_ada_b, v_ab_w_in, v_ab_conv_w, v_ab_ln_g, v_ab_ln_b, v_ab_sgu_w, v_ab_sgu_b, v_ab_w_out, v_c_w_in, v_c_pool_w, v_c_pool_scale, v_c_w_out, v_final_g):
    x_pos, y_pos, c_pos = _position()
    me = _index((x_pos, y_pos, c_pos))
    core = c_pos.astype(jnp.int32).reshape(1)
    my_chip = (2 * x_pos + y_pos).astype(jnp.int32).reshape(1)
    me1 = me.astype(jnp.int32).reshape(1)
    x0 = x.reshape(S, D)
    target = loss_target.reshape(S, D)
    norm_g3 = norm_g.reshape(DEPTH, 1, D)
    ln_g3, ln_b3 = ab_ln_g.reshape(2, 1, E_A), ab_ln_b.reshape(2, 1, E_A)
    sgu_bcol = ab_sgu_b.reshape(2, NDEV, CHUNK, 1)
    rb = G_C // NDEV
    pool_w3, m_pool_w3, v_pool_w3 = (a.reshape(2, 4 * rb, G_C) for a in (c_pool_w, m_c_pool_w, v_c_pool_w))

    cact_all, mod = _ada_forward(c, ada_w, ada_b)
    convw_all, pscale_all = _gather([ab_conv_w, c_pool_scale], "gather_small_weights")
    convw = jnp.transpose(convw_all, (1, 2, 0, 3)).reshape(2, 3, E_A)
    pscale4 = jnp.transpose(pscale_all, (1, 0, 2)).reshape(2, 1, 4 * G_C)
    flights, after = [], [convw_all]
    for layer in range(DEPTH):
        wl = layer // 2
        if layer % 2 == 0:
            zones = [_to_zone(ab_w_in, wl, me1, BF16, f"cast_w_in_{layer}"), _to_zone(ab_w_out, wl, me1, BF16, f"cast_w_out_{layer}")]
        else:
            zones = [_to_zone(c_w_in, wl, me1, BF16, f"cast_w_in_{layer}"), _to_zone(c_w_out, wl, me1, BF16, f"cast_w_out_{layer}"),
                     _to_zone(pool_w3, wl, me1, BF16, f"cast_pool_w_{layer}")]
        send1, recv1, zones, (mod,) = _gather_start(zones, after, [mod], f"gather_start_{layer}")
        flights.append((send1, recv1, zones))
        after = []

    def finish_gather(layer, passed, after):
        send1, recv1, _ = flights[layer]
        send2, recv2, zones = passed
        wg = _gather_end(zones, send1, recv1, send2, recv2, after, f"gather_end_{layer}")
        return wg[:2] + [w.reshape(NDEV, 4, rb, G_C) for w in wg[2:]]

    xs, hs, projs, ycats, outs, gathered_w = [x0], [], [], [], [], []
    gathered_w.append(finish_gather(0, _gather_mid(flights[0][2], flights[0][1], mod, "gather_mid_0"), mod))
    for layer in range(DEPTH):
        wl = layer // 2
        even = layer % 2 == 0
        wg = gathered_w[layer]
        h, proj = _norm_proj(xs[-1], mod, norm_g3, wg[0], layer, f"norm_proj_{layer}")
        if layer + 1 < DEPTH:
            passed = _gather_mid(flights[layer + 1][2], flights[layer + 1][1], h, f"gather_mid_{layer + 1}")
        if even:
            ycat = _even_mix_fwd(proj, convw, ln_g3, ln_b3, ab_sgu_w, sgu_bcol, wl, f"even_mix_fwd_{layer}")
        else:
            ycat = _pool_mix_fwd(proj, wg[2], pscale4, wl, f"pool_mix_fwd_{layer}")
        x_new, out = _out_proj(ycat, wg[1], xs[-1], mod, layer, f"out_proj_{layer}")
        if layer + 1 < DEPTH:
            gathered_w.append(finish_gather(layer + 1, passed, x_new))
        xs.append(x_new)
        hs.append(h)
        projs.append(proj)
        ycats.append(ycat)
        outs.append(out)

    dx, loss_part, d_final_g = _final_loss(xs[DEPTH], target, final_g.reshape(1, D))
    loss = lax.psum(loss_part[0, 0], ("x", "y", "c"))

    d_mod, d_norm_g = [None] * DEPTH, [None] * DEPTH
    small, scatters = {}, {}
    for layer in reversed(range(DEPTH)):
        wl = layer // 2
        even = layer % 2 == 0
        wg = gathered_w[layer]
        d_out, d_gate = _dout(dx, outs[layer], mod, layer, f"dout_{layer}")
        d_ycat = _dycat(d_out, wg[1], f"dycat_{layer}")
        parts = [None, _weight_grad(ycats[layer], d_out, f"grad_w_out_{layer}")]
        if even:
            d_proj, d_cw, d_lg, d_lb, d_sw, d_sb = _even_mix_bwd(
                projs[layer], d_ycat, convw, ln_g3, ln_b3, ab_sgu_w, sgu_bcol, wl, f"even_mix_bwd_{layer}")
            small[layer] = (d_cw, d_lg, d_lb, d_sw, d_sb)
        else:
            d_proj, d_ps, d_pool = _pool_mix_bwd(projs[layer], d_ycat, wg[2], pscale4, wl, f"pool_mix_bwd_{layer}")
            small[layer] = (d_ps,)
            parts.append(d_pool)
        parts[0] = _weight_grad(hs[layer], d_proj, f"grad_w_in_{layer}")
        pair_send, pair_recv, parts, from_sibling = _pair_start(parts, f"pair_start_{layer}")
        d_h, parts[0] = _dh(d_proj, wg[0], parts[0], f"dh_{layer}")
        parts, from_sibling = _pair_end(parts, from_sibling, pair_send, pair_recv, d_h, f"pair_end_{layer}")
        chip_parts = [_pair_sum(p, q, core, f"pair_sum_{layer}_{j}") for j, (p, q) in enumerate(zip(parts, from_sibling))]
        send_sems, recv_sems, chip_parts, zones = _scatter_start(chip_parts, f"scatter_start_{layer}")
        dx, d_shift, d_scale, d_norm_g[layer], chip_parts[0] = _norm_bwd(
            d_h, xs[layer], dx, mod, norm_g3, layer, chip_parts[0], f"norm_bwd_{layer}")
        scatters[layer] = (send_sems, recv_sems, chip_parts, zones)
        d_mod[layer] = jnp.concatenate([d_shift, d_scale, d_gate], axis=0)

    sections = [("norm_g", jnp.concatenate(d_norm_g, axis=0), 8),
                ("d_mod", jnp.concatenate(d_mod, axis=0), 16),
                ("ab_ln_g", jnp.concatenate([small[0][1], small[2][1]], axis=0), 8),
                ("ab_ln_b", jnp.concatenate([small[0][2], small[2][2]], axis=0), 8),
                ("ab_sgu_b", jnp.stack([small[0][4], small[2][4]]), 8),
                ("final_g", d_final_g, 8),
                ("ab_conv_w", jnp.stack([small[0][0], small[2][0]]), 8),
                ("c_pool_scale", jnp.concatenate([small[1][0], small[3][0]], axis=0), 8),
                ("ab_sgu_w", jnp.stack([small[0][3], small[2][3]]), 256)]
    offsets, at = {}, 0
    for name, _, rows in sections:
        offsets[name] = (at, rows)
        at += rows
    packed = jnp.concatenate([_pad_rows(a, rows) for _, a, rows in sections], axis=0)
    small_zone = _to_zone(packed[None], 0, me1, F32, "place_small_grads")
    small_send1, small_recv1, small_zone, (dx,) = _gather_start([small_zone], [], [dx], "gather_small_start")
    grad_x = dx.reshape(x.shape)

    landed = {}
    for layer in range(DEPTH):
        send_sems, recv_sems, chip_parts, zones = scatters[layer]
        landed[layer] = _scatter_end(chip_parts, zones, send_sems, recv_sems, dx, f"scatter_end_{layer}")

    def flat(a):
        return a.reshape(a.shape[0], -1, a.shape[-1])

    res = {}
    for k, j, layers, (w, m, v) in [
            ("ab_w_in", 0, (0, 2), (ab_w_in, m_ab_w_in, v_ab_w_in)), ("ab_w_out", 1, (0, 2), (ab_w_out, m_ab_w_out, v_ab_w_out)),
            ("c_w_in", 0, (1, 3), (c_w_in, m_c_w_in, v_c_w_in)), ("c_w_out", 1, (1, 3), (c_w_out, m_c_w_out, v_c_w_out)),
            ("c_pool_w", 2, (1, 3), (pool_w3, m_pool_w3, v_pool_w3))]:
        outs4 = _adam_sharded(w, m, v, [flat(landed[l][0][j]) for l in layers], [flat(landed[l][1][j]) for l in layers],
                              my_chip, "adam_" + k)
        res[k] = [o.reshape(c_pool_w.shape) if k == "c_pool_w" else o for o in outs4]

    last = res["c_pool_w"][0]
    small_send2, small_recv2, small_zone = _gather_mid(small_zone, small_recv1, last, "gather_small_mid")
    gathered = _gather_end(small_zone, small_send1, small_recv1, small_send2, small_recv2, last, "gather_small_end")[0]
    summed = _sum_devices(gathered, "sum_small_grads")

    def section(name, nrows, src=summed):
        start = offsets[name][0]
        return src[..., start:start + nrows, :]

    grads = {
        "norm_g": section("norm_g", DEPTH),
        "ada_b": section("d_mod", 3 * DEPTH).reshape(DEPTH, 3 * D),
        "ab_ln_g": section("ab_ln_g", 2), "ab_ln_b": section("ab_ln_b", 2),
        "ab_sgu_b": section("ab_sgu_b", 2).reshape(ab_sgu_b.shape),
        "final_g": section("final_g", 1),
        "ab_sgu_w": section("ab_sgu_w", 256).reshape(ab_sgu_w.shape),
        "ab_conv_w": lax.dynamic_slice_in_dim(section("ab_conv_w", 6).reshape(2, 3, E_A), me * HEAD, HEAD, axis=2),
        "c_pool_scale": lax.dynamic_slice_in_dim(section("c_pool_scale", 4).reshape(2, 4 * G_C), me * 256, 256, axis=1),
    }
    small_w = {"norm_g": (norm_g, m_norm_g, v_norm_g), "ada_b": (ada_b, m_ada_b, v_ada_b),
               "ab_ln_g": (ab_ln_g, m_ab_ln_g, v_ab_ln_g), "ab_ln_b": (ab_ln_b, m_ab_ln_b, v_ab_ln_b),
               "ab_sgu_b": (ab_sgu_b, m_ab_sgu_b, v_ab_sgu_b),
               "final_g": (final_g.reshape(1, D), m_final_g.reshape(1, D), v_final_g.reshape(1, D)),
               "ab_sgu_w": (ab_sgu_w, m_ab_sgu_w, v_ab_sgu_w), "ab_conv_w": (ab_conv_w, m_ab_conv_w, v_ab_conv_w),
               "c_pool_scale": (c_pool_scale, m_c_pool_scale, v_c_pool_scale)}
    for k, (w, m, v) in small_w.items():
        res[k] = [grads[k]] + list(_adam_small(w, grads[k], m, v, "adam_" + k))
    res["final_g"] = [a.reshape(D) for a in res["final_g"]]

    dmod_all = section("d_mod", 3 * DEPTH, gathered).reshape(NDEV, DEPTH, 3 * D)
    dmod_mine = jnp.transpose(lax.dynamic_slice_in_dim(dmod_all, me * ADA_NC, ADA_NC, axis=2), (1, 0, 2))
    res["ada_w"] = _ada_weight_adam(jnp.transpose(cact_all.reshape(NDEV, D)), dmod_mine, ada_w, m_ada_w, v_ada_w)

    order = ["norm_g", "ada_w", "ada_b", "ab_w_in", "ab_conv_w", "ab_ln_g", "ab_ln_b", "ab_sgu_w", "ab_sgu_b",
             "ab_w_out", "c_w_in", "c_pool_w", "c_pool_scale", "c_w_out", "final_g"]
    return (loss, grad_x, *[res[k][0] for k in order], *[res[k][1] for k in order],
            *[res[k][2] for k in order], *[res[k][3] for k in order])
```

```python
import jax
import jax.numpy as jnp
from jax import lax
from jax.experimental import pallas as pl
from jax.experimental.pallas import tpu as pltpu

F32, BF16 = jnp.float32, jnp.bfloat16
S, D = 2048, 1024
NDEV = 8
DEPTH = 4
EPS = 1e-6
E_A = 1024
HEAD = 128
CHUNK = 128
POOL_WINDOWS = (2, 4, 8, 16)
G_C = 512
HALO = 16
ADA_NC = 384
MIB = 1024 * 1024
LANE = 128

ADAM_LR, ADAM_B1, ADAM_B2, ADAM_EPS, ADAM_WD, ADAM_STEP = 0.001, 0.9, 0.999, 1e-08, 0.01, 10

ANY = pl.BlockSpec(memory_space=pl.ANY)
VMEM_FULL = pl.BlockSpec(memory_space=pltpu.VMEM)
IN_HBM = pl.BlockSpec(memory_space=pltpu.HBM)
SEMAPHORES = pl.BlockSpec(memory_space=pltpu.SEMAPHORE)
IN_FLIGHT = pltpu.SideEffectType.DATAFLOW_SIDE_EFFECTING


V7X_VMEM_MIB = 64
VMEM_LIMIT_MIB = V7X_VMEM_MIB - 4


def _params(semantics=None):
    return pltpu.CompilerParams(dimension_semantics=semantics, vmem_limit_bytes=VMEM_LIMIT_MIB * MIB)


def _silu(z):
    return z * jax.nn.sigmoid(z)


def _silu_and_grad(z):
    sig = jax.nn.sigmoid(z)
    return z * sig, sig * (1.0 + z * (1.0 - sig))


def _position():
    return lax.axis_index("x"), lax.axis_index("y"), lax.axis_index("c")


def _index(pos):
    return 4 * pos[0] + 2 * pos[1] + pos[2]


def _peer(pos, k):
    flipped = tuple(1 - p if (k >> (2 - b)) & 1 else p for b, p in enumerate(pos))
    return flipped, _index(flipped)


def _remote(src, dst, send_sem, recv_sem, device):
    return pltpu.make_async_remote_copy(src_ref=src, dst_ref=dst, send_sem=send_sem, recv_sem=recv_sem,
                                        device_id=device, device_id_type=pl.DeviceIdType.MESH)


def _gather(arrays, name):
    n = len(arrays)
    out_shape = [jax.ShapeDtypeStruct((NDEV,) + a.shape, a.dtype) for a in arrays]

    def body(*refs):
        ins, outs = refs[:n], refs[n:2 * n]
        send_sems, recv_sems, own_sems = refs[2 * n:]
        x, y, c = _position()
        me = _index((x, y, c))
        sibling = (x, y, 1 - c)
        chips = [(1 - x, y), (x, 1 - y), (1 - x, 1 - y)]

        def block_copy(j, k, owner, to, src=None):
            rows = outs[j].at[_index(owner)]
            return _remote(rows if src is None else src, rows, send_sems.at[j, k], recv_sems.at[j, k], to)

        own, first, passed = [], [], []
        for j in range(n):
            own.append(pltpu.make_async_copy(ins[j], outs[j].at[me], own_sems.at[j]))
            first.append(block_copy(j, 0, (x, y, c), sibling, src=ins[j]))
            first += [block_copy(j, 1 + q, (x, y, c), (*chip, c), src=ins[j]) for q, chip in enumerate(chips)]
        for copy in own + first:
            copy.start()
        for q, chip in enumerate(chips):
            for j in range(n):
                block_copy(j, 1 + q, (*chip, c), (x, y, c)).wait_recv()
                forward = block_copy(j, 4 + q, (*chip, c), sibling)
                forward.start()
                passed.append(forward)
        for j in range(n):
            block_copy(j, 0, sibling, (x, y, c)).wait_recv()
            for q, chip in enumerate(chips):
                block_copy(j, 4 + q, (*chip, 1 - c), (x, y, c)).wait_recv()
        for copy in first + passed:
            copy.wait_send()
        for copy in own:
            copy.wait()

    return pl.pallas_call(
        body, name=name, out_shape=out_shape, in_specs=[ANY] * n, out_specs=[ANY] * n,
        scratch_shapes=[pltpu.SemaphoreType.DMA((n, NDEV - 1)), pltpu.SemaphoreType.DMA((n, NDEV - 1)),
                        pltpu.SemaphoreType.DMA((n,))],
    )(*arrays)


def _pair_start(parts, name):
    n = len(parts)
    lands = [_in_hbm(lax.empty(p.shape[1:], p.dtype)) for p in parts]

    def body(*refs):
        ins, zones = refs[:n], refs[n:2 * n]
        send_sems, recv_sems = refs[2 * n:2 * n + 2]
        x, y, c = _position()
        for j in range(n):
            _remote(ins[j].at[1 - c], zones[j], send_sems.at[j], recv_sems.at[j], (x, y, 1 - c)).start()

    outs = pl.pallas_call(
        body, name=name,
        out_shape=(pltpu.SemaphoreType.DMA((n,)), pltpu.SemaphoreType.DMA((n,)),
                   *[pltpu.HBM(p.shape, p.dtype) for p in parts], *[pltpu.HBM(z.shape, z.dtype) for z in lands]),
        in_specs=[IN_HBM] * (2 * n), out_specs=(SEMAPHORES, SEMAPHORES, *[IN_HBM] * (2 * n)),
        input_output_aliases={j: 2 + j for j in range(2 * n)},
        compiler_params=pltpu.CompilerParams(has_side_effects=IN_FLIGHT),
    )(*[_in_hbm(p) for p in parts], *lands)
    return outs[0], outs[1], list(outs[2:2 + n]), list(outs[2 + n:])


def _pair_end(parts, zones, send_sems, recv_sems, after, name):
    n = len(parts)

    def body(*refs):
        ins, zs = refs[:n], refs[n:2 * n]
        s, r = refs[2 * n:2 * n + 2]
        me = _position()
        for j in range(n):
            copy = _remote(ins[j].at[0], zs[j], s.at[j], r.at[j], me)
            copy.wait_send()
            copy.wait_recv()

    outs = pl.pallas_call(
        body, name=name,
        out_shape=(*[pltpu.HBM(p.shape, p.dtype) for p in parts], *[pltpu.HBM(z.shape, z.dtype) for z in zones]),
        in_specs=[IN_HBM] * (2 * n) + [SEMAPHORES, SEMAPHORES, ANY], out_specs=tuple([IN_HBM] * (2 * n)),
        input_output_aliases={j: j for j in range(2 * n)},
        compiler_params=pltpu.CompilerParams(has_side_effects=IN_FLIGHT),
    )(*parts, *zones, send_sems, recv_sems, after)
    return list(outs[:n]), list(outs[n:])


def _pair_sum(part, from_sibling, core, name):
    ncol = part.shape[-1]
    p3 = part.reshape(2, -1, ncol)
    q2 = from_sibling.reshape(-1, ncol)
    nrows = q2.shape[0]
    tr = 512

    def body(core_ref, p_ref, q_ref, o_ref):
        o_ref[...] = (p_ref[...].astype(F32) + q_ref[...].astype(F32)).astype(BF16)

    grid_spec = pltpu.PrefetchScalarGridSpec(
        num_scalar_prefetch=1, grid=(nrows // tr,),
        in_specs=[pl.BlockSpec((None, tr, ncol), lambda i, core_ref: (core_ref[0], i, 0)),
                  pl.BlockSpec((tr, ncol), lambda i, core_ref: (i, 0))],
        out_specs=pl.BlockSpec((tr, ncol), lambda i, core_ref: (i, 0)))
    out = pl.pallas_call(body, name=name, grid_spec=grid_spec, out_shape=jax.ShapeDtypeStruct(q2.shape, BF16),
                         compiler_params=_params(("arbitrary",)))(core, p3, q2)
    return out.reshape(from_sibling.shape)


def _in_hbm(a):
    return pltpu.with_memory_space_constraint(a, pltpu.HBM)


def _chips(x, y):
    return [(1 - x, y), (x, 1 - y), (1 - x, 1 - y)]


def _to_zone(a, wl, me, dtype, name):
    _, rows, cols = a.shape
    tr = 256 if rows % 256 == 0 else rows

    def body(me_ref, a_ref, o_ref):
        o_ref[...] = a_ref[...].astype(dtype)

    grid_spec = pltpu.PrefetchScalarGridSpec(
        num_scalar_prefetch=1, grid=(rows // tr,),
        in_specs=[pl.BlockSpec((None, tr, cols), lambda i, me_ref: (wl, i, 0))],
        out_specs=pl.BlockSpec((None, tr, cols), lambda i, me_ref: (me_ref[0], i, 0)))
    return pl.pallas_call(body, name=name, grid_spec=grid_spec, out_shape=jax.ShapeDtypeStruct((NDEV, rows, cols), dtype),
                          compiler_params=_params(("arbitrary",)))(me, a)


def _gather_start(zones, after, carried, name):
    n, m = len(zones), len(carried)

    def body(*refs):
        zs = refs[:n]
        send_sems, recv_sems = refs[n + m + len(after):n + m + len(after) + 2]
        x, y, c = _position()
        me = _index((x, y, c))
        for j in range(n):
            mine = zs[j].at[me]
            _remote(mine, mine, send_sems.at[4 * j], recv_sems.at[4 * j], (x, y, 1 - c)).start()
            for q, chip in enumerate(_chips(x, y)):
                _remote(mine, mine, send_sems.at[4 * j + 1 + q], recv_sems.at[4 * j + 1 + q], (*chip, c)).start()

    outs = pl.pallas_call(
        body, name=name,
        out_shape=(pltpu.SemaphoreType.DMA((4 * n,)), pltpu.SemaphoreType.DMA((4 * n,)),
                   *[pltpu.HBM(z.shape, z.dtype) for z in zones], *[jax.ShapeDtypeStruct(a.shape, a.dtype) for a in carried]),
        in_specs=[IN_HBM] * n + [ANY] * (m + len(after)),
        out_specs=(SEMAPHORES, SEMAPHORES, *[IN_HBM] * n, *[ANY] * m),
        input_output_aliases={j: 2 + j for j in range(n + m)},
        compiler_params=pltpu.CompilerParams(has_side_effects=IN_FLIGHT),
    )(*[_in_hbm(z) for z in zones], *carried, *after)
    return outs[0], outs[1], list(outs[2:2 + n]), list(outs[2 + n:])


def _gather_mid(zones, recv_sems, after, name):
    n = len(zones)

    def body(*refs):
        zs, first_recv = refs[:n], refs[n]
        send_sems, recv_sems2 = refs[n + 2:n + 4]
        x, y, c = _position()
        for q, chip in enumerate(_chips(x, y)):
            for j in range(n):
                rows = zs[j].at[_index((*chip, c))]
                _remote(rows, rows, send_sems.at[3 * j + q], first_recv.at[4 * j + 1 + q], (x, y, c)).wait_recv()
                _remote(rows, rows, send_sems.at[3 * j + q], recv_sems2.at[3 * j + q], (x, y, 1 - c)).start()

    outs = pl.pallas_call(
        body, name=name,
        out_shape=(pltpu.SemaphoreType.DMA((3 * n,)), pltpu.SemaphoreType.DMA((3 * n,)),
                   *[pltpu.HBM(z.shape, z.dtype) for z in zones]),
        in_specs=[IN_HBM] * n + [SEMAPHORES, ANY], out_specs=(SEMAPHORES, SEMAPHORES, *[IN_HBM] * n),
        input_output_aliases={j: 2 + j for j in range(n)},
        compiler_params=pltpu.CompilerParams(has_side_effects=IN_FLIGHT),
    )(*zones, recv_sems, after)
    return outs[0], outs[1], list(outs[2:])


def _gather_end(zones, send1, recv1, send2, recv2, after, name):
    n = len(zones)

    def body(*refs):
        zs = refs[:n]
        s1, r1, s2, r2 = refs[n:n + 4]
        x, y, c = _position()
        me = (x, y, c)
        for j in range(n):
            rows = zs[j].at[_index((x, y, 1 - c))]
            _remote(rows, rows, s1.at[4 * j], r1.at[4 * j], me).wait_recv()
            for q, chip in enumerate(_chips(x, y)):
                rows = zs[j].at[_index((*chip, 1 - c))]
                _remote(rows, rows, s2.at[3 * j + q], r2.at[3 * j + q], me).wait_recv()
        for j in range(n):
            block = zs[j].at[0]
            for k in range(4):
                _remote(block, block, s1.at[4 * j + k], r1.at[4 * j + k], me).wait_send()
            for q in range(3):
                _remote(block, block, s2.at[3 * j + q], r2.at[3 * j + q], me).wait_send()

    outs = pl.pallas_call(
        body, name=name, out_shape=tuple(pltpu.HBM(z.shape, z.dtype) for z in zones),
        in_specs=[IN_HBM] * n + [SEMAPHORES] * 4 + [ANY], out_specs=tuple([IN_HBM] * n),
        input_output_aliases={j: j for j in range(n)},
        compiler_params=pltpu.CompilerParams(has_side_effects=IN_FLIGHT),
    )(*zones, send1, recv1, send2, recv2, after)
    return list(outs)


def _scatter_start(parts, name):
    n = len(parts)
    lands = [_in_hbm(lax.empty((3,) + p.shape[1:], p.dtype)) for p in parts]

    def body(*refs):
        ins, zones = refs[:n], refs[n:2 * n]
        send_sems, recv_sems = refs[2 * n:2 * n + 2]
        x, y, c = _position()
        for j in range(n):
            for q, (px, py) in enumerate(_chips(x, y)):
                _remote(ins[j].at[2 * px + py], zones[j].at[q], send_sems.at[3 * j + q], recv_sems.at[3 * j + q],
                        (px, py, c)).start()

    outs = pl.pallas_call(
        body, name=name,
        out_shape=(pltpu.SemaphoreType.DMA((3 * n,)), pltpu.SemaphoreType.DMA((3 * n,)),
                   *[pltpu.HBM(p.shape, p.dtype) for p in parts], *[pltpu.HBM(z.shape, z.dtype) for z in lands]),
        in_specs=[IN_HBM] * (2 * n), out_specs=(SEMAPHORES, SEMAPHORES, *[IN_HBM] * (2 * n)),
        input_output_aliases={j: 2 + j for j in range(2 * n)},
        compiler_params=pltpu.CompilerParams(has_side_effects=IN_FLIGHT),
    )(*[_in_hbm(p) for p in parts], *lands)
    return outs[0], outs[1], list(outs[2:2 + n]), list(outs[2 + n:])


def _scatter_end(parts, zones, send_sems, recv_sems, after, name):
    n = len(parts)

    def body(*refs):
        ins, zs = refs[:n], refs[n:2 * n]
        s, r = refs[2 * n:2 * n + 2]
        me = _position()
        for j in range(n):
            for q in range(3):
                copy = _remote(ins[j].at[0], zs[j].at[q], s.at[3 * j + q], r.at[3 * j + q], me)
                copy.wait_send()
                copy.wait_recv()

    outs = pl.pallas_call(
        body, name=name,
        out_shape=(*[pltpu.HBM(p.shape, p.dtype) for p in parts], *[pltpu.HBM(z.shape, z.dtype) for z in zones]),
        in_specs=[IN_HBM] * (2 * n) + [SEMAPHORES, SEMAPHORES, ANY], out_specs=tuple([IN_HBM] * (2 * n)),
        input_output_aliases={j: j for j in range(2 * n)},
        compiler_params=pltpu.CompilerParams(has_side_effects=IN_FLIGHT),
    )(*parts, *zones, send_sems, recv_sems, after)
    return list(outs[:n]), list(outs[n:])


def _ada_forward(c, ada_w, ada_b):
    def body(c_ref, w_ref, b_ref, cact_ref, mod_ref, gbuf, modrow, send_sems, recv_sems):
        pos = _position()
        me = _index(pos)

        def to_all(ref, round_):
            copies = []
            for k in range(1, NDEV):
                peer, _ = _peer(pos, k)
                copy = pltpu.make_async_remote_copy(
                    src_ref=ref.at[me], dst_ref=ref.at[me], send_sem=send_sems.at[round_, k - 1],
                    recv_sem=recv_sems.at[round_, k - 1], device_id=peer, device_id_type=pl.DeviceIdType.MESH)
                copy.start()
                copies.append(copy)
            for copy in copies:
                copy.wait()

        cact_ref[me] = _silu(c_ref[...])
        to_all(cact_ref, 0)
        rows = lax.broadcasted_iota(jnp.int32, (NDEV, D), 0)
        cact = jnp.zeros((NDEV, D), F32)
        for e in range(NDEV):
            cact = jnp.where(rows == e, cact_ref[e], cact)
        cact = cact.astype(BF16)
        for l in range(DEPTH):
            gbuf[me, l] = jnp.dot(cact, w_ref[l].astype(BF16), preferred_element_type=F32)
        to_all(gbuf, 1)
        mine = lax.broadcasted_iota(jnp.int32, (NDEV, ADA_NC), 0) == me
        for l in range(DEPTH):
            for d in range(NDEV):
                modrow[:, d * ADA_NC:(d + 1) * ADA_NC] = jnp.sum(jnp.where(mine, gbuf[d, l], 0.0), axis=0, keepdims=True)
            full = modrow[...] + b_ref[l:l + 1, :]
            for w in range(3):
                mod_ref[l, w] = full[:, w * D:(w + 1) * D]

    return pl.pallas_call(
        body, name="ada_forward",
        out_shape=[jax.ShapeDtypeStruct((NDEV, 1, D), F32), jax.ShapeDtypeStruct((DEPTH, 3, 1, D), F32)],
        in_specs=[VMEM_FULL] * 3, out_specs=[VMEM_FULL] * 2,
        scratch_shapes=[pltpu.VMEM((NDEV, DEPTH, NDEV, ADA_NC), F32), pltpu.VMEM((1, 3 * D), F32),
                        pltpu.SemaphoreType.DMA((2, NDEV - 1)), pltpu.SemaphoreType.DMA((2, NDEV - 1))],
        compiler_params=_params(),
    )(c, ada_w, ada_b)


def _mod_spec(layer, which, ngrid):
    index = {1: lambda i: (layer, which, 0, 0), 2: lambda i, j: (layer, which, 0, 0)}[ngrid]
    return pl.BlockSpec((None, None, 1, D), index)


def _norm_proj(x, mod, norm_g3, wg, layer, name):
    nb = wg.shape[-1]
    tm = 1024

    def body(x_ref, g_ref, shift_ref, scale_ref, w_ref, h_ref, p_ref):
        @pl.when(pl.program_id(1) == 0)
        def _():
            xv = x_ref[...]
            r = lax.rsqrt(jnp.mean(xv * xv, axis=-1, keepdims=True) + EPS)
            hn = xv * r * g_ref[...]
            h_ref[...] = (hn * (1.0 + scale_ref[...]) + shift_ref[...]).astype(BF16)

        p_ref[...] = jnp.dot(h_ref[...], w_ref[...], preferred_element_type=F32).astype(BF16)

    return pl.pallas_call(
        body, name=name, grid=(S // tm, NDEV),
        out_shape=[jax.ShapeDtypeStruct((S, D), BF16), jax.ShapeDtypeStruct((S, NDEV * nb), BF16)],
        in_specs=[pl.BlockSpec((tm, D), lambda i, d: (i, 0)),
                  pl.BlockSpec((None, 1, D), lambda i, d: (layer, 0, 0)),
                  _mod_spec(layer, 0, 2), _mod_spec(layer, 1, 2),
                  pl.BlockSpec((None, D, nb), lambda i, d: (d, 0, 0))],
        out_specs=[pl.BlockSpec((tm, D), lambda i, d: (i, 0)), pl.BlockSpec((tm, nb), lambda i, d: (i, d))],
        compiler_params=_params(("arbitrary", "arbitrary")),
    )(x, norm_g3, mod, mod, wg)


def _out_proj(ycat, wg_out, x, mod, layer, name):
    tm = 512
    rb = wg_out.shape[1]

    def body(y_ref, w_ref, x_ref, gate_ref, xn_ref, o_ref):
        acc = jnp.zeros((tm, D), F32)
        for d in range(NDEV):
            acc = acc + jnp.dot(y_ref[:, d * rb:(d + 1) * rb], w_ref[d], preferred_element_type=F32)
        o_ref[...] = acc.astype(BF16)
        xn_ref[...] = x_ref[...] + gate_ref[...] * acc

    return pl.pallas_call(
        body, name=name, grid=(S // tm,),
        out_shape=[jax.ShapeDtypeStruct((S, D), F32), jax.ShapeDtypeStruct((S, D), BF16)],
        in_specs=[pl.BlockSpec((tm, NDEV * rb), lambda i: (i, 0)),
                  pl.BlockSpec((NDEV, rb, D), lambda i: (0, 0, 0)),
                  pl.BlockSpec((tm, D), lambda i: (i, 0)), _mod_spec(layer, 2, 1)],
        out_specs=[pl.BlockSpec((tm, D), lambda i: (i, 0))] * 2,
        compiler_params=_params(("arbitrary",)),
    )(ycat, wg_out, x, mod)


def _final_loss(x, target, final_g2):
    tm = 256

    def body(x_ref, t_ref, g_ref, dx_ref, loss_ref, dg_ref):
        @pl.when(pl.program_id(0) == 0)
        def _():
            loss_ref[...] = jnp.zeros_like(loss_ref)
            dg_ref[...] = jnp.zeros_like(dg_ref)

        xv, g = x_ref[...], g_ref[...]
        r = lax.rsqrt(jnp.mean(xv * xv, axis=-1, keepdims=True) + EPS)
        xn = xv * r
        err = xn * g - t_ref[...]
        loss_ref[...] += 0.5 * jnp.sum(jnp.mean(err * err, axis=-1, keepdims=True), axis=0, keepdims=True)
        dy = err * (1.0 / D)
        dg_ref[...] += jnp.sum(dy * xn, axis=0, keepdims=True)
        u = dy * g
        dx_ref[...] = r * (u - xn * jnp.mean(xn * u, axis=-1, keepdims=True))

    tile = pl.BlockSpec((tm, D), lambda i: (i, 0))
    row = pl.BlockSpec((1, D), lambda i: (0, 0))
    return pl.pallas_call(
        body, name="final_loss", grid=(S // tm,),
        out_shape=[jax.ShapeDtypeStruct((S, D), F32), jax.ShapeDtypeStruct((1, LANE), F32), jax.ShapeDtypeStruct((1, D), F32)],
        in_specs=[tile, tile, row], out_specs=[tile, pl.BlockSpec((1, LANE), lambda i: (0, 0)), row],
        compiler_params=_params(("arbitrary",)),
    )(x, target, final_g2)


def _dout(dx, out, mod, layer, name):
    tm = 256

    def body(dx_ref, o_ref, gate_ref, do_ref, dgate_ref):
        @pl.when(pl.program_id(0) == 0)
        def _():
            dgate_ref[...] = jnp.zeros_like(dgate_ref)

        dxv = dx_ref[...]
        do_ref[...] = (gate_ref[...] * dxv).astype(BF16)
        dgate_ref[...] += jnp.sum(dxv * o_ref[...].astype(F32), axis=0, keepdims=True)

    tile = pl.BlockSpec((tm, D), lambda i: (i, 0))
    return pl.pallas_call(
        body, name=name, grid=(S // tm,),
        out_shape=[jax.ShapeDtypeStruct((S, D), BF16), jax.ShapeDtypeStruct((1, D), F32)],
        in_specs=[tile, tile, _mod_spec(layer, 2, 1)], out_specs=[tile, pl.BlockSpec((1, D), lambda i: (0, 0))],
        compiler_params=_params(("arbitrary",)),
    )(dx, out, mod)


def _dycat(d_out, wg_out, name):
    tm = 512
    rb = wg_out.shape[1]

    def body(do_ref, w_ref, dy_ref):
        dov = do_ref[...]
        for d in range(NDEV):
            dy_ref[:, d * rb:(d + 1) * rb] = lax.dot_general(
                dov, w_ref[d], (((1,), (1,)), ((), ())), preferred_element_type=F32).astype(BF16)

    return pl.pallas_call(
        body, name=name, grid=(S // tm,), out_shape=jax.ShapeDtypeStruct((S, NDEV * rb), BF16),
        in_specs=[pl.BlockSpec((tm, D), lambda i: (i, 0)), pl.BlockSpec((NDEV, rb, D), lambda i: (0, 0, 0))],
        out_specs=pl.BlockSpec((tm, NDEV * rb), lambda i: (i, 0)),
        compiler_params=_params(("arbitrary",)),
    )(d_out, wg_out)


def _weight_grad(a, b, name):
    a_blocks = a.shape[1] != D
    ka = a.shape[1] // NDEV if a_blocks else a.shape[1]
    nb = b.shape[1] if a_blocks else b.shape[1] // NDEV

    def body(a_ref, b_ref, o_ref):
        o_ref[...] = lax.dot_general(a_ref[...], b_ref[...], (((0,), (0,)), ((), ())),
                                     preferred_element_type=F32).astype(BF16)

    return pl.pallas_call(
        body, name=name, grid=(NDEV,), out_shape=jax.ShapeDtypeStruct((2, NDEV // 2, ka, nb), BF16),
        in_specs=[pl.BlockSpec((S, ka), (lambda d: (0, d)) if a_blocks else (lambda d: (0, 0))),
                  pl.BlockSpec((S, nb), (lambda d: (0, 0)) if a_blocks else (lambda d: (0, d)))],
        out_specs=pl.BlockSpec((None, None, ka, nb), lambda d: (d % 2, d // 2, 0, 0)),
        compiler_params=_params(("arbitrary",)),
    )(a, b)


def _dh(d_proj, wg, carried, name):
    nb = wg.shape[-1]
    tm = 1024

    def body(dp_ref, w_ref, carried_ref, dh_ref, carried_out):
        part = lax.dot_general(dp_ref[...], w_ref[...], (((1,), (1,)), ((), ())), preferred_element_type=F32)

        @pl.when(pl.program_id(1) == 0)
        def _():
            dh_ref[...] = part

        @pl.when(pl.program_id(1) != 0)
        def _():
            dh_ref[...] += part

    return pl.pallas_call(
        body, name=name, grid=(S // tm, NDEV),
        out_shape=[jax.ShapeDtypeStruct((S, D), F32), jax.ShapeDtypeStruct(carried.shape, carried.dtype)],
        in_specs=[pl.BlockSpec((tm, nb), lambda i, d: (i, d)), pl.BlockSpec((None, D, nb), lambda i, d: (d, 0, 0)), ANY],
        out_specs=[pl.BlockSpec((tm, D), lambda i, d: (i, 0)), ANY],
        input_output_aliases={2: 1},
        compiler_params=_params(("arbitrary", "arbitrary")),
    )(d_proj, wg, carried)


def _norm_bwd(d_h, x, dx, mod, norm_g3, layer, carried, name):
    tm = 256

    def body(dh_ref, x_ref, dx_ref, g_ref, scale_ref, carried_ref, dxi_ref, dshift_ref, dscale_ref, dg_ref, carried_out):
        @pl.when(pl.program_id(0) == 0)
        def _():
            dshift_ref[...] = jnp.zeros_like(dshift_ref)
            dscale_ref[...] = jnp.zeros_like(dscale_ref)
            dg_ref[...] = jnp.zeros_like(dg_ref)

        xv, dhv, g = x_ref[...], dh_ref[...], g_ref[...]
        r = lax.rsqrt(jnp.mean(xv * xv, axis=-1, keepdims=True) + EPS)
        xn = xv * r
        dshift_ref[...] += jnp.sum(dhv, axis=0, keepdims=True)
        dscale_ref[...] += jnp.sum(dhv * (xn * g), axis=0, keepdims=True)
        dhn = dhv * (1.0 + scale_ref[...])
        dg_ref[...] += jnp.sum(dhn * xn, axis=0, keepdims=True)
        u = dhn * g
        dxi_ref[...] = dx_ref[...] + r * (u - xn * jnp.mean(xn * u, axis=-1, keepdims=True))

    tile = pl.BlockSpec((tm, D), lambda i: (i, 0))
    row = pl.BlockSpec((1, D), lambda i: (0, 0))
    return pl.pallas_call(
        body, name=name, grid=(S // tm,),
        out_shape=[jax.ShapeDtypeStruct((S, D), F32)] + [jax.ShapeDtypeStruct((1, D), F32)] * 3
        + [jax.ShapeDtypeStruct(carried.shape, carried.dtype)],
        in_specs=[tile, tile, tile, pl.BlockSpec((None, 1, D), lambda i: (layer, 0, 0)), _mod_spec(layer, 1, 1), ANY],
        out_specs=[tile, row, row, row, ANY], input_output_aliases={5: 4}, compiler_params=_params(("arbitrary",)),
    )(d_h, x, dx, norm_g3, mod, carried)


TS = 256
NCH = TS // CHUNK
HALO_BLOCKS = TS // HALO


def _halo_before(width, col_block):
    return pl.BlockSpec((HALO, width), lambda i: (jnp.maximum(i * HALO_BLOCKS - 1, 0), col_block))


def _halo_after(width, col_block):
    return pl.BlockSpec((HALO, width), lambda i: (jnp.minimum((i + 1) * HALO_BLOCKS, S // HALO - 1), col_block))


def _shift_down(ext, k):
    return pltpu.roll(ext, k, 0)[HALO:]


def _shift_up(ext, k):
    return pltpu.roll(ext, ext.shape[0] - k, 0)[:ext.shape[0] - HALO]


def _layer_norm_head(v, lg, lb):
    mu = jnp.mean(v, axis=-1, keepdims=True)
    vc = v - mu
    rstd = lax.rsqrt(jnp.mean(vc * vc, axis=-1, keepdims=True) + EPS)
    vhat = vc * rstd
    return vhat, rstd, vhat * lg + lb


def _causal_mask():
    return lax.broadcasted_iota(jnp.int32, (CHUNK, CHUNK), 0) >= lax.broadcasted_iota(jnp.int32, (CHUNK, CHUNK), 1)


def _even_mix_fwd(proj, convw, ln_g3, ln_b3, sgu_w, sgu_bcol, wl, name):
    def body(pj_ref, hh_ref, hc_ref, cw_ref, lg_ref, lb_ref, sw_ref, sb_ref, y_ref):
        live = (pl.program_id(0) > 0).astype(F32)
        causal = _causal_mask()
        for j in range(E_A // HEAD):
            cols = slice(j * HEAD, (j + 1) * HEAD)
            w0, w1, w2 = cw_ref[0:1, cols], cw_ref[1:2, cols], cw_ref[2:3, cols]
            lg, lb = lg_ref[:, cols], lb_ref[:, cols]
            wm = jnp.where(causal, sw_ref[j], 0.0).astype(BF16)
            bias = sb_ref[j]

            def split(s, rows, cols=cols):
                return pj_ref[rows, s * E_A + cols.start:s * E_A + cols.stop].astype(F32)

            prev_tail = hc_ref[:, cols].astype(F32) * hh_ref[:, cols].astype(F32) * live
            for n in range(NCH):
                rows = slice(n * CHUNK, (n + 1) * CHUNK)
                p = split(2, rows) * split(0, rows)
                ext = jnp.concatenate([prev_tail, p], axis=0)
                prev_tail = p[CHUNK - HALO:]
                cv = w2 * p + w1 * _shift_down(ext, 1) + w0 * _shift_down(ext, 2)
                y_ref[rows, cols] = (split(1, rows) * cv * _silu(split(3, rows))).astype(BF16)
                _, _, vn = _layer_norm_head(split(5, rows), lg, lb)
                mixed = jnp.dot(wm, vn.astype(BF16), preferred_element_type=F32) + bias
                y_ref[rows, E_A + cols.start:E_A + cols.stop] = (split(4, rows) * mixed * _silu(split(6, rows))).astype(BF16)

    const3 = lambda i: (wl, 0, 0)
    const4 = lambda i: (wl, 0, 0, 0)
    return pl.pallas_call(
        body, name=name, grid=(S // TS,), out_shape=jax.ShapeDtypeStruct((S, 2 * E_A), BF16),
        in_specs=[pl.BlockSpec((TS, 7 * E_A), lambda i: (i, 0)), _halo_before(E_A, 0), _halo_before(E_A, 2),
                  pl.BlockSpec((None, 3, E_A), const3), pl.BlockSpec((None, 1, E_A), const3),
                  pl.BlockSpec((None, 1, E_A), const3), pl.BlockSpec((None, NDEV, CHUNK, CHUNK), const4),
                  pl.BlockSpec((None, NDEV, CHUNK, 1), const4)],
        out_specs=pl.BlockSpec((TS, 2 * E_A), lambda i: (i, 0)),
        compiler_params=_params(("arbitrary",)),
    )(proj, proj, proj, convw, ln_g3, ln_b3, sgu_w, sgu_bcol)


def _even_mix_bwd(proj, d_ycat, convw, ln_g3, ln_b3, sgu_w, sgu_bcol, wl, name):
    nsteps = S // TS

    def body(pj_ref, hh_ref, hc_ref, hb_ref, hz_ref, dy_ref, hdy_ref, cw_ref, lg_ref, lb_ref, sw_ref, sb_ref,
             dp_ref, dcw_ref, dlg_ref, dlb_ref, dsw_ref, dsb_ref):
        step = pl.program_id(0)

        @pl.when(step == 0)
        def _():
            for ref in (dcw_ref, dlg_ref, dlb_ref, dsw_ref, dsb_ref):
                ref[...] = jnp.zeros_like(ref)

        live_before = (step > 0).astype(F32)
        live_after = (step < nsteps - 1).astype(F32)
        causal = _causal_mask()
        for j in range(E_A // HEAD):
            cols = slice(j * HEAD, (j + 1) * HEAD)
            w0, w1, w2 = cw_ref[0:1, cols], cw_ref[1:2, cols], cw_ref[2:3, cols]
            lg, lb = lg_ref[:, cols], lb_ref[:, cols]
            wmf = jnp.where(causal, sw_ref[j], 0.0)
            wm, wmt = wmf.astype(BF16), wmf.T.astype(BF16)
            bias = sb_ref[j]

            def split(s, rows, cols=cols):
                return pj_ref[rows, s * E_A + cols.start:s * E_A + cols.stop].astype(F32)

            def put(s, rows, val, cols=cols):
                dp_ref[rows, s * E_A + cols.start:s * E_A + cols.stop] = val.astype(BF16)

            ps = [split(2, slice(n * CHUNK, (n + 1) * CHUNK)) * split(0, slice(n * CHUNK, (n + 1) * CHUNK)) for n in range(NCH)]
            next_head = (hdy_ref[:, cols].astype(F32) * hb_ref[:, cols].astype(F32) * _silu(hz_ref[:, cols].astype(F32))
                         * live_after)
            acc_w = [jnp.zeros((1, HEAD), F32) for _ in range(3)]
            for n in reversed(range(NCH)):
                rows = slice(n * CHUNK, (n + 1) * CHUNK)
                p = ps[n]
                tail = ps[n - 1][CHUNK - HALO:] if n > 0 else hc_ref[:, cols].astype(F32) * hh_ref[:, cols].astype(F32) * live_before
                ext = jnp.concatenate([tail, p], axis=0)
                p1, p2 = _shift_down(ext, 1), _shift_down(ext, 2)
                cv = w2 * p + w1 * p1 + w0 * p2
                a_b, a_z = split(1, rows), split(3, rows)
                sz, dsz = _silu_and_grad(a_z)
                dya = dy_ref[rows, cols].astype(F32)
                put(1, rows, dya * cv * sz)
                put(3, rows, dya * a_b * cv * dsz)
                gcv = dya * a_b * sz
                acc_w[0] += jnp.sum(gcv * p2, axis=0, keepdims=True)
                acc_w[1] += jnp.sum(gcv * p1, axis=0, keepdims=True)
                acc_w[2] += jnp.sum(gcv * p, axis=0, keepdims=True)
                gext = jnp.concatenate([gcv, next_head], axis=0)
                next_head = gcv[:HALO]
                dpv = w2 * gcv + w1 * _shift_up(gext, 1) + w0 * _shift_up(gext, 2)
                put(2, rows, dpv * split(0, rows))
                put(0, rows, dpv * split(2, rows))
            for k in range(3):
                dcw_ref[k:k + 1, cols] += acc_w[k]

            acc_lg, acc_lb = jnp.zeros((1, HEAD), F32), jnp.zeros((1, HEAD), F32)
            acc_sw, acc_sb = jnp.zeros((CHUNK, CHUNK), F32), jnp.zeros((CHUNK, 1), F32)
            for n in range(NCH):
                rows = slice(n * CHUNK, (n + 1) * CHUNK)
                u, z = split(4, rows), split(6, rows)
                vhat, rstd, vn = _layer_norm_head(split(5, rows), lg, lb)
                vn16 = vn.astype(BF16)
                mixed = jnp.dot(wm, vn16, preferred_element_type=F32) + bias
                sz, dsz = _silu_and_grad(z)
                dyb = dy_ref[rows, E_A + cols.start:E_A + cols.stop].astype(F32)
                put(4, rows, dyb * mixed * sz)
                put(6, rows, dyb * u * mixed * dsz)
                dmix = dyb * u * sz
                dmix16 = dmix.astype(BF16)
                acc_sb += jnp.sum(dmix, axis=1, keepdims=True)
                acc_sw += lax.dot_general(dmix16, vn16, (((1,), (1,)), ((), ())), preferred_element_type=F32)
                dvn = jnp.dot(wmt, dmix16, preferred_element_type=F32)
                acc_lg += jnp.sum(dvn * vhat, axis=0, keepdims=True)
                acc_lb += jnp.sum(dvn, axis=0, keepdims=True)
                dvh = dvn * lg
                put(5, rows, rstd * (dvh - jnp.mean(dvh, axis=-1, keepdims=True)
                                     - vhat * jnp.mean(dvh * vhat, axis=-1, keepdims=True)))
            dlg_ref[:, cols] += acc_lg
            dlb_ref[:, cols] += acc_lb
            dsw_ref[j] += jnp.where(causal, acc_sw, 0.0)
            dsb_ref[j] += acc_sb

    const3 = lambda i: (wl, 0, 0)
    const4 = lambda i: (wl, 0, 0, 0)
    fixed2 = lambda i: (0, 0)
    fixed3 = lambda i: (0, 0, 0)
    return pl.pallas_call(
        body, name=name, grid=(nsteps,),
        out_shape=[jax.ShapeDtypeStruct((S, 7 * E_A), BF16), jax.ShapeDtypeStruct((3, E_A), F32),
                   jax.ShapeDtypeStruct((1, E_A), F32), jax.ShapeDtypeStruct((1, E_A), F32),
                   jax.ShapeDtypeStruct((NDEV, CHUNK, CHUNK), F32), jax.ShapeDtypeStruct((NDEV, CHUNK, 1), F32)],
        in_specs=[pl.BlockSpec((TS, 7 * E_A), lambda i: (i, 0)), _halo_before(E_A, 0), _halo_before(E_A, 2),
                  _halo_after(E_A, 1), _halo_after(E_A, 3),
                  pl.BlockSpec((TS, 2 * E_A), lambda i: (i, 0)), _halo_after(E_A, 0),
                  pl.BlockSpec((None, 3, E_A), const3), pl.BlockSpec((None, 1, E_A), const3),
                  pl.BlockSpec((None, 1, E_A), const3), pl.BlockSpec((None, NDEV, CHUNK, CHUNK), const4),
                  pl.BlockSpec((None, NDEV, CHUNK, 1), const4)],
        out_specs=[pl.BlockSpec((TS, 7 * E_A), lambda i: (i, 0)), pl.BlockSpec((3, E_A), fixed2),
                   pl.BlockSpec((1, E_A), fixed2), pl.BlockSpec((1, E_A), fixed2),
                   pl.BlockSpec((NDEV, CHUNK, CHUNK), fixed3), pl.BlockSpec((NDEV, CHUNK, 1), fixed3)],
        compiler_params=_params(("arbitrary",)),
    )(proj, proj, proj, proj, proj, d_ycat, d_ycat, convw, ln_g3, ln_b3, sgu_w, sgu_bcol)


def _window_count(step, n, win, ext_before):
    rows = CHUNK if ext_before else CHUNK + HALO
    t = step * TS + n * CHUNK + lax.broadcasted_iota(jnp.int32, (rows, 1), 0)
    return jnp.minimum(t + 1, win).astype(F32)


def _pool_weight(wp_ref, g):
    return jnp.concatenate([wp_ref[d, g] for d in range(NDEV)], axis=0)


def _pooled_chunk(p, tail, win, count):
    sums = jnp.concatenate([tail, p], axis=0)
    shift = 1
    while shift < win:
        sums = sums + pltpu.roll(sums, shift, 0)
        shift *= 2
    return sums[HALO:] / count - p


def _pool_mix_fwd(proj, wpool, pscale4, wl, name):
    e_c = 4 * G_C

    def body(pj_ref, hp_ref, wp_ref, ps_ref, y_ref, pooled_scr, yraw_scr):
        step = pl.program_id(0)
        live = (step > 0).astype(F32)
        for g, win in enumerate(POOL_WINDOWS):
            for q in range(G_C // LANE):
                cols = slice(g * G_C + q * LANE, g * G_C + (q + 1) * LANE)
                tail = hp_ref[:, cols].astype(F32) * live
                for n in range(NCH):
                    rows = slice(n * CHUNK, (n + 1) * CHUNK)
                    p = pj_ref[rows, cols].astype(F32)
                    pooled_scr[rows, q * LANE:(q + 1) * LANE] = _pooled_chunk(
                        p, tail, win, _window_count(step, n, win, True)).astype(BF16)
                    tail = p[CHUNK - HALO:]
            yraw_scr[...] = jnp.dot(pooled_scr[...], _pool_weight(wp_ref, g), preferred_element_type=F32)
            for q in range(G_C // LANE):
                cols = slice(g * G_C + q * LANE, g * G_C + (q + 1) * LANE)
                for n in range(NCH):
                    rows = slice(n * CHUNK, (n + 1) * CHUNK)
                    z = pj_ref[rows, e_c + cols.start:e_c + cols.stop].astype(F32)
                    y_ref[rows, cols] = (yraw_scr[rows, q * LANE:(q + 1) * LANE] * ps_ref[:, cols] * _silu(z)).astype(BF16)

    return pl.pallas_call(
        body, name=name, grid=(S // TS,), out_shape=jax.ShapeDtypeStruct((S, e_c), BF16),
        in_specs=[pl.BlockSpec((TS, 2 * e_c), lambda i: (i, 0)), _halo_before(e_c, 0),
                  pl.BlockSpec((NDEV, 4, G_C // NDEV, G_C), lambda i: (0, 0, 0, 0)),
                  pl.BlockSpec((None, 1, e_c), lambda i: (wl, 0, 0))],
        out_specs=pl.BlockSpec((TS, e_c), lambda i: (i, 0)),
        scratch_shapes=[pltpu.VMEM((TS, G_C), BF16), pltpu.VMEM((TS, G_C), F32)],
        compiler_params=_params(("arbitrary",)),
    )(proj, proj, wpool, pscale4)


def _pool_mix_bwd(proj, d_ycat, wpool, pscale4, wl, name):
    e_c = 4 * G_C
    nsteps = S // TS
    rb = G_C // NDEV

    def body(pj_ref, hp_ref, hz_ref, dy_ref, hdy_ref, wp_ref, ps_ref,
             dp_ref, dps_ref, dwp_ref, pooled_scr, yraw_scr, dyraw_scr, dpool_scr, acc_w):
        step = pl.program_id(0)

        @pl.when(step == 0)
        def _():
            dps_ref[...] = jnp.zeros_like(dps_ref)
            acc_w[...] = jnp.zeros_like(acc_w)

        live_before = (step > 0).astype(F32)
        live_after = (step < nsteps - 1).astype(F32)
        for g, win in enumerate(POOL_WINDOWS):
            weight = _pool_weight(wp_ref, g)
            for q in range(G_C // LANE):
                cols = slice(g * G_C + q * LANE, g * G_C + (q + 1) * LANE)
                tail = hp_ref[:, cols].astype(F32) * live_before
                for n in range(NCH):
                    rows = slice(n * CHUNK, (n + 1) * CHUNK)
                    p = pj_ref[rows, cols].astype(F32)
                    pooled_scr[rows, q * LANE:(q + 1) * LANE] = _pooled_chunk(
                        p, tail, win, _window_count(step, n, win, True)).astype(BF16)
                    tail = p[CHUNK - HALO:]
            yraw_scr[...] = jnp.dot(pooled_scr[...], weight, preferred_element_type=F32)
            for q in range(G_C // LANE):
                cols = slice(g * G_C + q * LANE, g * G_C + (q + 1) * LANE)
                local = slice(q * LANE, (q + 1) * LANE)
                scale = ps_ref[:, cols]
                acc_ps = jnp.zeros((1, LANE), F32)
                for n in range(NCH):
                    rows = slice(n * CHUNK, (n + 1) * CHUNK)
                    sz, dsz = _silu_and_grad(pj_ref[rows, e_c + cols.start:e_c + cols.stop].astype(F32))
                    dyv = dy_ref[rows, cols].astype(F32)
                    yraw = yraw_scr[rows, local]
                    dyraw_scr[rows, local] = (dyv * scale * sz).astype(BF16)
                    acc_ps += jnp.sum(dyv * yraw * sz, axis=0, keepdims=True)
                    dp_ref[rows, e_c + cols.start:e_c + cols.stop] = (dyv * yraw * scale * dsz).astype(BF16)
                dps_ref[:, cols] += acc_ps
                dyraw_scr[TS:, local] = (hdy_ref[:, cols].astype(F32) * scale * _silu(hz_ref[:, cols].astype(F32))
                                         * live_after).astype(BF16)
            dpool_scr[...] = lax.dot_general(dyraw_scr[...], weight, (((1,), (1,)), ((), ())), preferred_element_type=F32)
            acc_w[g] += lax.dot_general(pooled_scr[...], dyraw_scr[:TS, :], (((0,), (0,)), ((), ())),
                                        preferred_element_type=F32)
            for q in range(G_C // LANE):
                cols = slice(g * G_C + q * LANE, g * G_C + (q + 1) * LANE)
                local = slice(q * LANE, (q + 1) * LANE)
                for n in range(NCH):
                    rows = slice(n * CHUNK, (n + 1) * CHUNK)
                    ext = dpool_scr[n * CHUNK:(n + 1) * CHUNK + HALO, local]
                    sums = ext / _window_count(step, n, win, False)
                    shift = 1
                    while shift < win:
                        sums = sums + pltpu.roll(sums, CHUNK + HALO - shift, 0)
                        shift *= 2
                    dp_ref[rows, cols] = (sums[:CHUNK] - ext[:CHUNK]).astype(BF16)

        @pl.when(step == nsteps - 1)
        def _():
            for g in range(4):
                for d in range(NDEV):
                    dwp_ref[d % 2, d // 2, g] = acc_w[g, d * rb:(d + 1) * rb, :].astype(BF16)

    in_specs = [pl.BlockSpec((TS, 2 * e_c), lambda i: (i, 0)), _halo_before(e_c, 0), _halo_after(e_c, 1),
                pl.BlockSpec((TS, e_c), lambda i: (i, 0)), _halo_after(e_c, 0),
                pl.BlockSpec((NDEV, 4, rb, G_C), lambda i: (0, 0, 0, 0)),
                pl.BlockSpec((None, 1, e_c), lambda i: (wl, 0, 0))]
    args = [proj, proj, proj, d_ycat, d_ycat, wpool, pscale4]
    return pl.pallas_call(
        body, name=name, grid=(nsteps,),
        out_shape=[jax.ShapeDtypeStruct((S, 2 * e_c), BF16), jax.ShapeDtypeStruct((1, e_c), F32),
                   jax.ShapeDtypeStruct((2, NDEV // 2) + wpool.shape[1:], BF16)],
        in_specs=in_specs,
        out_specs=[pl.BlockSpec((TS, 2 * e_c), lambda i: (i, 0)), pl.BlockSpec((1, e_c), lambda i: (0, 0)),
                   pl.BlockSpec((2, NDEV // 2, 4, rb, G_C), lambda i: (0, 0, 0, 0, 0))],
        scratch_shapes=[pltpu.VMEM((TS, G_C), BF16), pltpu.VMEM((TS, G_C), F32), pltpu.VMEM((TS + HALO, G_C), BF16),
                        pltpu.VMEM((TS + HALO, G_C), F32), pltpu.VMEM((4, G_C, G_C), F32)],
        compiler_params=_params(("arbitrary",)),
    )(*args)


def _adamw(w, g, m, v):
    m = ADAM_B1 * m + (1.0 - ADAM_B1) * g
    v = ADAM_B2 * v + (1.0 - ADAM_B2) * jnp.square(g)
    m_hat = m / (1.0 - ADAM_B1 ** ADAM_STEP)
    v_hat = v / (1.0 - ADAM_B2 ** ADAM_STEP)
    delta = -ADAM_LR * (m_hat / (jnp.sqrt(v_hat) + ADAM_EPS) + ADAM_WD * w)
    return delta, m, v


def _adam_sharded(w, m, v, chip_parts, landed, my_chip, carried, name):
    nl, nr, ncol = w.shape
    tr = 128
    steps = nr // tr
    nc = len(carried)

    def body(chip_ref, w_ref, m_ref, v_ref, *rest):
        parts, zones = rest[:nl], rest[nl:2 * nl]
        g_ref, d_ref, nm_ref, nv_ref = rest[2 * nl + nc:2 * nl + nc + 4]
        layer = pl.program_id(0)
        g = jnp.zeros((tr, ncol), F32)
        for l in range(nl):
            gl = parts[l][...].astype(F32)
            for q in range(3):
                gl = gl + zones[l][q].astype(F32)
            g = jnp.where(layer == l, gl, g)
        g_ref[...] = g
        d_ref[...], nm_ref[...], nv_ref[...] = _adamw(w_ref[...], g, m_ref[...], v_ref[...])

    def rows_of(l):
        return lambda layer, i, chip_ref: jnp.where(layer == l, i, jnp.where(layer < l, 0, steps - 1))

    spec = pl.BlockSpec((None, tr, ncol), lambda layer, i, chip_ref: (layer, i, 0))
    part_specs = [pl.BlockSpec((None, tr, ncol), lambda layer, i, chip_ref, l=l: (chip_ref[0], rows_of(l)(layer, i, chip_ref), 0))
                  for l in range(nl)]
    zone_specs = [pl.BlockSpec((3, tr, ncol), lambda layer, i, chip_ref, l=l: (0, rows_of(l)(layer, i, chip_ref), 0))
                  for l in range(nl)]
    grid_spec = pltpu.PrefetchScalarGridSpec(
        num_scalar_prefetch=1, grid=(nl, steps), in_specs=[spec, spec, spec] + part_specs + zone_specs + [ANY] * nc,
        out_specs=[spec] * 4 + [ANY] * nc)
    return pl.pallas_call(
        body, name=name, grid_spec=grid_spec,
        out_shape=[jax.ShapeDtypeStruct(w.shape, F32)] * 4 + [jax.ShapeDtypeStruct(a.shape, a.dtype) for a in carried],
        input_output_aliases={4 + 2 * nl + k: 4 + k for k in range(nc)},
        compiler_params=_params(("arbitrary", "arbitrary")),
    )(my_chip, w, m, v, *chip_parts, *landed, *carried)


def _adam_small(w, g, m, v, name):
    def body(w_ref, g_ref, m_ref, v_ref, d_ref, nm_ref, nv_ref):
        d_ref[...], nm_ref[...], nv_ref[...] = _adamw(w_ref[...], g_ref[...], m_ref[...], v_ref[...])

    return pl.pallas_call(body, name=name, out_shape=[jax.ShapeDtypeStruct(w.shape, F32)] * 3,
                          in_specs=[VMEM_FULL] * 4, out_specs=[VMEM_FULL] * 3, compiler_params=_params())(w, g, m, v)


def _sum_devices(gathered, name):
    _, nr, ncol = gathered.shape

    def body(g_ref, o_ref):
        acc = g_ref[0]
        for s in range(1, NDEV):
            acc = acc + g_ref[s]
        o_ref[...] = acc

    return pl.pallas_call(body, name=name, grid=(1,), out_shape=jax.ShapeDtypeStruct((nr, ncol), F32),
                          in_specs=[pl.BlockSpec((NDEV, nr, ncol), lambda i: (0, 0, 0))],
                          out_specs=pl.BlockSpec((nr, ncol), lambda i: (0, 0)),
                          compiler_params=_params(("arbitrary",)))(gathered)


def _ada_weight_adam(cact_t, dmod_mine, w, m, v):
    def body(ct_ref, dm_ref, w_ref, m_ref, v_ref, g_ref, d_ref, nm_ref, nv_ref):
        ct, dm = ct_ref[...], dm_ref[...]
        g = ct[:, 0:1] * dm[0:1, :]
        for e in range(1, NDEV):
            g = g + ct[:, e:e + 1] * dm[e:e + 1, :]
        g_ref[...] = g
        d_ref[...], nm_ref[...], nv_ref[...] = _adamw(w_ref[...], g, m_ref[...], v_ref[...])

    spec = pl.BlockSpec((None, D, ADA_NC), lambda l: (l, 0, 0))
    return pl.pallas_call(
        body, name="ada_weight_adam", grid=(DEPTH,), out_shape=[jax.ShapeDtypeStruct(w.shape, F32)] * 4,
        in_specs=[pl.BlockSpec((D, NDEV), lambda l: (0, 0)), pl.BlockSpec((None, NDEV, ADA_NC), lambda l: (l, 0, 0)),
                  spec, spec, spec],
        out_specs=[spec] * 4, compiler_params=_params(("arbitrary",)),
    )(cact_t, dmod_mine, w, m, v)


def _pad_rows(a, rows):
    a = a.reshape(-1, D)
    return jnp.pad(a, ((0, rows - a.shape[0]), (0, 0)))


def kernel(x, c, norm_g, ada_w, ada_b, ab_w_in, ab_conv_w, ab_ln_g, ab_ln_b, ab_sgu_w, ab_sgu_b, ab_w_out, c_w_in, c_pool_w, c_pool_scale, c_w_out, final_g, loss_target, m_norm_g, m_ada_w, m_ada_b, m_ab_w_in, m_ab_conv_w, m_ab_ln_g, m_ab_ln_b, m_ab_sgu_w, m_ab_sgu_b, m_ab_w_out, m_c_w_in, m_c_pool_w, m_c_pool_scale, m_c_w_out, m_final_g, v_norm_g, v_ada_w, v_ada_b, v_ab_w_in, v_ab_conv_w, v_ab_ln_g, v_ab_ln_b, v_ab_sgu_w, v_ab_sgu_b, v_ab_w_out, v_c_w_in, v_c_pool_w, v_c_pool_scale, v_c_w_out, v_final_g):
    x_pos, y_pos, c_pos = _position()
    me = _index((x_pos, y_pos, c_pos))
    core = c_pos.astype(jnp.int32).reshape(1)
    my_chip = (2 * x_pos + y_pos).astype(jnp.int32).reshape(1)
    me1 = me.astype(jnp.int32).reshape(1)
    x0 = x.reshape(S, D)
    target = loss_target.reshape(S, D)
    norm_g3 = norm_g.reshape(DEPTH, 1, D)
    ln_g3, ln_b3 = ab_ln_g.reshape(2, 1, E_A), ab_ln_b.reshape(2, 1, E_A)
    sgu_bcol = ab_sgu_b.reshape(2, NDEV, CHUNK, 1)
    rb = G_C // NDEV
    pool_w3, m_pool_w3, v_pool_w3 = (a.reshape(2, 4 * rb, G_C) for a in (c_pool_w, m_c_pool_w, v_c_pool_w))

    flights = []
    for layer in range(DEPTH):
        wl = layer // 2
        if layer % 2 == 0:
            zones = [_to_zone(ab_w_in, wl, me1, BF16, f"cast_w_in_{layer}"), _to_zone(ab_w_out, wl, me1, BF16, f"cast_w_out_{layer}")]
        else:
            zones = [_to_zone(c_w_in, wl, me1, BF16, f"cast_w_in_{layer}"), _to_zone(c_w_out, wl, me1, BF16, f"cast_w_out_{layer}"),
                     _to_zone(pool_w3, wl, me1, BF16, f"cast_pool_w_{layer}")]
        if layer == 0:
            send1, recv1, zones, (c, c_pool_scale_sent) = _gather_start(zones, [], [c, c_pool_scale], "gather_start_0")
            cact_all, mod = _ada_forward(c, ada_w, ada_b)
            convw_all, pscale_all = _gather([ab_conv_w, c_pool_scale_sent], "gather_small_weights")
            convw = jnp.transpose(convw_all, (1, 2, 0, 3)).reshape(2, 3, E_A)
            pscale4 = jnp.transpose(pscale_all, (1, 0, 2)).reshape(2, 1, 4 * G_C)
        else:
            send1, recv1, zones, (mod,) = _gather_start(zones, [convw_all], [mod], f"gather_start_{layer}")
        flights.append((send1, recv1, zones))

    def finish_gather(layer, passed, after):
        send1, recv1, _ = flights[layer]
        send2, recv2, zones = passed
        wg = _gather_end(zones, send1, recv1, send2, recv2, after, f"gather_end_{layer}")
        return wg[:2] + [w.reshape(NDEV, 4, rb, G_C) for w in wg[2:]]

    xs, hs, projs, ycats, outs, gathered_w = [x0], [], [], [], [], []
    gathered_w.append(finish_gather(0, _gather_mid(flights[0][2], flights[0][1], mod, "gather_mid_0"), mod))
    for layer in range(DEPTH):
        wl = layer // 2
        even = layer % 2 == 0
        wg = gathered_w[layer]
        h, proj = _norm_proj(xs[-1], mod, norm_g3, wg[0], layer, f"norm_proj_{layer}")
        if layer + 1 < DEPTH:
            passed = _gather_mid(flights[layer + 1][2], flights[layer + 1][1], h, f"gather_mid_{layer + 1}")
        if even:
            ycat = _even_mix_fwd(proj, convw, ln_g3, ln_b3, ab_sgu_w, sgu_bcol, wl, f"even_mix_fwd_{layer}")
        else:
            ycat = _pool_mix_fwd(proj, wg[2], pscale4, wl, f"pool_mix_fwd_{layer}")
        x_new, out = _out_proj(ycat, wg[1], xs[-1], mod, layer, f"out_proj_{layer}")
        if layer + 1 < DEPTH:
            gathered_w.append(finish_gather(layer + 1, passed, x_new))
        xs.append(x_new)
        hs.append(h)
        projs.append(proj)
        ycats.append(ycat)
        outs.append(out)

    dx, loss_part, d_final_g = _final_loss(xs[DEPTH], target, final_g.reshape(1, D))
    loss = lax.psum(loss_part[0, 0], ("x", "y", "c"))

    d_mod, d_norm_g = [None] * DEPTH, [None] * DEPTH
    small, scatters, landed, res = {}, {}, {}, {}

    def finish_scatter(layer, after):
        send_sems, recv_sems, chip_parts, zones = scatters[layer]
        landed[layer] = _scatter_end(chip_parts, zones, send_sems, recv_sems, after, f"scatter_end_{layer}")

    def flat(a):
        return a.reshape(a.shape[0], -1, a.shape[-1])

    def sharded_adam(k, j, layers, w, m, v, carried):
        outs4 = _adam_sharded(w, m, v, [flat(landed[l][0][j]) for l in layers], [flat(landed[l][1][j]) for l in layers],
                              my_chip, carried, "adam_" + k)
        res[k] = [o.reshape(c_pool_w.shape) if k == "c_pool_w" else o for o in outs4[:4]]
        return list(outs4[4:])

    for layer in reversed(range(DEPTH)):
        wl = layer // 2
        even = layer % 2 == 0
        wg = gathered_w[layer]
        d_out, d_gate = _dout(dx, outs[layer], mod, layer, f"dout_{layer}")
        d_ycat = _dycat(d_out, wg[1], f"dycat_{layer}")
        parts = [None, _weight_grad(ycats[layer], d_out, f"grad_w_out_{layer}")]
        if even:
            d_proj, d_cw, d_lg, d_lb, d_sw, d_sb = _even_mix_bwd(
                projs[layer], d_ycat, convw, ln_g3, ln_b3, ab_sgu_w, sgu_bcol, wl, f"even_mix_bwd_{layer}")
            small[layer] = (d_cw, d_lg, d_lb, d_sw, d_sb)
        else:
            d_proj, d_ps, d_pool = _pool_mix_bwd(projs[layer], d_ycat, wg[2], pscale4, wl, f"pool_mix_bwd_{layer}")
            small[layer] = (d_ps,)
            parts.append(d_pool)
        parts[0] = _weight_grad(hs[layer], d_proj, f"grad_w_in_{layer}")
        pair_send, pair_recv, parts, from_sibling = _pair_start(parts, f"pair_start_{layer}")
        if layer > 0:
            d_h, parts[0] = _dh(d_proj, wg[0], parts[0], f"dh_{layer}")
            pair_after = d_h
        else:
            finish_scatter(1, d_proj)
            finish_scatter(3, d_proj)
            parts[0], = sharded_adam("c_w_in", 0, (1, 3), c_w_in, m_c_w_in, v_c_w_in, [parts[0]])
            parts[0], = sharded_adam("c_w_out", 1, (1, 3), c_w_out, m_c_w_out, v_c_w_out, [parts[0]])
            parts[0], = sharded_adam("c_pool_w", 2, (1, 3), pool_w3, m_pool_w3, v_pool_w3, [parts[0]])
            pair_after = res["c_pool_w"][0]
        parts, from_sibling = _pair_end(parts, from_sibling, pair_send, pair_recv, pair_after, f"pair_end_{layer}")
        chip_parts = [_pair_sum(p, q, core, f"pair_sum_{layer}_{j}") for j, (p, q) in enumerate(zip(parts, from_sibling))]
        send_sems, recv_sems, chip_parts, zones = _scatter_start(chip_parts, f"scatter_start_{layer}")
        if layer == 0:
            d_h, chip_parts[0] = _dh(d_proj, wg[0], chip_parts[0], f"dh_{layer}")
        dx, d_shift, d_scale, d_norm_g[layer], chip_parts[0] = _norm_bwd(
            d_h, xs[layer], dx, mod, norm_g3, layer, chip_parts[0], f"norm_bwd_{layer}")
        scatters[layer] = (send_sems, recv_sems, chip_parts, zones)
        d_mod[layer] = jnp.concatenate([d_shift, d_scale, d_gate], axis=0)

    sections = [("norm_g", jnp.concatenate(d_norm_g, axis=0), 8),
                ("d_mod", jnp.concatenate(d_mod, axis=0), 16),
                ("ab_ln_g", jnp.concatenate([small[0][1], small[2][1]], axis=0), 8),
                ("ab_ln_b", jnp.concatenate([small[0][2], small[2][2]], axis=0), 8),
                ("ab_sgu_b", jnp.stack([small[0][4], small[2][4]]), 8),
                ("final_g", d_final_g, 8),
                ("ab_conv_w", jnp.stack([small[0][0], small[2][0]]), 8),
                ("c_pool_scale", jnp.concatenate([small[1][0], small[3][0]], axis=0), 8),
                ("ab_sgu_w", jnp.stack([small[0][3], small[2][3]]), 256)]
    offsets, at = {}, 0
    for name, _, rows in sections:
        offsets[name] = (at, rows)
        at += rows
    packed = jnp.concatenate([_pad_rows(a, rows) for _, a, rows in sections], axis=0)
    small_zone = _to_zone(packed[None], 0, me1, F32, "place_small_grads")
    small_send1, small_recv1, small_zone, (dx,) = _gather_start([small_zone], [], [dx], "gather_small_start")
    grad_x = dx.reshape(x.shape)

    finish_scatter(2, dx)
    finish_scatter(0, dx)
    sharded_adam("ab_w_out", 1, (0, 2), ab_w_out, m_ab_w_out, v_ab_w_out, [])
    sharded_adam("ab_w_in", 0, (0, 2), ab_w_in, m_ab_w_in, v_ab_w_in, [])

    last = res["ab_w_in"][0]
    small_send2, small_recv2, small_zone = _gather_mid(small_zone, small_recv1, last, "gather_small_mid")
    gathered = _gather_end(small_zone, small_send1, small_recv1, small_send2, small_recv2, last, "gather_small_end")[0]
    summed = _sum_devices(gathered, "sum_small_grads")

    def section(name, nrows, src=summed):
        start = offsets[name][0]
        return src[..., start:start + nrows, :]

    grads = {
        "norm_g": section("norm_g", DEPTH),
        "ada_b": section("d_mod", 3 * DEPTH).reshape(DEPTH, 3 * D),
        "ab_ln_g": section("ab_ln_g", 2), "ab_ln_b": section("ab_ln_b", 2),
        "ab_sgu_b": section("ab_sgu_b", 2).reshape(ab_sgu_b.shape),
        "final_g": section("final_g", 1),
        "ab_sgu_w": section("ab_sgu_w", 256).reshape(ab_sgu_w.shape),
        "ab_conv_w": lax.dynamic_slice_in_dim(section("ab_conv_w", 6).reshape(2, 3, E_A), me * HEAD, HEAD, axis=2),
        "c_pool_scale": lax.dynamic_slice_in_dim(section("c_pool_scale", 4).reshape(2, 4 * G_C), me * 256, 256, axis=1),
    }
    small_w = {"norm_g": (norm_g, m_norm_g, v_norm_g), "ada_b": (ada_b, m_ada_b, v_ada_b),
               "ab_ln_g": (ab_ln_g, m_ab_ln_g, v_ab_ln_g), "ab_ln_b": (ab_ln_b, m_ab_ln_b, v_ab_ln_b),
               "ab_sgu_b": (ab_sgu_b, m_ab_sgu_b, v_ab_sgu_b),
               "final_g": (final_g.reshape(1, D), m_final_g.reshape(1, D), v_final_g.reshape(1, D)),
               "ab_sgu_w": (ab_sgu_w, m_ab_sgu_w, v_ab_sgu_w), "ab_conv_w": (ab_conv_w, m_ab_conv_w, v_ab_conv_w),
               "c_pool_scale": (c_pool_scale, m_c_pool_scale, v_c_pool_scale)}
    for k, (w, m, v) in small_w.items():
        res[k] = [grads[k]] + list(_adam_small(w, grads[k], m, v, "adam_" + k))
    res["final_g"] = [a.reshape(D) for a in res["final_g"]]

    dmod_all = section("d_mod", 3 * DEPTH, gathered).reshape(NDEV, DEPTH, 3 * D)
    dmod_mine = jnp.transpose(lax.dynamic_slice_in_dim(dmod_all, me * ADA_NC, ADA_NC, axis=2), (1, 0, 2))
    res["ada_w"] = _ada_weight_adam(jnp.transpose(cact_all.reshape(NDEV, D)), dmod_mine, ada_w, m_ada_w, v_ada_w)

    order = ["norm_g", "ada_w", "ada_b", "ab_w_in", "ab_conv_w", "ab_ln_g", "ab_ln_b", "ab_sgu_w", "ab_sgu_b",
             "ab_w_out", "c_w_in", "c_pool_w", "c_pool_scale", "c_w_out", "final_g"]
    return (loss, grad_x, *[res[k][0] for k in order], *[res[k][1] for k in order],
            *[res[k][2] for k in order], *[res[k][3] for k in order])
```

```python
import jax
import jax.numpy as jnp
from jax import lax
from jax.experimental import pallas as pl
from jax.experimental.pallas import tpu as pltpu

F32, BF16 = jnp.float32, jnp.bfloat16
S, D = 2048, 1024
NDEV = 8
DEPTH = 4
EPS = 1e-6
E_A = 1024
HEAD = 128
CHUNK = 128
POOL_WINDOWS = (2, 4, 8, 16)
G_C = 512
HALO = 16
ADA_NC = 384
MIB = 1024 * 1024
LANE = 128

ADAM_LR, ADAM_B1, ADAM_B2, ADAM_EPS, ADAM_WD, ADAM_STEP = 0.001, 0.9, 0.999, 1e-08, 0.01, 10

ANY = pl.BlockSpec(memory_space=pl.ANY)
VMEM_FULL = pl.BlockSpec(memory_space=pltpu.VMEM)
IN_HBM = pl.BlockSpec(memory_space=pltpu.HBM)
SEMAPHORES = pl.BlockSpec(memory_space=pltpu.SEMAPHORE)
IN_FLIGHT = pltpu.SideEffectType.DATAFLOW_SIDE_EFFECTING


V7X_VMEM_MIB = 64
VMEM_LIMIT_MIB = V7X_VMEM_MIB - 4


def _params(semantics=None):
    return pltpu.CompilerParams(dimension_semantics=semantics, vmem_limit_bytes=VMEM_LIMIT_MIB * MIB)


def _silu(z):
    return z * jax.nn.sigmoid(z)


def _silu_and_grad(z):
    sig = jax.nn.sigmoid(z)
    return z * sig, sig * (1.0 + z * (1.0 - sig))


def _position():
    return lax.axis_index("x"), lax.axis_index("y"), lax.axis_index("c")


def _index(pos):
    return 4 * pos[0] + 2 * pos[1] + pos[2]


def _peer(pos, k):
    flipped = tuple(1 - p if (k >> (2 - b)) & 1 else p for b, p in enumerate(pos))
    return flipped, _index(flipped)


def _remote(src, dst, send_sem, recv_sem, device):
    return pltpu.make_async_remote_copy(src_ref=src, dst_ref=dst, send_sem=send_sem, recv_sem=recv_sem,
                                        device_id=device, device_id_type=pl.DeviceIdType.MESH)


def _gather(arrays, name):
    n = len(arrays)
    out_shape = [jax.ShapeDtypeStruct((NDEV,) + a.shape, a.dtype) for a in arrays]

    def body(*refs):
        ins, outs = refs[:n], refs[n:2 * n]
        send_sems, recv_sems, own_sems = refs[2 * n:]
        x, y, c = _position()
        me = _index((x, y, c))
        sibling = (x, y, 1 - c)
        chips = [(1 - x, y), (x, 1 - y), (1 - x, 1 - y)]

        def block_copy(j, k, owner, to, src=None):
            rows = outs[j].at[_index(owner)]
            return _remote(rows if src is None else src, rows, send_sems.at[j, k], recv_sems.at[j, k], to)

        own, first, passed = [], [], []
        for j in range(n):
            own.append(pltpu.make_async_copy(ins[j], outs[j].at[me], own_sems.at[j]))
            first.append(block_copy(j, 0, (x, y, c), sibling, src=ins[j]))
            first += [block_copy(j, 1 + q, (x, y, c), (*chip, c), src=ins[j]) for q, chip in enumerate(chips)]
        for copy in own + first:
            copy.start()
        for q, chip in enumerate(chips):
            for j in range(n):
                block_copy(j, 1 + q, (*chip, c), (x, y, c)).wait_recv()
                forward = block_copy(j, 4 + q, (*chip, c), sibling)
                forward.start()
                passed.append(forward)
        for j in range(n):
            block_copy(j, 0, sibling, (x, y, c)).wait_recv()
            for q, chip in enumerate(chips):
                block_copy(j, 4 + q, (*chip, 1 - c), (x, y, c)).wait_recv()
        for copy in first + passed:
            copy.wait_send()
        for copy in own:
            copy.wait()

    return pl.pallas_call(
        body, name=name, out_shape=out_shape, in_specs=[ANY] * n, out_specs=[ANY] * n,
        scratch_shapes=[pltpu.SemaphoreType.DMA((n, NDEV - 1)), pltpu.SemaphoreType.DMA((n, NDEV - 1)),
                        pltpu.SemaphoreType.DMA((n,))],
    )(*arrays)


def _pair_start(parts, name):
    n = len(parts)
    lands = [_in_hbm(lax.empty(p.shape[1:], p.dtype)) for p in parts]

    def body(*refs):
        ins, zones = refs[:n], refs[n:2 * n]
        send_sems, recv_sems = refs[2 * n:2 * n + 2]
        x, y, c = _position()
        for j in range(n):
            _remote(ins[j].at[1 - c], zones[j], send_sems.at[j], recv_sems.at[j], (x, y, 1 - c)).start()

    outs = pl.pallas_call(
        body, name=name,
        out_shape=(pltpu.SemaphoreType.DMA((n,)), pltpu.SemaphoreType.DMA((n,)),
                   *[pltpu.HBM(p.shape, p.dtype) for p in parts], *[pltpu.HBM(z.shape, z.dtype) for z in lands]),
        in_specs=[IN_HBM] * (2 * n), out_specs=(SEMAPHORES, SEMAPHORES, *[IN_HBM] * (2 * n)),
        input_output_aliases={j: 2 + j for j in range(2 * n)},
        compiler_params=pltpu.CompilerParams(has_side_effects=IN_FLIGHT),
    )(*[_in_hbm(p) for p in parts], *lands)
    return outs[0], outs[1], list(outs[2:2 + n]), list(outs[2 + n:])


def _pair_end(parts, zones, send_sems, recv_sems, after, name):
    n = len(parts)

    def body(*refs):
        ins, zs = refs[:n], refs[n:2 * n]
        s, r = refs[2 * n:2 * n + 2]
        me = _position()
        for j in range(n):
            copy = _remote(ins[j].at[0], zs[j], s.at[j], r.at[j], me)
            copy.wait_send()
            copy.wait_recv()

    outs = pl.pallas_call(
        body, name=name,
        out_shape=(*[pltpu.HBM(p.shape, p.dtype) for p in parts], *[pltpu.HBM(z.shape, z.dtype) for z in zones]),
        in_specs=[IN_HBM] * (2 * n) + [SEMAPHORES, SEMAPHORES, ANY], out_specs=tuple([IN_HBM] * (2 * n)),
        input_output_aliases={j: j for j in range(2 * n)},
        compiler_params=pltpu.CompilerParams(has_side_effects=IN_FLIGHT),
    )(*parts, *zones, send_sems, recv_sems, after)
    return list(outs[:n]), list(outs[n:])


def _pair_sum(part, from_sibling, core, name):
    ncol = part.shape[-1]
    p3 = part.reshape(2, -1, ncol)
    q2 = from_sibling.reshape(-1, ncol)
    nrows = q2.shape[0]
    tr = 512

    def body(core_ref, p_ref, q_ref, o_ref):
        o_ref[...] = (p_ref[...].astype(F32) + q_ref[...].astype(F32)).astype(BF16)

    grid_spec = pltpu.PrefetchScalarGridSpec(
        num_scalar_prefetch=1, grid=(nrows // tr,),
        in_specs=[pl.BlockSpec((None, tr, ncol), lambda i, core_ref: (core_ref[0], i, 0)),
                  pl.BlockSpec((tr, ncol), lambda i, core_ref: (i, 0))],
        out_specs=pl.BlockSpec((tr, ncol), lambda i, core_ref: (i, 0)))
    out = pl.pallas_call(body, name=name, grid_spec=grid_spec, out_shape=jax.ShapeDtypeStruct(q2.shape, BF16),
                         compiler_params=_params(("arbitrary",)))(core, p3, q2)
    return out.reshape(from_sibling.shape)


def _in_hbm(a):
    return pltpu.with_memory_space_constraint(a, pltpu.HBM)


def _chips(x, y):
    return [(1 - x, y), (x, 1 - y), (1 - x, 1 - y)]


def _to_zone(a, wl, me, dtype, name):
    _, rows, cols = a.shape
    tr = 256 if rows % 256 == 0 else rows

    def body(me_ref, a_ref, o_ref):
        o_ref[...] = a_ref[...].astype(dtype)

    grid_spec = pltpu.PrefetchScalarGridSpec(
        num_scalar_prefetch=1, grid=(rows // tr,),
        in_specs=[pl.BlockSpec((None, tr, cols), lambda i, me_ref: (wl, i, 0))],
        out_specs=pl.BlockSpec((None, tr, cols), lambda i, me_ref: (me_ref[0], i, 0)))
    return pl.pallas_call(body, name=name, grid_spec=grid_spec, out_shape=jax.ShapeDtypeStruct((NDEV, rows, cols), dtype),
                          compiler_params=_params(("arbitrary",)))(me, a)


def _gather_start(zones, after, carried, name):
    n, m = len(zones), len(carried)

    def body(*refs):
        zs = refs[:n]
        send_sems, recv_sems = refs[n + m + len(after):n + m + len(after) + 2]
        x, y, c = _position()
        me = _index((x, y, c))
        for j in range(n):
            mine = zs[j].at[me]
            _remote(mine, mine, send_sems.at[4 * j], recv_sems.at[4 * j], (x, y, 1 - c)).start()
            for q, chip in enumerate(_chips(x, y)):
                _remote(mine, mine, send_sems.at[4 * j + 1 + q], recv_sems.at[4 * j + 1 + q], (*chip, c)).start()

    outs = pl.pallas_call(
        body, name=name,
        out_shape=(pltpu.SemaphoreType.DMA((4 * n,)), pltpu.SemaphoreType.DMA((4 * n,)),
                   *[pltpu.HBM(z.shape, z.dtype) for z in zones], *[jax.ShapeDtypeStruct(a.shape, a.dtype) for a in carried]),
        in_specs=[IN_HBM] * n + [ANY] * (m + len(after)),
        out_specs=(SEMAPHORES, SEMAPHORES, *[IN_HBM] * n, *[ANY] * m),
        input_output_aliases={j: 2 + j for j in range(n + m)},
        compiler_params=pltpu.CompilerParams(has_side_effects=IN_FLIGHT),
    )(*[_in_hbm(z) for z in zones], *carried, *after)
    return outs[0], outs[1], list(outs[2:2 + n]), list(outs[2 + n:])


def _gather_mid(zones, recv_sems, after, name):
    n = len(zones)

    def body(*refs):
        zs, first_recv = refs[:n], refs[n]
        send_sems, recv_sems2 = refs[n + 2:n + 4]
        x, y, c = _position()
        for q, chip in enumerate(_chips(x, y)):
            for j in range(n):
                rows = zs[j].at[_index((*chip, c))]
                _remote(rows, rows, send_sems.at[3 * j + q], first_recv.at[4 * j + 1 + q], (x, y, c)).wait_recv()
                _remote(rows, rows, send_sems.at[3 * j + q], recv_sems2.at[3 * j + q], (x, y, 1 - c)).start()

    outs = pl.pallas_call(
        body, name=name,
        out_shape=(pltpu.SemaphoreType.DMA((3 * n,)), pltpu.SemaphoreType.DMA((3 * n,)),
                   *[pltpu.HBM(z.shape, z.dtype) for z in zones]),
        in_specs=[IN_HBM] * n + [SEMAPHORES, ANY], out_specs=(SEMAPHORES, SEMAPHORES, *[IN_HBM] * n),
        input_output_aliases={j: 2 + j for j in range(n)},
        compiler_params=pltpu.CompilerParams(has_side_effects=IN_FLIGHT),
    )(*zones, recv_sems, after)
    return outs[0], outs[1], list(outs[2:])


def _gather_end(zones, send1, recv1, send2, recv2, after, name):
    n = len(zones)

    def body(*refs):
        zs = refs[:n]
        s1, r1, s2, r2 = refs[n:n + 4]
        x, y, c = _position()
        me = (x, y, c)
        for j in range(n):
            rows = zs[j].at[_index((x, y, 1 - c))]
            _remote(rows, rows, s1.at[4 * j], r1.at[4 * j], me).wait_recv()
            for q, chip in enumerate(_chips(x, y)):
                rows = zs[j].at[_index((*chip, 1 - c))]
                _remote(rows, rows, s2.at[3 * j + q], r2.at[3 * j + q], me).wait_recv()
        for j in range(n):
            block = zs[j].at[0]
            for k in range(4):
                _remote(block, block, s1.at[4 * j + k], r1.at[4 * j + k], me).wait_send()
            for q in range(3):
                _remote(block, block, s2.at[3 * j + q], r2.at[3 * j + q], me).wait_send()

    outs = pl.pallas_call(
        body, name=name, out_shape=tuple(pltpu.HBM(z.shape, z.dtype) for z in zones),
        in_specs=[IN_HBM] * n + [SEMAPHORES] * 4 + [ANY], out_specs=tuple([IN_HBM] * n),
        input_output_aliases={j: j for j in range(n)},
        compiler_params=pltpu.CompilerParams(has_side_effects=IN_FLIGHT),
    )(*zones, send1, recv1, send2, recv2, after)
    return list(outs)


def _scatter_start(parts, name):
    n = len(parts)
    lands = [_in_hbm(lax.empty((3,) + p.shape[1:], p.dtype)) for p in parts]

    def body(*refs):
        ins, zones = refs[:n], refs[n:2 * n]
        send_sems, recv_sems = refs[2 * n:2 * n + 2]
        x, y, c = _position()
        for j in range(n):
            for q, (px, py) in enumerate(_chips(x, y)):
                _remote(ins[j].at[2 * px + py], zones[j].at[q], send_sems.at[3 * j + q], recv_sems.at[3 * j + q],
                        (px, py, c)).start()

    outs = pl.pallas_call(
        body, name=name,
        out_shape=(pltpu.SemaphoreType.DMA((3 * n,)), pltpu.SemaphoreType.DMA((3 * n,)),
                   *[pltpu.HBM(p.shape, p.dtype) for p in parts], *[pltpu.HBM(z.shape, z.dtype) for z in lands]),
        in_specs=[IN_HBM] * (2 * n), out_specs=(SEMAPHORES, SEMAPHORES, *[IN_HBM] * (2 * n)),
        input_output_aliases={j: 2 + j for j in range(2 * n)},
        compiler_params=pltpu.CompilerParams(has_side_effects=IN_FLIGHT),
    )(*[_in_hbm(p) for p in parts], *lands)
    return outs[0], outs[1], list(outs[2:2 + n]), list(outs[2 + n:])


def _scatter_end(parts, zones, send_sems, recv_sems, after, name):
    n = len(parts)

    def body(*refs):
        ins, zs = refs[:n], refs[n:2 * n]
        s, r = refs[2 * n:2 * n + 2]
        me = _position()
        for j in range(n):
            for q in range(3):
                copy = _remote(ins[j].at[0], zs[j].at[q], s.at[3 * j + q], r.at[3 * j + q], me)
                copy.wait_send()
                copy.wait_recv()

    outs = pl.pallas_call(
        body, name=name,
        out_shape=(*[pltpu.HBM(p.shape, p.dtype) for p in parts], *[pltpu.HBM(z.shape, z.dtype) for z in zones]),
        in_specs=[IN_HBM] * (2 * n) + [SEMAPHORES, SEMAPHORES, ANY], out_specs=tuple([IN_HBM] * (2 * n)),
        input_output_aliases={j: j for j in range(2 * n)},
        compiler_params=pltpu.CompilerParams(has_side_effects=IN_FLIGHT),
    )(*parts, *zones, send_sems, recv_sems, after)
    return list(outs[:n]), list(outs[n:])


def _ada_forward(c, ada_w, ada_b):
    def body(c_ref, w_ref, b_ref, cact_ref, mod_ref, gbuf, modrow, send_sems, recv_sems):
        pos = _position()
        me = _index(pos)

        def to_all(ref, round_):
            copies = []
            for k in range(1, NDEV):
                peer, _ = _peer(pos, k)
                copy = pltpu.make_async_remote_copy(
                    src_ref=ref.at[me], dst_ref=ref.at[me], send_sem=send_sems.at[round_, k - 1],
                    recv_sem=recv_sems.at[round_, k - 1], device_id=peer, device_id_type=pl.DeviceIdType.MESH)
                copy.start()
                copies.append(copy)
            for copy in copies:
                copy.wait()

        cact_ref[me] = _silu(c_ref[...])
        to_all(cact_ref, 0)
        rows = lax.broadcasted_iota(jnp.int32, (NDEV, D), 0)
        cact = jnp.zeros((NDEV, D), F32)
        for e in range(NDEV):
            cact = jnp.where(rows == e, cact_ref[e], cact)
        cact = cact.astype(BF16)
        for l in range(DEPTH):
            gbuf[me, l] = jnp.dot(cact, w_ref[l].astype(BF16), preferred_element_type=F32)
        to_all(gbuf, 1)
        mine = lax.broadcasted_iota(jnp.int32, (NDEV, ADA_NC), 0) == me
        for l in range(DEPTH):
            for d in range(NDEV):
                modrow[:, d * ADA_NC:(d + 1) * ADA_NC] = jnp.sum(jnp.where(mine, gbuf[d, l], 0.0), axis=0, keepdims=True)
            full = modrow[...] + b_ref[l:l + 1, :]
            for w in range(3):
                mod_ref[l, w] = full[:, w * D:(w + 1) * D]

    return pl.pallas_call(
        body, name="ada_forward",
        out_shape=[jax.ShapeDtypeStruct((NDEV, 1, D), F32), jax.ShapeDtypeStruct((DEPTH, 3, 1, D), F32)],
        in_specs=[VMEM_FULL] * 3, out_specs=[VMEM_FULL] * 2,
        scratch_shapes=[pltpu.VMEM((NDEV, DEPTH, NDEV, ADA_NC), F32), pltpu.VMEM((1, 3 * D), F32),
                        pltpu.SemaphoreType.DMA((2, NDEV - 1)), pltpu.SemaphoreType.DMA((2, NDEV - 1))],
        compiler_params=_params(),
    )(c, ada_w, ada_b)


def _mod_spec(layer, which, ngrid):
    index = {1: lambda i: (layer, which, 0, 0), 2: lambda i, j: (layer, which, 0, 0)}[ngrid]
    return pl.BlockSpec((None, None, 1, D), index)


def _norm_proj(x, mod, norm_g3, wg, layer, name):
    nb = wg.shape[-1]
    tm = 1024

    def body(x_ref, g_ref, shift_ref, scale_ref, w_ref, h_ref, p_ref):
        @pl.when(pl.program_id(1) == 0)
        def _():
            xv = x_ref[...]
            r = lax.rsqrt(jnp.mean(xv * xv, axis=-1, keepdims=True) + EPS)
            hn = xv * r * g_ref[...]
            h_ref[...] = (hn * (1.0 + scale_ref[...]) + shift_ref[...]).astype(BF16)

        p_ref[...] = jnp.dot(h_ref[...], w_ref[...], preferred_element_type=F32).astype(BF16)

    return pl.pallas_call(
        body, name=name, grid=(S // tm, NDEV),
        out_shape=[jax.ShapeDtypeStruct((S, D), BF16), jax.ShapeDtypeStruct((S, NDEV * nb), BF16)],
        in_specs=[pl.BlockSpec((tm, D), lambda i, d: (i, 0)),
                  pl.BlockSpec((None, 1, D), lambda i, d: (layer, 0, 0)),
                  _mod_spec(layer, 0, 2), _mod_spec(layer, 1, 2),
                  pl.BlockSpec((None, D, nb), lambda i, d: (d, 0, 0))],
        out_specs=[pl.BlockSpec((tm, D), lambda i, d: (i, 0)), pl.BlockSpec((tm, nb), lambda i, d: (i, d))],
        compiler_params=_params(("arbitrary", "arbitrary")),
    )(x, norm_g3, mod, mod, wg)


def _out_proj(ycat, w_out, x, mod, layer, name):
    tm = 512
    e = w_out.shape[0]

    def body(y_ref, w_ref, x_ref, gate_ref, xn_ref, o_ref):
        acc = jnp.dot(y_ref[...], w_ref[...], preferred_element_type=F32)
        o_ref[...] = acc.astype(BF16)
        xn_ref[...] = x_ref[...] + gate_ref[...] * acc

    return pl.pallas_call(
        body, name=name, grid=(S // tm,),
        out_shape=[jax.ShapeDtypeStruct((S, D), F32), jax.ShapeDtypeStruct((S, D), BF16)],
        in_specs=[pl.BlockSpec((tm, e), lambda i: (i, 0)), pl.BlockSpec((e, D), lambda i: (0, 0)),
                  pl.BlockSpec((tm, D), lambda i: (i, 0)), _mod_spec(layer, 2, 1)],
        out_specs=[pl.BlockSpec((tm, D), lambda i: (i, 0))] * 2,
        compiler_params=_params(("arbitrary",)),
    )(ycat, w_out, x, mod)


def _final_loss(x, target, final_g2):
    tm = 256

    def body(x_ref, t_ref, g_ref, dx_ref, loss_ref, dg_ref):
        @pl.when(pl.program_id(0) == 0)
        def _():
            loss_ref[...] = jnp.zeros_like(loss_ref)
            dg_ref[...] = jnp.zeros_like(dg_ref)

        xv, g = x_ref[...], g_ref[...]
        r = lax.rsqrt(jnp.mean(xv * xv, axis=-1, keepdims=True) + EPS)
        xn = xv * r
        err = xn * g - t_ref[...]
        loss_ref[...] += 0.5 * jnp.sum(jnp.mean(err * err, axis=-1, keepdims=True), axis=0, keepdims=True)
        dy = err * (1.0 / D)
        dg_ref[...] += jnp.sum(dy * xn, axis=0, keepdims=True)
        u = dy * g
        dx_ref[...] = r * (u - xn * jnp.mean(xn * u, axis=-1, keepdims=True))

    tile = pl.BlockSpec((tm, D), lambda i: (i, 0))
    row = pl.BlockSpec((1, D), lambda i: (0, 0))
    return pl.pallas_call(
        body, name="final_loss", grid=(S // tm,),
        out_shape=[jax.ShapeDtypeStruct((S, D), F32), jax.ShapeDtypeStruct((1, LANE), F32), jax.ShapeDtypeStruct((1, D), F32)],
        in_specs=[tile, tile, row], out_specs=[tile, pl.BlockSpec((1, LANE), lambda i: (0, 0)), row],
        compiler_params=_params(("arbitrary",)),
    )(x, target, final_g2)


def _out_bwd(dx, out, ycat, w_out, mod, layer, carried, name):
    tm = 512
    nsteps = S // tm
    e = ycat.shape[1]
    rb = e // NDEV
    nc = len(carried)

    def body(dx_ref, o_ref, y_ref, w_ref, gate_ref, *rest):
        dy_ref, gw_ref, dgate_ref = rest[nc:nc + 3]
        acc = rest[-1]
        step = pl.program_id(0)

        @pl.when(step == 0)
        def _():
            dgate_ref[...] = jnp.zeros_like(dgate_ref)
            acc[...] = jnp.zeros_like(acc)

        dxv = dx_ref[...]
        d_out = (gate_ref[...] * dxv).astype(BF16)
        dgate_ref[...] += jnp.sum(dxv * o_ref[...].astype(F32), axis=0, keepdims=True)
        dy_ref[...] = lax.dot_general(d_out, w_ref[...], (((1,), (1,)), ((), ())), preferred_element_type=F32).astype(BF16)
        acc[...] += lax.dot_general(y_ref[...], d_out, (((0,), (0,)), ((), ())), preferred_element_type=F32)

        @pl.when(step == nsteps - 1)
        def _():
            for d in range(NDEV):
                gw_ref[d % 2, d // 2] = acc[d * rb:(d + 1) * rb, :].astype(BF16)

    tile = pl.BlockSpec((tm, D), lambda i: (i, 0))
    wide = pl.BlockSpec((tm, e), lambda i: (i, 0))
    outs = pl.pallas_call(
        body, name=name, grid=(nsteps,),
        out_shape=[jax.ShapeDtypeStruct((S, e), BF16), jax.ShapeDtypeStruct((2, NDEV // 2, rb, D), BF16),
                   jax.ShapeDtypeStruct((1, D), F32)] + [jax.ShapeDtypeStruct(a.shape, a.dtype) for a in carried],
        in_specs=[tile, tile, wide, pl.BlockSpec((e, D), lambda i: (0, 0)), _mod_spec(layer, 2, 1)] + [ANY] * nc,
        out_specs=[wide, pl.BlockSpec((2, NDEV // 2, rb, D), lambda i: (0, 0, 0, 0)), pl.BlockSpec((1, D), lambda i: (0, 0))]
        + [ANY] * nc,
        scratch_shapes=[pltpu.VMEM((e, D), F32)],
        input_output_aliases={5 + k: 3 + k for k in range(nc)},
        compiler_params=_params(("arbitrary",)),
    )(dx, out, ycat, w_out, mod, *carried)
    return outs[0], outs[1], outs[2], list(outs[3:])


def _weight_grad(h, d_proj, name):
    nb = d_proj.shape[1] // NDEV

    def body(h_ref, dp_ref, o_ref):
        o_ref[...] = lax.dot_general(h_ref[...], dp_ref[...], (((0,), (0,)), ((), ())),
                                     preferred_element_type=F32).astype(BF16)

    return pl.pallas_call(
        body, name=name, grid=(NDEV,), out_shape=jax.ShapeDtypeStruct((2, NDEV // 2, D, nb), BF16),
        in_specs=[pl.BlockSpec((S, D), lambda d: (0, 0)), pl.BlockSpec((S, nb), lambda d: (0, d))],
        out_specs=pl.BlockSpec((None, None, D, nb), lambda d: (d % 2, d // 2, 0, 0)),
        compiler_params=_params(("arbitrary",)),
    )(h, d_proj)


def _dh_norm_bwd(d_proj, wg, x, dx, mod, norm_g3, layer, carried, name):
    nb = wg.shape[-1]
    tm = 1024
    rc = 128

    def body(dp_ref, w_ref, x_ref, dx_ref, g_ref, scale_ref, carried_ref,
             dxi_ref, dshift_ref, dscale_ref, dg_ref, carried_out, acc):
        i, d = pl.program_id(0), pl.program_id(1)
        part = lax.dot_general(dp_ref[...], w_ref[...], (((1,), (1,)), ((), ())), preferred_element_type=F32)

        @pl.when(d == 0)
        def _():
            acc[...] = part

        @pl.when(d != 0)
        def _():
            acc[...] += part

        @pl.when(jnp.logical_and(i == 0, d == 0))
        def _():
            dshift_ref[...] = jnp.zeros_like(dshift_ref)
            dscale_ref[...] = jnp.zeros_like(dscale_ref)
            dg_ref[...] = jnp.zeros_like(dg_ref)

        @pl.when(d == NDEV - 1)
        def _():
            g = g_ref[...]
            scale1 = 1.0 + scale_ref[...]

            def chunk(k, sums):
                rows = pl.ds(pl.multiple_of(k * rc, rc), rc)
                xv, dhv = x_ref[rows, :], acc[rows, :]
                r = lax.rsqrt(jnp.mean(xv * xv, axis=-1, keepdims=True) + EPS)
                xn = xv * r
                dhn = dhv * scale1
                u = dhn * g
                dxi_ref[rows, :] = dx_ref[rows, :] + r * (u - xn * jnp.mean(xn * u, axis=-1, keepdims=True))
                return (sums[0] + jnp.sum(dhv, axis=0, keepdims=True),
                        sums[1] + jnp.sum(dhv * (xn * g), axis=0, keepdims=True),
                        sums[2] + jnp.sum(dhn * xn, axis=0, keepdims=True))

            zero = jnp.zeros((1, D), F32)
            sums = lax.fori_loop(0, tm // rc, chunk, (zero, zero, zero))
            dshift_ref[...] += sums[0]
            dscale_ref[...] += sums[1]
            dg_ref[...] += sums[2]

    tile = pl.BlockSpec((tm, D), lambda i, d: (i, 0))
    row = pl.BlockSpec((1, D), lambda i, d: (0, 0))
    return pl.pallas_call(
        body, name=name, grid=(S // tm, NDEV),
        out_shape=[jax.ShapeDtypeStruct((S, D), F32)] + [jax.ShapeDtypeStruct((1, D), F32)] * 3
        + [jax.ShapeDtypeStruct(carried.shape, carried.dtype)],
        in_specs=[pl.BlockSpec((tm, nb), lambda i, d: (i, d)), pl.BlockSpec((None, D, nb), lambda i, d: (d, 0, 0)),
                  tile, tile, pl.BlockSpec((None, 1, D), lambda i, d: (layer, 0, 0)), _mod_spec(layer, 1, 2), ANY],
        out_specs=[tile, row, row, row, ANY], scratch_shapes=[pltpu.VMEM((tm, D), F32)],
        input_output_aliases={6: 4}, compiler_params=_params(("arbitrary", "arbitrary")),
    )(d_proj, wg, x, dx, norm_g3, mod, carried)


TS = 256
NCH = TS // CHUNK
HALO_BLOCKS = TS // HALO


def _halo_before(width, col_block):
    return pl.BlockSpec((HALO, width), lambda i: (jnp.maximum(i * HALO_BLOCKS - 1, 0), col_block))


def _halo_after(width, col_block):
    return pl.BlockSpec((HALO, width), lambda i: (jnp.minimum((i + 1) * HALO_BLOCKS, S // HALO - 1), col_block))


def _shift_down(ext, k):
    return pltpu.roll(ext, k, 0)[HALO:]


def _shift_up(ext, k):
    return pltpu.roll(ext, ext.shape[0] - k, 0)[:ext.shape[0] - HALO]


def _layer_norm_head(v, lg, lb):
    mu = jnp.mean(v, axis=-1, keepdims=True)
    vc = v - mu
    rstd = lax.rsqrt(jnp.mean(vc * vc, axis=-1, keepdims=True) + EPS)
    vhat = vc * rstd
    return vhat, rstd, vhat * lg + lb


def _causal_mask():
    return lax.broadcasted_iota(jnp.int32, (CHUNK, CHUNK), 0) >= lax.broadcasted_iota(jnp.int32, (CHUNK, CHUNK), 1)


def _even_mix_fwd(proj, convw, ln_g3, ln_b3, sgu_w, sgu_bcol, wl, name):
    def body(pj_ref, hh_ref, hc_ref, cw_ref, lg_ref, lb_ref, sw_ref, sb_ref, y_ref):
        live = (pl.program_id(0) > 0).astype(F32)
        causal = _causal_mask()
        for j in range(E_A // HEAD):
            cols = slice(j * HEAD, (j + 1) * HEAD)
            w0, w1, w2 = cw_ref[0:1, cols], cw_ref[1:2, cols], cw_ref[2:3, cols]
            lg, lb = lg_ref[:, cols], lb_ref[:, cols]
            wm = jnp.where(causal, sw_ref[j], 0.0).astype(BF16)
            bias = sb_ref[j]

            def split(s, rows, cols=cols):
                return pj_ref[rows, s * E_A + cols.start:s * E_A + cols.stop].astype(F32)

            prev_tail = hc_ref[:, cols].astype(F32) * hh_ref[:, cols].astype(F32) * live
            for n in range(NCH):
                rows = slice(n * CHUNK, (n + 1) * CHUNK)
                p = split(2, rows) * split(0, rows)
                ext = jnp.concatenate([prev_tail, p], axis=0)
                prev_tail = p[CHUNK - HALO:]
                cv = w2 * p + w1 * _shift_down(ext, 1) + w0 * _shift_down(ext, 2)
                y_ref[rows, cols] = (split(1, rows) * cv * _silu(split(3, rows))).astype(BF16)
                _, _, vn = _layer_norm_head(split(5, rows), lg, lb)
                mixed = jnp.dot(wm, vn.astype(BF16), preferred_element_type=F32) + bias
                y_ref[rows, E_A + cols.start:E_A + cols.stop] = (split(4, rows) * mixed * _silu(split(6, rows))).astype(BF16)

    const3 = lambda i: (wl, 0, 0)
    const4 = lambda i: (wl, 0, 0, 0)
    return pl.pallas_call(
        body, name=name, grid=(S // TS,), out_shape=jax.ShapeDtypeStruct((S, 2 * E_A), BF16),
        in_specs=[pl.BlockSpec((TS, 7 * E_A), lambda i: (i, 0)), _halo_before(E_A, 0), _halo_before(E_A, 2),
                  pl.BlockSpec((None, 3, E_A), const3), pl.BlockSpec((None, 1, E_A), const3),
                  pl.BlockSpec((None, 1, E_A), const3), pl.BlockSpec((None, NDEV, CHUNK, CHUNK), const4),
                  pl.BlockSpec((None, NDEV, CHUNK, 1), const4)],
        out_specs=pl.BlockSpec((TS, 2 * E_A), lambda i: (i, 0)),
        compiler_params=_params(("arbitrary",)),
    )(proj, proj, proj, convw, ln_g3, ln_b3, sgu_w, sgu_bcol)


def _even_mix_bwd(proj, d_ycat, convw, ln_g3, ln_b3, sgu_w, sgu_bcol, wl, name):
    nsteps = S // TS

    def body(pj_ref, hh_ref, hc_ref, hb_ref, hz_ref, dy_ref, hdy_ref, cw_ref, lg_ref, lb_ref, sw_ref, sb_ref,
             dp_ref, dcw_ref, dlg_ref, dlb_ref, dsw_ref, dsb_ref):
        step = pl.program_id(0)

        @pl.when(step == 0)
        def _():
            for ref in (dcw_ref, dlg_ref, dlb_ref, dsw_ref, dsb_ref):
                ref[...] = jnp.zeros_like(ref)

        live_before = (step > 0).astype(F32)
        live_after = (step < nsteps - 1).astype(F32)
        causal = _causal_mask()
        for j in range(E_A // HEAD):
            cols = slice(j * HEAD, (j + 1) * HEAD)
            w0, w1, w2 = cw_ref[0:1, cols], cw_ref[1:2, cols], cw_ref[2:3, cols]
            lg, lb = lg_ref[:, cols], lb_ref[:, cols]
            wmf = jnp.where(causal, sw_ref[j], 0.0)
            wm, wmt = wmf.astype(BF16), wmf.T.astype(BF16)
            bias = sb_ref[j]

            def split(s, rows, cols=cols):
                return pj_ref[rows, s * E_A + cols.start:s * E_A + cols.stop].astype(F32)

            def put(s, rows, val, cols=cols):
                dp_ref[rows, s * E_A + cols.start:s * E_A + cols.stop] = val.astype(BF16)

            ps = [split(2, slice(n * CHUNK, (n + 1) * CHUNK)) * split(0, slice(n * CHUNK, (n + 1) * CHUNK)) for n in range(NCH)]
            next_head = (hdy_ref[:, cols].astype(F32) * hb_ref[:, cols].astype(F32) * _silu(hz_ref[:, cols].astype(F32))
                         * live_after)
            acc_w = [jnp.zeros((1, HEAD), F32) for _ in range(3)]
            for n in reversed(range(NCH)):
                rows = slice(n * CHUNK, (n + 1) * CHUNK)
                p = ps[n]
                tail = ps[n - 1][CHUNK - HALO:] if n > 0 else hc_ref[:, cols].astype(F32) * hh_ref[:, cols].astype(F32) * live_before
                ext = jnp.concatenate([tail, p], axis=0)
                p1, p2 = _shift_down(ext, 1), _shift_down(ext, 2)
                cv = w2 * p + w1 * p1 + w0 * p2
                a_b, a_z = split(1, rows), split(3, rows)
                sz, dsz = _silu_and_grad(a_z)
                dya = dy_ref[rows, cols].astype(F32)
                put(1, rows, dya * cv * sz)
                put(3, rows, dya * a_b * cv * dsz)
                gcv = dya * a_b * sz
                acc_w[0] += jnp.sum(gcv * p2, axis=0, keepdims=True)
                acc_w[1] += jnp.sum(gcv * p1, axis=0, keepdims=True)
                acc_w[2] += jnp.sum(gcv * p, axis=0, keepdims=True)
                gext = jnp.concatenate([gcv, next_head], axis=0)
                next_head = gcv[:HALO]
                dpv = w2 * gcv + w1 * _shift_up(gext, 1) + w0 * _shift_up(gext, 2)
                put(2, rows, dpv * split(0, rows))
                put(0, rows, dpv * split(2, rows))
            for k in range(3):
                dcw_ref[k:k + 1, cols] += acc_w[k]

            acc_lg, acc_lb = jnp.zeros((1, HEAD), F32), jnp.zeros((1, HEAD), F32)
            acc_sw, acc_sb = jnp.zeros((CHUNK, CHUNK), F32), jnp.zeros((CHUNK, 1), F32)
            for n in range(NCH):
                rows = slice(n * CHUNK, (n + 1) * CHUNK)
                u, z = split(4, rows), split(6, rows)
                vhat, rstd, vn = _layer_norm_head(split(5, rows), lg, lb)
                vn16 = vn.astype(BF16)
                mixed = jnp.dot(wm, vn16, preferred_element_type=F32) + bias
                sz, dsz = _silu_and_grad(z)
                dyb = dy_ref[rows, E_A + cols.start:E_A + cols.stop].astype(F32)
                put(4, rows, dyb * mixed * sz)
                put(6, rows, dyb * u * mixed * dsz)
                dmix = dyb * u * sz
                dmix16 = dmix.astype(BF16)
                acc_sb += jnp.sum(dmix, axis=1, keepdims=True)
                acc_sw += lax.dot_general(dmix16, vn16, (((1,), (1,)), ((), ())), preferred_element_type=F32)
                dvn = jnp.dot(wmt, dmix16, preferred_element_type=F32)
                acc_lg += jnp.sum(dvn * vhat, axis=0, keepdims=True)
                acc_lb += jnp.sum(dvn, axis=0, keepdims=True)
                dvh = dvn * lg
                put(5, rows, rstd * (dvh - jnp.mean(dvh, axis=-1, keepdims=True)
                                     - vhat * jnp.mean(dvh * vhat, axis=-1, keepdims=True)))
            dlg_ref[:, cols] += acc_lg
            dlb_ref[:, cols] += acc_lb
            dsw_ref[j] += jnp.where(causal, acc_sw, 0.0)
            dsb_ref[j] += acc_sb

    const3 = lambda i: (wl, 0, 0)
    const4 = lambda i: (wl, 0, 0, 0)
    fixed2 = lambda i: (0, 0)
    fixed3 = lambda i: (0, 0, 0)
    return pl.pallas_call(
        body, name=name, grid=(nsteps,),
        out_shape=[jax.ShapeDtypeStruct((S, 7 * E_A), BF16), jax.ShapeDtypeStruct((3, E_A), F32),
                   jax.ShapeDtypeStruct((1, E_A), F32), jax.ShapeDtypeStruct((1, E_A), F32),
                   jax.ShapeDtypeStruct((NDEV, CHUNK, CHUNK), F32), jax.ShapeDtypeStruct((NDEV, CHUNK, 1), F32)],
        in_specs=[pl.BlockSpec((TS, 7 * E_A), lambda i: (i, 0)), _halo_before(E_A, 0), _halo_before(E_A, 2),
                  _halo_after(E_A, 1), _halo_after(E_A, 3),
                  pl.BlockSpec((TS, 2 * E_A), lambda i: (i, 0)), _halo_after(E_A, 0),
                  pl.BlockSpec((None, 3, E_A), const3), pl.BlockSpec((None, 1, E_A), const3),
                  pl.BlockSpec((None, 1, E_A), const3), pl.BlockSpec((None, NDEV, CHUNK, CHUNK), const4),
                  pl.BlockSpec((None, NDEV, CHUNK, 1), const4)],
        out_specs=[pl.BlockSpec((TS, 7 * E_A), lambda i: (i, 0)), pl.BlockSpec((3, E_A), fixed2),
                   pl.BlockSpec((1, E_A), fixed2), pl.BlockSpec((1, E_A), fixed2),
                   pl.BlockSpec((NDEV, CHUNK, CHUNK), fixed3), pl.BlockSpec((NDEV, CHUNK, 1), fixed3)],
        compiler_params=_params(("arbitrary",)),
    )(proj, proj, proj, proj, proj, d_ycat, d_ycat, convw, ln_g3, ln_b3, sgu_w, sgu_bcol)


def _window_count(step, n, win, ext_before):
    rows = CHUNK if ext_before else CHUNK + HALO
    t = step * TS + n * CHUNK + lax.broadcasted_iota(jnp.int32, (rows, 1), 0)
    return jnp.minimum(t + 1, win).astype(F32)


def _pool_weight(wp_ref, g):
    return jnp.concatenate([wp_ref[d, g] for d in range(NDEV)], axis=0)


def _pooled_chunk(p, tail, win, count):
    sums = jnp.concatenate([tail, p], axis=0)
    shift = 1
    while shift < win:
        sums = sums + pltpu.roll(sums, shift, 0)
        shift *= 2
    return sums[HALO:] / count - p


def _pool_mix_fwd(proj, wpool, pscale4, wl, name):
    e_c = 4 * G_C

    def body(pj_ref, hp_ref, wp_ref, ps_ref, y_ref, pooled_scr, yraw_scr):
        step = pl.program_id(0)
        live = (step > 0).astype(F32)
        for g, win in enumerate(POOL_WINDOWS):
            for q in range(G_C // LANE):
                cols = slice(g * G_C + q * LANE, g * G_C + (q + 1) * LANE)
                tail = hp_ref[:, cols].astype(F32) * live
                for n in range(NCH):
                    rows = slice(n * CHUNK, (n + 1) * CHUNK)
                    p = pj_ref[rows, cols].astype(F32)
                    pooled_scr[rows, q * LANE:(q + 1) * LANE] = _pooled_chunk(
                        p, tail, win, _window_count(step, n, win, True)).astype(BF16)
                    tail = p[CHUNK - HALO:]
            yraw_scr[...] = jnp.dot(pooled_scr[...], _pool_weight(wp_ref, g), preferred_element_type=F32)
            for q in range(G_C // LANE):
                cols = slice(g * G_C + q * LANE, g * G_C + (q + 1) * LANE)
                for n in range(NCH):
                    rows = slice(n * CHUNK, (n + 1) * CHUNK)
                    z = pj_ref[rows, e_c + cols.start:e_c + cols.stop].astype(F32)
                    y_ref[rows, cols] = (yraw_scr[rows, q * LANE:(q + 1) * LANE] * ps_ref[:, cols] * _silu(z)).astype(BF16)

    return pl.pallas_call(
        body, name=name, grid=(S // TS,), out_shape=jax.ShapeDtypeStruct((S, e_c), BF16),
        in_specs=[pl.BlockSpec((TS, 2 * e_c), lambda i: (i, 0)), _halo_before(e_c, 0),
                  pl.BlockSpec((NDEV, 4, G_C // NDEV, G_C), lambda i: (0, 0, 0, 0)),
                  pl.BlockSpec((None, 1, e_c), lambda i: (wl, 0, 0))],
        out_specs=pl.BlockSpec((TS, e_c), lambda i: (i, 0)),
        scratch_shapes=[pltpu.VMEM((TS, G_C), BF16), pltpu.VMEM((TS, G_C), F32)],
        compiler_params=_params(("arbitrary",)),
    )(proj, proj, wpool, pscale4)


def _pool_mix_bwd(proj, d_ycat, wpool, pscale4, wl, name):
    e_c = 4 * G_C
    nsteps = S // TS
    rb = G_C // NDEV

    def body(pj_ref, hp_ref, hz_ref, dy_ref, hdy_ref, wp_ref, ps_ref,
             dp_ref, dps_ref, dwp_ref, pooled_scr, yraw_scr, dyraw_scr, dpool_scr, acc_w):
        step = pl.program_id(0)

        @pl.when(step == 0)
        def _():
            dps_ref[...] = jnp.zeros_like(dps_ref)
            acc_w[...] = jnp.zeros_like(acc_w)

        live_before = (step > 0).astype(F32)
        live_after = (step < nsteps - 1).astype(F32)
        for g, win in enumerate(POOL_WINDOWS):
            weight = _pool_weight(wp_ref, g)
            for q in range(G_C // LANE):
                cols = slice(g * G_C + q * LANE, g * G_C + (q + 1) * LANE)
                tail = hp_ref[:, cols].astype(F32) * live_before
                for n in range(NCH):
                    rows = slice(n * CHUNK, (n + 1) * CHUNK)
                    p = pj_ref[rows, cols].astype(F32)
                    pooled_scr[rows, q * LANE:(q + 1) * LANE] = _pooled_chunk(
                        p, tail, win, _window_count(step, n, win, True)).astype(BF16)
                    tail = p[CHUNK - HALO:]
            yraw_scr[...] = jnp.dot(pooled_scr[...], weight, preferred_element_type=F32)
            for q in range(G_C // LANE):
                cols = slice(g * G_C + q * LANE, g * G_C + (q + 1) * LANE)
                local = slice(q * LANE, (q + 1) * LANE)
                scale = ps_ref[:, cols]
                acc_ps = jnp.zeros((1, LANE), F32)
                for n in range(NCH):
                    rows = slice(n * CHUNK, (n + 1) * CHUNK)
                    sz, dsz = _silu_and_grad(pj_ref[rows, e_c + cols.start:e_c + cols.stop].astype(F32))
                    dyv = dy_ref[rows, cols].astype(F32)
                    yraw = yraw_scr[rows, local]
                    dyraw_scr[rows, local] = (dyv * scale * sz).astype(BF16)
                    acc_ps += jnp.sum(dyv * yraw * sz, axis=0, keepdims=True)
                    dp_ref[rows, e_c + cols.start:e_c + cols.stop] = (dyv * yraw * scale * dsz).astype(BF16)
                dps_ref[:, cols] += acc_ps
                dyraw_scr[TS:, local] = (hdy_ref[:, cols].astype(F32) * scale * _silu(hz_ref[:, cols].astype(F32))
                                         * live_after).astype(BF16)
            dpool_scr[...] = lax.dot_general(dyraw_scr[...], weight, (((1,), (1,)), ((), ())), preferred_element_type=F32)
            acc_w[g] += lax.dot_general(pooled_scr[...], dyraw_scr[:TS, :], (((0,), (0,)), ((), ())),
                                        preferred_element_type=F32)
            for q in range(G_C // LANE):
                cols = slice(g * G_C + q * LANE, g * G_C + (q + 1) * LANE)
                local = slice(q * LANE, (q + 1) * LANE)
                for n in range(NCH):
                    rows = slice(n * CHUNK, (n + 1) * CHUNK)
                    ext = dpool_scr[n * CHUNK:(n + 1) * CHUNK + HALO, local]
                    sums = ext / _window_count(step, n, win, False)
                    shift = 1
                    while shift < win:
                        sums = sums + pltpu.roll(sums, CHUNK + HALO - shift, 0)
                        shift *= 2
                    dp_ref[rows, cols] = (sums[:CHUNK] - ext[:CHUNK]).astype(BF16)

        @pl.when(step == nsteps - 1)
        def _():
            for g in range(4):
                for d in range(NDEV):
                    dwp_ref[d % 2, d // 2, g] = acc_w[g, d * rb:(d + 1) * rb, :].astype(BF16)

    in_specs = [pl.BlockSpec((TS, 2 * e_c), lambda i: (i, 0)), _halo_before(e_c, 0), _halo_after(e_c, 1),
                pl.BlockSpec((TS, e_c), lambda i: (i, 0)), _halo_after(e_c, 0),
                pl.BlockSpec((NDEV, 4, rb, G_C), lambda i: (0, 0, 0, 0)),
                pl.BlockSpec((None, 1, e_c), lambda i: (wl, 0, 0))]
    args = [proj, proj, proj, d_ycat, d_ycat, wpool, pscale4]
    return pl.pallas_call(
        body, name=name, grid=(nsteps,),
        out_shape=[jax.ShapeDtypeStruct((S, 2 * e_c), BF16), jax.ShapeDtypeStruct((1, e_c), F32),
                   jax.ShapeDtypeStruct((2, NDEV // 2) + wpool.shape[1:], BF16)],
        in_specs=in_specs,
        out_specs=[pl.BlockSpec((TS, 2 * e_c), lambda i: (i, 0)), pl.BlockSpec((1, e_c), lambda i: (0, 0)),
                   pl.BlockSpec((2, NDEV // 2, 4, rb, G_C), lambda i: (0, 0, 0, 0, 0))],
        scratch_shapes=[pltpu.VMEM((TS, G_C), BF16), pltpu.VMEM((TS, G_C), F32), pltpu.VMEM((TS + HALO, G_C), BF16),
                        pltpu.VMEM((TS + HALO, G_C), F32), pltpu.VMEM((4, G_C, G_C), F32)],
        compiler_params=_params(("arbitrary",)),
    )(*args)


def _adamw(w, g, m, v):
    m = ADAM_B1 * m + (1.0 - ADAM_B1) * g
    v = ADAM_B2 * v + (1.0 - ADAM_B2) * jnp.square(g)
    m_hat = m / (1.0 - ADAM_B1 ** ADAM_STEP)
    v_hat = v / (1.0 - ADAM_B2 ** ADAM_STEP)
    delta = -ADAM_LR * (m_hat / (jnp.sqrt(v_hat) + ADAM_EPS) + ADAM_WD * w)
    return delta, m, v


def _adam_sharded(w, m, v, chip_parts, landed, my_chip, carried, name):
    nl, nr, ncol = w.shape
    tr = 128
    steps = nr // tr
    nc = len(carried)

    def body(chip_ref, w_ref, m_ref, v_ref, *rest):
        parts, zones = rest[:nl], rest[nl:2 * nl]
        g_ref, d_ref, nm_ref, nv_ref = rest[2 * nl + nc:2 * nl + nc + 4]
        layer = pl.program_id(0)
        g = jnp.zeros((tr, ncol), F32)
        for l in range(nl):
            gl = parts[l][...].astype(F32)
            for q in range(3):
                gl = gl + zones[l][q].astype(F32)
            g = jnp.where(layer == l, gl, g)
        g_ref[...] = g
        d_ref[...], nm_ref[...], nv_ref[...] = _adamw(w_ref[...], g, m_ref[...], v_ref[...])

    def rows_of(l):
        return lambda layer, i, chip_ref: jnp.where(layer == l, i, jnp.where(layer < l, 0, steps - 1))

    spec = pl.BlockSpec((None, tr, ncol), lambda layer, i, chip_ref: (layer, i, 0))
    part_specs = [pl.BlockSpec((None, tr, ncol), lambda layer, i, chip_ref, l=l: (chip_ref[0], rows_of(l)(layer, i, chip_ref), 0))
                  for l in range(nl)]
    zone_specs = [pl.BlockSpec((3, tr, ncol), lambda layer, i, chip_ref, l=l: (0, rows_of(l)(layer, i, chip_ref), 0))
                  for l in range(nl)]
    grid_spec = pltpu.PrefetchScalarGridSpec(
        num_scalar_prefetch=1, grid=(nl, steps), in_specs=[spec, spec, spec] + part_specs + zone_specs + [ANY] * nc,
        out_specs=[spec] * 4 + [ANY] * nc)
    return pl.pallas_call(
        body, name=name, grid_spec=grid_spec,
        out_shape=[jax.ShapeDtypeStruct(w.shape, F32)] * 4 + [jax.ShapeDtypeStruct(a.shape, a.dtype) for a in carried],
        input_output_aliases={4 + 2 * nl + k: 4 + k for k in range(nc)},
        compiler_params=_params(("arbitrary", "arbitrary")),
    )(my_chip, w, m, v, *chip_parts, *landed, *carried)


def _adam_small(w, g, m, v, name):
    def body(w_ref, g_ref, m_ref, v_ref, d_ref, nm_ref, nv_ref):
        d_ref[...], nm_ref[...], nv_ref[...] = _adamw(w_ref[...], g_ref[...], m_ref[...], v_ref[...])

    return pl.pallas_call(body, name=name, out_shape=[jax.ShapeDtypeStruct(w.shape, F32)] * 3,
                          in_specs=[VMEM_FULL] * 4, out_specs=[VMEM_FULL] * 3, compiler_params=_params())(w, g, m, v)


def _sum_devices(gathered, name):
    _, nr, ncol = gathered.shape

    def body(g_ref, o_ref):
        acc = g_ref[0]
        for s in range(1, NDEV):
            acc = acc + g_ref[s]
        o_ref[...] = acc

    return pl.pallas_call(body, name=name, grid=(1,), out_shape=jax.ShapeDtypeStruct((nr, ncol), F32),
                          in_specs=[pl.BlockSpec((NDEV, nr, ncol), lambda i: (0, 0, 0))],
                          out_specs=pl.BlockSpec((nr, ncol), lambda i: (0, 0)),
                          compiler_params=_params(("arbitrary",)))(gathered)


def _ada_weight_adam(cact_t, dmod_mine, w, m, v):
    def body(ct_ref, dm_ref, w_ref, m_ref, v_ref, g_ref, d_ref, nm_ref, nv_ref):
        ct, dm = ct_ref[...], dm_ref[...]
        g = ct[:, 0:1] * dm[0:1, :]
        for e in range(1, NDEV):
            g = g + ct[:, e:e + 1] * dm[e:e + 1, :]
        g_ref[...] = g
        d_ref[...], nm_ref[...], nv_ref[...] = _adamw(w_ref[...], g, m_ref[...], v_ref[...])

    spec = pl.BlockSpec((None, D, ADA_NC), lambda l: (l, 0, 0))
    return pl.pallas_call(
        body, name="ada_weight_adam", grid=(DEPTH,), out_shape=[jax.ShapeDtypeStruct(w.shape, F32)] * 4,
        in_specs=[pl.BlockSpec((D, NDEV), lambda l: (0, 0)), pl.BlockSpec((None, NDEV, ADA_NC), lambda l: (l, 0, 0)),
                  spec, spec, spec],
        out_specs=[spec] * 4, compiler_params=_params(("arbitrary",)),
    )(cact_t, dmod_mine, w, m, v)


def _pad_rows(a, rows):
    a = a.reshape(-1, D)
    return jnp.pad(a, ((0, rows - a.shape[0]), (0, 0)))


def kernel(x, c, norm_g, ada_w, ada_b, ab_w_in, ab_conv_w, ab_ln_g, ab_ln_b, ab_sgu_w, ab_sgu_b, ab_w_out, c_w_in, c_pool_w, c_pool_scale, c_w_out, final_g, loss_target, m_norm_g, m_ada_w, m_ada_b, m_ab_w_in, m_ab_conv_w, m_ab_ln_g, m_ab_ln_b, m_ab_sgu_w, m_ab_sgu_b, m_ab_w_out, m_c_w_in, m_c_pool_w, m_c_pool_scale, m_c_w_out, m_final_g, v_norm_g, v_ada_w, v_ada_b, v_ab_w_in, v_ab_conv_w, v_ab_ln_g, v_ab_ln_b, v_ab_sgu_w, v_ab_sgu_b, v_ab_w_out, v_c_w_in, v_c_pool_w, v_c_pool_scale, v_c_w_out, v_final_g):
    x_pos, y_pos, c_pos = _position()
    me = _index((x_pos, y_pos, c_pos))
    core = c_pos.astype(jnp.int32).reshape(1)
    my_chip = (2 * x_pos + y_pos).astype(jnp.int32).reshape(1)
    me1 = me.astype(jnp.int32).reshape(1)
    x0 = x.reshape(S, D)
    target = loss_target.reshape(S, D)
    norm_g3 = norm_g.reshape(DEPTH, 1, D)
    ln_g3, ln_b3 = ab_ln_g.reshape(2, 1, E_A), ab_ln_b.reshape(2, 1, E_A)
    sgu_bcol = ab_sgu_b.reshape(2, NDEV, CHUNK, 1)
    rb = G_C // NDEV
    pool_w3, m_pool_w3, v_pool_w3 = (a.reshape(2, 4 * rb, G_C) for a in (c_pool_w, m_c_pool_w, v_c_pool_w))

    cact_all, mod = _ada_forward(c, ada_w, ada_b)
    convw_all, pscale_all = _gather([ab_conv_w, c_pool_scale], "gather_small_weights")
    convw = jnp.transpose(convw_all, (1, 2, 0, 3)).reshape(2, 3, E_A)
    pscale4 = jnp.transpose(pscale_all, (1, 0, 2)).reshape(2, 1, 4 * G_C)
    flights = []
    for layer in range(DEPTH):
        wl = layer // 2
        if layer % 2 == 0:
            zones = [_to_zone(ab_w_in, wl, me1, BF16, f"cast_w_in_{layer}"), _to_zone(ab_w_out, wl, me1, BF16, f"cast_w_out_{layer}")]
        else:
            zones = [_to_zone(c_w_in, wl, me1, BF16, f"cast_w_in_{layer}"), _to_zone(c_w_out, wl, me1, BF16, f"cast_w_out_{layer}"),
                     _to_zone(pool_w3, wl, me1, BF16, f"cast_pool_w_{layer}")]
        send1, recv1, zones, (mod,) = _gather_start(zones, [convw_all] if layer == 0 else [], [mod], f"gather_start_{layer}")
        flights.append((send1, recv1, zones))

    def finish_gather(layer, passed, after):
        send1, recv1, _ = flights[layer]
        send2, recv2, zones = passed
        wg = _gather_end(zones, send1, recv1, send2, recv2, after, f"gather_end_{layer}")
        return [wg[0], wg[1].reshape(-1, D)] + [w.reshape(NDEV, 4, rb, G_C) for w in wg[2:]]

    xs, hs, projs, ycats, outs, gathered_w = [x0], [], [], [], [], []
    gathered_w.append(finish_gather(0, _gather_mid(flights[0][2], flights[0][1], mod, "gather_mid_0"), mod))
    for layer in range(DEPTH):
        wl = layer // 2
        even = layer % 2 == 0
        wg = gathered_w[layer]
        h, proj = _norm_proj(xs[-1], mod, norm_g3, wg[0], layer, f"norm_proj_{layer}")
        if layer + 1 < DEPTH:
            passed = _gather_mid(flights[layer + 1][2], flights[layer + 1][1], h, f"gather_mid_{layer + 1}")
        if even:
            ycat = _even_mix_fwd(proj, convw, ln_g3, ln_b3, ab_sgu_w, sgu_bcol, wl, f"even_mix_fwd_{layer}")
        else:
            ycat = _pool_mix_fwd(proj, wg[2], pscale4, wl, f"pool_mix_fwd_{layer}")
        x_new, out = _out_proj(ycat, wg[1], xs[-1], mod, layer, f"out_proj_{layer}")
        if layer + 1 < DEPTH:
            gathered_w.append(finish_gather(layer + 1, passed, x_new))
        xs.append(x_new)
        hs.append(h)
        projs.append(proj)
        ycats.append(ycat)
        outs.append(out)

    dx, loss_part, d_final_g = _final_loss(xs[DEPTH], target, final_g.reshape(1, D))

    d_mod, d_norm_g = [None] * DEPTH, [None] * DEPTH
    small, scatters, landed, res = {}, {}, {}, {}

    def finish_scatter(layer, after):
        send_sems, recv_sems, chip_parts, zones = scatters[layer]
        landed[layer] = _scatter_end(chip_parts, zones, send_sems, recv_sems, after, f"scatter_end_{layer}")

    def flat(a):
        return a.reshape(a.shape[0], -1, a.shape[-1])

    def sharded_adam(k, j, layers, w, m, v, carried):
        outs4 = _adam_sharded(w, m, v, [flat(landed[l][0][j]) for l in layers], [flat(landed[l][1][j]) for l in layers],
                              my_chip, carried, "adam_" + k)
        res[k] = [o.reshape(c_pool_w.shape) if k == "c_pool_w" else o for o in outs4[:4]]
        return list(outs4[4:])

    previous = None
    for layer in reversed(range(DEPTH)):
        wl = layer // 2
        even = layer % 2 == 0
        wg = gathered_w[layer]
        carried = [] if previous is None else [scatters[previous][2][0]]
        d_ycat, grad_out, d_gate, carried = _out_bwd(dx, outs[layer], ycats[layer], wg[1], mod, layer, carried, f"out_bwd_{layer}")
        if previous is not None:
            scatters[previous][2][0] = carried[0]
        parts = [None, grad_out]
        if even:
            d_proj, d_cw, d_lg, d_lb, d_sw, d_sb = _even_mix_bwd(
                projs[layer], d_ycat, convw, ln_g3, ln_b3, ab_sgu_w, sgu_bcol, wl, f"even_mix_bwd_{layer}")
            small[layer] = (d_cw, d_lg, d_lb, d_sw, d_sb)
        else:
            d_proj, d_ps, d_pool = _pool_mix_bwd(projs[layer], d_ycat, wg[2], pscale4, wl, f"pool_mix_bwd_{layer}")
            small[layer] = (d_ps,)
            parts.append(d_pool)
        parts[0] = _weight_grad(hs[layer], d_proj, f"grad_w_in_{layer}")
        pair_send, pair_recv, parts, from_sibling = _pair_start(parts, f"pair_start_{layer}")
        if layer > 0:
            dx, d_shift, d_scale, d_norm_g[layer], parts[0] = _dh_norm_bwd(
                d_proj, wg[0], xs[layer], dx, mod, norm_g3, layer, parts[0], f"dh_norm_bwd_{layer}")
            pair_after = dx
        else:
            finish_scatter(1, d_proj)
            finish_scatter(3, d_proj)
            parts[0], = sharded_adam("c_w_in", 0, (1, 3), c_w_in, m_c_w_in, v_c_w_in, [parts[0]])
            parts[0], = sharded_adam("c_w_out", 1, (1, 3), c_w_out, m_c_w_out, v_c_w_out, [parts[0]])
            parts[0], = sharded_adam("c_pool_w", 2, (1, 3), pool_w3, m_pool_w3, v_pool_w3, [parts[0]])
            pair_after = res["c_pool_w"][0]
        parts, from_sibling = _pair_end(parts, from_sibling, pair_send, pair_recv, pair_after, f"pair_end_{layer}")
        chip_parts = [_pair_sum(p, q, core, f"pair_sum_{layer}_{j}") for j, (p, q) in enumerate(zip(parts, from_sibling))]
        send_sems, recv_sems, chip_parts, zones = _scatter_start(chip_parts, f"scatter_start_{layer}")
        if layer == 0:
            dx, d_shift, d_scale, d_norm_g[layer], chip_parts[0] = _dh_norm_bwd(
                d_proj, wg[0], xs[layer], dx, mod, norm_g3, layer, chip_parts[0], f"dh_norm_bwd_{layer}")
        scatters[layer] = [send_sems, recv_sems, chip_parts, zones]
        previous = layer
        d_mod[layer] = jnp.concatenate([d_shift, d_scale, d_gate], axis=0)
    grad_x = dx.reshape(x.shape)

    sections = [("norm_g", jnp.concatenate(d_norm_g, axis=0), 8),
                ("d_mod", jnp.concatenate(d_mod, axis=0), 16),
                ("ab_ln_g", jnp.concatenate([small[0][1], small[2][1]], axis=0), 8),
                ("ab_ln_b", jnp.concatenate([small[0][2], small[2][2]], axis=0), 8),
                ("ab_sgu_b", jnp.stack([small[0][4], small[2][4]]), 8),
                ("final_g", d_final_g, 8),
                ("loss", jnp.pad(loss_part, ((0, 0), (0, D - LANE))), 8),
                ("ab_conv_w", jnp.stack([small[0][0], small[2][0]]), 8),
                ("c_pool_scale", jnp.concatenate([small[1][0], small[3][0]], axis=0), 8),
                ("ab_sgu_w", jnp.stack([small[0][3], small[2][3]]), 256)]
    offsets, at = {}, 0
    for name, _, rows in sections:
        offsets[name] = (at, rows)
        at += rows
    packed = jnp.concatenate([_pad_rows(a, rows) for _, a, rows in sections], axis=0)
    small_zone = _to_zone(packed[None], 0, me1, F32, "place_small_grads")
    small_send1, small_recv1, small_zone, (mod,) = _gather_start([small_zone], [], [mod], "gather_small_start")

    finish_scatter(2, mod)
    finish_scatter(0, mod)
    sharded_adam("ab_w_out", 1, (0, 2), ab_w_out, m_ab_w_out, v_ab_w_out, [])
    sharded_adam("ab_w_in", 0, (0, 2), ab_w_in, m_ab_w_in, v_ab_w_in, [])

    last = res["ab_w_in"][0]
    small_send2, small_recv2, small_zone = _gather_mid(small_zone, small_recv1, last, "gather_small_mid")
    gathered = _gather_end(small_zone, small_send1, small_recv1, small_send2, small_recv2, last, "gather_small_end")[0]
    summed = _sum_devices(gathered, "sum_small_grads")

    def section(name, nrows, src=summed):
        start = offsets[name][0]
        return src[..., start:start + nrows, :]

    loss = section("loss", 1)[0, 0]
    grads = {
        "norm_g": section("norm_g", DEPTH),
        "ada_b": section("d_mod", 3 * DEPTH).reshape(DEPTH, 3 * D),
        "ab_ln_g": section("ab_ln_g", 2), "ab_ln_b": section("ab_ln_b", 2),
        "ab_sgu_b": section("ab_sgu_b", 2).reshape(ab_sgu_b.shape),
        "final_g": section("final_g", 1),
        "ab_sgu_w": section("ab_sgu_w", 256).reshape(ab_sgu_w.shape),
        "ab_conv_w": lax.dynamic_slice_in_dim(section("ab_conv_w", 6).reshape(2, 3, E_A), me * HEAD, HEAD, axis=2),
        "c_pool_scale": lax.dynamic_slice_in_dim(section("c_pool_scale", 4).reshape(2, 4 * G_C), me * 256, 256, axis=1),
    }
    small_w = {"norm_g": (norm_g, m_norm_g, v_norm_g), "ada_b": (ada_b, m_ada_b, v_ada_b),
               "ab_ln_g": (ab_ln_g, m_ab_ln_g, v_ab_ln_g), "ab_ln_b": (ab_ln_b, m_ab_ln_b, v_ab_ln_b),
               "ab_sgu_b": (ab_sgu_b, m_ab_sgu_b, v_ab_sgu_b),
               "final_g": (final_g.reshape(1, D), m_final_g.reshape(1, D), v_final_g.reshape(1, D)),
               "ab_sgu_w": (ab_sgu_w, m_ab_sgu_w, v_ab_sgu_w), "ab_conv_w": (ab_conv_w, m_ab_conv_w, v_ab_conv_w),
               "c_pool_scale": (c_pool_scale, m_c_pool_scale, v_c_pool_scale)}
    for k, (w, m, v) in small_w.items():
        res[k] = [grads[k]] + list(_adam_small(w, grads[k], m, v, "adam_" + k))
    res["final_g"] = [a.reshape(D) for a in res["final_g"]]

    dmod_all = section("d_mod", 3 * DEPTH, gathered).reshape(NDEV, DEPTH, 3 * D)
    dmod_mine = jnp.transpose(lax.dynamic_slice_in_dim(dmod_all, me * ADA_NC, ADA_NC, axis=2), (1, 0, 2))
    res["ada_w"] = _ada_weight_adam(jnp.transpose(cact_all.reshape(NDEV, D)), dmod_mine, ada_w, m_ada_w, v_ada_w)

    order = ["norm_g", "ada_w", "ada_b", "ab_w_in", "ab_conv_w", "ab_ln_g", "ab_ln_b", "ab_sgu_w", "ab_sgu_b",
             "ab_w_out", "c_w_in", "c_pool_w", "c_pool_scale", "c_w_out", "final_g"]
    return (loss, grad_x, *[res[k][0] for k in order], *[res[k][1] for k in order],
            *[res[k][2] for k in order], *[res[k][3] for k in order])
```

```python
import jax
import jax.numpy as jnp
from jax import lax
from jax.experimental import pallas as pl
from jax.experimental.pallas import tpu as pltpu

F32, BF16 = jnp.float32, jnp.bfloat16
S, D = 2048, 1024
NDEV = 8
DEPTH = 4
EPS = 1e-6
E_A = 1024
HEAD = 128
CHUNK = 128
POOL_WINDOWS = (2, 4, 8, 16)
G_C = 512
HALO = 16
ADA_NC = 384
MIB = 1024 * 1024
LANE = 128

ADAM_LR, ADAM_B1, ADAM_B2, ADAM_EPS, ADAM_WD, ADAM_STEP = 0.001, 0.9, 0.999, 1e-08, 0.01, 10

ANY = pl.BlockSpec(memory_space=pl.ANY)
VMEM_FULL = pl.BlockSpec(memory_space=pltpu.VMEM)
IN_HBM = pl.BlockSpec(memory_space=pltpu.HBM)
SEMAPHORES = pl.BlockSpec(memory_space=pltpu.SEMAPHORE)
IN_FLIGHT = pltpu.SideEffectType.DATAFLOW_SIDE_EFFECTING


V7X_VMEM_MIB = 64
VMEM_LIMIT_MIB = V7X_VMEM_MIB - 4


def _params(semantics=None):
    return pltpu.CompilerParams(dimension_semantics=semantics, vmem_limit_bytes=VMEM_LIMIT_MIB * MIB)


def _silu(z):
    return z * jax.nn.sigmoid(z)


def _silu_and_grad(z):
    sig = jax.nn.sigmoid(z)
    return z * sig, sig * (1.0 + z * (1.0 - sig))


def _position():
    return lax.axis_index("x"), lax.axis_index("y"), lax.axis_index("c")


def _index(pos):
    return 4 * pos[0] + 2 * pos[1] + pos[2]


def _peer(pos, k):
    flipped = tuple(1 - p if (k >> (2 - b)) & 1 else p for b, p in enumerate(pos))
    return flipped, _index(flipped)


def _remote(src, dst, send_sem, recv_sem, device):
    return pltpu.make_async_remote_copy(src_ref=src, dst_ref=dst, send_sem=send_sem, recv_sem=recv_sem,
                                        device_id=device, device_id_type=pl.DeviceIdType.MESH)


def _gather(arrays, name):
    n = len(arrays)
    out_shape = [jax.ShapeDtypeStruct((NDEV,) + a.shape, a.dtype) for a in arrays]

    def body(*refs):
        ins, outs = refs[:n], refs[n:2 * n]
        send_sems, recv_sems, own_sems = refs[2 * n:]
        x, y, c = _position()
        me = _index((x, y, c))
        sibling = (x, y, 1 - c)
        chips = [(1 - x, y), (x, 1 - y), (1 - x, 1 - y)]

        def block_copy(j, k, owner, to, src=None):
            rows = outs[j].at[_index(owner)]
            return _remote(rows if src is None else src, rows, send_sems.at[j, k], recv_sems.at[j, k], to)

        own, first, passed = [], [], []
        for j in range(n):
            own.append(pltpu.make_async_copy(ins[j], outs[j].at[me], own_sems.at[j]))
            first.append(block_copy(j, 0, (x, y, c), sibling, src=ins[j]))
            first += [block_copy(j, 1 + q, (x, y, c), (*chip, c), src=ins[j]) for q, chip in enumerate(chips)]
        for copy in own + first:
            copy.start()
        for q, chip in enumerate(chips):
            for j in range(n):
                block_copy(j, 1 + q, (*chip, c), (x, y, c)).wait_recv()
                forward = block_copy(j, 4 + q, (*chip, c), sibling)
                forward.start()
                passed.append(forward)
        for j in range(n):
            block_copy(j, 0, sibling, (x, y, c)).wait_recv()
            for q, chip in enumerate(chips):
                block_copy(j, 4 + q, (*chip, 1 - c), (x, y, c)).wait_recv()
        for copy in first + passed:
            copy.wait_send()
        for copy in own:
            copy.wait()

    return pl.pallas_call(
        body, name=name, out_shape=out_shape, in_specs=[ANY] * n, out_specs=[ANY] * n,
        scratch_shapes=[pltpu.SemaphoreType.DMA((n, NDEV - 1)), pltpu.SemaphoreType.DMA((n, NDEV - 1)),
                        pltpu.SemaphoreType.DMA((n,))],
    )(*arrays)


def _pair_start(parts, name):
    n = len(parts)
    lands = [_in_hbm(lax.empty(p.shape[1:], p.dtype)) for p in parts]

    def body(*refs):
        ins, zones = refs[:n], refs[n:2 * n]
        send_sems, recv_sems = refs[2 * n:2 * n + 2]
        x, y, c = _position()
        for j in range(n):
            _remote(ins[j].at[1 - c], zones[j], send_sems.at[j], recv_sems.at[j], (x, y, 1 - c)).start()

    outs = pl.pallas_call(
        body, name=name,
        out_shape=(pltpu.SemaphoreType.DMA((n,)), pltpu.SemaphoreType.DMA((n,)),
                   *[pltpu.HBM(p.shape, p.dtype) for p in parts], *[pltpu.HBM(z.shape, z.dtype) for z in lands]),
        in_specs=[IN_HBM] * (2 * n), out_specs=(SEMAPHORES, SEMAPHORES, *[IN_HBM] * (2 * n)),
        input_output_aliases={j: 2 + j for j in range(2 * n)},
        compiler_params=pltpu.CompilerParams(has_side_effects=IN_FLIGHT),
    )(*[_in_hbm(p) for p in parts], *lands)
    return outs[0], outs[1], list(outs[2:2 + n]), list(outs[2 + n:])


def _pair_end(parts, zones, send_sems, recv_sems, after, name):
    n = len(parts)

    def body(*refs):
        ins, zs = refs[:n], refs[n:2 * n]
        s, r = refs[2 * n:2 * n + 2]
        me = _position()
        for j in range(n):
            copy = _remote(ins[j].at[0], zs[j], s.at[j], r.at[j], me)
            copy.wait_send()
            copy.wait_recv()

    outs = pl.pallas_call(
        body, name=name,
        out_shape=(*[pltpu.HBM(p.shape, p.dtype) for p in parts], *[pltpu.HBM(z.shape, z.dtype) for z in zones]),
        in_specs=[IN_HBM] * (2 * n) + [SEMAPHORES, SEMAPHORES, ANY], out_specs=tuple([IN_HBM] * (2 * n)),
        input_output_aliases={j: j for j in range(2 * n)},
        compiler_params=pltpu.CompilerParams(has_side_effects=IN_FLIGHT),
    )(*parts, *zones, send_sems, recv_sems, after)
    return list(outs[:n]), list(outs[n:])


def _pair_sum(part, from_sibling, core, name):
    ncol = part.shape[-1]
    p3 = part.reshape(2, -1, ncol)
    q2 = from_sibling.reshape(-1, ncol)
    nrows = q2.shape[0]
    tr = 512

    def body(core_ref, p_ref, q_ref, o_ref):
        o_ref[...] = (p_ref[...].astype(F32) + q_ref[...].astype(F32)).astype(BF16)

    grid_spec = pltpu.PrefetchScalarGridSpec(
        num_scalar_prefetch=1, grid=(nrows // tr,),
        in_specs=[pl.BlockSpec((None, tr, ncol), lambda i, core_ref: (core_ref[0], i, 0)),
                  pl.BlockSpec((tr, ncol), lambda i, core_ref: (i, 0))],
        out_specs=pl.BlockSpec((tr, ncol), lambda i, core_ref: (i, 0)))
    out = pl.pallas_call(body, name=name, grid_spec=grid_spec, out_shape=jax.ShapeDtypeStruct(q2.shape, BF16),
                         compiler_params=_params(("arbitrary",)))(core, p3, q2)
    return out.reshape(from_sibling.shape)


def _in_hbm(a):
    return pltpu.with_memory_space_constraint(a, pltpu.HBM)


def _chips(x, y):
    return [(1 - x, y), (x, 1 - y), (1 - x, 1 - y)]


def _to_zone(a, wl, me, dtype, name):
    _, rows, cols = a.shape
    tr = 256 if rows % 256 == 0 else rows

    def body(me_ref, a_ref, o_ref):
        o_ref[...] = a_ref[...].astype(dtype)

    grid_spec = pltpu.PrefetchScalarGridSpec(
        num_scalar_prefetch=1, grid=(rows // tr,),
        in_specs=[pl.BlockSpec((None, tr, cols), lambda i, me_ref: (wl, i, 0))],
        out_specs=pl.BlockSpec((None, tr, cols), lambda i, me_ref: (me_ref[0], i, 0)))
    return pl.pallas_call(body, name=name, grid_spec=grid_spec, out_shape=jax.ShapeDtypeStruct((NDEV, rows, cols), dtype),
                          compiler_params=_params(("arbitrary",)))(me, a)


def _halves(block):
    rows = block.shape[0] // 2
    return block.at[pl.ds(0, rows)], block.at[pl.ds(rows, rows)]


def _around(x, y, c):
    return (x, y, 1 - c), (1 - x, y, c), (x, 1 - y, c), (1 - x, 1 - y, c)


def _gather_step1(zs, send, recv, pos):
    sibling, xn, yn, _ = _around(*pos)
    for j, z in enumerate(zs):
        mine = z.at[_index(pos)]
        for k, peer in enumerate((sibling, xn, yn)):
            _remote(mine, mine, send.at[3 * j + k], recv.at[3 * j + k], peer).start()


def _gather_step2(zs, recv1, send, recv, pos):
    sibling, xn, yn, _ = _around(*pos)
    for j, z in enumerate(zs):
        xb, yb = z.at[_index(xn)], z.at[_index(yn)]
        _remote(xb, xb, send.at[4 * j], recv1.at[3 * j + 1], pos).wait_recv()
        _remote(yb, yb, send.at[4 * j], recv1.at[3 * j + 2], pos).wait_recv()
        _remote(xb, xb, send.at[4 * j], recv.at[4 * j], sibling).start()
        _remote(yb, yb, send.at[4 * j + 1], recv.at[4 * j + 1], sibling).start()
        first, second = _halves(xb)[0], _halves(yb)[1]
        _remote(first, first, send.at[4 * j + 2], recv.at[4 * j + 2], yn).start()
        _remote(second, second, send.at[4 * j + 3], recv.at[4 * j + 3], xn).start()


def _gather_step3(zs, recv2, send, recv, pos):
    sibling, _, _, diagonal = _around(*pos)
    for j, z in enumerate(zs):
        db = z.at[_index(diagonal)]
        first, second = _halves(db)
        _remote(first, first, send.at[j], recv2.at[4 * j + 2], pos).wait_recv()
        _remote(second, second, send.at[j], recv2.at[4 * j + 3], pos).wait_recv()
        _remote(db, db, send.at[j], recv.at[j], sibling).start()


def _gather_step4(zs, send1, recv1, send2, recv2, send3, recv3, pos):
    x, y, c = pos
    sibling = (x, y, 1 - c)
    _, sx, sy, sd = _around(*sibling)
    for j, z in enumerate(zs):
        for owner, send, recv, k in ((sibling, send1, recv1, 3 * j), (sx, send2, recv2, 4 * j), (sy, send2, recv2, 4 * j + 1),
                                     (sd, send3, recv3, j)):
            block = z.at[_index(owner)]
            _remote(block, block, send.at[k], recv.at[k], pos).wait_recv()
    for j, z in enumerate(zs):
        block = z.at[0]
        half = _halves(block)[0]
        for ref, send, recv, k in ([(block, send1, recv1, 3 * j + k) for k in range(3)]
                                   + [(block, send2, recv2, 4 * j), (block, send2, recv2, 4 * j + 1),
                                      (half, send2, recv2, 4 * j + 2), (half, send2, recv2, 4 * j + 3), (block, send3, recv3, j)]):
            _remote(ref, ref, send.at[k], recv.at[k], pos).wait_send()


def _flight_call(step, name, zones, sems_in, nsems_out, after, carried):
    n, m, k = len(zones), len(carried), len(sems_in)

    def body(*refs):
        zs = refs[:n]
        given = refs[n + m:n + m + k]
        made = refs[n + m + k + len(after):n + m + k + len(after) + (2 if nsems_out else 0)]
        step(zs, *given, *made, _position())

    sem_out = (pltpu.SemaphoreType.DMA((nsems_out,)),) * 2 if nsems_out else ()
    outs = pl.pallas_call(
        body, name=name,
        out_shape=(*sem_out, *[pltpu.HBM(z.shape, z.dtype) for z in zones], *[jax.ShapeDtypeStruct(a.shape, a.dtype) for a in carried]),
        in_specs=[IN_HBM] * n + [ANY] * m + [SEMAPHORES] * k + [ANY] * len(after),
        out_specs=(*[SEMAPHORES] * len(sem_out), *[IN_HBM] * n, *[ANY] * m),
        input_output_aliases={j: len(sem_out) + j for j in range(n + m)},
        compiler_params=pltpu.CompilerParams(has_side_effects=IN_FLIGHT),
    )(*[_in_hbm(z) for z in zones], *carried, *sems_in, *after)
    sems = list(outs[:len(sem_out)])
    return sems, list(outs[len(sem_out):len(sem_out) + n]), list(outs[len(sem_out) + n:])


def _gather_start(zones, after, carried, name):
    (send1, recv1), zones, carried = _flight_call(_gather_step1, name, zones, [], 3 * len(zones), after, carried)
    return {"s1": send1, "r1": recv1, "zones": zones}, carried


def _gather_mid(flight, after, carried, name):
    step = lambda zs, recv1, send, recv, pos: _gather_step2(zs, recv1, send, recv, pos)
    (send2, recv2), zones, carried = _flight_call(step, name, flight["zones"], [flight["r1"]], 4 * len(flight["zones"]), after, carried)
    return {**flight, "s2": send2, "r2": recv2, "zones": zones}, carried


def _gather_late(flight, after, name):
    step = lambda zs, recv2, send, recv, pos: _gather_step3(zs, recv2, send, recv, pos)
    (send3, recv3), zones, _ = _flight_call(step, name, flight["zones"], [flight["r2"]], len(flight["zones"]), after, [])
    return {**flight, "s3": send3, "r3": recv3, "zones": zones}


def _gather_end(flight, after, name):
    sems = [flight[k] for k in ("s1", "r1", "s2", "r2", "s3", "r3")]
    _, zones, _ = _flight_call(_gather_step4, name, flight["zones"], sems, 0, after, [])
    return zones


def _scatter_start(parts, name):
    n = len(parts)
    lands = [_in_hbm(lax.empty((3,) + p.shape[1:], p.dtype)) for p in parts]

    def body(*refs):
        ins, zones = refs[:n], refs[n:2 * n]
        send_sems, recv_sems = refs[2 * n:2 * n + 2]
        x, y, c = _position()
        for j in range(n):
            for q, (px, py) in enumerate(_chips(x, y)):
                _remote(ins[j].at[2 * px + py], zones[j].at[q], send_sems.at[3 * j + q], recv_sems.at[3 * j + q],
                        (px, py, c)).start()

    outs = pl.pallas_call(
        body, name=name,
        out_shape=(pltpu.SemaphoreType.DMA((3 * n,)), pltpu.SemaphoreType.DMA((3 * n,)),
                   *[pltpu.HBM(p.shape, p.dtype) for p in parts], *[pltpu.HBM(z.shape, z.dtype) for z in lands]),
        in_specs=[IN_HBM] * (2 * n), out_specs=(SEMAPHORES, SEMAPHORES, *[IN_HBM] * (2 * n)),
        input_output_aliases={j: 2 + j for j in range(2 * n)},
        compiler_params=pltpu.CompilerParams(has_side_effects=IN_FLIGHT),
    )(*[_in_hbm(p) for p in parts], *lands)
    return outs[0], outs[1], list(outs[2:2 + n]), list(outs[2 + n:])


def _scatter_end(parts, zones, send_sems, recv_sems, after, name):
    n = len(parts)

    def body(*refs):
        ins, zs = refs[:n], refs[n:2 * n]
        s, r = refs[2 * n:2 * n + 2]
        me = _position()
        for j in range(n):
            for q in range(3):
                copy = _remote(ins[j].at[0], zs[j].at[q], s.at[3 * j + q], r.at[3 * j + q], me)
                copy.wait_send()
                copy.wait_recv()

    outs = pl.pallas_call(
        body, name=name,
        out_shape=(*[pltpu.HBM(p.shape, p.dtype) for p in parts], *[pltpu.HBM(z.shape, z.dtype) for z in zones]),
        in_specs=[IN_HBM] * (2 * n) + [SEMAPHORES, SEMAPHORES, ANY], out_specs=tuple([IN_HBM] * (2 * n)),
        input_output_aliases={j: j for j in range(2 * n)},
        compiler_params=pltpu.CompilerParams(has_side_effects=IN_FLIGHT),
    )(*parts, *zones, send_sems, recv_sems, after)
    return list(outs[:n]), list(outs[n:])


def _ada_forward(c, ada_w, ada_b):
    def body(c_ref, w_ref, b_ref, cact_ref, mod_ref, gbuf, modrow, send_sems, recv_sems):
        pos = _position()
        me = _index(pos)

        def to_all(ref, round_):
            copies = []
            for k in range(1, NDEV):
                peer, _ = _peer(pos, k)
                copy = pltpu.make_async_remote_copy(
                    src_ref=ref.at[me], dst_ref=ref.at[me], send_sem=send_sems.at[round_, k - 1],
                    recv_sem=recv_sems.at[round_, k - 1], device_id=peer, device_id_type=pl.DeviceIdType.MESH)
                copy.start()
                copies.append(copy)
            for copy in copies:
                copy.wait()

        cact_ref[me] = _silu(c_ref[...])
        to_all(cact_ref, 0)
        rows = lax.broadcasted_iota(jnp.int32, (NDEV, D), 0)
        cact = jnp.zeros((NDEV, D), F32)
        for e in range(NDEV):
            cact = jnp.where(rows == e, cact_ref[e], cact)
        cact = cact.astype(BF16)
        for l in range(DEPTH):
            gbuf[me, l] = jnp.dot(cact, w_ref[l].astype(BF16), preferred_element_type=F32)
        to_all(gbuf, 1)
        mine = lax.broadcasted_iota(jnp.int32, (NDEV, ADA_NC), 0) == me
        for l in range(DEPTH):
            for d in range(NDEV):
                modrow[:, d * ADA_NC:(d + 1) * ADA_NC] = jnp.sum(jnp.where(mine, gbuf[d, l], 0.0), axis=0, keepdims=True)
            full = modrow[...] + b_ref[l:l + 1, :]
            for w in range(3):
                mod_ref[l, w] = full[:, w * D:(w + 1) * D]

    return pl.pallas_call(
        body, name="ada_forward",
        out_shape=[jax.ShapeDtypeStruct((NDEV, 1, D), F32), jax.ShapeDtypeStruct((DEPTH, 3, 1, D), F32)],
        in_specs=[VMEM_FULL] * 3, out_specs=[VMEM_FULL] * 2,
        scratch_shapes=[pltpu.VMEM((NDEV, DEPTH, NDEV, ADA_NC), F32), pltpu.VMEM((1, 3 * D), F32),
                        pltpu.SemaphoreType.DMA((2, NDEV - 1)), pltpu.SemaphoreType.DMA((2, NDEV - 1))],
        compiler_params=_params(),
    )(c, ada_w, ada_b)


def _mod_spec(layer, which, ngrid):
    index = {1: lambda i: (layer, which, 0, 0), 2: lambda i, j: (layer, which, 0, 0)}[ngrid]
    return pl.BlockSpec((None, None, 1, D), index)


def _norm_proj(x, mod, norm_g3, wg, layer, name):
    nb = wg.shape[-1]
    tm = 1024

    def body(x_ref, g_ref, shift_ref, scale_ref, w_ref, ht_ref, p_ref, h_ref):
        @pl.when(pl.program_id(1) == 0)
        def _():
            xv = x_ref[...]
            r = lax.rsqrt(jnp.mean(xv * xv, axis=-1, keepdims=True) + EPS)
            hn = xv * r * g_ref[...]
            h = hn * (1.0 + scale_ref[...]) + shift_ref[...]
            h_ref[...] = h.astype(BF16)
            ht_ref[...] = h.T.astype(BF16)

        p_ref[...] = jnp.dot(h_ref[...], w_ref[...], preferred_element_type=F32).astype(BF16)

    return pl.pallas_call(
        body, name=name, grid=(S // tm, NDEV),
        out_shape=[jax.ShapeDtypeStruct((D, S), BF16), jax.ShapeDtypeStruct((S, NDEV * nb), BF16)],
        in_specs=[pl.BlockSpec((tm, D), lambda i, d: (i, 0)),
                  pl.BlockSpec((None, 1, D), lambda i, d: (layer, 0, 0)),
                  _mod_spec(layer, 0, 2), _mod_spec(layer, 1, 2),
                  pl.BlockSpec((None, D, nb), lambda i, d: (d, 0, 0))],
        out_specs=[pl.BlockSpec((D, tm), lambda i, d: (0, i)), pl.BlockSpec((tm, nb), lambda i, d: (i, d))],
        scratch_shapes=[pltpu.VMEM((tm, D), BF16)],
        compiler_params=_params(("arbitrary", "arbitrary")),
    )(x, norm_g3, mod, mod, wg)


def _out_proj(ycat, w_out, x, mod, layer, name):
    tm = 512
    e = w_out.shape[0]

    def body(y_ref, w_ref, x_ref, gate_ref, xn_ref, o_ref):
        acc = jnp.dot(y_ref[...], w_ref[...], preferred_element_type=F32)
        o_ref[...] = acc.astype(BF16)
        xn_ref[...] = x_ref[...] + gate_ref[...] * acc

    return pl.pallas_call(
        body, name=name, grid=(S // tm,),
        out_shape=[jax.ShapeDtypeStruct((S, D), F32), jax.ShapeDtypeStruct((S, D), BF16)],
        in_specs=[pl.BlockSpec((tm, e), lambda i: (i, 0)), pl.BlockSpec((e, D), lambda i: (0, 0)),
                  pl.BlockSpec((tm, D), lambda i: (i, 0)), _mod_spec(layer, 2, 1)],
        out_specs=[pl.BlockSpec((tm, D), lambda i: (i, 0))] * 2,
        compiler_params=_params(("arbitrary",)),
    )(ycat, w_out, x, mod)


def _final_loss(x, target, final_g2):
    tm = 256

    def body(x_ref, t_ref, g_ref, dx_ref, loss_ref, dg_ref):
        @pl.when(pl.program_id(0) == 0)
        def _():
            loss_ref[...] = jnp.zeros_like(loss_ref)
            dg_ref[...] = jnp.zeros_like(dg_ref)

        xv, g = x_ref[...], g_ref[...]
        r = lax.rsqrt(jnp.mean(xv * xv, axis=-1, keepdims=True) + EPS)
        xn = xv * r
        err = xn * g - t_ref[...]
        loss_ref[...] += 0.5 * jnp.sum(jnp.mean(err * err, axis=-1, keepdims=True), axis=0, keepdims=True)
        dy = err * (1.0 / D)
        dg_ref[...] += jnp.sum(dy * xn, axis=0, keepdims=True)
        u = dy * g
        dx_ref[...] = r * (u - xn * jnp.mean(xn * u, axis=-1, keepdims=True))

    tile = pl.BlockSpec((tm, D), lambda i: (i, 0))
    row = pl.BlockSpec((1, D), lambda i: (0, 0))
    return pl.pallas_call(
        body, name="final_loss", grid=(S // tm,),
        out_shape=[jax.ShapeDtypeStruct((S, D), F32), jax.ShapeDtypeStruct((1, LANE), F32), jax.ShapeDtypeStruct((1, D), F32)],
        in_specs=[tile, tile, row], out_specs=[tile, pl.BlockSpec((1, LANE), lambda i: (0, 0)), row],
        compiler_params=_params(("arbitrary",)),
    )(x, target, final_g2)


def _out_bwd(dx, out, ycat, w_out, mod, layer, carried, name):
    tm = 512
    nsteps = S // tm
    e = ycat.shape[1]
    rb = e // NDEV
    nc = len(carried)

    def body(dx_ref, o_ref, y_ref, w_ref, gate_ref, *rest):
        dy_ref, gw_ref, dgate_ref = rest[nc:nc + 3]
        acc = rest[-1]
        step = pl.program_id(0)

        @pl.when(step == 0)
        def _():
            dgate_ref[...] = jnp.zeros_like(dgate_ref)
            acc[...] = jnp.zeros_like(acc)

        dxv = dx_ref[...]
        d_out = (gate_ref[...] * dxv).astype(BF16)
        dgate_ref[...] += jnp.sum(dxv * o_ref[...].astype(F32), axis=0, keepdims=True)
        dy_ref[...] = lax.dot_general(d_out, w_ref[...], (((1,), (1,)), ((), ())), preferred_element_type=F32).astype(BF16)
        acc[...] += lax.dot_general(y_ref[...], d_out, (((0,), (0,)), ((), ())), preferred_element_type=F32)

        @pl.when(step == nsteps - 1)
        def _():
            for d in range(NDEV):
                gw_ref[d % 2, d // 2] = acc[d * rb:(d + 1) * rb, :].astype(BF16)

    tile = pl.BlockSpec((tm, D), lambda i: (i, 0))
    wide = pl.BlockSpec((tm, e), lambda i: (i, 0))
    outs = pl.pallas_call(
        body, name=name, grid=(nsteps,),
        out_shape=[jax.ShapeDtypeStruct((S, e), BF16), jax.ShapeDtypeStruct((2, NDEV // 2, rb, D), BF16),
                   jax.ShapeDtypeStruct((1, D), F32)] + [jax.ShapeDtypeStruct(a.shape, a.dtype) for a in carried],
        in_specs=[tile, tile, wide, pl.BlockSpec((e, D), lambda i: (0, 0)), _mod_spec(layer, 2, 1)] + [ANY] * nc,
        out_specs=[wide, pl.BlockSpec((2, NDEV // 2, rb, D), lambda i: (0, 0, 0, 0)), pl.BlockSpec((1, D), lambda i: (0, 0))]
        + [ANY] * nc,
        scratch_shapes=[pltpu.VMEM((e, D), F32)],
        input_output_aliases={5 + k: 3 + k for k in range(nc)},
        compiler_params=_params(("arbitrary",)),
    )(dx, out, ycat, w_out, mod, *carried)
    return outs[0], outs[1], outs[2], list(outs[3:])


def _weight_grad(h_t, d_proj, name):
    nb = d_proj.shape[1] // NDEV

    def body(ht_ref, dp_ref, o_ref):
        o_ref[...] = jnp.dot(ht_ref[...], dp_ref[...], preferred_element_type=F32).astype(BF16)

    return pl.pallas_call(
        body, name=name, grid=(NDEV,), out_shape=jax.ShapeDtypeStruct((2, NDEV // 2, D, nb), BF16),
        in_specs=[pl.BlockSpec((D, S), lambda d: (0, 0)), pl.BlockSpec((S, nb), lambda d: (0, d))],
        out_specs=pl.BlockSpec((None, None, D, nb), lambda d: (d % 2, d // 2, 0, 0)),
        compiler_params=_params(("arbitrary",)),
    )(h_t, d_proj)


def _dh_norm_bwd(d_proj, wg, x, dx, mod, norm_g3, layer, carried, name):
    nb = wg.shape[-1]
    tm = 1024
    rc = 128

    def body(dp_ref, w_ref, x_ref, dx_ref, g_ref, scale_ref, carried_ref,
             dxi_ref, dshift_ref, dscale_ref, dg_ref, carried_out, acc):
        i, d = pl.program_id(0), pl.program_id(1)
        part = lax.dot_general(dp_ref[...], w_ref[...], (((1,), (1,)), ((), ())), preferred_element_type=F32)

        @pl.when(d == 0)
        def _():
            acc[...] = part

        @pl.when(d != 0)
        def _():
            acc[...] += part

        @pl.when(jnp.logical_and(i == 0, d == 0))
        def _():
            dshift_ref[...] = jnp.zeros_like(dshift_ref)
            dscale_ref[...] = jnp.zeros_like(dscale_ref)
            dg_ref[...] = jnp.zeros_like(dg_ref)

        @pl.when(d == NDEV - 1)
        def _():
            g = g_ref[...]
            scale1 = 1.0 + scale_ref[...]

            def chunk(k, sums):
                rows = pl.ds(pl.multiple_of(k * rc, rc), rc)
                xv, dhv = x_ref[rows, :], acc[rows, :]
                r = lax.rsqrt(jnp.mean(xv * xv, axis=-1, keepdims=True) + EPS)
                xn = xv * r
                dhn = dhv * scale1
                u = dhn * g
                dxi_ref[rows, :] = dx_ref[rows, :] + r * (u - xn * jnp.mean(xn * u, axis=-1, keepdims=True))
                return (sums[0] + jnp.sum(dhv, axis=0, keepdims=True),
                        sums[1] + jnp.sum(dhv * (xn * g), axis=0, keepdims=True),
                        sums[2] + jnp.sum(dhn * xn, axis=0, keepdims=True))

            zero = jnp.zeros((1, D), F32)
            sums = lax.fori_loop(0, tm // rc, chunk, (zero, zero, zero))
            dshift_ref[...] += sums[0]
            dscale_ref[...] += sums[1]
            dg_ref[...] += sums[2]

    tile = pl.BlockSpec((tm, D), lambda i, d: (i, 0))
    row = pl.BlockSpec((1, D), lambda i, d: (0, 0))
    return pl.pallas_call(
        body, name=name, grid=(S // tm, NDEV),
        out_shape=[jax.ShapeDtypeStruct((S, D), F32)] + [jax.ShapeDtypeStruct((1, D), F32)] * 3
        + [jax.ShapeDtypeStruct(carried.shape, carried.dtype)],
        in_specs=[pl.BlockSpec((tm, nb), lambda i, d: (i, d)), pl.BlockSpec((None, D, nb), lambda i, d: (d, 0, 0)),
                  tile, tile, pl.BlockSpec((None, 1, D), lambda i, d: (layer, 0, 0)), _mod_spec(layer, 1, 2), ANY],
        out_specs=[tile, row, row, row, ANY], scratch_shapes=[pltpu.VMEM((tm, D), F32)],
        input_output_aliases={6: 4}, compiler_params=_params(("arbitrary", "arbitrary")),
    )(d_proj, wg, x, dx, norm_g3, mod, carried)


TS = 256
NCH = TS // CHUNK
HALO_BLOCKS = TS // HALO


def _halo_before(width, col_block):
    return pl.BlockSpec((HALO, width), lambda i: (jnp.maximum(i * HALO_BLOCKS - 1, 0), col_block))


def _halo_after(width, col_block):
    return pl.BlockSpec((HALO, width), lambda i: (jnp.minimum((i + 1) * HALO_BLOCKS, S // HALO - 1), col_block))


def _shift_down(ext, k):
    return pltpu.roll(ext, k, 0)[HALO:]


def _shift_up(ext, k):
    return pltpu.roll(ext, ext.shape[0] - k, 0)[:ext.shape[0] - HALO]


def _layer_norm_head(v, lg, lb):
    mu = jnp.mean(v, axis=-1, keepdims=True)
    vc = v - mu
    rstd = lax.rsqrt(jnp.mean(vc * vc, axis=-1, keepdims=True) + EPS)
    vhat = vc * rstd
    return vhat, rstd, vhat * lg + lb


def _causal_mask():
    return lax.broadcasted_iota(jnp.int32, (CHUNK, CHUNK), 0) >= lax.broadcasted_iota(jnp.int32, (CHUNK, CHUNK), 1)


def _even_mix_fwd(proj, convw, ln_g3, ln_b3, sgu_w, sgu_bcol, wl, after, name):
    def body(pj_ref, hh_ref, hc_ref, cw_ref, lg_ref, lb_ref, sw_ref, sb_ref, *rest):
        y_ref = rest[-1]
        live = (pl.program_id(0) > 0).astype(F32)
        causal = _causal_mask()
        for j in range(E_A // HEAD):
            cols = slice(j * HEAD, (j + 1) * HEAD)
            w0, w1, w2 = cw_ref[0:1, cols], cw_ref[1:2, cols], cw_ref[2:3, cols]
            lg, lb = lg_ref[:, cols], lb_ref[:, cols]
            wm = jnp.where(causal, sw_ref[j], 0.0).astype(BF16)
            bias = sb_ref[j]

            def split(s, rows, cols=cols):
                return pj_ref[rows, s * E_A + cols.start:s * E_A + cols.stop].astype(F32)

            prev_tail = hc_ref[:, cols].astype(F32) * hh_ref[:, cols].astype(F32) * live
            for n in range(NCH):
                rows = slice(n * CHUNK, (n + 1) * CHUNK)
                p = split(2, rows) * split(0, rows)
                ext = jnp.concatenate([prev_tail, p], axis=0)
                prev_tail = p[CHUNK - HALO:]
                cv = w2 * p + w1 * _shift_down(ext, 1) + w0 * _shift_down(ext, 2)
                y_ref[rows, cols] = (split(1, rows) * cv * _silu(split(3, rows))).astype(BF16)
                _, _, vn = _layer_norm_head(split(5, rows), lg, lb)
                mixed = jnp.dot(wm, vn.astype(BF16), preferred_element_type=F32) + bias
                y_ref[rows, E_A + cols.start:E_A + cols.stop] = (split(4, rows) * mixed * _silu(split(6, rows))).astype(BF16)

    const3 = lambda i: (wl, 0, 0)
    const4 = lambda i: (wl, 0, 0, 0)
    return pl.pallas_call(
        body, name=name, grid=(S // TS,), out_shape=jax.ShapeDtypeStruct((S, 2 * E_A), BF16),
        in_specs=[pl.BlockSpec((TS, 7 * E_A), lambda i: (i, 0)), _halo_before(E_A, 0), _halo_before(E_A, 2),
                  pl.BlockSpec((None, 3, E_A), const3), pl.BlockSpec((None, 1, E_A), const3),
                  pl.BlockSpec((None, 1, E_A), const3), pl.BlockSpec((None, NDEV, CHUNK, CHUNK), const4),
                  pl.BlockSpec((None, NDEV, CHUNK, 1), const4)] + [ANY] * len(after),
        out_specs=pl.BlockSpec((TS, 2 * E_A), lambda i: (i, 0)),
        compiler_params=_params(("arbitrary",)),
    )(proj, proj, proj, convw, ln_g3, ln_b3, sgu_w, sgu_bcol, *after)


def _even_mix_bwd(proj, d_ycat, convw, ln_g3, ln_b3, sgu_w, sgu_bcol, wl, name):
    nsteps = S // TS

    def body(pj_ref, hh_ref, hc_ref, hb_ref, hz_ref, dy_ref, hdy_ref, cw_ref, lg_ref, lb_ref, sw_ref, sb_ref,
             dp_ref, dcw_ref, dlg_ref, dlb_ref, dsw_ref, dsb_ref):
        step = pl.program_id(0)

        @pl.when(step == 0)
        def _():
            for ref in (dcw_ref, dlg_ref, dlb_ref, dsw_ref, dsb_ref):
                ref[...] = jnp.zeros_like(ref)

        live_before = (step > 0).astype(F32)
        live_after = (step < nsteps - 1).astype(F32)
        causal = _causal_mask()
        for j in range(E_A // HEAD):
            cols = slice(j * HEAD, (j + 1) * HEAD)
            w0, w1, w2 = cw_ref[0:1, cols], cw_ref[1:2, cols], cw_ref[2:3, cols]
            lg, lb = lg_ref[:, cols], lb_ref[:, cols]
            wmf = jnp.where(causal, sw_ref[j], 0.0)
            wm, wmt = wmf.astype(BF16), wmf.T.astype(BF16)
            bias = sb_ref[j]

            def split(s, rows, cols=cols):
                return pj_ref[rows, s * E_A + cols.start:s * E_A + cols.stop].astype(F32)

            def put(s, rows, val, cols=cols):
                dp_ref[rows, s * E_A + cols.start:s * E_A + cols.stop] = val.astype(BF16)

            ps = [split(2, slice(n * CHUNK, (n + 1) * CHUNK)) * split(0, slice(n * CHUNK, (n + 1) * CHUNK)) for n in range(NCH)]
            next_head = (hdy_ref[:, cols].astype(F32) * hb_ref[:, cols].astype(F32) * _silu(hz_ref[:, cols].astype(F32))
                         * live_after)
            acc_w = [jnp.zeros((1, HEAD), F32) for _ in range(3)]
            for n in reversed(range(NCH)):
                rows = slice(n * CHUNK, (n + 1) * CHUNK)
                p = ps[n]
                tail = ps[n - 1][CHUNK - HALO:] if n > 0 else hc_ref[:, cols].astype(F32) * hh_ref[:, cols].astype(F32) * live_before
                ext = jnp.concatenate([tail, p], axis=0)
                p1, p2 = _shift_down(ext, 1), _shift_down(ext, 2)
                cv = w2 * p + w1 * p1 + w0 * p2
                a_b, a_z = split(1, rows), split(3, rows)
                sz, dsz = _silu_and_grad(a_z)
                dya = dy_ref[rows, cols].astype(F32)
                put(1, rows, dya * cv * sz)
                put(3, rows, dya * a_b * cv * dsz)
                gcv = dya * a_b * sz
                acc_w[0] += jnp.sum(gcv * p2, axis=0, keepdims=True)
                acc_w[1] += jnp.sum(gcv * p1, axis=0, keepdims=True)
                acc_w[2] += jnp.sum(gcv * p, axis=0, keepdims=True)
                gext = jnp.concatenate([gcv, next_head], axis=0)
                next_head = gcv[:HALO]
                dpv = w2 * gcv + w1 * _shift_up(gext, 1) + w0 * _shift_up(gext, 2)
                put(2, rows, dpv * split(0, rows))
                put(0, rows, dpv * split(2, rows))
            for k in range(3):
                dcw_ref[k:k + 1, cols] += acc_w[k]

            acc_lg, acc_lb = jnp.zeros((1, HEAD), F32), jnp.zeros((1, HEAD), F32)
            acc_sw, acc_sb = jnp.zeros((CHUNK, CHUNK), F32), jnp.zeros((CHUNK, 1), F32)
            for n in range(NCH):
                rows = slice(n * CHUNK, (n + 1) * CHUNK)
                u, z = split(4, rows), split(6, rows)
                vhat, rstd, vn = _layer_norm_head(split(5, rows), lg, lb)
                vn16 = vn.astype(BF16)
                mixed = jnp.dot(wm, vn16, preferred_element_type=F32) + bias
                sz, dsz = _silu_and_grad(z)
                dyb = dy_ref[rows, E_A + cols.start:E_A + cols.stop].astype(F32)
                put(4, rows, dyb * mixed * sz)
                put(6, rows, dyb * u * mixed * dsz)
                dmix = dyb * u * sz
                dmix16 = dmix.astype(BF16)
                acc_sb += jnp.sum(dmix, axis=1, keepdims=True)
                acc_sw += lax.dot_general(dmix16, vn16, (((1,), (1,)), ((), ())), preferred_element_type=F32)
                dvn = jnp.dot(wmt, dmix16, preferred_element_type=F32)
                acc_lg += jnp.sum(dvn * vhat, axis=0, keepdims=True)
                acc_lb += jnp.sum(dvn, axis=0, keepdims=True)
                dvh = dvn * lg
                put(5, rows, rstd * (dvh - jnp.mean(dvh, axis=-1, keepdims=True)
                                     - vhat * jnp.mean(dvh * vhat, axis=-1, keepdims=True)))
            dlg_ref[:, cols] += acc_lg
            dlb_ref[:, cols] += acc_lb
            dsw_ref[j] += jnp.where(causal, acc_sw, 0.0)
            dsb_ref[j] += acc_sb

    const3 = lambda i: (wl, 0, 0)
    const4 = lambda i: (wl, 0, 0, 0)
    fixed2 = lambda i: (0, 0)
    fixed3 = lambda i: (0, 0, 0)
    return pl.pallas_call(
        body, name=name, grid=(nsteps,),
        out_shape=[jax.ShapeDtypeStruct((S, 7 * E_A), BF16), jax.ShapeDtypeStruct((3, E_A), F32),
                   jax.ShapeDtypeStruct((1, E_A), F32), jax.ShapeDtypeStruct((1, E_A), F32),
                   jax.ShapeDtypeStruct((NDEV, CHUNK, CHUNK), F32), jax.ShapeDtypeStruct((NDEV, CHUNK, 1), F32)],
        in_specs=[pl.BlockSpec((TS, 7 * E_A), lambda i: (i, 0)), _halo_before(E_A, 0), _halo_before(E_A, 2),
                  _halo_after(E_A, 1), _halo_after(E_A, 3),
                  pl.BlockSpec((TS, 2 * E_A), lambda i: (i, 0)), _halo_after(E_A, 0),
                  pl.BlockSpec((None, 3, E_A), const3), pl.BlockSpec((None, 1, E_A), const3),
                  pl.BlockSpec((None, 1, E_A), const3), pl.BlockSpec((None, NDEV, CHUNK, CHUNK), const4),
                  pl.BlockSpec((None, NDEV, CHUNK, 1), const4)],
        out_specs=[pl.BlockSpec((TS, 7 * E_A), lambda i: (i, 0)), pl.BlockSpec((3, E_A), fixed2),
                   pl.BlockSpec((1, E_A), fixed2), pl.BlockSpec((1, E_A), fixed2),
                   pl.BlockSpec((NDEV, CHUNK, CHUNK), fixed3), pl.BlockSpec((NDEV, CHUNK, 1), fixed3)],
        compiler_params=_params(("arbitrary",)),
    )(proj, proj, proj, proj, proj, d_ycat, d_ycat, convw, ln_g3, ln_b3, sgu_w, sgu_bcol)


def _window_count(step, n, win, ext_before):
    rows = CHUNK if ext_before else CHUNK + HALO
    t = step * TS + n * CHUNK + lax.broadcasted_iota(jnp.int32, (rows, 1), 0)
    return jnp.minimum(t + 1, win).astype(F32)


def _pool_weight(wp_ref, g):
    return jnp.concatenate([wp_ref[d, g] for d in range(NDEV)], axis=0)


def _pooled_chunk(p, tail, win, count):
    sums = jnp.concatenate([tail, p], axis=0)
    shift = 1
    while shift < win:
        sums = sums + pltpu.roll(sums, shift, 0)
        shift *= 2
    return sums[HALO:] / count - p


def _pool_mix_fwd(proj, wpool, pscale4, wl, after, name):
    e_c = 4 * G_C

    def body(pj_ref, hp_ref, wp_ref, ps_ref, *rest):
        y_ref, pooled_scr, yraw_scr = rest[-3:]
        step = pl.program_id(0)
        live = (step > 0).astype(F32)
        for g, win in enumerate(POOL_WINDOWS):
            for q in range(G_C // LANE):
                cols = slice(g * G_C + q * LANE, g * G_C + (q + 1) * LANE)
                tail = hp_ref[:, cols].astype(F32) * live
                for n in range(NCH):
                    rows = slice(n * CHUNK, (n + 1) * CHUNK)
                    p = pj_ref[rows, cols].astype(F32)
                    pooled_scr[rows, q * LANE:(q + 1) * LANE] = _pooled_chunk(
                        p, tail, win, _window_count(step, n, win, True)).astype(BF16)
                    tail = p[CHUNK - HALO:]
            yraw_scr[...] = jnp.dot(pooled_scr[...], _pool_weight(wp_ref, g), preferred_element_type=F32)
            for q in range(G_C // LANE):
                cols = slice(g * G_C + q * LANE, g * G_C + (q + 1) * LANE)
                for n in range(NCH):
                    rows = slice(n * CHUNK, (n + 1) * CHUNK)
                    z = pj_ref[rows, e_c + cols.start:e_c + cols.stop].astype(F32)
                    y_ref[rows, cols] = (yraw_scr[rows, q * LANE:(q + 1) * LANE] * ps_ref[:, cols] * _silu(z)).astype(BF16)

    return pl.pallas_call(
        body, name=name, grid=(S // TS,), out_shape=jax.ShapeDtypeStruct((S, e_c), BF16),
        in_specs=[pl.BlockSpec((TS, 2 * e_c), lambda i: (i, 0)), _halo_before(e_c, 0),
                  pl.BlockSpec((NDEV, 4, G_C // NDEV, G_C), lambda i: (0, 0, 0, 0)),
                  pl.BlockSpec((None, 1, e_c), lambda i: (wl, 0, 0))] + [ANY] * len(after),
        out_specs=pl.BlockSpec((TS, e_c), lambda i: (i, 0)),
        scratch_shapes=[pltpu.VMEM((TS, G_C), BF16), pltpu.VMEM((TS, G_C), F32)],
        compiler_params=_params(("arbitrary",)),
    )(proj, proj, wpool, pscale4, *after)


def _pool_mix_bwd(proj, d_ycat, wpool, pscale4, wl, name):
    e_c = 4 * G_C
    nsteps = S // TS
    rb = G_C // NDEV

    def body(pj_ref, hp_ref, hz_ref, dy_ref, hdy_ref, wp_ref, ps_ref,
             dp_ref, dps_ref, dwp_ref, pooled_scr, yraw_scr, dyraw_scr, dpool_scr, acc_w):
        step = pl.program_id(0)

        @pl.when(step == 0)
        def _():
            dps_ref[...] = jnp.zeros_like(dps_ref)
            acc_w[...] = jnp.zeros_like(acc_w)

        live_before = (step > 0).astype(F32)
        live_after = (step < nsteps - 1).astype(F32)
        for g, win in enumerate(POOL_WINDOWS):
            weight = _pool_weight(wp_ref, g)
            for q in range(G_C // LANE):
                cols = slice(g * G_C + q * LANE, g * G_C + (q + 1) * LANE)
                tail = hp_ref[:, cols].astype(F32) * live_before
                for n in range(NCH):
                    rows = slice(n * CHUNK, (n + 1) * CHUNK)
                    p = pj_ref[rows, cols].astype(F32)
                    pooled_scr[rows, q * LANE:(q + 1) * LANE] = _pooled_chunk(
                        p, tail, win, _window_count(step, n, win, True)).astype(BF16)
                    tail = p[CHUNK - HALO:]
            yraw_scr[...] = jnp.dot(pooled_scr[...], weight, preferred_element_type=F32)
            for q in range(G_C // LANE):
                cols = slice(g * G_C + q * LANE, g * G_C + (q + 1) * LANE)
                local = slice(q * LANE, (q + 1) * LANE)
                scale = ps_ref[:, cols]
                acc_ps = jnp.zeros((1, LANE), F32)
                for n in range(NCH):
                    rows = slice(n * CHUNK, (n + 1) * CHUNK)
                    sz, dsz = _silu_and_grad(pj_ref[rows, e_c + cols.start:e_c + cols.stop].astype(F32))
                    dyv = dy_ref[rows, cols].astype(F32)
                    yraw = yraw_scr[rows, local]
                    dyraw_scr[rows, local] = (dyv * scale * sz).astype(BF16)
                    acc_ps += jnp.sum(dyv * yraw * sz, axis=0, keepdims=True)
                    dp_ref[rows, e_c + cols.start:e_c + cols.stop] = (dyv * yraw * scale * dsz).astype(BF16)
                dps_ref[:, cols] += acc_ps
                dyraw_scr[TS:, local] = (hdy_ref[:, cols].astype(F32) * scale * _silu(hz_ref[:, cols].astype(F32))
                                         * live_after).astype(BF16)
            dpool_scr[...] = lax.dot_general(dyraw_scr[...], weight, (((1,), (1,)), ((), ())), preferred_element_type=F32)
            acc_w[g] += lax.dot_general(pooled_scr[...], dyraw_scr[:TS, :], (((0,), (0,)), ((), ())),
                                        preferred_element_type=F32)
            for q in range(G_C // LANE):
                cols = slice(g * G_C + q * LANE, g * G_C + (q + 1) * LANE)
                local = slice(q * LANE, (q + 1) * LANE)
                for n in range(NCH):
                    rows = slice(n * CHUNK, (n + 1) * CHUNK)
                    ext = dpool_scr[n * CHUNK:(n + 1) * CHUNK + HALO, local]
                    sums = ext / _window_count(step, n, win, False)
                    shift = 1
                    while shift < win:
                        sums = sums + pltpu.roll(sums, CHUNK + HALO - shift, 0)
                        shift *= 2
                    dp_ref[rows, cols] = (sums[:CHUNK] - ext[:CHUNK]).astype(BF16)

        @pl.when(step == nsteps - 1)
        def _():
            for g in range(4):
                for d in range(NDEV):
                    dwp_ref[d % 2, d // 2, g] = acc_w[g, d * rb:(d + 1) * rb, :].astype(BF16)

    in_specs = [pl.BlockSpec((TS, 2 * e_c), lambda i: (i, 0)), _halo_before(e_c, 0), _halo_after(e_c, 1),
                pl.BlockSpec((TS, e_c), lambda i: (i, 0)), _halo_after(e_c, 0),
                pl.BlockSpec((NDEV, 4, rb, G_C), lambda i: (0, 0, 0, 0)),
                pl.BlockSpec((None, 1, e_c), lambda i: (wl, 0, 0))]
    args = [proj, proj, proj, d_ycat, d_ycat, wpool, pscale4]
    return pl.pallas_call(
        body, name=name, grid=(nsteps,),
        out_shape=[jax.ShapeDtypeStruct((S, 2 * e_c), BF16), jax.ShapeDtypeStruct((1, e_c), F32),
                   jax.ShapeDtypeStruct((2, NDEV // 2) + wpool.shape[1:], BF16)],
        in_specs=in_specs,
        out_specs=[pl.BlockSpec((TS, 2 * e_c), lambda i: (i, 0)), pl.BlockSpec((1, e_c), lambda i: (0, 0)),
                   pl.BlockSpec((2, NDEV // 2, 4, rb, G_C), lambda i: (0, 0, 0, 0, 0))],
        scratch_shapes=[pltpu.VMEM((TS, G_C), BF16), pltpu.VMEM((TS, G_C), F32), pltpu.VMEM((TS + HALO, G_C), BF16),
                        pltpu.VMEM((TS + HALO, G_C), F32), pltpu.VMEM((4, G_C, G_C), F32)],
        compiler_params=_params(("arbitrary",)),
    )(*args)


def _adamw(w, g, m, v):
    m = ADAM_B1 * m + (1.0 - ADAM_B1) * g
    v = ADAM_B2 * v + (1.0 - ADAM_B2) * jnp.square(g)
    m_hat = m / (1.0 - ADAM_B1 ** ADAM_STEP)
    v_hat = v / (1.0 - ADAM_B2 ** ADAM_STEP)
    delta = -ADAM_LR * (m_hat / (jnp.sqrt(v_hat) + ADAM_EPS) + ADAM_WD * w)
    return delta, m, v


def _adam_sharded(w, m, v, chip_parts, landed, my_chip, carried, name):
    nl, nr, ncol = w.shape
    tr = 128
    steps = nr // tr
    nc = len(carried)

    def body(chip_ref, w_ref, m_ref, v_ref, *rest):
        parts, zones = rest[:nl], rest[nl:2 * nl]
        g_ref, d_ref, nm_ref, nv_ref = rest[2 * nl + nc:2 * nl + nc + 4]
        layer = pl.program_id(0)
        g = jnp.zeros((tr, ncol), F32)
        for l in range(nl):
            gl = parts[l][...].astype(F32)
            for q in range(3):
                gl = gl + zones[l][q].astype(F32)
            g = jnp.where(layer == l, gl, g)
        g_ref[...] = g
        d_ref[...], nm_ref[...], nv_ref[...] = _adamw(w_ref[...], g, m_ref[...], v_ref[...])

    def rows_of(l):
        return lambda layer, i, chip_ref: jnp.where(layer == l, i, jnp.where(layer < l, 0, steps - 1))

    spec = pl.BlockSpec((None, tr, ncol), lambda layer, i, chip_ref: (layer, i, 0))
    part_specs = [pl.BlockSpec((None, tr, ncol), lambda layer, i, chip_ref, l=l: (chip_ref[0], rows_of(l)(layer, i, chip_ref), 0))
                  for l in range(nl)]
    zone_specs = [pl.BlockSpec((3, tr, ncol), lambda layer, i, chip_ref, l=l: (0, rows_of(l)(layer, i, chip_ref), 0))
                  for l in range(nl)]
    grid_spec = pltpu.PrefetchScalarGridSpec(
        num_scalar_prefetch=1, grid=(nl, steps), in_specs=[spec, spec, spec] + part_specs + zone_specs + [ANY] * nc,
        out_specs=[spec] * 4 + [ANY] * nc)
    return pl.pallas_call(
        body, name=name, grid_spec=grid_spec,
        out_shape=[jax.ShapeDtypeStruct(w.shape, F32)] * 4 + [jax.ShapeDtypeStruct(a.shape, a.dtype) for a in carried],
        input_output_aliases={4 + 2 * nl + k: 4 + k for k in range(nc)},
        compiler_params=_params(("arbitrary", "arbitrary")),
    )(my_chip, w, m, v, *chip_parts, *landed, *carried)


def _adam_small(w, g, m, v, name):
    def body(w_ref, g_ref, m_ref, v_ref, d_ref, nm_ref, nv_ref):
        d_ref[...], nm_ref[...], nv_ref[...] = _adamw(w_ref[...], g_ref[...], m_ref[...], v_ref[...])

    return pl.pallas_call(body, name=name, out_shape=[jax.ShapeDtypeStruct(w.shape, F32)] * 3,
                          in_specs=[VMEM_FULL] * 4, out_specs=[VMEM_FULL] * 3, compiler_params=_params())(w, g, m, v)


def _sum_devices(gathered, name):
    _, nr, ncol = gathered.shape

    def body(g_ref, o_ref):
        acc = g_ref[0]
        for s in range(1, NDEV):
            acc = acc + g_ref[s]
        o_ref[...] = acc

    return pl.pallas_call(body, name=name, grid=(1,), out_shape=jax.ShapeDtypeStruct((nr, ncol), F32),
                          in_specs=[pl.BlockSpec((NDEV, nr, ncol), lambda i: (0, 0, 0))],
                          out_specs=pl.BlockSpec((nr, ncol), lambda i: (0, 0)),
                          compiler_params=_params(("arbitrary",)))(gathered)


def _ada_weight_adam(cact_t, dmod_mine, w, m, v):
    def body(ct_ref, dm_ref, w_ref, m_ref, v_ref, g_ref, d_ref, nm_ref, nv_ref):
        ct, dm = ct_ref[...], dm_ref[...]
        g = ct[:, 0:1] * dm[0:1, :]
        for e in range(1, NDEV):
            g = g + ct[:, e:e + 1] * dm[e:e + 1, :]
        g_ref[...] = g
        d_ref[...], nm_ref[...], nv_ref[...] = _adamw(w_ref[...], g, m_ref[...], v_ref[...])

    spec = pl.BlockSpec((None, D, ADA_NC), lambda l: (l, 0, 0))
    return pl.pallas_call(
        body, name="ada_weight_adam", grid=(DEPTH,), out_shape=[jax.ShapeDtypeStruct(w.shape, F32)] * 4,
        in_specs=[pl.BlockSpec((D, NDEV), lambda l: (0, 0)), pl.BlockSpec((None, NDEV, ADA_NC), lambda l: (l, 0, 0)),
                  spec, spec, spec],
        out_specs=[spec] * 4, compiler_params=_params(("arbitrary",)),
    )(cact_t, dmod_mine, w, m, v)


def _pad_rows(a, rows):
    a = a.reshape(-1, D)
    return jnp.pad(a, ((0, rows - a.shape[0]), (0, 0)))


def kernel(x, c, norm_g, ada_w, ada_b, ab_w_in, ab_conv_w, ab_ln_g, ab_ln_b, ab_sgu_w, ab_sgu_b, ab_w_out, c_w_in, c_pool_w, c_pool_scale, c_w_out, final_g, loss_target, m_norm_g, m_ada_w, m_ada_b, m_ab_w_in, m_ab_conv_w, m_ab_ln_g, m_ab_ln_b, m_ab_sgu_w, m_ab_sgu_b, m_ab_w_out, m_c_w_in, m_c_pool_w, m_c_pool_scale, m_c_w_out, m_final_g, v_norm_g, v_ada_w, v_ada_b, v_ab_w_in, v_ab_conv_w, v_ab_ln_g, v_ab_ln_b, v_ab_sgu_w, v_ab_sgu_b, v_ab_w_out, v_c_w_in, v_c_pool_w, v_c_pool_scale, v_c_w_out, v_final_g):
    x_pos, y_pos, c_pos = _position()
    me = _index((x_pos, y_pos, c_pos))
    core = c_pos.astype(jnp.int32).reshape(1)
    my_chip = (2 * x_pos + y_pos).astype(jnp.int32).reshape(1)
    me1 = me.astype(jnp.int32).reshape(1)
    x0 = x.reshape(S, D)
    target = loss_target.reshape(S, D)
    norm_g3 = norm_g.reshape(DEPTH, 1, D)
    ln_g3, ln_b3 = ab_ln_g.reshape(2, 1, E_A), ab_ln_b.reshape(2, 1, E_A)
    sgu_bcol = ab_sgu_b.reshape(2, NDEV, CHUNK, 1)
    rb = G_C // NDEV
    pool_w3, m_pool_w3, v_pool_w3 = (a.reshape(2, 4 * rb, G_C) for a in (c_pool_w, m_c_pool_w, v_c_pool_w))

    cact_all, mod = _ada_forward(c, ada_w, ada_b)
    convw_all, pscale_all = _gather([ab_conv_w, c_pool_scale], "gather_small_weights")
    convw = jnp.transpose(convw_all, (1, 2, 0, 3)).reshape(2, 3, E_A)
    pscale4 = jnp.transpose(pscale_all, (1, 0, 2)).reshape(2, 1, 4 * G_C)
    zones = []
    for layer in range(DEPTH):
        wl = layer // 2
        if layer % 2 == 0:
            zones.append([_to_zone(ab_w_in, wl, me1, BF16, f"cast_w_in_{layer}"), _to_zone(ab_w_out, wl, me1, BF16, f"cast_w_out_{layer}")])
        else:
            zones.append([_to_zone(c_w_in, wl, me1, BF16, f"cast_w_in_{layer}"), _to_zone(c_w_out, wl, me1, BF16, f"cast_w_out_{layer}"),
                          _to_zone(pool_w3, wl, me1, BF16, f"cast_pool_w_{layer}")])

    def gathered(flight, after, layer):
        wg = _gather_end(flight, [after], f"gather_end_{layer}")
        return [wg[0], wg[1].reshape(-1, D)] + [w.reshape(NDEV, 4, rb, G_C) for w in wg[2:]]

    flight, (mod,) = _gather_start(zones[0], [convw_all], [mod], "gather_start_0")
    flight, (mod,) = _gather_mid(flight, [], [mod], "gather_mid_0")
    next_flight, (mod,) = _gather_start(zones[1], [], [mod], "gather_start_1")
    flight = _gather_late(flight, [mod], "gather_late_0")
    xs, hts, projs, ycats, outs, gathered_w = [x0], [], [], [], [], [gathered(flight, mod, 0)]
    for layer in range(DEPTH):
        wl = layer // 2
        even = layer % 2 == 0
        wg = gathered_w[layer]
        h_t, proj = _norm_proj(xs[-1], mod, norm_g3, wg[0], layer, f"norm_proj_{layer}")
        if layer + 1 < DEPTH:
            flight, (h_t,) = _gather_mid(next_flight, [], [h_t], f"gather_mid_{layer + 1}")
            if layer + 2 < DEPTH:
                next_flight, (h_t,) = _gather_start(zones[layer + 2], [], [h_t], f"gather_start_{layer + 2}")
        if even:
            ycat = _even_mix_fwd(proj, convw, ln_g3, ln_b3, ab_sgu_w, sgu_bcol, wl, [h_t], f"even_mix_fwd_{layer}")
        else:
            ycat = _pool_mix_fwd(proj, wg[2], pscale4, wl, [h_t], f"pool_mix_fwd_{layer}")
        if layer + 1 < DEPTH:
            flight = _gather_late(flight, [ycat], f"gather_late_{layer + 1}")
        x_new, out = _out_proj(ycat, wg[1], xs[-1], mod, layer, f"out_proj_{layer}")
        if layer + 1 < DEPTH:
            gathered_w.append(gathered(flight, x_new, layer + 1))
        xs.append(x_new)
        hts.append(h_t)
        projs.append(proj)
        ycats.append(ycat)
        outs.append(out)

    dx, loss_part, d_final_g = _final_loss(xs[DEPTH], target, final_g.reshape(1, D))

    d_mod, d_norm_g = [None] * DEPTH, [None] * DEPTH
    small, scatters, landed, res = {}, {}, {}, {}

    def finish_scatter(layer, after):
        send_sems, recv_sems, chip_parts, zones = scatters[layer]
        landed[layer] = _scatter_end(chip_parts, zones, send_sems, recv_sems, after, f"scatter_end_{layer}")

    def flat(a):
        return a.reshape(a.shape[0], -1, a.shape[-1])

    def sharded_adam(k, j, layers, w, m, v, carried):
        outs4 = _adam_sharded(w, m, v, [flat(landed[l][0][j]) for l in layers], [flat(landed[l][1][j]) for l in layers],
                              my_chip, carried, "adam_" + k)
        res[k] = [o.reshape(c_pool_w.shape) if k == "c_pool_w" else o for o in outs4[:4]]
        return list(outs4[4:])

    previous = None
    for layer in reversed(range(DEPTH)):
        wl = layer // 2
        even = layer % 2 == 0
        wg = gathered_w[layer]
        carried = [] if previous is None else [scatters[previous][2][0]]
        d_ycat, grad_out, d_gate, carried = _out_bwd(dx, outs[layer], ycats[layer], wg[1], mod, layer, carried, f"out_bwd_{layer}")
        if previous is not None:
            scatters[previous][2][0] = carried[0]
        parts = [None, grad_out]
        if even:
            d_proj, d_cw, d_lg, d_lb, d_sw, d_sb = _even_mix_bwd(
                projs[layer], d_ycat, convw, ln_g3, ln_b3, ab_sgu_w, sgu_bcol, wl, f"even_mix_bwd_{layer}")
            small[layer] = (d_cw, d_lg, d_lb, d_sw, d_sb)
        else:
            d_proj, d_ps, d_pool = _pool_mix_bwd(projs[layer], d_ycat, wg[2], pscale4, wl, f"pool_mix_bwd_{layer}")
            small[layer] = (d_ps,)
            parts.append(d_pool)
        parts[0] = _weight_grad(hts[layer], d_proj, f"grad_w_in_{layer}")
        pair_send, pair_recv, parts, from_sibling = _pair_start(parts, f"pair_start_{layer}")
        if layer > 0:
            dx, d_shift, d_scale, d_norm_g[layer], parts[0] = _dh_norm_bwd(
                d_proj, wg[0], xs[layer], dx, mod, norm_g3, layer, parts[0], f"dh_norm_bwd_{layer}")
            pair_after = dx
        else:
            finish_scatter(1, d_proj)
            finish_scatter(3, d_proj)
            parts[0], = sharded_adam("c_w_in", 0, (1, 3), c_w_in, m_c_w_in, v_c_w_in, [parts[0]])
            parts[0], = sharded_adam("c_w_out", 1, (1, 3), c_w_out, m_c_w_out, v_c_w_out, [parts[0]])
            parts[0], = sharded_adam("c_pool_w", 2, (1, 3), pool_w3, m_pool_w3, v_pool_w3, [parts[0]])
            pair_after = res["c_pool_w"][0]
        parts, from_sibling = _pair_end(parts, from_sibling, pair_send, pair_recv, pair_after, f"pair_end_{layer}")
        chip_parts = [_pair_sum(p, q, core, f"pair_sum_{layer}_{j}") for j, (p, q) in enumerate(zip(parts, from_sibling))]
        send_sems, recv_sems, chip_parts, zones = _scatter_start(chip_parts, f"scatter_start_{layer}")
        if layer == 0:
            dx, d_shift, d_scale, d_norm_g[layer], chip_parts[0] = _dh_norm_bwd(
                d_proj, wg[0], xs[layer], dx, mod, norm_g3, layer, chip_parts[0], f"dh_norm_bwd_{layer}")
        scatters[layer] = [send_sems, recv_sems, chip_parts, zones]
        previous = layer
        d_mod[layer] = jnp.concatenate([d_shift, d_scale, d_gate], axis=0)
    grad_x = dx.reshape(x.shape)

    sections = [("norm_g", jnp.concatenate(d_norm_g, axis=0), 8),
                ("d_mod", jnp.concatenate(d_mod, axis=0), 16),
                ("ab_ln_g", jnp.concatenate([small[0][1], small[2][1]], axis=0), 8),
                ("ab_ln_b", jnp.concatenate([small[0][2], small[2][2]], axis=0), 8),
                ("ab_sgu_b", jnp.stack([small[0][4], small[2][4]]), 8),
                ("final_g", d_final_g, 8),
                ("loss", jnp.pad(loss_part, ((0, 0), (0, D - LANE))), 8),
                ("ab_conv_w", jnp.stack([small[0][0], small[2][0]]), 8),
                ("c_pool_scale", jnp.concatenate([small[1][0], small[3][0]], axis=0), 8),
                ("ab_sgu_w", jnp.stack([small[0][3], small[2][3]]), 256)]
    offsets, at = {}, 0
    for name, _, rows in sections:
        offsets[name] = (at, rows)
        at += rows
    packed = jnp.concatenate([_pad_rows(a, rows) for _, a, rows in sections] + [jnp.zeros((-at % 16, D), F32)], axis=0)
    small_flight, (mod,) = _gather_start([_to_zone(packed[None], 0, me1, F32, "place_small_grads")], [], [mod], "gather_small_start")

    finish_scatter(2, mod)
    finish_scatter(0, mod)
    sharded_adam("ab_w_out", 1, (0, 2), ab_w_out, m_ab_w_out, v_ab_w_out, [])
    sharded_adam("ab_w_in", 0, (0, 2), ab_w_in, m_ab_w_in, v_ab_w_in, [])

    last = res["ab_w_in"][0]
    small_flight, _ = _gather_mid(small_flight, [last], [], "gather_small_mid")
    small_flight = _gather_late(small_flight, [last], "gather_small_late")
    small_grads = _gather_end(small_flight, [last], "gather_small_end")[0]
    summed = _sum_devices(small_grads, "sum_small_grads")

    def section(name, nrows, src=summed):
        start = offsets[name][0]
        return src[..., start:start + nrows, :]

    loss = section("loss", 1)[0, 0]
    grads = {
        "norm_g": section("norm_g", DEPTH),
        "ada_b": section("d_mod", 3 * DEPTH).reshape(DEPTH, 3 * D),
        "ab_ln_g": section("ab_ln_g", 2), "ab_ln_b": section("ab_ln_b", 2),
        "ab_sgu_b": section("ab_sgu_b", 2).reshape(ab_sgu_b.shape),
        "final_g": section("final_g", 1),
        "ab_sgu_w": section("ab_sgu_w", 256).reshape(ab_sgu_w.shape),
        "ab_conv_w": lax.dynamic_slice_in_dim(section("ab_conv_w", 6).reshape(2, 3, E_A), me * HEAD, HEAD, axis=2),
        "c_pool_scale": lax.dynamic_slice_in_dim(section("c_pool_scale", 4).reshape(2, 4 * G_C), me * 256, 256, axis=1),
    }
    small_w = {"norm_g": (norm_g, m_norm_g, v_norm_g), "ada_b": (ada_b, m_ada_b, v_ada_b),
               "ab_ln_g": (ab_ln_g, m_ab_ln_g, v_ab_ln_g), "ab_ln_b": (ab_ln_b, m_ab_ln_b, v_ab_ln_b),
               "ab_sgu_b": (ab_sgu_b, m_ab_sgu_b, v_ab_sgu_b),
               "final_g": (final_g.reshape(1, D), m_final_g.reshape(1, D), v_final_g.reshape(1, D)),
               "ab_sgu_w": (ab_sgu_w, m_ab_sgu_w, v_ab_sgu_w), "ab_conv_w": (ab_conv_w, m_ab_conv_w, v_ab_conv_w),
               "c_pool_scale": (c_pool_scale, m_c_pool_scale, v_c_pool_scale)}
    for k, (w, m, v) in small_w.items():
        res[k] = [grads[k]] + list(_adam_small(w, grads[k], m, v, "adam_" + k))
    res["final_g"] = [a.reshape(D) for a in res["final_g"]]

    dmod_all = section("d_mod", 3 * DEPTH, small_grads).reshape(NDEV, DEPTH, 3 * D)
    dmod_mine = jnp.transpose(lax.dynamic_slice_in_dim(dmod_all, me * ADA_NC, ADA_NC, axis=2), (1, 0, 2))
    res["ada_w"] = _ada_weight_adam(jnp.transpose(cact_all.reshape(NDEV, D)), dmod_mine, ada_w, m_ada_w, v_ada_w)

    order = ["norm_g", "ada_w", "ada_b", "ab_w_in", "ab_conv_w", "ab_ln_g", "ab_ln_b", "ab_sgu_w", "ab_sgu_b",
             "ab_w_out", "c_w_in", "c_pool_w", "c_pool_scale", "c_w_out", "final_g"]
    return (loss, grad_x, *[res[k][0] for k in order], *[res[k][1] for k in order],
            *[res[k][2] for k in order], *[res[k][3] for k in order])
```

```python
import jax
import jax.numpy as jnp
from jax import lax
from jax.experimental import pallas as pl
from jax.experimental.pallas import tpu as pltpu

F32, BF16 = jnp.float32, jnp.bfloat16
S, D = 2048, 1024
NDEV = 8
DEPTH = 4
EPS = 1e-6
E_A = 1024
HEAD = 128
CHUNK = 128
POOL_WINDOWS = (2, 4, 8, 16)
G_C = 512
HALO = 16
ADA_NC = 384
MIB = 1024 * 1024
LANE = 128

ADAM_LR, ADAM_B1, ADAM_B2, ADAM_EPS, ADAM_WD, ADAM_STEP = 0.001, 0.9, 0.999, 1e-08, 0.01, 10

ANY = pl.BlockSpec(memory_space=pl.ANY)
VMEM_FULL = pl.BlockSpec(memory_space=pltpu.VMEM)
IN_HBM = pl.BlockSpec(memory_space=pltpu.HBM)
SEMAPHORES = pl.BlockSpec(memory_space=pltpu.SEMAPHORE)
IN_FLIGHT = pltpu.SideEffectType.DATAFLOW_SIDE_EFFECTING


V7X_VMEM_MIB = 64
VMEM_LIMIT_MIB = V7X_VMEM_MIB - 4


def _params(semantics=None):
    return pltpu.CompilerParams(dimension_semantics=semantics, vmem_limit_bytes=VMEM_LIMIT_MIB * MIB)


def _silu(z):
    return z * jax.nn.sigmoid(z)


def _silu_and_grad(z):
    sig = jax.nn.sigmoid(z)
    return z * sig, sig * (1.0 + z * (1.0 - sig))


def _position():
    return lax.axis_index("x"), lax.axis_index("y"), lax.axis_index("c")


def _index(pos):
    return 4 * pos[0] + 2 * pos[1] + pos[2]


def _peer(pos, k):
    flipped = tuple(1 - p if (k >> (2 - b)) & 1 else p for b, p in enumerate(pos))
    return flipped, _index(flipped)


def _remote(src, dst, send_sem, recv_sem, device):
    return pltpu.make_async_remote_copy(src_ref=src, dst_ref=dst, send_sem=send_sem, recv_sem=recv_sem,
                                        device_id=device, device_id_type=pl.DeviceIdType.MESH)


def _gather(arrays, name):
    n = len(arrays)
    out_shape = [jax.ShapeDtypeStruct((NDEV,) + a.shape, a.dtype) for a in arrays]

    def body(*refs):
        ins, outs = refs[:n], refs[n:2 * n]
        send_sems, recv_sems, own_sems = refs[2 * n:]
        x, y, c = _position()
        me = _index((x, y, c))
        sibling = (x, y, 1 - c)
        chips = [(1 - x, y), (x, 1 - y), (1 - x, 1 - y)]

        def block_copy(j, k, owner, to, src=None):
            rows = outs[j].at[_index(owner)]
            return _remote(rows if src is None else src, rows, send_sems.at[j, k], recv_sems.at[j, k], to)

        own, first, passed = [], [], []
        for j in range(n):
            own.append(pltpu.make_async_copy(ins[j], outs[j].at[me], own_sems.at[j]))
            first.append(block_copy(j, 0, (x, y, c), sibling, src=ins[j]))
            first += [block_copy(j, 1 + q, (x, y, c), (*chip, c), src=ins[j]) for q, chip in enumerate(chips)]
        for copy in own + first:
            copy.start()
        for q, chip in enumerate(chips):
            for j in range(n):
                block_copy(j, 1 + q, (*chip, c), (x, y, c)).wait_recv()
                forward = block_copy(j, 4 + q, (*chip, c), sibling)
                forward.start()
                passed.append(forward)
        for j in range(n):
            block_copy(j, 0, sibling, (x, y, c)).wait_recv()
            for q, chip in enumerate(chips):
                block_copy(j, 4 + q, (*chip, 1 - c), (x, y, c)).wait_recv()
        for copy in first + passed:
            copy.wait_send()
        for copy in own:
            copy.wait()

    return pl.pallas_call(
        body, name=name, out_shape=out_shape, in_specs=[ANY] * n, out_specs=[ANY] * n,
        scratch_shapes=[pltpu.SemaphoreType.DMA((n, NDEV - 1)), pltpu.SemaphoreType.DMA((n, NDEV - 1)),
                        pltpu.SemaphoreType.DMA((n,))],
    )(*arrays)


def _pair_start(parts, name):
    n = len(parts)
    lands = [_in_hbm(lax.empty(p.shape[1:], p.dtype)) for p in parts]

    def body(*refs):
        ins, zones = refs[:n], refs[n:2 * n]
        send_sems, recv_sems = refs[2 * n:2 * n + 2]
        x, y, c = _position()
        for j in range(n):
            _remote(ins[j].at[1 - c], zones[j], send_sems.at[j], recv_sems.at[j], (x, y, 1 - c)).start()

    outs = pl.pallas_call(
        body, name=name,
        out_shape=(pltpu.SemaphoreType.DMA((n,)), pltpu.SemaphoreType.DMA((n,)),
                   *[pltpu.HBM(p.shape, p.dtype) for p in parts], *[pltpu.HBM(z.shape, z.dtype) for z in lands]),
        in_specs=[IN_HBM] * (2 * n), out_specs=(SEMAPHORES, SEMAPHORES, *[IN_HBM] * (2 * n)),
        input_output_aliases={j: 2 + j for j in range(2 * n)},
        compiler_params=pltpu.CompilerParams(has_side_effects=IN_FLIGHT),
    )(*[_in_hbm(p) for p in parts], *lands)
    return outs[0], outs[1], list(outs[2:2 + n]), list(outs[2 + n:])


def _pair_end(parts, zones, send_sems, recv_sems, after, name):
    n = len(parts)

    def body(*refs):
        ins, zs = refs[:n], refs[n:2 * n]
        s, r = refs[2 * n:2 * n + 2]
        me = _position()
        for j in range(n):
            copy = _remote(ins[j].at[0], zs[j], s.at[j], r.at[j], me)
            copy.wait_send()
            copy.wait_recv()

    outs = pl.pallas_call(
        body, name=name,
        out_shape=(*[pltpu.HBM(p.shape, p.dtype) for p in parts], *[pltpu.HBM(z.shape, z.dtype) for z in zones]),
        in_specs=[IN_HBM] * (2 * n) + [SEMAPHORES, SEMAPHORES, ANY], out_specs=tuple([IN_HBM] * (2 * n)),
        input_output_aliases={j: j for j in range(2 * n)},
        compiler_params=pltpu.CompilerParams(has_side_effects=IN_FLIGHT),
    )(*parts, *zones, send_sems, recv_sems, after)
    return list(outs[:n]), list(outs[n:])


def _pair_sum(part, from_sibling, core, name):
    ncol = part.shape[-1]
    p3 = part.reshape(2, -1, ncol)
    q2 = from_sibling.reshape(-1, ncol)
    nrows = q2.shape[0]
    tr = 512

    def body(core_ref, p_ref, q_ref, o_ref):
        o_ref[...] = (p_ref[...].astype(F32) + q_ref[...].astype(F32)).astype(BF16)

    grid_spec = pltpu.PrefetchScalarGridSpec(
        num_scalar_prefetch=1, grid=(nrows // tr,),
        in_specs=[pl.BlockSpec((None, tr, ncol), lambda i, core_ref: (core_ref[0], i, 0)),
                  pl.BlockSpec((tr, ncol), lambda i, core_ref: (i, 0))],
        out_specs=pl.BlockSpec((tr, ncol), lambda i, core_ref: (i, 0)))
    out = pl.pallas_call(body, name=name, grid_spec=grid_spec, out_shape=jax.ShapeDtypeStruct(q2.shape, BF16),
                         compiler_params=_params(("arbitrary",)))(core, p3, q2)
    return out.reshape(from_sibling.shape)


def _in_hbm(a):
    return pltpu.with_memory_space_constraint(a, pltpu.HBM)


def _chips(x, y):
    return [(1 - x, y), (x, 1 - y), (1 - x, 1 - y)]


def _to_zone(a, wl, me, dtype, name):
    _, rows, cols = a.shape
    tr = 256 if rows % 256 == 0 else rows

    def body(me_ref, a_ref, o_ref):
        o_ref[...] = a_ref[...].astype(dtype)

    grid_spec = pltpu.PrefetchScalarGridSpec(
        num_scalar_prefetch=1, grid=(rows // tr,),
        in_specs=[pl.BlockSpec((None, tr, cols), lambda i, me_ref: (wl, i, 0))],
        out_specs=pl.BlockSpec((None, tr, cols), lambda i, me_ref: (me_ref[0], i, 0)))
    return pl.pallas_call(body, name=name, grid_spec=grid_spec, out_shape=jax.ShapeDtypeStruct((NDEV, rows, cols), dtype),
                          compiler_params=_params(("arbitrary",)))(me, a)


def _halves(block):
    rows = block.shape[0] // 2
    return block.at[pl.ds(0, rows)], block.at[pl.ds(rows, rows)]


def _around(x, y, c):
    return (x, y, 1 - c), (1 - x, y, c), (x, 1 - y, c), (1 - x, 1 - y, c)


def _gather_step1(zs, send, recv, pos):
    sibling, xn, yn, _ = _around(*pos)
    for j, z in enumerate(zs):
        mine = z.at[_index(pos)]
        for k, peer in enumerate((sibling, xn, yn)):
            _remote(mine, mine, send.at[3 * j + k], recv.at[3 * j + k], peer).start()


def _gather_step2(zs, recv1, send, recv, pos):
    sibling, xn, yn, _ = _around(*pos)
    for j, z in enumerate(zs):
        xb, yb = z.at[_index(xn)], z.at[_index(yn)]
        _remote(xb, xb, send.at[4 * j], recv1.at[3 * j + 1], pos).wait_recv()
        _remote(yb, yb, send.at[4 * j], recv1.at[3 * j + 2], pos).wait_recv()
        _remote(xb, xb, send.at[4 * j], recv.at[4 * j], sibling).start()
        _remote(yb, yb, send.at[4 * j + 1], recv.at[4 * j + 1], sibling).start()
        first, second = _halves(xb)[0], _halves(yb)[1]
        _remote(first, first, send.at[4 * j + 2], recv.at[4 * j + 2], yn).start()
        _remote(second, second, send.at[4 * j + 3], recv.at[4 * j + 3], xn).start()


def _gather_step3(zs, recv2, send, recv, pos):
    sibling, _, _, diagonal = _around(*pos)
    for j, z in enumerate(zs):
        db = z.at[_index(diagonal)]
        first, second = _halves(db)
        _remote(first, first, send.at[j], recv2.at[4 * j + 2], pos).wait_recv()
        _remote(second, second, send.at[j], recv2.at[4 * j + 3], pos).wait_recv()
        _remote(db, db, send.at[j], recv.at[j], sibling).start()


def _gather_step4(zs, send1, recv1, send2, recv2, send3, recv3, pos):
    x, y, c = pos
    sibling = (x, y, 1 - c)
    _, sx, sy, sd = _around(*sibling)
    for j, z in enumerate(zs):
        for owner, send, recv, k in ((sibling, send1, recv1, 3 * j), (sx, send2, recv2, 4 * j), (sy, send2, recv2, 4 * j + 1),
                                     (sd, send3, recv3, j)):
            block = z.at[_index(owner)]
            _remote(block, block, send.at[k], recv.at[k], pos).wait_recv()
    for j, z in enumerate(zs):
        block = z.at[0]
        half = _halves(block)[0]
        for ref, send, recv, k in ([(block, send1, recv1, 3 * j + k) for k in range(3)]
                                   + [(block, send2, recv2, 4 * j), (block, send2, recv2, 4 * j + 1),
                                      (half, send2, recv2, 4 * j + 2), (half, send2, recv2, 4 * j + 3), (block, send3, recv3, j)]):
            _remote(ref, ref, send.at[k], recv.at[k], pos).wait_send()


def _flight_call(step, name, zones, sems_in, nsems_out, after, carried):
    n, m, k = len(zones), len(carried), len(sems_in)

    def body(*refs):
        zs = refs[:n]
        given = refs[n + m:n + m + k]
        made = refs[n + m + k + len(after):n + m + k + len(after) + (2 if nsems_out else 0)]
        step(zs, *given, *made, _position())

    sem_out = (pltpu.SemaphoreType.DMA((nsems_out,)),) * 2 if nsems_out else ()
    outs = pl.pallas_call(
        body, name=name,
        out_shape=(*sem_out, *[pltpu.HBM(z.shape, z.dtype) for z in zones], *[jax.ShapeDtypeStruct(a.shape, a.dtype) for a in carried]),
        in_specs=[IN_HBM] * n + [ANY] * m + [SEMAPHORES] * k + [ANY] * len(after),
        out_specs=(*[SEMAPHORES] * len(sem_out), *[IN_HBM] * n, *[ANY] * m),
        input_output_aliases={j: len(sem_out) + j for j in range(n + m)},
        compiler_params=pltpu.CompilerParams(has_side_effects=IN_FLIGHT),
    )(*[_in_hbm(z) for z in zones], *carried, *sems_in, *after)
    sems = list(outs[:len(sem_out)])
    return sems, list(outs[len(sem_out):len(sem_out) + n]), list(outs[len(sem_out) + n:])


def _gather_start(zones, after, carried, name):
    (send1, recv1), zones, carried = _flight_call(_gather_step1, name, zones, [], 3 * len(zones), after, carried)
    return {"s1": send1, "r1": recv1, "zones": zones}, carried


def _gather_mid(flight, after, carried, name):
    step = lambda zs, recv1, send, recv, pos: _gather_step2(zs, recv1, send, recv, pos)
    (send2, recv2), zones, carried = _flight_call(step, name, flight["zones"], [flight["r1"]], 4 * len(flight["zones"]), after, carried)
    return {**flight, "s2": send2, "r2": recv2, "zones": zones}, carried


def _gather_late(flight, after, name):
    step = lambda zs, recv2, send, recv, pos: _gather_step3(zs, recv2, send, recv, pos)
    (send3, recv3), zones, _ = _flight_call(step, name, flight["zones"], [flight["r2"]], len(flight["zones"]), after, [])
    return {**flight, "s3": send3, "r3": recv3, "zones": zones}


def _gather_end(flight, after, name):
    sems = [flight[k] for k in ("s1", "r1", "s2", "r2", "s3", "r3")]
    _, zones, _ = _flight_call(_gather_step4, name, flight["zones"], sems, 0, after, [])
    return zones


def _scatter_start(parts, name):
    n = len(parts)
    lands = [_in_hbm(lax.empty((3,) + p.shape[1:], p.dtype)) for p in parts]

    def body(*refs):
        ins, zones = refs[:n], refs[n:2 * n]
        send_sems, recv_sems = refs[2 * n:2 * n + 2]
        x, y, c = _position()
        for j in range(n):
            for q, (px, py) in enumerate(_chips(x, y)):
                _remote(ins[j].at[2 * px + py], zones[j].at[q], send_sems.at[3 * j + q], recv_sems.at[3 * j + q],
                        (px, py, c)).start()

    outs = pl.pallas_call(
        body, name=name,
        out_shape=(pltpu.SemaphoreType.DMA((3 * n,)), pltpu.SemaphoreType.DMA((3 * n,)),
                   *[pltpu.HBM(p.shape, p.dtype) for p in parts], *[pltpu.HBM(z.shape, z.dtype) for z in lands]),
        in_specs=[IN_HBM] * (2 * n), out_specs=(SEMAPHORES, SEMAPHORES, *[IN_HBM] * (2 * n)),
        input_output_aliases={j: 2 + j for j in range(2 * n)},
        compiler_params=pltpu.CompilerParams(has_side_effects=IN_FLIGHT),
    )(*[_in_hbm(p) for p in parts], *lands)
    return outs[0], outs[1], list(outs[2:2 + n]), list(outs[2 + n:])


def _scatter_end(parts, zones, send_sems, recv_sems, after, name):
    n = len(parts)

    def body(*refs):
        ins, zs = refs[:n], refs[n:2 * n]
        s, r = refs[2 * n:2 * n + 2]
        me = _position()
        for j in range(n):
            for q in range(3):
                copy = _remote(ins[j].at[0], zs[j].at[q], s.at[3 * j + q], r.at[3 * j + q], me)
                copy.wait_send()
                copy.wait_recv()

    outs = pl.pallas_call(
        body, name=name,
        out_shape=(*[pltpu.HBM(p.shape, p.dtype) for p in parts], *[pltpu.HBM(z.shape, z.dtype) for z in zones]),
        in_specs=[IN_HBM] * (2 * n) + [SEMAPHORES, SEMAPHORES, ANY], out_specs=tuple([IN_HBM] * (2 * n)),
        input_output_aliases={j: j for j in range(2 * n)},
        compiler_params=pltpu.CompilerParams(has_side_effects=IN_FLIGHT),
    )(*parts, *zones, send_sems, recv_sems, after)
    return list(outs[:n]), list(outs[n:])


def _ada_forward(c, ada_w, ada_b):
    def body(c_ref, w_ref, b_ref, cact_ref, mod_ref, gbuf, modrow, send_sems, recv_sems):
        pos = _position()
        me = _index(pos)

        def to_all(ref, round_):
            copies = []
            for k in range(1, NDEV):
                peer, _ = _peer(pos, k)
                copy = pltpu.make_async_remote_copy(
                    src_ref=ref.at[me], dst_ref=ref.at[me], send_sem=send_sems.at[round_, k - 1],
                    recv_sem=recv_sems.at[round_, k - 1], device_id=peer, device_id_type=pl.DeviceIdType.MESH)
                copy.start()
                copies.append(copy)
            for copy in copies:
                copy.wait()

        cact_ref[me] = _silu(c_ref[...])
        to_all(cact_ref, 0)
        rows = lax.broadcasted_iota(jnp.int32, (NDEV, D), 0)
        cact = jnp.zeros((NDEV, D), F32)
        for e in range(NDEV):
            cact = jnp.where(rows == e, cact_ref[e], cact)
        cact = cact.astype(BF16)
        for l in range(DEPTH):
            gbuf[me, l] = jnp.dot(cact, w_ref[l].astype(BF16), preferred_element_type=F32)
        to_all(gbuf, 1)
        mine = lax.broadcasted_iota(jnp.int32, (NDEV, ADA_NC), 0) == me
        for l in range(DEPTH):
            for d in range(NDEV):
                modrow[:, d * ADA_NC:(d + 1) * ADA_NC] = jnp.sum(jnp.where(mine, gbuf[d, l], 0.0), axis=0, keepdims=True)
            full = modrow[...] + b_ref[l:l + 1, :]
            for w in range(3):
                mod_ref[l, w] = full[:, w * D:(w + 1) * D]

    return pl.pallas_call(
        body, name="ada_forward",
        out_shape=[jax.ShapeDtypeStruct((NDEV, 1, D), F32), jax.ShapeDtypeStruct((DEPTH, 3, 1, D), F32)],
        in_specs=[VMEM_FULL] * 3, out_specs=[VMEM_FULL] * 2,
        scratch_shapes=[pltpu.VMEM((NDEV, DEPTH, NDEV, ADA_NC), F32), pltpu.VMEM((1, 3 * D), F32),
                        pltpu.SemaphoreType.DMA((2, NDEV - 1)), pltpu.SemaphoreType.DMA((2, NDEV - 1))],
        compiler_params=_params(),
    )(c, ada_w, ada_b)


def _mod_spec(layer, which, ngrid):
    index = {1: lambda i: (layer, which, 0, 0), 2: lambda i, j: (layer, which, 0, 0)}[ngrid]
    return pl.BlockSpec((None, None, 1, D), index)


def _norm_proj(x, mod, norm_g3, wg, layer, name):
    nb = wg.shape[-1]
    tm = 1024

    def body(x_ref, g_ref, shift_ref, scale_ref, w_ref, ht_ref, p_ref, h_ref):
        @pl.when(pl.program_id(1) == 0)
        def _():
            xv = x_ref[...]
            r = lax.rsqrt(jnp.mean(xv * xv, axis=-1, keepdims=True) + EPS)
            hn = xv * r * g_ref[...]
            h = hn * (1.0 + scale_ref[...]) + shift_ref[...]
            h_ref[...] = h.astype(BF16)
            ht_ref[...] = h.T.astype(BF16)

        p_ref[...] = jnp.dot(h_ref[...], w_ref[...], preferred_element_type=F32).astype(BF16)

    return pl.pallas_call(
        body, name=name, grid=(S // tm, NDEV),
        out_shape=[jax.ShapeDtypeStruct((D, S), BF16), jax.ShapeDtypeStruct((S, NDEV * nb), BF16)],
        in_specs=[pl.BlockSpec((tm, D), lambda i, d: (i, 0)),
                  pl.BlockSpec((None, 1, D), lambda i, d: (layer, 0, 0)),
                  _mod_spec(layer, 0, 2), _mod_spec(layer, 1, 2),
                  pl.BlockSpec((None, D, nb), lambda i, d: (d, 0, 0))],
        out_specs=[pl.BlockSpec((D, tm), lambda i, d: (0, i)), pl.BlockSpec((tm, nb), lambda i, d: (i, d))],
        scratch_shapes=[pltpu.VMEM((tm, D), BF16)],
        compiler_params=_params(("arbitrary", "arbitrary")),
    )(x, norm_g3, mod, mod, wg)


def _out_proj(ycat, w_out, x, mod, layer, name):
    tm = 512
    e = w_out.shape[0]

    def body(y_ref, w_ref, x_ref, gate_ref, xn_ref, o_ref):
        acc = jnp.dot(y_ref[...], w_ref[...], preferred_element_type=F32)
        o_ref[...] = acc.astype(BF16)
        xn_ref[...] = x_ref[...] + gate_ref[...] * acc

    return pl.pallas_call(
        body, name=name, grid=(S // tm,),
        out_shape=[jax.ShapeDtypeStruct((S, D), F32), jax.ShapeDtypeStruct((S, D), BF16)],
        in_specs=[pl.BlockSpec((tm, e), lambda i: (i, 0)), pl.BlockSpec((e, D), lambda i: (0, 0)),
                  pl.BlockSpec((tm, D), lambda i: (i, 0)), _mod_spec(layer, 2, 1)],
        out_specs=[pl.BlockSpec((tm, D), lambda i: (i, 0))] * 2,
        compiler_params=_params(("arbitrary",)),
    )(ycat, w_out, x, mod)


def _final_loss(x, target, final_g2):
    tm = 256

    def body(x_ref, t_ref, g_ref, dx_ref, loss_ref, dg_ref):
        @pl.when(pl.program_id(0) == 0)
        def _():
            loss_ref[...] = jnp.zeros_like(loss_ref)
            dg_ref[...] = jnp.zeros_like(dg_ref)

        xv, g = x_ref[...], g_ref[...]
        r = lax.rsqrt(jnp.mean(xv * xv, axis=-1, keepdims=True) + EPS)
        xn = xv * r
        err = xn * g - t_ref[...]
        loss_ref[...] += 0.5 * jnp.sum(jnp.mean(err * err, axis=-1, keepdims=True), axis=0, keepdims=True)
        dy = err * (1.0 / D)
        dg_ref[...] += jnp.sum(dy * xn, axis=0, keepdims=True)
        u = dy * g
        dx_ref[...] = r * (u - xn * jnp.mean(xn * u, axis=-1, keepdims=True))

    tile = pl.BlockSpec((tm, D), lambda i: (i, 0))
    row = pl.BlockSpec((1, D), lambda i: (0, 0))
    return pl.pallas_call(
        body, name="final_loss", grid=(S // tm,),
        out_shape=[jax.ShapeDtypeStruct((S, D), F32), jax.ShapeDtypeStruct((1, LANE), F32), jax.ShapeDtypeStruct((1, D), F32)],
        in_specs=[tile, tile, row], out_specs=[tile, pl.BlockSpec((1, LANE), lambda i: (0, 0)), row],
        compiler_params=_params(("arbitrary",)),
    )(x, target, final_g2)


def _out_bwd(dx, out, ycat, w_out, mod, layer, carried, name):
    tm = 512
    nsteps = S // tm
    e = ycat.shape[1]
    rb = e // NDEV
    nc = len(carried)

    def body(dx_ref, o_ref, y_ref, w_ref, gate_ref, *rest):
        dy_ref, gw_ref, dgate_ref = rest[nc:nc + 3]
        acc = rest[-1]
        step = pl.program_id(0)

        @pl.when(step == 0)
        def _():
            dgate_ref[...] = jnp.zeros_like(dgate_ref)
            acc[...] = jnp.zeros_like(acc)

        dxv = dx_ref[...]
        d_out = (gate_ref[...] * dxv).astype(BF16)
        dgate_ref[...] += jnp.sum(dxv * o_ref[...].astype(F32), axis=0, keepdims=True)
        dy_ref[...] = lax.dot_general(d_out, w_ref[...], (((1,), (1,)), ((), ())), preferred_element_type=F32).astype(BF16)
        acc[...] += lax.dot_general(y_ref[...], d_out, (((0,), (0,)), ((), ())), preferred_element_type=F32)

        @pl.when(step == nsteps - 1)
        def _():
            for d in range(NDEV):
                gw_ref[d % 2, d // 2] = acc[d * rb:(d + 1) * rb, :].astype(BF16)

    tile = pl.BlockSpec((tm, D), lambda i: (i, 0))
    wide = pl.BlockSpec((tm, e), lambda i: (i, 0))
    outs = pl.pallas_call(
        body, name=name, grid=(nsteps,),
        out_shape=[jax.ShapeDtypeStruct((S, e), BF16), jax.ShapeDtypeStruct((2, NDEV // 2, rb, D), BF16),
                   jax.ShapeDtypeStruct((1, D), F32)] + [jax.ShapeDtypeStruct(a.shape, a.dtype) for a in carried],
        in_specs=[tile, tile, wide, pl.BlockSpec((e, D), lambda i: (0, 0)), _mod_spec(layer, 2, 1)] + [ANY] * nc,
        out_specs=[wide, pl.BlockSpec((2, NDEV // 2, rb, D), lambda i: (0, 0, 0, 0)), pl.BlockSpec((1, D), lambda i: (0, 0))]
        + [ANY] * nc,
        scratch_shapes=[pltpu.VMEM((e, D), F32)],
        input_output_aliases={5 + k: 3 + k for k in range(nc)},
        compiler_params=_params(("arbitrary",)),
    )(dx, out, ycat, w_out, mod, *carried)
    return outs[0], outs[1], outs[2], list(outs[3:])


def _weight_grad(h_t, d_proj, name):
    nb = d_proj.shape[1] // NDEV

    def body(ht_ref, dp_ref, o_ref):
        o_ref[...] = jnp.dot(ht_ref[...], dp_ref[...], preferred_element_type=F32).astype(BF16)

    return pl.pallas_call(
        body, name=name, grid=(NDEV,), out_shape=jax.ShapeDtypeStruct((2, NDEV // 2, D, nb), BF16),
        in_specs=[pl.BlockSpec((D, S), lambda d: (0, 0)), pl.BlockSpec((S, nb), lambda d: (0, d))],
        out_specs=pl.BlockSpec((None, None, D, nb), lambda d: (d % 2, d // 2, 0, 0)),
        compiler_params=_params(("arbitrary",)),
    )(h_t, d_proj)


def _dh_norm_bwd(d_proj, wg, x, dx, mod, norm_g3, layer, carried, name):
    nb = wg.shape[-1]
    tm = 1024
    rc = 128

    def body(dp_ref, w_ref, x_ref, dx_ref, g_ref, scale_ref, carried_ref,
             dxi_ref, dshift_ref, dscale_ref, dg_ref, carried_out, acc):
        i, d = pl.program_id(0), pl.program_id(1)
        part = lax.dot_general(dp_ref[...], w_ref[...], (((1,), (1,)), ((), ())), preferred_element_type=F32)

        @pl.when(d == 0)
        def _():
            acc[...] = part

        @pl.when(d != 0)
        def _():
            acc[...] += part

        @pl.when(jnp.logical_and(i == 0, d == 0))
        def _():
            dshift_ref[...] = jnp.zeros_like(dshift_ref)
            dscale_ref[...] = jnp.zeros_like(dscale_ref)
            dg_ref[...] = jnp.zeros_like(dg_ref)

        @pl.when(d == NDEV - 1)
        def _():
            g = g_ref[...]
            scale1 = 1.0 + scale_ref[...]

            def chunk(k, sums):
                rows = pl.ds(pl.multiple_of(k * rc, rc), rc)
                xv, dhv = x_ref[rows, :], acc[rows, :]
                r = lax.rsqrt(jnp.mean(xv * xv, axis=-1, keepdims=True) + EPS)
                xn = xv * r
                dhn = dhv * scale1
                u = dhn * g
                dxi_ref[rows, :] = dx_ref[rows, :] + r * (u - xn * jnp.mean(xn * u, axis=-1, keepdims=True))
                return (sums[0] + jnp.sum(dhv, axis=0, keepdims=True),
                        sums[1] + jnp.sum(dhv * (xn * g), axis=0, keepdims=True),
                        sums[2] + jnp.sum(dhn * xn, axis=0, keepdims=True))

            zero = jnp.zeros((1, D), F32)
            sums = lax.fori_loop(0, tm // rc, chunk, (zero, zero, zero))
            dshift_ref[...] += sums[0]
            dscale_ref[...] += sums[1]
            dg_ref[...] += sums[2]

    tile = pl.BlockSpec((tm, D), lambda i, d: (i, 0))
    row = pl.BlockSpec((1, D), lambda i, d: (0, 0))
    return pl.pallas_call(
        body, name=name, grid=(S // tm, NDEV),
        out_shape=[jax.ShapeDtypeStruct((S, D), F32)] + [jax.ShapeDtypeStruct((1, D), F32)] * 3
        + [jax.ShapeDtypeStruct(carried.shape, carried.dtype)],
        in_specs=[pl.BlockSpec((tm, nb), lambda i, d: (i, d)), pl.BlockSpec((None, D, nb), lambda i, d: (d, 0, 0)),
                  tile, tile, pl.BlockSpec((None, 1, D), lambda i, d: (layer, 0, 0)), _mod_spec(layer, 1, 2), ANY],
        out_specs=[tile, row, row, row, ANY], scratch_shapes=[pltpu.VMEM((tm, D), F32)],
        input_output_aliases={6: 4}, compiler_params=_params(("arbitrary", "arbitrary")),
    )(d_proj, wg, x, dx, norm_g3, mod, carried)


TS = 256
NCH = TS // CHUNK
HALO_BLOCKS = TS // HALO


def _halo_before(width, col_block):
    return pl.BlockSpec((HALO, width), lambda i: (jnp.maximum(i * HALO_BLOCKS - 1, 0), col_block))


def _halo_after(width, col_block):
    return pl.BlockSpec((HALO, width), lambda i: (jnp.minimum((i + 1) * HALO_BLOCKS, S // HALO - 1), col_block))


def _shift_down(ext, k):
    return pltpu.roll(ext, k, 0)[HALO:]


def _shift_up(ext, k):
    return pltpu.roll(ext, ext.shape[0] - k, 0)[:ext.shape[0] - HALO]


def _layer_norm_head(v, lg, lb):
    mu = jnp.mean(v, axis=-1, keepdims=True)
    vc = v - mu
    rstd = lax.rsqrt(jnp.mean(vc * vc, axis=-1, keepdims=True) + EPS)
    vhat = vc * rstd
    return vhat, rstd, vhat * lg + lb


def _causal_mask():
    return lax.broadcasted_iota(jnp.int32, (CHUNK, CHUNK), 0) >= lax.broadcasted_iota(jnp.int32, (CHUNK, CHUNK), 1)


def _even_mix_fwd(proj, convw, ln_g3, ln_b3, sgu_w, sgu_bcol, wl, after, name):
    def body(pj_ref, hh_ref, hc_ref, cw_ref, lg_ref, lb_ref, sw_ref, sb_ref, *rest):
        y_ref = rest[-1]
        live = (pl.program_id(0) > 0).astype(F32)
        causal = _causal_mask()
        for j in range(E_A // HEAD):
            cols = slice(j * HEAD, (j + 1) * HEAD)
            w0, w1, w2 = cw_ref[0:1, cols], cw_ref[1:2, cols], cw_ref[2:3, cols]
            lg, lb = lg_ref[:, cols], lb_ref[:, cols]
            wm = jnp.where(causal, sw_ref[j], 0.0).astype(BF16)
            bias = sb_ref[j]

            def split(s, rows, cols=cols):
                return pj_ref[rows, s * E_A + cols.start:s * E_A + cols.stop].astype(F32)

            prev_tail = hc_ref[:, cols].astype(F32) * hh_ref[:, cols].astype(F32) * live
            for n in range(NCH):
                rows = slice(n * CHUNK, (n + 1) * CHUNK)
                p = split(2, rows) * split(0, rows)
                ext = jnp.concatenate([prev_tail, p], axis=0)
                prev_tail = p[CHUNK - HALO:]
                cv = w2 * p + w1 * _shift_down(ext, 1) + w0 * _shift_down(ext, 2)
                y_ref[rows, cols] = (split(1, rows) * cv * _silu(split(3, rows))).astype(BF16)
                _, _, vn = _layer_norm_head(split(5, rows), lg, lb)
                mixed = jnp.dot(wm, vn.astype(BF16), preferred_element_type=F32) + bias
                y_ref[rows, E_A + cols.start:E_A + cols.stop] = (split(4, rows) * mixed * _silu(split(6, rows))).astype(BF16)

    const3 = lambda i: (wl, 0, 0)
    const4 = lambda i: (wl, 0, 0, 0)
    return pl.pallas_call(
        body, name=name, grid=(S // TS,), out_shape=jax.ShapeDtypeStruct((S, 2 * E_A), BF16),
        in_specs=[pl.BlockSpec((TS, 7 * E_A), lambda i: (i, 0)), _halo_before(E_A, 0), _halo_before(E_A, 2),
                  pl.BlockSpec((None, 3, E_A), const3), pl.BlockSpec((None, 1, E_A), const3),
                  pl.BlockSpec((None, 1, E_A), const3), pl.BlockSpec((None, NDEV, CHUNK, CHUNK), const4),
                  pl.BlockSpec((None, NDEV, CHUNK, 1), const4)] + [ANY] * len(after),
        out_specs=pl.BlockSpec((TS, 2 * E_A), lambda i: (i, 0)),
        compiler_params=_params(("arbitrary",)),
    )(proj, proj, proj, convw, ln_g3, ln_b3, sgu_w, sgu_bcol, *after)


def _even_mix_bwd(proj, d_ycat, convw, ln_g3, ln_b3, sgu_w, sgu_bcol, wl, name):
    nsteps = S // TS

    def body(pj_ref, hh_ref, hc_ref, hb_ref, hz_ref, dy_ref, hdy_ref, cw_ref, lg_ref, lb_ref, sw_ref, sb_ref,
             dp_ref, dcw_ref, dlg_ref, dlb_ref, dsw_ref, dsb_ref):
        step = pl.program_id(0)

        @pl.when(step == 0)
        def _():
            for ref in (dcw_ref, dlg_ref, dlb_ref, dsw_ref, dsb_ref):
                ref[...] = jnp.zeros_like(ref)

        live_before = (step > 0).astype(F32)
        live_after = (step < nsteps - 1).astype(F32)
        causal = _causal_mask()
        for j in range(E_A // HEAD):
            cols = slice(j * HEAD, (j + 1) * HEAD)
            w0, w1, w2 = cw_ref[0:1, cols], cw_ref[1:2, cols], cw_ref[2:3, cols]
            lg, lb = lg_ref[:, cols], lb_ref[:, cols]
            wmf = jnp.where(causal, sw_ref[j], 0.0)
            wm, wmt = wmf.astype(BF16), wmf.T.astype(BF16)
            bias = sb_ref[j]

            def split(s, rows, cols=cols):
                return pj_ref[rows, s * E_A + cols.start:s * E_A + cols.stop].astype(F32)

            def put(s, rows, val, cols=cols):
                dp_ref[rows, s * E_A + cols.start:s * E_A + cols.stop] = val.astype(BF16)

            ps = [split(2, slice(n * CHUNK, (n + 1) * CHUNK)) * split(0, slice(n * CHUNK, (n + 1) * CHUNK)) for n in range(NCH)]
            next_head = (hdy_ref[:, cols].astype(F32) * hb_ref[:, cols].astype(F32) * _silu(hz_ref[:, cols].astype(F32))
                         * live_after)
            acc_w = [jnp.zeros((1, HEAD), F32) for _ in range(3)]
            for n in reversed(range(NCH)):
                rows = slice(n * CHUNK, (n + 1) * CHUNK)
                p = ps[n]
                tail = ps[n - 1][CHUNK - HALO:] if n > 0 else hc_ref[:, cols].astype(F32) * hh_ref[:, cols].astype(F32) * live_before
                ext = jnp.concatenate([tail, p], axis=0)
                p1, p2 = _shift_down(ext, 1), _shift_down(ext, 2)
                cv = w2 * p + w1 * p1 + w0 * p2
                a_b, a_z = split(1, rows), split(3, rows)
                sz, dsz = _silu_and_grad(a_z)
                dya = dy_ref[rows, cols].astype(F32)
                put(1, rows, dya * cv * sz)
                put(3, rows, dya * a_b * cv * dsz)
                gcv = dya * a_b * sz
                acc_w[0] += jnp.sum(gcv * p2, axis=0, keepdims=True)
                acc_w[1] += jnp.sum(gcv * p1, axis=0, keepdims=True)
                acc_w[2] += jnp.sum(gcv * p, axis=0, keepdims=True)
                gext = jnp.concatenate([gcv, next_head], axis=0)
                next_head = gcv[:HALO]
                dpv = w2 * gcv + w1 * _shift_up(gext, 1) + w0 * _shift_up(gext, 2)
                put(2, rows, dpv * split(0, rows))
                put(0, rows, dpv * split(2, rows))
            for k in range(3):
                dcw_ref[k:k + 1, cols] += acc_w[k]

            acc_lg, acc_lb = jnp.zeros((1, HEAD), F32), jnp.zeros((1, HEAD), F32)
            acc_sw, acc_sb = jnp.zeros((CHUNK, CHUNK), F32), jnp.zeros((CHUNK, 1), F32)
            for n in range(NCH):
                rows = slice(n * CHUNK, (n + 1) * CHUNK)
                u, z = split(4, rows), split(6, rows)
                vhat, rstd, vn = _layer_norm_head(split(5, rows), lg, lb)
                vn16 = vn.astype(BF16)
                mixed = jnp.dot(wm, vn16, preferred_element_type=F32) + bias
                sz, dsz = _silu_and_grad(z)
                dyb = dy_ref[rows, E_A + cols.start:E_A + cols.stop].astype(F32)
                put(4, rows, dyb * mixed * sz)
                put(6, rows, dyb * u * mixed * dsz)
                dmix = dyb * u * sz
                dmix16 = dmix.astype(BF16)
                acc_sb += jnp.sum(dmix, axis=1, keepdims=True)
                acc_sw += lax.dot_general(dmix16, vn16, (((1,), (1,)), ((), ())), preferred_element_type=F32)
                dvn = jnp.dot(wmt, dmix16, preferred_element_type=F32)
                acc_lg += jnp.sum(dvn * vhat, axis=0, keepdims=True)
                acc_lb += jnp.sum(dvn, axis=0, keepdims=True)
                dvh = dvn * lg
                put(5, rows, rstd * (dvh - jnp.mean(dvh, axis=-1, keepdims=True)
                                     - vhat * jnp.mean(dvh * vhat, axis=-1, keepdims=True)))
            dlg_ref[:, cols] += acc_lg
            dlb_ref[:, cols] += acc_lb
            dsw_ref[j] += jnp.where(causal, acc_sw, 0.0)
            dsb_ref[j] += acc_sb

    const3 = lambda i: (wl, 0, 0)
    const4 = lambda i: (wl, 0, 0, 0)
    fixed2 = lambda i: (0, 0)
    fixed3 = lambda i: (0, 0, 0)
    return pl.pallas_call(
        body, name=name, grid=(nsteps,),
        out_shape=[jax.ShapeDtypeStruct((S, 7 * E_A), BF16), jax.ShapeDtypeStruct((3, E_A), F32),
                   jax.ShapeDtypeStruct((1, E_A), F32), jax.ShapeDtypeStruct((1, E_A), F32),
                   jax.ShapeDtypeStruct((NDEV, CHUNK, CHUNK), F32), jax.ShapeDtypeStruct((NDEV, CHUNK, 1), F32)],
        in_specs=[pl.BlockSpec((TS, 7 * E_A), lambda i: (i, 0)), _halo_before(E_A, 0), _halo_before(E_A, 2),
                  _halo_after(E_A, 1), _halo_after(E_A, 3),
                  pl.BlockSpec((TS, 2 * E_A), lambda i: (i, 0)), _halo_after(E_A, 0),
                  pl.BlockSpec((None, 3, E_A), const3), pl.BlockSpec((None, 1, E_A), const3),
                  pl.BlockSpec((None, 1, E_A), const3), pl.BlockSpec((None, NDEV, CHUNK, CHUNK), const4),
                  pl.BlockSpec((None, NDEV, CHUNK, 1), const4)],
        out_specs=[pl.BlockSpec((TS, 7 * E_A), lambda i: (i, 0)), pl.BlockSpec((3, E_A), fixed2),
                   pl.BlockSpec((1, E_A), fixed2), pl.BlockSpec((1, E_A), fixed2),
                   pl.BlockSpec((NDEV, CHUNK, CHUNK), fixed3), pl.BlockSpec((NDEV, CHUNK, 1), fixed3)],
        compiler_params=_params(("arbitrary",)),
    )(proj, proj, proj, proj, proj, d_ycat, d_ycat, convw, ln_g3, ln_b3, sgu_w, sgu_bcol)


def _window_count(step, n, win, ext_before):
    rows = CHUNK if ext_before else CHUNK + HALO
    t = step * TS + n * CHUNK + lax.broadcasted_iota(jnp.int32, (rows, 1), 0)
    return jnp.minimum(t + 1, win).astype(F32)


def _pool_weight(wp_ref, g):
    return jnp.concatenate([wp_ref[d, g] for d in range(NDEV)], axis=0)


def _pooled_chunk(p, tail, win, count):
    sums = jnp.concatenate([tail, p], axis=0)
    shift = 1
    while shift < win:
        sums = sums + pltpu.roll(sums, shift, 0)
        shift *= 2
    return sums[HALO:] / count - p


def _pool_mix_fwd(proj, wpool, pscale4, wl, after, name):
    e_c = 4 * G_C

    def body(pj_ref, hp_ref, wp_ref, ps_ref, *rest):
        y_ref, pooled_scr, yraw_scr = rest[-3:]
        step = pl.program_id(0)
        live = (step > 0).astype(F32)
        for g, win in enumerate(POOL_WINDOWS):
            for q in range(G_C // LANE):
                cols = slice(g * G_C + q * LANE, g * G_C + (q + 1) * LANE)
                tail = hp_ref[:, cols].astype(F32) * live
                for n in range(NCH):
                    rows = slice(n * CHUNK, (n + 1) * CHUNK)
                    p = pj_ref[rows, cols].astype(F32)
                    pooled_scr[rows, q * LANE:(q + 1) * LANE] = _pooled_chunk(
                        p, tail, win, _window_count(step, n, win, True)).astype(BF16)
                    tail = p[CHUNK - HALO:]
            yraw_scr[...] = jnp.dot(pooled_scr[...], _pool_weight(wp_ref, g), preferred_element_type=F32)
            for q in range(G_C // LANE):
                cols = slice(g * G_C + q * LANE, g * G_C + (q + 1) * LANE)
                for n in range(NCH):
                    rows = slice(n * CHUNK, (n + 1) * CHUNK)
                    z = pj_ref[rows, e_c + cols.start:e_c + cols.stop].astype(F32)
                    y_ref[rows, cols] = (yraw_scr[rows, q * LANE:(q + 1) * LANE] * ps_ref[:, cols] * _silu(z)).astype(BF16)

    return pl.pallas_call(
        body, name=name, grid=(S // TS,), out_shape=jax.ShapeDtypeStruct((S, e_c), BF16),
        in_specs=[pl.BlockSpec((TS, 2 * e_c), lambda i: (i, 0)), _halo_before(e_c, 0),
                  pl.BlockSpec((NDEV, 4, G_C // NDEV, G_C), lambda i: (0, 0, 0, 0)),
                  pl.BlockSpec((None, 1, e_c), lambda i: (wl, 0, 0))] + [ANY] * len(after),
        out_specs=pl.BlockSpec((TS, e_c), lambda i: (i, 0)),
        scratch_shapes=[pltpu.VMEM((TS, G_C), BF16), pltpu.VMEM((TS, G_C), F32)],
        compiler_params=_params(("arbitrary",)),
    )(proj, proj, wpool, pscale4, *after)


def _pool_mix_bwd(proj, d_ycat, wpool, pscale4, wl, name):
    e_c = 4 * G_C
    nsteps = S // TS
    rb = G_C // NDEV

    def body(pj_ref, hp_ref, hz_ref, dy_ref, hdy_ref, wp_ref, ps_ref,
             dp_ref, dps_ref, dwp_ref, pooled_scr, yraw_scr, dyraw_scr, dpool_scr, acc_w):
        step = pl.program_id(0)

        @pl.when(step == 0)
        def _():
            dps_ref[...] = jnp.zeros_like(dps_ref)
            acc_w[...] = jnp.zeros_like(acc_w)

        live_before = (step > 0).astype(F32)
        live_after = (step < nsteps - 1).astype(F32)
        for g, win in enumerate(POOL_WINDOWS):
            weight = _pool_weight(wp_ref, g)
            for q in range(G_C // LANE):
                cols = slice(g * G_C + q * LANE, g * G_C + (q + 1) * LANE)
                tail = hp_ref[:, cols].astype(F32) * live_before
                for n in range(NCH):
                    rows = slice(n * CHUNK, (n + 1) * CHUNK)
                    p = pj_ref[rows, cols].astype(F32)
                    pooled_scr[rows, q * LANE:(q + 1) * LANE] = _pooled_chunk(
                        p, tail, win, _window_count(step, n, win, True)).astype(BF16)
                    tail = p[CHUNK - HALO:]
            yraw_scr[...] = jnp.dot(pooled_scr[...], weight, preferred_element_type=F32)
            for q in range(G_C // LANE):
                cols = slice(g * G_C + q * LANE, g * G_C + (q + 1) * LANE)
                local = slice(q * LANE, (q + 1) * LANE)
                scale = ps_ref[:, cols]
                acc_ps = jnp.zeros((1, LANE), F32)
                for n in range(NCH):
                    rows = slice(n * CHUNK, (n + 1) * CHUNK)
                    sz, dsz = _silu_and_grad(pj_ref[rows, e_c + cols.start:e_c + cols.stop].astype(F32))
                    dyv = dy_ref[rows, cols].astype(F32)
                    yraw = yraw_scr[rows, local]
                    dyraw_scr[rows, local] = (dyv * scale * sz).astype(BF16)
                    acc_ps += jnp.sum(dyv * yraw * sz, axis=0, keepdims=True)
                    dp_ref[rows, e_c + cols.start:e_c + cols.stop] = (dyv * yraw * scale * dsz).astype(BF16)
                dps_ref[:, cols] += acc_ps
                dyraw_scr[TS:, local] = (hdy_ref[:, cols].astype(F32) * scale * _silu(hz_ref[:, cols].astype(F32))
                                         * live_after).astype(BF16)
            dpool_scr[...] = lax.dot_general(dyraw_scr[...], weight, (((1,), (1,)), ((), ())), preferred_element_type=F32)
            acc_w[g] += lax.dot_general(pooled_scr[...], dyraw_scr[:TS, :], (((0,), (0,)), ((), ())),
                                        preferred_element_type=F32)
            for q in range(G_C // LANE):
                cols = slice(g * G_C + q * LANE, g * G_C + (q + 1) * LANE)
                local = slice(q * LANE, (q + 1) * LANE)
                for n in range(NCH):
                    rows = slice(n * CHUNK, (n + 1) * CHUNK)
                    ext = dpool_scr[n * CHUNK:(n + 1) * CHUNK + HALO, local]
                    sums = ext / _window_count(step, n, win, False)
                    shift = 1
                    while shift < win:
                        sums = sums + pltpu.roll(sums, CHUNK + HALO - shift, 0)
                        shift *= 2
                    dp_ref[rows, cols] = (sums[:CHUNK] - ext[:CHUNK]).astype(BF16)

        @pl.when(step == nsteps - 1)
        def _():
            for g in range(4):
                for d in range(NDEV):
                    dwp_ref[d % 2, d // 2, g] = acc_w[g, d * rb:(d + 1) * rb, :].astype(BF16)

    in_specs = [pl.BlockSpec((TS, 2 * e_c), lambda i: (i, 0)), _halo_before(e_c, 0), _halo_after(e_c, 1),
                pl.BlockSpec((TS, e_c), lambda i: (i, 0)), _halo_after(e_c, 0),
                pl.BlockSpec((NDEV, 4, rb, G_C), lambda i: (0, 0, 0, 0)),
                pl.BlockSpec((None, 1, e_c), lambda i: (wl, 0, 0))]
    args = [proj, proj, proj, d_ycat, d_ycat, wpool, pscale4]
    return pl.pallas_call(
        body, name=name, grid=(nsteps,),
        out_shape=[jax.ShapeDtypeStruct((S, 2 * e_c), BF16), jax.ShapeDtypeStruct((1, e_c), F32),
                   jax.ShapeDtypeStruct((2, NDEV // 2) + wpool.shape[1:], BF16)],
        in_specs=in_specs,
        out_specs=[pl.BlockSpec((TS, 2 * e_c), lambda i: (i, 0)), pl.BlockSpec((1, e_c), lambda i: (0, 0)),
                   pl.BlockSpec((2, NDEV // 2, 4, rb, G_C), lambda i: (0, 0, 0, 0, 0))],
        scratch_shapes=[pltpu.VMEM((TS, G_C), BF16), pltpu.VMEM((TS, G_C), F32), pltpu.VMEM((TS + HALO, G_C), BF16),
                        pltpu.VMEM((TS + HALO, G_C), F32), pltpu.VMEM((4, G_C, G_C), F32)],
        compiler_params=_params(("arbitrary",)),
    )(*args)


def _adamw(w, g, m, v):
    m = ADAM_B1 * m + (1.0 - ADAM_B1) * g
    v = ADAM_B2 * v + (1.0 - ADAM_B2) * jnp.square(g)
    m_hat = m / (1.0 - ADAM_B1 ** ADAM_STEP)
    v_hat = v / (1.0 - ADAM_B2 ** ADAM_STEP)
    delta = -ADAM_LR * (m_hat / (jnp.sqrt(v_hat) + ADAM_EPS) + ADAM_WD * w)
    return delta, m, v


def _adam_sharded(w, m, v, chip_parts, landed, my_chip, carried, name):
    nl, nr, ncol = w.shape
    tr = 128
    steps = nr // tr
    nc = len(carried)

    def body(chip_ref, w_ref, m_ref, v_ref, *rest):
        parts, zones = rest[:nl], rest[nl:2 * nl]
        g_ref, d_ref, nm_ref, nv_ref = rest[2 * nl + nc:2 * nl + nc + 4]
        layer = pl.program_id(0)
        g = jnp.zeros((tr, ncol), F32)
        for l in range(nl):
            gl = parts[l][...].astype(F32)
            for q in range(3):
                gl = gl + zones[l][q].astype(F32)
            g = jnp.where(layer == l, gl, g)
        g_ref[...] = g
        d_ref[...], nm_ref[...], nv_ref[...] = _adamw(w_ref[...], g, m_ref[...], v_ref[...])

    def rows_of(l):
        return lambda layer, i, chip_ref: jnp.where(layer == l, i, jnp.where(layer < l, 0, steps - 1))

    spec = pl.BlockSpec((None, tr, ncol), lambda layer, i, chip_ref: (layer, i, 0))
    part_specs = [pl.BlockSpec((None, tr, ncol), lambda layer, i, chip_ref, l=l: (chip_ref[0], rows_of(l)(layer, i, chip_ref), 0))
                  for l in range(nl)]
    zone_specs = [pl.BlockSpec((3, tr, ncol), lambda layer, i, chip_ref, l=l: (0, rows_of(l)(layer, i, chip_ref), 0))
                  for l in range(nl)]
    grid_spec = pltpu.PrefetchScalarGridSpec(
        num_scalar_prefetch=1, grid=(nl, steps), in_specs=[spec, spec, spec] + part_specs + zone_specs + [ANY] * nc,
        out_specs=[spec] * 4 + [ANY] * nc)
    return pl.pallas_call(
        body, name=name, grid_spec=grid_spec,
        out_shape=[jax.ShapeDtypeStruct(w.shape, F32)] * 4 + [jax.ShapeDtypeStruct(a.shape, a.dtype) for a in carried],
        input_output_aliases={4 + 2 * nl + k: 4 + k for k in range(nc)},
        compiler_params=_params(("arbitrary", "arbitrary")),
    )(my_chip, w, m, v, *chip_parts, *landed, *carried)


def _adam_small(w, g, m, v, name):
    def body(w_ref, g_ref, m_ref, v_ref, d_ref, nm_ref, nv_ref):
        d_ref[...], nm_ref[...], nv_ref[...] = _adamw(w_ref[...], g_ref[...], m_ref[...], v_ref[...])

    return pl.pallas_call(body, name=name, out_shape=[jax.ShapeDtypeStruct(w.shape, F32)] * 3,
                          in_specs=[VMEM_FULL] * 4, out_specs=[VMEM_FULL] * 3, compiler_params=_params())(w, g, m, v)


def _sum_devices(gathered, name):
    _, nr, ncol = gathered.shape

    def body(g_ref, o_ref):
        acc = g_ref[0].astype(F32)
        for s in range(1, NDEV):
            acc = acc + g_ref[s].astype(F32)
        o_ref[...] = acc

    return pl.pallas_call(body, name=name, grid=(1,), out_shape=jax.ShapeDtypeStruct((nr, ncol), F32),
                          in_specs=[pl.BlockSpec((NDEV, nr, ncol), lambda i: (0, 0, 0))],
                          out_specs=pl.BlockSpec((nr, ncol), lambda i: (0, 0)),
                          compiler_params=_params(("arbitrary",)))(gathered)


def _ada_weight_adam(cact_t, dmod_mine, w, m, v):
    def body(ct_ref, dm_ref, w_ref, m_ref, v_ref, g_ref, d_ref, nm_ref, nv_ref):
        ct, dm = ct_ref[...], dm_ref[...]
        g = ct[:, 0:1] * dm[0:1, :]
        for e in range(1, NDEV):
            g = g + ct[:, e:e + 1] * dm[e:e + 1, :]
        g_ref[...] = g
        d_ref[...], nm_ref[...], nv_ref[...] = _adamw(w_ref[...], g, m_ref[...], v_ref[...])

    spec = pl.BlockSpec((None, D, ADA_NC), lambda l: (l, 0, 0))
    return pl.pallas_call(
        body, name="ada_weight_adam", grid=(DEPTH,), out_shape=[jax.ShapeDtypeStruct(w.shape, F32)] * 4,
        in_specs=[pl.BlockSpec((D, NDEV), lambda l: (0, 0)), pl.BlockSpec((None, NDEV, ADA_NC), lambda l: (l, 0, 0)),
                  spec, spec, spec],
        out_specs=[spec] * 4, compiler_params=_params(("arbitrary",)),
    )(cact_t, dmod_mine, w, m, v)


def _pad_rows(a, rows):
    a = a.reshape(-1, D)
    return jnp.pad(a, ((0, rows - a.shape[0]), (0, 0)))


def kernel(x, c, norm_g, ada_w, ada_b, ab_w_in, ab_conv_w, ab_ln_g, ab_ln_b, ab_sgu_w, ab_sgu_b, ab_w_out, c_w_in, c_pool_w, c_pool_scale, c_w_out, final_g, loss_target, m_norm_g, m_ada_w, m_ada_b, m_ab_w_in, m_ab_conv_w, m_ab_ln_g, m_ab_ln_b, m_ab_sgu_w, m_ab_sgu_b, m_ab_w_out, m_c_w_in, m_c_pool_w, m_c_pool_scale, m_c_w_out, m_final_g, v_norm_g, v_ada_w, v_ada_b, v_ab_w_in, v_ab_conv_w, v_ab_ln_g, v_ab_ln_b, v_ab_sgu_w, v_ab_sgu_b, v_ab_w_out, v_c_w_in, v_c_pool_w, v_c_pool_scale, v_c_w_out, v_final_g):
    x_pos, y_pos, c_pos = _position()
    me = _index((x_pos, y_pos, c_pos))
    core = c_pos.astype(jnp.int32).reshape(1)
    my_chip = (2 * x_pos + y_pos).astype(jnp.int32).reshape(1)
    me1 = me.astype(jnp.int32).reshape(1)
    x0 = x.reshape(S, D)
    target = loss_target.reshape(S, D)
    norm_g3 = norm_g.reshape(DEPTH, 1, D)
    ln_g3, ln_b3 = ab_ln_g.reshape(2, 1, E_A), ab_ln_b.reshape(2, 1, E_A)
    sgu_bcol = ab_sgu_b.reshape(2, NDEV, CHUNK, 1)
    rb = G_C // NDEV
    pool_w3, m_pool_w3, v_pool_w3 = (a.reshape(2, 4 * rb, G_C) for a in (c_pool_w, m_c_pool_w, v_c_pool_w))

    cact_all, mod = _ada_forward(c, ada_w, ada_b)
    convw_all, pscale_all = _gather([ab_conv_w, c_pool_scale], "gather_small_weights")
    convw = jnp.transpose(convw_all, (1, 2, 0, 3)).reshape(2, 3, E_A)
    pscale4 = jnp.transpose(pscale_all, (1, 0, 2)).reshape(2, 1, 4 * G_C)
    zones = []
    for layer in range(DEPTH):
        wl = layer // 2
        if layer % 2 == 0:
            zones.append([_to_zone(ab_w_in, wl, me1, BF16, f"cast_w_in_{layer}"), _to_zone(ab_w_out, wl, me1, BF16, f"cast_w_out_{layer}")])
        else:
            zones.append([_to_zone(c_w_in, wl, me1, BF16, f"cast_w_in_{layer}"), _to_zone(c_w_out, wl, me1, BF16, f"cast_w_out_{layer}"),
                          _to_zone(pool_w3, wl, me1, BF16, f"cast_pool_w_{layer}")])

    def gathered(flight, after, layer):
        wg = _gather_end(flight, [after], f"gather_end_{layer}")
        return [wg[0], wg[1].reshape(-1, D)] + [w.reshape(NDEV, 4, rb, G_C) for w in wg[2:]]

    flight, (mod,) = _gather_start(zones[0], [convw_all], [mod], "gather_start_0")
    flight, (mod,) = _gather_mid(flight, [], [mod], "gather_mid_0")
    next_flight, (mod,) = _gather_start(zones[1], [], [mod], "gather_start_1")
    flight = _gather_late(flight, [mod], "gather_late_0")
    xs, hts, projs, ycats, outs, gathered_w = [x0], [], [], [], [], [gathered(flight, mod, 0)]
    for layer in range(DEPTH):
        wl = layer // 2
        even = layer % 2 == 0
        wg = gathered_w[layer]
        h_t, proj = _norm_proj(xs[-1], mod, norm_g3, wg[0], layer, f"norm_proj_{layer}")
        if layer + 1 < DEPTH:
            flight, (h_t,) = _gather_mid(next_flight, [], [h_t], f"gather_mid_{layer + 1}")
            if layer + 2 < DEPTH:
                next_flight, (h_t,) = _gather_start(zones[layer + 2], [], [h_t], f"gather_start_{layer + 2}")
        if even:
            ycat = _even_mix_fwd(proj, convw, ln_g3, ln_b3, ab_sgu_w, sgu_bcol, wl, [h_t], f"even_mix_fwd_{layer}")
        else:
            ycat = _pool_mix_fwd(proj, wg[2], pscale4, wl, [h_t], f"pool_mix_fwd_{layer}")
        if layer + 1 < DEPTH:
            flight = _gather_late(flight, [ycat], f"gather_late_{layer + 1}")
        x_new, out = _out_proj(ycat, wg[1], xs[-1], mod, layer, f"out_proj_{layer}")
        if layer + 1 < DEPTH:
            gathered_w.append(gathered(flight, x_new, layer + 1))
        xs.append(x_new)
        hts.append(h_t)
        projs.append(proj)
        ycats.append(ycat)
        outs.append(out)

    dx, loss_part, d_final_g = _final_loss(xs[DEPTH], target, final_g.reshape(1, D))

    d_mod, d_norm_g = [None] * DEPTH, [None] * DEPTH
    small, scatters, landed, res = {}, {}, {}, {}

    def finish_scatter(layer, after):
        send_sems, recv_sems, chip_parts, zones = scatters[layer]
        landed[layer] = _scatter_end(chip_parts, zones, send_sems, recv_sems, after, f"scatter_end_{layer}")

    def flat(a):
        return a.reshape(a.shape[0], -1, a.shape[-1])

    def sharded_adam(k, j, layers, w, m, v, carried):
        outs4 = _adam_sharded(w, m, v, [flat(landed[l][0][j]) for l in layers], [flat(landed[l][1][j]) for l in layers],
                              my_chip, carried, "adam_" + k)
        res[k] = [o.reshape(c_pool_w.shape) if k == "c_pool_w" else o for o in outs4[:4]]
        return list(outs4[4:])

    previous = None
    for layer in reversed(range(DEPTH)):
        wl = layer // 2
        even = layer % 2 == 0
        wg = gathered_w[layer]
        carried = [] if previous is None else [scatters[previous][2][0]]
        d_ycat, grad_out, d_gate, carried = _out_bwd(dx, outs[layer], ycats[layer], wg[1], mod, layer, carried, f"out_bwd_{layer}")
        if previous is not None:
            scatters[previous][2][0] = carried[0]
        parts = [None, grad_out]
        if even:
            d_proj, d_cw, d_lg, d_lb, d_sw, d_sb = _even_mix_bwd(
                projs[layer], d_ycat, convw, ln_g3, ln_b3, ab_sgu_w, sgu_bcol, wl, f"even_mix_bwd_{layer}")
            small[layer] = (d_cw, d_lg, d_lb, d_sw, d_sb)
        else:
            d_proj, d_ps, d_pool = _pool_mix_bwd(projs[layer], d_ycat, wg[2], pscale4, wl, f"pool_mix_bwd_{layer}")
            small[layer] = (d_ps,)
            parts.append(d_pool)
        parts[0] = _weight_grad(hts[layer], d_proj, f"grad_w_in_{layer}")
        pair_send, pair_recv, parts, from_sibling = _pair_start(parts, f"pair_start_{layer}")
        if layer > 0:
            dx, d_shift, d_scale, d_norm_g[layer], parts[0] = _dh_norm_bwd(
                d_proj, wg[0], xs[layer], dx, mod, norm_g3, layer, parts[0], f"dh_norm_bwd_{layer}")
            pair_after = dx
        else:
            finish_scatter(1, d_proj)
            finish_scatter(3, d_proj)
            parts[0], = sharded_adam("c_w_out", 1, (1, 3), c_w_out, m_c_w_out, v_c_w_out, [parts[0]])
            parts[0], = sharded_adam("c_pool_w", 2, (1, 3), pool_w3, m_pool_w3, v_pool_w3, [parts[0]])
            pair_after = res["c_pool_w"][0]
        parts, from_sibling = _pair_end(parts, from_sibling, pair_send, pair_recv, pair_after, f"pair_end_{layer}")
        chip_parts = [_pair_sum(p, q, core, f"pair_sum_{layer}_{j}") for j, (p, q) in enumerate(zip(parts, from_sibling))]
        send_sems, recv_sems, chip_parts, zones = _scatter_start(chip_parts, f"scatter_start_{layer}")
        if layer == 0:
            chip_parts[0], = sharded_adam("c_w_in", 0, (1, 3), c_w_in, m_c_w_in, v_c_w_in, [chip_parts[0]])
            dx, d_shift, d_scale, d_norm_g[layer], chip_parts[0] = _dh_norm_bwd(
                d_proj, wg[0], xs[layer], dx, mod, norm_g3, layer, chip_parts[0], f"dh_norm_bwd_{layer}")
        scatters[layer] = [send_sems, recv_sems, chip_parts, zones]
        previous = layer
        d_mod[layer] = jnp.concatenate([d_shift, d_scale, d_gate], axis=0)
    grad_x = dx.reshape(x.shape)

    sections = [("norm_g", jnp.concatenate(d_norm_g, axis=0), 8),
                ("d_mod", jnp.concatenate(d_mod, axis=0), 16),
                ("ab_ln_g", jnp.concatenate([small[0][1], small[2][1]], axis=0), 8),
                ("ab_ln_b", jnp.concatenate([small[0][2], small[2][2]], axis=0), 8),
                ("ab_sgu_b", jnp.stack([small[0][4], small[2][4]]), 8),
                ("final_g", d_final_g, 8),
                ("ab_conv_w", jnp.stack([small[0][0], small[2][0]]), 8),
                ("c_pool_scale", jnp.concatenate([small[1][0], small[3][0]], axis=0), 8),
                ("ab_sgu_w", jnp.stack([small[0][3], small[2][3]]), 256)]
    offsets, at = {}, 0
    for name, _, rows in sections:
        offsets[name] = (at, rows)
        at += rows
    packed = jnp.concatenate([_pad_rows(a, rows) for _, a, rows in sections] + [jnp.zeros((-at % 32, D), F32)], axis=0)
    loss_rows = jnp.pad(loss_part, ((0, 15), (0, D - LANE)))
    small_zones = [_to_zone(packed[None], 0, me1, BF16, "place_small_grads"), _to_zone(loss_rows[None], 0, me1, F32, "place_loss")]
    small_flight, (mod,) = _gather_start(small_zones, [], [mod], "gather_small_start")

    finish_scatter(2, mod)
    finish_scatter(0, mod)
    sharded_adam("ab_w_out", 1, (0, 2), ab_w_out, m_ab_w_out, v_ab_w_out, [])
    sharded_adam("ab_w_in", 0, (0, 2), ab_w_in, m_ab_w_in, v_ab_w_in, [])

    last = res["ab_w_in"][0]
    small_flight, _ = _gather_mid(small_flight, [last], [], "gather_small_mid")
    small_flight = _gather_late(small_flight, [last], "gather_small_late")
    small_grads, losses = _gather_end(small_flight, [last], "gather_small_end")
    summed = _sum_devices(small_grads, "sum_small_grads")
    loss = _sum_devices(losses, "sum_loss")[0, 0]

    def section(name, nrows, src=summed):
        start = offsets[name][0]
        return src[..., start:start + nrows, :]

    grads = {
        "norm_g": section("norm_g", DEPTH),
        "ada_b": section("d_mod", 3 * DEPTH).reshape(DEPTH, 3 * D),
        "ab_ln_g": section("ab_ln_g", 2), "ab_ln_b": section("ab_ln_b", 2),
        "ab_sgu_b": section("ab_sgu_b", 2).reshape(ab_sgu_b.shape),
        "final_g": section("final_g", 1),
        "ab_sgu_w": section("ab_sgu_w", 256).reshape(ab_sgu_w.shape),
        "ab_conv_w": lax.dynamic_slice_in_dim(section("ab_conv_w", 6).reshape(2, 3, E_A), me * HEAD, HEAD, axis=2),
        "c_pool_scale": lax.dynamic_slice_in_dim(section("c_pool_scale", 4).reshape(2, 4 * G_C), me * 256, 256, axis=1),
    }
    small_w = {"norm_g": (norm_g, m_norm_g, v_norm_g), "ada_b": (ada_b, m_ada_b, v_ada_b),
               "ab_ln_g": (ab_ln_g, m_ab_ln_g, v_ab_ln_g), "ab_ln_b": (ab_ln_b, m_ab_ln_b, v_ab_ln_b),
               "ab_sgu_b": (ab_sgu_b, m_ab_sgu_b, v_ab_sgu_b),
               "final_g": (final_g.reshape(1, D), m_final_g.reshape(1, D), v_final_g.reshape(1, D)),
               "ab_sgu_w": (ab_sgu_w, m_ab_sgu_w, v_ab_sgu_w), "ab_conv_w": (ab_conv_w, m_ab_conv_w, v_ab_conv_w),
               "c_pool_scale": (c_pool_scale, m_c_pool_scale, v_c_pool_scale)}
    for k, (w, m, v) in small_w.items():
        res[k] = [grads[k]] + list(_adam_small(w, grads[k], m, v, "adam_" + k))
    res["final_g"] = [a.reshape(D) for a in res["final_g"]]

    dmod_all = section("d_mod", 3 * DEPTH, small_grads).reshape(NDEV, DEPTH, 3 * D)
    dmod_mine = jnp.transpose(lax.dynamic_slice_in_dim(dmod_all, me * ADA_NC, ADA_NC, axis=2), (1, 0, 2)).astype(F32)
    res["ada_w"] = _ada_weight_adam(jnp.transpose(cact_all.reshape(NDEV, D)), dmod_mine, ada_w, m_ada_w, v_ada_w)

    order = ["norm_g", "ada_w", "ada_b", "ab_w_in", "ab_conv_w", "ab_ln_g", "ab_ln_b", "ab_sgu_w", "ab_sgu_b",
             "ab_w_out", "c_w_in", "c_pool_w", "c_pool_scale", "c_w_out", "final_g"]
    return (loss, grad_x, *[res[k][0] for k in order], *[res[k][1] for k in order],
            *[res[k][2] for k in order], *[res[k][3] for k in order])
```

```python
import jax
import jax.numpy as jnp
from jax import lax
from jax.experimental import pallas as pl
from jax.experimental.pallas import tpu as pltpu

F32, BF16 = jnp.float32, jnp.bfloat16
S, D = 2048, 1024
NDEV = 8
DEPTH = 4
EPS = 1e-6
E_A = 1024
HEAD = 128
CHUNK = 128
POOL_WINDOWS = (2, 4, 8, 16)
G_C = 512
HALO = 16
ADA_NC = 384
MIB = 1024 * 1024
LANE = 128

ADAM_LR, ADAM_B1, ADAM_B2, ADAM_EPS, ADAM_WD, ADAM_STEP = 0.001, 0.9, 0.999, 1e-08, 0.01, 10

ANY = pl.BlockSpec(memory_space=pl.ANY)
VMEM_FULL = pl.BlockSpec(memory_space=pltpu.VMEM)
IN_HBM = pl.BlockSpec(memory_space=pltpu.HBM)
SEMAPHORES = pl.BlockSpec(memory_space=pltpu.SEMAPHORE)
IN_FLIGHT = pltpu.SideEffectType.DATAFLOW_SIDE_EFFECTING


V7X_VMEM_MIB = 64
VMEM_LIMIT_MIB = V7X_VMEM_MIB - 4


def _params(semantics=None):
    return pltpu.CompilerParams(dimension_semantics=semantics, vmem_limit_bytes=VMEM_LIMIT_MIB * MIB)


def _silu(z):
    return z * jax.nn.sigmoid(z)


def _silu_and_grad(z):
    sig = jax.nn.sigmoid(z)
    return z * sig, sig * (1.0 + z * (1.0 - sig))


def _position():
    return lax.axis_index("x"), lax.axis_index("y"), lax.axis_index("c")


def _index(pos):
    return 4 * pos[0] + 2 * pos[1] + pos[2]


def _peer(pos, k):
    flipped = tuple(1 - p if (k >> (2 - b)) & 1 else p for b, p in enumerate(pos))
    return flipped, _index(flipped)


def _remote(src, dst, send_sem, recv_sem, device):
    return pltpu.make_async_remote_copy(src_ref=src, dst_ref=dst, send_sem=send_sem, recv_sem=recv_sem,
                                        device_id=device, device_id_type=pl.DeviceIdType.MESH)


def _gather(arrays, name):
    n = len(arrays)
    out_shape = [jax.ShapeDtypeStruct((NDEV,) + a.shape, a.dtype) for a in arrays]

    def body(*refs):
        ins, outs = refs[:n], refs[n:2 * n]
        send_sems, recv_sems, own_sems = refs[2 * n:]
        x, y, c = _position()
        me = _index((x, y, c))
        sibling = (x, y, 1 - c)
        chips = [(1 - x, y), (x, 1 - y), (1 - x, 1 - y)]

        def block_copy(j, k, owner, to, src=None):
            rows = outs[j].at[_index(owner)]
            return _remote(rows if src is None else src, rows, send_sems.at[j, k], recv_sems.at[j, k], to)

        own, first, passed = [], [], []
        for j in range(n):
            own.append(pltpu.make_async_copy(ins[j], outs[j].at[me], own_sems.at[j]))
            first.append(block_copy(j, 0, (x, y, c), sibling, src=ins[j]))
            first += [block_copy(j, 1 + q, (x, y, c), (*chip, c), src=ins[j]) for q, chip in enumerate(chips)]
        for copy in own + first:
            copy.start()
        for q, chip in enumerate(chips):
            for j in range(n):
                block_copy(j, 1 + q, (*chip, c), (x, y, c)).wait_recv()
                forward = block_copy(j, 4 + q, (*chip, c), sibling)
                forward.start()
                passed.append(forward)
        for j in range(n):
            block_copy(j, 0, sibling, (x, y, c)).wait_recv()
            for q, chip in enumerate(chips):
                block_copy(j, 4 + q, (*chip, 1 - c), (x, y, c)).wait_recv()
        for copy in first + passed:
            copy.wait_send()
        for copy in own:
            copy.wait()

    return pl.pallas_call(
        body, name=name, out_shape=out_shape, in_specs=[ANY] * n, out_specs=[ANY] * n,
        scratch_shapes=[pltpu.SemaphoreType.DMA((n, NDEV - 1)), pltpu.SemaphoreType.DMA((n, NDEV - 1)),
                        pltpu.SemaphoreType.DMA((n,))],
    )(*arrays)


def _pair_start(parts, name):
    n = len(parts)
    lands = [_in_hbm(lax.empty(p.shape[1:], p.dtype)) for p in parts]

    def body(*refs):
        ins, zones = refs[:n], refs[n:2 * n]
        send_sems, recv_sems = refs[2 * n:2 * n + 2]
        x, y, c = _position()
        for j in range(n):
            _remote(ins[j].at[1 - c], zones[j], send_sems.at[j], recv_sems.at[j], (x, y, 1 - c)).start()

    outs = pl.pallas_call(
        body, name=name,
        out_shape=(pltpu.SemaphoreType.DMA((n,)), pltpu.SemaphoreType.DMA((n,)),
                   *[pltpu.HBM(p.shape, p.dtype) for p in parts], *[pltpu.HBM(z.shape, z.dtype) for z in lands]),
        in_specs=[IN_HBM] * (2 * n), out_specs=(SEMAPHORES, SEMAPHORES, *[IN_HBM] * (2 * n)),
        input_output_aliases={j: 2 + j for j in range(2 * n)},
        compiler_params=pltpu.CompilerParams(has_side_effects=IN_FLIGHT),
    )(*[_in_hbm(p) for p in parts], *lands)
    return outs[0], outs[1], list(outs[2:2 + n]), list(outs[2 + n:])


def _pair_end(parts, zones, send_sems, recv_sems, after, name):
    n = len(parts)

    def body(*refs):
        ins, zs = refs[:n], refs[n:2 * n]
        s, r = refs[2 * n:2 * n + 2]
        me = _position()
        for j in range(n):
            copy = _remote(ins[j].at[0], zs[j], s.at[j], r.at[j], me)
            copy.wait_send()
            copy.wait_recv()

    outs = pl.pallas_call(
        body, name=name,
        out_shape=(*[pltpu.HBM(p.shape, p.dtype) for p in parts], *[pltpu.HBM(z.shape, z.dtype) for z in zones]),
        in_specs=[IN_HBM] * (2 * n) + [SEMAPHORES, SEMAPHORES, ANY], out_specs=tuple([IN_HBM] * (2 * n)),
        input_output_aliases={j: j for j in range(2 * n)},
        compiler_params=pltpu.CompilerParams(has_side_effects=IN_FLIGHT),
    )(*parts, *zones, send_sems, recv_sems, after)
    return list(outs[:n]), list(outs[n:])


def _pair_sum(part, from_sibling, core, name):
    ncol = part.shape[-1]
    p3 = part.reshape(2, -1, ncol)
    q2 = from_sibling.reshape(-1, ncol)
    nrows = q2.shape[0]
    tr = 512

    def body(core_ref, p_ref, q_ref, o_ref):
        o_ref[...] = (p_ref[...].astype(F32) + q_ref[...].astype(F32)).astype(BF16)

    grid_spec = pltpu.PrefetchScalarGridSpec(
        num_scalar_prefetch=1, grid=(nrows // tr,),
        in_specs=[pl.BlockSpec((None, tr, ncol), lambda i, core_ref: (core_ref[0], i, 0)),
                  pl.BlockSpec((tr, ncol), lambda i, core_ref: (i, 0))],
        out_specs=pl.BlockSpec((tr, ncol), lambda i, core_ref: (i, 0)))
    out = pl.pallas_call(body, name=name, grid_spec=grid_spec, out_shape=jax.ShapeDtypeStruct(q2.shape, BF16),
                         compiler_params=_params(("arbitrary",)))(core, p3, q2)
    return out.reshape(from_sibling.shape)


def _in_hbm(a):
    return pltpu.with_memory_space_constraint(a, pltpu.HBM)


def _chips(x, y):
    return [(1 - x, y), (x, 1 - y), (1 - x, 1 - y)]


def _to_zone(a, wl, me, dtype, name):
    _, rows, cols = a.shape
    tr = 256 if rows % 256 == 0 else rows

    def body(me_ref, a_ref, o_ref):
        o_ref[...] = a_ref[...].astype(dtype)

    grid_spec = pltpu.PrefetchScalarGridSpec(
        num_scalar_prefetch=1, grid=(rows // tr,),
        in_specs=[pl.BlockSpec((None, tr, cols), lambda i, me_ref: (wl, i, 0))],
        out_specs=pl.BlockSpec((None, tr, cols), lambda i, me_ref: (me_ref[0], i, 0)))
    return pl.pallas_call(body, name=name, grid_spec=grid_spec, out_shape=jax.ShapeDtypeStruct((NDEV, rows, cols), dtype),
                          compiler_params=_params(("arbitrary",)))(me, a)


def _halves(block):
    rows = block.shape[0] // 2
    return block.at[pl.ds(0, rows)], block.at[pl.ds(rows, rows)]


def _around(x, y, c):
    return (x, y, 1 - c), (1 - x, y, c), (x, 1 - y, c), (1 - x, 1 - y, c)


def _gather_step1(zs, send, recv, pos):
    sibling, xn, yn, _ = _around(*pos)
    for j, z in enumerate(zs):
        mine = z.at[_index(pos)]
        for k, peer in enumerate((sibling, xn, yn)):
            _remote(mine, mine, send.at[3 * j + k], recv.at[3 * j + k], peer).start()


def _gather_step2(zs, recv1, send, recv, pos):
    sibling, xn, yn, _ = _around(*pos)
    for j, z in enumerate(zs):
        xb, yb = z.at[_index(xn)], z.at[_index(yn)]
        _remote(xb, xb, send.at[4 * j], recv1.at[3 * j + 1], pos).wait_recv()
        _remote(yb, yb, send.at[4 * j], recv1.at[3 * j + 2], pos).wait_recv()
        _remote(xb, xb, send.at[4 * j], recv.at[4 * j], sibling).start()
        _remote(yb, yb, send.at[4 * j + 1], recv.at[4 * j + 1], sibling).start()
        first, second = _halves(xb)[0], _halves(yb)[1]
        _remote(first, first, send.at[4 * j + 2], recv.at[4 * j + 2], yn).start()
        _remote(second, second, send.at[4 * j + 3], recv.at[4 * j + 3], xn).start()


def _gather_step3(zs, recv2, send, recv, pos):
    sibling, _, _, diagonal = _around(*pos)
    for j, z in enumerate(zs):
        db = z.at[_index(diagonal)]
        first, second = _halves(db)
        _remote(first, first, send.at[j], recv2.at[4 * j + 2], pos).wait_recv()
        _remote(second, second, send.at[j], recv2.at[4 * j + 3], pos).wait_recv()
        _remote(db, db, send.at[j], recv.at[j], sibling).start()


def _gather_step4(zs, send1, recv1, send2, recv2, send3, recv3, pos):
    x, y, c = pos
    sibling = (x, y, 1 - c)
    _, sx, sy, sd = _around(*sibling)
    for j, z in enumerate(zs):
        for owner, send, recv, k in ((sibling, send1, recv1, 3 * j), (sx, send2, recv2, 4 * j), (sy, send2, recv2, 4 * j + 1),
                                     (sd, send3, recv3, j)):
            block = z.at[_index(owner)]
            _remote(block, block, send.at[k], recv.at[k], pos).wait_recv()
    for j, z in enumerate(zs):
        block = z.at[0]
        half = _halves(block)[0]
        for ref, send, recv, k in ([(block, send1, recv1, 3 * j + k) for k in range(3)]
                                   + [(block, send2, recv2, 4 * j), (block, send2, recv2, 4 * j + 1),
                                      (half, send2, recv2, 4 * j + 2), (half, send2, recv2, 4 * j + 3), (block, send3, recv3, j)]):
            _remote(ref, ref, send.at[k], recv.at[k], pos).wait_send()


def _flight_call(step, name, zones, sems_in, nsems_out, after, carried):
    n, m, k = len(zones), len(carried), len(sems_in)

    def body(*refs):
        zs = refs[:n]
        given = refs[n + m:n + m + k]
        made = refs[n + m + k + len(after):n + m + k + len(after) + (2 if nsems_out else 0)]
        step(zs, *given, *made, _position())

    sem_out = (pltpu.SemaphoreType.DMA((nsems_out,)),) * 2 if nsems_out else ()
    outs = pl.pallas_call(
        body, name=name,
        out_shape=(*sem_out, *[pltpu.HBM(z.shape, z.dtype) for z in zones], *[jax.ShapeDtypeStruct(a.shape, a.dtype) for a in carried]),
        in_specs=[IN_HBM] * n + [ANY] * m + [SEMAPHORES] * k + [ANY] * len(after),
        out_specs=(*[SEMAPHORES] * len(sem_out), *[IN_HBM] * n, *[ANY] * m),
        input_output_aliases={j: len(sem_out) + j for j in range(n + m)},
        compiler_params=pltpu.CompilerParams(has_side_effects=IN_FLIGHT),
    )(*[_in_hbm(z) for z in zones], *carried, *sems_in, *after)
    sems = list(outs[:len(sem_out)])
    return sems, list(outs[len(sem_out):len(sem_out) + n]), list(outs[len(sem_out) + n:])


def _gather_start(zones, after, carried, name):
    (send1, recv1), zones, carried = _flight_call(_gather_step1, name, zones, [], 3 * len(zones), after, carried)
    return {"s1": send1, "r1": recv1, "zones": zones}, carried


def _gather_mid(flight, after, carried, name):
    step = lambda zs, recv1, send, recv, pos: _gather_step2(zs, recv1, send, recv, pos)
    (send2, recv2), zones, carried = _flight_call(step, name, flight["zones"], [flight["r1"]], 4 * len(flight["zones"]), after, carried)
    return {**flight, "s2": send2, "r2": recv2, "zones": zones}, carried


def _gather_late(flight, after, name):
    step = lambda zs, recv2, send, recv, pos: _gather_step3(zs, recv2, send, recv, pos)
    (send3, recv3), zones, _ = _flight_call(step, name, flight["zones"], [flight["r2"]], len(flight["zones"]), after, [])
    return {**flight, "s3": send3, "r3": recv3, "zones": zones}


def _gather_end(flight, after, name):
    sems = [flight[k] for k in ("s1", "r1", "s2", "r2", "s3", "r3")]
    _, zones, _ = _flight_call(_gather_step4, name, flight["zones"], sems, 0, after, [])
    return zones


def _scatter_start(parts, name):
    n = len(parts)
    lands = [_in_hbm(lax.empty((3,) + p.shape[1:], p.dtype)) for p in parts]

    def body(*refs):
        ins, zones = refs[:n], refs[n:2 * n]
        send_sems, recv_sems = refs[2 * n:2 * n + 2]
        x, y, c = _position()
        for j in range(n):
            for q, (px, py) in enumerate(_chips(x, y)):
                _remote(ins[j].at[2 * px + py], zones[j].at[q], send_sems.at[3 * j + q], recv_sems.at[3 * j + q],
                        (px, py, c)).start()

    outs = pl.pallas_call(
        body, name=name,
        out_shape=(pltpu.SemaphoreType.DMA((3 * n,)), pltpu.SemaphoreType.DMA((3 * n,)),
                   *[pltpu.HBM(p.shape, p.dtype) for p in parts], *[pltpu.HBM(z.shape, z.dtype) for z in lands]),
        in_specs=[IN_HBM] * (2 * n), out_specs=(SEMAPHORES, SEMAPHORES, *[IN_HBM] * (2 * n)),
        input_output_aliases={j: 2 + j for j in range(2 * n)},
        compiler_params=pltpu.CompilerParams(has_side_effects=IN_FLIGHT),
    )(*[_in_hbm(p) for p in parts], *lands)
    return outs[0], outs[1], list(outs[2:2 + n]), list(outs[2 + n:])


def _scatter_end(parts, zones, send_sems, recv_sems, after, name):
    n = len(parts)

    def body(*refs):
        ins, zs = refs[:n], refs[n:2 * n]
        s, r = refs[2 * n:2 * n + 2]
        me = _position()
        for j in range(n):
            for q in range(3):
                copy = _remote(ins[j].at[0], zs[j].at[q], s.at[3 * j + q], r.at[3 * j + q], me)
                copy.wait_send()
                copy.wait_recv()

    outs = pl.pallas_call(
        body, name=name,
        out_shape=(*[pltpu.HBM(p.shape, p.dtype) for p in parts], *[pltpu.HBM(z.shape, z.dtype) for z in zones]),
        in_specs=[IN_HBM] * (2 * n) + [SEMAPHORES, SEMAPHORES, ANY], out_specs=tuple([IN_HBM] * (2 * n)),
        input_output_aliases={j: j for j in range(2 * n)},
        compiler_params=pltpu.CompilerParams(has_side_effects=IN_FLIGHT),
    )(*parts, *zones, send_sems, recv_sems, after)
    return list(outs[:n]), list(outs[n:])


def _ada_forward(c, ada_w, ada_b):
    def body(c_ref, w_ref, b_ref, cact_ref, mod_ref, gbuf, modrow, send_sems, recv_sems):
        pos = _position()
        me = _index(pos)

        def to_all(ref, round_):
            copies = []
            for k in range(1, NDEV):
                peer, _ = _peer(pos, k)
                copy = pltpu.make_async_remote_copy(
                    src_ref=ref.at[me], dst_ref=ref.at[me], send_sem=send_sems.at[round_, k - 1],
                    recv_sem=recv_sems.at[round_, k - 1], device_id=peer, device_id_type=pl.DeviceIdType.MESH)
                copy.start()
                copies.append(copy)
            for copy in copies:
                copy.wait()

        cact_ref[me] = _silu(c_ref[...])
        to_all(cact_ref, 0)
        rows = lax.broadcasted_iota(jnp.int32, (NDEV, D), 0)
        cact = jnp.zeros((NDEV, D), F32)
        for e in range(NDEV):
            cact = jnp.where(rows == e, cact_ref[e], cact)
        cact = cact.astype(BF16)
        for l in range(DEPTH):
            gbuf[me, l] = jnp.dot(cact, w_ref[l].astype(BF16), preferred_element_type=F32)
        to_all(gbuf, 1)
        mine = lax.broadcasted_iota(jnp.int32, (NDEV, ADA_NC), 0) == me
        for l in range(DEPTH):
            for d in range(NDEV):
                modrow[:, d * ADA_NC:(d + 1) * ADA_NC] = jnp.sum(jnp.where(mine, gbuf[d, l], 0.0), axis=0, keepdims=True)
            full = modrow[...] + b_ref[l:l + 1, :]
            for w in range(3):
                mod_ref[l, w] = full[:, w * D:(w + 1) * D]

    return pl.pallas_call(
        body, name="ada_forward",
        out_shape=[jax.ShapeDtypeStruct((NDEV, 1, D), F32), jax.ShapeDtypeStruct((DEPTH, 3, 1, D), F32)],
        in_specs=[VMEM_FULL] * 3, out_specs=[VMEM_FULL] * 2,
        scratch_shapes=[pltpu.VMEM((NDEV, DEPTH, NDEV, ADA_NC), F32), pltpu.VMEM((1, 3 * D), F32),
                        pltpu.SemaphoreType.DMA((2, NDEV - 1)), pltpu.SemaphoreType.DMA((2, NDEV - 1))],
        compiler_params=_params(),
    )(c, ada_w, ada_b)


def _mod_spec(layer, which, ngrid):
    index = {1: lambda i: (layer, which, 0, 0), 2: lambda i, j: (layer, which, 0, 0)}[ngrid]
    return pl.BlockSpec((None, None, 1, D), index)


W_BLOCKS = 4


def _norm_proj(x, mod, norm_g3, wg, layer, name):
    nb = wg.shape[-1]
    tm = 1024
    wb = W_BLOCKS

    def body(x_ref, g_ref, shift_ref, scale_ref, w_ref, ht_ref, p_ref, h_ref):
        @pl.when(pl.program_id(1) == 0)
        def _():
            xv = x_ref[...]
            r = lax.rsqrt(jnp.mean(xv * xv, axis=-1, keepdims=True) + EPS)
            hn = xv * r * g_ref[...]
            h = hn * (1.0 + scale_ref[...]) + shift_ref[...]
            h_ref[...] = h.astype(BF16)
            ht_ref[...] = h.T.astype(BF16)

        hv = h_ref[...]
        for b in range(wb):
            p_ref[:, b * nb:(b + 1) * nb] = jnp.dot(hv, w_ref[b], preferred_element_type=F32).astype(BF16)

    return pl.pallas_call(
        body, name=name, grid=(S // tm, NDEV // wb),
        out_shape=[jax.ShapeDtypeStruct((D, S), BF16), jax.ShapeDtypeStruct((S, NDEV * nb), BF16)],
        in_specs=[pl.BlockSpec((tm, D), lambda i, d: (i, 0)),
                  pl.BlockSpec((None, 1, D), lambda i, d: (layer, 0, 0)),
                  _mod_spec(layer, 0, 2), _mod_spec(layer, 1, 2),
                  pl.BlockSpec((wb, D, nb), lambda i, d: (d, 0, 0))],
        out_specs=[pl.BlockSpec((D, tm), lambda i, d: (0, i)), pl.BlockSpec((tm, wb * nb), lambda i, d: (i, d))],
        scratch_shapes=[pltpu.VMEM((tm, D), BF16)],
        compiler_params=_params(("arbitrary", "arbitrary")),
    )(x, norm_g3, mod, mod, wg)


def _out_proj(ycat, w_out, x, mod, layer, name):
    tm = 512
    e = w_out.shape[0]

    def body(y_ref, w_ref, x_ref, gate_ref, xn_ref, o_ref):
        acc = jnp.dot(y_ref[...], w_ref[...], preferred_element_type=F32)
        o_ref[...] = acc.astype(BF16)
        xn_ref[...] = x_ref[...] + gate_ref[...] * acc

    return pl.pallas_call(
        body, name=name, grid=(S // tm,),
        out_shape=[jax.ShapeDtypeStruct((S, D), F32), jax.ShapeDtypeStruct((S, D), BF16)],
        in_specs=[pl.BlockSpec((tm, e), lambda i: (i, 0)), pl.BlockSpec((e, D), lambda i: (0, 0)),
                  pl.BlockSpec((tm, D), lambda i: (i, 0)), _mod_spec(layer, 2, 1)],
        out_specs=[pl.BlockSpec((tm, D), lambda i: (i, 0))] * 2,
        compiler_params=_params(("arbitrary",)),
    )(ycat, w_out, x, mod)


def _final_loss(x, target, final_g2):
    tm = 256

    def body(x_ref, t_ref, g_ref, dx_ref, loss_ref, dg_ref):
        @pl.when(pl.program_id(0) == 0)
        def _():
            loss_ref[...] = jnp.zeros_like(loss_ref)
            dg_ref[...] = jnp.zeros_like(dg_ref)

        xv, g = x_ref[...], g_ref[...]
        r = lax.rsqrt(jnp.mean(xv * xv, axis=-1, keepdims=True) + EPS)
        xn = xv * r
        err = xn * g - t_ref[...]
        loss_ref[...] += 0.5 * jnp.sum(jnp.mean(err * err, axis=-1, keepdims=True), axis=0, keepdims=True)
        dy = err * (1.0 / D)
        dg_ref[...] += jnp.sum(dy * xn, axis=0, keepdims=True)
        u = dy * g
        dx_ref[...] = r * (u - xn * jnp.mean(xn * u, axis=-1, keepdims=True))

    tile = pl.BlockSpec((tm, D), lambda i: (i, 0))
    row = pl.BlockSpec((1, D), lambda i: (0, 0))
    return pl.pallas_call(
        body, name="final_loss", grid=(S // tm,),
        out_shape=[jax.ShapeDtypeStruct((S, D), F32), jax.ShapeDtypeStruct((1, LANE), F32), jax.ShapeDtypeStruct((1, D), F32)],
        in_specs=[tile, tile, row], out_specs=[tile, pl.BlockSpec((1, LANE), lambda i: (0, 0)), row],
        compiler_params=_params(("arbitrary",)),
    )(x, target, final_g2)


def _out_bwd(dx, out, ycat, w_out, mod, layer, carried, name):
    tm = 512
    nsteps = S // tm
    e = ycat.shape[1]
    rb = e // NDEV
    nc = len(carried)

    def body(dx_ref, o_ref, y_ref, w_ref, gate_ref, *rest):
        dy_ref, gw_ref, dgate_ref = rest[nc:nc + 3]
        acc = rest[-1]
        step = pl.program_id(0)

        @pl.when(step == 0)
        def _():
            dgate_ref[...] = jnp.zeros_like(dgate_ref)
            acc[...] = jnp.zeros_like(acc)

        dxv = dx_ref[...]
        d_out = (gate_ref[...] * dxv).astype(BF16)
        dgate_ref[...] += jnp.sum(dxv * o_ref[...].astype(F32), axis=0, keepdims=True)
        dy_ref[...] = lax.dot_general(d_out, w_ref[...], (((1,), (1,)), ((), ())), preferred_element_type=F32).astype(BF16)
        acc[...] += lax.dot_general(y_ref[...], d_out, (((0,), (0,)), ((), ())), preferred_element_type=F32)

        @pl.when(step == nsteps - 1)
        def _():
            for d in range(NDEV):
                gw_ref[d % 2, d // 2] = acc[d * rb:(d + 1) * rb, :].astype(BF16)

    tile = pl.BlockSpec((tm, D), lambda i: (i, 0))
    wide = pl.BlockSpec((tm, e), lambda i: (i, 0))
    outs = pl.pallas_call(
        body, name=name, grid=(nsteps,),
        out_shape=[jax.ShapeDtypeStruct((S, e), BF16), jax.ShapeDtypeStruct((2, NDEV // 2, rb, D), BF16),
                   jax.ShapeDtypeStruct((1, D), F32)] + [jax.ShapeDtypeStruct(a.shape, a.dtype) for a in carried],
        in_specs=[tile, tile, wide, pl.BlockSpec((e, D), lambda i: (0, 0)), _mod_spec(layer, 2, 1)] + [ANY] * nc,
        out_specs=[wide, pl.BlockSpec((2, NDEV // 2, rb, D), lambda i: (0, 0, 0, 0)), pl.BlockSpec((1, D), lambda i: (0, 0))]
        + [ANY] * nc,
        scratch_shapes=[pltpu.VMEM((e, D), F32)],
        input_output_aliases={5 + k: 3 + k for k in range(nc)},
        compiler_params=_params(("arbitrary",)),
    )(dx, out, ycat, w_out, mod, *carried)
    return outs[0], outs[1], outs[2], list(outs[3:])


def _weight_grad(h_t, d_proj, name):
    nb = d_proj.shape[1] // NDEV

    def body(ht_ref, dp_ref, o_ref):
        o_ref[...] = jnp.dot(ht_ref[...], dp_ref[...], preferred_element_type=F32).astype(BF16)

    return pl.pallas_call(
        body, name=name, grid=(NDEV,), out_shape=jax.ShapeDtypeStruct((2, NDEV // 2, D, nb), BF16),
        in_specs=[pl.BlockSpec((D, S), lambda d: (0, 0)), pl.BlockSpec((S, nb), lambda d: (0, d))],
        out_specs=pl.BlockSpec((None, None, D, nb), lambda d: (d % 2, d // 2, 0, 0)),
        compiler_params=_params(("arbitrary",)),
    )(h_t, d_proj)


def _dh_norm_bwd(d_proj, wg, x, dx, mod, norm_g3, layer, carried, name):
    nb = wg.shape[-1]
    tm = 512
    wb = W_BLOCKS
    rc = 128

    def body(dp_ref, w_ref, x_ref, dx_ref, g_ref, scale_ref, carried_ref,
             dxi_ref, dshift_ref, dscale_ref, dg_ref, carried_out, acc):
        i, d = pl.program_id(0), pl.program_id(1)
        nt = (((1,), (1,)), ((), ()))
        part = lax.dot_general(dp_ref[:, :nb], w_ref[0], nt, preferred_element_type=F32)
        for b in range(1, wb):
            part += lax.dot_general(dp_ref[:, b * nb:(b + 1) * nb], w_ref[b], nt, preferred_element_type=F32)

        @pl.when(d == 0)
        def _():
            acc[...] = part

        @pl.when(d != 0)
        def _():
            acc[...] += part

        @pl.when(jnp.logical_and(i == 0, d == 0))
        def _():
            dshift_ref[...] = jnp.zeros_like(dshift_ref)
            dscale_ref[...] = jnp.zeros_like(dscale_ref)
            dg_ref[...] = jnp.zeros_like(dg_ref)

        @pl.when(d == NDEV // wb - 1)
        def _():
            g = g_ref[...]
            scale1 = 1.0 + scale_ref[...]

            def chunk(k, sums):
                rows = pl.ds(pl.multiple_of(k * rc, rc), rc)
                xv, dhv = x_ref[rows, :], acc[rows, :]
                r = lax.rsqrt(jnp.mean(xv * xv, axis=-1, keepdims=True) + EPS)
                xn = xv * r
                dhn = dhv * scale1
                u = dhn * g
                dxi_ref[rows, :] = dx_ref[rows, :] + r * (u - xn * jnp.mean(xn * u, axis=-1, keepdims=True))
                return (sums[0] + jnp.sum(dhv, axis=0, keepdims=True),
                        sums[1] + jnp.sum(dhv * (xn * g), axis=0, keepdims=True),
                        sums[2] + jnp.sum(dhn * xn, axis=0, keepdims=True))

            zero = jnp.zeros((1, D), F32)
            sums = lax.fori_loop(0, tm // rc, chunk, (zero, zero, zero))
            dshift_ref[...] += sums[0]
            dscale_ref[...] += sums[1]
            dg_ref[...] += sums[2]

    tile = pl.BlockSpec((tm, D), lambda i, d: (i, 0))
    row = pl.BlockSpec((1, D), lambda i, d: (0, 0))
    return pl.pallas_call(
        body, name=name, grid=(S // tm, NDEV // wb),
        out_shape=[jax.ShapeDtypeStruct((S, D), F32)] + [jax.ShapeDtypeStruct((1, D), F32)] * 3
        + [jax.ShapeDtypeStruct(carried.shape, carried.dtype)],
        in_specs=[pl.BlockSpec((tm, wb * nb), lambda i, d: (i, d)), pl.BlockSpec((wb, D, nb), lambda i, d: (d, 0, 0)),
                  tile, tile, pl.BlockSpec((None, 1, D), lambda i, d: (layer, 0, 0)), _mod_spec(layer, 1, 2), ANY],
        out_specs=[tile, row, row, row, ANY], scratch_shapes=[pltpu.VMEM((tm, D), F32)],
        input_output_aliases={6: 4}, compiler_params=_params(("arbitrary", "arbitrary")),
    )(d_proj, wg, x, dx, norm_g3, mod, carried)


TS = 256
NCH = TS // CHUNK
HALO_BLOCKS = TS // HALO


def _halo_before(width, col_block):
    return pl.BlockSpec((HALO, width), lambda i: (jnp.maximum(i * HALO_BLOCKS - 1, 0), col_block))


def _halo_after(width, col_block):
    return pl.BlockSpec((HALO, width), lambda i: (jnp.minimum((i + 1) * HALO_BLOCKS, S // HALO - 1), col_block))


def _shift_down(ext, k):
    return pltpu.roll(ext, k, 0)[HALO:]


def _shift_up(ext, k):
    return pltpu.roll(ext, ext.shape[0] - k, 0)[:ext.shape[0] - HALO]


def _layer_norm_head(v, lg, lb):
    mu = jnp.mean(v, axis=-1, keepdims=True)
    vc = v - mu
    rstd = lax.rsqrt(jnp.mean(vc * vc, axis=-1, keepdims=True) + EPS)
    vhat = vc * rstd
    return vhat, rstd, vhat * lg + lb


def _causal_mask():
    return lax.broadcasted_iota(jnp.int32, (CHUNK, CHUNK), 0) >= lax.broadcasted_iota(jnp.int32, (CHUNK, CHUNK), 1)


def _even_mix_fwd(proj, convw, ln_g3, ln_b3, sgu_w, sgu_bcol, wl, after, name):
    def body(pj_ref, hh_ref, hc_ref, cw_ref, lg_ref, lb_ref, sw_ref, sb_ref, *rest):
        y_ref = rest[-1]
        live = (pl.program_id(0) > 0).astype(F32)
        causal = _causal_mask()
        for j in range(E_A // HEAD):
            cols = slice(j * HEAD, (j + 1) * HEAD)
            w0, w1, w2 = cw_ref[0:1, cols], cw_ref[1:2, cols], cw_ref[2:3, cols]
            lg, lb = lg_ref[:, cols], lb_ref[:, cols]
            wm = jnp.where(causal, sw_ref[j], 0.0).astype(BF16)
            bias = sb_ref[j]

            def split(s, rows, cols=cols):
                return pj_ref[rows, s * E_A + cols.start:s * E_A + cols.stop].astype(F32)

            prev_tail = hc_ref[:, cols].astype(F32) * hh_ref[:, cols].astype(F32) * live
            for n in range(NCH):
                rows = slice(n * CHUNK, (n + 1) * CHUNK)
                p = split(2, rows) * split(0, rows)
                ext = jnp.concatenate([prev_tail, p], axis=0)
                prev_tail = p[CHUNK - HALO:]
                cv = w2 * p + w1 * _shift_down(ext, 1) + w0 * _shift_down(ext, 2)
                y_ref[rows, cols] = (split(1, rows) * cv * _silu(split(3, rows))).astype(BF16)
                _, _, vn = _layer_norm_head(split(5, rows), lg, lb)
                mixed = jnp.dot(wm, vn.astype(BF16), preferred_element_type=F32) + bias
                y_ref[rows, E_A + cols.start:E_A + cols.stop] = (split(4, rows) * mixed * _silu(split(6, rows))).astype(BF16)

    const3 = lambda i: (wl, 0, 0)
    const4 = lambda i: (wl, 0, 0, 0)
    return pl.pallas_call(
        body, name=name, grid=(S // TS,), out_shape=jax.ShapeDtypeStruct((S, 2 * E_A), BF16),
        in_specs=[pl.BlockSpec((TS, 7 * E_A), lambda i: (i, 0)), _halo_before(E_A, 0), _halo_before(E_A, 2),
                  pl.BlockSpec((None, 3, E_A), const3), pl.BlockSpec((None, 1, E_A), const3),
                  pl.BlockSpec((None, 1, E_A), const3), pl.BlockSpec((None, NDEV, CHUNK, CHUNK), const4),
                  pl.BlockSpec((None, NDEV, CHUNK, 1), const4)] + [ANY] * len(after),
        out_specs=pl.BlockSpec((TS, 2 * E_A), lambda i: (i, 0)),
        compiler_params=_params(("arbitrary",)),
    )(proj, proj, proj, convw, ln_g3, ln_b3, sgu_w, sgu_bcol, *after)


def _even_mix_bwd(proj, d_ycat, convw, ln_g3, ln_b3, sgu_w, sgu_bcol, wl, name):
    nsteps = S // TS

    def body(pj_ref, hh_ref, hc_ref, hb_ref, hz_ref, dy_ref, hdy_ref, cw_ref, lg_ref, lb_ref, sw_ref, sb_ref,
             dp_ref, dcw_ref, dlg_ref, dlb_ref, dsw_ref, dsb_ref):
        step = pl.program_id(0)

        @pl.when(step == 0)
        def _():
            for ref in (dcw_ref, dlg_ref, dlb_ref, dsw_ref, dsb_ref):
                ref[...] = jnp.zeros_like(ref)

        live_before = (step > 0).astype(F32)
        live_after = (step < nsteps - 1).astype(F32)
        causal = _causal_mask()
        for j in range(E_A // HEAD):
            cols = slice(j * HEAD, (j + 1) * HEAD)
            w0, w1, w2 = cw_ref[0:1, cols], cw_ref[1:2, cols], cw_ref[2:3, cols]
            lg, lb = lg_ref[:, cols], lb_ref[:, cols]
            wmf = jnp.where(causal, sw_ref[j], 0.0)
            wm, wmt = wmf.astype(BF16), wmf.T.astype(BF16)
            bias = sb_ref[j]

            def split(s, rows, cols=cols):
                return pj_ref[rows, s * E_A + cols.start:s * E_A + cols.stop].astype(F32)

            def put(s, rows, val, cols=cols):
                dp_ref[rows, s * E_A + cols.start:s * E_A + cols.stop] = val.astype(BF16)

            ps = [split(2, slice(n * CHUNK, (n + 1) * CHUNK)) * split(0, slice(n * CHUNK, (n + 1) * CHUNK)) for n in range(NCH)]
            next_head = (hdy_ref[:, cols].astype(F32) * hb_ref[:, cols].astype(F32) * _silu(hz_ref[:, cols].astype(F32))
                         * live_after)
            acc_w = [jnp.zeros((1, HEAD), F32) for _ in range(3)]
            for n in reversed(range(NCH)):
                rows = slice(n * CHUNK, (n + 1) * CHUNK)
                p = ps[n]
                tail = ps[n - 1][CHUNK - HALO:] if n > 0 else hc_ref[:, cols].astype(F32) * hh_ref[:, cols].astype(F32) * live_before
                ext = jnp.concatenate([tail, p], axis=0)
                p1, p2 = _shift_down(ext, 1), _shift_down(ext, 2)
                cv = w2 * p + w1 * p1 + w0 * p2
                a_b, a_z = split(1, rows), split(3, rows)
                sz, dsz = _silu_and_grad(a_z)
                dya = dy_ref[rows, cols].astype(F32)
                put(1, rows, dya * cv * sz)
                put(3, rows, dya * a_b * cv * dsz)
                gcv = dya * a_b * sz
                acc_w[0] += jnp.sum(gcv * p2, axis=0, keepdims=True)
                acc_w[1] += jnp.sum(gcv * p1, axis=0, keepdims=True)
                acc_w[2] += jnp.sum(gcv * p, axis=0, keepdims=True)
                gext = jnp.concatenate([gcv, next_head], axis=0)
                next_head = gcv[:HALO]
                dpv = w2 * gcv + w1 * _shift_up(gext, 1) + w0 * _shift_up(gext, 2)
                put(2, rows, dpv * split(0, rows))
                put(0, rows, dpv * split(2, rows))
            for k in range(3):
                dcw_ref[k:k + 1, cols] += acc_w[k]

            acc_lg, acc_lb = jnp.zeros((1, HEAD), F32), jnp.zeros((1, HEAD), F32)
            acc_sw, acc_sb = jnp.zeros((CHUNK, CHUNK), F32), jnp.zeros((CHUNK, 1), F32)
            for n in range(NCH):
                rows = slice(n * CHUNK, (n + 1) * CHUNK)
                u, z = split(4, rows), split(6, rows)
                vhat, rstd, vn = _layer_norm_head(split(5, rows), lg, lb)
                vn16 = vn.astype(BF16)
                mixed = jnp.dot(wm, vn16, preferred_element_type=F32) + bias
                sz, dsz = _silu_and_grad(z)
                dyb = dy_ref[rows, E_A + cols.start:E_A + cols.stop].astype(F32)
                put(4, rows, dyb * mixed * sz)
                put(6, rows, dyb * u * mixed * dsz)
                dmix = dyb * u * sz
                dmix16 = dmix.astype(BF16)
                acc_sb += jnp.sum(dmix, axis=1, keepdims=True)
                acc_sw += lax.dot_general(dmix16, vn16, (((1,), (1,)), ((), ())), preferred_element_type=F32)
                dvn = jnp.dot(wmt, dmix16, preferred_element_type=F32)
                acc_lg += jnp.sum(dvn * vhat, axis=0, keepdims=True)
                acc_lb += jnp.sum(dvn, axis=0, keepdims=True)
                dvh = dvn * lg
                put(5, rows, rstd * (dvh - jnp.mean(dvh, axis=-1, keepdims=True)
                                     - vhat * jnp.mean(dvh * vhat, axis=-1, keepdims=True)))
            dlg_ref[:, cols] += acc_lg
            dlb_ref[:, cols] += acc_lb
            dsw_ref[j] += jnp.where(causal, acc_sw, 0.0)
            dsb_ref[j] += acc_sb

    const3 = lambda i: (wl, 0, 0)
    const4 = lambda i: (wl, 0, 0, 0)
    fixed2 = lambda i: (0, 0)
    fixed3 = lambda i: (0, 0, 0)
    return pl.pallas_call(
        body, name=name, grid=(nsteps,),
        out_shape=[jax.ShapeDtypeStruct((S, 7 * E_A), BF16), jax.ShapeDtypeStruct((3, E_A), F32),
                   jax.ShapeDtypeStruct((1, E_A), F32), jax.ShapeDtypeStruct((1, E_A), F32),
                   jax.ShapeDtypeStruct((NDEV, CHUNK, CHUNK), F32), jax.ShapeDtypeStruct((NDEV, CHUNK, 1), F32)],
        in_specs=[pl.BlockSpec((TS, 7 * E_A), lambda i: (i, 0)), _halo_before(E_A, 0), _halo_before(E_A, 2),
                  _halo_after(E_A, 1), _halo_after(E_A, 3),
                  pl.BlockSpec((TS, 2 * E_A), lambda i: (i, 0)), _halo_after(E_A, 0),
                  pl.BlockSpec((None, 3, E_A), const3), pl.BlockSpec((None, 1, E_A), const3),
                  pl.BlockSpec((None, 1, E_A), const3), pl.BlockSpec((None, NDEV, CHUNK, CHUNK), const4),
                  pl.BlockSpec((None, NDEV, CHUNK, 1), const4)],
        out_specs=[pl.BlockSpec((TS, 7 * E_A), lambda i: (i, 0)), pl.BlockSpec((3, E_A), fixed2),
                   pl.BlockSpec((1, E_A), fixed2), pl.BlockSpec((1, E_A), fixed2),
                   pl.BlockSpec((NDEV, CHUNK, CHUNK), fixed3), pl.BlockSpec((NDEV, CHUNK, 1), fixed3)],
        compiler_params=_params(("arbitrary",)),
    )(proj, proj, proj, proj, proj, d_ycat, d_ycat, convw, ln_g3, ln_b3, sgu_w, sgu_bcol)


def _window_count(step, n, win, ext_before):
    rows = CHUNK if ext_before else CHUNK + HALO
    t = step * TS + n * CHUNK + lax.broadcasted_iota(jnp.int32, (rows, 1), 0)
    return jnp.minimum(t + 1, win).astype(F32)


def _pool_weight(wp_ref, g):
    return jnp.concatenate([wp_ref[d, g] for d in range(NDEV)], axis=0)


def _pooled_chunk(p, tail, win, count):
    sums = jnp.concatenate([tail, p], axis=0)
    shift = 1
    while shift < win:
        sums = sums + pltpu.roll(sums, shift, 0)
        shift *= 2
    return sums[HALO:] / count - p


def _pool_mix_fwd(proj, wpool, pscale4, wl, after, name):
    e_c = 4 * G_C

    def body(pj_ref, hp_ref, wp_ref, ps_ref, *rest):
        y_ref, pooled_scr, yraw_scr = rest[-3:]
        step = pl.program_id(0)
        live = (step > 0).astype(F32)
        for g, win in enumerate(POOL_WINDOWS):
            for q in range(G_C // LANE):
                cols = slice(g * G_C + q * LANE, g * G_C + (q + 1) * LANE)
                tail = hp_ref[:, cols].astype(F32) * live
                for n in range(NCH):
                    rows = slice(n * CHUNK, (n + 1) * CHUNK)
                    p = pj_ref[rows, cols].astype(F32)
                    pooled_scr[rows, q * LANE:(q + 1) * LANE] = _pooled_chunk(
                        p, tail, win, _window_count(step, n, win, True)).astype(BF16)
                    tail = p[CHUNK - HALO:]
            yraw_scr[...] = jnp.dot(pooled_scr[...], _pool_weight(wp_ref, g), preferred_element_type=F32)
            for q in range(G_C // LANE):
                cols = slice(g * G_C + q * LANE, g * G_C + (q + 1) * LANE)
                for n in range(NCH):
                    rows = slice(n * CHUNK, (n + 1) * CHUNK)
                    z = pj_ref[rows, e_c + cols.start:e_c + cols.stop].astype(F32)
                    y_ref[rows, cols] = (yraw_scr[rows, q * LANE:(q + 1) * LANE] * ps_ref[:, cols] * _silu(z)).astype(BF16)

    return pl.pallas_call(
        body, name=name, grid=(S // TS,), out_shape=jax.ShapeDtypeStruct((S, e_c), BF16),
        in_specs=[pl.BlockSpec((TS, 2 * e_c), lambda i: (i, 0)), _halo_before(e_c, 0),
                  pl.BlockSpec((NDEV, 4, G_C // NDEV, G_C), lambda i: (0, 0, 0, 0)),
                  pl.BlockSpec((None, 1, e_c), lambda i: (wl, 0, 0))] + [ANY] * len(after),
        out_specs=pl.BlockSpec((TS, e_c), lambda i: (i, 0)),
        scratch_shapes=[pltpu.VMEM((TS, G_C), BF16), pltpu.VMEM((TS, G_C), F32)],
        compiler_params=_params(("arbitrary",)),
    )(proj, proj, wpool, pscale4, *after)


def _pool_mix_bwd(proj, d_ycat, wpool, pscale4, wl, name):
    e_c = 4 * G_C
    nsteps = S // TS
    rb = G_C // NDEV

    def body(pj_ref, hp_ref, hz_ref, dy_ref, hdy_ref, wp_ref, ps_ref,
             dp_ref, dps_ref, dwp_ref, pooled_scr, yraw_scr, dyraw_scr, dpool_scr, acc_w):
        step = pl.program_id(0)

        @pl.when(step == 0)
        def _():
            dps_ref[...] = jnp.zeros_like(dps_ref)
            acc_w[...] = jnp.zeros_like(acc_w)

        live_before = (step > 0).astype(F32)
        live_after = (step < nsteps - 1).astype(F32)
        for g, win in enumerate(POOL_WINDOWS):
            weight = _pool_weight(wp_ref, g)
            for q in range(G_C // LANE):
                cols = slice(g * G_C + q * LANE, g * G_C + (q + 1) * LANE)
                tail = hp_ref[:, cols].astype(F32) * live_before
                for n in range(NCH):
                    rows = slice(n * CHUNK, (n + 1) * CHUNK)
                    p = pj_ref[rows, cols].astype(F32)
                    pooled_scr[rows, q * LANE:(q + 1) * LANE] = _pooled_chunk(
                        p, tail, win, _window_count(step, n, win, True)).astype(BF16)
                    tail = p[CHUNK - HALO:]
            yraw_scr[...] = jnp.dot(pooled_scr[...], weight, preferred_element_type=F32)
            for q in range(G_C // LANE):
                cols = slice(g * G_C + q * LANE, g * G_C + (q + 1) * LANE)
                local = slice(q * LANE, (q + 1) * LANE)
                scale = ps_ref[:, cols]
                acc_ps = jnp.zeros((1, LANE), F32)
                for n in range(NCH):
                    rows = slice(n * CHUNK, (n + 1) * CHUNK)
                    sz, dsz = _silu_and_grad(pj_ref[rows, e_c + cols.start:e_c + cols.stop].astype(F32))
                    dyv = dy_ref[rows, cols].astype(F32)
                    yraw = yraw_scr[rows, local]
                    dyraw_scr[rows, local] = (dyv * scale * sz).astype(BF16)
                    acc_ps += jnp.sum(dyv * yraw * sz, axis=0, keepdims=True)
                    dp_ref[rows, e_c + cols.start:e_c + cols.stop] = (dyv * yraw * scale * dsz).astype(BF16)
                dps_ref[:, cols] += acc_ps
                dyraw_scr[TS:, local] = (hdy_ref[:, cols].astype(F32) * scale * _silu(hz_ref[:, cols].astype(F32))
                                         * live_after).astype(BF16)
            dpool_scr[...] = lax.dot_general(dyraw_scr[...], weight, (((1,), (1,)), ((), ())), preferred_element_type=F32)
            acc_w[g] += lax.dot_general(pooled_scr[...], dyraw_scr[:TS, :], (((0,), (0,)), ((), ())),
                                        preferred_element_type=F32)
            for q in range(G_C // LANE):
                cols = slice(g * G_C + q * LANE, g * G_C + (q + 1) * LANE)
                local = slice(q * LANE, (q + 1) * LANE)
                for n in range(NCH):
                    rows = slice(n * CHUNK, (n + 1) * CHUNK)
                    ext = dpool_scr[n * CHUNK:(n + 1) * CHUNK + HALO, local]
                    sums = ext / _window_count(step, n, win, False)
                    shift = 1
                    while shift < win:
                        sums = sums + pltpu.roll(sums, CHUNK + HALO - shift, 0)
                        shift *= 2
                    dp_ref[rows, cols] = (sums[:CHUNK] - ext[:CHUNK]).astype(BF16)

        @pl.when(step == nsteps - 1)
        def _():
            for g in range(4):
                for d in range(NDEV):
                    dwp_ref[d % 2, d // 2, g] = acc_w[g, d * rb:(d + 1) * rb, :].astype(BF16)

    in_specs = [pl.BlockSpec((TS, 2 * e_c), lambda i: (i, 0)), _halo_before(e_c, 0), _halo_after(e_c, 1),
                pl.BlockSpec((TS, e_c), lambda i: (i, 0)), _halo_after(e_c, 0),
                pl.BlockSpec((NDEV, 4, rb, G_C), lambda i: (0, 0, 0, 0)),
                pl.BlockSpec((None, 1, e_c), lambda i: (wl, 0, 0))]
    args = [proj, proj, proj, d_ycat, d_ycat, wpool, pscale4]
    return pl.pallas_call(
        body, name=name, grid=(nsteps,),
        out_shape=[jax.ShapeDtypeStruct((S, 2 * e_c), BF16), jax.ShapeDtypeStruct((1, e_c), F32),
                   jax.ShapeDtypeStruct((2, NDEV // 2) + wpool.shape[1:], BF16)],
        in_specs=in_specs,
        out_specs=[pl.BlockSpec((TS, 2 * e_c), lambda i: (i, 0)), pl.BlockSpec((1, e_c), lambda i: (0, 0)),
                   pl.BlockSpec((2, NDEV // 2, 4, rb, G_C), lambda i: (0, 0, 0, 0, 0))],
        scratch_shapes=[pltpu.VMEM((TS, G_C), BF16), pltpu.VMEM((TS, G_C), F32), pltpu.VMEM((TS + HALO, G_C), BF16),
                        pltpu.VMEM((TS + HALO, G_C), F32), pltpu.VMEM((4, G_C, G_C), F32)],
        compiler_params=_params(("arbitrary",)),
    )(*args)


def _adamw(w, g, m, v):
    m = ADAM_B1 * m + (1.0 - ADAM_B1) * g
    v = ADAM_B2 * v + (1.0 - ADAM_B2) * jnp.square(g)
    m_hat = m / (1.0 - ADAM_B1 ** ADAM_STEP)
    v_hat = v / (1.0 - ADAM_B2 ** ADAM_STEP)
    delta = -ADAM_LR * (m_hat / (jnp.sqrt(v_hat) + ADAM_EPS) + ADAM_WD * w)
    return delta, m, v


def _adam_sharded(w, m, v, chip_parts, landed, my_chip, carried, name):
    nl, nr, ncol = w.shape
    tr = 128
    steps = nr // tr
    nc = len(carried)

    def body(chip_ref, w_ref, m_ref, v_ref, *rest):
        parts, zones = rest[:nl], rest[nl:2 * nl]
        g_ref, d_ref, nm_ref, nv_ref = rest[2 * nl + nc:2 * nl + nc + 4]
        layer = pl.program_id(0)
        g = jnp.zeros((tr, ncol), F32)
        for l in range(nl):
            gl = parts[l][...].astype(F32)
            for q in range(3):
                gl = gl + zones[l][q].astype(F32)
            g = jnp.where(layer == l, gl, g)
        g_ref[...] = g
        d_ref[...], nm_ref[...], nv_ref[...] = _adamw(w_ref[...], g, m_ref[...], v_ref[...])

    def rows_of(l):
        return lambda layer, i, chip_ref: jnp.where(layer == l, i, jnp.where(layer < l, 0, steps - 1))

    spec = pl.BlockSpec((None, tr, ncol), lambda layer, i, chip_ref: (layer, i, 0))
    part_specs = [pl.BlockSpec((None, tr, ncol), lambda layer, i, chip_ref, l=l: (chip_ref[0], rows_of(l)(layer, i, chip_ref), 0))
                  for l in range(nl)]
    zone_specs = [pl.BlockSpec((3, tr, ncol), lambda layer, i, chip_ref, l=l: (0, rows_of(l)(layer, i, chip_ref), 0))
                  for l in range(nl)]
    grid_spec = pltpu.PrefetchScalarGridSpec(
        num_scalar_prefetch=1, grid=(nl, steps), in_specs=[spec, spec, spec] + part_specs + zone_specs + [ANY] * nc,
        out_specs=[spec] * 4 + [ANY] * nc)
    return pl.pallas_call(
        body, name=name, grid_spec=grid_spec,
        out_shape=[jax.ShapeDtypeStruct(w.shape, F32)] * 4 + [jax.ShapeDtypeStruct(a.shape, a.dtype) for a in carried],
        input_output_aliases={4 + 2 * nl + k: 4 + k for k in range(nc)},
        compiler_params=_params(("arbitrary", "arbitrary")),
    )(my_chip, w, m, v, *chip_parts, *landed, *carried)


def _adam_small(w, g, m, v, name):
    def body(w_ref, g_ref, m_ref, v_ref, d_ref, nm_ref, nv_ref):
        d_ref[...], nm_ref[...], nv_ref[...] = _adamw(w_ref[...], g_ref[...], m_ref[...], v_ref[...])

    return pl.pallas_call(body, name=name, out_shape=[jax.ShapeDtypeStruct(w.shape, F32)] * 3,
                          in_specs=[VMEM_FULL] * 4, out_specs=[VMEM_FULL] * 3, compiler_params=_params())(w, g, m, v)


def _sum_devices(gathered, name):
    _, nr, ncol = gathered.shape

    def body(g_ref, o_ref):
        acc = g_ref[0].astype(F32)
        for s in range(1, NDEV):
            acc = acc + g_ref[s].astype(F32)
        o_ref[...] = acc

    return pl.pallas_call(body, name=name, grid=(1,), out_shape=jax.ShapeDtypeStruct((nr, ncol), F32),
                          in_specs=[pl.BlockSpec((NDEV, nr, ncol), lambda i: (0, 0, 0))],
                          out_specs=pl.BlockSpec((nr, ncol), lambda i: (0, 0)),
                          compiler_params=_params(("arbitrary",)))(gathered)


def _ada_weight_adam(cact_t, dmod_mine, w, m, v):
    def body(ct_ref, dm_ref, w_ref, m_ref, v_ref, g_ref, d_ref, nm_ref, nv_ref):
        ct, dm = ct_ref[...], dm_ref[...]
        g = ct[:, 0:1] * dm[0:1, :]
        for e in range(1, NDEV):
            g = g + ct[:, e:e + 1] * dm[e:e + 1, :]
        g_ref[...] = g
        d_ref[...], nm_ref[...], nv_ref[...] = _adamw(w_ref[...], g, m_ref[...], v_ref[...])

    spec = pl.BlockSpec((None, D, ADA_NC), lambda l: (l, 0, 0))
    return pl.pallas_call(
        body, name="ada_weight_adam", grid=(DEPTH,), out_shape=[jax.ShapeDtypeStruct(w.shape, F32)] * 4,
        in_specs=[pl.BlockSpec((D, NDEV), lambda l: (0, 0)), pl.BlockSpec((None, NDEV, ADA_NC), lambda l: (l, 0, 0)),
                  spec, spec, spec],
        out_specs=[spec] * 4, compiler_params=_params(("arbitrary",)),
    )(cact_t, dmod_mine, w, m, v)


def _pad_rows(a, rows):
    a = a.reshape(-1, D)
    return jnp.pad(a, ((0, rows - a.shape[0]), (0, 0)))


def kernel(x, c, norm_g, ada_w, ada_b, ab_w_in, ab_conv_w, ab_ln_g, ab_ln_b, ab_sgu_w, ab_sgu_b, ab_w_out, c_w_in, c_pool_w, c_pool_scale, c_w_out, final_g, loss_target, m_norm_g, m_ada_w, m_ada_b, m_ab_w_in, m_ab_conv_w, m_ab_ln_g, m_ab_ln_b, m_ab_sgu_w, m_ab_sgu_b, m_ab_w_out, m_c_w_in, m_c_pool_w, m_c_pool_scale, m_c_w_out, m_final_g, v_norm_g, v_ada_w, v_ada_b, v_ab_w_in, v_ab_conv_w, v_ab_ln_g, v_ab_ln_b, v_ab_sgu_w, v_ab_sgu_b, v_ab_w_out, v_c_w_in, v_c_pool_w, v_c_pool_scale, v_c_w_out, v_final_g):
    x_pos, y_pos, c_pos = _position()
    me = _index((x_pos, y_pos, c_pos))
    core = c_pos.astype(jnp.int32).reshape(1)
    my_chip = (2 * x_pos + y_pos).astype(jnp.int32).reshape(1)
    me1 = me.astype(jnp.int32).reshape(1)
    x0 = x.reshape(S, D)
    target = loss_target.reshape(S, D)
    norm_g3 = norm_g.reshape(DEPTH, 1, D)
    ln_g3, ln_b3 = ab_ln_g.reshape(2, 1, E_A), ab_ln_b.reshape(2, 1, E_A)
    sgu_bcol = ab_sgu_b.reshape(2, NDEV, CHUNK, 1)
    rb = G_C // NDEV
    pool_w3, m_pool_w3, v_pool_w3 = (a.reshape(2, 4 * rb, G_C) for a in (c_pool_w, m_c_pool_w, v_c_pool_w))

    cact_all, mod = _ada_forward(c, ada_w, ada_b)
    convw_all, pscale_all = _gather([ab_conv_w, c_pool_scale], "gather_small_weights")
    convw = jnp.transpose(convw_all, (1, 2, 0, 3)).reshape(2, 3, E_A)
    pscale4 = jnp.transpose(pscale_all, (1, 0, 2)).reshape(2, 1, 4 * G_C)
    zones = []
    for layer in range(DEPTH):
        wl = layer // 2
        if layer % 2 == 0:
            zones.append([_to_zone(ab_w_in, wl, me1, BF16, f"cast_w_in_{layer}"), _to_zone(ab_w_out, wl, me1, BF16, f"cast_w_out_{layer}")])
        else:
            zones.append([_to_zone(c_w_in, wl, me1, BF16, f"cast_w_in_{layer}"), _to_zone(c_w_out, wl, me1, BF16, f"cast_w_out_{layer}"),
                          _to_zone(pool_w3, wl, me1, BF16, f"cast_pool_w_{layer}")])

    def gathered(flight, after, layer):
        wg = _gather_end(flight, [after], f"gather_end_{layer}")
        return [wg[0], wg[1].reshape(-1, D)] + [w.reshape(NDEV, 4, rb, G_C) for w in wg[2:]]

    flight, (mod,) = _gather_start(zones[0], [convw_all], [mod], "gather_start_0")
    flight, (mod,) = _gather_mid(flight, [], [mod], "gather_mid_0")
    next_flight, (mod,) = _gather_start(zones[1], [], [mod], "gather_start_1")
    flight = _gather_late(flight, [mod], "gather_late_0")
    xs, hts, projs, ycats, outs, gathered_w = [x0], [], [], [], [], [gathered(flight, mod, 0)]
    for layer in range(DEPTH):
        wl = layer // 2
        even = layer % 2 == 0
        wg = gathered_w[layer]
        h_t, proj = _norm_proj(xs[-1], mod, norm_g3, wg[0], layer, f"norm_proj_{layer}")
        if layer + 1 < DEPTH:
            flight, (h_t,) = _gather_mid(next_flight, [], [h_t], f"gather_mid_{layer + 1}")
            if layer + 2 < DEPTH:
                next_flight, (h_t,) = _gather_start(zones[layer + 2], [], [h_t], f"gather_start_{layer + 2}")
        if even:
            ycat = _even_mix_fwd(proj, convw, ln_g3, ln_b3, ab_sgu_w, sgu_bcol, wl, [h_t], f"even_mix_fwd_{layer}")
        else:
            ycat = _pool_mix_fwd(proj, wg[2], pscale4, wl, [h_t], f"pool_mix_fwd_{layer}")
        if layer + 1 < DEPTH:
            flight = _gather_late(flight, [ycat], f"gather_late_{layer + 1}")
        x_new, out = _out_proj(ycat, wg[1], xs[-1], mod, layer, f"out_proj_{layer}")
        if layer + 1 < DEPTH:
            gathered_w.append(gathered(flight, x_new, layer + 1))
        xs.append(x_new)
        hts.append(h_t)
        projs.append(proj)
        ycats.append(ycat)
        outs.append(out)

    dx, loss_part, d_final_g = _final_loss(xs[DEPTH], target, final_g.reshape(1, D))

    d_mod, d_norm_g = [None] * DEPTH, [None] * DEPTH
    small, scatters, landed, res = {}, {}, {}, {}

    def finish_scatter(layer, after):
        send_sems, recv_sems, chip_parts, zones = scatters[layer]
        landed[layer] = _scatter_end(chip_parts, zones, send_sems, recv_sems, after, f"scatter_end_{layer}")

    def flat(a):
        return a.reshape(a.shape[0], -1, a.shape[-1])

    def sharded_adam(k, j, layers, w, m, v, carried):
        outs4 = _adam_sharded(w, m, v, [flat(landed[l][0][j]) for l in layers], [flat(landed[l][1][j]) for l in layers],
                              my_chip, carried, "adam_" + k)
        res[k] = [o.reshape(c_pool_w.shape) if k == "c_pool_w" else o for o in outs4[:4]]
        return list(outs4[4:])

    previous = None
    for layer in reversed(range(DEPTH)):
        wl = layer // 2
        even = layer % 2 == 0
        wg = gathered_w[layer]
        carried = [] if previous is None else [scatters[previous][2][0]]
        d_ycat, grad_out, d_gate, carried = _out_bwd(dx, outs[layer], ycats[layer], wg[1], mod, layer, carried, f"out_bwd_{layer}")
        if previous is not None:
            scatters[previous][2][0] = carried[0]
        parts = [None, grad_out]
        if even:
            d_proj, d_cw, d_lg, d_lb, d_sw, d_sb = _even_mix_bwd(
                projs[layer], d_ycat, convw, ln_g3, ln_b3, ab_sgu_w, sgu_bcol, wl, f"even_mix_bwd_{layer}")
            small[layer] = (d_cw, d_lg, d_lb, d_sw, d_sb)
        else:
            d_proj, d_ps, d_pool = _pool_mix_bwd(projs[layer], d_ycat, wg[2], pscale4, wl, f"pool_mix_bwd_{layer}")
            small[layer] = (d_ps,)
            parts.append(d_pool)
        parts[0] = _weight_grad(hts[layer], d_proj, f"grad_w_in_{layer}")
        pair_send, pair_recv, parts, from_sibling = _pair_start(parts, f"pair_start_{layer}")
        if layer > 0:
            dx, d_shift, d_scale, d_norm_g[layer], parts[0] = _dh_norm_bwd(
                d_proj, wg[0], xs[layer], dx, mod, norm_g3, layer, parts[0], f"dh_norm_bwd_{layer}")
            pair_after = dx
        else:
            finish_scatter(1, d_proj)
            finish_scatter(3, d_proj)
            parts[0], = sharded_adam("c_w_out", 1, (1, 3), c_w_out, m_c_w_out, v_c_w_out, [parts[0]])
            parts[0], = sharded_adam("c_pool_w", 2, (1, 3), pool_w3, m_pool_w3, v_pool_w3, [parts[0]])
            pair_after = res["c_pool_w"][0]
        parts, from_sibling = _pair_end(parts, from_sibling, pair_send, pair_recv, pair_after, f"pair_end_{layer}")
        chip_parts = [_pair_sum(p, q, core, f"pair_sum_{layer}_{j}") for j, (p, q) in enumerate(zip(parts, from_sibling))]
        send_sems, recv_sems, chip_parts, zones = _scatter_start(chip_parts, f"scatter_start_{layer}")
        if layer == 0:
            chip_parts[0], = sharded_adam("c_w_in", 0, (1, 3), c_w_in, m_c_w_in, v_c_w_in, [chip_parts[0]])
            dx, d_shift, d_scale, d_norm_g[layer], chip_parts[0] = _dh_norm_bwd(
                d_proj, wg[0], xs[layer], dx, mod, norm_g3, layer, chip_parts[0], f"dh_norm_bwd_{layer}")
        scatters[layer] = [send_sems, recv_sems, chip_parts, zones]
        previous = layer
        d_mod[layer] = jnp.concatenate([d_shift, d_scale, d_gate], axis=0)
    grad_x = dx.reshape(x.shape)

    sections = [("norm_g", jnp.concatenate(d_norm_g, axis=0), 8),
                ("d_mod", jnp.concatenate(d_mod, axis=0), 16),
                ("ab_ln_g", jnp.concatenate([small[0][1], small[2][1]], axis=0), 8),
                ("ab_ln_b", jnp.concatenate([small[0][2], small[2][2]], axis=0), 8),
                ("ab_sgu_b", jnp.stack([small[0][4], small[2][4]]), 8),
                ("final_g", d_final_g, 8),
                ("ab_conv_w", jnp.stack([small[0][0], small[2][0]]), 8),
                ("c_pool_scale", jnp.concatenate([small[1][0], small[3][0]], axis=0), 8),
                ("ab_sgu_w", jnp.stack([small[0][3], small[2][3]]), 256)]
    offsets, at = {}, 0
    for name, _, rows in sections:
        offsets[name] = (at, rows)
        at += rows
    packed = jnp.concatenate([_pad_rows(a, rows) for _, a, rows in sections] + [jnp.zeros((-at % 32, D), F32)], axis=0)
    loss_rows = jnp.pad(loss_part, ((0, 15), (0, D - LANE)))
    small_zones = [_to_zone(packed[None], 0, me1, BF16, "place_small_grads"), _to_zone(loss_rows[None], 0, me1, F32, "place_loss")]
    small_flight, (mod,) = _gather_start(small_zones, [], [mod], "gather_small_start")

    finish_scatter(2, mod)
    finish_scatter(0, mod)
    sharded_adam("ab_w_out", 1, (0, 2), ab_w_out, m_ab_w_out, v_ab_w_out, [])
    sharded_adam("ab_w_in", 0, (0, 2), ab_w_in, m_ab_w_in, v_ab_w_in, [])

    last = res["ab_w_in"][0]
    small_flight, _ = _gather_mid(small_flight, [last], [], "gather_small_mid")
    small_flight = _gather_late(small_flight, [last], "gather_small_late")
    small_grads, losses = _gather_end(small_flight, [last], "gather_small_end")
    summed = _sum_devices(small_grads, "sum_small_grads")
    loss = _sum_devices(losses, "sum_loss")[0, 0]

    def section(name, nrows, src=summed):
        start = offsets[name][0]
        return src[..., start:start + nrows, :]

    grads = {
        "norm_g": section("norm_g", DEPTH),
        "ada_b": section("d_mod", 3 * DEPTH).reshape(DEPTH, 3 * D),
        "ab_ln_g": section("ab_ln_g", 2), "ab_ln_b": section("ab_ln_b", 2),
        "ab_sgu_b": section("ab_sgu_b", 2).reshape(ab_sgu_b.shape),
        "final_g": section("final_g", 1),
        "ab_sgu_w": section("ab_sgu_w", 256).reshape(ab_sgu_w.shape),
        "ab_conv_w": lax.dynamic_slice_in_dim(section("ab_conv_w", 6).reshape(2, 3, E_A), me * HEAD, HEAD, axis=2),
        "c_pool_scale": lax.dynamic_slice_in_dim(section("c_pool_scale", 4).reshape(2, 4 * G_C), me * 256, 256, axis=1),
    }
    small_w = {"norm_g": (norm_g, m_norm_g, v_norm_g), "ada_b": (ada_b, m_ada_b, v_ada_b),
               "ab_ln_g": (ab_ln_g, m_ab_ln_g, v_ab_ln_g), "ab_ln_b": (ab_ln_b, m_ab_ln_b, v_ab_ln_b),
               "ab_sgu_b": (ab_sgu_b, m_ab_sgu_b, v_ab_sgu_b),
               "final_g": (final_g.reshape(1, D), m_final_g.reshape(1, D), v_final_g.reshape(1, D)),
               "ab_sgu_w": (ab_sgu_w, m_ab_sgu_w, v_ab_sgu_w), "ab_conv_w": (ab_conv_w, m_ab_conv_w, v_ab_conv_w),
               "c_pool_scale": (c_pool_scale, m_c_pool_scale, v_c_pool_scale)}
    for k, (w, m, v) in small_w.items():
        res[k] = [grads[k]] + list(_adam_small(w, grads[k], m, v, "adam_" + k))
    res["final_g"] = [a.reshape(D) for a in res["final_g"]]

    dmod_all = section("d_mod", 3 * DEPTH, small_grads).reshape(NDEV, DEPTH, 3 * D)
    dmod_mine = jnp.transpose(lax.dynamic_slice_in_dim(dmod_all, me * ADA_NC, ADA_NC, axis=2), (1, 0, 2)).astype(F32)
    res["ada_w"] = _ada_weight_adam(jnp.transpose(cact_all.reshape(NDEV, D)), dmod_mine, ada_w, m_ada_w, v_ada_w)

    order = ["norm_g", "ada_w", "ada_b", "ab_w_in", "ab_conv_w", "ab_ln_g", "ab_ln_b", "ab_sgu_w", "ab_sgu_b",
             "ab_w_out", "c_w_in", "c_pool_w", "c_pool_scale", "c_w_out", "final_g"]
    return (loss, grad_x, *[res[k][0] for k in order], *[res[k][1] for k in order],
            *[res[k][2] for k in order], *[res[k][3] for k in order])
```

```python
import jax
import jax.numpy as jnp
from jax import lax
from jax.experimental import pallas as pl
from jax.experimental.pallas import tpu as pltpu

F32, BF16 = jnp.float32, jnp.bfloat16
S, D = 2048, 1024
NDEV = 8
DEPTH = 4
EPS = 1e-6
E_A = 1024
HEAD = 128
CHUNK = 128
POOL_WINDOWS = (2, 4, 8, 16)
G_C = 512
HALO = 16
ADA_NC = 384
MIB = 1024 * 1024
LANE = 128

ADAM_LR, ADAM_B1, ADAM_B2, ADAM_EPS, ADAM_WD, ADAM_STEP = 0.001, 0.9, 0.999, 1e-08, 0.01, 10

ANY = pl.BlockSpec(memory_space=pl.ANY)
VMEM_FULL = pl.BlockSpec(memory_space=pltpu.VMEM)
IN_HBM = pl.BlockSpec(memory_space=pltpu.HBM)
SEMAPHORES = pl.BlockSpec(memory_space=pltpu.SEMAPHORE)
IN_FLIGHT = pltpu.SideEffectType.DATAFLOW_SIDE_EFFECTING


V7X_VMEM_MIB = 64
VMEM_LIMIT_MIB = V7X_VMEM_MIB - 4


def _params(semantics=None):
    return pltpu.CompilerParams(dimension_semantics=semantics, vmem_limit_bytes=VMEM_LIMIT_MIB * MIB)


def _silu(z):
    return z * jax.nn.sigmoid(z)


def _silu_and_grad(z):
    sig = jax.nn.sigmoid(z)
    return z * sig, sig * (1.0 + z * (1.0 - sig))


def _position():
    return lax.axis_index("x"), lax.axis_index("y"), lax.axis_index("c")


def _index(pos):
    return 4 * pos[0] + 2 * pos[1] + pos[2]


def _peer(pos, k):
    flipped = tuple(1 - p if (k >> (2 - b)) & 1 else p for b, p in enumerate(pos))
    return flipped, _index(flipped)


def _remote(src, dst, send_sem, recv_sem, device):
    return pltpu.make_async_remote_copy(src_ref=src, dst_ref=dst, send_sem=send_sem, recv_sem=recv_sem,
                                        device_id=device, device_id_type=pl.DeviceIdType.MESH)


def _gather(arrays, name):
    n = len(arrays)
    out_shape = [jax.ShapeDtypeStruct((NDEV,) + a.shape, a.dtype) for a in arrays]

    def body(*refs):
        ins, outs = refs[:n], refs[n:2 * n]
        send_sems, recv_sems, own_sems = refs[2 * n:]
        x, y, c = _position()
        me = _index((x, y, c))
        sibling = (x, y, 1 - c)
        chips = [(1 - x, y), (x, 1 - y), (1 - x, 1 - y)]

        def block_copy(j, k, owner, to, src=None):
            rows = outs[j].at[_index(owner)]
            return _remote(rows if src is None else src, rows, send_sems.at[j, k], recv_sems.at[j, k], to)

        own, first, passed = [], [], []
        for j in range(n):
            own.append(pltpu.make_async_copy(ins[j], outs[j].at[me], own_sems.at[j]))
            first.append(block_copy(j, 0, (x, y, c), sibling, src=ins[j]))
            first += [block_copy(j, 1 + q, (x, y, c), (*chip, c), src=ins[j]) for q, chip in enumerate(chips)]
        for copy in own + first:
            copy.start()
        for q, chip in enumerate(chips):
            for j in range(n):
                block_copy(j, 1 + q, (*chip, c), (x, y, c)).wait_recv()
                forward = block_copy(j, 4 + q, (*chip, c), sibling)
                forward.start()
                passed.append(forward)
        for j in range(n):
            block_copy(j, 0, sibling, (x, y, c)).wait_recv()
            for q, chip in enumerate(chips):
                block_copy(j, 4 + q, (*chip, 1 - c), (x, y, c)).wait_recv()
        for copy in first + passed:
            copy.wait_send()
        for copy in own:
            copy.wait()

    return pl.pallas_call(
        body, name=name, out_shape=out_shape, in_specs=[ANY] * n, out_specs=[ANY] * n,
        scratch_shapes=[pltpu.SemaphoreType.DMA((n, NDEV - 1)), pltpu.SemaphoreType.DMA((n, NDEV - 1)),
                        pltpu.SemaphoreType.DMA((n,))],
    )(*arrays)


def _pair_start(parts, name):
    n = len(parts)
    lands = [_in_hbm(lax.empty(p.shape[1:], p.dtype)) for p in parts]

    def body(*refs):
        ins, zones = refs[:n], refs[n:2 * n]
        send_sems, recv_sems = refs[2 * n:2 * n + 2]
        x, y, c = _position()
        for j in range(n):
            _remote(ins[j].at[1 - c], zones[j], send_sems.at[j], recv_sems.at[j], (x, y, 1 - c)).start()

    outs = pl.pallas_call(
        body, name=name,
        out_shape=(pltpu.SemaphoreType.DMA((n,)), pltpu.SemaphoreType.DMA((n,)),
                   *[pltpu.HBM(p.shape, p.dtype) for p in parts], *[pltpu.HBM(z.shape, z.dtype) for z in lands]),
        in_specs=[IN_HBM] * (2 * n), out_specs=(SEMAPHORES, SEMAPHORES, *[IN_HBM] * (2 * n)),
        input_output_aliases={j: 2 + j for j in range(2 * n)},
        compiler_params=pltpu.CompilerParams(has_side_effects=IN_FLIGHT),
    )(*[_in_hbm(p) for p in parts], *lands)
    return outs[0], outs[1], list(outs[2:2 + n]), list(outs[2 + n:])


def _pair_end(parts, zones, send_sems, recv_sems, after, name):
    n = len(parts)

    def body(*refs):
        ins, zs = refs[:n], refs[n:2 * n]
        s, r = refs[2 * n:2 * n + 2]
        me = _position()
        for j in range(n):
            copy = _remote(ins[j].at[0], zs[j], s.at[j], r.at[j], me)
            copy.wait_send()
            copy.wait_recv()

    outs = pl.pallas_call(
        body, name=name,
        out_shape=(*[pltpu.HBM(p.shape, p.dtype) for p in parts], *[pltpu.HBM(z.shape, z.dtype) for z in zones]),
        in_specs=[IN_HBM] * (2 * n) + [SEMAPHORES, SEMAPHORES, ANY], out_specs=tuple([IN_HBM] * (2 * n)),
        input_output_aliases={j: j for j in range(2 * n)},
        compiler_params=pltpu.CompilerParams(has_side_effects=IN_FLIGHT),
    )(*parts, *zones, send_sems, recv_sems, after)
    return list(outs[:n]), list(outs[n:])


def _pair_sum(part, from_sibling, core, name):
    ncol = part.shape[-1]
    p3 = part.reshape(2, -1, ncol)
    q2 = from_sibling.reshape(-1, ncol)
    nrows = q2.shape[0]
    tr = 512

    def body(core_ref, p_ref, q_ref, o_ref):
        o_ref[...] = (p_ref[...].astype(F32) + q_ref[...].astype(F32)).astype(BF16)

    grid_spec = pltpu.PrefetchScalarGridSpec(
        num_scalar_prefetch=1, grid=(nrows // tr,),
        in_specs=[pl.BlockSpec((None, tr, ncol), lambda i, core_ref: (core_ref[0], i, 0)),
                  pl.BlockSpec((tr, ncol), lambda i, core_ref: (i, 0))],
        out_specs=pl.BlockSpec((tr, ncol), lambda i, core_ref: (i, 0)))
    out = pl.pallas_call(body, name=name, grid_spec=grid_spec, out_shape=jax.ShapeDtypeStruct(q2.shape, BF16),
                         compiler_params=_params(("arbitrary",)))(core, p3, q2)
    return out.reshape(from_sibling.shape)


def _in_hbm(a):
    return pltpu.with_memory_space_constraint(a, pltpu.HBM)


def _chips(x, y):
    return [(1 - x, y), (x, 1 - y), (1 - x, 1 - y)]


def _to_zone(a, wl, me, dtype, name):
    _, rows, cols = a.shape
    tr = 256 if rows % 256 == 0 else rows

    def body(me_ref, a_ref, o_ref):
        o_ref[...] = a_ref[...].astype(dtype)

    grid_spec = pltpu.PrefetchScalarGridSpec(
        num_scalar_prefetch=1, grid=(rows // tr,),
        in_specs=[pl.BlockSpec((None, tr, cols), lambda i, me_ref: (wl, i, 0))],
        out_specs=pl.BlockSpec((None, tr, cols), lambda i, me_ref: (me_ref[0], i, 0)))
    return pl.pallas_call(body, name=name, grid_spec=grid_spec, out_shape=jax.ShapeDtypeStruct((NDEV, rows, cols), dtype),
                          compiler_params=_params(("arbitrary",)))(me, a)


def _halves(block):
    rows = block.shape[0] // 2
    return block.at[pl.ds(0, rows)], block.at[pl.ds(rows, rows)]


def _around(x, y, c):
    return (x, y, 1 - c), (1 - x, y, c), (x, 1 - y, c), (1 - x, 1 - y, c)


def _gather_step1(zs, send, recv, pos):
    sibling, xn, yn, _ = _around(*pos)
    for j, z in enumerate(zs):
        mine = z.at[_index(pos)]
        for k, peer in enumerate((sibling, xn, yn)):
            _remote(mine, mine, send.at[3 * j + k], recv.at[3 * j + k], peer).start()


def _gather_step2(zs, recv1, send, recv, pos):
    sibling, xn, yn, _ = _around(*pos)
    for j, z in enumerate(zs):
        xb, yb = z.at[_index(xn)], z.at[_index(yn)]
        _remote(xb, xb, send.at[4 * j], recv1.at[3 * j + 1], pos).wait_recv()
        _remote(yb, yb, send.at[4 * j], recv1.at[3 * j + 2], pos).wait_recv()
        _remote(xb, xb, send.at[4 * j], recv.at[4 * j], sibling).start()
        _remote(yb, yb, send.at[4 * j + 1], recv.at[4 * j + 1], sibling).start()
        first, second = _halves(xb)[0], _halves(yb)[1]
        _remote(first, first, send.at[4 * j + 2], recv.at[4 * j + 2], yn).start()
        _remote(second, second, send.at[4 * j + 3], recv.at[4 * j + 3], xn).start()


def _gather_step3(zs, recv2, send, recv, pos):
    sibling, _, _, diagonal = _around(*pos)
    for j, z in enumerate(zs):
        db = z.at[_index(diagonal)]
        first, second = _halves(db)
        _remote(first, first, send.at[j], recv2.at[4 * j + 2], pos).wait_recv()
        _remote(second, second, send.at[j], recv2.at[4 * j + 3], pos).wait_recv()
        _remote(db, db, send.at[j], recv.at[j], sibling).start()


def _gather_step4(zs, send1, recv1, send2, recv2, send3, recv3, pos):
    x, y, c = pos
    sibling = (x, y, 1 - c)
    _, sx, sy, sd = _around(*sibling)
    for j, z in enumerate(zs):
        for owner, send, recv, k in ((sibling, send1, recv1, 3 * j), (sx, send2, recv2, 4 * j), (sy, send2, recv2, 4 * j + 1),
                                     (sd, send3, recv3, j)):
            block = z.at[_index(owner)]
            _remote(block, block, send.at[k], recv.at[k], pos).wait_recv()
    for j, z in enumerate(zs):
        block = z.at[0]
        half = _halves(block)[0]
        for ref, send, recv, k in ([(block, send1, recv1, 3 * j + k) for k in range(3)]
                                   + [(block, send2, recv2, 4 * j), (block, send2, recv2, 4 * j + 1),
                                      (half, send2, recv2, 4 * j + 2), (half, send2, recv2, 4 * j + 3), (block, send3, recv3, j)]):
            _remote(ref, ref, send.at[k], recv.at[k], pos).wait_send()


def _flight_call(step, name, zones, sems_in, nsems_out, after, carried):
    n, m, k = len(zones), len(carried), len(sems_in)

    def body(*refs):
        zs = refs[:n]
        given = refs[n + m:n + m + k]
        made = refs[n + m + k + len(after):n + m + k + len(after) + (2 if nsems_out else 0)]
        step(zs, *given, *made, _position())

    sem_out = (pltpu.SemaphoreType.DMA((nsems_out,)),) * 2 if nsems_out else ()
    outs = pl.pallas_call(
        body, name=name,
        out_shape=(*sem_out, *[pltpu.HBM(z.shape, z.dtype) for z in zones], *[jax.ShapeDtypeStruct(a.shape, a.dtype) for a in carried]),
        in_specs=[IN_HBM] * n + [ANY] * m + [SEMAPHORES] * k + [ANY] * len(after),
        out_specs=(*[SEMAPHORES] * len(sem_out), *[IN_HBM] * n, *[ANY] * m),
        input_output_aliases={j: len(sem_out) + j for j in range(n + m)},
        compiler_params=pltpu.CompilerParams(has_side_effects=IN_FLIGHT),
    )(*[_in_hbm(z) for z in zones], *carried, *sems_in, *after)
    sems = list(outs[:len(sem_out)])
    return sems, list(outs[len(sem_out):len(sem_out) + n]), list(outs[len(sem_out) + n:])


def _gather_start(zones, after, carried, name):
    (send1, recv1), zones, carried = _flight_call(_gather_step1, name, zones, [], 3 * len(zones), after, carried)
    return {"s1": send1, "r1": recv1, "zones": zones}, carried


def _gather_mid(flight, after, carried, name):
    step = lambda zs, recv1, send, recv, pos: _gather_step2(zs, recv1, send, recv, pos)
    (send2, recv2), zones, carried = _flight_call(step, name, flight["zones"], [flight["r1"]], 4 * len(flight["zones"]), after, carried)
    return {**flight, "s2": send2, "r2": recv2, "zones": zones}, carried


def _gather_late(flight, after, name):
    step = lambda zs, recv2, send, recv, pos: _gather_step3(zs, recv2, send, recv, pos)
    (send3, recv3), zones, _ = _flight_call(step, name, flight["zones"], [flight["r2"]], len(flight["zones"]), after, [])
    return {**flight, "s3": send3, "r3": recv3, "zones": zones}


def _gather_end(flight, after, name):
    sems = [flight[k] for k in ("s1", "r1", "s2", "r2", "s3", "r3")]
    _, zones, _ = _flight_call(_gather_step4, name, flight["zones"], sems, 0, after, [])
    return zones


def _scatter_start(parts, name):
    n = len(parts)
    lands = [_in_hbm(lax.empty((3,) + p.shape[1:], p.dtype)) for p in parts]

    def body(*refs):
        ins, zones = refs[:n], refs[n:2 * n]
        send_sems, recv_sems = refs[2 * n:2 * n + 2]
        x, y, c = _position()
        for j in range(n):
            for q, (px, py) in enumerate(_chips(x, y)):
                _remote(ins[j].at[2 * px + py], zones[j].at[q], send_sems.at[3 * j + q], recv_sems.at[3 * j + q],
                        (px, py, c)).start()

    outs = pl.pallas_call(
        body, name=name,
        out_shape=(pltpu.SemaphoreType.DMA((3 * n,)), pltpu.SemaphoreType.DMA((3 * n,)),
                   *[pltpu.HBM(p.shape, p.dtype) for p in parts], *[pltpu.HBM(z.shape, z.dtype) for z in lands]),
        in_specs=[IN_HBM] * (2 * n), out_specs=(SEMAPHORES, SEMAPHORES, *[IN_HBM] * (2 * n)),
        input_output_aliases={j: 2 + j for j in range(2 * n)},
        compiler_params=pltpu.CompilerParams(has_side_effects=IN_FLIGHT),
    )(*[_in_hbm(p) for p in parts], *lands)
    return outs[0], outs[1], list(outs[2:2 + n]), list(outs[2 + n:])


def _scatter_end(parts, zones, send_sems, recv_sems, after, name):
    n = len(parts)

    def body(*refs):
        ins, zs = refs[:n], refs[n:2 * n]
        s, r = refs[2 * n:2 * n + 2]
        me = _position()
        for j in range(n):
            for q in range(3):
                copy = _remote(ins[j].at[0], zs[j].at[q], s.at[3 * j + q], r.at[3 * j + q], me)
                copy.wait_send()
                copy.wait_recv()

    outs = pl.pallas_call(
        body, name=name,
        out_shape=(*[pltpu.HBM(p.shape, p.dtype) for p in parts], *[pltpu.HBM(z.shape, z.dtype) for z in zones]),
        in_specs=[IN_HBM] * (2 * n) + [SEMAPHORES, SEMAPHORES, ANY], out_specs=tuple([IN_HBM] * (2 * n)),
        input_output_aliases={j: j for j in range(2 * n)},
        compiler_params=pltpu.CompilerParams(has_side_effects=IN_FLIGHT),
    )(*parts, *zones, send_sems, recv_sems, after)
    return list(outs[:n]), list(outs[n:])


def _ada_forward(c, ada_w, ada_b):
    def body(c_ref, w_ref, b_ref, cact_ref, mod_ref, gbuf, modrow, send_sems, recv_sems):
        pos = _position()
        me = _index(pos)

        def to_all(ref, round_):
            copies = []
            for k in range(1, NDEV):
                peer, _ = _peer(pos, k)
                copy = pltpu.make_async_remote_copy(
                    src_ref=ref.at[me], dst_ref=ref.at[me], send_sem=send_sems.at[round_, k - 1],
                    recv_sem=recv_sems.at[round_, k - 1], device_id=peer, device_id_type=pl.DeviceIdType.MESH)
                copy.start()
                copies.append(copy)
            for copy in copies:
                copy.wait()

        cact_ref[me] = _silu(c_ref[...])
        to_all(cact_ref, 0)
        rows = lax.broadcasted_iota(jnp.int32, (NDEV, D), 0)
        cact = jnp.zeros((NDEV, D), F32)
        for e in range(NDEV):
            cact = jnp.where(rows == e, cact_ref[e], cact)
        cact = cact.astype(BF16)
        for l in range(DEPTH):
            gbuf[me, l] = jnp.dot(cact, w_ref[l].astype(BF16), preferred_element_type=F32)
        to_all(gbuf, 1)
        mine = lax.broadcasted_iota(jnp.int32, (NDEV, ADA_NC), 0) == me
        for l in range(DEPTH):
            for d in range(NDEV):
                modrow[:, d * ADA_NC:(d + 1) * ADA_NC] = jnp.sum(jnp.where(mine, gbuf[d, l], 0.0), axis=0, keepdims=True)
            full = modrow[...] + b_ref[l:l + 1, :]
            for w in range(3):
                mod_ref[l, w] = full[:, w * D:(w + 1) * D]

    return pl.pallas_call(
        body, name="ada_forward",
        out_shape=[jax.ShapeDtypeStruct((NDEV, 1, D), F32), jax.ShapeDtypeStruct((DEPTH, 3, 1, D), F32)],
        in_specs=[VMEM_FULL] * 3, out_specs=[VMEM_FULL] * 2,
        scratch_shapes=[pltpu.VMEM((NDEV, DEPTH, NDEV, ADA_NC), F32), pltpu.VMEM((1, 3 * D), F32),
                        pltpu.SemaphoreType.DMA((2, NDEV - 1)), pltpu.SemaphoreType.DMA((2, NDEV - 1))],
        compiler_params=_params(),
    )(c, ada_w, ada_b)


def _mod_spec(layer, which, ngrid):
    index = {1: lambda i: (layer, which, 0, 0), 2: lambda i, j: (layer, which, 0, 0)}[ngrid]
    return pl.BlockSpec((None, None, 1, D), index)


W_BLOCKS = 4


def _norm_proj(x, mod, norm_g3, wg, layer, name):
    nb = wg.shape[-1]
    tm = 1024
    wb = W_BLOCKS

    def body(x_ref, g_ref, shift_ref, scale_ref, w_ref, ht_ref, p_ref, h_ref):
        @pl.when(pl.program_id(1) == 0)
        def _():
            xv = x_ref[...]
            r = lax.rsqrt(jnp.mean(xv * xv, axis=-1, keepdims=True) + EPS)
            hn = xv * r * g_ref[...]
            h = hn * (1.0 + scale_ref[...]) + shift_ref[...]
            h_ref[...] = h.astype(BF16)
            ht_ref[...] = h.T.astype(BF16)

        hv = h_ref[...]
        for b in range(wb):
            p_ref[:, b * nb:(b + 1) * nb] = jnp.dot(hv, w_ref[b], preferred_element_type=F32).astype(BF16)

    return pl.pallas_call(
        body, name=name, grid=(S // tm, NDEV // wb),
        out_shape=[jax.ShapeDtypeStruct((D, S), BF16), jax.ShapeDtypeStruct((S, NDEV * nb), BF16)],
        in_specs=[pl.BlockSpec((tm, D), lambda i, d: (i, 0)),
                  pl.BlockSpec((None, 1, D), lambda i, d: (layer, 0, 0)),
                  _mod_spec(layer, 0, 2), _mod_spec(layer, 1, 2),
                  pl.BlockSpec((wb, D, nb), lambda i, d: (d, 0, 0))],
        out_specs=[pl.BlockSpec((D, tm), lambda i, d: (0, i)), pl.BlockSpec((tm, wb * nb), lambda i, d: (i, d))],
        scratch_shapes=[pltpu.VMEM((tm, D), BF16)],
        compiler_params=_params(("arbitrary", "arbitrary")),
    )(x, norm_g3, mod, mod, wg)


def _out_proj(ycat, w_out, x, mod, layer, name):
    tm = 512
    e = w_out.shape[0]

    def body(y_ref, w_ref, x_ref, gate_ref, xn_ref, o_ref):
        acc = jnp.dot(y_ref[...], w_ref[...], preferred_element_type=F32)
        o_ref[...] = acc.astype(BF16)
        xn_ref[...] = x_ref[...] + gate_ref[...] * acc

    return pl.pallas_call(
        body, name=name, grid=(S // tm,),
        out_shape=[jax.ShapeDtypeStruct((S, D), F32), jax.ShapeDtypeStruct((S, D), BF16)],
        in_specs=[pl.BlockSpec((tm, e), lambda i: (i, 0)), pl.BlockSpec((e, D), lambda i: (0, 0)),
                  pl.BlockSpec((tm, D), lambda i: (i, 0)), _mod_spec(layer, 2, 1)],
        out_specs=[pl.BlockSpec((tm, D), lambda i: (i, 0))] * 2,
        compiler_params=_params(("arbitrary",)),
    )(ycat, w_out, x, mod)


def _final_loss(x, target, final_g2):
    tm = 256

    def body(x_ref, t_ref, g_ref, dx_ref, loss_ref, dg_ref):
        @pl.when(pl.program_id(0) == 0)
        def _():
            loss_ref[...] = jnp.zeros_like(loss_ref)
            dg_ref[...] = jnp.zeros_like(dg_ref)

        xv, g = x_ref[...], g_ref[...]
        r = lax.rsqrt(jnp.mean(xv * xv, axis=-1, keepdims=True) + EPS)
        xn = xv * r
        err = xn * g - t_ref[...]
        loss_ref[...] += 0.5 * jnp.sum(jnp.mean(err * err, axis=-1, keepdims=True), axis=0, keepdims=True)
        dy = err * (1.0 / D)
        dg_ref[...] += jnp.sum(dy * xn, axis=0, keepdims=True)
        u = dy * g
        dx_ref[...] = r * (u - xn * jnp.mean(xn * u, axis=-1, keepdims=True))

    tile = pl.BlockSpec((tm, D), lambda i: (i, 0))
    row = pl.BlockSpec((1, D), lambda i: (0, 0))
    return pl.pallas_call(
        body, name="final_loss", grid=(S // tm,),
        out_shape=[jax.ShapeDtypeStruct((S, D), F32), jax.ShapeDtypeStruct((1, LANE), F32), jax.ShapeDtypeStruct((1, D), F32)],
        in_specs=[tile, tile, row], out_specs=[tile, pl.BlockSpec((1, LANE), lambda i: (0, 0)), row],
        compiler_params=_params(("arbitrary",)),
    )(x, target, final_g2)


def _out_bwd(dx, out, ycat, w_out, mod, layer, carried, name):
    tm = 512
    nsteps = S // tm
    e = ycat.shape[1]
    rb = e // NDEV
    nc = len(carried)

    def body(dx_ref, o_ref, y_ref, w_ref, gate_ref, *rest):
        dy_ref, gw_ref, dgate_ref = rest[nc:nc + 3]
        acc = rest[-1]
        step = pl.program_id(0)

        @pl.when(step == 0)
        def _():
            dgate_ref[...] = jnp.zeros_like(dgate_ref)
            acc[...] = jnp.zeros_like(acc)

        dxv = dx_ref[...]
        d_out = (gate_ref[...] * dxv).astype(BF16)
        dgate_ref[...] += jnp.sum(dxv * o_ref[...].astype(F32), axis=0, keepdims=True)
        dy_ref[...] = lax.dot_general(d_out, w_ref[...], (((1,), (1,)), ((), ())), preferred_element_type=F32).astype(BF16)
        acc[...] += lax.dot_general(y_ref[...], d_out, (((0,), (0,)), ((), ())), preferred_element_type=F32)

        @pl.when(step == nsteps - 1)
        def _():
            for d in range(NDEV):
                gw_ref[d % 2, d // 2] = acc[d * rb:(d + 1) * rb, :].astype(BF16)

    tile = pl.BlockSpec((tm, D), lambda i: (i, 0))
    wide = pl.BlockSpec((tm, e), lambda i: (i, 0))
    outs = pl.pallas_call(
        body, name=name, grid=(nsteps,),
        out_shape=[jax.ShapeDtypeStruct((S, e), BF16), jax.ShapeDtypeStruct((2, NDEV // 2, rb, D), BF16),
                   jax.ShapeDtypeStruct((1, D), F32)] + [jax.ShapeDtypeStruct(a.shape, a.dtype) for a in carried],
        in_specs=[tile, tile, wide, pl.BlockSpec((e, D), lambda i: (0, 0)), _mod_spec(layer, 2, 1)] + [ANY] * nc,
        out_specs=[wide, pl.BlockSpec((2, NDEV // 2, rb, D), lambda i: (0, 0, 0, 0)), pl.BlockSpec((1, D), lambda i: (0, 0))]
        + [ANY] * nc,
        scratch_shapes=[pltpu.VMEM((e, D), F32)],
        input_output_aliases={5 + k: 3 + k for k in range(nc)},
        compiler_params=_params(("arbitrary",)),
    )(dx, out, ycat, w_out, mod, *carried)
    return outs[0], outs[1], outs[2], list(outs[3:])


def _weight_grad(h_t, d_proj, name):
    nb = d_proj.shape[1] // NDEV

    def body(ht_ref, dp_ref, o_ref):
        o_ref[...] = jnp.dot(ht_ref[...], dp_ref[...], preferred_element_type=F32).astype(BF16)

    return pl.pallas_call(
        body, name=name, grid=(NDEV,), out_shape=jax.ShapeDtypeStruct((2, NDEV // 2, D, nb), BF16),
        in_specs=[pl.BlockSpec((D, S), lambda d: (0, 0)), pl.BlockSpec((S, nb), lambda d: (0, d))],
        out_specs=pl.BlockSpec((None, None, D, nb), lambda d: (d % 2, d // 2, 0, 0)),
        compiler_params=_params(("arbitrary",)),
    )(h_t, d_proj)


def _dh_norm_bwd(d_proj, wg, x, dx, mod, norm_g3, layer, carried, name):
    nb = wg.shape[-1]
    tm = 512
    wb = W_BLOCKS
    rc = 128

    def body(dp_ref, w_ref, x_ref, dx_ref, g_ref, scale_ref, carried_ref,
             dxi_ref, dshift_ref, dscale_ref, dg_ref, carried_out, acc):
        i, d = pl.program_id(0), pl.program_id(1)
        nt = (((1,), (1,)), ((), ()))
        part = lax.dot_general(dp_ref[:, :nb], w_ref[0], nt, preferred_element_type=F32)
        for b in range(1, wb):
            part += lax.dot_general(dp_ref[:, b * nb:(b + 1) * nb], w_ref[b], nt, preferred_element_type=F32)

        @pl.when(d == 0)
        def _():
            acc[...] = part

        @pl.when(d != 0)
        def _():
            acc[...] += part

        @pl.when(jnp.logical_and(i == 0, d == 0))
        def _():
            dshift_ref[...] = jnp.zeros_like(dshift_ref)
            dscale_ref[...] = jnp.zeros_like(dscale_ref)
            dg_ref[...] = jnp.zeros_like(dg_ref)

        @pl.when(d == NDEV // wb - 1)
        def _():
            g = g_ref[...]
            scale1 = 1.0 + scale_ref[...]

            def chunk(k, sums):
                rows = pl.ds(pl.multiple_of(k * rc, rc), rc)
                xv, dhv = x_ref[rows, :], acc[rows, :]
                r = lax.rsqrt(jnp.mean(xv * xv, axis=-1, keepdims=True) + EPS)
                xn = xv * r
                dhn = dhv * scale1
                u = dhn * g
                dxi_ref[rows, :] = dx_ref[rows, :] + r * (u - xn * jnp.mean(xn * u, axis=-1, keepdims=True))
                return (sums[0] + jnp.sum(dhv, axis=0, keepdims=True),
                        sums[1] + jnp.sum(dhv * (xn * g), axis=0, keepdims=True),
                        sums[2] + jnp.sum(dhn * xn, axis=0, keepdims=True))

            zero = jnp.zeros((1, D), F32)
            sums = lax.fori_loop(0, tm // rc, chunk, (zero, zero, zero))
            dshift_ref[...] += sums[0]
            dscale_ref[...] += sums[1]
            dg_ref[...] += sums[2]

    tile = pl.BlockSpec((tm, D), lambda i, d: (i, 0))
    row = pl.BlockSpec((1, D), lambda i, d: (0, 0))
    return pl.pallas_call(
        body, name=name, grid=(S // tm, NDEV // wb),
        out_shape=[jax.ShapeDtypeStruct((S, D), F32)] + [jax.ShapeDtypeStruct((1, D), F32)] * 3
        + [jax.ShapeDtypeStruct(carried.shape, carried.dtype)],
        in_specs=[pl.BlockSpec((tm, wb * nb), lambda i, d: (i, d)), pl.BlockSpec((wb, D, nb), lambda i, d: (d, 0, 0)),
                  tile, tile, pl.BlockSpec((None, 1, D), lambda i, d: (layer, 0, 0)), _mod_spec(layer, 1, 2), ANY],
        out_specs=[tile, row, row, row, ANY], scratch_shapes=[pltpu.VMEM((tm, D), F32)],
        input_output_aliases={6: 4}, compiler_params=_params(("arbitrary", "arbitrary")),
    )(d_proj, wg, x, dx, norm_g3, mod, carried)


TS = 256
NCH = TS // CHUNK
HALO_BLOCKS = TS // HALO


def _halo_before(width, col_block):
    return pl.BlockSpec((HALO, width), lambda i: (jnp.maximum(i * HALO_BLOCKS - 1, 0), col_block))


def _halo_after(width, col_block):
    return pl.BlockSpec((HALO, width), lambda i: (jnp.minimum((i + 1) * HALO_BLOCKS, S // HALO - 1), col_block))


def _shift_down(ext, k):
    return pltpu.roll(ext, k, 0)[HALO:]


def _shift_up(ext, k):
    return pltpu.roll(ext, ext.shape[0] - k, 0)[:ext.shape[0] - HALO]


def _layer_norm_head(v, lg, lb):
    mu = jnp.mean(v, axis=-1, keepdims=True)
    vc = v - mu
    rstd = lax.rsqrt(jnp.mean(vc * vc, axis=-1, keepdims=True) + EPS)
    vhat = vc * rstd
    return vhat, rstd, vhat * lg + lb


def _causal_mask():
    return lax.broadcasted_iota(jnp.int32, (CHUNK, CHUNK), 0) >= lax.broadcasted_iota(jnp.int32, (CHUNK, CHUNK), 1)


def _even_mix_fwd(proj, convw, ln_g3, ln_b3, sgu_w, sgu_bcol, wl, after, name):
    def body(pj_ref, hh_ref, hc_ref, cw_ref, lg_ref, lb_ref, sw_ref, sb_ref, *rest):
        y_ref = rest[-1]
        live = (pl.program_id(0) > 0).astype(F32)
        causal = _causal_mask()
        for j in range(E_A // HEAD):
            cols = slice(j * HEAD, (j + 1) * HEAD)
            w0, w1, w2 = cw_ref[0:1, cols], cw_ref[1:2, cols], cw_ref[2:3, cols]
            lg, lb = lg_ref[:, cols], lb_ref[:, cols]
            wm = jnp.where(causal, sw_ref[j], 0.0).astype(BF16)
            bias = sb_ref[j]

            def split(s, rows, cols=cols):
                return pj_ref[rows, s * E_A + cols.start:s * E_A + cols.stop].astype(F32)

            prev_tail = hc_ref[:, cols].astype(F32) * hh_ref[:, cols].astype(F32) * live
            for n in range(NCH):
                rows = slice(n * CHUNK, (n + 1) * CHUNK)
                p = split(2, rows) * split(0, rows)
                ext = jnp.concatenate([prev_tail, p], axis=0)
                prev_tail = p[CHUNK - HALO:]
                cv = w2 * p + w1 * _shift_down(ext, 1) + w0 * _shift_down(ext, 2)
                y_ref[rows, cols] = (split(1, rows) * cv * _silu(split(3, rows))).astype(BF16)
                _, _, vn = _layer_norm_head(split(5, rows), lg, lb)
                mixed = jnp.dot(wm, vn.astype(BF16), preferred_element_type=F32) + bias
                y_ref[rows, E_A + cols.start:E_A + cols.stop] = (split(4, rows) * mixed * _silu(split(6, rows))).astype(BF16)

    const3 = lambda i: (wl, 0, 0)
    const4 = lambda i: (wl, 0, 0, 0)
    return pl.pallas_call(
        body, name=name, grid=(S // TS,), out_shape=jax.ShapeDtypeStruct((S, 2 * E_A), BF16),
        in_specs=[pl.BlockSpec((TS, 7 * E_A), lambda i: (i, 0)), _halo_before(E_A, 0), _halo_before(E_A, 2),
                  pl.BlockSpec((None, 3, E_A), const3), pl.BlockSpec((None, 1, E_A), const3),
                  pl.BlockSpec((None, 1, E_A), const3), pl.BlockSpec((None, NDEV, CHUNK, CHUNK), const4),
                  pl.BlockSpec((None, NDEV, CHUNK, 1), const4)] + [ANY] * len(after),
        out_specs=pl.BlockSpec((TS, 2 * E_A), lambda i: (i, 0)),
        compiler_params=_params(("arbitrary",)),
    )(proj, proj, proj, convw, ln_g3, ln_b3, sgu_w, sgu_bcol, *after)


def _even_mix_bwd(proj, d_ycat, convw, ln_g3, ln_b3, sgu_w, sgu_bcol, wl, name):
    nsteps = S // TS

    def body(pj_ref, hh_ref, hc_ref, hb_ref, hz_ref, dy_ref, hdy_ref, cw_ref, lg_ref, lb_ref, sw_ref, sb_ref,
             dp_ref, dcw_ref, dlg_ref, dlb_ref, dsw_ref, dsb_ref):
        step = pl.program_id(0)

        @pl.when(step == 0)
        def _():
            for ref in (dcw_ref, dlg_ref, dlb_ref, dsw_ref, dsb_ref):
                ref[...] = jnp.zeros_like(ref)

        live_before = (step > 0).astype(F32)
        live_after = (step < nsteps - 1).astype(F32)
        causal = _causal_mask()
        for j in range(E_A // HEAD):
            cols = slice(j * HEAD, (j + 1) * HEAD)
            w0, w1, w2 = cw_ref[0:1, cols], cw_ref[1:2, cols], cw_ref[2:3, cols]
            lg, lb = lg_ref[:, cols], lb_ref[:, cols]
            wmf = jnp.where(causal, sw_ref[j], 0.0)
            wm, wmt = wmf.astype(BF16), wmf.T.astype(BF16)
            bias = sb_ref[j]

            def split(s, rows, cols=cols):
                return pj_ref[rows, s * E_A + cols.start:s * E_A + cols.stop].astype(F32)

            def put(s, rows, val, cols=cols):
                dp_ref[rows, s * E_A + cols.start:s * E_A + cols.stop] = val.astype(BF16)

            ps = [split(2, slice(n * CHUNK, (n + 1) * CHUNK)) * split(0, slice(n * CHUNK, (n + 1) * CHUNK)) for n in range(NCH)]
            next_head = (hdy_ref[:, cols].astype(F32) * hb_ref[:, cols].astype(F32) * _silu(hz_ref[:, cols].astype(F32))
                         * live_after)
            acc_w = [jnp.zeros((1, HEAD), F32) for _ in range(3)]
            for n in reversed(range(NCH)):
                rows = slice(n * CHUNK, (n + 1) * CHUNK)
                p = ps[n]
                tail = ps[n - 1][CHUNK - HALO:] if n > 0 else hc_ref[:, cols].astype(F32) * hh_ref[:, cols].astype(F32) * live_before
                ext = jnp.concatenate([tail, p], axis=0)
                p1, p2 = _shift_down(ext, 1), _shift_down(ext, 2)
                cv = w2 * p + w1 * p1 + w0 * p2
                a_b, a_z = split(1, rows), split(3, rows)
                sz, dsz = _silu_and_grad(a_z)
                dya = dy_ref[rows, cols].astype(F32)
                put(1, rows, dya * cv * sz)
                put(3, rows, dya * a_b * cv * dsz)
                gcv = dya * a_b * sz
                acc_w[0] += jnp.sum(gcv * p2, axis=0, keepdims=True)
                acc_w[1] += jnp.sum(gcv * p1, axis=0, keepdims=True)
                acc_w[2] += jnp.sum(gcv * p, axis=0, keepdims=True)
                gext = jnp.concatenate([gcv, next_head], axis=0)
                next_head = gcv[:HALO]
                dpv = w2 * gcv + w1 * _shift_up(gext, 1) + w0 * _shift_up(gext, 2)
                put(2, rows, dpv * split(0, rows))
                put(0, rows, dpv * split(2, rows))
            for k in range(3):
                dcw_ref[k:k + 1, cols] += acc_w[k]

            acc_lg, acc_lb = jnp.zeros((1, HEAD), F32), jnp.zeros((1, HEAD), F32)
            acc_sw, acc_sb = jnp.zeros((CHUNK, CHUNK), F32), jnp.zeros((CHUNK, 1), F32)
            for n in range(NCH):
                rows = slice(n * CHUNK, (n + 1) * CHUNK)
                u, z = split(4, rows), split(6, rows)
                vhat, rstd, vn = _layer_norm_head(split(5, rows), lg, lb)
                vn16 = vn.astype(BF16)
                mixed = jnp.dot(wm, vn16, preferred_element_type=F32) + bias
                sz, dsz = _silu_and_grad(z)
                dyb = dy_ref[rows, E_A + cols.start:E_A + cols.stop].astype(F32)
                put(4, rows, dyb * mixed * sz)
                put(6, rows, dyb * u * mixed * dsz)
                dmix = dyb * u * sz
                dmix16 = dmix.astype(BF16)
                acc_sb += jnp.sum(dmix, axis=1, keepdims=True)
                acc_sw += lax.dot_general(dmix16, vn16, (((1,), (1,)), ((), ())), preferred_element_type=F32)
                dvn = jnp.dot(wmt, dmix16, preferred_element_type=F32)
                acc_lg += jnp.sum(dvn * vhat, axis=0, keepdims=True)
                acc_lb += jnp.sum(dvn, axis=0, keepdims=True)
                dvh = dvn * lg
                put(5, rows, rstd * (dvh - jnp.mean(dvh, axis=-1, keepdims=True)
                                     - vhat * jnp.mean(dvh * vhat, axis=-1, keepdims=True)))
            dlg_ref[:, cols] += acc_lg
            dlb_ref[:, cols] += acc_lb
            dsw_ref[j] += jnp.where(causal, acc_sw, 0.0)
            dsb_ref[j] += acc_sb

    const3 = lambda i: (wl, 0, 0)
    const4 = lambda i: (wl, 0, 0, 0)
    fixed2 = lambda i: (0, 0)
    fixed3 = lambda i: (0, 0, 0)
    return pl.pallas_call(
        body, name=name, grid=(nsteps,),
        out_shape=[jax.ShapeDtypeStruct((S, 7 * E_A), BF16), jax.ShapeDtypeStruct((3, E_A), F32),
                   jax.ShapeDtypeStruct((1, E_A), F32), jax.ShapeDtypeStruct((1, E_A), F32),
                   jax.ShapeDtypeStruct((NDEV, CHUNK, CHUNK), F32), jax.ShapeDtypeStruct((NDEV, CHUNK, 1), F32)],
        in_specs=[pl.BlockSpec((TS, 7 * E_A), lambda i: (i, 0)), _halo_before(E_A, 0), _halo_before(E_A, 2),
                  _halo_after(E_A, 1), _halo_after(E_A, 3),
                  pl.BlockSpec((TS, 2 * E_A), lambda i: (i, 0)), _halo_after(E_A, 0),
                  pl.BlockSpec((None, 3, E_A), const3), pl.BlockSpec((None, 1, E_A), const3),
                  pl.BlockSpec((None, 1, E_A), const3), pl.BlockSpec((None, NDEV, CHUNK, CHUNK), const4),
                  pl.BlockSpec((None, NDEV, CHUNK, 1), const4)],
        out_specs=[pl.BlockSpec((TS, 7 * E_A), lambda i: (i, 0)), pl.BlockSpec((3, E_A), fixed2),
                   pl.BlockSpec((1, E_A), fixed2), pl.BlockSpec((1, E_A), fixed2),
                   pl.BlockSpec((NDEV, CHUNK, CHUNK), fixed3), pl.BlockSpec((NDEV, CHUNK, 1), fixed3)],
        compiler_params=_params(("arbitrary",)),
    )(proj, proj, proj, proj, proj, d_ycat, d_ycat, convw, ln_g3, ln_b3, sgu_w, sgu_bcol)


def _window_count(step, n, win, ext_before):
    rows = CHUNK if ext_before else CHUNK + HALO
    t = step * TS + n * CHUNK + lax.broadcasted_iota(jnp.int32, (rows, 1), 0)
    return jnp.minimum(t + 1, win).astype(F32)


def _pool_weight(wp_ref, g):
    return jnp.concatenate([wp_ref[d, g] for d in range(NDEV)], axis=0)


def _pooled_chunk(p, tail, win, count):
    sums = jnp.concatenate([tail, p], axis=0)
    shift = 1
    while shift < win:
        sums = sums + pltpu.roll(sums, shift, 0)
        shift *= 2
    return sums[HALO:] / count - p


def _pool_mix_fwd(proj, wpool, pscale4, wl, after, name):
    e_c = 4 * G_C

    def body(pj_ref, hp_ref, wp_ref, ps_ref, *rest):
        y_ref, pooled_scr, yraw_scr = rest[-3:]
        step = pl.program_id(0)
        live = (step > 0).astype(F32)
        for g, win in enumerate(POOL_WINDOWS):
            for q in range(G_C // LANE):
                cols = slice(g * G_C + q * LANE, g * G_C + (q + 1) * LANE)
                tail = hp_ref[:, cols].astype(F32) * live
                for n in range(NCH):
                    rows = slice(n * CHUNK, (n + 1) * CHUNK)
                    p = pj_ref[rows, cols].astype(F32)
                    pooled_scr[rows, q * LANE:(q + 1) * LANE] = _pooled_chunk(
                        p, tail, win, _window_count(step, n, win, True)).astype(BF16)
                    tail = p[CHUNK - HALO:]
            yraw_scr[...] = jnp.dot(pooled_scr[...], _pool_weight(wp_ref, g), preferred_element_type=F32)
            for q in range(G_C // LANE):
                cols = slice(g * G_C + q * LANE, g * G_C + (q + 1) * LANE)
                for n in range(NCH):
                    rows = slice(n * CHUNK, (n + 1) * CHUNK)
                    z = pj_ref[rows, e_c + cols.start:e_c + cols.stop].astype(F32)
                    y_ref[rows, cols] = (yraw_scr[rows, q * LANE:(q + 1) * LANE] * ps_ref[:, cols] * _silu(z)).astype(BF16)

    return pl.pallas_call(
        body, name=name, grid=(S // TS,), out_shape=jax.ShapeDtypeStruct((S, e_c), BF16),
        in_specs=[pl.BlockSpec((TS, 2 * e_c), lambda i: (i, 0)), _halo_before(e_c, 0),
                  pl.BlockSpec((NDEV, 4, G_C // NDEV, G_C), lambda i: (0, 0, 0, 0)),
                  pl.BlockSpec((None, 1, e_c), lambda i: (wl, 0, 0))] + [ANY] * len(after),
        out_specs=pl.BlockSpec((TS, e_c), lambda i: (i, 0)),
        scratch_shapes=[pltpu.VMEM((TS, G_C), BF16), pltpu.VMEM((TS, G_C), F32)],
        compiler_params=_params(("arbitrary",)),
    )(proj, proj, wpool, pscale4, *after)


def _pool_mix_bwd(proj, d_ycat, wpool, pscale4, wl, name):
    e_c = 4 * G_C
    nsteps = S // TS
    rb = G_C // NDEV

    def body(pj_ref, hp_ref, hz_ref, dy_ref, hdy_ref, wp_ref, ps_ref,
             dp_ref, dps_ref, dwp_ref, pooled_scr, yraw_scr, dyraw_scr, dpool_scr, acc_w):
        step = pl.program_id(0)

        @pl.when(step == 0)
        def _():
            dps_ref[...] = jnp.zeros_like(dps_ref)
            acc_w[...] = jnp.zeros_like(acc_w)

        live_before = (step > 0).astype(F32)
        live_after = (step < nsteps - 1).astype(F32)
        for g, win in enumerate(POOL_WINDOWS):
            weight = _pool_weight(wp_ref, g)
            for q in range(G_C // LANE):
                cols = slice(g * G_C + q * LANE, g * G_C + (q + 1) * LANE)
                tail = hp_ref[:, cols].astype(F32) * live_before
                for n in range(NCH):
                    rows = slice(n * CHUNK, (n + 1) * CHUNK)
                    p = pj_ref[rows, cols].astype(F32)
                    pooled_scr[rows, q * LANE:(q + 1) * LANE] = _pooled_chunk(
                        p, tail, win, _window_count(step, n, win, True)).astype(BF16)
                    tail = p[CHUNK - HALO:]
            yraw_scr[...] = jnp.dot(pooled_scr[...], weight, preferred_element_type=F32)
            for q in range(G_C // LANE):
                cols = slice(g * G_C + q * LANE, g * G_C + (q + 1) * LANE)
                local = slice(q * LANE, (q + 1) * LANE)
                scale = ps_ref[:, cols]
                acc_ps = jnp.zeros((1, LANE), F32)
                for n in range(NCH):
                    rows = slice(n * CHUNK, (n + 1) * CHUNK)
                    sz, dsz = _silu_and_grad(pj_ref[rows, e_c + cols.start:e_c + cols.stop].astype(F32))
                    dyv = dy_ref[rows, cols].astype(F32)
                    yraw = yraw_scr[rows, local]
                    dyraw_scr[rows, local] = (dyv * scale * sz).astype(BF16)
                    acc_ps += jnp.sum(dyv * yraw * sz, axis=0, keepdims=True)
                    dp_ref[rows, e_c + cols.start:e_c + cols.stop] = (dyv * yraw * scale * dsz).astype(BF16)
                dps_ref[:, cols] += acc_ps
                dyraw_scr[TS:, local] = (hdy_ref[:, cols].astype(F32) * scale * _silu(hz_ref[:, cols].astype(F32))
                                         * live_after).astype(BF16)
            dpool_scr[...] = lax.dot_general(dyraw_scr[...], weight, (((1,), (1,)), ((), ())), preferred_element_type=F32)
            acc_w[g] += lax.dot_general(pooled_scr[...], dyraw_scr[:TS, :], (((0,), (0,)), ((), ())),
                                        preferred_element_type=F32)
            for q in range(G_C // LANE):
                cols = slice(g * G_C + q * LANE, g * G_C + (q + 1) * LANE)
                local = slice(q * LANE, (q + 1) * LANE)
                for n in range(NCH):
                    rows = slice(n * CHUNK, (n + 1) * CHUNK)
                    ext = dpool_scr[n * CHUNK:(n + 1) * CHUNK + HALO, local]
                    sums = ext / _window_count(step, n, win, False)
                    shift = 1
                    while shift < win:
                        sums = sums + pltpu.roll(sums, CHUNK + HALO - shift, 0)
                        shift *= 2
                    dp_ref[rows, cols] = (sums[:CHUNK] - ext[:CHUNK]).astype(BF16)

        @pl.when(step == nsteps - 1)
        def _():
            for g in range(4):
                for d in range(NDEV):
                    dwp_ref[d % 2, d // 2, g] = acc_w[g, d * rb:(d + 1) * rb, :].astype(BF16)

    in_specs = [pl.BlockSpec((TS, 2 * e_c), lambda i: (i, 0)), _halo_before(e_c, 0), _halo_after(e_c, 1),
                pl.BlockSpec((TS, e_c), lambda i: (i, 0)), _halo_after(e_c, 0),
                pl.BlockSpec((NDEV, 4, rb, G_C), lambda i: (0, 0, 0, 0)),
                pl.BlockSpec((None, 1, e_c), lambda i: (wl, 0, 0))]
    args = [proj, proj, proj, d_ycat, d_ycat, wpool, pscale4]
    return pl.pallas_call(
        body, name=name, grid=(nsteps,),
        out_shape=[jax.ShapeDtypeStruct((S, 2 * e_c), BF16), jax.ShapeDtypeStruct((1, e_c), F32),
                   jax.ShapeDtypeStruct((2, NDEV // 2) + wpool.shape[1:], BF16)],
        in_specs=in_specs,
        out_specs=[pl.BlockSpec((TS, 2 * e_c), lambda i: (i, 0)), pl.BlockSpec((1, e_c), lambda i: (0, 0)),
                   pl.BlockSpec((2, NDEV // 2, 4, rb, G_C), lambda i: (0, 0, 0, 0, 0))],
        scratch_shapes=[pltpu.VMEM((TS, G_C), BF16), pltpu.VMEM((TS, G_C), F32), pltpu.VMEM((TS + HALO, G_C), BF16),
                        pltpu.VMEM((TS + HALO, G_C), F32), pltpu.VMEM((4, G_C, G_C), F32)],
        compiler_params=_params(("arbitrary",)),
    )(*args)


def _adamw(w, g, m, v):
    m = ADAM_B1 * m + (1.0 - ADAM_B1) * g
    v = ADAM_B2 * v + (1.0 - ADAM_B2) * jnp.square(g)
    m_hat = m / (1.0 - ADAM_B1 ** ADAM_STEP)
    v_hat = v / (1.0 - ADAM_B2 ** ADAM_STEP)
    delta = -ADAM_LR * (m_hat / (jnp.sqrt(v_hat) + ADAM_EPS) + ADAM_WD * w)
    return delta, m, v


def _adam_sharded(w, m, v, chip_parts, landed, my_chip, carried, name, first=0, into=()):
    _, nr, ncol = w.shape
    nl = len(chip_parts)
    tr = 128
    steps = nr // tr
    nc, ni = len(carried), len(into)

    def body(chip_ref, w_ref, m_ref, v_ref, *rest):
        parts, zones = rest[:nl], rest[nl:2 * nl]
        g_ref, d_ref, nm_ref, nv_ref = rest[2 * nl + nc + ni:2 * nl + nc + ni + 4]
        layer = pl.program_id(0)
        g = jnp.zeros((tr, ncol), F32)
        for l in range(nl):
            gl = parts[l][...].astype(F32)
            for q in range(3):
                gl = gl + zones[l][q].astype(F32)
            g = jnp.where(layer == l, gl, g)
        g_ref[...] = g
        d_ref[...], nm_ref[...], nv_ref[...] = _adamw(w_ref[...], g, m_ref[...], v_ref[...])

    def rows_of(l):
        return lambda layer, i, chip_ref: jnp.where(layer == l, i, jnp.where(layer < l, 0, steps - 1))

    spec = pl.BlockSpec((None, tr, ncol), lambda layer, i, chip_ref: (first + layer, i, 0))
    part_specs = [pl.BlockSpec((None, tr, ncol), lambda layer, i, chip_ref, l=l: (chip_ref[0], rows_of(l)(layer, i, chip_ref), 0))
                  for l in range(nl)]
    zone_specs = [pl.BlockSpec((3, tr, ncol), lambda layer, i, chip_ref, l=l: (0, rows_of(l)(layer, i, chip_ref), 0))
                  for l in range(nl)]
    grid_spec = pltpu.PrefetchScalarGridSpec(
        num_scalar_prefetch=1, grid=(nl, steps),
        in_specs=[spec, spec, spec] + part_specs + zone_specs + [ANY] * (nc + ni), out_specs=[spec] * 4 + [ANY] * nc)
    aliases = {4 + 2 * nl + k: 4 + k for k in range(nc)}
    aliases.update({4 + 2 * nl + nc + k: k for k in range(ni)})
    return pl.pallas_call(
        body, name=name, grid_spec=grid_spec,
        out_shape=[jax.ShapeDtypeStruct(w.shape, F32)] * 4 + [jax.ShapeDtypeStruct(a.shape, a.dtype) for a in carried],
        input_output_aliases=aliases, compiler_params=_params(("arbitrary", "arbitrary")),
    )(my_chip, w, m, v, *chip_parts, *landed, *carried, *into)


def _adam_small(w, g, m, v, name):
    def body(w_ref, g_ref, m_ref, v_ref, d_ref, nm_ref, nv_ref):
        d_ref[...], nm_ref[...], nv_ref[...] = _adamw(w_ref[...], g_ref[...], m_ref[...], v_ref[...])

    return pl.pallas_call(body, name=name, out_shape=[jax.ShapeDtypeStruct(w.shape, F32)] * 3,
                          in_specs=[VMEM_FULL] * 4, out_specs=[VMEM_FULL] * 3, compiler_params=_params())(w, g, m, v)


def _sum_devices(gathered, name):
    _, nr, ncol = gathered.shape

    def body(g_ref, o_ref):
        acc = g_ref[0].astype(F32)
        for s in range(1, NDEV):
            acc = acc + g_ref[s].astype(F32)
        o_ref[...] = acc

    return pl.pallas_call(body, name=name, grid=(1,), out_shape=jax.ShapeDtypeStruct((nr, ncol), F32),
                          in_specs=[pl.BlockSpec((NDEV, nr, ncol), lambda i: (0, 0, 0))],
                          out_specs=pl.BlockSpec((nr, ncol), lambda i: (0, 0)),
                          compiler_params=_params(("arbitrary",)))(gathered)


def _ada_weight_adam(cact_t, dmod_mine, w, m, v):
    def body(ct_ref, dm_ref, w_ref, m_ref, v_ref, g_ref, d_ref, nm_ref, nv_ref):
        ct, dm = ct_ref[...], dm_ref[...]
        g = ct[:, 0:1] * dm[0:1, :]
        for e in range(1, NDEV):
            g = g + ct[:, e:e + 1] * dm[e:e + 1, :]
        g_ref[...] = g
        d_ref[...], nm_ref[...], nv_ref[...] = _adamw(w_ref[...], g, m_ref[...], v_ref[...])

    spec = pl.BlockSpec((None, D, ADA_NC), lambda l: (l, 0, 0))
    return pl.pallas_call(
        body, name="ada_weight_adam", grid=(DEPTH,), out_shape=[jax.ShapeDtypeStruct(w.shape, F32)] * 4,
        in_specs=[pl.BlockSpec((D, NDEV), lambda l: (0, 0)), pl.BlockSpec((None, NDEV, ADA_NC), lambda l: (l, 0, 0)),
                  spec, spec, spec],
        out_specs=[spec] * 4, compiler_params=_params(("arbitrary",)),
    )(cact_t, dmod_mine, w, m, v)


def _pad_rows(a, rows):
    a = a.reshape(-1, D)
    return jnp.pad(a, ((0, rows - a.shape[0]), (0, 0)))


def kernel(x, c, norm_g, ada_w, ada_b, ab_w_in, ab_conv_w, ab_ln_g, ab_ln_b, ab_sgu_w, ab_sgu_b, ab_w_out, c_w_in, c_pool_w, c_pool_scale, c_w_out, final_g, loss_target, m_norm_g, m_ada_w, m_ada_b, m_ab_w_in, m_ab_conv_w, m_ab_ln_g, m_ab_ln_b, m_ab_sgu_w, m_ab_sgu_b, m_ab_w_out, m_c_w_in, m_c_pool_w, m_c_pool_scale, m_c_w_out, m_final_g, v_norm_g, v_ada_w, v_ada_b, v_ab_w_in, v_ab_conv_w, v_ab_ln_g, v_ab_ln_b, v_ab_sgu_w, v_ab_sgu_b, v_ab_w_out, v_c_w_in, v_c_pool_w, v_c_pool_scale, v_c_w_out, v_final_g):
    x_pos, y_pos, c_pos = _position()
    me = _index((x_pos, y_pos, c_pos))
    core = c_pos.astype(jnp.int32).reshape(1)
    my_chip = (2 * x_pos + y_pos).astype(jnp.int32).reshape(1)
    me1 = me.astype(jnp.int32).reshape(1)
    x0 = x.reshape(S, D)
    target = loss_target.reshape(S, D)
    norm_g3 = norm_g.reshape(DEPTH, 1, D)
    ln_g3, ln_b3 = ab_ln_g.reshape(2, 1, E_A), ab_ln_b.reshape(2, 1, E_A)
    sgu_bcol = ab_sgu_b.reshape(2, NDEV, CHUNK, 1)
    rb = G_C // NDEV
    pool_w3, m_pool_w3, v_pool_w3 = (a.reshape(2, 4 * rb, G_C) for a in (c_pool_w, m_c_pool_w, v_c_pool_w))

    cact_all, mod = _ada_forward(c, ada_w, ada_b)
    convw_all, pscale_all = _gather([ab_conv_w, c_pool_scale], "gather_small_weights")
    convw = jnp.transpose(convw_all, (1, 2, 0, 3)).reshape(2, 3, E_A)
    pscale4 = jnp.transpose(pscale_all, (1, 0, 2)).reshape(2, 1, 4 * G_C)
    zones = []
    for layer in range(DEPTH):
        wl = layer // 2
        if layer % 2 == 0:
            zones.append([_to_zone(ab_w_in, wl, me1, BF16, f"cast_w_in_{layer}"), _to_zone(ab_w_out, wl, me1, BF16, f"cast_w_out_{layer}")])
        else:
            zones.append([_to_zone(c_w_in, wl, me1, BF16, f"cast_w_in_{layer}"), _to_zone(c_w_out, wl, me1, BF16, f"cast_w_out_{layer}"),
                          _to_zone(pool_w3, wl, me1, BF16, f"cast_pool_w_{layer}")])

    def gathered(flight, after, layer):
        wg = _gather_end(flight, [after], f"gather_end_{layer}")
        return [wg[0], wg[1].reshape(-1, D)] + [w.reshape(NDEV, 4, rb, G_C) for w in wg[2:]]

    flight, (mod,) = _gather_start(zones[0], [convw_all], [mod], "gather_start_0")
    flight, (mod,) = _gather_mid(flight, [], [mod], "gather_mid_0")
    next_flight, (mod,) = _gather_start(zones[1], [], [mod], "gather_start_1")
    flight = _gather_late(flight, [mod], "gather_late_0")
    xs, hts, projs, ycats, outs, gathered_w = [x0], [], [], [], [], [gathered(flight, mod, 0)]
    for layer in range(DEPTH):
        wl = layer // 2
        even = layer % 2 == 0
        wg = gathered_w[layer]
        h_t, proj = _norm_proj(xs[-1], mod, norm_g3, wg[0], layer, f"norm_proj_{layer}")
        if layer + 1 < DEPTH:
            flight, (h_t,) = _gather_mid(next_flight, [], [h_t], f"gather_mid_{layer + 1}")
            if layer + 2 < DEPTH:
                next_flight, (h_t,) = _gather_start(zones[layer + 2], [], [h_t], f"gather_start_{layer + 2}")
        if even:
            ycat = _even_mix_fwd(proj, convw, ln_g3, ln_b3, ab_sgu_w, sgu_bcol, wl, [h_t], f"even_mix_fwd_{layer}")
        else:
            ycat = _pool_mix_fwd(proj, wg[2], pscale4, wl, [h_t], f"pool_mix_fwd_{layer}")
        if layer + 1 < DEPTH:
            flight = _gather_late(flight, [ycat], f"gather_late_{layer + 1}")
        x_new, out = _out_proj(ycat, wg[1], xs[-1], mod, layer, f"out_proj_{layer}")
        if layer + 1 < DEPTH:
            gathered_w.append(gathered(flight, x_new, layer + 1))
        xs.append(x_new)
        hts.append(h_t)
        projs.append(proj)
        ycats.append(ycat)
        outs.append(out)

    dx, loss_part, d_final_g = _final_loss(xs[DEPTH], target, final_g.reshape(1, D))

    d_mod, d_norm_g = [None] * DEPTH, [None] * DEPTH
    small, scatters, landed, res = {}, {}, {}, {}

    def finish_scatter(layer, after):
        send_sems, recv_sems, chip_parts, zones = scatters[layer]
        landed[layer] = _scatter_end(chip_parts, zones, send_sems, recv_sems, after, f"scatter_end_{layer}")

    def flat(a):
        return a.reshape(a.shape[0], -1, a.shape[-1])

    def sharded_adam(k, j, layers, w, m, v, carried, first=0, into=()):
        outs4 = _adam_sharded(w, m, v, [flat(landed[l][0][j]) for l in layers], [flat(landed[l][1][j]) for l in layers],
                              my_chip, carried, f"adam_{k}_{first}" if len(layers) < w.shape[0] else "adam_" + k, first, into)
        res[k] = [o.reshape(c_pool_w.shape) if k == "c_pool_w" else o for o in outs4[:4]]
        return list(outs4[4:])

    previous = None
    for layer in reversed(range(DEPTH)):
        wl = layer // 2
        even = layer % 2 == 0
        wg = gathered_w[layer]
        carried = [] if previous is None else [scatters[previous][2][0]]
        d_ycat, grad_out, d_gate, carried = _out_bwd(dx, outs[layer], ycats[layer], wg[1], mod, layer, carried, f"out_bwd_{layer}")
        if previous is not None:
            scatters[previous][2][0] = carried[0]
        parts = [None, grad_out]
        if even:
            d_proj, d_cw, d_lg, d_lb, d_sw, d_sb = _even_mix_bwd(
                projs[layer], d_ycat, convw, ln_g3, ln_b3, ab_sgu_w, sgu_bcol, wl, f"even_mix_bwd_{layer}")
            small[layer] = (d_cw, d_lg, d_lb, d_sw, d_sb)
        else:
            d_proj, d_ps, d_pool = _pool_mix_bwd(projs[layer], d_ycat, wg[2], pscale4, wl, f"pool_mix_bwd_{layer}")
            small[layer] = (d_ps,)
            parts.append(d_pool)
        parts[0] = _weight_grad(hts[layer], d_proj, f"grad_w_in_{layer}")
        pair_send, pair_recv, parts, from_sibling = _pair_start(parts, f"pair_start_{layer}")
        if layer > 0:
            dx, d_shift, d_scale, d_norm_g[layer], parts[0] = _dh_norm_bwd(
                d_proj, wg[0], xs[layer], dx, mod, norm_g3, layer, parts[0], f"dh_norm_bwd_{layer}")
            pair_after = dx
        else:
            finish_scatter(1, d_proj)
            finish_scatter(3, d_proj)
            parts[0], = sharded_adam("c_w_out", 1, (1, 3), c_w_out, m_c_w_out, v_c_w_out, [parts[0]])
            parts[0], = sharded_adam("c_pool_w", 2, (1, 3), pool_w3, m_pool_w3, v_pool_w3, [parts[0]])
            pair_after = res["c_pool_w"][0]
        parts, from_sibling = _pair_end(parts, from_sibling, pair_send, pair_recv, pair_after, f"pair_end_{layer}")
        chip_parts = [_pair_sum(p, q, core, f"pair_sum_{layer}_{j}") for j, (p, q) in enumerate(zip(parts, from_sibling))]
        send_sems, recv_sems, chip_parts, zones = _scatter_start(chip_parts, f"scatter_start_{layer}")
        if layer == 0:
            chip_parts[0], = sharded_adam("c_w_in", 0, (1, 3), c_w_in, m_c_w_in, v_c_w_in, [chip_parts[0]])
            dx, d_shift, d_scale, d_norm_g[layer], chip_parts[0] = _dh_norm_bwd(
                d_proj, wg[0], xs[layer], dx, mod, norm_g3, layer, chip_parts[0], f"dh_norm_bwd_{layer}")
        scatters[layer] = [send_sems, recv_sems, chip_parts, zones]
        previous = layer
        d_mod[layer] = jnp.concatenate([d_shift, d_scale, d_gate], axis=0)
    grad_x = dx.reshape(x.shape)

    sections = [("norm_g", jnp.concatenate(d_norm_g, axis=0), 8),
                ("d_mod", jnp.concatenate(d_mod, axis=0), 16),
                ("ab_ln_g", jnp.concatenate([small[0][1], small[2][1]], axis=0), 8),
                ("ab_ln_b", jnp.concatenate([small[0][2], small[2][2]], axis=0), 8),
                ("ab_sgu_b", jnp.stack([small[0][4], small[2][4]]), 8),
                ("final_g", d_final_g, 8),
                ("ab_conv_w", jnp.stack([small[0][0], small[2][0]]), 8),
                ("c_pool_scale", jnp.concatenate([small[1][0], small[3][0]], axis=0), 8),
                ("ab_sgu_w", jnp.stack([small[0][3], small[2][3]]), 256)]
    offsets, at = {}, 0
    for name, _, rows in sections:
        offsets[name] = (at, rows)
        at += rows
    packed = jnp.concatenate([_pad_rows(a, rows) for _, a, rows in sections] + [jnp.zeros((-at % 32, D), F32)], axis=0)
    loss_rows = jnp.pad(loss_part, ((0, 15), (0, D - LANE)))
    small_zones = [_to_zone(packed[None], 0, me1, BF16, "place_small_grads"), _to_zone(loss_rows[None], 0, me1, F32, "place_loss")]
    small_flight, (mod,) = _gather_start(small_zones, [], [mod], "gather_small_start")

    finish_scatter(2, mod)
    sharded_adam("ab_w_out", 1, (2,), ab_w_out, m_ab_w_out, v_ab_w_out, [], first=1)
    sharded_adam("ab_w_in", 0, (2,), ab_w_in, m_ab_w_in, v_ab_w_in, [], first=1)
    finish_scatter(0, res["ab_w_in"][0])
    sharded_adam("ab_w_out", 1, (0,), ab_w_out, m_ab_w_out, v_ab_w_out, [], into=res["ab_w_out"])
    sharded_adam("ab_w_in", 0, (0,), ab_w_in, m_ab_w_in, v_ab_w_in, [], into=res["ab_w_in"])

    last = res["ab_w_in"][0]
    small_flight, _ = _gather_mid(small_flight, [last], [], "gather_small_mid")
    small_flight = _gather_late(small_flight, [last], "gather_small_late")
    small_grads, losses = _gather_end(small_flight, [last], "gather_small_end")
    summed = _sum_devices(small_grads, "sum_small_grads")
    loss = _sum_devices(losses, "sum_loss")[0, 0]

    def section(name, nrows, src=summed):
        start = offsets[name][0]
        return src[..., start:start + nrows, :]

    grads = {
        "norm_g": section("norm_g", DEPTH),
        "ada_b": section("d_mod", 3 * DEPTH).reshape(DEPTH, 3 * D),
        "ab_ln_g": section("ab_ln_g", 2), "ab_ln_b": section("ab_ln_b", 2),
        "ab_sgu_b": section("ab_sgu_b", 2).reshape(ab_sgu_b.shape),
        "final_g": section("final_g", 1),
        "ab_sgu_w": section("ab_sgu_w", 256).reshape(ab_sgu_w.shape),
        "ab_conv_w": lax.dynamic_slice_in_dim(section("ab_conv_w", 6).reshape(2, 3, E_A), me * HEAD, HEAD, axis=2),
        "c_pool_scale": lax.dynamic_slice_in_dim(section("c_pool_scale", 4).reshape(2, 4 * G_C), me * 256, 256, axis=1),
    }
    small_w = {"norm_g": (norm_g, m_norm_g, v_norm_g), "ada_b": (ada_b, m_ada_b, v_ada_b),
               "ab_ln_g": (ab_ln_g, m_ab_ln_g, v_ab_ln_g), "ab_ln_b": (ab_ln_b, m_ab_ln_b, v_ab_ln_b),
               "ab_sgu_b": (ab_sgu_b, m_ab_sgu_b, v_ab_sgu_b),
               "final_g": (final_g.reshape(1, D), m_final_g.reshape(1, D), v_final_g.reshape(1, D)),
               "ab_sgu_w": (ab_sgu_w, m_ab_sgu_w, v_ab_sgu_w), "ab_conv_w": (ab_conv_w, m_ab_conv_w, v_ab_conv_w),
               "c_pool_scale": (c_pool_scale, m_c_pool_scale, v_c_pool_scale)}
    for k, (w, m, v) in small_w.items():
        res[k] = [grads[k]] + list(_adam_small(w, grads[k], m, v, "adam_" + k))
    res["final_g"] = [a.reshape(D) for a in res["final_g"]]

    dmod_all = section("d_mod", 3 * DEPTH, small_grads).reshape(NDEV, DEPTH, 3 * D)
    dmod_mine = jnp.transpose(lax.dynamic_slice_in_dim(dmod_all, me * ADA_NC, ADA_NC, axis=2), (1, 0, 2)).astype(F32)
    res["ada_w"] = _ada_weight_adam(jnp.transpose(cact_all.reshape(NDEV, D)), dmod_mine, ada_w, m_ada_w, v_ada_w)

    order = ["norm_g", "ada_w", "ada_b", "ab_w_in", "ab_conv_w", "ab_ln_g", "ab_ln_b", "ab_sgu_w", "ab_sgu_b",
             "ab_w_out", "c_w_in", "c_pool_w", "c_pool_scale", "c_w_out", "final_g"]
    return (loss, grad_x, *[res[k][0] for k in order], *[res[k][1] for k in order],
            *[res[k][2] for k in order], *[res[k][3] for k in order])
```

```python
import jax
import jax.numpy as jnp
from jax import lax
from jax.experimental import pallas as pl
from jax.experimental.pallas import tpu as pltpu

F32, BF16 = jnp.float32, jnp.bfloat16
S, D = 2048, 1024
NDEV = 8
DEPTH = 4
EPS = 1e-6
E_A = 1024
HEAD = 128
CHUNK = 128
POOL_WINDOWS = (2, 4, 8, 16)
G_C = 512
HALO = 16
ADA_NC = 384
MIB = 1024 * 1024
LANE = 128

ADAM_LR, ADAM_B1, ADAM_B2, ADAM_EPS, ADAM_WD, ADAM_STEP = 0.001, 0.9, 0.999, 1e-08, 0.01, 10

ANY = pl.BlockSpec(memory_space=pl.ANY)
VMEM_FULL = pl.BlockSpec(memory_space=pltpu.VMEM)
IN_HBM = pl.BlockSpec(memory_space=pltpu.HBM)
SEMAPHORES = pl.BlockSpec(memory_space=pltpu.SEMAPHORE)
IN_FLIGHT = pltpu.SideEffectType.DATAFLOW_SIDE_EFFECTING


V7X_VMEM_MIB = 64
VMEM_LIMIT_MIB = V7X_VMEM_MIB - 4


def _params(semantics=None):
    return pltpu.CompilerParams(dimension_semantics=semantics, vmem_limit_bytes=VMEM_LIMIT_MIB * MIB)


def _silu(z):
    return z * jax.nn.sigmoid(z)


def _silu_and_grad(z):
    sig = jax.nn.sigmoid(z)
    return z * sig, sig * (1.0 + z * (1.0 - sig))


def _position():
    return lax.axis_index("x"), lax.axis_index("y"), lax.axis_index("c")


def _index(pos):
    return 4 * pos[0] + 2 * pos[1] + pos[2]


def _peer(pos, k):
    flipped = tuple(1 - p if (k >> (2 - b)) & 1 else p for b, p in enumerate(pos))
    return flipped, _index(flipped)


def _remote(src, dst, send_sem, recv_sem, device):
    return pltpu.make_async_remote_copy(src_ref=src, dst_ref=dst, send_sem=send_sem, recv_sem=recv_sem,
                                        device_id=device, device_id_type=pl.DeviceIdType.MESH)


def _gather(arrays, name):
    n = len(arrays)
    out_shape = [jax.ShapeDtypeStruct((NDEV,) + a.shape, a.dtype) for a in arrays]

    def body(*refs):
        ins, outs = refs[:n], refs[n:2 * n]
        send_sems, recv_sems, own_sems = refs[2 * n:]
        x, y, c = _position()
        me = _index((x, y, c))
        sibling = (x, y, 1 - c)
        chips = [(1 - x, y), (x, 1 - y), (1 - x, 1 - y)]

        def block_copy(j, k, owner, to, src=None):
            rows = outs[j].at[_index(owner)]
            return _remote(rows if src is None else src, rows, send_sems.at[j, k], recv_sems.at[j, k], to)

        own, first, passed = [], [], []
        for j in range(n):
            own.append(pltpu.make_async_copy(ins[j], outs[j].at[me], own_sems.at[j]))
            first.append(block_copy(j, 0, (x, y, c), sibling, src=ins[j]))
            first += [block_copy(j, 1 + q, (x, y, c), (*chip, c), src=ins[j]) for q, chip in enumerate(chips)]
        for copy in own + first:
            copy.start()
        for q, chip in enumerate(chips):
            for j in range(n):
                block_copy(j, 1 + q, (*chip, c), (x, y, c)).wait_recv()
                forward = block_copy(j, 4 + q, (*chip, c), sibling)
                forward.start()
                passed.append(forward)
        for j in range(n):
            block_copy(j, 0, sibling, (x, y, c)).wait_recv()
            for q, chip in enumerate(chips):
                block_copy(j, 4 + q, (*chip, 1 - c), (x, y, c)).wait_recv()
        for copy in first + passed:
            copy.wait_send()
        for copy in own:
            copy.wait()

    return pl.pallas_call(
        body, name=name, out_shape=out_shape, in_specs=[ANY] * n, out_specs=[ANY] * n,
        scratch_shapes=[pltpu.SemaphoreType.DMA((n, NDEV - 1)), pltpu.SemaphoreType.DMA((n, NDEV - 1)),
                        pltpu.SemaphoreType.DMA((n,))],
    )(*arrays)


def _pair_start(parts, name):
    n = len(parts)
    lands = [_in_hbm(lax.empty(p.shape[1:], p.dtype)) for p in parts]

    def body(*refs):
        ins, zones = refs[:n], refs[n:2 * n]
        send_sems, recv_sems = refs[2 * n:2 * n + 2]
        x, y, c = _position()
        for j in range(n):
            _remote(ins[j].at[1 - c], zones[j], send_sems.at[j], recv_sems.at[j], (x, y, 1 - c)).start()

    outs = pl.pallas_call(
        body, name=name,
        out_shape=(pltpu.SemaphoreType.DMA((n,)), pltpu.SemaphoreType.DMA((n,)),
                   *[pltpu.HBM(p.shape, p.dtype) for p in parts], *[pltpu.HBM(z.shape, z.dtype) for z in lands]),
        in_specs=[IN_HBM] * (2 * n), out_specs=(SEMAPHORES, SEMAPHORES, *[IN_HBM] * (2 * n)),
        input_output_aliases={j: 2 + j for j in range(2 * n)},
        compiler_params=pltpu.CompilerParams(has_side_effects=IN_FLIGHT),
    )(*[_in_hbm(p) for p in parts], *lands)
    return outs[0], outs[1], list(outs[2:2 + n]), list(outs[2 + n:])


def _pair_end(parts, zones, send_sems, recv_sems, after, name):
    n = len(parts)

    def body(*refs):
        ins, zs = refs[:n], refs[n:2 * n]
        s, r = refs[2 * n:2 * n + 2]
        me = _position()
        for j in range(n):
            copy = _remote(ins[j].at[0], zs[j], s.at[j], r.at[j], me)
            copy.wait_send()
            copy.wait_recv()

    outs = pl.pallas_call(
        body, name=name,
        out_shape=(*[pltpu.HBM(p.shape, p.dtype) for p in parts], *[pltpu.HBM(z.shape, z.dtype) for z in zones]),
        in_specs=[IN_HBM] * (2 * n) + [SEMAPHORES, SEMAPHORES, ANY], out_specs=tuple([IN_HBM] * (2 * n)),
        input_output_aliases={j: j for j in range(2 * n)},
        compiler_params=pltpu.CompilerParams(has_side_effects=IN_FLIGHT),
    )(*parts, *zones, send_sems, recv_sems, after)
    return list(outs[:n]), list(outs[n:])


def _pair_sum(part, from_sibling, core, name):
    ncol = part.shape[-1]
    p3 = part.reshape(2, -1, ncol)
    q2 = from_sibling.reshape(-1, ncol)
    nrows = q2.shape[0]
    tr = 512

    def body(core_ref, p_ref, q_ref, o_ref):
        o_ref[...] = (p_ref[...].astype(F32) + q_ref[...].astype(F32)).astype(BF16)

    grid_spec = pltpu.PrefetchScalarGridSpec(
        num_scalar_prefetch=1, grid=(nrows // tr,),
        in_specs=[pl.BlockSpec((None, tr, ncol), lambda i, core_ref: (core_ref[0], i, 0)),
                  pl.BlockSpec((tr, ncol), lambda i, core_ref: (i, 0))],
        out_specs=pl.BlockSpec((tr, ncol), lambda i, core_ref: (i, 0)))
    out = pl.pallas_call(body, name=name, grid_spec=grid_spec, out_shape=jax.ShapeDtypeStruct(q2.shape, BF16),
                         compiler_params=_params(("arbitrary",)))(core, p3, q2)
    return out.reshape(from_sibling.shape)


def _in_hbm(a):
    return pltpu.with_memory_space_constraint(a, pltpu.HBM)


def _chips(x, y):
    return [(1 - x, y), (x, 1 - y), (1 - x, 1 - y)]


def _to_zone(a, wl, me, dtype, name):
    _, rows, cols = a.shape
    tr = 256 if rows % 256 == 0 else rows

    def body(me_ref, a_ref, o_ref):
        o_ref[...] = a_ref[...].astype(dtype)

    grid_spec = pltpu.PrefetchScalarGridSpec(
        num_scalar_prefetch=1, grid=(rows // tr,),
        in_specs=[pl.BlockSpec((None, tr, cols), lambda i, me_ref: (wl, i, 0))],
        out_specs=pl.BlockSpec((None, tr, cols), lambda i, me_ref: (me_ref[0], i, 0)))
    return pl.pallas_call(body, name=name, grid_spec=grid_spec, out_shape=jax.ShapeDtypeStruct((NDEV, rows, cols), dtype),
                          compiler_params=_params(("arbitrary",)))(me, a)


def _halves(block):
    rows = block.shape[0] // 2
    return block.at[pl.ds(0, rows)], block.at[pl.ds(rows, rows)]


def _around(x, y, c):
    return (x, y, 1 - c), (1 - x, y, c), (x, 1 - y, c), (1 - x, 1 - y, c)


def _gather_step1(zs, send, recv, pos):
    sibling, xn, yn, _ = _around(*pos)
    for j, z in enumerate(zs):
        mine = z.at[_index(pos)]
        for k, peer in enumerate((sibling, xn, yn)):
            _remote(mine, mine, send.at[3 * j + k], recv.at[3 * j + k], peer).start()


def _gather_step2(zs, recv1, send, recv, pos):
    sibling, xn, yn, _ = _around(*pos)
    for j, z in enumerate(zs):
        xb, yb = z.at[_index(xn)], z.at[_index(yn)]
        _remote(xb, xb, send.at[4 * j], recv1.at[3 * j + 1], pos).wait_recv()
        _remote(yb, yb, send.at[4 * j], recv1.at[3 * j + 2], pos).wait_recv()
        _remote(xb, xb, send.at[4 * j], recv.at[4 * j], sibling).start()
        _remote(yb, yb, send.at[4 * j + 1], recv.at[4 * j + 1], sibling).start()
        first, second = _halves(xb)[0], _halves(yb)[1]
        _remote(first, first, send.at[4 * j + 2], recv.at[4 * j + 2], yn).start()
        _remote(second, second, send.at[4 * j + 3], recv.at[4 * j + 3], xn).start()


def _gather_step3(zs, recv2, send, recv, pos):
    sibling, _, _, diagonal = _around(*pos)
    for j, z in enumerate(zs):
        db = z.at[_index(diagonal)]
        first, second = _halves(db)
        _remote(first, first, send.at[j], recv2.at[4 * j + 2], pos).wait_recv()
        _remote(second, second, send.at[j], recv2.at[4 * j + 3], pos).wait_recv()
        _remote(db, db, send.at[j], recv.at[j], sibling).start()


def _gather_step4(zs, send1, recv1, send2, recv2, send3, recv3, pos):
    x, y, c = pos
    sibling = (x, y, 1 - c)
    _, sx, sy, sd = _around(*sibling)
    for j, z in enumerate(zs):
        for owner, send, recv, k in ((sibling, send1, recv1, 3 * j), (sx, send2, recv2, 4 * j), (sy, send2, recv2, 4 * j + 1),
                                     (sd, send3, recv3, j)):
            block = z.at[_index(owner)]
            _remote(block, block, send.at[k], recv.at[k], pos).wait_recv()
    for j, z in enumerate(zs):
        block = z.at[0]
        half = _halves(block)[0]
        for ref, send, recv, k in ([(block, send1, recv1, 3 * j + k) for k in range(3)]
                                   + [(block, send2, recv2, 4 * j), (block, send2, recv2, 4 * j + 1),
                                      (half, send2, recv2, 4 * j + 2), (half, send2, recv2, 4 * j + 3), (block, send3, recv3, j)]):
            _remote(ref, ref, send.at[k], recv.at[k], pos).wait_send()


def _flight_call(step, name, zones, sems_in, nsems_out, after, carried):
    n, m, k = len(zones), len(carried), len(sems_in)

    def body(*refs):
        zs = refs[:n]
        given = refs[n + m:n + m + k]
        made = refs[n + m + k + len(after):n + m + k + len(after) + (2 if nsems_out else 0)]
        step(zs, *given, *made, _position())

    sem_out = (pltpu.SemaphoreType.DMA((nsems_out,)),) * 2 if nsems_out else ()
    outs = pl.pallas_call(
        body, name=name,
        out_shape=(*sem_out, *[pltpu.HBM(z.shape, z.dtype) for z in zones], *[jax.ShapeDtypeStruct(a.shape, a.dtype) for a in carried]),
        in_specs=[IN_HBM] * n + [ANY] * m + [SEMAPHORES] * k + [ANY] * len(after),
        out_specs=(*[SEMAPHORES] * len(sem_out), *[IN_HBM] * n, *[ANY] * m),
        input_output_aliases={j: len(sem_out) + j for j in range(n + m)},
        compiler_params=pltpu.CompilerParams(has_side_effects=IN_FLIGHT),
    )(*[_in_hbm(z) for z in zones], *carried, *sems_in, *after)
    sems = list(outs[:len(sem_out)])
    return sems, list(outs[len(sem_out):len(sem_out) + n]), list(outs[len(sem_out) + n:])


def _gather_start(zones, after, carried, name):
    (send1, recv1), zones, carried = _flight_call(_gather_step1, name, zones, [], 3 * len(zones), after, carried)
    return {"s1": send1, "r1": recv1, "zones": zones}, carried


def _gather_mid(flight, after, carried, name):
    step = lambda zs, recv1, send, recv, pos: _gather_step2(zs, recv1, send, recv, pos)
    (send2, recv2), zones, carried = _flight_call(step, name, flight["zones"], [flight["r1"]], 4 * len(flight["zones"]), after, carried)
    return {**flight, "s2": send2, "r2": recv2, "zones": zones}, carried


def _gather_late(flight, after, name):
    step = lambda zs, recv2, send, recv, pos: _gather_step3(zs, recv2, send, recv, pos)
    (send3, recv3), zones, _ = _flight_call(step, name, flight["zones"], [flight["r2"]], len(flight["zones"]), after, [])
    return {**flight, "s3": send3, "r3": recv3, "zones": zones}


def _gather_end(flight, after, name):
    sems = [flight[k] for k in ("s1", "r1", "s2", "r2", "s3", "r3")]
    _, zones, _ = _flight_call(_gather_step4, name, flight["zones"], sems, 0, after, [])
    return zones


def _scatter_start(parts, name):
    n = len(parts)
    lands = [_in_hbm(lax.empty((3,) + p.shape[1:], p.dtype)) for p in parts]

    def body(*refs):
        ins, zones = refs[:n], refs[n:2 * n]
        send_sems, recv_sems = refs[2 * n:2 * n + 2]
        x, y, c = _position()
        for j in range(n):
            for q, (px, py) in enumerate(_chips(x, y)):
                _remote(ins[j].at[2 * px + py], zones[j].at[q], send_sems.at[3 * j + q], recv_sems.at[3 * j + q],
                        (px, py, c)).start()

    outs = pl.pallas_call(
        body, name=name,
        out_shape=(pltpu.SemaphoreType.DMA((3 * n,)), pltpu.SemaphoreType.DMA((3 * n,)),
                   *[pltpu.HBM(p.shape, p.dtype) for p in parts], *[pltpu.HBM(z.shape, z.dtype) for z in lands]),
        in_specs=[IN_HBM] * (2 * n), out_specs=(SEMAPHORES, SEMAPHORES, *[IN_HBM] * (2 * n)),
        input_output_aliases={j: 2 + j for j in range(2 * n)},
        compiler_params=pltpu.CompilerParams(has_side_effects=IN_FLIGHT),
    )(*[_in_hbm(p) for p in parts], *lands)
    return outs[0], outs[1], list(outs[2:2 + n]), list(outs[2 + n:])


def _scatter_end(parts, zones, send_sems, recv_sems, after, name):
    n = len(parts)

    def body(*refs):
        ins, zs = refs[:n], refs[n:2 * n]
        s, r = refs[2 * n:2 * n + 2]
        me = _position()
        for j in range(n):
            for q in range(3):
                copy = _remote(ins[j].at[0], zs[j].at[q], s.at[3 * j + q], r.at[3 * j + q], me)
                copy.wait_send()
                copy.wait_recv()

    outs = pl.pallas_call(
        body, name=name,
        out_shape=(*[pltpu.HBM(p.shape, p.dtype) for p in parts], *[pltpu.HBM(z.shape, z.dtype) for z in zones]),
        in_specs=[IN_HBM] * (2 * n) + [SEMAPHORES, SEMAPHORES, ANY], out_specs=tuple([IN_HBM] * (2 * n)),
        input_output_aliases={j: j for j in range(2 * n)},
        compiler_params=pltpu.CompilerParams(has_side_effects=IN_FLIGHT),
    )(*parts, *zones, send_sems, recv_sems, after)
    return list(outs[:n]), list(outs[n:])


def _ada_forward(c, ada_w, ada_b):
    def body(c_ref, w_ref, b_ref, cact_ref, mod_ref, gbuf, modrow, send_sems, recv_sems):
        pos = _position()
        me = _index(pos)

        def to_all(ref, round_):
            copies = []
            for k in range(1, NDEV):
                peer, _ = _peer(pos, k)
                copy = pltpu.make_async_remote_copy(
                    src_ref=ref.at[me], dst_ref=ref.at[me], send_sem=send_sems.at[round_, k - 1],
                    recv_sem=recv_sems.at[round_, k - 1], device_id=peer, device_id_type=pl.DeviceIdType.MESH)
                copy.start()
                copies.append(copy)
            for copy in copies:
                copy.wait()

        cact_ref[me] = _silu(c_ref[...])
        to_all(cact_ref, 0)
        rows = lax.broadcasted_iota(jnp.int32, (NDEV, D), 0)
        cact = jnp.zeros((NDEV, D), F32)
        for e in range(NDEV):
            cact = jnp.where(rows == e, cact_ref[e], cact)
        cact = cact.astype(BF16)
        for l in range(DEPTH):
            gbuf[me, l] = jnp.dot(cact, w_ref[l].astype(BF16), preferred_element_type=F32)
        to_all(gbuf, 1)
        mine = lax.broadcasted_iota(jnp.int32, (NDEV, ADA_NC), 0) == me
        for l in range(DEPTH):
            for d in range(NDEV):
                modrow[:, d * ADA_NC:(d + 1) * ADA_NC] = jnp.sum(jnp.where(mine, gbuf[d, l], 0.0), axis=0, keepdims=True)
            full = modrow[...] + b_ref[l:l + 1, :]
            for w in range(3):
                mod_ref[l, w] = full[:, w * D:(w + 1) * D]

    return pl.pallas_call(
        body, name="ada_forward",
        out_shape=[jax.ShapeDtypeStruct((NDEV, 1, D), F32), jax.ShapeDtypeStruct((DEPTH, 3, 1, D), F32)],
        in_specs=[VMEM_FULL] * 3, out_specs=[VMEM_FULL] * 2,
        scratch_shapes=[pltpu.VMEM((NDEV, DEPTH, NDEV, ADA_NC), F32), pltpu.VMEM((1, 3 * D), F32),
                        pltpu.SemaphoreType.DMA((2, NDEV - 1)), pltpu.SemaphoreType.DMA((2, NDEV - 1))],
        compiler_params=_params(),
    )(c, ada_w, ada_b)


def _mod_spec(layer, which, ngrid):
    index = {1: lambda i: (layer, which, 0, 0), 2: lambda i, j: (layer, which, 0, 0)}[ngrid]
    return pl.BlockSpec((None, None, 1, D), index)


W_BLOCKS = 4


def _norm_proj(x, mod, norm_g3, wg, layer, name):
    nb = wg.shape[-1]
    tm = 1024
    wb = W_BLOCKS

    def body(x_ref, g_ref, shift_ref, scale_ref, w_ref, ht_ref, p_ref, h_ref):
        @pl.when(pl.program_id(1) == 0)
        def _():
            xv = x_ref[...]
            r = lax.rsqrt(jnp.mean(xv * xv, axis=-1, keepdims=True) + EPS)
            hn = xv * r * g_ref[...]
            h = hn * (1.0 + scale_ref[...]) + shift_ref[...]
            h_ref[...] = h.astype(BF16)
            ht_ref[...] = h.T.astype(BF16)

        hv = h_ref[...]
        for b in range(wb):
            p_ref[:, b * nb:(b + 1) * nb] = jnp.dot(hv, w_ref[b], preferred_element_type=F32).astype(BF16)

    return pl.pallas_call(
        body, name=name, grid=(S // tm, NDEV // wb),
        out_shape=[jax.ShapeDtypeStruct((D, S), BF16), jax.ShapeDtypeStruct((S, NDEV * nb), BF16)],
        in_specs=[pl.BlockSpec((tm, D), lambda i, d: (i, 0)),
                  pl.BlockSpec((None, 1, D), lambda i, d: (layer, 0, 0)),
                  _mod_spec(layer, 0, 2), _mod_spec(layer, 1, 2),
                  pl.BlockSpec((wb, D, nb), lambda i, d: (d, 0, 0))],
        out_specs=[pl.BlockSpec((D, tm), lambda i, d: (0, i)), pl.BlockSpec((tm, wb * nb), lambda i, d: (i, d))],
        scratch_shapes=[pltpu.VMEM((tm, D), BF16)],
        compiler_params=_params(("arbitrary", "arbitrary")),
    )(x, norm_g3, mod, mod, wg)


def _out_proj(ycat, w_out, x, mod, layer, name):
    tm = 512
    e = w_out.shape[0]

    def body(y_ref, w_ref, x_ref, gate_ref, xn_ref, o_ref):
        acc = jnp.dot(y_ref[...], w_ref[...], preferred_element_type=F32)
        o_ref[...] = acc.astype(BF16)
        xn_ref[...] = x_ref[...] + gate_ref[...] * acc

    return pl.pallas_call(
        body, name=name, grid=(S // tm,),
        out_shape=[jax.ShapeDtypeStruct((S, D), F32), jax.ShapeDtypeStruct((S, D), BF16)],
        in_specs=[pl.BlockSpec((tm, e), lambda i: (i, 0)), pl.BlockSpec((e, D), lambda i: (0, 0)),
                  pl.BlockSpec((tm, D), lambda i: (i, 0)), _mod_spec(layer, 2, 1)],
        out_specs=[pl.BlockSpec((tm, D), lambda i: (i, 0))] * 2,
        compiler_params=_params(("arbitrary",)),
    )(ycat, w_out, x, mod)


def _final_loss(x, target, final_g2):
    tm = 256

    def body(x_ref, t_ref, g_ref, dx_ref, loss_ref, dg_ref):
        @pl.when(pl.program_id(0) == 0)
        def _():
            loss_ref[...] = jnp.zeros_like(loss_ref)
            dg_ref[...] = jnp.zeros_like(dg_ref)

        xv, g = x_ref[...], g_ref[...]
        r = lax.rsqrt(jnp.mean(xv * xv, axis=-1, keepdims=True) + EPS)
        xn = xv * r
        err = xn * g - t_ref[...]
        loss_ref[...] += 0.5 * jnp.sum(jnp.mean(err * err, axis=-1, keepdims=True), axis=0, keepdims=True)
        dy = err * (1.0 / D)
        dg_ref[...] += jnp.sum(dy * xn, axis=0, keepdims=True)
        u = dy * g
        dx_ref[...] = r * (u - xn * jnp.mean(xn * u, axis=-1, keepdims=True))

    tile = pl.BlockSpec((tm, D), lambda i: (i, 0))
    row = pl.BlockSpec((1, D), lambda i: (0, 0))
    return pl.pallas_call(
        body, name="final_loss", grid=(S // tm,),
        out_shape=[jax.ShapeDtypeStruct((S, D), F32), jax.ShapeDtypeStruct((1, LANE), F32), jax.ShapeDtypeStruct((1, D), F32)],
        in_specs=[tile, tile, row], out_specs=[tile, pl.BlockSpec((1, LANE), lambda i: (0, 0)), row],
        compiler_params=_params(("arbitrary",)),
    )(x, target, final_g2)


def _out_bwd(dx, out, ycat, w_out, mod, layer, carried, name):
    tm = 512
    nsteps = S // tm
    e = ycat.shape[1]
    rb = e // NDEV
    nc = len(carried)

    def body(dx_ref, o_ref, y_ref, w_ref, gate_ref, *rest):
        dy_ref, gw_ref, dgate_ref = rest[nc:nc + 3]
        acc = rest[-1]
        step = pl.program_id(0)

        @pl.when(step == 0)
        def _():
            dgate_ref[...] = jnp.zeros_like(dgate_ref)
            acc[...] = jnp.zeros_like(acc)

        dxv = dx_ref[...]
        d_out = (gate_ref[...] * dxv).astype(BF16)
        dgate_ref[...] += jnp.sum(dxv * o_ref[...].astype(F32), axis=0, keepdims=True)
        dy_ref[...] = lax.dot_general(d_out, w_ref[...], (((1,), (1,)), ((), ())), preferred_element_type=F32).astype(BF16)
        acc[...] += lax.dot_general(y_ref[...], d_out, (((0,), (0,)), ((), ())), preferred_element_type=F32)

        @pl.when(step == nsteps - 1)
        def _():
            for d in range(NDEV):
                gw_ref[d % 2, d // 2] = acc[d * rb:(d + 1) * rb, :].astype(BF16)

    tile = pl.BlockSpec((tm, D), lambda i: (i, 0))
    wide = pl.BlockSpec((tm, e), lambda i: (i, 0))
    outs = pl.pallas_call(
        body, name=name, grid=(nsteps,),
        out_shape=[jax.ShapeDtypeStruct((S, e), BF16), jax.ShapeDtypeStruct((2, NDEV // 2, rb, D), BF16),
                   jax.ShapeDtypeStruct((1, D), F32)] + [jax.ShapeDtypeStruct(a.shape, a.dtype) for a in carried],
        in_specs=[tile, tile, wide, pl.BlockSpec((e, D), lambda i: (0, 0)), _mod_spec(layer, 2, 1)] + [ANY] * nc,
        out_specs=[wide, pl.BlockSpec((2, NDEV // 2, rb, D), lambda i: (0, 0, 0, 0)), pl.BlockSpec((1, D), lambda i: (0, 0))]
        + [ANY] * nc,
        scratch_shapes=[pltpu.VMEM((e, D), F32)],
        input_output_aliases={5 + k: 3 + k for k in range(nc)},
        compiler_params=_params(("arbitrary",)),
    )(dx, out, ycat, w_out, mod, *carried)
    return outs[0], outs[1], outs[2], list(outs[3:])


def _weight_grad(h_t, d_proj, name):
    nb = d_proj.shape[1] // NDEV

    def body(ht_ref, dp_ref, o_ref):
        o_ref[...] = jnp.dot(ht_ref[...], dp_ref[...], preferred_element_type=F32).astype(BF16)

    return pl.pallas_call(
        body, name=name, grid=(NDEV,), out_shape=jax.ShapeDtypeStruct((2, NDEV // 2, D, nb), BF16),
        in_specs=[pl.BlockSpec((D, S), lambda d: (0, 0)), pl.BlockSpec((S, nb), lambda d: (0, d))],
        out_specs=pl.BlockSpec((None, None, D, nb), lambda d: (d % 2, d // 2, 0, 0)),
        compiler_params=_params(("arbitrary",)),
    )(h_t, d_proj)


def _dh_norm_bwd(d_proj, wg, x, dx, mod, norm_g3, layer, carried, name):
    nb = wg.shape[-1]
    tm = 512
    wb = W_BLOCKS
    rc = 128

    def body(dp_ref, w_ref, x_ref, dx_ref, g_ref, scale_ref, carried_ref,
             dxi_ref, dshift_ref, dscale_ref, dg_ref, carried_out, acc):
        i, d = pl.program_id(0), pl.program_id(1)
        nt = (((1,), (1,)), ((), ()))
        part = lax.dot_general(dp_ref[:, :nb], w_ref[0], nt, preferred_element_type=F32)
        for b in range(1, wb):
            part += lax.dot_general(dp_ref[:, b * nb:(b + 1) * nb], w_ref[b], nt, preferred_element_type=F32)

        @pl.when(d == 0)
        def _():
            acc[...] = part

        @pl.when(d != 0)
        def _():
            acc[...] += part

        @pl.when(jnp.logical_and(i == 0, d == 0))
        def _():
            dshift_ref[...] = jnp.zeros_like(dshift_ref)
            dscale_ref[...] = jnp.zeros_like(dscale_ref)
            dg_ref[...] = jnp.zeros_like(dg_ref)

        @pl.when(d == NDEV // wb - 1)
        def _():
            g = g_ref[...]
            scale1 = 1.0 + scale_ref[...]

            def chunk(k, sums):
                rows = pl.ds(pl.multiple_of(k * rc, rc), rc)
                xv, dhv = x_ref[rows, :], acc[rows, :]
                r = lax.rsqrt(jnp.mean(xv * xv, axis=-1, keepdims=True) + EPS)
                xn = xv * r
                dhn = dhv * scale1
                u = dhn * g
                dxi_ref[rows, :] = dx_ref[rows, :] + r * (u - xn * jnp.mean(xn * u, axis=-1, keepdims=True))
                return (sums[0] + jnp.sum(dhv, axis=0, keepdims=True),
                        sums[1] + jnp.sum(dhv * (xn * g), axis=0, keepdims=True),
                        sums[2] + jnp.sum(dhn * xn, axis=0, keepdims=True))

            zero = jnp.zeros((1, D), F32)
            sums = lax.fori_loop(0, tm // rc, chunk, (zero, zero, zero))
            dshift_ref[...] += sums[0]
            dscale_ref[...] += sums[1]
            dg_ref[...] += sums[2]

    tile = pl.BlockSpec((tm, D), lambda i, d: (i, 0))
    row = pl.BlockSpec((1, D), lambda i, d: (0, 0))
    return pl.pallas_call(
        body, name=name, grid=(S // tm, NDEV // wb),
        out_shape=[jax.ShapeDtypeStruct((S, D), F32)] + [jax.ShapeDtypeStruct((1, D), F32)] * 3
        + [jax.ShapeDtypeStruct(carried.shape, carried.dtype)],
        in_specs=[pl.BlockSpec((tm, wb * nb), lambda i, d: (i, d)), pl.BlockSpec((wb, D, nb), lambda i, d: (d, 0, 0)),
                  tile, tile, pl.BlockSpec((None, 1, D), lambda i, d: (layer, 0, 0)), _mod_spec(layer, 1, 2), ANY],
        out_specs=[tile, row, row, row, ANY], scratch_shapes=[pltpu.VMEM((tm, D), F32)],
        input_output_aliases={6: 4}, compiler_params=_params(("arbitrary", "arbitrary")),
    )(d_proj, wg, x, dx, norm_g3, mod, carried)


TS = 256
NCH = TS // CHUNK
HALO_BLOCKS = TS // HALO


def _halo_before(width, col_block):
    return pl.BlockSpec((HALO, width), lambda i: (jnp.maximum(i * HALO_BLOCKS - 1, 0), col_block))


def _halo_after(width, col_block):
    return pl.BlockSpec((HALO, width), lambda i: (jnp.minimum((i + 1) * HALO_BLOCKS, S // HALO - 1), col_block))


def _shift_down(ext, k):
    return pltpu.roll(ext, k, 0)[HALO:]


def _shift_up(ext, k):
    return pltpu.roll(ext, ext.shape[0] - k, 0)[:ext.shape[0] - HALO]


def _layer_norm_head(v, lg, lb):
    mu = jnp.mean(v, axis=-1, keepdims=True)
    vc = v - mu
    rstd = lax.rsqrt(jnp.mean(vc * vc, axis=-1, keepdims=True) + EPS)
    vhat = vc * rstd
    return vhat, rstd, vhat * lg + lb


def _causal_mask():
    return lax.broadcasted_iota(jnp.int32, (CHUNK, CHUNK), 0) >= lax.broadcasted_iota(jnp.int32, (CHUNK, CHUNK), 1)


def _even_mix_fwd(proj, convw, ln_g3, ln_b3, sgu_w, sgu_bcol, wl, after, name):
    def body(pj_ref, hh_ref, hc_ref, cw_ref, lg_ref, lb_ref, sw_ref, sb_ref, *rest):
        y_ref = rest[-1]
        live = (pl.program_id(0) > 0).astype(F32)
        causal = _causal_mask()
        for j in range(E_A // HEAD):
            cols = slice(j * HEAD, (j + 1) * HEAD)
            w0, w1, w2 = cw_ref[0:1, cols], cw_ref[1:2, cols], cw_ref[2:3, cols]
            lg, lb = lg_ref[:, cols], lb_ref[:, cols]
            wm = jnp.where(causal, sw_ref[j], 0.0).astype(BF16)
            bias = sb_ref[j]

            def split(s, rows, cols=cols):
                return pj_ref[rows, s * E_A + cols.start:s * E_A + cols.stop].astype(F32)

            prev_tail = hc_ref[:, cols].astype(F32) * hh_ref[:, cols].astype(F32) * live
            for n in range(NCH):
                rows = slice(n * CHUNK, (n + 1) * CHUNK)
                p = split(2, rows) * split(0, rows)
                ext = jnp.concatenate([prev_tail, p], axis=0)
                prev_tail = p[CHUNK - HALO:]
                cv = w2 * p + w1 * _shift_down(ext, 1) + w0 * _shift_down(ext, 2)
                y_ref[rows, cols] = (split(1, rows) * cv * _silu(split(3, rows))).astype(BF16)
                _, _, vn = _layer_norm_head(split(5, rows), lg, lb)
                mixed = jnp.dot(wm, vn.astype(BF16), preferred_element_type=F32) + bias
                y_ref[rows, E_A + cols.start:E_A + cols.stop] = (split(4, rows) * mixed * _silu(split(6, rows))).astype(BF16)

    const3 = lambda i: (wl, 0, 0)
    const4 = lambda i: (wl, 0, 0, 0)
    return pl.pallas_call(
        body, name=name, grid=(S // TS,), out_shape=jax.ShapeDtypeStruct((S, 2 * E_A), BF16),
        in_specs=[pl.BlockSpec((TS, 7 * E_A), lambda i: (i, 0)), _halo_before(E_A, 0), _halo_before(E_A, 2),
                  pl.BlockSpec((None, 3, E_A), const3), pl.BlockSpec((None, 1, E_A), const3),
                  pl.BlockSpec((None, 1, E_A), const3), pl.BlockSpec((None, NDEV, CHUNK, CHUNK), const4),
                  pl.BlockSpec((None, NDEV, CHUNK, 1), const4)] + [ANY] * len(after),
        out_specs=pl.BlockSpec((TS, 2 * E_A), lambda i: (i, 0)),
        compiler_params=_params(("arbitrary",)),
    )(proj, proj, proj, convw, ln_g3, ln_b3, sgu_w, sgu_bcol, *after)


def _even_mix_bwd(proj, d_ycat, convw, ln_g3, ln_b3, sgu_w, sgu_bcol, wl, name):
    nsteps = S // TS

    def body(pj_ref, hh_ref, hc_ref, hb_ref, hz_ref, dy_ref, hdy_ref, cw_ref, lg_ref, lb_ref, sw_ref, sb_ref,
             dp_ref, dcw_ref, dlg_ref, dlb_ref, dsw_ref, dsb_ref):
        step = pl.program_id(0)

        @pl.when(step == 0)
        def _():
            for ref in (dcw_ref, dlg_ref, dlb_ref, dsw_ref, dsb_ref):
                ref[...] = jnp.zeros_like(ref)

        live_before = (step > 0).astype(F32)
        live_after = (step < nsteps - 1).astype(F32)
        causal = _causal_mask()
        for j in range(E_A // HEAD):
            cols = slice(j * HEAD, (j + 1) * HEAD)
            w0, w1, w2 = cw_ref[0:1, cols], cw_ref[1:2, cols], cw_ref[2:3, cols]
            lg, lb = lg_ref[:, cols], lb_ref[:, cols]
            wmf = jnp.where(causal, sw_ref[j], 0.0)
            wm, wmt = wmf.astype(BF16), wmf.T.astype(BF16)
            bias = sb_ref[j]

            def split(s, rows, cols=cols):
                return pj_ref[rows, s * E_A + cols.start:s * E_A + cols.stop].astype(F32)

            def put(s, rows, val, cols=cols):
                dp_ref[rows, s * E_A + cols.start:s * E_A + cols.stop] = val.astype(BF16)

            ps = [split(2, slice(n * CHUNK, (n + 1) * CHUNK)) * split(0, slice(n * CHUNK, (n + 1) * CHUNK)) for n in range(NCH)]
            next_head = (hdy_ref[:, cols].astype(F32) * hb_ref[:, cols].astype(F32) * _silu(hz_ref[:, cols].astype(F32))
                         * live_after)
            acc_w = [jnp.zeros((1, HEAD), F32) for _ in range(3)]
            for n in reversed(range(NCH)):
                rows = slice(n * CHUNK, (n + 1) * CHUNK)
                p = ps[n]
                tail = ps[n - 1][CHUNK - HALO:] if n > 0 else hc_ref[:, cols].astype(F32) * hh_ref[:, cols].astype(F32) * live_before
                ext = jnp.concatenate([tail, p], axis=0)
                p1, p2 = _shift_down(ext, 1), _shift_down(ext, 2)
                cv = w2 * p + w1 * p1 + w0 * p2
                a_b, a_z = split(1, rows), split(3, rows)
                sz, dsz = _silu_and_grad(a_z)
                dya = dy_ref[rows, cols].astype(F32)
                put(1, rows, dya * cv * sz)
                put(3, rows, dya * a_b * cv * dsz)
                gcv = dya * a_b * sz
                acc_w[0] += jnp.sum(gcv * p2, axis=0, keepdims=True)
                acc_w[1] += jnp.sum(gcv * p1, axis=0, keepdims=True)
                acc_w[2] += jnp.sum(gcv * p, axis=0, keepdims=True)
                gext = jnp.concatenate([gcv, next_head], axis=0)
                next_head = gcv[:HALO]
                dpv = w2 * gcv + w1 * _shift_up(gext, 1) + w0 * _shift_up(gext, 2)
                put(2, rows, dpv * split(0, rows))
                put(0, rows, dpv * split(2, rows))
            for k in range(3):
                dcw_ref[k:k + 1, cols] += acc_w[k]

            acc_lg, acc_lb = jnp.zeros((1, HEAD), F32), jnp.zeros((1, HEAD), F32)
            acc_sw, acc_sb = jnp.zeros((CHUNK, CHUNK), F32), jnp.zeros((CHUNK, 1), F32)
            for n in range(NCH):
                rows = slice(n * CHUNK, (n + 1) * CHUNK)
                u, z = split(4, rows), split(6, rows)
                vhat, rstd, vn = _layer_norm_head(split(5, rows), lg, lb)
                vn16 = vn.astype(BF16)
                mixed = jnp.dot(wm, vn16, preferred_element_type=F32) + bias
                sz, dsz = _silu_and_grad(z)
                dyb = dy_ref[rows, E_A + cols.start:E_A + cols.stop].astype(F32)
                put(4, rows, dyb * mixed * sz)
                put(6, rows, dyb * u * mixed * dsz)
                dmix = dyb * u * sz
                dmix16 = dmix.astype(BF16)
                acc_sb += jnp.sum(dmix, axis=1, keepdims=True)
                acc_sw += lax.dot_general(dmix16, vn16, (((1,), (1,)), ((), ())), preferred_element_type=F32)
                dvn = jnp.dot(wmt, dmix16, preferred_element_type=F32)
                acc_lg += jnp.sum(dvn * vhat, axis=0, keepdims=True)
                acc_lb += jnp.sum(dvn, axis=0, keepdims=True)
                dvh = dvn * lg
                put(5, rows, rstd * (dvh - jnp.mean(dvh, axis=-1, keepdims=True)
                                     - vhat * jnp.mean(dvh * vhat, axis=-1, keepdims=True)))
            dlg_ref[:, cols] += acc_lg
            dlb_ref[:, cols] += acc_lb
            dsw_ref[j] += jnp.where(causal, acc_sw, 0.0)
            dsb_ref[j] += acc_sb

    const3 = lambda i: (wl, 0, 0)
    const4 = lambda i: (wl, 0, 0, 0)
    fixed2 = lambda i: (0, 0)
    fixed3 = lambda i: (0, 0, 0)
    return pl.pallas_call(
        body, name=name, grid=(nsteps,),
        out_shape=[jax.ShapeDtypeStruct((S, 7 * E_A), BF16), jax.ShapeDtypeStruct((3, E_A), F32),
                   jax.ShapeDtypeStruct((1, E_A), F32), jax.ShapeDtypeStruct((1, E_A), F32),
                   jax.ShapeDtypeStruct((NDEV, CHUNK, CHUNK), F32), jax.ShapeDtypeStruct((NDEV, CHUNK, 1), F32)],
        in_specs=[pl.BlockSpec((TS, 7 * E_A), lambda i: (i, 0)), _halo_before(E_A, 0), _halo_before(E_A, 2),
                  _halo_after(E_A, 1), _halo_after(E_A, 3),
                  pl.BlockSpec((TS, 2 * E_A), lambda i: (i, 0)), _halo_after(E_A, 0),
                  pl.BlockSpec((None, 3, E_A), const3), pl.BlockSpec((None, 1, E_A), const3),
                  pl.BlockSpec((None, 1, E_A), const3), pl.BlockSpec((None, NDEV, CHUNK, CHUNK), const4),
                  pl.BlockSpec((None, NDEV, CHUNK, 1), const4)],
        out_specs=[pl.BlockSpec((TS, 7 * E_A), lambda i: (i, 0)), pl.BlockSpec((3, E_A), fixed2),
                   pl.BlockSpec((1, E_A), fixed2), pl.BlockSpec((1, E_A), fixed2),
                   pl.BlockSpec((NDEV, CHUNK, CHUNK), fixed3), pl.BlockSpec((NDEV, CHUNK, 1), fixed3)],
        compiler_params=_params(("arbitrary",)),
    )(proj, proj, proj, proj, proj, d_ycat, d_ycat, convw, ln_g3, ln_b3, sgu_w, sgu_bcol)


def _window_count(step, n, win, ext_before):
    rows = CHUNK if ext_before else CHUNK + HALO
    t = step * TS + n * CHUNK + lax.broadcasted_iota(jnp.int32, (rows, 1), 0)
    return jnp.minimum(t + 1, win).astype(F32)


def _pool_weight(wp_ref, g):
    return jnp.concatenate([wp_ref[d, g] for d in range(NDEV)], axis=0)


def _pooled_chunk(p, tail, win, count):
    sums = jnp.concatenate([tail, p], axis=0)
    shift = 1
    while shift < win:
        sums = sums + pltpu.roll(sums, shift, 0)
        shift *= 2
    return sums[HALO:] / count - p


def _pool_mix_fwd(proj, wpool, pscale4, wl, after, name):
    e_c = 4 * G_C

    def body(pj_ref, hp_ref, wp_ref, ps_ref, *rest):
        y_ref, pooled_scr, yraw_scr = rest[-3:]
        step = pl.program_id(0)
        live = (step > 0).astype(F32)
        for g, win in enumerate(POOL_WINDOWS):
            for q in range(G_C // LANE):
                cols = slice(g * G_C + q * LANE, g * G_C + (q + 1) * LANE)
                tail = hp_ref[:, cols].astype(F32) * live
                for n in range(NCH):
                    rows = slice(n * CHUNK, (n + 1) * CHUNK)
                    p = pj_ref[rows, cols].astype(F32)
                    pooled_scr[rows, q * LANE:(q + 1) * LANE] = _pooled_chunk(
                        p, tail, win, _window_count(step, n, win, True)).astype(BF16)
                    tail = p[CHUNK - HALO:]
            yraw_scr[...] = jnp.dot(pooled_scr[...], _pool_weight(wp_ref, g), preferred_element_type=F32)
            for q in range(G_C // LANE):
                cols = slice(g * G_C + q * LANE, g * G_C + (q + 1) * LANE)
                for n in range(NCH):
                    rows = slice(n * CHUNK, (n + 1) * CHUNK)
                    z = pj_ref[rows, e_c + cols.start:e_c + cols.stop].astype(F32)
                    y_ref[rows, cols] = (yraw_scr[rows, q * LANE:(q + 1) * LANE] * ps_ref[:, cols] * _silu(z)).astype(BF16)

    return pl.pallas_call(
        body, name=name, grid=(S // TS,), out_shape=jax.ShapeDtypeStruct((S, e_c), BF16),
        in_specs=[pl.BlockSpec((TS, 2 * e_c), lambda i: (i, 0)), _halo_before(e_c, 0),
                  pl.BlockSpec((NDEV, 4, G_C // NDEV, G_C), lambda i: (0, 0, 0, 0)),
                  pl.BlockSpec((None, 1, e_c), lambda i: (wl, 0, 0))] + [ANY] * len(after),
        out_specs=pl.BlockSpec((TS, e_c), lambda i: (i, 0)),
        scratch_shapes=[pltpu.VMEM((TS, G_C), BF16), pltpu.VMEM((TS, G_C), F32)],
        compiler_params=_params(("arbitrary",)),
    )(proj, proj, wpool, pscale4, *after)


def _pool_mix_bwd(proj, d_ycat, wpool, pscale4, wl, name):
    e_c = 4 * G_C
    nsteps = S // TS
    rb = G_C // NDEV

    def body(pj_ref, hp_ref, hz_ref, dy_ref, hdy_ref, wp_ref, ps_ref,
             dp_ref, dps_ref, dwp_ref, pooled_scr, yraw_scr, dyraw_scr, dpool_scr, acc_w):
        step = pl.program_id(0)

        @pl.when(step == 0)
        def _():
            dps_ref[...] = jnp.zeros_like(dps_ref)
            acc_w[...] = jnp.zeros_like(acc_w)

        live_before = (step > 0).astype(F32)
        live_after = (step < nsteps - 1).astype(F32)
        for g, win in enumerate(POOL_WINDOWS):
            weight = _pool_weight(wp_ref, g)
            for q in range(G_C // LANE):
                cols = slice(g * G_C + q * LANE, g * G_C + (q + 1) * LANE)
                tail = hp_ref[:, cols].astype(F32) * live_before
                for n in range(NCH):
                    rows = slice(n * CHUNK, (n + 1) * CHUNK)
                    p = pj_ref[rows, cols].astype(F32)
                    pooled_scr[rows, q * LANE:(q + 1) * LANE] = _pooled_chunk(
                        p, tail, win, _window_count(step, n, win, True)).astype(BF16)
                    tail = p[CHUNK - HALO:]
            yraw_scr[...] = jnp.dot(pooled_scr[...], weight, preferred_element_type=F32)
            for q in range(G_C // LANE):
                cols = slice(g * G_C + q * LANE, g * G_C + (q + 1) * LANE)
                local = slice(q * LANE, (q + 1) * LANE)
                scale = ps_ref[:, cols]
                acc_ps = jnp.zeros((1, LANE), F32)
                for n in range(NCH):
                    rows = slice(n * CHUNK, (n + 1) * CHUNK)
                    sz, dsz = _silu_and_grad(pj_ref[rows, e_c + cols.start:e_c + cols.stop].astype(F32))
                    dyv = dy_ref[rows, cols].astype(F32)
                    yraw = yraw_scr[rows, local]
                    dyraw_scr[rows, local] = (dyv * scale * sz).astype(BF16)
                    acc_ps += jnp.sum(dyv * yraw * sz, axis=0, keepdims=True)
                    dp_ref[rows, e_c + cols.start:e_c + cols.stop] = (dyv * yraw * scale * dsz).astype(BF16)
                dps_ref[:, cols] += acc_ps
                dyraw_scr[TS:, local] = (hdy_ref[:, cols].astype(F32) * scale * _silu(hz_ref[:, cols].astype(F32))
                                         * live_after).astype(BF16)
            dpool_scr[...] = lax.dot_general(dyraw_scr[...], weight, (((1,), (1,)), ((), ())), preferred_element_type=F32)
            acc_w[g] += lax.dot_general(pooled_scr[...], dyraw_scr[:TS, :], (((0,), (0,)), ((), ())),
                                        preferred_element_type=F32)
            for q in range(G_C // LANE):
                cols = slice(g * G_C + q * LANE, g * G_C + (q + 1) * LANE)
                local = slice(q * LANE, (q + 1) * LANE)
                for n in range(NCH):
                    rows = slice(n * CHUNK, (n + 1) * CHUNK)
                    ext = dpool_scr[n * CHUNK:(n + 1) * CHUNK + HALO, local]
                    sums = ext / _window_count(step, n, win, False)
                    shift = 1
                    while shift < win:
                        sums = sums + pltpu.roll(sums, CHUNK + HALO - shift, 0)
                        shift *= 2
                    dp_ref[rows, cols] = (sums[:CHUNK] - ext[:CHUNK]).astype(BF16)

        @pl.when(step == nsteps - 1)
        def _():
            for g in range(4):
                for d in range(NDEV):
                    dwp_ref[d % 2, d // 2, g] = acc_w[g, d * rb:(d + 1) * rb, :].astype(BF16)

    in_specs = [pl.BlockSpec((TS, 2 * e_c), lambda i: (i, 0)), _halo_before(e_c, 0), _halo_after(e_c, 1),
                pl.BlockSpec((TS, e_c), lambda i: (i, 0)), _halo_after(e_c, 0),
                pl.BlockSpec((NDEV, 4, rb, G_C), lambda i: (0, 0, 0, 0)),
                pl.BlockSpec((None, 1, e_c), lambda i: (wl, 0, 0))]
    args = [proj, proj, proj, d_ycat, d_ycat, wpool, pscale4]
    return pl.pallas_call(
        body, name=name, grid=(nsteps,),
        out_shape=[jax.ShapeDtypeStruct((S, 2 * e_c), BF16), jax.ShapeDtypeStruct((1, e_c), F32),
                   jax.ShapeDtypeStruct((2, NDEV // 2) + wpool.shape[1:], BF16)],
        in_specs=in_specs,
        out_specs=[pl.BlockSpec((TS, 2 * e_c), lambda i: (i, 0)), pl.BlockSpec((1, e_c), lambda i: (0, 0)),
                   pl.BlockSpec((2, NDEV // 2, 4, rb, G_C), lambda i: (0, 0, 0, 0, 0))],
        scratch_shapes=[pltpu.VMEM((TS, G_C), BF16), pltpu.VMEM((TS, G_C), F32), pltpu.VMEM((TS + HALO, G_C), BF16),
                        pltpu.VMEM((TS + HALO, G_C), F32), pltpu.VMEM((4, G_C, G_C), F32)],
        compiler_params=_params(("arbitrary",)),
    )(*args)


def _adamw(w, g, m, v):
    m = ADAM_B1 * m + (1.0 - ADAM_B1) * g
    v = ADAM_B2 * v + (1.0 - ADAM_B2) * jnp.square(g)
    m_hat = m / (1.0 - ADAM_B1 ** ADAM_STEP)
    v_hat = v / (1.0 - ADAM_B2 ** ADAM_STEP)
    delta = -ADAM_LR * (m_hat / (jnp.sqrt(v_hat) + ADAM_EPS) + ADAM_WD * w)
    return delta, m, v


def _adam_sharded(w, m, v, chip_parts, landed, my_chip, carried, name, first=0, into=()):
    _, nr, ncol = w.shape
    nl = len(chip_parts)
    tr = 128
    steps = nr // tr
    nc, ni = len(carried), len(into)

    def body(chip_ref, w_ref, m_ref, v_ref, *rest):
        parts, zones = rest[:nl], rest[nl:2 * nl]
        g_ref, d_ref, nm_ref, nv_ref = rest[2 * nl + nc + ni:2 * nl + nc + ni + 4]
        layer = pl.program_id(0)
        g = jnp.zeros((tr, ncol), F32)
        for l in range(nl):
            gl = parts[l][...].astype(F32)
            for q in range(3):
                gl = gl + zones[l][q].astype(F32)
            g = jnp.where(layer == l, gl, g)
        g_ref[...] = g
        d_ref[...], nm_ref[...], nv_ref[...] = _adamw(w_ref[...], g, m_ref[...], v_ref[...])

    def rows_of(l):
        return lambda layer, i, chip_ref: jnp.where(layer == l, i, jnp.where(layer < l, 0, steps - 1))

    spec = pl.BlockSpec((None, tr, ncol), lambda layer, i, chip_ref: (first + layer, i, 0))
    part_specs = [pl.BlockSpec((None, tr, ncol), lambda layer, i, chip_ref, l=l: (chip_ref[0], rows_of(l)(layer, i, chip_ref), 0))
                  for l in range(nl)]
    zone_specs = [pl.BlockSpec((3, tr, ncol), lambda layer, i, chip_ref, l=l: (0, rows_of(l)(layer, i, chip_ref), 0))
                  for l in range(nl)]
    grid_spec = pltpu.PrefetchScalarGridSpec(
        num_scalar_prefetch=1, grid=(nl, steps),
        in_specs=[spec, spec, spec] + part_specs + zone_specs + [ANY] * (nc + ni), out_specs=[spec] * 4 + [ANY] * nc)
    aliases = {4 + 2 * nl + k: 4 + k for k in range(nc)}
    aliases.update({4 + 2 * nl + nc + k: k for k in range(ni)})
    return pl.pallas_call(
        body, name=name, grid_spec=grid_spec,
        out_shape=[jax.ShapeDtypeStruct(w.shape, F32)] * 4 + [jax.ShapeDtypeStruct(a.shape, a.dtype) for a in carried],
        input_output_aliases=aliases, compiler_params=_params(("arbitrary", "arbitrary")),
    )(my_chip, w, m, v, *chip_parts, *landed, *carried, *into)


def _adam_small(w, g, m, v, name):
    def body(w_ref, g_ref, m_ref, v_ref, d_ref, nm_ref, nv_ref):
        d_ref[...], nm_ref[...], nv_ref[...] = _adamw(w_ref[...], g_ref[...], m_ref[...], v_ref[...])

    return pl.pallas_call(body, name=name, out_shape=[jax.ShapeDtypeStruct(w.shape, F32)] * 3,
                          in_specs=[VMEM_FULL] * 4, out_specs=[VMEM_FULL] * 3, compiler_params=_params())(w, g, m, v)


def _sum_devices(gathered, name):
    _, nr, ncol = gathered.shape

    def body(g_ref, o_ref):
        acc = g_ref[0].astype(F32)
        for s in range(1, NDEV):
            acc = acc + g_ref[s].astype(F32)
        o_ref[...] = acc

    return pl.pallas_call(body, name=name, grid=(1,), out_shape=jax.ShapeDtypeStruct((nr, ncol), F32),
                          in_specs=[pl.BlockSpec((NDEV, nr, ncol), lambda i: (0, 0, 0))],
                          out_specs=pl.BlockSpec((nr, ncol), lambda i: (0, 0)),
                          compiler_params=_params(("arbitrary",)))(gathered)


def _ada_weight_adam(cact_t, dmod_mine, w, m, v):
    def body(ct_ref, dm_ref, w_ref, m_ref, v_ref, g_ref, d_ref, nm_ref, nv_ref):
        ct, dm = ct_ref[...], dm_ref[...]
        g = ct[:, 0:1] * dm[0:1, :]
        for e in range(1, NDEV):
            g = g + ct[:, e:e + 1] * dm[e:e + 1, :]
        g_ref[...] = g
        d_ref[...], nm_ref[...], nv_ref[...] = _adamw(w_ref[...], g, m_ref[...], v_ref[...])

    spec = pl.BlockSpec((None, D, ADA_NC), lambda l: (l, 0, 0))
    return pl.pallas_call(
        body, name="ada_weight_adam", grid=(DEPTH,), out_shape=[jax.ShapeDtypeStruct(w.shape, F32)] * 4,
        in_specs=[pl.BlockSpec((D, NDEV), lambda l: (0, 0)), pl.BlockSpec((None, NDEV, ADA_NC), lambda l: (l, 0, 0)),
                  spec, spec, spec],
        out_specs=[spec] * 4, compiler_params=_params(("arbitrary",)),
    )(cact_t, dmod_mine, w, m, v)


def _pad_rows(a, rows):
    a = a.reshape(-1, D)
    return jnp.pad(a, ((0, rows - a.shape[0]), (0, 0)))


def kernel(x, c, norm_g, ada_w, ada_b, ab_w_in, ab_conv_w, ab_ln_g, ab_ln_b, ab_sgu_w, ab_sgu_b, ab_w_out, c_w_in, c_pool_w, c_pool_scale, c_w_out, final_g, loss_target, m_norm_g, m_ada_w, m_ada_b, m_ab_w_in, m_ab_conv_w, m_ab_ln_g, m_ab_ln_b, m_ab_sgu_w, m_ab_sgu_b, m_ab_w_out, m_c_w_in, m_c_pool_w, m_c_pool_scale, m_c_w_out, m_final_g, v_norm_g, v_ada_w, v_ada_b, v_ab_w_in, v_ab_conv_w, v_ab_ln_g, v_ab_ln_b, v_ab_sgu_w, v_ab_sgu_b, v_ab_w_out, v_c_w_in, v_c_pool_w, v_c_pool_scale, v_c_w_out, v_final_g):
    x_pos, y_pos, c_pos = _position()
    me = _index((x_pos, y_pos, c_pos))
    core = c_pos.astype(jnp.int32).reshape(1)
    my_chip = (2 * x_pos + y_pos).astype(jnp.int32).reshape(1)
    me1 = me.astype(jnp.int32).reshape(1)
    x0 = x.reshape(S, D)
    target = loss_target.reshape(S, D)
    norm_g3 = norm_g.reshape(DEPTH, 1, D)
    ln_g3, ln_b3 = ab_ln_g.reshape(2, 1, E_A), ab_ln_b.reshape(2, 1, E_A)
    sgu_bcol = ab_sgu_b.reshape(2, NDEV, CHUNK, 1)
    rb = G_C // NDEV
    pool_w3, m_pool_w3, v_pool_w3 = (a.reshape(2, 4 * rb, G_C) for a in (c_pool_w, m_c_pool_w, v_c_pool_w))

    cact_all, mod = _ada_forward(c, ada_w, ada_b)
    convw_all, pscale_all = _gather([ab_conv_w, c_pool_scale], "gather_small_weights")
    convw = jnp.transpose(convw_all, (1, 2, 0, 3)).reshape(2, 3, E_A)
    pscale4 = jnp.transpose(pscale_all, (1, 0, 2)).reshape(2, 1, 4 * G_C)
    zones = []
    for layer in range(DEPTH):
        wl = layer // 2
        if layer % 2 == 0:
            zones.append([_to_zone(ab_w_in, wl, me1, BF16, f"cast_w_in_{layer}"), _to_zone(ab_w_out, wl, me1, BF16, f"cast_w_out_{layer}")])
        else:
            zones.append([_to_zone(c_w_in, wl, me1, BF16, f"cast_w_in_{layer}"), _to_zone(c_w_out, wl, me1, BF16, f"cast_w_out_{layer}"),
                          _to_zone(pool_w3, wl, me1, BF16, f"cast_pool_w_{layer}")])

    def gathered(flight, after, layer):
        wg = _gather_end(flight, [after], f"gather_end_{layer}")
        return [wg[0], wg[1].reshape(-1, D)] + [w.reshape(NDEV, 4, rb, G_C) for w in wg[2:]]

    flight, (mod,) = _gather_start(zones[0], [convw_all], [mod], "gather_start_0")
    flight, (mod,) = _gather_mid(flight, [z for zs in zones[1:] for z in zs], [mod], "gather_mid_0")
    next_flight, (mod,) = _gather_start(zones[1], [], [mod], "gather_start_1")
    flight = _gather_late(flight, [mod], "gather_late_0")
    xs, hts, projs, ycats, outs, gathered_w = [x0], [], [], [], [], [gathered(flight, mod, 0)]
    for layer in range(DEPTH):
        wl = layer // 2
        even = layer % 2 == 0
        wg = gathered_w[layer]
        h_t, proj = _norm_proj(xs[-1], mod, norm_g3, wg[0], layer, f"norm_proj_{layer}")
        if layer + 1 < DEPTH:
            flight, (h_t,) = _gather_mid(next_flight, [], [h_t], f"gather_mid_{layer + 1}")
            if layer + 2 < DEPTH:
                next_flight, (h_t,) = _gather_start(zones[layer + 2], [], [h_t], f"gather_start_{layer + 2}")
        if even:
            ycat = _even_mix_fwd(proj, convw, ln_g3, ln_b3, ab_sgu_w, sgu_bcol, wl, [h_t], f"even_mix_fwd_{layer}")
        else:
            ycat = _pool_mix_fwd(proj, wg[2], pscale4, wl, [h_t], f"pool_mix_fwd_{layer}")
        if layer + 1 < DEPTH:
            flight = _gather_late(flight, [ycat], f"gather_late_{layer + 1}")
        x_new, out = _out_proj(ycat, wg[1], xs[-1], mod, layer, f"out_proj_{layer}")
        if layer + 1 < DEPTH:
            gathered_w.append(gathered(flight, x_new, layer + 1))
        xs.append(x_new)
        hts.append(h_t)
        projs.append(proj)
        ycats.append(ycat)
        outs.append(out)

    dx, loss_part, d_final_g = _final_loss(xs[DEPTH], target, final_g.reshape(1, D))

    d_mod, d_norm_g = [None] * DEPTH, [None] * DEPTH
    small, scatters, landed, res = {}, {}, {}, {}

    def finish_scatter(layer, after):
        send_sems, recv_sems, chip_parts, zones = scatters[layer]
        landed[layer] = _scatter_end(chip_parts, zones, send_sems, recv_sems, after, f"scatter_end_{layer}")

    def flat(a):
        return a.reshape(a.shape[0], -1, a.shape[-1])

    def sharded_adam(k, j, layers, w, m, v, carried, first=0, into=()):
        outs4 = _adam_sharded(w, m, v, [flat(landed[l][0][j]) for l in layers], [flat(landed[l][1][j]) for l in layers],
                              my_chip, carried, f"adam_{k}_{first}" if len(layers) < w.shape[0] else "adam_" + k, first, into)
        res[k] = [o.reshape(c_pool_w.shape) if k == "c_pool_w" else o for o in outs4[:4]]
        return list(outs4[4:])

    previous = None
    for layer in reversed(range(DEPTH)):
        wl = layer // 2
        even = layer % 2 == 0
        wg = gathered_w[layer]
        carried = [] if previous is None else [scatters[previous][2][0]]
        d_ycat, grad_out, d_gate, carried = _out_bwd(dx, outs[layer], ycats[layer], wg[1], mod, layer, carried, f"out_bwd_{layer}")
        if previous is not None:
            scatters[previous][2][0] = carried[0]
        parts = [None, grad_out]
        if even:
            d_proj, d_cw, d_lg, d_lb, d_sw, d_sb = _even_mix_bwd(
                projs[layer], d_ycat, convw, ln_g3, ln_b3, ab_sgu_w, sgu_bcol, wl, f"even_mix_bwd_{layer}")
            small[layer] = (d_cw, d_lg, d_lb, d_sw, d_sb)
        else:
            d_proj, d_ps, d_pool = _pool_mix_bwd(projs[layer], d_ycat, wg[2], pscale4, wl, f"pool_mix_bwd_{layer}")
            small[layer] = (d_ps,)
            parts.append(d_pool)
        parts[0] = _weight_grad(hts[layer], d_proj, f"grad_w_in_{layer}")
        pair_send, pair_recv, parts, from_sibling = _pair_start(parts, f"pair_start_{layer}")
        if layer > 0:
            dx, d_shift, d_scale, d_norm_g[layer], parts[0] = _dh_norm_bwd(
                d_proj, wg[0], xs[layer], dx, mod, norm_g3, layer, parts[0], f"dh_norm_bwd_{layer}")
            pair_after = dx
        else:
            finish_scatter(1, d_proj)
            finish_scatter(3, d_proj)
            parts[0], = sharded_adam("c_w_out", 1, (1, 3), c_w_out, m_c_w_out, v_c_w_out, [parts[0]])
            parts[0], = sharded_adam("c_pool_w", 2, (1, 3), pool_w3, m_pool_w3, v_pool_w3, [parts[0]])
            pair_after = res["c_pool_w"][0]
        parts, from_sibling = _pair_end(parts, from_sibling, pair_send, pair_recv, pair_after, f"pair_end_{layer}")
        chip_parts = [_pair_sum(p, q, core, f"pair_sum_{layer}_{j}") for j, (p, q) in enumerate(zip(parts, from_sibling))]
        send_sems, recv_sems, chip_parts, zones = _scatter_start(chip_parts, f"scatter_start_{layer}")
        if layer == 0:
            chip_parts[0], = sharded_adam("c_w_in", 0, (1, 3), c_w_in, m_c_w_in, v_c_w_in, [chip_parts[0]])
            dx, d_shift, d_scale, d_norm_g[layer], chip_parts[0] = _dh_norm_bwd(
                d_proj, wg[0], xs[layer], dx, mod, norm_g3, layer, chip_parts[0], f"dh_norm_bwd_{layer}")
        scatters[layer] = [send_sems, recv_sems, chip_parts, zones]
        previous = layer
        d_mod[layer] = jnp.concatenate([d_shift, d_scale, d_gate], axis=0)
    grad_x = dx.reshape(x.shape)

    sections = [("norm_g", jnp.concatenate(d_norm_g, axis=0), 8),
                ("d_mod", jnp.concatenate(d_mod, axis=0), 16),
                ("ab_ln_g", jnp.concatenate([small[0][1], small[2][1]], axis=0), 8),
                ("ab_ln_b", jnp.concatenate([small[0][2], small[2][2]], axis=0), 8),
                ("ab_sgu_b", jnp.stack([small[0][4], small[2][4]]), 8),
                ("final_g", d_final_g, 8),
                ("ab_conv_w", jnp.stack([small[0][0], small[2][0]]), 8),
                ("c_pool_scale", jnp.concatenate([small[1][0], small[3][0]], axis=0), 8),
                ("ab_sgu_w", jnp.stack([small[0][3], small[2][3]]), 256)]
    offsets, at = {}, 0
    for name, _, rows in sections:
        offsets[name] = (at, rows)
        at += rows
    packed = jnp.concatenate([_pad_rows(a, rows) for _, a, rows in sections] + [jnp.zeros((-at % 32, D), F32)], axis=0)
    loss_rows = jnp.pad(loss_part, ((0, 15), (0, D - LANE)))
    small_zones = [_to_zone(packed[None], 0, me1, BF16, "place_small_grads"), _to_zone(loss_rows[None], 0, me1, F32, "place_loss")]
    small_flight, (mod,) = _gather_start(small_zones, [], [mod], "gather_small_start")

    finish_scatter(2, mod)
    sharded_adam("ab_w_out", 1, (2,), ab_w_out, m_ab_w_out, v_ab_w_out, [], first=1)
    sharded_adam("ab_w_in", 0, (2,), ab_w_in, m_ab_w_in, v_ab_w_in, [], first=1)
    finish_scatter(0, res["ab_w_in"][0])
    sharded_adam("ab_w_out", 1, (0,), ab_w_out, m_ab_w_out, v_ab_w_out, [], into=res["ab_w_out"])
    small_flight, (mod,) = _gather_mid(small_flight, [res["ab_w_out"][0]], [mod], "gather_small_mid")
    sharded_adam("ab_w_in", 0, (0,), ab_w_in, m_ab_w_in, v_ab_w_in, [mod], into=res["ab_w_in"])

    last = res["ab_w_in"][0]
    small_flight = _gather_late(small_flight, [last], "gather_small_late")
    small_grads, losses = _gather_end(small_flight, [last], "gather_small_end")
    summed = _sum_devices(small_grads, "sum_small_grads")
    loss = _sum_devices(losses, "sum_loss")[0, 0]

    def section(name, nrows, src=summed):
        start = offsets[name][0]
        return src[..., start:start + nrows, :]

    grads = {
        "norm_g": section("norm_g", DEPTH),
        "ada_b": section("d_mod", 3 * DEPTH).reshape(DEPTH, 3 * D),
        "ab_ln_g": section("ab_ln_g", 2), "ab_ln_b": section("ab_ln_b", 2),
        "ab_sgu_b": section("ab_sgu_b", 2).reshape(ab_sgu_b.shape),
        "final_g": section("final_g", 1),
        "ab_sgu_w": section("ab_sgu_w", 256).reshape(ab_sgu_w.shape),
        "ab_conv_w": lax.dynamic_slice_in_dim(section("ab_conv_w", 6).reshape(2, 3, E_A), me * HEAD, HEAD, axis=2),
        "c_pool_scale": lax.dynamic_slice_in_dim(section("c_pool_scale", 4).reshape(2, 4 * G_C), me * 256, 256, axis=1),
    }
    small_w = {"norm_g": (norm_g, m_norm_g, v_norm_g), "ada_b": (ada_b, m_ada_b, v_ada_b),
               "ab_ln_g": (ab_ln_g, m_ab_ln_g, v_ab_ln_g), "ab_ln_b": (ab_ln_b, m_ab_ln_b, v_ab_ln_b),
               "ab_sgu_b": (ab_sgu_b, m_ab_sgu_b, v_ab_sgu_b),
               "final_g": (final_g.reshape(1, D), m_final_g.reshape(1, D), v_final_g.reshape(1, D)),
               "ab_sgu_w": (ab_sgu_w, m_ab_sgu_w, v_ab_sgu_w), "ab_conv_w": (ab_conv_w, m_ab_conv_w, v_ab_conv_w),
               "c_pool_scale": (c_pool_scale, m_c_pool_scale, v_c_pool_scale)}
    for k, (w, m, v) in small_w.items():
        res[k] = [grads[k]] + list(_adam_small(w, grads[k], m, v, "adam_" + k))
    res["final_g"] = [a.reshape(D) for a in res["final_g"]]

    dmod_all = section("d_mod", 3 * DEPTH, small_grads).reshape(NDEV, DEPTH, 3 * D)
    dmod_mine = jnp.transpose(lax.dynamic_slice_in_dim(dmod_all, me * ADA_NC, ADA_NC, axis=2), (1, 0, 2)).astype(F32)
    res["ada_w"] = _ada_weight_adam(jnp.transpose(cact_all.reshape(NDEV, D)), dmod_mine, ada_w, m_ada_w, v_ada_w)

    order = ["norm_g", "ada_w", "ada_b", "ab_w_in", "ab_conv_w", "ab_ln_g", "ab_ln_b", "ab_sgu_w", "ab_sgu_b",
             "ab_w_out", "c_w_in", "c_pool_w", "c_pool_scale", "c_w_out", "final_g"]
    return (loss, grad_x, *[res[k][0] for k in order], *[res[k][1] for k in order],
            *[res[k][2] for k in order], *[res[k][3] for k in order])
```

```python
import jax
import jax.numpy as jnp
from jax import lax
from jax.experimental import pallas as pl
from jax.experimental.pallas import tpu as pltpu

F32, BF16 = jnp.float32, jnp.bfloat16
S, D = 2048, 1024
NDEV = 8
DEPTH = 4
EPS = 1e-6
E_A = 1024
HEAD = 128
CHUNK = 128
POOL_WINDOWS = (2, 4, 8, 16)
G_C = 512
HALO = 16
ADA_NC = 384
MIB = 1024 * 1024
LANE = 128

ADAM_LR, ADAM_B1, ADAM_B2, ADAM_EPS, ADAM_WD, ADAM_STEP = 0.001, 0.9, 0.999, 1e-08, 0.01, 10

ANY = pl.BlockSpec(memory_space=pl.ANY)
VMEM_FULL = pl.BlockSpec(memory_space=pltpu.VMEM)
IN_HBM = pl.BlockSpec(memory_space=pltpu.HBM)
SEMAPHORES = pl.BlockSpec(memory_space=pltpu.SEMAPHORE)
IN_FLIGHT = pltpu.SideEffectType.DATAFLOW_SIDE_EFFECTING


V7X_VMEM_MIB = 64
VMEM_LIMIT_MIB = V7X_VMEM_MIB - 4


def _params(semantics=None):
    return pltpu.CompilerParams(dimension_semantics=semantics, vmem_limit_bytes=VMEM_LIMIT_MIB * MIB)


def _silu(z):
    return z * jax.nn.sigmoid(z)


def _silu_and_grad(z):
    sig = jax.nn.sigmoid(z)
    return z * sig, sig * (1.0 + z * (1.0 - sig))


def _position():
    return lax.axis_index("x"), lax.axis_index("y"), lax.axis_index("c")


def _index(pos):
    return 4 * pos[0] + 2 * pos[1] + pos[2]


def _peer(pos, k):
    flipped = tuple(1 - p if (k >> (2 - b)) & 1 else p for b, p in enumerate(pos))
    return flipped, _index(flipped)


def _remote(src, dst, send_sem, recv_sem, device):
    return pltpu.make_async_remote_copy(src_ref=src, dst_ref=dst, send_sem=send_sem, recv_sem=recv_sem,
                                        device_id=device, device_id_type=pl.DeviceIdType.MESH)


def _gather(arrays, name):
    n = len(arrays)
    out_shape = [jax.ShapeDtypeStruct((NDEV,) + a.shape, a.dtype) for a in arrays]

    def body(*refs):
        ins, outs = refs[:n], refs[n:2 * n]
        send_sems, recv_sems, own_sems = refs[2 * n:]
        x, y, c = _position()
        me = _index((x, y, c))
        sibling = (x, y, 1 - c)
        chips = [(1 - x, y), (x, 1 - y), (1 - x, 1 - y)]

        def block_copy(j, k, owner, to, src=None):
            rows = outs[j].at[_index(owner)]
            return _remote(rows if src is None else src, rows, send_sems.at[j, k], recv_sems.at[j, k], to)

        own, first, passed = [], [], []
        for j in range(n):
            own.append(pltpu.make_async_copy(ins[j], outs[j].at[me], own_sems.at[j]))
            first.append(block_copy(j, 0, (x, y, c), sibling, src=ins[j]))
            first += [block_copy(j, 1 + q, (x, y, c), (*chip, c), src=ins[j]) for q, chip in enumerate(chips)]
        for copy in own + first:
            copy.start()
        for q, chip in enumerate(chips):
            for j in range(n):
                block_copy(j, 1 + q, (*chip, c), (x, y, c)).wait_recv()
                forward = block_copy(j, 4 + q, (*chip, c), sibling)
                forward.start()
                passed.append(forward)
        for j in range(n):
            block_copy(j, 0, sibling, (x, y, c)).wait_recv()
            for q, chip in enumerate(chips):
                block_copy(j, 4 + q, (*chip, 1 - c), (x, y, c)).wait_recv()
        for copy in first + passed:
            copy.wait_send()
        for copy in own:
            copy.wait()

    return pl.pallas_call(
        body, name=name, out_shape=out_shape, in_specs=[ANY] * n, out_specs=[ANY] * n,
        scratch_shapes=[pltpu.SemaphoreType.DMA((n, NDEV - 1)), pltpu.SemaphoreType.DMA((n, NDEV - 1)),
                        pltpu.SemaphoreType.DMA((n,))],
    )(*arrays)


def _pair_start(parts, name):
    n = len(parts)
    lands = [_in_hbm(lax.empty(p.shape[1:], p.dtype)) for p in parts]

    def body(*refs):
        ins, zones = refs[:n], refs[n:2 * n]
        send_sems, recv_sems = refs[2 * n:2 * n + 2]
        x, y, c = _position()
        for j in range(n):
            _remote(ins[j].at[1 - c], zones[j], send_sems.at[j], recv_sems.at[j], (x, y, 1 - c)).start()

    outs = pl.pallas_call(
        body, name=name,
        out_shape=(pltpu.SemaphoreType.DMA((n,)), pltpu.SemaphoreType.DMA((n,)),
                   *[pltpu.HBM(p.shape, p.dtype) for p in parts], *[pltpu.HBM(z.shape, z.dtype) for z in lands]),
        in_specs=[IN_HBM] * (2 * n), out_specs=(SEMAPHORES, SEMAPHORES, *[IN_HBM] * (2 * n)),
        input_output_aliases={j: 2 + j for j in range(2 * n)},
        compiler_params=pltpu.CompilerParams(has_side_effects=IN_FLIGHT),
    )(*[_in_hbm(p) for p in parts], *lands)
    return outs[0], outs[1], list(outs[2:2 + n]), list(outs[2 + n:])


def _pair_end(parts, zones, send_sems, recv_sems, after, name):
    n = len(parts)

    def body(*refs):
        ins, zs = refs[:n], refs[n:2 * n]
        s, r = refs[2 * n:2 * n + 2]
        me = _position()
        for j in range(n):
            copy = _remote(ins[j].at[0], zs[j], s.at[j], r.at[j], me)
            copy.wait_send()
            copy.wait_recv()

    outs = pl.pallas_call(
        body, name=name,
        out_shape=(*[pltpu.HBM(p.shape, p.dtype) for p in parts], *[pltpu.HBM(z.shape, z.dtype) for z in zones]),
        in_specs=[IN_HBM] * (2 * n) + [SEMAPHORES, SEMAPHORES, ANY], out_specs=tuple([IN_HBM] * (2 * n)),
        input_output_aliases={j: j for j in range(2 * n)},
        compiler_params=pltpu.CompilerParams(has_side_effects=IN_FLIGHT),
    )(*parts, *zones, send_sems, recv_sems, after)
    return list(outs[:n]), list(outs[n:])


def _pair_sum(part, from_sibling, core, name):
    ncol = part.shape[-1]
    p3 = part.reshape(2, -1, ncol)
    q2 = from_sibling.reshape(-1, ncol)
    nrows = q2.shape[0]
    tr = 512

    def body(core_ref, p_ref, q_ref, o_ref):
        o_ref[...] = (p_ref[...].astype(F32) + q_ref[...].astype(F32)).astype(BF16)

    grid_spec = pltpu.PrefetchScalarGridSpec(
        num_scalar_prefetch=1, grid=(nrows // tr,),
        in_specs=[pl.BlockSpec((None, tr, ncol), lambda i, core_ref: (core_ref[0], i, 0)),
                  pl.BlockSpec((tr, ncol), lambda i, core_ref: (i, 0))],
        out_specs=pl.BlockSpec((tr, ncol), lambda i, core_ref: (i, 0)))
    out = pl.pallas_call(body, name=name, grid_spec=grid_spec, out_shape=jax.ShapeDtypeStruct(q2.shape, BF16),
                         compiler_params=_params(("arbitrary",)))(core, p3, q2)
    return out.reshape(from_sibling.shape)


def _in_hbm(a):
    return pltpu.with_memory_space_constraint(a, pltpu.HBM)


def _chips(x, y):
    return [(1 - x, y), (x, 1 - y), (1 - x, 1 - y)]


def _to_zone(a, wl, me, dtype, name):
    _, rows, cols = a.shape
    tr = 256 if rows % 256 == 0 else rows

    def body(me_ref, a_ref, o_ref):
        o_ref[...] = a_ref[...].astype(dtype)

    grid_spec = pltpu.PrefetchScalarGridSpec(
        num_scalar_prefetch=1, grid=(rows // tr,),
        in_specs=[pl.BlockSpec((None, tr, cols), lambda i, me_ref: (wl, i, 0))],
        out_specs=pl.BlockSpec((None, tr, cols), lambda i, me_ref: (me_ref[0], i, 0)))
    return pl.pallas_call(body, name=name, grid_spec=grid_spec, out_shape=jax.ShapeDtypeStruct((NDEV, rows, cols), dtype),
                          compiler_params=_params(("arbitrary",)))(me, a)


def _halves(block):
    rows = block.shape[0] // 2
    return block.at[pl.ds(0, rows)], block.at[pl.ds(rows, rows)]


def _around(x, y, c):
    return (x, y, 1 - c), (1 - x, y, c), (x, 1 - y, c), (1 - x, 1 - y, c)


def _gather_step1(zs, send, recv, pos):
    sibling, xn, yn, _ = _around(*pos)
    for j, z in enumerate(zs):
        mine = z.at[_index(pos)]
        for k, peer in enumerate((sibling, xn, yn)):
            _remote(mine, mine, send.at[3 * j + k], recv.at[3 * j + k], peer).start()


def _gather_step2(zs, recv1, send, recv, pos):
    sibling, xn, yn, _ = _around(*pos)
    for j, z in enumerate(zs):
        xb, yb = z.at[_index(xn)], z.at[_index(yn)]
        _remote(xb, xb, send.at[4 * j], recv1.at[3 * j + 1], pos).wait_recv()
        _remote(yb, yb, send.at[4 * j], recv1.at[3 * j + 2], pos).wait_recv()
        _remote(xb, xb, send.at[4 * j], recv.at[4 * j], sibling).start()
        _remote(yb, yb, send.at[4 * j + 1], recv.at[4 * j + 1], sibling).start()
        first, second = _halves(xb)[0], _halves(yb)[1]
        _remote(first, first, send.at[4 * j + 2], recv.at[4 * j + 2], yn).start()
        _remote(second, second, send.at[4 * j + 3], recv.at[4 * j + 3], xn).start()


def _gather_step3(zs, recv2, send, recv, pos):
    sibling, _, _, diagonal = _around(*pos)
    for j, z in enumerate(zs):
        db = z.at[_index(diagonal)]
        first, second = _halves(db)
        _remote(first, first, send.at[j], recv2.at[4 * j + 2], pos).wait_recv()
        _remote(second, second, send.at[j], recv2.at[4 * j + 3], pos).wait_recv()
        _remote(db, db, send.at[j], recv.at[j], sibling).start()


def _gather_step4(zs, send1, recv1, send2, recv2, send3, recv3, pos):
    x, y, c = pos
    sibling = (x, y, 1 - c)
    _, sx, sy, sd = _around(*sibling)
    for j, z in enumerate(zs):
        for owner, send, recv, k in ((sibling, send1, recv1, 3 * j), (sx, send2, recv2, 4 * j), (sy, send2, recv2, 4 * j + 1),
                                     (sd, send3, recv3, j)):
            block = z.at[_index(owner)]
            _remote(block, block, send.at[k], recv.at[k], pos).wait_recv()
    for j, z in enumerate(zs):
        block = z.at[0]
        half = _halves(block)[0]
        for ref, send, recv, k in ([(block, send1, recv1, 3 * j + k) for k in range(3)]
                                   + [(block, send2, recv2, 4 * j), (block, send2, recv2, 4 * j + 1),
                                      (half, send2, recv2, 4 * j + 2), (half, send2, recv2, 4 * j + 3), (block, send3, recv3, j)]):
            _remote(ref, ref, send.at[k], recv.at[k], pos).wait_send()


def _flight_call(step, name, zones, sems_in, nsems_out, after, carried):
    n, m, k = len(zones), len(carried), len(sems_in)

    def body(*refs):
        zs = refs[:n]
        given = refs[n + m:n + m + k]
        made = refs[n + m + k + len(after):n + m + k + len(after) + (2 if nsems_out else 0)]
        step(zs, *given, *made, _position())

    sem_out = (pltpu.SemaphoreType.DMA((nsems_out,)),) * 2 if nsems_out else ()
    outs = pl.pallas_call(
        body, name=name,
        out_shape=(*sem_out, *[pltpu.HBM(z.shape, z.dtype) for z in zones], *[jax.ShapeDtypeStruct(a.shape, a.dtype) for a in carried]),
        in_specs=[IN_HBM] * n + [ANY] * m + [SEMAPHORES] * k + [ANY] * len(after),
        out_specs=(*[SEMAPHORES] * len(sem_out), *[IN_HBM] * n, *[ANY] * m),
        input_output_aliases={j: len(sem_out) + j for j in range(n + m)},
        compiler_params=pltpu.CompilerParams(has_side_effects=IN_FLIGHT),
    )(*[_in_hbm(z) for z in zones], *carried, *sems_in, *after)
    sems = list(outs[:len(sem_out)])
    return sems, list(outs[len(sem_out):len(sem_out) + n]), list(outs[len(sem_out) + n:])


def _gather_start(zones, after, carried, name):
    (send1, recv1), zones, carried = _flight_call(_gather_step1, name, zones, [], 3 * len(zones), after, carried)
    return {"s1": send1, "r1": recv1, "zones": zones}, carried


def _gather_mid(flight, after, carried, name):
    step = lambda zs, recv1, send, recv, pos: _gather_step2(zs, recv1, send, recv, pos)
    (send2, recv2), zones, carried = _flight_call(step, name, flight["zones"], [flight["r1"]], 4 * len(flight["zones"]), after, carried)
    return {**flight, "s2": send2, "r2": recv2, "zones": zones}, carried


def _gather_late(flight, after, name):
    step = lambda zs, recv2, send, recv, pos: _gather_step3(zs, recv2, send, recv, pos)
    (send3, recv3), zones, _ = _flight_call(step, name, flight["zones"], [flight["r2"]], len(flight["zones"]), after, [])
    return {**flight, "s3": send3, "r3": recv3, "zones": zones}


def _gather_end(flight, after, name):
    sems = [flight[k] for k in ("s1", "r1", "s2", "r2", "s3", "r3")]
    _, zones, _ = _flight_call(_gather_step4, name, flight["zones"], sems, 0, after, [])
    return zones


def _scatter_start(parts, name):
    n = len(parts)
    lands = [_in_hbm(lax.empty((3,) + p.shape[1:], p.dtype)) for p in parts]

    def body(*refs):
        ins, zones = refs[:n], refs[n:2 * n]
        send_sems, recv_sems = refs[2 * n:2 * n + 2]
        x, y, c = _position()
        for j in range(n):
            for q, (px, py) in enumerate(_chips(x, y)):
                _remote(ins[j].at[2 * px + py], zones[j].at[q], send_sems.at[3 * j + q], recv_sems.at[3 * j + q],
                        (px, py, c)).start()

    outs = pl.pallas_call(
        body, name=name,
        out_shape=(pltpu.SemaphoreType.DMA((3 * n,)), pltpu.SemaphoreType.DMA((3 * n,)),
                   *[pltpu.HBM(p.shape, p.dtype) for p in parts], *[pltpu.HBM(z.shape, z.dtype) for z in lands]),
        in_specs=[IN_HBM] * (2 * n), out_specs=(SEMAPHORES, SEMAPHORES, *[IN_HBM] * (2 * n)),
        input_output_aliases={j: 2 + j for j in range(2 * n)},
        compiler_params=pltpu.CompilerParams(has_side_effects=IN_FLIGHT),
    )(*[_in_hbm(p) for p in parts], *lands)
    return outs[0], outs[1], list(outs[2:2 + n]), list(outs[2 + n:])


def _scatter_end(parts, zones, send_sems, recv_sems, after, name):
    n = len(parts)

    def body(*refs):
        ins, zs = refs[:n], refs[n:2 * n]
        s, r = refs[2 * n:2 * n + 2]
        me = _position()
        for j in range(n):
            for q in range(3):
                copy = _remote(ins[j].at[0], zs[j].at[q], s.at[3 * j + q], r.at[3 * j + q], me)
                copy.wait_send()
                copy.wait_recv()

    outs = pl.pallas_call(
        body, name=name,
        out_shape=(*[pltpu.HBM(p.shape, p.dtype) for p in parts], *[pltpu.HBM(z.shape, z.dtype) for z in zones]),
        in_specs=[IN_HBM] * (2 * n) + [SEMAPHORES, SEMAPHORES, ANY], out_specs=tuple([IN_HBM] * (2 * n)),
        input_output_aliases={j: j for j in range(2 * n)},
        compiler_params=pltpu.CompilerParams(has_side_effects=IN_FLIGHT),
    )(*parts, *zones, send_sems, recv_sems, after)
    return list(outs[:n]), list(outs[n:])


def _ada_forward(c, ada_w, ada_b):
    def body(c_ref, w_ref, b_ref, cact_ref, mod_ref, gbuf, modrow, send_sems, recv_sems):
        pos = _position()
        me = _index(pos)

        def to_all(ref, round_):
            copies = []
            for k in range(1, NDEV):
                peer, _ = _peer(pos, k)
                copy = pltpu.make_async_remote_copy(
                    src_ref=ref.at[me], dst_ref=ref.at[me], send_sem=send_sems.at[round_, k - 1],
                    recv_sem=recv_sems.at[round_, k - 1], device_id=peer, device_id_type=pl.DeviceIdType.MESH)
                copy.start()
                copies.append(copy)
            for copy in copies:
                copy.wait()

        cact_ref[me] = _silu(c_ref[...])
        to_all(cact_ref, 0)
        rows = lax.broadcasted_iota(jnp.int32, (NDEV, D), 0)
        cact = jnp.zeros((NDEV, D), F32)
        for e in range(NDEV):
            cact = jnp.where(rows == e, cact_ref[e], cact)
        cact = cact.astype(BF16)
        for l in range(DEPTH):
            gbuf[me, l] = jnp.dot(cact, w_ref[l].astype(BF16), preferred_element_type=F32)
        to_all(gbuf, 1)
        mine = lax.broadcasted_iota(jnp.int32, (NDEV, ADA_NC), 0) == me
        for l in range(DEPTH):
            for d in range(NDEV):
                modrow[:, d * ADA_NC:(d + 1) * ADA_NC] = jnp.sum(jnp.where(mine, gbuf[d, l], 0.0), axis=0, keepdims=True)
            full = modrow[...] + b_ref[l:l + 1, :]
            for w in range(3):
                mod_ref[l, w] = full[:, w * D:(w + 1) * D]

    return pl.pallas_call(
        body, name="ada_forward",
        out_shape=[jax.ShapeDtypeStruct((NDEV, 1, D), F32), jax.ShapeDtypeStruct((DEPTH, 3, 1, D), F32)],
        in_specs=[VMEM_FULL] * 3, out_specs=[VMEM_FULL] * 2,
        scratch_shapes=[pltpu.VMEM((NDEV, DEPTH, NDEV, ADA_NC), F32), pltpu.VMEM((1, 3 * D), F32),
                        pltpu.SemaphoreType.DMA((2, NDEV - 1)), pltpu.SemaphoreType.DMA((2, NDEV - 1))],
        compiler_params=_params(),
    )(c, ada_w, ada_b)


def _mod_spec(layer, which, ngrid):
    index = {1: lambda i: (layer, which, 0, 0), 2: lambda i, j: (layer, which, 0, 0)}[ngrid]
    return pl.BlockSpec((None, None, 1, D), index)


W_BLOCKS = 4


def _norm_proj(x, mod, norm_g3, wg, layer, name):
    nb = wg.shape[-1]
    tm = 1024
    wb = W_BLOCKS

    def body(x_ref, g_ref, shift_ref, scale_ref, w_ref, ht_ref, p_ref, h_ref):
        @pl.when(pl.program_id(1) == 0)
        def _():
            xv = x_ref[...]
            r = lax.rsqrt(jnp.mean(xv * xv, axis=-1, keepdims=True) + EPS)
            hn = xv * r * g_ref[...]
            h = hn * (1.0 + scale_ref[...]) + shift_ref[...]
            h_ref[...] = h.astype(BF16)
            ht_ref[...] = h.T.astype(BF16)

        hv = h_ref[...]
        for b in range(wb):
            p_ref[:, b * nb:(b + 1) * nb] = jnp.dot(hv, w_ref[b], preferred_element_type=F32).astype(BF16)

    return pl.pallas_call(
        body, name=name, grid=(S // tm, NDEV // wb),
        out_shape=[jax.ShapeDtypeStruct((D, S), BF16), jax.ShapeDtypeStruct((S, NDEV * nb), BF16)],
        in_specs=[pl.BlockSpec((tm, D), lambda i, d: (i, 0)),
                  pl.BlockSpec((None, 1, D), lambda i, d: (layer, 0, 0)),
                  _mod_spec(layer, 0, 2), _mod_spec(layer, 1, 2),
                  pl.BlockSpec((wb, D, nb), lambda i, d: (d, 0, 0))],
        out_specs=[pl.BlockSpec((D, tm), lambda i, d: (0, i)), pl.BlockSpec((tm, wb * nb), lambda i, d: (i, d))],
        scratch_shapes=[pltpu.VMEM((tm, D), BF16)],
        compiler_params=_params(("arbitrary", "arbitrary")),
    )(x, norm_g3, mod, mod, wg)


def _out_proj(ycat, w_out, x, mod, layer, name):
    tm = 512
    e = w_out.shape[0]

    def body(y_ref, w_ref, x_ref, gate_ref, xn_ref, o_ref):
        acc = jnp.dot(y_ref[...], w_ref[...], preferred_element_type=F32)
        o_ref[...] = acc.astype(BF16)
        xn_ref[...] = x_ref[...] + gate_ref[...] * acc

    return pl.pallas_call(
        body, name=name, grid=(S // tm,),
        out_shape=[jax.ShapeDtypeStruct((S, D), F32), jax.ShapeDtypeStruct((S, D), BF16)],
        in_specs=[pl.BlockSpec((tm, e), lambda i: (i, 0)), pl.BlockSpec((e, D), lambda i: (0, 0)),
                  pl.BlockSpec((tm, D), lambda i: (i, 0)), _mod_spec(layer, 2, 1)],
        out_specs=[pl.BlockSpec((tm, D), lambda i: (i, 0))] * 2,
        compiler_params=_params(("arbitrary",)),
    )(ycat, w_out, x, mod)


def _final_loss(x, target, final_g2):
    tm = 256

    def body(x_ref, t_ref, g_ref, dx_ref, loss_ref, dg_ref):
        @pl.when(pl.program_id(0) == 0)
        def _():
            loss_ref[...] = jnp.zeros_like(loss_ref)
            dg_ref[...] = jnp.zeros_like(dg_ref)

        xv, g = x_ref[...], g_ref[...]
        r = lax.rsqrt(jnp.mean(xv * xv, axis=-1, keepdims=True) + EPS)
        xn = xv * r
        err = xn * g - t_ref[...]
        loss_ref[...] += 0.5 * jnp.sum(jnp.mean(err * err, axis=-1, keepdims=True), axis=0, keepdims=True)
        dy = err * (1.0 / D)
        dg_ref[...] += jnp.sum(dy * xn, axis=0, keepdims=True)
        u = dy * g
        dx_ref[...] = r * (u - xn * jnp.mean(xn * u, axis=-1, keepdims=True))

    tile = pl.BlockSpec((tm, D), lambda i: (i, 0))
    row = pl.BlockSpec((1, D), lambda i: (0, 0))
    return pl.pallas_call(
        body, name="final_loss", grid=(S // tm,),
        out_shape=[jax.ShapeDtypeStruct((S, D), F32), jax.ShapeDtypeStruct((1, LANE), F32), jax.ShapeDtypeStruct((1, D), F32)],
        in_specs=[tile, tile, row], out_specs=[tile, pl.BlockSpec((1, LANE), lambda i: (0, 0)), row],
        compiler_params=_params(("arbitrary",)),
    )(x, target, final_g2)


def _out_bwd(dx, out, ycat, w_out, mod, layer, carried, name):
    tm = 512
    nsteps = S // tm
    e = ycat.shape[1]
    rb = e // NDEV
    nc = len(carried)

    def body(dx_ref, o_ref, y_ref, w_ref, gate_ref, *rest):
        dy_ref, gw_ref, dgate_ref = rest[nc:nc + 3]
        acc = rest[-1]
        step = pl.program_id(0)

        @pl.when(step == 0)
        def _():
            dgate_ref[...] = jnp.zeros_like(dgate_ref)
            acc[...] = jnp.zeros_like(acc)

        dxv = dx_ref[...]
        d_out = (gate_ref[...] * dxv).astype(BF16)
        dgate_ref[...] += jnp.sum(dxv * o_ref[...].astype(F32), axis=0, keepdims=True)
        dy_ref[...] = lax.dot_general(d_out, w_ref[...], (((1,), (1,)), ((), ())), preferred_element_type=F32).astype(BF16)
        acc[...] += lax.dot_general(y_ref[...], d_out, (((0,), (0,)), ((), ())), preferred_element_type=F32)

        @pl.when(step == nsteps - 1)
        def _():
            for d in range(NDEV):
                gw_ref[d % 2, d // 2] = acc[d * rb:(d + 1) * rb, :].astype(BF16)

    tile = pl.BlockSpec((tm, D), lambda i: (i, 0))
    wide = pl.BlockSpec((tm, e), lambda i: (i, 0))
    outs = pl.pallas_call(
        body, name=name, grid=(nsteps,),
        out_shape=[jax.ShapeDtypeStruct((S, e), BF16), jax.ShapeDtypeStruct((2, NDEV // 2, rb, D), BF16),
                   jax.ShapeDtypeStruct((1, D), F32)] + [jax.ShapeDtypeStruct(a.shape, a.dtype) for a in carried],
        in_specs=[tile, tile, wide, pl.BlockSpec((e, D), lambda i: (0, 0)), _mod_spec(layer, 2, 1)] + [ANY] * nc,
        out_specs=[wide, pl.BlockSpec((2, NDEV // 2, rb, D), lambda i: (0, 0, 0, 0)), pl.BlockSpec((1, D), lambda i: (0, 0))]
        + [ANY] * nc,
        scratch_shapes=[pltpu.VMEM((e, D), F32)],
        input_output_aliases={5 + k: 3 + k for k in range(nc)},
        compiler_params=_params(("arbitrary",)),
    )(dx, out, ycat, w_out, mod, *carried)
    return outs[0], outs[1], outs[2], list(outs[3:])


def _weight_grad(h_t, d_proj, name):
    nb = d_proj.shape[1] // NDEV

    def body(ht_ref, dp_ref, o_ref):
        o_ref[...] = jnp.dot(ht_ref[...], dp_ref[...], preferred_element_type=F32).astype(BF16)

    return pl.pallas_call(
        body, name=name, grid=(NDEV,), out_shape=jax.ShapeDtypeStruct((2, NDEV // 2, D, nb), BF16),
        in_specs=[pl.BlockSpec((D, S), lambda d: (0, 0)), pl.BlockSpec((S, nb), lambda d: (0, d))],
        out_specs=pl.BlockSpec((None, None, D, nb), lambda d: (d % 2, d // 2, 0, 0)),
        compiler_params=_params(("arbitrary",)),
    )(h_t, d_proj)


def _dh_norm_bwd(d_proj, wg, x, dx, mod, norm_g3, layer, carried, name):
    nb = wg.shape[-1]
    tm = 512
    wb = W_BLOCKS
    rc = 128

    def body(dp_ref, w_ref, x_ref, dx_ref, g_ref, scale_ref, carried_ref,
             dxi_ref, dshift_ref, dscale_ref, dg_ref, carried_out, acc):
        i, d = pl.program_id(0), pl.program_id(1)
        nt = (((1,), (1,)), ((), ()))
        part = lax.dot_general(dp_ref[:, :nb], w_ref[0], nt, preferred_element_type=F32)
        for b in range(1, wb):
            part += lax.dot_general(dp_ref[:, b * nb:(b + 1) * nb], w_ref[b], nt, preferred_element_type=F32)

        @pl.when(d == 0)
        def _():
            acc[...] = part

        @pl.when(d != 0)
        def _():
            acc[...] += part

        @pl.when(jnp.logical_and(i == 0, d == 0))
        def _():
            dshift_ref[...] = jnp.zeros_like(dshift_ref)
            dscale_ref[...] = jnp.zeros_like(dscale_ref)
            dg_ref[...] = jnp.zeros_like(dg_ref)

        @pl.when(d == NDEV // wb - 1)
        def _():
            g = g_ref[...]
            scale1 = 1.0 + scale_ref[...]

            def chunk(k, sums):
                rows = pl.ds(pl.multiple_of(k * rc, rc), rc)
                xv, dhv = x_ref[rows, :], acc[rows, :]
                r = lax.rsqrt(jnp.mean(xv * xv, axis=-1, keepdims=True) + EPS)
                xn = xv * r
                dhn = dhv * scale1
                u = dhn * g
                dxi_ref[rows, :] = dx_ref[rows, :] + r * (u - xn * jnp.mean(xn * u, axis=-1, keepdims=True))
                return (sums[0] + jnp.sum(dhv, axis=0, keepdims=True),
                        sums[1] + jnp.sum(dhv * (xn * g), axis=0, keepdims=True),
                        sums[2] + jnp.sum(dhn * xn, axis=0, keepdims=True))

            zero = jnp.zeros((1, D), F32)
            sums = lax.fori_loop(0, tm // rc, chunk, (zero, zero, zero))
            dshift_ref[...] += sums[0]
            dscale_ref[...] += sums[1]
            dg_ref[...] += sums[2]

    tile = pl.BlockSpec((tm, D), lambda i, d: (i, 0))
    row = pl.BlockSpec((1, D), lambda i, d: (0, 0))
    return pl.pallas_call(
        body, name=name, grid=(S // tm, NDEV // wb),
        out_shape=[jax.ShapeDtypeStruct((S, D), F32)] + [jax.ShapeDtypeStruct((1, D), F32)] * 3
        + [jax.ShapeDtypeStruct(carried.shape, carried.dtype)],
        in_specs=[pl.BlockSpec((tm, wb * nb), lambda i, d: (i, d)), pl.BlockSpec((wb, D, nb), lambda i, d: (d, 0, 0)),
                  tile, tile, pl.BlockSpec((None, 1, D), lambda i, d: (layer, 0, 0)), _mod_spec(layer, 1, 2), ANY],
        out_specs=[tile, row, row, row, ANY], scratch_shapes=[pltpu.VMEM((tm, D), F32)],
        input_output_aliases={6: 4}, compiler_params=_params(("arbitrary", "arbitrary")),
    )(d_proj, wg, x, dx, norm_g3, mod, carried)


TS = 256
NCH = TS // CHUNK
HALO_BLOCKS = TS // HALO


def _halo_before(width, col_block):
    return pl.BlockSpec((HALO, width), lambda i: (jnp.maximum(i * HALO_BLOCKS - 1, 0), col_block))


def _halo_after(width, col_block):
    return pl.BlockSpec((HALO, width), lambda i: (jnp.minimum((i + 1) * HALO_BLOCKS, S // HALO - 1), col_block))


def _shift_down(ext, k):
    return pltpu.roll(ext, k, 0)[HALO:]


def _shift_up(ext, k):
    return pltpu.roll(ext, ext.shape[0] - k, 0)[:ext.shape[0] - HALO]


def _layer_norm_head(v, lg, lb):
    mu = jnp.mean(v, axis=-1, keepdims=True)
    vc = v - mu
    rstd = lax.rsqrt(jnp.mean(vc * vc, axis=-1, keepdims=True) + EPS)
    vhat = vc * rstd
    return vhat, rstd, vhat * lg + lb


def _causal_mask():
    return lax.broadcasted_iota(jnp.int32, (CHUNK, CHUNK), 0) >= lax.broadcasted_iota(jnp.int32, (CHUNK, CHUNK), 1)


def _even_mix_fwd(proj, convw, ln_g3, ln_b3, sgu_w, sgu_bcol, wl, after, name):
    def body(pj_ref, hh_ref, hc_ref, cw_ref, lg_ref, lb_ref, sw_ref, sb_ref, *rest):
        y_ref = rest[-1]
        live = (pl.program_id(0) > 0).astype(F32)
        causal = _causal_mask()
        for j in range(E_A // HEAD):
            cols = slice(j * HEAD, (j + 1) * HEAD)
            w0, w1, w2 = cw_ref[0:1, cols], cw_ref[1:2, cols], cw_ref[2:3, cols]
            lg, lb = lg_ref[:, cols], lb_ref[:, cols]
            wm = jnp.where(causal, sw_ref[j], 0.0).astype(BF16)
            bias = sb_ref[j]

            def split(s, rows, cols=cols):
                return pj_ref[rows, s * E_A + cols.start:s * E_A + cols.stop].astype(F32)

            prev_tail = hc_ref[:, cols].astype(F32) * hh_ref[:, cols].astype(F32) * live
            for n in range(NCH):
                rows = slice(n * CHUNK, (n + 1) * CHUNK)
                p = split(2, rows) * split(0, rows)
                ext = jnp.concatenate([prev_tail, p], axis=0)
                prev_tail = p[CHUNK - HALO:]
                cv = w2 * p + w1 * _shift_down(ext, 1) + w0 * _shift_down(ext, 2)
                y_ref[rows, cols] = (split(1, rows) * cv * _silu(split(3, rows))).astype(BF16)
                _, _, vn = _layer_norm_head(split(5, rows), lg, lb)
                mixed = jnp.dot(wm, vn.astype(BF16), preferred_element_type=F32) + bias
                y_ref[rows, E_A + cols.start:E_A + cols.stop] = (split(4, rows) * mixed * _silu(split(6, rows))).astype(BF16)

    const3 = lambda i: (wl, 0, 0)
    const4 = lambda i: (wl, 0, 0, 0)
    return pl.pallas_call(
        body, name=name, grid=(S // TS,), out_shape=jax.ShapeDtypeStruct((S, 2 * E_A), BF16),
        in_specs=[pl.BlockSpec((TS, 7 * E_A), lambda i: (i, 0)), _halo_before(E_A, 0), _halo_before(E_A, 2),
                  pl.BlockSpec((None, 3, E_A), const3), pl.BlockSpec((None, 1, E_A), const3),
                  pl.BlockSpec((None, 1, E_A), const3), pl.BlockSpec((None, NDEV, CHUNK, CHUNK), const4),
                  pl.BlockSpec((None, NDEV, CHUNK, 1), const4)] + [ANY] * len(after),
        out_specs=pl.BlockSpec((TS, 2 * E_A), lambda i: (i, 0)),
        compiler_params=_params(("arbitrary",)),
    )(proj, proj, proj, convw, ln_g3, ln_b3, sgu_w, sgu_bcol, *after)


def _even_mix_bwd(proj, d_ycat, convw, ln_g3, ln_b3, sgu_w, sgu_bcol, wl, name):
    nsteps = S // TS

    def body(pj_ref, hh_ref, hc_ref, hb_ref, hz_ref, dy_ref, hdy_ref, cw_ref, lg_ref, lb_ref, sw_ref, sb_ref,
             dp_ref, dcw_ref, dlg_ref, dlb_ref, dsw_ref, dsb_ref):
        step = pl.program_id(0)

        @pl.when(step == 0)
        def _():
            for ref in (dcw_ref, dlg_ref, dlb_ref, dsw_ref, dsb_ref):
                ref[...] = jnp.zeros_like(ref)

        live_before = (step > 0).astype(F32)
        live_after = (step < nsteps - 1).astype(F32)
        causal = _causal_mask()
        for j in range(E_A // HEAD):
            cols = slice(j * HEAD, (j + 1) * HEAD)
            w0, w1, w2 = cw_ref[0:1, cols], cw_ref[1:2, cols], cw_ref[2:3, cols]
            lg, lb = lg_ref[:, cols], lb_ref[:, cols]
            wmf = jnp.where(causal, sw_ref[j], 0.0)
            wm, wmt = wmf.astype(BF16), wmf.T.astype(BF16)
            bias = sb_ref[j]

            def split(s, rows, cols=cols):
                return pj_ref[rows, s * E_A + cols.start:s * E_A + cols.stop].astype(F32)

            def put(s, rows, val, cols=cols):
                dp_ref[rows, s * E_A + cols.start:s * E_A + cols.stop] = val.astype(BF16)

            ps = [split(2, slice(n * CHUNK, (n + 1) * CHUNK)) * split(0, slice(n * CHUNK, (n + 1) * CHUNK)) for n in range(NCH)]
            next_head = (hdy_ref[:, cols].astype(F32) * hb_ref[:, cols].astype(F32) * _silu(hz_ref[:, cols].astype(F32))
                         * live_after)
            acc_w = [jnp.zeros((1, HEAD), F32) for _ in range(3)]
            for n in reversed(range(NCH)):
                rows = slice(n * CHUNK, (n + 1) * CHUNK)
                p = ps[n]
                tail = ps[n - 1][CHUNK - HALO:] if n > 0 else hc_ref[:, cols].astype(F32) * hh_ref[:, cols].astype(F32) * live_before
                ext = jnp.concatenate([tail, p], axis=0)
                p1, p2 = _shift_down(ext, 1), _shift_down(ext, 2)
                cv = w2 * p + w1 * p1 + w0 * p2
                a_b, a_z = split(1, rows), split(3, rows)
                sz, dsz = _silu_and_grad(a_z)
                dya = dy_ref[rows, cols].astype(F32)
                put(1, rows, dya * cv * sz)
                put(3, rows, dya * a_b * cv * dsz)
                gcv = dya * a_b * sz
                acc_w[0] += jnp.sum(gcv * p2, axis=0, keepdims=True)
                acc_w[1] += jnp.sum(gcv * p1, axis=0, keepdims=True)
                acc_w[2] += jnp.sum(gcv * p, axis=0, keepdims=True)
                gext = jnp.concatenate([gcv, next_head], axis=0)
                next_head = gcv[:HALO]
                dpv = w2 * gcv + w1 * _shift_up(gext, 1) + w0 * _shift_up(gext, 2)
                put(2, rows, dpv * split(0, rows))
                put(0, rows, dpv * split(2, rows))
            for k in range(3):
                dcw_ref[k:k + 1, cols] += acc_w[k]

            acc_lg, acc_lb = jnp.zeros((1, HEAD), F32), jnp.zeros((1, HEAD), F32)
            acc_sw, acc_sb = jnp.zeros((CHUNK, CHUNK), F32), jnp.zeros((CHUNK, 1), F32)
            for n in range(NCH):
                rows = slice(n * CHUNK, (n + 1) * CHUNK)
                u, z = split(4, rows), split(6, rows)
                vhat, rstd, vn = _layer_norm_head(split(5, rows), lg, lb)
                vn16 = vn.astype(BF16)
                mixed = jnp.dot(wm, vn16, preferred_element_type=F32) + bias
                sz, dsz = _silu_and_grad(z)
                dyb = dy_ref[rows, E_A + cols.start:E_A + cols.stop].astype(F32)
                put(4, rows, dyb * mixed * sz)
                put(6, rows, dyb * u * mixed * dsz)
                dmix = dyb * u * sz
                dmix16 = dmix.astype(BF16)
                acc_sb += jnp.sum(dmix, axis=1, keepdims=True)
                acc_sw += lax.dot_general(dmix16, vn16, (((1,), (1,)), ((), ())), preferred_element_type=F32)
                dvn = jnp.dot(wmt, dmix16, preferred_element_type=F32)
                acc_lg += jnp.sum(dvn * vhat, axis=0, keepdims=True)
                acc_lb += jnp.sum(dvn, axis=0, keepdims=True)
                dvh = dvn * lg
                put(5, rows, rstd * (dvh - jnp.mean(dvh, axis=-1, keepdims=True)
                                     - vhat * jnp.mean(dvh * vhat, axis=-1, keepdims=True)))
            dlg_ref[:, cols] += acc_lg
            dlb_ref[:, cols] += acc_lb
            dsw_ref[j] += jnp.where(causal, acc_sw, 0.0)
            dsb_ref[j] += acc_sb

    const3 = lambda i: (wl, 0, 0)
    const4 = lambda i: (wl, 0, 0, 0)
    fixed2 = lambda i: (0, 0)
    fixed3 = lambda i: (0, 0, 0)
    return pl.pallas_call(
        body, name=name, grid=(nsteps,),
        out_shape=[jax.ShapeDtypeStruct((S, 7 * E_A), BF16), jax.ShapeDtypeStruct((3, E_A), F32),
                   jax.ShapeDtypeStruct((1, E_A), F32), jax.ShapeDtypeStruct((1, E_A), F32),
                   jax.ShapeDtypeStruct((NDEV, CHUNK, CHUNK), F32), jax.ShapeDtypeStruct((NDEV, CHUNK, 1), F32)],
        in_specs=[pl.BlockSpec((TS, 7 * E_A), lambda i: (i, 0)), _halo_before(E_A, 0), _halo_before(E_A, 2),
                  _halo_after(E_A, 1), _halo_after(E_A, 3),
                  pl.BlockSpec((TS, 2 * E_A), lambda i: (i, 0)), _halo_after(E_A, 0),
                  pl.BlockSpec((None, 3, E_A), const3), pl.BlockSpec((None, 1, E_A), const3),
                  pl.BlockSpec((None, 1, E_A), const3), pl.BlockSpec((None, NDEV, CHUNK, CHUNK), const4),
                  pl.BlockSpec((None, NDEV, CHUNK, 1), const4)],
        out_specs=[pl.BlockSpec((TS, 7 * E_A), lambda i: (i, 0)), pl.BlockSpec((3, E_A), fixed2),
                   pl.BlockSpec((1, E_A), fixed2), pl.BlockSpec((1, E_A), fixed2),
                   pl.BlockSpec((NDEV, CHUNK, CHUNK), fixed3), pl.BlockSpec((NDEV, CHUNK, 1), fixed3)],
        compiler_params=_params(("arbitrary",)),
    )(proj, proj, proj, proj, proj, d_ycat, d_ycat, convw, ln_g3, ln_b3, sgu_w, sgu_bcol)


def _window_count(step, n, win, ext_before):
    rows = CHUNK if ext_before else CHUNK + HALO
    t = step * TS + n * CHUNK + lax.broadcasted_iota(jnp.int32, (rows, 1), 0)
    return jnp.minimum(t + 1, win).astype(F32)


def _pool_weight(wp_ref, g):
    return jnp.concatenate([wp_ref[d, g] for d in range(NDEV)], axis=0)


def _pooled_chunk(p, tail, win, count):
    sums = jnp.concatenate([tail, p], axis=0)
    shift = 1
    while shift < win:
        sums = sums + pltpu.roll(sums, shift, 0)
        shift *= 2
    return sums[HALO:] / count - p


def _pool_mix_fwd(proj, wpool, pscale4, wl, after, name):
    e_c = 4 * G_C

    def body(pj_ref, hp_ref, wp_ref, ps_ref, *rest):
        y_ref, pooled_scr, yraw_scr = rest[-3:]
        step = pl.program_id(0)
        live = (step > 0).astype(F32)
        for g, win in enumerate(POOL_WINDOWS):
            for q in range(G_C // LANE):
                cols = slice(g * G_C + q * LANE, g * G_C + (q + 1) * LANE)
                tail = hp_ref[:, cols].astype(F32) * live
                for n in range(NCH):
                    rows = slice(n * CHUNK, (n + 1) * CHUNK)
                    p = pj_ref[rows, cols].astype(F32)
                    pooled_scr[rows, q * LANE:(q + 1) * LANE] = _pooled_chunk(
                        p, tail, win, _window_count(step, n, win, True)).astype(BF16)
                    tail = p[CHUNK - HALO:]
            yraw_scr[...] = jnp.dot(pooled_scr[...], _pool_weight(wp_ref, g), preferred_element_type=F32)
            for q in range(G_C // LANE):
                cols = slice(g * G_C + q * LANE, g * G_C + (q + 1) * LANE)
                for n in range(NCH):
                    rows = slice(n * CHUNK, (n + 1) * CHUNK)
                    z = pj_ref[rows, e_c + cols.start:e_c + cols.stop].astype(F32)
                    y_ref[rows, cols] = (yraw_scr[rows, q * LANE:(q + 1) * LANE] * ps_ref[:, cols] * _silu(z)).astype(BF16)

    return pl.pallas_call(
        body, name=name, grid=(S // TS,), out_shape=jax.ShapeDtypeStruct((S, e_c), BF16),
        in_specs=[pl.BlockSpec((TS, 2 * e_c), lambda i: (i, 0)), _halo_before(e_c, 0),
                  pl.BlockSpec((NDEV, 4, G_C // NDEV, G_C), lambda i: (0, 0, 0, 0)),
                  pl.BlockSpec((None, 1, e_c), lambda i: (wl, 0, 0))] + [ANY] * len(after),
        out_specs=pl.BlockSpec((TS, e_c), lambda i: (i, 0)),
        scratch_shapes=[pltpu.VMEM((TS, G_C), BF16), pltpu.VMEM((TS, G_C), F32)],
        compiler_params=_params(("arbitrary",)),
    )(proj, proj, wpool, pscale4, *after)


def _pool_mix_bwd(proj, d_ycat, wpool, pscale4, wl, name):
    e_c = 4 * G_C
    nsteps = S // TS
    rb = G_C // NDEV

    def body(pj_ref, hp_ref, hz_ref, dy_ref, hdy_ref, wp_ref, ps_ref,
             dp_ref, dps_ref, dwp_ref, pooled_scr, yraw_scr, dyraw_scr, dpool_scr, acc_w):
        step = pl.program_id(0)

        @pl.when(step == 0)
        def _():
            dps_ref[...] = jnp.zeros_like(dps_ref)
            acc_w[...] = jnp.zeros_like(acc_w)

        live_before = (step > 0).astype(F32)
        live_after = (step < nsteps - 1).astype(F32)
        for g, win in enumerate(POOL_WINDOWS):
            weight = _pool_weight(wp_ref, g)
            for q in range(G_C // LANE):
                cols = slice(g * G_C + q * LANE, g * G_C + (q + 1) * LANE)
                tail = hp_ref[:, cols].astype(F32) * live_before
                for n in range(NCH):
                    rows = slice(n * CHUNK, (n + 1) * CHUNK)
                    p = pj_ref[rows, cols].astype(F32)
                    pooled_scr[rows, q * LANE:(q + 1) * LANE] = _pooled_chunk(
                        p, tail, win, _window_count(step, n, win, True)).astype(BF16)
                    tail = p[CHUNK - HALO:]
            yraw_scr[...] = jnp.dot(pooled_scr[...], weight, preferred_element_type=F32)
            for q in range(G_C // LANE):
                cols = slice(g * G_C + q * LANE, g * G_C + (q + 1) * LANE)
                local = slice(q * LANE, (q + 1) * LANE)
                scale = ps_ref[:, cols]
                acc_ps = jnp.zeros((1, LANE), F32)
                for n in range(NCH):
                    rows = slice(n * CHUNK, (n + 1) * CHUNK)
                    sz, dsz = _silu_and_grad(pj_ref[rows, e_c + cols.start:e_c + cols.stop].astype(F32))
                    dyv = dy_ref[rows, cols].astype(F32)
                    yraw = yraw_scr[rows, local]
                    dyraw_scr[rows, local] = (dyv * scale * sz).astype(BF16)
                    acc_ps += jnp.sum(dyv * yraw * sz, axis=0, keepdims=True)
                    dp_ref[rows, e_c + cols.start:e_c + cols.stop] = (dyv * yraw * scale * dsz).astype(BF16)
                dps_ref[:, cols] += acc_ps
                dyraw_scr[TS:, local] = (hdy_ref[:, cols].astype(F32) * scale * _silu(hz_ref[:, cols].astype(F32))
                                         * live_after).astype(BF16)
            dpool_scr[...] = lax.dot_general(dyraw_scr[...], weight, (((1,), (1,)), ((), ())), preferred_element_type=F32)
            acc_w[g] += lax.dot_general(pooled_scr[...], dyraw_scr[:TS, :], (((0,), (0,)), ((), ())),
                                        preferred_element_type=F32)
            for q in range(G_C // LANE):
                cols = slice(g * G_C + q * LANE, g * G_C + (q + 1) * LANE)
                local = slice(q * LANE, (q + 1) * LANE)
                for n in range(NCH):
                    rows = slice(n * CHUNK, (n + 1) * CHUNK)
                    ext = dpool_scr[n * CHUNK:(n + 1) * CHUNK + HALO, local]
                    sums = ext / _window_count(step, n, win, False)
                    shift = 1
                    while shift < win:
                        sums = sums + pltpu.roll(sums, CHUNK + HALO - shift, 0)
                        shift *= 2
                    dp_ref[rows, cols] = (sums[:CHUNK] - ext[:CHUNK]).astype(BF16)

        @pl.when(step == nsteps - 1)
        def _():
            for g in range(4):
                for d in range(NDEV):
                    dwp_ref[d % 2, d // 2, g] = acc_w[g, d * rb:(d + 1) * rb, :].astype(BF16)

    in_specs = [pl.BlockSpec((TS, 2 * e_c), lambda i: (i, 0)), _halo_before(e_c, 0), _halo_after(e_c, 1),
                pl.BlockSpec((TS, e_c), lambda i: (i, 0)), _halo_after(e_c, 0),
                pl.BlockSpec((NDEV, 4, rb, G_C), lambda i: (0, 0, 0, 0)),
                pl.BlockSpec((None, 1, e_c), lambda i: (wl, 0, 0))]
    args = [proj, proj, proj, d_ycat, d_ycat, wpool, pscale4]
    return pl.pallas_call(
        body, name=name, grid=(nsteps,),
        out_shape=[jax.ShapeDtypeStruct((S, 2 * e_c), BF16), jax.ShapeDtypeStruct((1, e_c), F32),
                   jax.ShapeDtypeStruct((2, NDEV // 2) + wpool.shape[1:], BF16)],
        in_specs=in_specs,
        out_specs=[pl.BlockSpec((TS, 2 * e_c), lambda i: (i, 0)), pl.BlockSpec((1, e_c), lambda i: (0, 0)),
                   pl.BlockSpec((2, NDEV // 2, 4, rb, G_C), lambda i: (0, 0, 0, 0, 0))],
        scratch_shapes=[pltpu.VMEM((TS, G_C), BF16), pltpu.VMEM((TS, G_C), F32), pltpu.VMEM((TS + HALO, G_C), BF16),
                        pltpu.VMEM((TS + HALO, G_C), F32), pltpu.VMEM((4, G_C, G_C), F32)],
        compiler_params=_params(("arbitrary",)),
    )(*args)


def _adamw(w, g, m, v):
    m = ADAM_B1 * m + (1.0 - ADAM_B1) * g
    v = ADAM_B2 * v + (1.0 - ADAM_B2) * jnp.square(g)
    m_hat = m / (1.0 - ADAM_B1 ** ADAM_STEP)
    v_hat = v / (1.0 - ADAM_B2 ** ADAM_STEP)
    delta = -ADAM_LR * (m_hat / (jnp.sqrt(v_hat) + ADAM_EPS) + ADAM_WD * w)
    return delta, m, v


def _adam_sharded(w, m, v, chip_parts, landed, my_chip, carried, name, first=0, into=()):
    _, nr, ncol = w.shape
    nl = len(chip_parts)
    tr = 128
    steps = nr // tr
    nc, ni = len(carried), len(into)

    def body(chip_ref, w_ref, m_ref, v_ref, *rest):
        parts, zones = rest[:nl], rest[nl:2 * nl]
        g_ref, d_ref, nm_ref, nv_ref = rest[2 * nl + nc + ni:2 * nl + nc + ni + 4]
        layer = pl.program_id(0)
        g = jnp.zeros((tr, ncol), F32)
        for l in range(nl):
            gl = parts[l][...].astype(F32)
            for q in range(3):
                gl = gl + zones[l][q].astype(F32)
            g = jnp.where(layer == l, gl, g)
        g_ref[...] = g
        d_ref[...], nm_ref[...], nv_ref[...] = _adamw(w_ref[...], g, m_ref[...], v_ref[...])

    def rows_of(l):
        return lambda layer, i, chip_ref: jnp.where(layer == l, i, jnp.where(layer < l, 0, steps - 1))

    spec = pl.BlockSpec((None, tr, ncol), lambda layer, i, chip_ref: (first + layer, i, 0))
    part_specs = [pl.BlockSpec((None, tr, ncol), lambda layer, i, chip_ref, l=l: (chip_ref[0], rows_of(l)(layer, i, chip_ref), 0))
                  for l in range(nl)]
    zone_specs = [pl.BlockSpec((3, tr, ncol), lambda layer, i, chip_ref, l=l: (0, rows_of(l)(layer, i, chip_ref), 0))
                  for l in range(nl)]
    grid_spec = pltpu.PrefetchScalarGridSpec(
        num_scalar_prefetch=1, grid=(nl, steps),
        in_specs=[spec, spec, spec] + part_specs + zone_specs + [ANY] * (nc + ni), out_specs=[spec] * 4 + [ANY] * nc)
    aliases = {4 + 2 * nl + k: 4 + k for k in range(nc)}
    aliases.update({4 + 2 * nl + nc + k: k for k in range(ni)})
    return pl.pallas_call(
        body, name=name, grid_spec=grid_spec,
        out_shape=[jax.ShapeDtypeStruct(w.shape, F32)] * 4 + [jax.ShapeDtypeStruct(a.shape, a.dtype) for a in carried],
        input_output_aliases=aliases, compiler_params=_params(("arbitrary", "arbitrary")),
    )(my_chip, w, m, v, *chip_parts, *landed, *carried, *into)


def _adam_small(w, g, m, v, name):
    def body(w_ref, g_ref, m_ref, v_ref, d_ref, nm_ref, nv_ref):
        d_ref[...], nm_ref[...], nv_ref[...] = _adamw(w_ref[...], g_ref[...], m_ref[...], v_ref[...])

    return pl.pallas_call(body, name=name, out_shape=[jax.ShapeDtypeStruct(w.shape, F32)] * 3,
                          in_specs=[VMEM_FULL] * 4, out_specs=[VMEM_FULL] * 3, compiler_params=_params())(w, g, m, v)


def _sum_devices(gathered, name):
    _, nr, ncol = gathered.shape

    def body(g_ref, o_ref):
        acc = g_ref[0].astype(F32)
        for s in range(1, NDEV):
            acc = acc + g_ref[s].astype(F32)
        o_ref[...] = acc

    return pl.pallas_call(body, name=name, grid=(1,), out_shape=jax.ShapeDtypeStruct((nr, ncol), F32),
                          in_specs=[pl.BlockSpec((NDEV, nr, ncol), lambda i: (0, 0, 0))],
                          out_specs=pl.BlockSpec((nr, ncol), lambda i: (0, 0)),
                          compiler_params=_params(("arbitrary",)))(gathered)


def _ada_weight_adam(cact_t, dmod_mine, w, m, v):
    def body(ct_ref, dm_ref, w_ref, m_ref, v_ref, g_ref, d_ref, nm_ref, nv_ref):
        ct, dm = ct_ref[...], dm_ref[...]
        g = ct[:, 0:1] * dm[0:1, :]
        for e in range(1, NDEV):
            g = g + ct[:, e:e + 1] * dm[e:e + 1, :]
        g_ref[...] = g
        d_ref[...], nm_ref[...], nv_ref[...] = _adamw(w_ref[...], g, m_ref[...], v_ref[...])

    spec = pl.BlockSpec((None, D, ADA_NC), lambda l: (l, 0, 0))
    return pl.pallas_call(
        body, name="ada_weight_adam", grid=(DEPTH,), out_shape=[jax.ShapeDtypeStruct(w.shape, F32)] * 4,
        in_specs=[pl.BlockSpec((D, NDEV), lambda l: (0, 0)), pl.BlockSpec((None, NDEV, ADA_NC), lambda l: (l, 0, 0)),
                  spec, spec, spec],
        out_specs=[spec] * 4, compiler_params=_params(("arbitrary",)),
    )(cact_t, dmod_mine, w, m, v)


def _pad_rows(a, rows):
    a = a.reshape(-1, D)
    return jnp.pad(a, ((0, rows - a.shape[0]), (0, 0)))


def kernel(x, c, norm_g, ada_w, ada_b, ab_w_in, ab_conv_w, ab_ln_g, ab_ln_b, ab_sgu_w, ab_sgu_b, ab_w_out, c_w_in, c_pool_w, c_pool_scale, c_w_out, final_g, loss_target, m_norm_g, m_ada_w, m_ada_b, m_ab_w_in, m_ab_conv_w, m_ab_ln_g, m_ab_ln_b, m_ab_sgu_w, m_ab_sgu_b, m_ab_w_out, m_c_w_in, m_c_pool_w, m_c_pool_scale, m_c_w_out, m_final_g, v_norm_g, v_ada_w, v_ada_b, v_ab_w_in, v_ab_conv_w, v_ab_ln_g, v_ab_ln_b, v_ab_sgu_w, v_ab_sgu_b, v_ab_w_out, v_c_w_in, v_c_pool_w, v_c_pool_scale, v_c_w_out, v_final_g):
    x_pos, y_pos, c_pos = _position()
    me = _index((x_pos, y_pos, c_pos))
    core = c_pos.astype(jnp.int32).reshape(1)
    my_chip = (2 * x_pos + y_pos).astype(jnp.int32).reshape(1)
    me1 = me.astype(jnp.int32).reshape(1)
    x0 = x.reshape(S, D)
    target = loss_target.reshape(S, D)
    norm_g3 = norm_g.reshape(DEPTH, 1, D)
    ln_g3, ln_b3 = ab_ln_g.reshape(2, 1, E_A), ab_ln_b.reshape(2, 1, E_A)
    sgu_bcol = ab_sgu_b.reshape(2, NDEV, CHUNK, 1)
    rb = G_C // NDEV
    pool_w3, m_pool_w3, v_pool_w3 = (a.reshape(2, 4 * rb, G_C) for a in (c_pool_w, m_c_pool_w, v_c_pool_w))

    cact_all, mod = _ada_forward(c, ada_w, ada_b)
    convw_all, pscale_all = _gather([ab_conv_w, c_pool_scale], "gather_small_weights")
    convw = jnp.transpose(convw_all, (1, 2, 0, 3)).reshape(2, 3, E_A)
    pscale4 = jnp.transpose(pscale_all, (1, 0, 2)).reshape(2, 1, 4 * G_C)
    zones = []
    for layer in range(DEPTH):
        wl = layer // 2
        if layer % 2 == 0:
            zones.append([_to_zone(ab_w_in, wl, me1, BF16, f"cast_w_in_{layer}"), _to_zone(ab_w_out, wl, me1, BF16, f"cast_w_out_{layer}")])
        else:
            zones.append([_to_zone(c_w_in, wl, me1, BF16, f"cast_w_in_{layer}"), _to_zone(c_w_out, wl, me1, BF16, f"cast_w_out_{layer}"),
                          _to_zone(pool_w3, wl, me1, BF16, f"cast_pool_w_{layer}")])

    def gathered(flight, after, layer):
        wg = _gather_end(flight, [after], f"gather_end_{layer}")
        return [wg[0], wg[1].reshape(-1, D)] + [w.reshape(NDEV, 4, rb, G_C) for w in wg[2:]]

    flight, (mod,) = _gather_start(zones[0][:1], [convw_all], [mod], "gather_start_0")
    flight, (mod,) = _gather_mid(flight, zones[0][1:] + [z for zs in zones[1:] for z in zs], [mod], "gather_mid_0")
    out_flight, (mod,) = _gather_start(zones[0][1:], [], [mod], "gather_start_0_out")
    next_flight, (mod,) = _gather_start(zones[1], [], [mod], "gather_start_1")
    flight = _gather_late(flight, [mod], "gather_late_0")
    w_in_0, = _gather_end(flight, [mod], "gather_end_0")
    xs, hts, projs, ycats, outs, gathered_w = [x0], [], [], [], [], [[w_in_0, None]]
    for layer in range(DEPTH):
        wl = layer // 2
        even = layer % 2 == 0
        wg = gathered_w[layer]
        h_t, proj = _norm_proj(xs[-1], mod, norm_g3, wg[0], layer, f"norm_proj_{layer}")
        if layer == 0:
            out_flight, (h_t,) = _gather_mid(out_flight, [], [h_t], "gather_mid_0_out")
        if layer + 1 < DEPTH:
            flight, (h_t,) = _gather_mid(next_flight, [], [h_t], f"gather_mid_{layer + 1}")
            if layer + 2 < DEPTH:
                next_flight, (h_t,) = _gather_start(zones[layer + 2], [], [h_t], f"gather_start_{layer + 2}")
        if even:
            ycat = _even_mix_fwd(proj, convw, ln_g3, ln_b3, ab_sgu_w, sgu_bcol, wl, [h_t], f"even_mix_fwd_{layer}")
        else:
            ycat = _pool_mix_fwd(proj, wg[2], pscale4, wl, [h_t], f"pool_mix_fwd_{layer}")
        if layer == 0:
            out_flight = _gather_late(out_flight, [ycat], "gather_late_0_out")
            wg[1] = _gather_end(out_flight, [ycat], "gather_end_0_out")[0].reshape(-1, D)
        if layer + 1 < DEPTH:
            flight = _gather_late(flight, [ycat], f"gather_late_{layer + 1}")
        x_new, out = _out_proj(ycat, wg[1], xs[-1], mod, layer, f"out_proj_{layer}")
        if layer + 1 < DEPTH:
            gathered_w.append(gathered(flight, x_new, layer + 1))
        xs.append(x_new)
        hts.append(h_t)
        projs.append(proj)
        ycats.append(ycat)
        outs.append(out)

    dx, loss_part, d_final_g = _final_loss(xs[DEPTH], target, final_g.reshape(1, D))

    d_mod, d_norm_g = [None] * DEPTH, [None] * DEPTH
    small, scatters, landed, res = {}, {}, {}, {}

    def finish_scatter(layer, after):
        send_sems, recv_sems, chip_parts, zones = scatters[layer]
        landed[layer] = _scatter_end(chip_parts, zones, send_sems, recv_sems, after, f"scatter_end_{layer}")

    def flat(a):
        return a.reshape(a.shape[0], -1, a.shape[-1])

    def sharded_adam(k, j, layers, w, m, v, carried, first=0, into=()):
        outs4 = _adam_sharded(w, m, v, [flat(landed[l][0][j]) for l in layers], [flat(landed[l][1][j]) for l in layers],
                              my_chip, carried, f"adam_{k}_{first}" if len(layers) < w.shape[0] else "adam_" + k, first, into)
        res[k] = [o.reshape(c_pool_w.shape) if k == "c_pool_w" else o for o in outs4[:4]]
        return list(outs4[4:])

    previous = None
    for layer in reversed(range(DEPTH)):
        wl = layer // 2
        even = layer % 2 == 0
        wg = gathered_w[layer]
        carried = [] if previous is None else [scatters[previous][2][0]]
        d_ycat, grad_out, d_gate, carried = _out_bwd(dx, outs[layer], ycats[layer], wg[1], mod, layer, carried, f"out_bwd_{layer}")
        if previous is not None:
            scatters[previous][2][0] = carried[0]
        parts = [None, grad_out]
        if even:
            d_proj, d_cw, d_lg, d_lb, d_sw, d_sb = _even_mix_bwd(
                projs[layer], d_ycat, convw, ln_g3, ln_b3, ab_sgu_w, sgu_bcol, wl, f"even_mix_bwd_{layer}")
            small[layer] = (d_cw, d_lg, d_lb, d_sw, d_sb)
        else:
            d_proj, d_ps, d_pool = _pool_mix_bwd(projs[layer], d_ycat, wg[2], pscale4, wl, f"pool_mix_bwd_{layer}")
            small[layer] = (d_ps,)
            parts.append(d_pool)
        parts[0] = _weight_grad(hts[layer], d_proj, f"grad_w_in_{layer}")
        pair_send, pair_recv, parts, from_sibling = _pair_start(parts, f"pair_start_{layer}")
        if layer > 0:
            dx, d_shift, d_scale, d_norm_g[layer], parts[0] = _dh_norm_bwd(
                d_proj, wg[0], xs[layer], dx, mod, norm_g3, layer, parts[0], f"dh_norm_bwd_{layer}")
            pair_after = dx
        else:
            finish_scatter(1, d_proj)
            finish_scatter(3, d_proj)
            parts[0], = sharded_adam("c_w_out", 1, (1, 3), c_w_out, m_c_w_out, v_c_w_out, [parts[0]])
            parts[0], = sharded_adam("c_pool_w", 2, (1, 3), pool_w3, m_pool_w3, v_pool_w3, [parts[0]])
            pair_after = res["c_pool_w"][0]
        parts, from_sibling = _pair_end(parts, from_sibling, pair_send, pair_recv, pair_after, f"pair_end_{layer}")
        chip_parts = [_pair_sum(p, q, core, f"pair_sum_{layer}_{j}") for j, (p, q) in enumerate(zip(parts, from_sibling))]
        send_sems, recv_sems, chip_parts, zones = _scatter_start(chip_parts, f"scatter_start_{layer}")
        if layer == 0:
            chip_parts[0], = sharded_adam("c_w_in", 0, (1, 3), c_w_in, m_c_w_in, v_c_w_in, [chip_parts[0]])
            dx, d_shift, d_scale, d_norm_g[layer], chip_parts[0] = _dh_norm_bwd(
                d_proj, wg[0], xs[layer], dx, mod, norm_g3, layer, chip_parts[0], f"dh_norm_bwd_{layer}")
        scatters[layer] = [send_sems, recv_sems, chip_parts, zones]
        previous = layer
        d_mod[layer] = jnp.concatenate([d_shift, d_scale, d_gate], axis=0)
    grad_x = dx.reshape(x.shape)

    sections = [("norm_g", jnp.concatenate(d_norm_g, axis=0), 8),
                ("d_mod", jnp.concatenate(d_mod, axis=0), 16),
                ("ab_ln_g", jnp.concatenate([small[0][1], small[2][1]], axis=0), 8),
                ("ab_ln_b", jnp.concatenate([small[0][2], small[2][2]], axis=0), 8),
                ("ab_sgu_b", jnp.stack([small[0][4], small[2][4]]), 8),
                ("final_g", d_final_g, 8),
                ("ab_conv_w", jnp.stack([small[0][0], small[2][0]]), 8),
                ("c_pool_scale", jnp.concatenate([small[1][0], small[3][0]], axis=0), 8),
                ("ab_sgu_w", jnp.stack([small[0][3], small[2][3]]), 256)]
    offsets, at = {}, 0
    for name, _, rows in sections:
        offsets[name] = (at, rows)
        at += rows
    packed = jnp.concatenate([_pad_rows(a, rows) for _, a, rows in sections] + [jnp.zeros((-at % 32, D), F32)], axis=0)
    loss_rows = jnp.pad(loss_part, ((0, 15), (0, D - LANE)))
    small_zones = [_to_zone(packed[None], 0, me1, BF16, "place_small_grads"), _to_zone(loss_rows[None], 0, me1, F32, "place_loss")]
    small_flight, (mod,) = _gather_start(small_zones, [], [mod], "gather_small_start")

    finish_scatter(2, mod)
    sharded_adam("ab_w_out", 1, (2,), ab_w_out, m_ab_w_out, v_ab_w_out, [], first=1)
    sharded_adam("ab_w_in", 0, (2,), ab_w_in, m_ab_w_in, v_ab_w_in, [], first=1)
    finish_scatter(0, res["ab_w_in"][0])
    sharded_adam("ab_w_out", 1, (0,), ab_w_out, m_ab_w_out, v_ab_w_out, [], into=res["ab_w_out"])
    small_flight, (mod,) = _gather_mid(small_flight, [res["ab_w_out"][0]], [mod], "gather_small_mid")
    sharded_adam("ab_w_in", 0, (0,), ab_w_in, m_ab_w_in, v_ab_w_in, [mod], into=res["ab_w_in"])

    last = res["ab_w_in"][0]
    small_flight = _gather_late(small_flight, [last], "gather_small_late")
    small_grads, losses = _gather_end(small_flight, [last], "gather_small_end")
    summed = _sum_devices(small_grads, "sum_small_grads")
    loss = _sum_devices(losses, "sum_loss")[0, 0]

    def section(name, nrows, src=summed):
        start = offsets[name][0]
        return src[..., start:start + nrows, :]

    grads = {
        "norm_g": section("norm_g", DEPTH),
        "ada_b": section("d_mod", 3 * DEPTH).reshape(DEPTH, 3 * D),
        "ab_ln_g": section("ab_ln_g", 2), "ab_ln_b": section("ab_ln_b", 2),
        "ab_sgu_b": section("ab_sgu_b", 2).reshape(ab_sgu_b.shape),
        "final_g": section("final_g", 1),
        "ab_sgu_w": section("ab_sgu_w", 256).reshape(ab_sgu_w.shape),
        "ab_conv_w": lax.dynamic_slice_in_dim(section("ab_conv_w", 6).reshape(2, 3, E_A), me * HEAD, HEAD, axis=2),
        "c_pool_scale": lax.dynamic_slice_in_dim(section("c_pool_scale", 4).reshape(2, 4 * G_C), me * 256, 256, axis=1),
    }
    small_w = {"norm_g": (norm_g, m_norm_g, v_norm_g), "ada_b": (ada_b, m_ada_b, v_ada_b),
               "ab_ln_g": (ab_ln_g, m_ab_ln_g, v_ab_ln_g), "ab_ln_b": (ab_ln_b, m_ab_ln_b, v_ab_ln_b),
               "ab_sgu_b": (ab_sgu_b, m_ab_sgu_b, v_ab_sgu_b),
               "final_g": (final_g.reshape(1, D), m_final_g.reshape(1, D), v_final_g.reshape(1, D)),
               "ab_sgu_w": (ab_sgu_w, m_ab_sgu_w, v_ab_sgu_w), "ab_conv_w": (ab_conv_w, m_ab_conv_w, v_ab_conv_w),
               "c_pool_scale": (c_pool_scale, m_c_pool_scale, v_c_pool_scale)}
    for k, (w, m, v) in small_w.items():
        res[k] = [grads[k]] + list(_adam_small(w, grads[k], m, v, "adam_" + k))
    res["final_g"] = [a.reshape(D) for a in res["final_g"]]

    dmod_all = section("d_mod", 3 * DEPTH, small_grads).reshape(NDEV, DEPTH, 3 * D)
    dmod_mine = jnp.transpose(lax.dynamic_slice_in_dim(dmod_all, me * ADA_NC, ADA_NC, axis=2), (1, 0, 2)).astype(F32)
    res["ada_w"] = _ada_weight_adam(jnp.transpose(cact_all.reshape(NDEV, D)), dmod_mine, ada_w, m_ada_w, v_ada_w)

    order = ["norm_g", "ada_w", "ada_b", "ab_w_in", "ab_conv_w", "ab_ln_g", "ab_ln_b", "ab_sgu_w", "ab_sgu_b",
             "ab_w_out", "c_w_in", "c_pool_w", "c_pool_scale", "c_w_out", "final_g"]
    return (loss, grad_x, *[res[k][0] for k in order], *[res[k][1] for k in order],
            *[res[k][2] for k in order], *[res[k][3] for k in order])
```

```python
import jax
import jax.numpy as jnp
from jax import lax
from jax.experimental import pallas as pl
from jax.experimental.pallas import tpu as pltpu

F32, BF16 = jnp.float32, jnp.bfloat16
S, D = 2048, 1024
NDEV = 8
DEPTH = 4
EPS = 1e-6
E_A = 1024
HEAD = 128
CHUNK = 128
POOL_WINDOWS = (2, 4, 8, 16)
G_C = 512
HALO = 16
ADA_NC = 384
MIB = 1024 * 1024
LANE = 128

ADAM_LR, ADAM_B1, ADAM_B2, ADAM_EPS, ADAM_WD, ADAM_STEP = 0.001, 0.9, 0.999, 1e-08, 0.01, 10

ANY = pl.BlockSpec(memory_space=pl.ANY)
VMEM_FULL = pl.BlockSpec(memory_space=pltpu.VMEM)
IN_HBM = pl.BlockSpec(memory_space=pltpu.HBM)
SEMAPHORES = pl.BlockSpec(memory_space=pltpu.SEMAPHORE)
IN_FLIGHT = pltpu.SideEffectType.DATAFLOW_SIDE_EFFECTING


V7X_VMEM_MIB = 64
VMEM_LIMIT_MIB = V7X_VMEM_MIB - 4


def _params(semantics=None):
    return pltpu.CompilerParams(dimension_semantics=semantics, vmem_limit_bytes=VMEM_LIMIT_MIB * MIB)


def _silu(z):
    return z * jax.nn.sigmoid(z)


def _silu_and_grad(z):
    sig = jax.nn.sigmoid(z)
    return z * sig, sig * (1.0 + z * (1.0 - sig))


def _position():
    return lax.axis_index("x"), lax.axis_index("y"), lax.axis_index("c")


def _index(pos):
    return 4 * pos[0] + 2 * pos[1] + pos[2]


def _peer(pos, k):
    flipped = tuple(1 - p if (k >> (2 - b)) & 1 else p for b, p in enumerate(pos))
    return flipped, _index(flipped)


def _remote(src, dst, send_sem, recv_sem, device):
    return pltpu.make_async_remote_copy(src_ref=src, dst_ref=dst, send_sem=send_sem, recv_sem=recv_sem,
                                        device_id=device, device_id_type=pl.DeviceIdType.MESH)


def _gather(arrays, name):
    n = len(arrays)
    out_shape = [jax.ShapeDtypeStruct((NDEV,) + a.shape, a.dtype) for a in arrays]

    def body(*refs):
        ins, outs = refs[:n], refs[n:2 * n]
        send_sems, recv_sems, own_sems = refs[2 * n:]
        x, y, c = _position()
        me = _index((x, y, c))
        sibling = (x, y, 1 - c)
        chips = [(1 - x, y), (x, 1 - y), (1 - x, 1 - y)]

        def block_copy(j, k, owner, to, src=None):
            rows = outs[j].at[_index(owner)]
            return _remote(rows if src is None else src, rows, send_sems.at[j, k], recv_sems.at[j, k], to)

        own, first, passed = [], [], []
        for j in range(n):
            own.append(pltpu.make_async_copy(ins[j], outs[j].at[me], own_sems.at[j]))
            first.append(block_copy(j, 0, (x, y, c), sibling, src=ins[j]))
            first += [block_copy(j, 1 + q, (x, y, c), (*chip, c), src=ins[j]) for q, chip in enumerate(chips)]
        for copy in own + first:
            copy.start()
        for q, chip in enumerate(chips):
            for j in range(n):
                block_copy(j, 1 + q, (*chip, c), (x, y, c)).wait_recv()
                forward = block_copy(j, 4 + q, (*chip, c), sibling)
                forward.start()
                passed.append(forward)
        for j in range(n):
            block_copy(j, 0, sibling, (x, y, c)).wait_recv()
            for q, chip in enumerate(chips):
                block_copy(j, 4 + q, (*chip, 1 - c), (x, y, c)).wait_recv()
        for copy in first + passed:
            copy.wait_send()
        for copy in own:
            copy.wait()

    return pl.pallas_call(
        body, name=name, out_shape=out_shape, in_specs=[ANY] * n, out_specs=[ANY] * n,
        scratch_shapes=[pltpu.SemaphoreType.DMA((n, NDEV - 1)), pltpu.SemaphoreType.DMA((n, NDEV - 1)),
                        pltpu.SemaphoreType.DMA((n,))],
    )(*arrays)


def _pair_start(parts, name):
    n = len(parts)
    lands = [_in_hbm(lax.empty(p.shape[1:], p.dtype)) for p in parts]

    def body(*refs):
        ins, zones = refs[:n], refs[n:2 * n]
        send_sems, recv_sems = refs[2 * n:2 * n + 2]
        x, y, c = _position()
        for j in range(n):
            _remote(ins[j].at[1 - c], zones[j], send_sems.at[j], recv_sems.at[j], (x, y, 1 - c)).start()

    outs = pl.pallas_call(
        body, name=name,
        out_shape=(pltpu.SemaphoreType.DMA((n,)), pltpu.SemaphoreType.DMA((n,)),
                   *[pltpu.HBM(p.shape, p.dtype) for p in parts], *[pltpu.HBM(z.shape, z.dtype) for z in lands]),
        in_specs=[IN_HBM] * (2 * n), out_specs=(SEMAPHORES, SEMAPHORES, *[IN_HBM] * (2 * n)),
        input_output_aliases={j: 2 + j for j in range(2 * n)},
        compiler_params=pltpu.CompilerParams(has_side_effects=IN_FLIGHT),
    )(*[_in_hbm(p) for p in parts], *lands)
    return outs[0], outs[1], list(outs[2:2 + n]), list(outs[2 + n:])


def _pair_end(parts, zones, send_sems, recv_sems, after, name):
    n = len(parts)

    def body(*refs):
        ins, zs = refs[:n], refs[n:2 * n]
        s, r = refs[2 * n:2 * n + 2]
        me = _position()
        for j in range(n):
            copy = _remote(ins[j].at[0], zs[j], s.at[j], r.at[j], me)
            copy.wait_send()
            copy.wait_recv()

    outs = pl.pallas_call(
        body, name=name,
        out_shape=(*[pltpu.HBM(p.shape, p.dtype) for p in parts], *[pltpu.HBM(z.shape, z.dtype) for z in zones]),
        in_specs=[IN_HBM] * (2 * n) + [SEMAPHORES, SEMAPHORES, ANY], out_specs=tuple([IN_HBM] * (2 * n)),
        input_output_aliases={j: j for j in range(2 * n)},
        compiler_params=pltpu.CompilerParams(has_side_effects=IN_FLIGHT),
    )(*parts, *zones, send_sems, recv_sems, after)
    return list(outs[:n]), list(outs[n:])


def _pair_sum(part, from_sibling, core, name):
    ncol = part.shape[-1]
    p3 = part.reshape(2, -1, ncol)
    q2 = from_sibling.reshape(-1, ncol)
    nrows = q2.shape[0]
    tr = 512

    def body(core_ref, p_ref, q_ref, o_ref):
        o_ref[...] = (p_ref[...].astype(F32) + q_ref[...].astype(F32)).astype(BF16)

    grid_spec = pltpu.PrefetchScalarGridSpec(
        num_scalar_prefetch=1, grid=(nrows // tr,),
        in_specs=[pl.BlockSpec((None, tr, ncol), lambda i, core_ref: (core_ref[0], i, 0)),
                  pl.BlockSpec((tr, ncol), lambda i, core_ref: (i, 0))],
        out_specs=pl.BlockSpec((tr, ncol), lambda i, core_ref: (i, 0)))
    out = pl.pallas_call(body, name=name, grid_spec=grid_spec, out_shape=jax.ShapeDtypeStruct(q2.shape, BF16),
                         compiler_params=_params(("arbitrary",)))(core, p3, q2)
    return out.reshape(from_sibling.shape)


def _in_hbm(a):
    return pltpu.with_memory_space_constraint(a, pltpu.HBM)


def _chips(x, y):
    return [(1 - x, y), (x, 1 - y), (1 - x, 1 - y)]


def _to_zone(a, wl, me, dtype, name):
    _, rows, cols = a.shape
    tr = 256 if rows % 256 == 0 else rows

    def body(me_ref, a_ref, o_ref):
        o_ref[...] = a_ref[...].astype(dtype)

    grid_spec = pltpu.PrefetchScalarGridSpec(
        num_scalar_prefetch=1, grid=(rows // tr,),
        in_specs=[pl.BlockSpec((None, tr, cols), lambda i, me_ref: (wl, i, 0))],
        out_specs=pl.BlockSpec((None, tr, cols), lambda i, me_ref: (me_ref[0], i, 0)))
    return pl.pallas_call(body, name=name, grid_spec=grid_spec, out_shape=jax.ShapeDtypeStruct((NDEV, rows, cols), dtype),
                          compiler_params=_params(("arbitrary",)))(me, a)


def _halves(block):
    rows = block.shape[0] // 2
    return block.at[pl.ds(0, rows)], block.at[pl.ds(rows, rows)]


def _around(x, y, c):
    return (x, y, 1 - c), (1 - x, y, c), (x, 1 - y, c), (1 - x, 1 - y, c)


def _gather_step1(zs, send, recv, pos):
    sibling, xn, yn, _ = _around(*pos)
    for j, z in enumerate(zs):
        mine = z.at[_index(pos)]
        for k, peer in enumerate((sibling, xn, yn)):
            _remote(mine, mine, send.at[3 * j + k], recv.at[3 * j + k], peer).start()


def _gather_step2(zs, recv1, send, recv, pos):
    sibling, xn, yn, _ = _around(*pos)
    for j, z in enumerate(zs):
        xb, yb = z.at[_index(xn)], z.at[_index(yn)]
        _remote(xb, xb, send.at[4 * j], recv1.at[3 * j + 1], pos).wait_recv()
        _remote(yb, yb, send.at[4 * j], recv1.at[3 * j + 2], pos).wait_recv()
        _remote(xb, xb, send.at[4 * j], recv.at[4 * j], sibling).start()
        _remote(yb, yb, send.at[4 * j + 1], recv.at[4 * j + 1], sibling).start()
        first, second = _halves(xb)[0], _halves(yb)[1]
        _remote(first, first, send.at[4 * j + 2], recv.at[4 * j + 2], yn).start()
        _remote(second, second, send.at[4 * j + 3], recv.at[4 * j + 3], xn).start()


def _gather_step3(zs, recv2, send, recv, pos):
    sibling, _, _, diagonal = _around(*pos)
    for j, z in enumerate(zs):
        db = z.at[_index(diagonal)]
        first, second = _halves(db)
        _remote(first, first, send.at[j], recv2.at[4 * j + 2], pos).wait_recv()
        _remote(second, second, send.at[j], recv2.at[4 * j + 3], pos).wait_recv()
        _remote(db, db, send.at[j], recv.at[j], sibling).start()


def _gather_step4(zs, send1, recv1, send2, recv2, send3, recv3, pos):
    x, y, c = pos
    sibling = (x, y, 1 - c)
    _, sx, sy, sd = _around(*sibling)
    for j, z in enumerate(zs):
        for owner, send, recv, k in ((sibling, send1, recv1, 3 * j), (sx, send2, recv2, 4 * j), (sy, send2, recv2, 4 * j + 1),
                                     (sd, send3, recv3, j)):
            block = z.at[_index(owner)]
            _remote(block, block, send.at[k], recv.at[k], pos).wait_recv()
    for j, z in enumerate(zs):
        block = z.at[0]
        half = _halves(block)[0]
        for ref, send, recv, k in ([(block, send1, recv1, 3 * j + k) for k in range(3)]
                                   + [(block, send2, recv2, 4 * j), (block, send2, recv2, 4 * j + 1),
                                      (half, send2, recv2, 4 * j + 2), (half, send2, recv2, 4 * j + 3), (block, send3, recv3, j)]):
            _remote(ref, ref, send.at[k], recv.at[k], pos).wait_send()


def _flight_call(step, name, zones, sems_in, nsems_out, after, carried):
    n, m, k = len(zones), len(carried), len(sems_in)

    def body(*refs):
        zs = refs[:n]
        given = refs[n + m:n + m + k]
        made = refs[n + m + k + len(after):n + m + k + len(after) + (2 if nsems_out else 0)]
        step(zs, *given, *made, _position())

    sem_out = (pltpu.SemaphoreType.DMA((nsems_out,)),) * 2 if nsems_out else ()
    outs = pl.pallas_call(
        body, name=name,
        out_shape=(*sem_out, *[pltpu.HBM(z.shape, z.dtype) for z in zones], *[jax.ShapeDtypeStruct(a.shape, a.dtype) for a in carried]),
        in_specs=[IN_HBM] * n + [ANY] * m + [SEMAPHORES] * k + [ANY] * len(after),
        out_specs=(*[SEMAPHORES] * len(sem_out), *[IN_HBM] * n, *[ANY] * m),
        input_output_aliases={j: len(sem_out) + j for j in range(n + m)},
        compiler_params=pltpu.CompilerParams(has_side_effects=IN_FLIGHT),
    )(*[_in_hbm(z) for z in zones], *carried, *sems_in, *after)
    sems = list(outs[:len(sem_out)])
    return sems, list(outs[len(sem_out):len(sem_out) + n]), list(outs[len(sem_out) + n:])


def _gather_start(zones, after, carried, name):
    (send1, recv1), zones, carried = _flight_call(_gather_step1, name, zones, [], 3 * len(zones), after, carried)
    return {"s1": send1, "r1": recv1, "zones": zones}, carried


def _gather_mid(flight, after, carried, name):
    step = lambda zs, recv1, send, recv, pos: _gather_step2(zs, recv1, send, recv, pos)
    (send2, recv2), zones, carried = _flight_call(step, name, flight["zones"], [flight["r1"]], 4 * len(flight["zones"]), after, carried)
    return {**flight, "s2": send2, "r2": recv2, "zones": zones}, carried


def _gather_late(flight, after, name):
    step = lambda zs, recv2, send, recv, pos: _gather_step3(zs, recv2, send, recv, pos)
    (send3, recv3), zones, _ = _flight_call(step, name, flight["zones"], [flight["r2"]], len(flight["zones"]), after, [])
    return {**flight, "s3": send3, "r3": recv3, "zones": zones}


def _gather_end(flight, after, name):
    sems = [flight[k] for k in ("s1", "r1", "s2", "r2", "s3", "r3")]
    _, zones, _ = _flight_call(_gather_step4, name, flight["zones"], sems, 0, after, [])
    return zones


def _scatter_start(parts, name):
    n = len(parts)
    lands = [_in_hbm(lax.empty((3,) + p.shape[1:], p.dtype)) for p in parts]

    def body(*refs):
        ins, zones = refs[:n], refs[n:2 * n]
        send_sems, recv_sems = refs[2 * n:2 * n + 2]
        x, y, c = _position()
        for j in range(n):
            for q, (px, py) in enumerate(_chips(x, y)):
                _remote(ins[j].at[2 * px + py], zones[j].at[q], send_sems.at[3 * j + q], recv_sems.at[3 * j + q],
                        (px, py, c)).start()

    outs = pl.pallas_call(
        body, name=name,
        out_shape=(pltpu.SemaphoreType.DMA((3 * n,)), pltpu.SemaphoreType.DMA((3 * n,)),
                   *[pltpu.HBM(p.shape, p.dtype) for p in parts], *[pltpu.HBM(z.shape, z.dtype) for z in lands]),
        in_specs=[IN_HBM] * (2 * n), out_specs=(SEMAPHORES, SEMAPHORES, *[IN_HBM] * (2 * n)),
        input_output_aliases={j: 2 + j for j in range(2 * n)},
        compiler_params=pltpu.CompilerParams(has_side_effects=IN_FLIGHT),
    )(*[_in_hbm(p) for p in parts], *lands)
    return outs[0], outs[1], list(outs[2:2 + n]), list(outs[2 + n:])


def _scatter_end(parts, zones, send_sems, recv_sems, after, name):
    n = len(parts)

    def body(*refs):
        ins, zs = refs[:n], refs[n:2 * n]
        s, r = refs[2 * n:2 * n + 2]
        me = _position()
        for j in range(n):
            for q in range(3):
                copy = _remote(ins[j].at[0], zs[j].at[q], s.at[3 * j + q], r.at[3 * j + q], me)
                copy.wait_send()
                copy.wait_recv()

    outs = pl.pallas_call(
        body, name=name,
        out_shape=(*[pltpu.HBM(p.shape, p.dtype) for p in parts], *[pltpu.HBM(z.shape, z.dtype) for z in zones]),
        in_specs=[IN_HBM] * (2 * n) + [SEMAPHORES, SEMAPHORES, ANY], out_specs=tuple([IN_HBM] * (2 * n)),
        input_output_aliases={j: j for j in range(2 * n)},
        compiler_params=pltpu.CompilerParams(has_side_effects=IN_FLIGHT),
    )(*parts, *zones, send_sems, recv_sems, after)
    return list(outs[:n]), list(outs[n:])


def _ada_forward(c, ada_w, ada_b):
    def body(c_ref, w_ref, b_ref, cact_ref, mod_ref, gbuf, modrow, send_sems, recv_sems):
        pos = _position()
        me = _index(pos)

        def to_all(ref, round_):
            copies = []
            for k in range(1, NDEV):
                peer, _ = _peer(pos, k)
                copy = pltpu.make_async_remote_copy(
                    src_ref=ref.at[me], dst_ref=ref.at[me], send_sem=send_sems.at[round_, k - 1],
                    recv_sem=recv_sems.at[round_, k - 1], device_id=peer, device_id_type=pl.DeviceIdType.MESH)
                copy.start()
                copies.append(copy)
            for copy in copies:
                copy.wait()

        cact_ref[me] = _silu(c_ref[...])
        to_all(cact_ref, 0)
        rows = lax.broadcasted_iota(jnp.int32, (NDEV, D), 0)
        cact = jnp.zeros((NDEV, D), F32)
        for e in range(NDEV):
            cact = jnp.where(rows == e, cact_ref[e], cact)
        cact = cact.astype(BF16)
        for l in range(DEPTH):
            gbuf[me, l] = jnp.dot(cact, w_ref[l].astype(BF16), preferred_element_type=F32)
        to_all(gbuf, 1)
        mine = lax.broadcasted_iota(jnp.int32, (NDEV, ADA_NC), 0) == me
        for l in range(DEPTH):
            for d in range(NDEV):
                modrow[:, d * ADA_NC:(d + 1) * ADA_NC] = jnp.sum(jnp.where(mine, gbuf[d, l], 0.0), axis=0, keepdims=True)
            full = modrow[...] + b_ref[l:l + 1, :]
            for w in range(3):
                mod_ref[l, w] = full[:, w * D:(w + 1) * D]

    return pl.pallas_call(
        body, name="ada_forward",
        out_shape=[jax.ShapeDtypeStruct((NDEV, 1, D), F32), jax.ShapeDtypeStruct((DEPTH, 3, 1, D), F32)],
        in_specs=[VMEM_FULL] * 3, out_specs=[VMEM_FULL] * 2,
        scratch_shapes=[pltpu.VMEM((NDEV, DEPTH, NDEV, ADA_NC), F32), pltpu.VMEM((1, 3 * D), F32),
                        pltpu.SemaphoreType.DMA((2, NDEV - 1)), pltpu.SemaphoreType.DMA((2, NDEV - 1))],
        compiler_params=_params(),
    )(c, ada_w, ada_b)


def _mod_spec(layer, which, ngrid):
    index = {1: lambda i: (layer, which, 0, 0), 2: lambda i, j: (layer, which, 0, 0)}[ngrid]
    return pl.BlockSpec((None, None, 1, D), index)


W_BLOCKS = 4


def _norm_proj(x, mod, norm_g3, wg, layer, name):
    nb = wg.shape[-1]
    tm = 1024
    wb = W_BLOCKS

    def body(x_ref, g_ref, shift_ref, scale_ref, w_ref, ht_ref, p_ref, h_ref):
        @pl.when(pl.program_id(1) == 0)
        def _():
            xv = x_ref[...]
            r = lax.rsqrt(jnp.mean(xv * xv, axis=-1, keepdims=True) + EPS)
            hn = xv * r * g_ref[...]
            h = hn * (1.0 + scale_ref[...]) + shift_ref[...]
            h_ref[...] = h.astype(BF16)
            ht_ref[...] = h.T.astype(BF16)

        hv = h_ref[...]
        for b in range(wb):
            p_ref[:, b * nb:(b + 1) * nb] = jnp.dot(hv, w_ref[b], preferred_element_type=F32).astype(BF16)

    return pl.pallas_call(
        body, name=name, grid=(S // tm, NDEV // wb),
        out_shape=[jax.ShapeDtypeStruct((D, S), BF16), jax.ShapeDtypeStruct((S, NDEV * nb), BF16)],
        in_specs=[pl.BlockSpec((tm, D), lambda i, d: (i, 0)),
                  pl.BlockSpec((None, 1, D), lambda i, d: (layer, 0, 0)),
                  _mod_spec(layer, 0, 2), _mod_spec(layer, 1, 2),
                  pl.BlockSpec((wb, D, nb), lambda i, d: (d, 0, 0))],
        out_specs=[pl.BlockSpec((D, tm), lambda i, d: (0, i)), pl.BlockSpec((tm, wb * nb), lambda i, d: (i, d))],
        scratch_shapes=[pltpu.VMEM((tm, D), BF16)],
        compiler_params=_params(("arbitrary", "arbitrary")),
    )(x, norm_g3, mod, mod, wg)


def _out_proj(ycat, w_out, x, mod, layer, name):
    tm = 512
    e = w_out.shape[0]

    def body(y_ref, w_ref, x_ref, gate_ref, xn_ref, o_ref):
        acc = jnp.dot(y_ref[...], w_ref[...], preferred_element_type=F32)
        o_ref[...] = acc.astype(BF16)
        xn_ref[...] = x_ref[...] + gate_ref[...] * acc

    return pl.pallas_call(
        body, name=name, grid=(S // tm,),
        out_shape=[jax.ShapeDtypeStruct((S, D), F32), jax.ShapeDtypeStruct((S, D), BF16)],
        in_specs=[pl.BlockSpec((tm, e), lambda i: (i, 0)), pl.BlockSpec((e, D), lambda i: (0, 0)),
                  pl.BlockSpec((tm, D), lambda i: (i, 0)), _mod_spec(layer, 2, 1)],
        out_specs=[pl.BlockSpec((tm, D), lambda i: (i, 0))] * 2,
        compiler_params=_params(("arbitrary",)),
    )(ycat, w_out, x, mod)


def _final_loss(x, target, final_g2):
    tm = 256

    def body(x_ref, t_ref, g_ref, dx_ref, loss_ref, dg_ref):
        @pl.when(pl.program_id(0) == 0)
        def _():
            loss_ref[...] = jnp.zeros_like(loss_ref)
            dg_ref[...] = jnp.zeros_like(dg_ref)

        xv, g = x_ref[...], g_ref[...]
        r = lax.rsqrt(jnp.mean(xv * xv, axis=-1, keepdims=True) + EPS)
        xn = xv * r
        err = xn * g - t_ref[...]
        loss_ref[...] += 0.5 * jnp.sum(jnp.mean(err * err, axis=-1, keepdims=True), axis=0, keepdims=True)
        dy = err * (1.0 / D)
        dg_ref[...] += jnp.sum(dy * xn, axis=0, keepdims=True)
        u = dy * g
        dx_ref[...] = r * (u - xn * jnp.mean(xn * u, axis=-1, keepdims=True))

    tile = pl.BlockSpec((tm, D), lambda i: (i, 0))
    row = pl.BlockSpec((1, D), lambda i: (0, 0))
    return pl.pallas_call(
        body, name="final_loss", grid=(S // tm,),
        out_shape=[jax.ShapeDtypeStruct((S, D), F32), jax.ShapeDtypeStruct((1, LANE), F32), jax.ShapeDtypeStruct((1, D), F32)],
        in_specs=[tile, tile, row], out_specs=[tile, pl.BlockSpec((1, LANE), lambda i: (0, 0)), row],
        compiler_params=_params(("arbitrary",)),
    )(x, target, final_g2)


def _out_bwd(dx, out, ycat, w_out, mod, layer, carried, name):
    tm = 512
    nsteps = S // tm
    e = ycat.shape[1]
    rb = e // NDEV
    nc = len(carried)

    def body(dx_ref, o_ref, y_ref, w_ref, gate_ref, *rest):
        dy_ref, gw_ref, dgate_ref = rest[nc:nc + 3]
        acc = rest[-1]
        step = pl.program_id(0)

        @pl.when(step == 0)
        def _():
            dgate_ref[...] = jnp.zeros_like(dgate_ref)
            acc[...] = jnp.zeros_like(acc)

        dxv = dx_ref[...]
        d_out = (gate_ref[...] * dxv).astype(BF16)
        dgate_ref[...] += jnp.sum(dxv * o_ref[...].astype(F32), axis=0, keepdims=True)
        dy_ref[...] = lax.dot_general(d_out, w_ref[...], (((1,), (1,)), ((), ())), preferred_element_type=F32).astype(BF16)
        acc[...] += lax.dot_general(y_ref[...], d_out, (((0,), (0,)), ((), ())), preferred_element_type=F32)

        @pl.when(step == nsteps - 1)
        def _():
            for d in range(NDEV):
                gw_ref[d % 2, d // 2] = acc[d * rb:(d + 1) * rb, :].astype(BF16)

    tile = pl.BlockSpec((tm, D), lambda i: (i, 0))
    wide = pl.BlockSpec((tm, e), lambda i: (i, 0))
    outs = pl.pallas_call(
        body, name=name, grid=(nsteps,),
        out_shape=[jax.ShapeDtypeStruct((S, e), BF16), jax.ShapeDtypeStruct((2, NDEV // 2, rb, D), BF16),
                   jax.ShapeDtypeStruct((1, D), F32)] + [jax.ShapeDtypeStruct(a.shape, a.dtype) for a in carried],
        in_specs=[tile, tile, wide, pl.BlockSpec((e, D), lambda i: (0, 0)), _mod_spec(layer, 2, 1)] + [ANY] * nc,
        out_specs=[wide, pl.BlockSpec((2, NDEV // 2, rb, D), lambda i: (0, 0, 0, 0)), pl.BlockSpec((1, D), lambda i: (0, 0))]
        + [ANY] * nc,
        scratch_shapes=[pltpu.VMEM((e, D), F32)],
        input_output_aliases={5 + k: 3 + k for k in range(nc)},
        compiler_params=_params(("arbitrary",)),
    )(dx, out, ycat, w_out, mod, *carried)
    return outs[0], outs[1], outs[2], list(outs[3:])


def _weight_grad(h_t, d_proj, name):
    nb = d_proj.shape[1] // NDEV

    def body(ht_ref, dp_ref, o_ref):
        o_ref[...] = jnp.dot(ht_ref[...], dp_ref[...], preferred_element_type=F32).astype(BF16)

    return pl.pallas_call(
        body, name=name, grid=(NDEV,), out_shape=jax.ShapeDtypeStruct((2, NDEV // 2, D, nb), BF16),
        in_specs=[pl.BlockSpec((D, S), lambda d: (0, 0)), pl.BlockSpec((S, nb), lambda d: (0, d))],
        out_specs=pl.BlockSpec((None, None, D, nb), lambda d: (d % 2, d // 2, 0, 0)),
        compiler_params=_params(("arbitrary",)),
    )(h_t, d_proj)


def _dh_norm_bwd(d_proj, wg, x, dx, mod, norm_g3, layer, carried, name):
    nb = wg.shape[-1]
    tm = 1024
    wb = W_BLOCKS // 2
    rc = 128

    def body(dp_ref, w_ref, x_ref, dx_ref, g_ref, scale_ref, carried_ref,
             dxi_ref, dshift_ref, dscale_ref, dg_ref, carried_out, acc):
        i, d = pl.program_id(0), pl.program_id(1)
        nt = (((1,), (1,)), ((), ()))
        part = lax.dot_general(dp_ref[:, :nb], w_ref[0], nt, preferred_element_type=F32)
        for b in range(1, wb):
            part += lax.dot_general(dp_ref[:, b * nb:(b + 1) * nb], w_ref[b], nt, preferred_element_type=F32)

        @pl.when(d == 0)
        def _():
            acc[...] = part

        @pl.when(d != 0)
        def _():
            acc[...] += part

        @pl.when(jnp.logical_and(i == 0, d == 0))
        def _():
            dshift_ref[...] = jnp.zeros_like(dshift_ref)
            dscale_ref[...] = jnp.zeros_like(dscale_ref)
            dg_ref[...] = jnp.zeros_like(dg_ref)

        @pl.when(d == NDEV // wb - 1)
        def _():
            g = g_ref[...]
            scale1 = 1.0 + scale_ref[...]

            def chunk(k, sums):
                rows = pl.ds(pl.multiple_of(k * rc, rc), rc)
                xv, dhv = x_ref[rows, :], acc[rows, :]
                r = lax.rsqrt(jnp.mean(xv * xv, axis=-1, keepdims=True) + EPS)
                xn = xv * r
                dhn = dhv * scale1
                u = dhn * g
                dxi_ref[rows, :] = dx_ref[rows, :] + r * (u - xn * jnp.mean(xn * u, axis=-1, keepdims=True))
                return (sums[0] + jnp.sum(dhv, axis=0, keepdims=True),
                        sums[1] + jnp.sum(dhv * (xn * g), axis=0, keepdims=True),
                        sums[2] + jnp.sum(dhn * xn, axis=0, keepdims=True))

            zero = jnp.zeros((1, D), F32)
            sums = lax.fori_loop(0, tm // rc, chunk, (zero, zero, zero))
            dshift_ref[...] += sums[0]
            dscale_ref[...] += sums[1]
            dg_ref[...] += sums[2]

    tile = pl.BlockSpec((tm, D), lambda i, d: (i, 0))
    row = pl.BlockSpec((1, D), lambda i, d: (0, 0))
    return pl.pallas_call(
        body, name=name, grid=(S // tm, NDEV // wb),
        out_shape=[jax.ShapeDtypeStruct((S, D), F32)] + [jax.ShapeDtypeStruct((1, D), F32)] * 3
        + [jax.ShapeDtypeStruct(carried.shape, carried.dtype)],
        in_specs=[pl.BlockSpec((tm, wb * nb), lambda i, d: (i, d)), pl.BlockSpec((wb, D, nb), lambda i, d: (d, 0, 0)),
                  tile, tile, pl.BlockSpec((None, 1, D), lambda i, d: (layer, 0, 0)), _mod_spec(layer, 1, 2), ANY],
        out_specs=[tile, row, row, row, ANY], scratch_shapes=[pltpu.VMEM((tm, D), F32)],
        input_output_aliases={6: 4}, compiler_params=_params(("arbitrary", "arbitrary")),
    )(d_proj, wg, x, dx, norm_g3, mod, carried)


TS = 256
NCH = TS // CHUNK
HALO_BLOCKS = TS // HALO


def _halo_before(width, col_block):
    return pl.BlockSpec((HALO, width), lambda i: (jnp.maximum(i * HALO_BLOCKS - 1, 0), col_block))


def _halo_after(width, col_block):
    return pl.BlockSpec((HALO, width), lambda i: (jnp.minimum((i + 1) * HALO_BLOCKS, S // HALO - 1), col_block))


def _shift_down(ext, k):
    return pltpu.roll(ext, k, 0)[HALO:]


def _shift_up(ext, k):
    return pltpu.roll(ext, ext.shape[0] - k, 0)[:ext.shape[0] - HALO]


def _layer_norm_head(v, lg, lb):
    mu = jnp.mean(v, axis=-1, keepdims=True)
    vc = v - mu
    rstd = lax.rsqrt(jnp.mean(vc * vc, axis=-1, keepdims=True) + EPS)
    vhat = vc * rstd
    return vhat, rstd, vhat * lg + lb


def _causal_mask():
    return lax.broadcasted_iota(jnp.int32, (CHUNK, CHUNK), 0) >= lax.broadcasted_iota(jnp.int32, (CHUNK, CHUNK), 1)


def _even_mix_fwd(proj, convw, ln_g3, ln_b3, sgu_w, sgu_bcol, wl, after, name):
    def body(pj_ref, hh_ref, hc_ref, cw_ref, lg_ref, lb_ref, sw_ref, sb_ref, *rest):
        y_ref = rest[-1]
        live = (pl.program_id(0) > 0).astype(F32)
        causal = _causal_mask()
        for j in range(E_A // HEAD):
            cols = slice(j * HEAD, (j + 1) * HEAD)
            w0, w1, w2 = cw_ref[0:1, cols], cw_ref[1:2, cols], cw_ref[2:3, cols]
            lg, lb = lg_ref[:, cols], lb_ref[:, cols]
            wm = jnp.where(causal, sw_ref[j], 0.0).astype(BF16)
            bias = sb_ref[j]

            def split(s, rows, cols=cols):
                return pj_ref[rows, s * E_A + cols.start:s * E_A + cols.stop].astype(F32)

            prev_tail = hc_ref[:, cols].astype(F32) * hh_ref[:, cols].astype(F32) * live
            for n in range(NCH):
                rows = slice(n * CHUNK, (n + 1) * CHUNK)
                p = split(2, rows) * split(0, rows)
                ext = jnp.concatenate([prev_tail, p], axis=0)
                prev_tail = p[CHUNK - HALO:]
                cv = w2 * p + w1 * _shift_down(ext, 1) + w0 * _shift_down(ext, 2)
                y_ref[rows, cols] = (split(1, rows) * cv * _silu(split(3, rows))).astype(BF16)
                _, _, vn = _layer_norm_head(split(5, rows), lg, lb)
                mixed = jnp.dot(wm, vn.astype(BF16), preferred_element_type=F32) + bias
                y_ref[rows, E_A + cols.start:E_A + cols.stop] = (split(4, rows) * mixed * _silu(split(6, rows))).astype(BF16)

    const3 = lambda i: (wl, 0, 0)
    const4 = lambda i: (wl, 0, 0, 0)
    return pl.pallas_call(
        body, name=name, grid=(S // TS,), out_shape=jax.ShapeDtypeStruct((S, 2 * E_A), BF16),
        in_specs=[pl.BlockSpec((TS, 7 * E_A), lambda i: (i, 0)), _halo_before(E_A, 0), _halo_before(E_A, 2),
                  pl.BlockSpec((None, 3, E_A), const3), pl.BlockSpec((None, 1, E_A), const3),
                  pl.BlockSpec((None, 1, E_A), const3), pl.BlockSpec((None, NDEV, CHUNK, CHUNK), const4),
                  pl.BlockSpec((None, NDEV, CHUNK, 1), const4)] + [ANY] * len(after),
        out_specs=pl.BlockSpec((TS, 2 * E_A), lambda i: (i, 0)),
        compiler_params=_params(("arbitrary",)),
    )(proj, proj, proj, convw, ln_g3, ln_b3, sgu_w, sgu_bcol, *after)


def _even_mix_bwd(proj, d_ycat, convw, ln_g3, ln_b3, sgu_w, sgu_bcol, wl, name):
    nsteps = S // TS

    def body(pj_ref, hh_ref, hc_ref, hb_ref, hz_ref, dy_ref, hdy_ref, cw_ref, lg_ref, lb_ref, sw_ref, sb_ref,
             dp_ref, dcw_ref, dlg_ref, dlb_ref, dsw_ref, dsb_ref):
        step = pl.program_id(0)

        @pl.when(step == 0)
        def _():
            for ref in (dcw_ref, dlg_ref, dlb_ref, dsw_ref, dsb_ref):
                ref[...] = jnp.zeros_like(ref)

        live_before = (step > 0).astype(F32)
        live_after = (step < nsteps - 1).astype(F32)
        causal = _causal_mask()
        for j in range(E_A // HEAD):
            cols = slice(j * HEAD, (j + 1) * HEAD)
            w0, w1, w2 = cw_ref[0:1, cols], cw_ref[1:2, cols], cw_ref[2:3, cols]
            lg, lb = lg_ref[:, cols], lb_ref[:, cols]
            wmf = jnp.where(causal, sw_ref[j], 0.0)
            wm, wmt = wmf.astype(BF16), wmf.T.astype(BF16)
            bias = sb_ref[j]

            def split(s, rows, cols=cols):
                return pj_ref[rows, s * E_A + cols.start:s * E_A + cols.stop].astype(F32)

            def put(s, rows, val, cols=cols):
                dp_ref[rows, s * E_A + cols.start:s * E_A + cols.stop] = val.astype(BF16)

            ps = [split(2, slice(n * CHUNK, (n + 1) * CHUNK)) * split(0, slice(n * CHUNK, (n + 1) * CHUNK)) for n in range(NCH)]
            next_head = (hdy_ref[:, cols].astype(F32) * hb_ref[:, cols].astype(F32) * _silu(hz_ref[:, cols].astype(F32))
                         * live_after)
            acc_w = [jnp.zeros((1, HEAD), F32) for _ in range(3)]
            for n in reversed(range(NCH)):
                rows = slice(n * CHUNK, (n + 1) * CHUNK)
                p = ps[n]
                tail = ps[n - 1][CHUNK - HALO:] if n > 0 else hc_ref[:, cols].astype(F32) * hh_ref[:, cols].astype(F32) * live_before
                ext = jnp.concatenate([tail, p], axis=0)
                p1, p2 = _shift_down(ext, 1), _shift_down(ext, 2)
                cv = w2 * p + w1 * p1 + w0 * p2
                a_b, a_z = split(1, rows), split(3, rows)
                sz, dsz = _silu_and_grad(a_z)
                dya = dy_ref[rows, cols].astype(F32)
                put(1, rows, dya * cv * sz)
                put(3, rows, dya * a_b * cv * dsz)
                gcv = dya * a_b * sz
                acc_w[0] += jnp.sum(gcv * p2, axis=0, keepdims=True)
                acc_w[1] += jnp.sum(gcv * p1, axis=0, keepdims=True)
                acc_w[2] += jnp.sum(gcv * p, axis=0, keepdims=True)
                gext = jnp.concatenate([gcv, next_head], axis=0)
                next_head = gcv[:HALO]
                dpv = w2 * gcv + w1 * _shift_up(gext, 1) + w0 * _shift_up(gext, 2)
                put(2, rows, dpv * split(0, rows))
                put(0, rows, dpv * split(2, rows))
            for k in range(3):
                dcw_ref[k:k + 1, cols] += acc_w[k]

            acc_lg, acc_lb = jnp.zeros((1, HEAD), F32), jnp.zeros((1, HEAD), F32)
            acc_sw, acc_sb = jnp.zeros((CHUNK, CHUNK), F32), jnp.zeros((CHUNK, 1), F32)
            for n in range(NCH):
                rows = slice(n * CHUNK, (n + 1) * CHUNK)
                u, z = split(4, rows), split(6, rows)
                vhat, rstd, vn = _layer_norm_head(split(5, rows), lg, lb)
                vn16 = vn.astype(BF16)
                mixed = jnp.dot(wm, vn16, preferred_element_type=F32) + bias
                sz, dsz = _silu_and_grad(z)
                dyb = dy_ref[rows, E_A + cols.start:E_A + cols.stop].astype(F32)
                put(4, rows, dyb * mixed * sz)
                put(6, rows, dyb * u * mixed * dsz)
                dmix = dyb * u * sz
                dmix16 = dmix.astype(BF16)
                acc_sb += jnp.sum(dmix, axis=1, keepdims=True)
                acc_sw += lax.dot_general(dmix16, vn16, (((1,), (1,)), ((), ())), preferred_element_type=F32)
                dvn = jnp.dot(wmt, dmix16, preferred_element_type=F32)
                acc_lg += jnp.sum(dvn * vhat, axis=0, keepdims=True)
                acc_lb += jnp.sum(dvn, axis=0, keepdims=True)
                dvh = dvn * lg
                put(5, rows, rstd * (dvh - jnp.mean(dvh, axis=-1, keepdims=True)
                                     - vhat * jnp.mean(dvh * vhat, axis=-1, keepdims=True)))
            dlg_ref[:, cols] += acc_lg
            dlb_ref[:, cols] += acc_lb
            dsw_ref[j] += jnp.where(causal, acc_sw, 0.0)
            dsb_ref[j] += acc_sb

    const3 = lambda i: (wl, 0, 0)
    const4 = lambda i: (wl, 0, 0, 0)
    fixed2 = lambda i: (0, 0)
    fixed3 = lambda i: (0, 0, 0)
    return pl.pallas_call(
        body, name=name, grid=(nsteps,),
        out_shape=[jax.ShapeDtypeStruct((S, 7 * E_A), BF16), jax.ShapeDtypeStruct((3, E_A), F32),
                   jax.ShapeDtypeStruct((1, E_A), F32), jax.ShapeDtypeStruct((1, E_A), F32),
                   jax.ShapeDtypeStruct((NDEV, CHUNK, CHUNK), F32), jax.ShapeDtypeStruct((NDEV, CHUNK, 1), F32)],
        in_specs=[pl.BlockSpec((TS, 7 * E_A), lambda i: (i, 0)), _halo_before(E_A, 0), _halo_before(E_A, 2),
                  _halo_after(E_A, 1), _halo_after(E_A, 3),
                  pl.BlockSpec((TS, 2 * E_A), lambda i: (i, 0)), _halo_after(E_A, 0),
                  pl.BlockSpec((None, 3, E_A), const3), pl.BlockSpec((None, 1, E_A), const3),
                  pl.BlockSpec((None, 1, E_A), const3), pl.BlockSpec((None, NDEV, CHUNK, CHUNK), const4),
                  pl.BlockSpec((None, NDEV, CHUNK, 1), const4)],
        out_specs=[pl.BlockSpec((TS, 7 * E_A), lambda i: (i, 0)), pl.BlockSpec((3, E_A), fixed2),
                   pl.BlockSpec((1, E_A), fixed2), pl.BlockSpec((1, E_A), fixed2),
                   pl.BlockSpec((NDEV, CHUNK, CHUNK), fixed3), pl.BlockSpec((NDEV, CHUNK, 1), fixed3)],
        compiler_params=_params(("arbitrary",)),
    )(proj, proj, proj, proj, proj, d_ycat, d_ycat, convw, ln_g3, ln_b3, sgu_w, sgu_bcol)


def _window_count(step, n, win, ext_before):
    rows = CHUNK if ext_before else CHUNK + HALO
    t = step * TS + n * CHUNK + lax.broadcasted_iota(jnp.int32, (rows, 1), 0)
    return jnp.minimum(t + 1, win).astype(F32)


def _pool_weight(wp_ref, g):
    return jnp.concatenate([wp_ref[d, g] for d in range(NDEV)], axis=0)


def _pooled_chunk(p, tail, win, count):
    sums = jnp.concatenate([tail, p], axis=0)
    shift = 1
    while shift < win:
        sums = sums + pltpu.roll(sums, shift, 0)
        shift *= 2
    return sums[HALO:] / count - p


def _pool_mix_fwd(proj, wpool, pscale4, wl, after, name):
    e_c = 4 * G_C

    def body(pj_ref, hp_ref, wp_ref, ps_ref, *rest):
        y_ref, pooled_scr, yraw_scr = rest[-3:]
        step = pl.program_id(0)
        live = (step > 0).astype(F32)
        for g, win in enumerate(POOL_WINDOWS):
            for q in range(G_C // LANE):
                cols = slice(g * G_C + q * LANE, g * G_C + (q + 1) * LANE)
                tail = hp_ref[:, cols].astype(F32) * live
                for n in range(NCH):
                    rows = slice(n * CHUNK, (n + 1) * CHUNK)
                    p = pj_ref[rows, cols].astype(F32)
                    pooled_scr[rows, q * LANE:(q + 1) * LANE] = _pooled_chunk(
                        p, tail, win, _window_count(step, n, win, True)).astype(BF16)
                    tail = p[CHUNK - HALO:]
            yraw_scr[...] = jnp.dot(pooled_scr[...], _pool_weight(wp_ref, g), preferred_element_type=F32)
            for q in range(G_C // LANE):
                cols = slice(g * G_C + q * LANE, g * G_C + (q + 1) * LANE)
                for n in range(NCH):
                    rows = slice(n * CHUNK, (n + 1) * CHUNK)
                    z = pj_ref[rows, e_c + cols.start:e_c + cols.stop].astype(F32)
                    y_ref[rows, cols] = (yraw_scr[rows, q * LANE:(q + 1) * LANE] * ps_ref[:, cols] * _silu(z)).astype(BF16)

    return pl.pallas_call(
        body, name=name, grid=(S // TS,), out_shape=jax.ShapeDtypeStruct((S, e_c), BF16),
        in_specs=[pl.BlockSpec((TS, 2 * e_c), lambda i: (i, 0)), _halo_before(e_c, 0),
                  pl.BlockSpec((NDEV, 4, G_C // NDEV, G_C), lambda i: (0, 0, 0, 0)),
                  pl.BlockSpec((None, 1, e_c), lambda i: (wl, 0, 0))] + [ANY] * len(after),
        out_specs=pl.BlockSpec((TS, e_c), lambda i: (i, 0)),
        scratch_shapes=[pltpu.VMEM((TS, G_C), BF16), pltpu.VMEM((TS, G_C), F32)],
        compiler_params=_params(("arbitrary",)),
    )(proj, proj, wpool, pscale4, *after)


def _pool_mix_bwd(proj, d_ycat, wpool, pscale4, wl, name):
    e_c = 4 * G_C
    nsteps = S // TS
    rb = G_C // NDEV

    def body(pj_ref, hp_ref, hz_ref, dy_ref, hdy_ref, wp_ref, ps_ref,
             dp_ref, dps_ref, dwp_ref, pooled_scr, yraw_scr, dyraw_scr, dpool_scr, acc_w):
        step = pl.program_id(0)

        @pl.when(step == 0)
        def _():
            dps_ref[...] = jnp.zeros_like(dps_ref)
            acc_w[...] = jnp.zeros_like(acc_w)

        live_before = (step > 0).astype(F32)
        live_after = (step < nsteps - 1).astype(F32)
        for g, win in enumerate(POOL_WINDOWS):
            weight = _pool_weight(wp_ref, g)
            for q in range(G_C // LANE):
                cols = slice(g * G_C + q * LANE, g * G_C + (q + 1) * LANE)
                tail = hp_ref[:, cols].astype(F32) * live_before
                for n in range(NCH):
                    rows = slice(n * CHUNK, (n + 1) * CHUNK)
                    p = pj_ref[rows, cols].astype(F32)
                    pooled_scr[rows, q * LANE:(q + 1) * LANE] = _pooled_chunk(
                        p, tail, win, _window_count(step, n, win, True)).astype(BF16)
                    tail = p[CHUNK - HALO:]
            yraw_scr[...] = jnp.dot(pooled_scr[...], weight, preferred_element_type=F32)
            for q in range(G_C // LANE):
                cols = slice(g * G_C + q * LANE, g * G_C + (q + 1) * LANE)
                local = slice(q * LANE, (q + 1) * LANE)
                scale = ps_ref[:, cols]
                acc_ps = jnp.zeros((1, LANE), F32)
                for n in range(NCH):
                    rows = slice(n * CHUNK, (n + 1) * CHUNK)
                    sz, dsz = _silu_and_grad(pj_ref[rows, e_c + cols.start:e_c + cols.stop].astype(F32))
                    dyv = dy_ref[rows, cols].astype(F32)
                    yraw = yraw_scr[rows, local]
                    dyraw_scr[rows, local] = (dyv * scale * sz).astype(BF16)
                    acc_ps += jnp.sum(dyv * yraw * sz, axis=0, keepdims=True)
                    dp_ref[rows, e_c + cols.start:e_c + cols.stop] = (dyv * yraw * scale * dsz).astype(BF16)
                dps_ref[:, cols] += acc_ps
                dyraw_scr[TS:, local] = (hdy_ref[:, cols].astype(F32) * scale * _silu(hz_ref[:, cols].astype(F32))
                                         * live_after).astype(BF16)
            dpool_scr[...] = lax.dot_general(dyraw_scr[...], weight, (((1,), (1,)), ((), ())), preferred_element_type=F32)
            acc_w[g] += lax.dot_general(pooled_scr[...], dyraw_scr[:TS, :], (((0,), (0,)), ((), ())),
                                        preferred_element_type=F32)
            for q in range(G_C // LANE):
                cols = slice(g * G_C + q * LANE, g * G_C + (q + 1) * LANE)
                local = slice(q * LANE, (q + 1) * LANE)
                for n in range(NCH):
                    rows = slice(n * CHUNK, (n + 1) * CHUNK)
                    ext = dpool_scr[n * CHUNK:(n + 1) * CHUNK + HALO, local]
                    sums = ext / _window_count(step, n, win, False)
                    shift = 1
                    while shift < win:
                        sums = sums + pltpu.roll(sums, CHUNK + HALO - shift, 0)
                        shift *= 2
                    dp_ref[rows, cols] = (sums[:CHUNK] - ext[:CHUNK]).astype(BF16)

        @pl.when(step == nsteps - 1)
        def _():
            for g in range(4):
                for d in range(NDEV):
                    dwp_ref[d % 2, d // 2, g] = acc_w[g, d * rb:(d + 1) * rb, :].astype(BF16)

    in_specs = [pl.BlockSpec((TS, 2 * e_c), lambda i: (i, 0)), _halo_before(e_c, 0), _halo_after(e_c, 1),
                pl.BlockSpec((TS, e_c), lambda i: (i, 0)), _halo_after(e_c, 0),
                pl.BlockSpec((NDEV, 4, rb, G_C), lambda i: (0, 0, 0, 0)),
                pl.BlockSpec((None, 1, e_c), lambda i: (wl, 0, 0))]
    args = [proj, proj, proj, d_ycat, d_ycat, wpool, pscale4]
    return pl.pallas_call(
        body, name=name, grid=(nsteps,),
        out_shape=[jax.ShapeDtypeStruct((S, 2 * e_c), BF16), jax.ShapeDtypeStruct((1, e_c), F32),
                   jax.ShapeDtypeStruct((2, NDEV // 2) + wpool.shape[1:], BF16)],
        in_specs=in_specs,
        out_specs=[pl.BlockSpec((TS, 2 * e_c), lambda i: (i, 0)), pl.BlockSpec((1, e_c), lambda i: (0, 0)),
                   pl.BlockSpec((2, NDEV // 2, 4, rb, G_C), lambda i: (0, 0, 0, 0, 0))],
        scratch_shapes=[pltpu.VMEM((TS, G_C), BF16), pltpu.VMEM((TS, G_C), F32), pltpu.VMEM((TS + HALO, G_C), BF16),
                        pltpu.VMEM((TS + HALO, G_C), F32), pltpu.VMEM((4, G_C, G_C), F32)],
        compiler_params=_params(("arbitrary",)),
    )(*args)


def _adamw(w, g, m, v):
    m = ADAM_B1 * m + (1.0 - ADAM_B1) * g
    v = ADAM_B2 * v + (1.0 - ADAM_B2) * jnp.square(g)
    m_hat = m / (1.0 - ADAM_B1 ** ADAM_STEP)
    v_hat = v / (1.0 - ADAM_B2 ** ADAM_STEP)
    delta = -ADAM_LR * (m_hat / (jnp.sqrt(v_hat) + ADAM_EPS) + ADAM_WD * w)
    return delta, m, v


def _adam_sharded(w, m, v, chip_parts, landed, my_chip, carried, name, first=0, into=()):
    _, nr, ncol = w.shape
    nl = len(chip_parts)
    tr = 128
    steps = nr // tr
    nc, ni = len(carried), len(into)

    def body(chip_ref, w_ref, m_ref, v_ref, *rest):
        parts, zones = rest[:nl], rest[nl:2 * nl]
        g_ref, d_ref, nm_ref, nv_ref = rest[2 * nl + nc + ni:2 * nl + nc + ni + 4]
        layer = pl.program_id(0)
        g = jnp.zeros((tr, ncol), F32)
        for l in range(nl):
            gl = parts[l][...].astype(F32)
            for q in range(3):
                gl = gl + zones[l][q].astype(F32)
            g = jnp.where(layer == l, gl, g)
        g_ref[...] = g
        d_ref[...], nm_ref[...], nv_ref[...] = _adamw(w_ref[...], g, m_ref[...], v_ref[...])

    def rows_of(l):
        return lambda layer, i, chip_ref: jnp.where(layer == l, i, jnp.where(layer < l, 0, steps - 1))

    spec = pl.BlockSpec((None, tr, ncol), lambda layer, i, chip_ref: (first + layer, i, 0))
    part_specs = [pl.BlockSpec((None, tr, ncol), lambda layer, i, chip_ref, l=l: (chip_ref[0], rows_of(l)(layer, i, chip_ref), 0))
                  for l in range(nl)]
    zone_specs = [pl.BlockSpec((3, tr, ncol), lambda layer, i, chip_ref, l=l: (0, rows_of(l)(layer, i, chip_ref), 0))
                  for l in range(nl)]
    grid_spec = pltpu.PrefetchScalarGridSpec(
        num_scalar_prefetch=1, grid=(nl, steps),
        in_specs=[spec, spec, spec] + part_specs + zone_specs + [ANY] * (nc + ni), out_specs=[spec] * 4 + [ANY] * nc)
    aliases = {4 + 2 * nl + k: 4 + k for k in range(nc)}
    aliases.update({4 + 2 * nl + nc + k: k for k in range(ni)})
    return pl.pallas_call(
        body, name=name, grid_spec=grid_spec,
        out_shape=[jax.ShapeDtypeStruct(w.shape, F32)] * 4 + [jax.ShapeDtypeStruct(a.shape, a.dtype) for a in carried],
        input_output_aliases=aliases, compiler_params=_params(("arbitrary", "arbitrary")),
    )(my_chip, w, m, v, *chip_parts, *landed, *carried, *into)


def _adam_small(w, g, m, v, name):
    def body(w_ref, g_ref, m_ref, v_ref, d_ref, nm_ref, nv_ref):
        d_ref[...], nm_ref[...], nv_ref[...] = _adamw(w_ref[...], g_ref[...], m_ref[...], v_ref[...])

    return pl.pallas_call(body, name=name, out_shape=[jax.ShapeDtypeStruct(w.shape, F32)] * 3,
                          in_specs=[VMEM_FULL] * 4, out_specs=[VMEM_FULL] * 3, compiler_params=_params())(w, g, m, v)


def _sum_devices(gathered, name):
    _, nr, ncol = gathered.shape

    def body(g_ref, o_ref):
        acc = g_ref[0].astype(F32)
        for s in range(1, NDEV):
            acc = acc + g_ref[s].astype(F32)
        o_ref[...] = acc

    return pl.pallas_call(body, name=name, grid=(1,), out_shape=jax.ShapeDtypeStruct((nr, ncol), F32),
                          in_specs=[pl.BlockSpec((NDEV, nr, ncol), lambda i: (0, 0, 0))],
                          out_specs=pl.BlockSpec((nr, ncol), lambda i: (0, 0)),
                          compiler_params=_params(("arbitrary",)))(gathered)


def _ada_weight_adam(cact_t, dmod_mine, w, m, v):
    def body(ct_ref, dm_ref, w_ref, m_ref, v_ref, g_ref, d_ref, nm_ref, nv_ref):
        ct, dm = ct_ref[...], dm_ref[...]
        g = ct[:, 0:1] * dm[0:1, :]
        for e in range(1, NDEV):
            g = g + ct[:, e:e + 1] * dm[e:e + 1, :]
        g_ref[...] = g
        d_ref[...], nm_ref[...], nv_ref[...] = _adamw(w_ref[...], g, m_ref[...], v_ref[...])

    spec = pl.BlockSpec((None, D, ADA_NC), lambda l: (l, 0, 0))
    return pl.pallas_call(
        body, name="ada_weight_adam", grid=(DEPTH,), out_shape=[jax.ShapeDtypeStruct(w.shape, F32)] * 4,
        in_specs=[pl.BlockSpec((D, NDEV), lambda l: (0, 0)), pl.BlockSpec((None, NDEV, ADA_NC), lambda l: (l, 0, 0)),
                  spec, spec, spec],
        out_specs=[spec] * 4, compiler_params=_params(("arbitrary",)),
    )(cact_t, dmod_mine, w, m, v)


def _pad_rows(a, rows):
    a = a.reshape(-1, D)
    return jnp.pad(a, ((0, rows - a.shape[0]), (0, 0)))


def kernel(x, c, norm_g, ada_w, ada_b, ab_w_in, ab_conv_w, ab_ln_g, ab_ln_b, ab_sgu_w, ab_sgu_b, ab_w_out, c_w_in, c_pool_w, c_pool_scale, c_w_out, final_g, loss_target, m_norm_g, m_ada_w, m_ada_b, m_ab_w_in, m_ab_conv_w, m_ab_ln_g, m_ab_ln_b, m_ab_sgu_w, m_ab_sgu_b, m_ab_w_out, m_c_w_in, m_c_pool_w, m_c_pool_scale, m_c_w_out, m_final_g, v_norm_g, v_ada_w, v_ada_b, v_ab_w_in, v_ab_conv_w, v_ab_ln_g, v_ab_ln_b, v_ab_sgu_w, v_ab_sgu_b, v_ab_w_out, v_c_w_in, v_c_pool_w, v_c_pool_scale, v_c_w_out, v_final_g):
    x_pos, y_pos, c_pos = _position()
    me = _index((x_pos, y_pos, c_pos))
    core = c_pos.astype(jnp.int32).reshape(1)
    my_chip = (2 * x_pos + y_pos).astype(jnp.int32).reshape(1)
    me1 = me.astype(jnp.int32).reshape(1)
    x0 = x.reshape(S, D)
    target = loss_target.reshape(S, D)
    norm_g3 = norm_g.reshape(DEPTH, 1, D)
    ln_g3, ln_b3 = ab_ln_g.reshape(2, 1, E_A), ab_ln_b.reshape(2, 1, E_A)
    sgu_bcol = ab_sgu_b.reshape(2, NDEV, CHUNK, 1)
    rb = G_C // NDEV
    pool_w3, m_pool_w3, v_pool_w3 = (a.reshape(2, 4 * rb, G_C) for a in (c_pool_w, m_c_pool_w, v_c_pool_w))

    cact_all, mod = _ada_forward(c, ada_w, ada_b)
    convw_all, pscale_all = _gather([ab_conv_w, c_pool_scale], "gather_small_weights")
    convw = jnp.transpose(convw_all, (1, 2, 0, 3)).reshape(2, 3, E_A)
    pscale4 = jnp.transpose(pscale_all, (1, 0, 2)).reshape(2, 1, 4 * G_C)
    zones = []
    for layer in range(DEPTH):
        wl = layer // 2
        if layer % 2 == 0:
            zones.append([_to_zone(ab_w_in, wl, me1, BF16, f"cast_w_in_{layer}"), _to_zone(ab_w_out, wl, me1, BF16, f"cast_w_out_{layer}")])
        else:
            zones.append([_to_zone(c_w_in, wl, me1, BF16, f"cast_w_in_{layer}"), _to_zone(c_w_out, wl, me1, BF16, f"cast_w_out_{layer}"),
                          _to_zone(pool_w3, wl, me1, BF16, f"cast_pool_w_{layer}")])

    def gathered(flight, after, layer):
        wg = _gather_end(flight, [after], f"gather_end_{layer}")
        return [wg[0], wg[1].reshape(-1, D)] + [w.reshape(NDEV, 4, rb, G_C) for w in wg[2:]]

    flight, (mod,) = _gather_start(zones[0], [convw_all], [mod], "gather_start_0")
    flight, (mod,) = _gather_mid(flight, [z for zs in zones[1:] for z in zs], [mod], "gather_mid_0")
    next_flight, (mod,) = _gather_start(zones[1], [], [mod], "gather_start_1")
    flight = _gather_late(flight, [mod], "gather_late_0")
    xs, hts, projs, ycats, outs, gathered_w = [x0], [], [], [], [], [gathered(flight, mod, 0)]
    for layer in range(DEPTH):
        wl = layer // 2
        even = layer % 2 == 0
        wg = gathered_w[layer]
        h_t, proj = _norm_proj(xs[-1], mod, norm_g3, wg[0], layer, f"norm_proj_{layer}")
        if layer + 1 < DEPTH:
            flight, (h_t,) = _gather_mid(next_flight, [], [h_t], f"gather_mid_{layer + 1}")
            if layer + 2 < DEPTH:
                next_flight, (h_t,) = _gather_start(zones[layer + 2], [], [h_t], f"gather_start_{layer + 2}")
        if even:
            ycat = _even_mix_fwd(proj, convw, ln_g3, ln_b3, ab_sgu_w, sgu_bcol, wl, [h_t], f"even_mix_fwd_{layer}")
        else:
            ycat = _pool_mix_fwd(proj, wg[2], pscale4, wl, [h_t], f"pool_mix_fwd_{layer}")
        if layer + 1 < DEPTH:
            flight = _gather_late(flight, [ycat], f"gather_late_{layer + 1}")
        x_new, out = _out_proj(ycat, wg[1], xs[-1], mod, layer, f"out_proj_{layer}")
        if layer + 1 < DEPTH:
            gathered_w.append(gathered(flight, x_new, layer + 1))
        xs.append(x_new)
        hts.append(h_t)
        projs.append(proj)
        ycats.append(ycat)
        outs.append(out)

    dx, loss_part, d_final_g = _final_loss(xs[DEPTH], target, final_g.reshape(1, D))

    d_mod, d_norm_g = [None] * DEPTH, [None] * DEPTH
    small, scatters, landed, res = {}, {}, {}, {}

    def finish_scatter(layer, after):
        send_sems, recv_sems, chip_parts, zones = scatters[layer]
        landed[layer] = _scatter_end(chip_parts, zones, send_sems, recv_sems, after, f"scatter_end_{layer}")

    def flat(a):
        return a.reshape(a.shape[0], -1, a.shape[-1])

    def sharded_adam(k, j, layers, w, m, v, carried, first=0, into=()):
        outs4 = _adam_sharded(w, m, v, [flat(landed[l][0][j]) for l in layers], [flat(landed[l][1][j]) for l in layers],
                              my_chip, carried, f"adam_{k}_{first}" if len(layers) < w.shape[0] else "adam_" + k, first, into)
        res[k] = [o.reshape(c_pool_w.shape) if k == "c_pool_w" else o for o in outs4[:4]]
        return list(outs4[4:])

    previous = None
    for layer in reversed(range(DEPTH)):
        wl = layer // 2
        even = layer % 2 == 0
        wg = gathered_w[layer]
        carried = [] if previous is None else [scatters[previous][2][0]]
        d_ycat, grad_out, d_gate, carried = _out_bwd(dx, outs[layer], ycats[layer], wg[1], mod, layer, carried, f"out_bwd_{layer}")
        if previous is not None:
            scatters[previous][2][0] = carried[0]
        parts = [None, grad_out]
        if even:
            d_proj, d_cw, d_lg, d_lb, d_sw, d_sb = _even_mix_bwd(
                projs[layer], d_ycat, convw, ln_g3, ln_b3, ab_sgu_w, sgu_bcol, wl, f"even_mix_bwd_{layer}")
            small[layer] = (d_cw, d_lg, d_lb, d_sw, d_sb)
        else:
            d_proj, d_ps, d_pool = _pool_mix_bwd(projs[layer], d_ycat, wg[2], pscale4, wl, f"pool_mix_bwd_{layer}")
            small[layer] = (d_ps,)
            parts.append(d_pool)
        parts[0] = _weight_grad(hts[layer], d_proj, f"grad_w_in_{layer}")
        pair_send, pair_recv, parts, from_sibling = _pair_start(parts, f"pair_start_{layer}")
        if layer > 0:
            dx, d_shift, d_scale, d_norm_g[layer], parts[0] = _dh_norm_bwd(
                d_proj, wg[0], xs[layer], dx, mod, norm_g3, layer, parts[0], f"dh_norm_bwd_{layer}")
            pair_after = dx
        else:
            finish_scatter(1, d_proj)
            finish_scatter(3, d_proj)
            parts[0], = sharded_adam("c_w_out", 1, (1, 3), c_w_out, m_c_w_out, v_c_w_out, [parts[0]])
            parts[0], = sharded_adam("c_pool_w", 2, (1, 3), pool_w3, m_pool_w3, v_pool_w3, [parts[0]])
            pair_after = res["c_pool_w"][0]
        parts, from_sibling = _pair_end(parts, from_sibling, pair_send, pair_recv, pair_after, f"pair_end_{layer}")
        chip_parts = [_pair_sum(p, q, core, f"pair_sum_{layer}_{j}") for j, (p, q) in enumerate(zip(parts, from_sibling))]
        send_sems, recv_sems, chip_parts, zones = _scatter_start(chip_parts, f"scatter_start_{layer}")
        if layer == 0:
            chip_parts[0], = sharded_adam("c_w_in", 0, (1, 3), c_w_in, m_c_w_in, v_c_w_in, [chip_parts[0]])
            dx, d_shift, d_scale, d_norm_g[layer], chip_parts[0] = _dh_norm_bwd(
                d_proj, wg[0], xs[layer], dx, mod, norm_g3, layer, chip_parts[0], f"dh_norm_bwd_{layer}")
        scatters[layer] = [send_sems, recv_sems, chip_parts, zones]
        previous = layer
        d_mod[layer] = jnp.concatenate([d_shift, d_scale, d_gate], axis=0)
    grad_x = dx.reshape(x.shape)

    sections = [("norm_g", jnp.concatenate(d_norm_g, axis=0), 8),
                ("d_mod", jnp.concatenate(d_mod, axis=0), 16),
                ("ab_ln_g", jnp.concatenate([small[0][1], small[2][1]], axis=0), 8),
                ("ab_ln_b", jnp.concatenate([small[0][2], small[2][2]], axis=0), 8),
                ("ab_sgu_b", jnp.stack([small[0][4], small[2][4]]), 8),
                ("final_g", d_final_g, 8),
                ("ab_conv_w", jnp.stack([small[0][0], small[2][0]]), 8),
                ("c_pool_scale", jnp.concatenate([small[1][0], small[3][0]], axis=0), 8),
                ("ab_sgu_w", jnp.stack([small[0][3], small[2][3]]), 256)]
    offsets, at = {}, 0
    for name, _, rows in sections:
        offsets[name] = (at, rows)
        at += rows
    packed = jnp.concatenate([_pad_rows(a, rows) for _, a, rows in sections] + [jnp.zeros((-at % 32, D), F32)], axis=0)
    loss_rows = jnp.pad(loss_part, ((0, 15), (0, D - LANE)))
    small_zones = [_to_zone(packed[None], 0, me1, BF16, "place_small_grads"), _to_zone(loss_rows[None], 0, me1, F32, "place_loss")]
    small_flight, (mod,) = _gather_start(small_zones, [], [mod], "gather_small_start")

    finish_scatter(2, mod)
    sharded_adam("ab_w_out", 1, (2,), ab_w_out, m_ab_w_out, v_ab_w_out, [], first=1)
    sharded_adam("ab_w_in", 0, (2,), ab_w_in, m_ab_w_in, v_ab_w_in, [], first=1)
    finish_scatter(0, res["ab_w_in"][0])
    sharded_adam("ab_w_out", 1, (0,), ab_w_out, m_ab_w_out, v_ab_w_out, [], into=res["ab_w_out"])
    small_flight, (mod,) = _gather_mid(small_flight, [res["ab_w_out"][0]], [mod], "gather_small_mid")
    sharded_adam("ab_w_in", 0, (0,), ab_w_in, m_ab_w_in, v_ab_w_in, [mod], into=res["ab_w_in"])

    last = res["ab_w_in"][0]
    small_flight = _gather_late(small_flight, [last], "gather_small_late")
    small_grads, losses = _gather_end(small_flight, [last], "gather_small_end")
    summed = _sum_devices(small_grads, "sum_small_grads")
    loss = _sum_devices(losses, "sum_loss")[0, 0]

    def section(name, nrows, src=summed):
        start = offsets[name][0]
        return src[..., start:start + nrows, :]

    grads = {
        "norm_g": section("norm_g", DEPTH),
        "ada_b": section("d_mod", 3 * DEPTH).reshape(DEPTH, 3 * D),
        "ab_ln_g": section("ab_ln_g", 2), "ab_ln_b": section("ab_ln_b", 2),
        "ab_sgu_b": section("ab_sgu_b", 2).reshape(ab_sgu_b.shape),
        "final_g": section("final_g", 1),
        "ab_sgu_w": section("ab_sgu_w", 256).reshape(ab_sgu_w.shape),
        "ab_conv_w": lax.dynamic_slice_in_dim(section("ab_conv_w", 6).reshape(2, 3, E_A), me * HEAD, HEAD, axis=2),
        "c_pool_scale": lax.dynamic_slice_in_dim(section("c_pool_scale", 4).reshape(2, 4 * G_C), me * 256, 256, axis=1),
    }
    small_w = {"norm_g": (norm_g, m_norm_g, v_norm_g), "ada_b": (ada_b, m_ada_b, v_ada_b),
               "ab_ln_g": (ab_ln_g, m_ab_ln_g, v_ab_ln_g), "ab_ln_b": (ab_ln_b, m_ab_ln_b, v_ab_ln_b),
               "ab_sgu_b": (ab_sgu_b, m_ab_sgu_b, v_ab_sgu_b),
               "final_g": (final_g.reshape(1, D), m_final_g.reshape(1, D), v_final_g.reshape(1, D)),
               "ab_sgu_w": (ab_sgu_w, m_ab_sgu_w, v_ab_sgu_w), "ab_conv_w": (ab_conv_w, m_ab_conv_w, v_ab_conv_w),
               "c_pool_scale": (c_pool_scale, m_c_pool_scale, v_c_pool_scale)}
    for k, (w, m, v) in small_w.items():
        res[k] = [grads[k]] + list(_adam_small(w, grads[k], m, v, "adam_" + k))
    res["final_g"] = [a.reshape(D) for a in res["final_g"]]

    dmod_all = section("d_mod", 3 * DEPTH, small_grads).reshape(NDEV, DEPTH, 3 * D)
    dmod_mine = jnp.transpose(lax.dynamic_slice_in_dim(dmod_all, me * ADA_NC, ADA_NC, axis=2), (1, 0, 2)).astype(F32)
    res["ada_w"] = _ada_weight_adam(jnp.transpose(cact_all.reshape(NDEV, D)), dmod_mine, ada_w, m_ada_w, v_ada_w)

    order = ["norm_g", "ada_w", "ada_b", "ab_w_in", "ab_conv_w", "ab_ln_g", "ab_ln_b", "ab_sgu_w", "ab_sgu_b",
             "ab_w_out", "c_w_in", "c_pool_w", "c_pool_scale", "c_w_out", "final_g"]
    return (loss, grad_x, *[res[k][0] for k in order], *[res[k][1] for k in order],
            *[res[k][2] for k in order], *[res[k][3] for k in order])
```

```python
import jax
import jax.numpy as jnp
from jax import lax
from jax.experimental import pallas as pl
from jax.experimental.pallas import tpu as pltpu

F32, BF16 = jnp.float32, jnp.bfloat16
S, D = 2048, 1024
NDEV = 8
DEPTH = 4
EPS = 1e-6
E_A = 1024
HEAD = 128
CHUNK = 128
POOL_WINDOWS = (2, 4, 8, 16)
G_C = 512
HALO = 16
ADA_NC = 384
MIB = 1024 * 1024
LANE = 128

ADAM_LR, ADAM_B1, ADAM_B2, ADAM_EPS, ADAM_WD, ADAM_STEP = 0.001, 0.9, 0.999, 1e-08, 0.01, 10

ANY = pl.BlockSpec(memory_space=pl.ANY)
VMEM_FULL = pl.BlockSpec(memory_space=pltpu.VMEM)
IN_HBM = pl.BlockSpec(memory_space=pltpu.HBM)
SEMAPHORES = pl.BlockSpec(memory_space=pltpu.SEMAPHORE)
IN_FLIGHT = pltpu.SideEffectType.DATAFLOW_SIDE_EFFECTING


V7X_VMEM_MIB = 64
VMEM_LIMIT_MIB = V7X_VMEM_MIB - 4


def _params(semantics=None):
    return pltpu.CompilerParams(dimension_semantics=semantics, vmem_limit_bytes=VMEM_LIMIT_MIB * MIB)


def _silu(z):
    return z * jax.nn.sigmoid(z)


def _silu_and_grad(z):
    sig = jax.nn.sigmoid(z)
    return z * sig, sig * (1.0 + z * (1.0 - sig))


def _position():
    return lax.axis_index("x"), lax.axis_index("y"), lax.axis_index("c")


def _index(pos):
    return 4 * pos[0] + 2 * pos[1] + pos[2]


def _peer(pos, k):
    flipped = tuple(1 - p if (k >> (2 - b)) & 1 else p for b, p in enumerate(pos))
    return flipped, _index(flipped)


def _remote(src, dst, send_sem, recv_sem, device):
    return pltpu.make_async_remote_copy(src_ref=src, dst_ref=dst, send_sem=send_sem, recv_sem=recv_sem,
                                        device_id=device, device_id_type=pl.DeviceIdType.MESH)


def _gather(arrays, name):
    n = len(arrays)
    out_shape = [jax.ShapeDtypeStruct((NDEV,) + a.shape, a.dtype) for a in arrays]

    def body(*refs):
        ins, outs = refs[:n], refs[n:2 * n]
        send_sems, recv_sems, own_sems = refs[2 * n:]
        x, y, c = _position()
        me = _index((x, y, c))
        sibling = (x, y, 1 - c)
        chips = [(1 - x, y), (x, 1 - y), (1 - x, 1 - y)]

        def block_copy(j, k, owner, to, src=None):
            rows = outs[j].at[_index(owner)]
            return _remote(rows if src is None else src, rows, send_sems.at[j, k], recv_sems.at[j, k], to)

        own, first, passed = [], [], []
        for j in range(n):
            own.append(pltpu.make_async_copy(ins[j], outs[j].at[me], own_sems.at[j]))
            first.append(block_copy(j, 0, (x, y, c), sibling, src=ins[j]))
            first += [block_copy(j, 1 + q, (x, y, c), (*chip, c), src=ins[j]) for q, chip in enumerate(chips)]
        for copy in own + first:
            copy.start()
        for q, chip in enumerate(chips):
            for j in range(n):
                block_copy(j, 1 + q, (*chip, c), (x, y, c)).wait_recv()
                forward = block_copy(j, 4 + q, (*chip, c), sibling)
                forward.start()
                passed.append(forward)
        for j in range(n):
            block_copy(j, 0, sibling, (x, y, c)).wait_recv()
            for q, chip in enumerate(chips):
                block_copy(j, 4 + q, (*chip, 1 - c), (x, y, c)).wait_recv()
        for copy in first + passed:
            copy.wait_send()
        for copy in own:
            copy.wait()

    return pl.pallas_call(
        body, name=name, out_shape=out_shape, in_specs=[ANY] * n, out_specs=[ANY] * n,
        scratch_shapes=[pltpu.SemaphoreType.DMA((n, NDEV - 1)), pltpu.SemaphoreType.DMA((n, NDEV - 1)),
                        pltpu.SemaphoreType.DMA((n,))],
    )(*arrays)


def _pair_start(parts, name):
    n = len(parts)
    lands = [_in_hbm(lax.empty(p.shape[1:], p.dtype)) for p in parts]

    def body(*refs):
        ins, zones = refs[:n], refs[n:2 * n]
        send_sems, recv_sems = refs[2 * n:2 * n + 2]
        x, y, c = _position()
        for j in range(n):
            _remote(ins[j].at[1 - c], zones[j], send_sems.at[j], recv_sems.at[j], (x, y, 1 - c)).start()

    outs = pl.pallas_call(
        body, name=name,
        out_shape=(pltpu.SemaphoreType.DMA((n,)), pltpu.SemaphoreType.DMA((n,)),
                   *[pltpu.HBM(p.shape, p.dtype) for p in parts], *[pltpu.HBM(z.shape, z.dtype) for z in lands]),
        in_specs=[IN_HBM] * (2 * n), out_specs=(SEMAPHORES, SEMAPHORES, *[IN_HBM] * (2 * n)),
        input_output_aliases={j: 2 + j for j in range(2 * n)},
        compiler_params=pltpu.CompilerParams(has_side_effects=IN_FLIGHT),
    )(*[_in_hbm(p) for p in parts], *lands)
    return outs[0], outs[1], list(outs[2:2 + n]), list(outs[2 + n:])


def _pair_end(parts, zones, send_sems, recv_sems, after, name):
    n = len(parts)

    def body(*refs):
        ins, zs = refs[:n], refs[n:2 * n]
        s, r = refs[2 * n:2 * n + 2]
        me = _position()
        for j in range(n):
            copy = _remote(ins[j].at[0], zs[j], s.at[j], r.at[j], me)
            copy.wait_send()
            copy.wait_recv()

    outs = pl.pallas_call(
        body, name=name,
        out_shape=(*[pltpu.HBM(p.shape, p.dtype) for p in parts], *[pltpu.HBM(z.shape, z.dtype) for z in zones]),
        in_specs=[IN_HBM] * (2 * n) + [SEMAPHORES, SEMAPHORES, ANY], out_specs=tuple([IN_HBM] * (2 * n)),
        input_output_aliases={j: j for j in range(2 * n)},
        compiler_params=pltpu.CompilerParams(has_side_effects=IN_FLIGHT),
    )(*parts, *zones, send_sems, recv_sems, after)
    return list(outs[:n]), list(outs[n:])


def _pair_sum(part, from_sibling, core, name):
    ncol = part.shape[-1]
    p3 = part.reshape(2, -1, ncol)
    q2 = from_sibling.reshape(-1, ncol)
    nrows = q2.shape[0]
    tr = 512

    def body(core_ref, p_ref, q_ref, o_ref):
        o_ref[...] = (p_ref[...].astype(F32) + q_ref[...].astype(F32)).astype(BF16)

    grid_spec = pltpu.PrefetchScalarGridSpec(
        num_scalar_prefetch=1, grid=(nrows // tr,),
        in_specs=[pl.BlockSpec((None, tr, ncol), lambda i, core_ref: (core_ref[0], i, 0)),
                  pl.BlockSpec((tr, ncol), lambda i, core_ref: (i, 0))],
        out_specs=pl.BlockSpec((tr, ncol), lambda i, core_ref: (i, 0)))
    out = pl.pallas_call(body, name=name, grid_spec=grid_spec, out_shape=jax.ShapeDtypeStruct(q2.shape, BF16),
                         compiler_params=_params(("arbitrary",)))(core, p3, q2)
    return out.reshape(from_sibling.shape)


def _in_hbm(a):
    return pltpu.with_memory_space_constraint(a, pltpu.HBM)


def _chips(x, y):
    return [(1 - x, y), (x, 1 - y), (1 - x, 1 - y)]


def _to_zone(a, wl, me, dtype, name):
    _, rows, cols = a.shape
    tr = 256 if rows % 256 == 0 else rows

    def body(me_ref, a_ref, o_ref):
        o_ref[...] = a_ref[...].astype(dtype)

    grid_spec = pltpu.PrefetchScalarGridSpec(
        num_scalar_prefetch=1, grid=(rows // tr,),
        in_specs=[pl.BlockSpec((None, tr, cols), lambda i, me_ref: (wl, i, 0))],
        out_specs=pl.BlockSpec((None, tr, cols), lambda i, me_ref: (me_ref[0], i, 0)))
    return pl.pallas_call(body, name=name, grid_spec=grid_spec, out_shape=jax.ShapeDtypeStruct((NDEV, rows, cols), dtype),
                          compiler_params=_params(("arbitrary",)))(me, a)


def _halves(block):
    rows = block.shape[0] // 2
    return block.at[pl.ds(0, rows)], block.at[pl.ds(rows, rows)]


def _around(x, y, c):
    return (x, y, 1 - c), (1 - x, y, c), (x, 1 - y, c), (1 - x, 1 - y, c)


def _gather_step1(zs, send, recv, pos):
    sibling, xn, yn, _ = _around(*pos)
    for j, z in enumerate(zs):
        mine = z.at[_index(pos)]
        for k, peer in enumerate((sibling, xn, yn)):
            _remote(mine, mine, send.at[3 * j + k], recv.at[3 * j + k], peer).start()


def _gather_step2(zs, recv1, send, recv, pos):
    sibling, xn, yn, _ = _around(*pos)
    for j, z in enumerate(zs):
        xb, yb = z.at[_index(xn)], z.at[_index(yn)]
        _remote(xb, xb, send.at[4 * j], recv1.at[3 * j + 1], pos).wait_recv()
        _remote(yb, yb, send.at[4 * j], recv1.at[3 * j + 2], pos).wait_recv()
        _remote(xb, xb, send.at[4 * j], recv.at[4 * j], sibling).start()
        _remote(yb, yb, send.at[4 * j + 1], recv.at[4 * j + 1], sibling).start()
        first, second = _halves(xb)[0], _halves(yb)[1]
        _remote(first, first, send.at[4 * j + 2], recv.at[4 * j + 2], yn).start()
        _remote(second, second, send.at[4 * j + 3], recv.at[4 * j + 3], xn).start()


def _gather_step3(zs, recv2, send, recv, pos):
    sibling, _, _, diagonal = _around(*pos)
    for j, z in enumerate(zs):
        db = z.at[_index(diagonal)]
        first, second = _halves(db)
        _remote(first, first, send.at[j], recv2.at[4 * j + 2], pos).wait_recv()
        _remote(second, second, send.at[j], recv2.at[4 * j + 3], pos).wait_recv()
        _remote(db, db, send.at[j], recv.at[j], sibling).start()


def _gather_step4(zs, send1, recv1, send2, recv2, send3, recv3, pos):
    x, y, c = pos
    sibling = (x, y, 1 - c)
    _, sx, sy, sd = _around(*sibling)
    for j, z in enumerate(zs):
        for owner, send, recv, k in ((sibling, send1, recv1, 3 * j), (sx, send2, recv2, 4 * j), (sy, send2, recv2, 4 * j + 1),
                                     (sd, send3, recv3, j)):
            block = z.at[_index(owner)]
            _remote(block, block, send.at[k], recv.at[k], pos).wait_recv()
    for j, z in enumerate(zs):
        block = z.at[0]
        half = _halves(block)[0]
        for ref, send, recv, k in ([(block, send1, recv1, 3 * j + k) for k in range(3)]
                                   + [(block, send2, recv2, 4 * j), (block, send2, recv2, 4 * j + 1),
                                      (half, send2, recv2, 4 * j + 2), (half, send2, recv2, 4 * j + 3), (block, send3, recv3, j)]):
            _remote(ref, ref, send.at[k], recv.at[k], pos).wait_send()


def _flight_call(step, name, zones, sems_in, nsems_out, after, carried):
    n, m, k = len(zones), len(carried), len(sems_in)

    def body(*refs):
        zs = refs[:n]
        given = refs[n + m:n + m + k]
        made = refs[n + m + k + len(after):n + m + k + len(after) + (2 if nsems_out else 0)]
        step(zs, *given, *made, _position())

    sem_out = (pltpu.SemaphoreType.DMA((nsems_out,)),) * 2 if nsems_out else ()
    outs = pl.pallas_call(
        body, name=name,
        out_shape=(*sem_out, *[pltpu.HBM(z.shape, z.dtype) for z in zones], *[jax.ShapeDtypeStruct(a.shape, a.dtype) for a in carried]),
        in_specs=[IN_HBM] * n + [ANY] * m + [SEMAPHORES] * k + [ANY] * len(after),
        out_specs=(*[SEMAPHORES] * len(sem_out), *[IN_HBM] * n, *[ANY] * m),
        input_output_aliases={j: len(sem_out) + j for j in range(n + m)},
        compiler_params=pltpu.CompilerParams(has_side_effects=IN_FLIGHT),
    )(*[_in_hbm(z) for z in zones], *carried, *sems_in, *after)
    sems = list(outs[:len(sem_out)])
    return sems, list(outs[len(sem_out):len(sem_out) + n]), list(outs[len(sem_out) + n:])


def _gather_start(zones, after, carried, name):
    (send1, recv1), zones, carried = _flight_call(_gather_step1, name, zones, [], 3 * len(zones), after, carried)
    return {"s1": send1, "r1": recv1, "zones": zones}, carried


def _gather_mid(flight, after, carried, name):
    step = lambda zs, recv1, send, recv, pos: _gather_step2(zs, recv1, send, recv, pos)
    (send2, recv2), zones, carried = _flight_call(step, name, flight["zones"], [flight["r1"]], 4 * len(flight["zones"]), after, carried)
    return {**flight, "s2": send2, "r2": recv2, "zones": zones}, carried


def _gather_late(flight, after, name):
    step = lambda zs, recv2, send, recv, pos: _gather_step3(zs, recv2, send, recv, pos)
    (send3, recv3), zones, _ = _flight_call(step, name, flight["zones"], [flight["r2"]], len(flight["zones"]), after, [])
    return {**flight, "s3": send3, "r3": recv3, "zones": zones}


def _gather_end(flight, after, name):
    sems = [flight[k] for k in ("s1", "r1", "s2", "r2", "s3", "r3")]
    _, zones, _ = _flight_call(_gather_step4, name, flight["zones"], sems, 0, after, [])
    return zones


def _scatter_start(parts, name):
    n = len(parts)
    lands = [_in_hbm(lax.empty((3,) + p.shape[1:], p.dtype)) for p in parts]

    def body(*refs):
        ins, zones = refs[:n], refs[n:2 * n]
        send_sems, recv_sems = refs[2 * n:2 * n + 2]
        x, y, c = _position()
        for j in range(n):
            for q, (px, py) in enumerate(_chips(x, y)):
                _remote(ins[j].at[2 * px + py], zones[j].at[q], send_sems.at[3 * j + q], recv_sems.at[3 * j + q],
                        (px, py, c)).start()

    outs = pl.pallas_call(
        body, name=name,
        out_shape=(pltpu.SemaphoreType.DMA((3 * n,)), pltpu.SemaphoreType.DMA((3 * n,)),
                   *[pltpu.HBM(p.shape, p.dtype) for p in parts], *[pltpu.HBM(z.shape, z.dtype) for z in lands]),
        in_specs=[IN_HBM] * (2 * n), out_specs=(SEMAPHORES, SEMAPHORES, *[IN_HBM] * (2 * n)),
        input_output_aliases={j: 2 + j for j in range(2 * n)},
        compiler_params=pltpu.CompilerParams(has_side_effects=IN_FLIGHT),
    )(*[_in_hbm(p) for p in parts], *lands)
    return outs[0], outs[1], list(outs[2:2 + n]), list(outs[2 + n:])


def _scatter_end(parts, zones, send_sems, recv_sems, after, name):
    n = len(parts)

    def body(*refs):
        ins, zs = refs[:n], refs[n:2 * n]
        s, r = refs[2 * n:2 * n + 2]
        me = _position()
        for j in range(n):
            for q in range(3):
                copy = _remote(ins[j].at[0], zs[j].at[q], s.at[3 * j + q], r.at[3 * j + q], me)
                copy.wait_send()
                copy.wait_recv()

    outs = pl.pallas_call(
        body, name=name,
        out_shape=(*[pltpu.HBM(p.shape, p.dtype) for p in parts], *[pltpu.HBM(z.shape, z.dtype) for z in zones]),
        in_specs=[IN_HBM] * (2 * n) + [SEMAPHORES, SEMAPHORES, ANY], out_specs=tuple([IN_HBM] * (2 * n)),
        input_output_aliases={j: j for j in range(2 * n)},
        compiler_params=pltpu.CompilerParams(has_side_effects=IN_FLIGHT),
    )(*parts, *zones, send_sems, recv_sems, after)
    return list(outs[:n]), list(outs[n:])


def _ada_forward(c, ada_w, ada_b):
    def body(c_ref, w_ref, b_ref, cact_ref, mod_ref, gbuf, modrow, send_sems, recv_sems):
        pos = _position()
        me = _index(pos)

        def to_all(ref, round_):
            copies = []
            for k in range(1, NDEV):
                peer, _ = _peer(pos, k)
                copy = pltpu.make_async_remote_copy(
                    src_ref=ref.at[me], dst_ref=ref.at[me], send_sem=send_sems.at[round_, k - 1],
                    recv_sem=recv_sems.at[round_, k - 1], device_id=peer, device_id_type=pl.DeviceIdType.MESH)
                copy.start()
                copies.append(copy)
            for copy in copies:
                copy.wait()

        cact_ref[me] = _silu(c_ref[...])
        to_all(cact_ref, 0)
        rows = lax.broadcasted_iota(jnp.int32, (NDEV, D), 0)
        cact = jnp.zeros((NDEV, D), F32)
        for e in range(NDEV):
            cact = jnp.where(rows == e, cact_ref[e], cact)
        cact = cact.astype(BF16)
        for l in range(DEPTH):
            gbuf[me, l] = jnp.dot(cact, w_ref[l].astype(BF16), preferred_element_type=F32)
        to_all(gbuf, 1)
        mine = lax.broadcasted_iota(jnp.int32, (NDEV, ADA_NC), 0) == me
        for l in range(DEPTH):
            for d in range(NDEV):
                modrow[:, d * ADA_NC:(d + 1) * ADA_NC] = jnp.sum(jnp.where(mine, gbuf[d, l], 0.0), axis=0, keepdims=True)
            full = modrow[...] + b_ref[l:l + 1, :]
            for w in range(3):
                mod_ref[l, w] = full[:, w * D:(w + 1) * D]

    return pl.pallas_call(
        body, name="ada_forward",
        out_shape=[jax.ShapeDtypeStruct((NDEV, 1, D), F32), jax.ShapeDtypeStruct((DEPTH, 3, 1, D), F32)],
        in_specs=[VMEM_FULL] * 3, out_specs=[VMEM_FULL] * 2,
        scratch_shapes=[pltpu.VMEM((NDEV, DEPTH, NDEV, ADA_NC), F32), pltpu.VMEM((1, 3 * D), F32),
                        pltpu.SemaphoreType.DMA((2, NDEV - 1)), pltpu.SemaphoreType.DMA((2, NDEV - 1))],
        compiler_params=_params(),
    )(c, ada_w, ada_b)


def _mod_spec(layer, which, ngrid):
    index = {1: lambda i: (layer, which, 0, 0), 2: lambda i, j: (layer, which, 0, 0)}[ngrid]
    return pl.BlockSpec((None, None, 1, D), index)


W_BLOCKS = 4


def _norm_proj(x, mod, norm_g3, wg, layer, name):
    nb = wg.shape[-1]
    tm = 1024
    wb = W_BLOCKS

    def body(x_ref, g_ref, shift_ref, scale_ref, w_ref, ht_ref, p_ref, h_ref):
        @pl.when(pl.program_id(1) == 0)
        def _():
            xv = x_ref[...]
            r = lax.rsqrt(jnp.mean(xv * xv, axis=-1, keepdims=True) + EPS)
            hn = xv * r * g_ref[...]
            h = hn * (1.0 + scale_ref[...]) + shift_ref[...]
            h_ref[...] = h.astype(BF16)
            ht_ref[...] = h.T.astype(BF16)

        hv = h_ref[...]
        for b in range(wb):
            p_ref[:, b * nb:(b + 1) * nb] = jnp.dot(hv, w_ref[b], preferred_element_type=F32).astype(BF16)

    return pl.pallas_call(
        body, name=name, grid=(S // tm, NDEV // wb),
        out_shape=[jax.ShapeDtypeStruct((D, S), BF16), jax.ShapeDtypeStruct((S, NDEV * nb), BF16)],
        in_specs=[pl.BlockSpec((tm, D), lambda i, d: (i, 0)),
                  pl.BlockSpec((None, 1, D), lambda i, d: (layer, 0, 0)),
                  _mod_spec(layer, 0, 2), _mod_spec(layer, 1, 2),
                  pl.BlockSpec((wb, D, nb), lambda i, d: (d, 0, 0))],
        out_specs=[pl.BlockSpec((D, tm), lambda i, d: (0, i)), pl.BlockSpec((tm, wb * nb), lambda i, d: (i, d))],
        scratch_shapes=[pltpu.VMEM((tm, D), BF16)],
        compiler_params=_params(("arbitrary", "arbitrary")),
    )(x, norm_g3, mod, mod, wg)


def _out_proj(ycat, w_out, x, mod, layer, name, final=None):
    tm = 512
    e = w_out.shape[0]

    def body(y_ref, w_ref, x_ref, gate_ref, *rest):
        acc = jnp.dot(y_ref[...], w_ref[...], preferred_element_type=F32)
        xv = x_ref[...] + gate_ref[...] * acc
        if final is None:
            xn_ref, o_ref = rest
            o_ref[...] = acc.astype(BF16)
            xn_ref[...] = xv
            return
        t_ref, g_ref, dx_ref, o_ref, loss_ref, dg_ref = rest
        o_ref[...] = acc.astype(BF16)

        @pl.when(pl.program_id(0) == 0)
        def _():
            loss_ref[...] = jnp.zeros_like(loss_ref)
            dg_ref[...] = jnp.zeros_like(dg_ref)

        g = g_ref[...]
        r = lax.rsqrt(jnp.mean(xv * xv, axis=-1, keepdims=True) + EPS)
        xn = xv * r
        err = xn * g - t_ref[...]
        loss_ref[...] += 0.5 * jnp.sum(jnp.mean(err * err, axis=-1, keepdims=True), axis=0, keepdims=True)
        dy = err * (1.0 / D)
        dg_ref[...] += jnp.sum(dy * xn, axis=0, keepdims=True)
        u = dy * g
        dx_ref[...] = r * (u - xn * jnp.mean(xn * u, axis=-1, keepdims=True))

    tile = pl.BlockSpec((tm, D), lambda i: (i, 0))
    row = pl.BlockSpec((1, D), lambda i: (0, 0))
    out_shape = [jax.ShapeDtypeStruct((S, D), F32), jax.ShapeDtypeStruct((S, D), BF16)]
    in_specs = [pl.BlockSpec((tm, e), lambda i: (i, 0)), pl.BlockSpec((e, D), lambda i: (0, 0)), tile, _mod_spec(layer, 2, 1)]
    out_specs, operands = [tile, tile], [ycat, w_out, x, mod]
    if final is not None:
        out_shape += [jax.ShapeDtypeStruct((1, LANE), F32), jax.ShapeDtypeStruct((1, D), F32)]
        in_specs += [tile, row]
        out_specs += [pl.BlockSpec((1, LANE), lambda i: (0, 0)), row]
        operands += list(final)
    return pl.pallas_call(
        body, name=name, grid=(S // tm,), out_shape=out_shape, in_specs=in_specs, out_specs=out_specs,
        compiler_params=_params(("arbitrary",)),
    )(*operands)


def _out_bwd(dx, out, ycat, w_out, mod, layer, carried, name):
    tm = 512
    nsteps = S // tm
    e = ycat.shape[1]
    rb = e // NDEV
    nc = len(carried)

    def body(dx_ref, o_ref, y_ref, w_ref, gate_ref, *rest):
        dy_ref, gw_ref, dgate_ref = rest[nc:nc + 3]
        acc = rest[-1]
        step = pl.program_id(0)

        @pl.when(step == 0)
        def _():
            dgate_ref[...] = jnp.zeros_like(dgate_ref)
            acc[...] = jnp.zeros_like(acc)

        dxv = dx_ref[...]
        d_out = (gate_ref[...] * dxv).astype(BF16)
        dgate_ref[...] += jnp.sum(dxv * o_ref[...].astype(F32), axis=0, keepdims=True)
        dy_ref[...] = lax.dot_general(d_out, w_ref[...], (((1,), (1,)), ((), ())), preferred_element_type=F32).astype(BF16)
        acc[...] += lax.dot_general(y_ref[...], d_out, (((0,), (0,)), ((), ())), preferred_element_type=F32)

        @pl.when(step == nsteps - 1)
        def _():
            for d in range(NDEV):
                gw_ref[d % 2, d // 2] = acc[d * rb:(d + 1) * rb, :].astype(BF16)

    tile = pl.BlockSpec((tm, D), lambda i: (i, 0))
    wide = pl.BlockSpec((tm, e), lambda i: (i, 0))
    outs = pl.pallas_call(
        body, name=name, grid=(nsteps,),
        out_shape=[jax.ShapeDtypeStruct((S, e), BF16), jax.ShapeDtypeStruct((2, NDEV // 2, rb, D), BF16),
                   jax.ShapeDtypeStruct((1, D), F32)] + [jax.ShapeDtypeStruct(a.shape, a.dtype) for a in carried],
        in_specs=[tile, tile, wide, pl.BlockSpec((e, D), lambda i: (0, 0)), _mod_spec(layer, 2, 1)] + [ANY] * nc,
        out_specs=[wide, pl.BlockSpec((2, NDEV // 2, rb, D), lambda i: (0, 0, 0, 0)), pl.BlockSpec((1, D), lambda i: (0, 0))]
        + [ANY] * nc,
        scratch_shapes=[pltpu.VMEM((e, D), F32)],
        input_output_aliases={5 + k: 3 + k for k in range(nc)},
        compiler_params=_params(("arbitrary",)),
    )(dx, out, ycat, w_out, mod, *carried)
    return outs[0], outs[1], outs[2], list(outs[3:])


def _weight_grad(h_t, d_proj, name):
    nb = d_proj.shape[1] // NDEV

    def body(ht_ref, dp_ref, o_ref):
        o_ref[...] = jnp.dot(ht_ref[...], dp_ref[...], preferred_element_type=F32).astype(BF16)

    return pl.pallas_call(
        body, name=name, grid=(NDEV,), out_shape=jax.ShapeDtypeStruct((2, NDEV // 2, D, nb), BF16),
        in_specs=[pl.BlockSpec((D, S), lambda d: (0, 0)), pl.BlockSpec((S, nb), lambda d: (0, d))],
        out_specs=pl.BlockSpec((None, None, D, nb), lambda d: (d % 2, d // 2, 0, 0)),
        compiler_params=_params(("arbitrary",)),
    )(h_t, d_proj)


def _dh_norm_bwd(d_proj, wg, x, dx, mod, norm_g3, layer, carried, name):
    nb = wg.shape[-1]
    tm = 512
    wb = W_BLOCKS
    rc = 128

    def body(dp_ref, w_ref, x_ref, dx_ref, g_ref, scale_ref, carried_ref,
             dxi_ref, dshift_ref, dscale_ref, dg_ref, carried_out, acc):
        i, d = pl.program_id(0), pl.program_id(1)
        nt = (((1,), (1,)), ((), ()))
        part = lax.dot_general(dp_ref[:, :nb], w_ref[0], nt, preferred_element_type=F32)
        for b in range(1, wb):
            part += lax.dot_general(dp_ref[:, b * nb:(b + 1) * nb], w_ref[b], nt, preferred_element_type=F32)

        @pl.when(d == 0)
        def _():
            acc[...] = part

        @pl.when(d != 0)
        def _():
            acc[...] += part

        @pl.when(jnp.logical_and(i == 0, d == 0))
        def _():
            dshift_ref[...] = jnp.zeros_like(dshift_ref)
            dscale_ref[...] = jnp.zeros_like(dscale_ref)
            dg_ref[...] = jnp.zeros_like(dg_ref)

        @pl.when(d == NDEV // wb - 1)
        def _():
            g = g_ref[...]
            scale1 = 1.0 + scale_ref[...]

            def chunk(k, sums):
                rows = pl.ds(pl.multiple_of(k * rc, rc), rc)
                xv, dhv = x_ref[rows, :], acc[rows, :]
                r = lax.rsqrt(jnp.mean(xv * xv, axis=-1, keepdims=True) + EPS)
                xn = xv * r
                dhn = dhv * scale1
                u = dhn * g
                dxi_ref[rows, :] = dx_ref[rows, :] + r * (u - xn * jnp.mean(xn * u, axis=-1, keepdims=True))
                return (sums[0] + jnp.sum(dhv, axis=0, keepdims=True),
                        sums[1] + jnp.sum(dhv * (xn * g), axis=0, keepdims=True),
                        sums[2] + jnp.sum(dhn * xn, axis=0, keepdims=True))

            zero = jnp.zeros((1, D), F32)
            sums = lax.fori_loop(0, tm // rc, chunk, (zero, zero, zero))
            dshift_ref[...] += sums[0]
            dscale_ref[...] += sums[1]
            dg_ref[...] += sums[2]

    tile = pl.BlockSpec((tm, D), lambda i, d: (i, 0))
    row = pl.BlockSpec((1, D), lambda i, d: (0, 0))
    return pl.pallas_call(
        body, name=name, grid=(S // tm, NDEV // wb),
        out_shape=[jax.ShapeDtypeStruct((S, D), F32)] + [jax.ShapeDtypeStruct((1, D), F32)] * 3
        + [jax.ShapeDtypeStruct(carried.shape, carried.dtype)],
        in_specs=[pl.BlockSpec((tm, wb * nb), lambda i, d: (i, d)), pl.BlockSpec((wb, D, nb), lambda i, d: (d, 0, 0)),
                  tile, tile, pl.BlockSpec((None, 1, D), lambda i, d: (layer, 0, 0)), _mod_spec(layer, 1, 2), ANY],
        out_specs=[tile, row, row, row, ANY], scratch_shapes=[pltpu.VMEM((tm, D), F32)],
        input_output_aliases={6: 4}, compiler_params=_params(("arbitrary", "arbitrary")),
    )(d_proj, wg, x, dx, norm_g3, mod, carried)


TS = 256
NCH = TS // CHUNK
HALO_BLOCKS = TS // HALO


def _halo_before(width, col_block):
    return pl.BlockSpec((HALO, width), lambda i: (jnp.maximum(i * HALO_BLOCKS - 1, 0), col_block))


def _halo_after(width, col_block):
    return pl.BlockSpec((HALO, width), lambda i: (jnp.minimum((i + 1) * HALO_BLOCKS, S // HALO - 1), col_block))


def _shift_down(ext, k):
    return pltpu.roll(ext, k, 0)[HALO:]


def _shift_up(ext, k):
    return pltpu.roll(ext, ext.shape[0] - k, 0)[:ext.shape[0] - HALO]


def _layer_norm_head(v, lg, lb):
    mu = jnp.mean(v, axis=-1, keepdims=True)
    vc = v - mu
    rstd = lax.rsqrt(jnp.mean(vc * vc, axis=-1, keepdims=True) + EPS)
    vhat = vc * rstd
    return vhat, rstd, vhat * lg + lb


def _causal_mask():
    return lax.broadcasted_iota(jnp.int32, (CHUNK, CHUNK), 0) >= lax.broadcasted_iota(jnp.int32, (CHUNK, CHUNK), 1)


def _even_mix_fwd(proj, convw, ln_g3, ln_b3, sgu_w, sgu_bcol, wl, after, name):
    def body(pj_ref, hh_ref, hc_ref, cw_ref, lg_ref, lb_ref, sw_ref, sb_ref, *rest):
        y_ref = rest[-1]
        live = (pl.program_id(0) > 0).astype(F32)
        causal = _causal_mask()
        for j in range(E_A // HEAD):
            cols = slice(j * HEAD, (j + 1) * HEAD)
            w0, w1, w2 = cw_ref[0:1, cols], cw_ref[1:2, cols], cw_ref[2:3, cols]
            lg, lb = lg_ref[:, cols], lb_ref[:, cols]
            wm = jnp.where(causal, sw_ref[j], 0.0).astype(BF16)
            bias = sb_ref[j]

            def split(s, rows, cols=cols):
                return pj_ref[rows, s * E_A + cols.start:s * E_A + cols.stop].astype(F32)

            prev_tail = hc_ref[:, cols].astype(F32) * hh_ref[:, cols].astype(F32) * live
            for n in range(NCH):
                rows = slice(n * CHUNK, (n + 1) * CHUNK)
                p = split(2, rows) * split(0, rows)
                ext = jnp.concatenate([prev_tail, p], axis=0)
                prev_tail = p[CHUNK - HALO:]
                cv = w2 * p + w1 * _shift_down(ext, 1) + w0 * _shift_down(ext, 2)
                y_ref[rows, cols] = (split(1, rows) * cv * _silu(split(3, rows))).astype(BF16)
                _, _, vn = _layer_norm_head(split(5, rows), lg, lb)
                mixed = jnp.dot(wm, vn.astype(BF16), preferred_element_type=F32) + bias
                y_ref[rows, E_A + cols.start:E_A + cols.stop] = (split(4, rows) * mixed * _silu(split(6, rows))).astype(BF16)

    const3 = lambda i: (wl, 0, 0)
    const4 = lambda i: (wl, 0, 0, 0)
    return pl.pallas_call(
        body, name=name, grid=(S // TS,), out_shape=jax.ShapeDtypeStruct((S, 2 * E_A), BF16),
        in_specs=[pl.BlockSpec((TS, 7 * E_A), lambda i: (i, 0)), _halo_before(E_A, 0), _halo_before(E_A, 2),
                  pl.BlockSpec((None, 3, E_A), const3), pl.BlockSpec((None, 1, E_A), const3),
                  pl.BlockSpec((None, 1, E_A), const3), pl.BlockSpec((None, NDEV, CHUNK, CHUNK), const4),
                  pl.BlockSpec((None, NDEV, CHUNK, 1), const4)] + [ANY] * len(after),
        out_specs=pl.BlockSpec((TS, 2 * E_A), lambda i: (i, 0)),
        compiler_params=_params(("arbitrary",)),
    )(proj, proj, proj, convw, ln_g3, ln_b3, sgu_w, sgu_bcol, *after)


def _even_mix_bwd(proj, d_ycat, convw, ln_g3, ln_b3, sgu_w, sgu_bcol, wl, name):
    nsteps = S // TS

    def body(pj_ref, hh_ref, hc_ref, hb_ref, hz_ref, dy_ref, hdy_ref, cw_ref, lg_ref, lb_ref, sw_ref, sb_ref,
             dp_ref, dcw_ref, dlg_ref, dlb_ref, dsw_ref, dsb_ref):
        step = pl.program_id(0)

        @pl.when(step == 0)
        def _():
            for ref in (dcw_ref, dlg_ref, dlb_ref, dsw_ref, dsb_ref):
                ref[...] = jnp.zeros_like(ref)

        live_before = (step > 0).astype(F32)
        live_after = (step < nsteps - 1).astype(F32)
        causal = _causal_mask()
        for j in range(E_A // HEAD):
            cols = slice(j * HEAD, (j + 1) * HEAD)
            w0, w1, w2 = cw_ref[0:1, cols], cw_ref[1:2, cols], cw_ref[2:3, cols]
            lg, lb = lg_ref[:, cols], lb_ref[:, cols]
            wmf = jnp.where(causal, sw_ref[j], 0.0)
            wm, wmt = wmf.astype(BF16), wmf.T.astype(BF16)
            bias = sb_ref[j]

            def split(s, rows, cols=cols):
                return pj_ref[rows, s * E_A + cols.start:s * E_A + cols.stop].astype(F32)

            def put(s, rows, val, cols=cols):
                dp_ref[rows, s * E_A + cols.start:s * E_A + cols.stop] = val.astype(BF16)

            ps = [split(2, slice(n * CHUNK, (n + 1) * CHUNK)) * split(0, slice(n * CHUNK, (n + 1) * CHUNK)) for n in range(NCH)]
            next_head = (hdy_ref[:, cols].astype(F32) * hb_ref[:, cols].astype(F32) * _silu(hz_ref[:, cols].astype(F32))
                         * live_after)
            acc_w = [jnp.zeros((1, HEAD), F32) for _ in range(3)]
            for n in reversed(range(NCH)):
                rows = slice(n * CHUNK, (n + 1) * CHUNK)
                p = ps[n]
                tail = ps[n - 1][CHUNK - HALO:] if n > 0 else hc_ref[:, cols].astype(F32) * hh_ref[:, cols].astype(F32) * live_before
                ext = jnp.concatenate([tail, p], axis=0)
                p1, p2 = _shift_down(ext, 1), _shift_down(ext, 2)
                cv = w2 * p + w1 * p1 + w0 * p2
                a_b, a_z = split(1, rows), split(3, rows)
                sz, dsz = _silu_and_grad(a_z)
                dya = dy_ref[rows, cols].astype(F32)
                put(1, rows, dya * cv * sz)
                put(3, rows, dya * a_b * cv * dsz)
                gcv = dya * a_b * sz
                acc_w[0] += jnp.sum(gcv * p2, axis=0, keepdims=True)
                acc_w[1] += jnp.sum(gcv * p1, axis=0, keepdims=True)
                acc_w[2] += jnp.sum(gcv * p, axis=0, keepdims=True)
                gext = jnp.concatenate([gcv, next_head], axis=0)
                next_head = gcv[:HALO]
                dpv = w2 * gcv + w1 * _shift_up(gext, 1) + w0 * _shift_up(gext, 2)
                put(2, rows, dpv * split(0, rows))
                put(0, rows, dpv * split(2, rows))
            for k in range(3):
                dcw_ref[k:k + 1, cols] += acc_w[k]

            acc_lg, acc_lb = jnp.zeros((1, HEAD), F32), jnp.zeros((1, HEAD), F32)
            acc_sw, acc_sb = jnp.zeros((CHUNK, CHUNK), F32), jnp.zeros((CHUNK, 1), F32)
            for n in range(NCH):
                rows = slice(n * CHUNK, (n + 1) * CHUNK)
                u, z = split(4, rows), split(6, rows)
                vhat, rstd, vn = _layer_norm_head(split(5, rows), lg, lb)
                vn16 = vn.astype(BF16)
                mixed = jnp.dot(wm, vn16, preferred_element_type=F32) + bias
                sz, dsz = _silu_and_grad(z)
                dyb = dy_ref[rows, E_A + cols.start:E_A + cols.stop].astype(F32)
                put(4, rows, dyb * mixed * sz)
                put(6, rows, dyb * u * mixed * dsz)
                dmix = dyb * u * sz
                dmix16 = dmix.astype(BF16)
                acc_sb += jnp.sum(dmix, axis=1, keepdims=True)
                acc_sw += lax.dot_general(dmix16, vn16, (((1,), (1,)), ((), ())), preferred_element_type=F32)
                dvn = jnp.dot(wmt, dmix16, preferred_element_type=F32)
                acc_lg += jnp.sum(dvn * vhat, axis=0, keepdims=True)
                acc_lb += jnp.sum(dvn, axis=0, keepdims=True)
                dvh = dvn * lg
                put(5, rows, rstd * (dvh - jnp.mean(dvh, axis=-1, keepdims=True)
                                     - vhat * jnp.mean(dvh * vhat, axis=-1, keepdims=True)))
            dlg_ref[:, cols] += acc_lg
            dlb_ref[:, cols] += acc_lb
            dsw_ref[j] += jnp.where(causal, acc_sw, 0.0)
            dsb_ref[j] += acc_sb

    const3 = lambda i: (wl, 0, 0)
    const4 = lambda i: (wl, 0, 0, 0)
    fixed2 = lambda i: (0, 0)
    fixed3 = lambda i: (0, 0, 0)
    return pl.pallas_call(
        body, name=name, grid=(nsteps,),
        out_shape=[jax.ShapeDtypeStruct((S, 7 * E_A), BF16), jax.ShapeDtypeStruct((3, E_A), F32),
                   jax.ShapeDtypeStruct((1, E_A), F32), jax.ShapeDtypeStruct((1, E_A), F32),
                   jax.ShapeDtypeStruct((NDEV, CHUNK, CHUNK), F32), jax.ShapeDtypeStruct((NDEV, CHUNK, 1), F32)],
        in_specs=[pl.BlockSpec((TS, 7 * E_A), lambda i: (i, 0)), _halo_before(E_A, 0), _halo_before(E_A, 2),
                  _halo_after(E_A, 1), _halo_after(E_A, 3),
                  pl.BlockSpec((TS, 2 * E_A), lambda i: (i, 0)), _halo_after(E_A, 0),
                  pl.BlockSpec((None, 3, E_A), const3), pl.BlockSpec((None, 1, E_A), const3),
                  pl.BlockSpec((None, 1, E_A), const3), pl.BlockSpec((None, NDEV, CHUNK, CHUNK), const4),
                  pl.BlockSpec((None, NDEV, CHUNK, 1), const4)],
        out_specs=[pl.BlockSpec((TS, 7 * E_A), lambda i: (i, 0)), pl.BlockSpec((3, E_A), fixed2),
                   pl.BlockSpec((1, E_A), fixed2), pl.BlockSpec((1, E_A), fixed2),
                   pl.BlockSpec((NDEV, CHUNK, CHUNK), fixed3), pl.BlockSpec((NDEV, CHUNK, 1), fixed3)],
        compiler_params=_params(("arbitrary",)),
    )(proj, proj, proj, proj, proj, d_ycat, d_ycat, convw, ln_g3, ln_b3, sgu_w, sgu_bcol)


def _window_count(step, n, win, ext_before):
    rows = CHUNK if ext_before else CHUNK + HALO
    t = step * TS + n * CHUNK + lax.broadcasted_iota(jnp.int32, (rows, 1), 0)
    return jnp.minimum(t + 1, win).astype(F32)


def _pool_weight(wp_ref, g):
    return jnp.concatenate([wp_ref[d, g] for d in range(NDEV)], axis=0)


def _pooled_chunk(p, tail, win, count):
    sums = jnp.concatenate([tail, p], axis=0)
    shift = 1
    while shift < win:
        sums = sums + pltpu.roll(sums, shift, 0)
        shift *= 2
    return sums[HALO:] / count - p


def _pool_mix_fwd(proj, wpool, pscale4, wl, after, name):
    e_c = 4 * G_C

    def body(pj_ref, hp_ref, wp_ref, ps_ref, *rest):
        y_ref, pooled_scr, yraw_scr = rest[-3:]
        step = pl.program_id(0)
        live = (step > 0).astype(F32)
        for g, win in enumerate(POOL_WINDOWS):
            for q in range(G_C // LANE):
                cols = slice(g * G_C + q * LANE, g * G_C + (q + 1) * LANE)
                tail = hp_ref[:, cols].astype(F32) * live
                for n in range(NCH):
                    rows = slice(n * CHUNK, (n + 1) * CHUNK)
                    p = pj_ref[rows, cols].astype(F32)
                    pooled_scr[rows, q * LANE:(q + 1) * LANE] = _pooled_chunk(
                        p, tail, win, _window_count(step, n, win, True)).astype(BF16)
                    tail = p[CHUNK - HALO:]
            yraw_scr[...] = jnp.dot(pooled_scr[...], _pool_weight(wp_ref, g), preferred_element_type=F32)
            for q in range(G_C // LANE):
                cols = slice(g * G_C + q * LANE, g * G_C + (q + 1) * LANE)
                for n in range(NCH):
                    rows = slice(n * CHUNK, (n + 1) * CHUNK)
                    z = pj_ref[rows, e_c + cols.start:e_c + cols.stop].astype(F32)
                    y_ref[rows, cols] = (yraw_scr[rows, q * LANE:(q + 1) * LANE] * ps_ref[:, cols] * _silu(z)).astype(BF16)

    return pl.pallas_call(
        body, name=name, grid=(S // TS,), out_shape=jax.ShapeDtypeStruct((S, e_c), BF16),
        in_specs=[pl.BlockSpec((TS, 2 * e_c), lambda i: (i, 0)), _halo_before(e_c, 0),
                  pl.BlockSpec((NDEV, 4, G_C // NDEV, G_C), lambda i: (0, 0, 0, 0)),
                  pl.BlockSpec((None, 1, e_c), lambda i: (wl, 0, 0))] + [ANY] * len(after),
        out_specs=pl.BlockSpec((TS, e_c), lambda i: (i, 0)),
        scratch_shapes=[pltpu.VMEM((TS, G_C), BF16), pltpu.VMEM((TS, G_C), F32)],
        compiler_params=_params(("arbitrary",)),
    )(proj, proj, wpool, pscale4, *after)


def _pool_mix_bwd(proj, d_ycat, wpool, pscale4, wl, name):
    e_c = 4 * G_C
    nsteps = S // TS
    rb = G_C // NDEV

    def body(pj_ref, hp_ref, hz_ref, dy_ref, hdy_ref, wp_ref, ps_ref,
             dp_ref, dps_ref, dwp_ref, pooled_scr, yraw_scr, dyraw_scr, dpool_scr, acc_w):
        step = pl.program_id(0)

        @pl.when(step == 0)
        def _():
            dps_ref[...] = jnp.zeros_like(dps_ref)
            acc_w[...] = jnp.zeros_like(acc_w)

        live_before = (step > 0).astype(F32)
        live_after = (step < nsteps - 1).astype(F32)
        for g, win in enumerate(POOL_WINDOWS):
            weight = _pool_weight(wp_ref, g)
            for q in range(G_C // LANE):
                cols = slice(g * G_C + q * LANE, g * G_C + (q + 1) * LANE)
                tail = hp_ref[:, cols].astype(F32) * live_before
                for n in range(NCH):
                    rows = slice(n * CHUNK, (n + 1) * CHUNK)
                    p = pj_ref[rows, cols].astype(F32)
                    pooled_scr[rows, q * LANE:(q + 1) * LANE] = _pooled_chunk(
                        p, tail, win, _window_count(step, n, win, True)).astype(BF16)
                    tail = p[CHUNK - HALO:]
            yraw_scr[...] = jnp.dot(pooled_scr[...], weight, preferred_element_type=F32)
            for q in range(G_C // LANE):
                cols = slice(g * G_C + q * LANE, g * G_C + (q + 1) * LANE)
                local = slice(q * LANE, (q + 1) * LANE)
                scale = ps_ref[:, cols]
                acc_ps = jnp.zeros((1, LANE), F32)
                for n in range(NCH):
                    rows = slice(n * CHUNK, (n + 1) * CHUNK)
                    sz, dsz = _silu_and_grad(pj_ref[rows, e_c + cols.start:e_c + cols.stop].astype(F32))
                    dyv = dy_ref[rows, cols].astype(F32)
                    yraw = yraw_scr[rows, local]
                    dyraw_scr[rows, local] = (dyv * scale * sz).astype(BF16)
                    acc_ps += jnp.sum(dyv * yraw * sz, axis=0, keepdims=True)
                    dp_ref[rows, e_c + cols.start:e_c + cols.stop] = (dyv * yraw * scale * dsz).astype(BF16)
                dps_ref[:, cols] += acc_ps
                dyraw_scr[TS:, local] = (hdy_ref[:, cols].astype(F32) * scale * _silu(hz_ref[:, cols].astype(F32))
                                         * live_after).astype(BF16)
            dpool_scr[...] = lax.dot_general(dyraw_scr[...], weight, (((1,), (1,)), ((), ())), preferred_element_type=F32)
            acc_w[g] += lax.dot_general(pooled_scr[...], dyraw_scr[:TS, :], (((0,), (0,)), ((), ())),
                                        preferred_element_type=F32)
            for q in range(G_C // LANE):
                cols = slice(g * G_C + q * LANE, g * G_C + (q + 1) * LANE)
                local = slice(q * LANE, (q + 1) * LANE)
                for n in range(NCH):
                    rows = slice(n * CHUNK, (n + 1) * CHUNK)
                    ext = dpool_scr[n * CHUNK:(n + 1) * CHUNK + HALO, local]
                    sums = ext / _window_count(step, n, win, False)
                    shift = 1
                    while shift < win:
                        sums = sums + pltpu.roll(sums, CHUNK + HALO - shift, 0)
                        shift *= 2
                    dp_ref[rows, cols] = (sums[:CHUNK] - ext[:CHUNK]).astype(BF16)

        @pl.when(step == nsteps - 1)
        def _():
            for g in range(4):
                for d in range(NDEV):
                    dwp_ref[d % 2, d // 2, g] = acc_w[g, d * rb:(d + 1) * rb, :].astype(BF16)

    in_specs = [pl.BlockSpec((TS, 2 * e_c), lambda i: (i, 0)), _halo_before(e_c, 0), _halo_after(e_c, 1),
                pl.BlockSpec((TS, e_c), lambda i: (i, 0)), _halo_after(e_c, 0),
                pl.BlockSpec((NDEV, 4, rb, G_C), lambda i: (0, 0, 0, 0)),
                pl.BlockSpec((None, 1, e_c), lambda i: (wl, 0, 0))]
    args = [proj, proj, proj, d_ycat, d_ycat, wpool, pscale4]
    return pl.pallas_call(
        body, name=name, grid=(nsteps,),
        out_shape=[jax.ShapeDtypeStruct((S, 2 * e_c), BF16), jax.ShapeDtypeStruct((1, e_c), F32),
                   jax.ShapeDtypeStruct((2, NDEV // 2) + wpool.shape[1:], BF16)],
        in_specs=in_specs,
        out_specs=[pl.BlockSpec((TS, 2 * e_c), lambda i: (i, 0)), pl.BlockSpec((1, e_c), lambda i: (0, 0)),
                   pl.BlockSpec((2, NDEV // 2, 4, rb, G_C), lambda i: (0, 0, 0, 0, 0))],
        scratch_shapes=[pltpu.VMEM((TS, G_C), BF16), pltpu.VMEM((TS, G_C), F32), pltpu.VMEM((TS + HALO, G_C), BF16),
                        pltpu.VMEM((TS + HALO, G_C), F32), pltpu.VMEM((4, G_C, G_C), F32)],
        compiler_params=_params(("arbitrary",)),
    )(*args)


def _adamw(w, g, m, v):
    m = ADAM_B1 * m + (1.0 - ADAM_B1) * g
    v = ADAM_B2 * v + (1.0 - ADAM_B2) * jnp.square(g)
    m_hat = m / (1.0 - ADAM_B1 ** ADAM_STEP)
    v_hat = v / (1.0 - ADAM_B2 ** ADAM_STEP)
    delta = -ADAM_LR * (m_hat / (jnp.sqrt(v_hat) + ADAM_EPS) + ADAM_WD * w)
    return delta, m, v


def _adam_sharded(w, m, v, chip_parts, landed, my_chip, carried, name, first=0, into=()):
    _, nr, ncol = w.shape
    nl = len(chip_parts)
    tr = 128
    steps = nr // tr
    nc, ni = len(carried), len(into)

    def body(chip_ref, w_ref, m_ref, v_ref, *rest):
        parts, zones = rest[:nl], rest[nl:2 * nl]
        g_ref, d_ref, nm_ref, nv_ref = rest[2 * nl + nc + ni:2 * nl + nc + ni + 4]
        layer = pl.program_id(0)
        g = jnp.zeros((tr, ncol), F32)
        for l in range(nl):
            gl = parts[l][...].astype(F32)
            for q in range(3):
                gl = gl + zones[l][q].astype(F32)
            g = jnp.where(layer == l, gl, g)
        g_ref[...] = g
        d_ref[...], nm_ref[...], nv_ref[...] = _adamw(w_ref[...], g, m_ref[...], v_ref[...])

    def rows_of(l):
        return lambda layer, i, chip_ref: jnp.where(layer == l, i, jnp.where(layer < l, 0, steps - 1))

    spec = pl.BlockSpec((None, tr, ncol), lambda layer, i, chip_ref: (first + layer, i, 0))
    part_specs = [pl.BlockSpec((None, tr, ncol), lambda layer, i, chip_ref, l=l: (chip_ref[0], rows_of(l)(layer, i, chip_ref), 0))
                  for l in range(nl)]
    zone_specs = [pl.BlockSpec((3, tr, ncol), lambda layer, i, chip_ref, l=l: (0, rows_of(l)(layer, i, chip_ref), 0))
                  for l in range(nl)]
    grid_spec = pltpu.PrefetchScalarGridSpec(
        num_scalar_prefetch=1, grid=(nl, steps),
        in_specs=[spec, spec, spec] + part_specs + zone_specs + [ANY] * (nc + ni), out_specs=[spec] * 4 + [ANY] * nc)
    aliases = {4 + 2 * nl + k: 4 + k for k in range(nc)}
    aliases.update({4 + 2 * nl + nc + k: k for k in range(ni)})
    return pl.pallas_call(
        body, name=name, grid_spec=grid_spec,
        out_shape=[jax.ShapeDtypeStruct(w.shape, F32)] * 4 + [jax.ShapeDtypeStruct(a.shape, a.dtype) for a in carried],
        input_output_aliases=aliases, compiler_params=_params(("arbitrary", "arbitrary")),
    )(my_chip, w, m, v, *chip_parts, *landed, *carried, *into)


def _adam_small(params, name):
    n = len(params)

    def body(*refs):
        ins, outs = refs[:4 * n], refs[4 * n:]
        for k in range(n):
            w_ref, g_ref, m_ref, v_ref = ins[4 * k:4 * k + 4]
            outs[3 * k][...], outs[3 * k + 1][...], outs[3 * k + 2][...] = _adamw(w_ref[...], g_ref[...], m_ref[...], v_ref[...])

    outs = pl.pallas_call(body, name=name, out_shape=[jax.ShapeDtypeStruct(p[0].shape, F32) for p in params for _ in range(3)],
                          in_specs=[VMEM_FULL] * (4 * n), out_specs=[VMEM_FULL] * (3 * n),
                          compiler_params=_params())(*[a for p in params for a in p])
    return [list(outs[3 * k:3 * k + 3]) for k in range(n)]


def _sum_devices(gathered, name):
    _, nr, ncol = gathered.shape

    def body(g_ref, o_ref):
        acc = g_ref[0].astype(F32)
        for s in range(1, NDEV):
            acc = acc + g_ref[s].astype(F32)
        o_ref[...] = acc

    return pl.pallas_call(body, name=name, grid=(1,), out_shape=jax.ShapeDtypeStruct((nr, ncol), F32),
                          in_specs=[pl.BlockSpec((NDEV, nr, ncol), lambda i: (0, 0, 0))],
                          out_specs=pl.BlockSpec((nr, ncol), lambda i: (0, 0)),
                          compiler_params=_params(("arbitrary",)))(gathered)


def _ada_weight_adam(cact_t, dmod_mine, w, m, v):
    def body(ct_ref, dm_ref, w_ref, m_ref, v_ref, g_ref, d_ref, nm_ref, nv_ref):
        ct, dm = ct_ref[...], dm_ref[...]
        g = ct[:, 0:1] * dm[0:1, :]
        for e in range(1, NDEV):
            g = g + ct[:, e:e + 1] * dm[e:e + 1, :]
        g_ref[...] = g
        d_ref[...], nm_ref[...], nv_ref[...] = _adamw(w_ref[...], g, m_ref[...], v_ref[...])

    spec = pl.BlockSpec((None, D, ADA_NC), lambda l: (l, 0, 0))
    return pl.pallas_call(
        body, name="ada_weight_adam", grid=(DEPTH,), out_shape=[jax.ShapeDtypeStruct(w.shape, F32)] * 4,
        in_specs=[pl.BlockSpec((D, NDEV), lambda l: (0, 0)), pl.BlockSpec((None, NDEV, ADA_NC), lambda l: (l, 0, 0)),
                  spec, spec, spec],
        out_specs=[spec] * 4, compiler_params=_params(("arbitrary",)),
    )(cact_t, dmod_mine, w, m, v)


def _pad_rows(a, rows):
    a = a.reshape(-1, D)
    return jnp.pad(a, ((0, rows - a.shape[0]), (0, 0)))


def kernel(x, c, norm_g, ada_w, ada_b, ab_w_in, ab_conv_w, ab_ln_g, ab_ln_b, ab_sgu_w, ab_sgu_b, ab_w_out, c_w_in, c_pool_w, c_pool_scale, c_w_out, final_g, loss_target, m_norm_g, m_ada_w, m_ada_b, m_ab_w_in, m_ab_conv_w, m_ab_ln_g, m_ab_ln_b, m_ab_sgu_w, m_ab_sgu_b, m_ab_w_out, m_c_w_in, m_c_pool_w, m_c_pool_scale, m_c_w_out, m_final_g, v_norm_g, v_ada_w, v_ada_b, v_ab_w_in, v_ab_conv_w, v_ab_ln_g, v_ab_ln_b, v_ab_sgu_w, v_ab_sgu_b, v_ab_w_out, v_c_w_in, v_c_pool_w, v_c_pool_scale, v_c_w_out, v_final_g):
    x_pos, y_pos, c_pos = _position()
    me = _index((x_pos, y_pos, c_pos))
    core = c_pos.astype(jnp.int32).reshape(1)
    my_chip = (2 * x_pos + y_pos).astype(jnp.int32).reshape(1)
    me1 = me.astype(jnp.int32).reshape(1)
    x0 = x.reshape(S, D)
    target = loss_target.reshape(S, D)
    norm_g3 = norm_g.reshape(DEPTH, 1, D)
    ln_g3, ln_b3 = ab_ln_g.reshape(2, 1, E_A), ab_ln_b.reshape(2, 1, E_A)
    sgu_bcol = ab_sgu_b.reshape(2, NDEV, CHUNK, 1)
    rb = G_C // NDEV
    pool_w3, m_pool_w3, v_pool_w3 = (a.reshape(2, 4 * rb, G_C) for a in (c_pool_w, m_c_pool_w, v_c_pool_w))

    cact_all, mod = _ada_forward(c, ada_w, ada_b)
    convw_all, pscale_all = _gather([ab_conv_w, c_pool_scale], "gather_small_weights")
    convw = jnp.transpose(convw_all, (1, 2, 0, 3)).reshape(2, 3, E_A)
    pscale4 = jnp.transpose(pscale_all, (1, 0, 2)).reshape(2, 1, 4 * G_C)
    zones = []
    for layer in range(DEPTH):
        wl = layer // 2
        if layer % 2 == 0:
            zones.append([_to_zone(ab_w_in, wl, me1, BF16, f"cast_w_in_{layer}"), _to_zone(ab_w_out, wl, me1, BF16, f"cast_w_out_{layer}")])
        else:
            zones.append([_to_zone(c_w_in, wl, me1, BF16, f"cast_w_in_{layer}"), _to_zone(c_w_out, wl, me1, BF16, f"cast_w_out_{layer}"),
                          _to_zone(pool_w3, wl, me1, BF16, f"cast_pool_w_{layer}")])

    def gathered(flight, after, layer):
        wg = _gather_end(flight, [after], f"gather_end_{layer}")
        return [wg[0], wg[1].reshape(-1, D)] + [w.reshape(NDEV, 4, rb, G_C) for w in wg[2:]]

    flight, (mod,) = _gather_start(zones[0], [convw_all], [mod], "gather_start_0")
    flight, (mod,) = _gather_mid(flight, [z for zs in zones[1:] for z in zs], [mod], "gather_mid_0")
    next_flight, (mod,) = _gather_start(zones[1], [], [mod], "gather_start_1")
    flight = _gather_late(flight, [mod], "gather_late_0")
    xs, hts, projs, ycats, outs, gathered_w = [x0], [], [], [], [], [gathered(flight, mod, 0)]
    for layer in range(DEPTH):
        wl = layer // 2
        even = layer % 2 == 0
        wg = gathered_w[layer]
        h_t, proj = _norm_proj(xs[-1], mod, norm_g3, wg[0], layer, f"norm_proj_{layer}")
        if layer + 1 < DEPTH:
            flight, (h_t,) = _gather_mid(next_flight, [], [h_t], f"gather_mid_{layer + 1}")
            if layer + 2 < DEPTH:
                next_flight, (h_t,) = _gather_start(zones[layer + 2], [], [h_t], f"gather_start_{layer + 2}")
        if even:
            ycat = _even_mix_fwd(proj, convw, ln_g3, ln_b3, ab_sgu_w, sgu_bcol, wl, [h_t], f"even_mix_fwd_{layer}")
        else:
            ycat = _pool_mix_fwd(proj, wg[2], pscale4, wl, [h_t], f"pool_mix_fwd_{layer}")
        if layer + 1 < DEPTH:
            flight = _gather_late(flight, [ycat], f"gather_late_{layer + 1}")
        if layer + 1 < DEPTH:
            x_new, out = _out_proj(ycat, wg[1], xs[-1], mod, layer, f"out_proj_{layer}")
            gathered_w.append(gathered(flight, x_new, layer + 1))
            xs.append(x_new)
        else:
            dx, out, loss_part, d_final_g = _out_proj(ycat, wg[1], xs[-1], mod, layer, f"out_proj_{layer}",
                                                      final=(target, final_g.reshape(1, D)))
        hts.append(h_t)
        projs.append(proj)
        ycats.append(ycat)
        outs.append(out)

    d_mod, d_norm_g = [None] * DEPTH, [None] * DEPTH
    small, scatters, landed, res = {}, {}, {}, {}

    def finish_scatter(layer, after):
        send_sems, recv_sems, chip_parts, zones = scatters[layer]
        landed[layer] = _scatter_end(chip_parts, zones, send_sems, recv_sems, after, f"scatter_end_{layer}")

    def flat(a):
        return a.reshape(a.shape[0], -1, a.shape[-1])

    def sharded_adam(k, j, layers, w, m, v, carried, first=0, into=()):
        outs4 = _adam_sharded(w, m, v, [flat(landed[l][0][j]) for l in layers], [flat(landed[l][1][j]) for l in layers],
                              my_chip, carried, f"adam_{k}_{first}" if len(layers) < w.shape[0] else "adam_" + k, first, into)
        res[k] = [o.reshape(c_pool_w.shape) if k == "c_pool_w" else o for o in outs4[:4]]
        return list(outs4[4:])

    previous = None
    for layer in reversed(range(DEPTH)):
        wl = layer // 2
        even = layer % 2 == 0
        wg = gathered_w[layer]
        carried = [] if previous is None else [scatters[previous][2][0]]
        d_ycat, grad_out, d_gate, carried = _out_bwd(dx, outs[layer], ycats[layer], wg[1], mod, layer, carried, f"out_bwd_{layer}")
        if previous is not None:
            scatters[previous][2][0] = carried[0]
        parts = [None, grad_out]
        if even:
            d_proj, d_cw, d_lg, d_lb, d_sw, d_sb = _even_mix_bwd(
                projs[layer], d_ycat, convw, ln_g3, ln_b3, ab_sgu_w, sgu_bcol, wl, f"even_mix_bwd_{layer}")
            small[layer] = (d_cw, d_lg, d_lb, d_sw, d_sb)
        else:
            d_proj, d_ps, d_pool = _pool_mix_bwd(projs[layer], d_ycat, wg[2], pscale4, wl, f"pool_mix_bwd_{layer}")
            small[layer] = (d_ps,)
            parts.append(d_pool)
        parts[0] = _weight_grad(hts[layer], d_proj, f"grad_w_in_{layer}")
        pair_send, pair_recv, parts, from_sibling = _pair_start(parts, f"pair_start_{layer}")
        if layer > 0:
            dx, d_shift, d_scale, d_norm_g[layer], parts[0] = _dh_norm_bwd(
                d_proj, wg[0], xs[layer], dx, mod, norm_g3, layer, parts[0], f"dh_norm_bwd_{layer}")
            pair_after = dx
        else:
            finish_scatter(1, d_proj)
            finish_scatter(3, d_proj)
            parts[0], = sharded_adam("c_w_out", 1, (1, 3), c_w_out, m_c_w_out, v_c_w_out, [parts[0]])
            parts[0], = sharded_adam("c_pool_w", 2, (1, 3), pool_w3, m_pool_w3, v_pool_w3, [parts[0]])
            pair_after = res["c_pool_w"][0]
        parts, from_sibling = _pair_end(parts, from_sibling, pair_send, pair_recv, pair_after, f"pair_end_{layer}")
        chip_parts = [_pair_sum(p, q, core, f"pair_sum_{layer}_{j}") for j, (p, q) in enumerate(zip(parts, from_sibling))]
        send_sems, recv_sems, chip_parts, zones = _scatter_start(chip_parts, f"scatter_start_{layer}")
        if layer == 0:
            chip_parts[0], = sharded_adam("c_w_in", 0, (1, 3), c_w_in, m_c_w_in, v_c_w_in, [chip_parts[0]])
            dx, d_shift, d_scale, d_norm_g[layer], chip_parts[0] = _dh_norm_bwd(
                d_proj, wg[0], xs[layer], dx, mod, norm_g3, layer, chip_parts[0], f"dh_norm_bwd_{layer}")
        scatters[layer] = [send_sems, recv_sems, chip_parts, zones]
        previous = layer
        d_mod[layer] = jnp.concatenate([d_shift, d_scale, d_gate], axis=0)
    grad_x = dx.reshape(x.shape)

    sections = [("norm_g", jnp.concatenate(d_norm_g, axis=0), 8),
                ("d_mod", jnp.concatenate(d_mod, axis=0), 16),
                ("ab_ln_g", jnp.concatenate([small[0][1], small[2][1]], axis=0), 8),
                ("ab_ln_b", jnp.concatenate([small[0][2], small[2][2]], axis=0), 8),
                ("ab_sgu_b", jnp.stack([small[0][4], small[2][4]]), 8),
                ("final_g", d_final_g, 8),
                ("ab_conv_w", jnp.stack([small[0][0], small[2][0]]), 8),
                ("c_pool_scale", jnp.concatenate([small[1][0], small[3][0]], axis=0), 8),
                ("ab_sgu_w", jnp.stack([small[0][3], small[2][3]]), 256)]
    offsets, at = {}, 0
    for name, _, rows in sections:
        offsets[name] = (at, rows)
        at += rows
    packed = jnp.concatenate([_pad_rows(a, rows) for _, a, rows in sections] + [jnp.zeros((-at % 32, D), F32)], axis=0)
    loss_rows = jnp.pad(loss_part, ((0, 15), (0, D - LANE)))
    small_zones = [_to_zone(packed[None], 0, me1, BF16, "place_small_grads"), _to_zone(loss_rows[None], 0, me1, F32, "place_loss")]
    small_flight, (mod,) = _gather_start(small_zones, [], [mod], "gather_small_start")

    finish_scatter(2, mod)
    sharded_adam("ab_w_out", 1, (2,), ab_w_out, m_ab_w_out, v_ab_w_out, [], first=1)
    sharded_adam("ab_w_in", 0, (2,), ab_w_in, m_ab_w_in, v_ab_w_in, [], first=1)
    finish_scatter(0, res["ab_w_in"][0])
    sharded_adam("ab_w_out", 1, (0,), ab_w_out, m_ab_w_out, v_ab_w_out, [], into=res["ab_w_out"])
    small_flight, (mod,) = _gather_mid(small_flight, [res["ab_w_out"][0]], [mod], "gather_small_mid")
    sharded_adam("ab_w_in", 0, (0,), ab_w_in, m_ab_w_in, v_ab_w_in, [mod], into=res["ab_w_in"])

    last = res["ab_w_in"][0]
    small_flight = _gather_late(small_flight, [last], "gather_small_late")
    small_grads, losses = _gather_end(small_flight, [last], "gather_small_end")
    summed = _sum_devices(small_grads, "sum_small_grads")
    loss = _sum_devices(losses, "sum_loss")[0, 0]

    def section(name, nrows, src=summed):
        start = offsets[name][0]
        return src[..., start:start + nrows, :]

    grads = {
        "norm_g": section("norm_g", DEPTH),
        "ada_b": section("d_mod", 3 * DEPTH).reshape(DEPTH, 3 * D),
        "ab_ln_g": section("ab_ln_g", 2), "ab_ln_b": section("ab_ln_b", 2),
        "ab_sgu_b": section("ab_sgu_b", 2).reshape(ab_sgu_b.shape),
        "final_g": section("final_g", 1),
        "ab_sgu_w": section("ab_sgu_w", 256).reshape(ab_sgu_w.shape),
        "ab_conv_w": lax.dynamic_slice_in_dim(section("ab_conv_w", 6).reshape(2, 3, E_A), me * HEAD, HEAD, axis=2),
        "c_pool_scale": lax.dynamic_slice_in_dim(section("c_pool_scale", 4).reshape(2, 4 * G_C), me * 256, 256, axis=1),
    }
    small_w = {"norm_g": (norm_g, m_norm_g, v_norm_g), "ada_b": (ada_b, m_ada_b, v_ada_b),
               "ab_ln_g": (ab_ln_g, m_ab_ln_g, v_ab_ln_g), "ab_ln_b": (ab_ln_b, m_ab_ln_b, v_ab_ln_b),
               "ab_sgu_b": (ab_sgu_b, m_ab_sgu_b, v_ab_sgu_b),
               "final_g": (final_g.reshape(1, D), m_final_g.reshape(1, D), v_final_g.reshape(1, D)),
               "ab_sgu_w": (ab_sgu_w, m_ab_sgu_w, v_ab_sgu_w), "ab_conv_w": (ab_conv_w, m_ab_conv_w, v_ab_conv_w),
               "c_pool_scale": (c_pool_scale, m_c_pool_scale, v_c_pool_scale)}
    updates = _adam_small([(w, grads[k], m, v) for k, (w, m, v) in small_w.items()], "adam_small")
    for k, update in zip(small_w, updates):
        res[k] = [grads[k]] + update
    res["final_g"] = [a.reshape(D) for a in res["final_g"]]

    dmod_all = section("d_mod", 3 * DEPTH, small_grads).reshape(NDEV, DEPTH, 3 * D)
    dmod_mine = jnp.transpose(lax.dynamic_slice_in_dim(dmod_all, me * ADA_NC, ADA_NC, axis=2), (1, 0, 2)).astype(F32)
    res["ada_w"] = _ada_weight_adam(jnp.transpose(cact_all.reshape(NDEV, D)), dmod_mine, ada_w, m_ada_w, v_ada_w)

    order = ["norm_g", "ada_w", "ada_b", "ab_w_in", "ab_conv_w", "ab_ln_g", "ab_ln_b", "ab_sgu_w", "ab_sgu_b",
             "ab_w_out", "c_w_in", "c_pool_w", "c_pool_scale", "c_w_out", "final_g"]
    return (loss, grad_x, *[res[k][0] for k in order], *[res[k][1] for k in order],
            *[res[k][2] for k in order], *[res[k][3] for k in order])
```

```python
import jax
import jax.numpy as jnp
from jax import lax
from jax.experimental import pallas as pl
from jax.experimental.pallas import tpu as pltpu

F32, BF16 = jnp.float32, jnp.bfloat16
S, D = 2048, 1024
NDEV = 8
DEPTH = 4
EPS = 1e-6
E_A = 1024
HEAD = 128
CHUNK = 128
POOL_WINDOWS = (2, 4, 8, 16)
G_C = 512
HALO = 16
ADA_NC = 384
MIB = 1024 * 1024
LANE = 128

ADAM_LR, ADAM_B1, ADAM_B2, ADAM_EPS, ADAM_WD, ADAM_STEP = 0.001, 0.9, 0.999, 1e-08, 0.01, 10

ANY = pl.BlockSpec(memory_space=pl.ANY)
VMEM_FULL = pl.BlockSpec(memory_space=pltpu.VMEM)
IN_HBM = pl.BlockSpec(memory_space=pltpu.HBM)
SEMAPHORES = pl.BlockSpec(memory_space=pltpu.SEMAPHORE)
IN_FLIGHT = pltpu.SideEffectType.DATAFLOW_SIDE_EFFECTING


V7X_VMEM_MIB = 64
VMEM_LIMIT_MIB = V7X_VMEM_MIB - 4


def _params(semantics=None):
    return pltpu.CompilerParams(dimension_semantics=semantics, vmem_limit_bytes=VMEM_LIMIT_MIB * MIB)


def _silu(z):
    return z * jax.nn.sigmoid(z)


def _silu_and_grad(z):
    sig = jax.nn.sigmoid(z)
    return z * sig, sig * (1.0 + z * (1.0 - sig))


def _position():
    return lax.axis_index("x"), lax.axis_index("y"), lax.axis_index("c")


def _index(pos):
    return 4 * pos[0] + 2 * pos[1] + pos[2]


def _peer(pos, k):
    flipped = tuple(1 - p if (k >> (2 - b)) & 1 else p for b, p in enumerate(pos))
    return flipped, _index(flipped)


def _remote(src, dst, send_sem, recv_sem, device):
    return pltpu.make_async_remote_copy(src_ref=src, dst_ref=dst, send_sem=send_sem, recv_sem=recv_sem,
                                        device_id=device, device_id_type=pl.DeviceIdType.MESH)


def _pair_start(parts, name):
    n = len(parts)
    lands = [_in_hbm(lax.empty(p.shape[1:], p.dtype)) for p in parts]

    def body(*refs):
        ins, zones = refs[:n], refs[n:2 * n]
        send_sems, recv_sems = refs[2 * n:2 * n + 2]
        x, y, c = _position()
        for j in range(n):
            _remote(ins[j].at[1 - c], zones[j], send_sems.at[j], recv_sems.at[j], (x, y, 1 - c)).start()

    outs = pl.pallas_call(
        body, name=name,
        out_shape=(pltpu.SemaphoreType.DMA((n,)), pltpu.SemaphoreType.DMA((n,)),
                   *[pltpu.HBM(p.shape, p.dtype) for p in parts], *[pltpu.HBM(z.shape, z.dtype) for z in lands]),
        in_specs=[IN_HBM] * (2 * n), out_specs=(SEMAPHORES, SEMAPHORES, *[IN_HBM] * (2 * n)),
        input_output_aliases={j: 2 + j for j in range(2 * n)},
        compiler_params=pltpu.CompilerParams(has_side_effects=IN_FLIGHT),
    )(*[_in_hbm(p) for p in parts], *lands)
    return outs[0], outs[1], list(outs[2:2 + n]), list(outs[2 + n:])


def _pair_end(parts, zones, send_sems, recv_sems, after, name):
    n = len(parts)

    def body(*refs):
        ins, zs = refs[:n], refs[n:2 * n]
        s, r = refs[2 * n:2 * n + 2]
        me = _position()
        for j in range(n):
            copy = _remote(ins[j].at[0], zs[j], s.at[j], r.at[j], me)
            copy.wait_send()
            copy.wait_recv()

    outs = pl.pallas_call(
        body, name=name,
        out_shape=(*[pltpu.HBM(p.shape, p.dtype) for p in parts], *[pltpu.HBM(z.shape, z.dtype) for z in zones]),
        in_specs=[IN_HBM] * (2 * n) + [SEMAPHORES, SEMAPHORES, ANY], out_specs=tuple([IN_HBM] * (2 * n)),
        input_output_aliases={j: j for j in range(2 * n)},
        compiler_params=pltpu.CompilerParams(has_side_effects=IN_FLIGHT),
    )(*parts, *zones, send_sems, recv_sems, after)
    return list(outs[:n]), list(outs[n:])


def _pair_sum(part, from_sibling, core, name):
    ncol = part.shape[-1]
    p3 = part.reshape(2, -1, ncol)
    q2 = from_sibling.reshape(-1, ncol)
    nrows = q2.shape[0]
    tr = 512

    def body(core_ref, p_ref, q_ref, o_ref):
        o_ref[...] = (p_ref[...].astype(F32) + q_ref[...].astype(F32)).astype(BF16)

    grid_spec = pltpu.PrefetchScalarGridSpec(
        num_scalar_prefetch=1, grid=(nrows // tr,),
        in_specs=[pl.BlockSpec((None, tr, ncol), lambda i, core_ref: (core_ref[0], i, 0)),
                  pl.BlockSpec((tr, ncol), lambda i, core_ref: (i, 0))],
        out_specs=pl.BlockSpec((tr, ncol), lambda i, core_ref: (i, 0)))
    out = pl.pallas_call(body, name=name, grid_spec=grid_spec, out_shape=jax.ShapeDtypeStruct(q2.shape, BF16),
                         compiler_params=_params(("arbitrary",)))(core, p3, q2)
    return out.reshape(from_sibling.shape)


def _in_hbm(a):
    return pltpu.with_memory_space_constraint(a, pltpu.HBM)


def _chips(x, y):
    return [(1 - x, y), (x, 1 - y), (1 - x, 1 - y)]


def _to_zone(a, wl, me, dtype, name):
    _, rows, cols = a.shape
    tr = 256 if rows % 256 == 0 else rows

    def body(me_ref, a_ref, o_ref):
        o_ref[...] = a_ref[...].astype(dtype)

    grid_spec = pltpu.PrefetchScalarGridSpec(
        num_scalar_prefetch=1, grid=(rows // tr,),
        in_specs=[pl.BlockSpec((None, tr, cols), lambda i, me_ref: (wl, i, 0))],
        out_specs=pl.BlockSpec((None, tr, cols), lambda i, me_ref: (me_ref[0], i, 0)))
    return pl.pallas_call(body, name=name, grid_spec=grid_spec, out_shape=jax.ShapeDtypeStruct((NDEV, rows, cols), dtype),
                          compiler_params=_params(("arbitrary",)))(me, a)


def _halves(block):
    rows = block.shape[0] // 2
    return block.at[pl.ds(0, rows)], block.at[pl.ds(rows, rows)]


def _around(x, y, c):
    return (x, y, 1 - c), (1 - x, y, c), (x, 1 - y, c), (1 - x, 1 - y, c)


def _gather_step1(zs, send, recv, pos):
    sibling, xn, yn, _ = _around(*pos)
    for j, z in enumerate(zs):
        mine = z.at[_index(pos)]
        for k, peer in enumerate((sibling, xn, yn)):
            _remote(mine, mine, send.at[3 * j + k], recv.at[3 * j + k], peer).start()


def _gather_step2(zs, recv1, send, recv, pos):
    sibling, xn, yn, _ = _around(*pos)
    for j, z in enumerate(zs):
        xb, yb = z.at[_index(xn)], z.at[_index(yn)]
        _remote(xb, xb, send.at[4 * j], recv1.at[3 * j + 1], pos).wait_recv()
        _remote(yb, yb, send.at[4 * j], recv1.at[3 * j + 2], pos).wait_recv()
        _remote(xb, xb, send.at[4 * j], recv.at[4 * j], sibling).start()
        _remote(yb, yb, send.at[4 * j + 1], recv.at[4 * j + 1], sibling).start()
        first, second = _halves(xb)[0], _halves(yb)[1]
        _remote(first, first, send.at[4 * j + 2], recv.at[4 * j + 2], yn).start()
        _remote(second, second, send.at[4 * j + 3], recv.at[4 * j + 3], xn).start()


def _gather_step3(zs, recv2, send, recv, pos):
    sibling, _, _, diagonal = _around(*pos)
    for j, z in enumerate(zs):
        db = z.at[_index(diagonal)]
        first, second = _halves(db)
        _remote(first, first, send.at[j], recv2.at[4 * j + 2], pos).wait_recv()
        _remote(second, second, send.at[j], recv2.at[4 * j + 3], pos).wait_recv()
        _remote(db, db, send.at[j], recv.at[j], sibling).start()


def _gather_step4(zs, send1, recv1, send2, recv2, send3, recv3, pos):
    x, y, c = pos
    sibling = (x, y, 1 - c)
    _, sx, sy, sd = _around(*sibling)
    for j, z in enumerate(zs):
        for owner, send, recv, k in ((sibling, send1, recv1, 3 * j), (sx, send2, recv2, 4 * j), (sy, send2, recv2, 4 * j + 1),
                                     (sd, send3, recv3, j)):
            block = z.at[_index(owner)]
            _remote(block, block, send.at[k], recv.at[k], pos).wait_recv()
    for j, z in enumerate(zs):
        block = z.at[0]
        half = _halves(block)[0]
        for ref, send, recv, k in ([(block, send1, recv1, 3 * j + k) for k in range(3)]
                                   + [(block, send2, recv2, 4 * j), (block, send2, recv2, 4 * j + 1),
                                      (half, send2, recv2, 4 * j + 2), (half, send2, recv2, 4 * j + 3), (block, send3, recv3, j)]):
            _remote(ref, ref, send.at[k], recv.at[k], pos).wait_send()


def _flight_call(step, name, zones, sems_in, nsems_out, after, carried):
    n, m, k = len(zones), len(carried), len(sems_in)

    def body(*refs):
        zs = refs[:n]
        given = refs[n + m:n + m + k]
        made = refs[n + m + k + len(after):n + m + k + len(after) + (2 if nsems_out else 0)]
        step(zs, *given, *made, _position())

    sem_out = (pltpu.SemaphoreType.DMA((nsems_out,)),) * 2 if nsems_out else ()
    outs = pl.pallas_call(
        body, name=name,
        out_shape=(*sem_out, *[pltpu.HBM(z.shape, z.dtype) for z in zones], *[jax.ShapeDtypeStruct(a.shape, a.dtype) for a in carried]),
        in_specs=[IN_HBM] * n + [ANY] * m + [SEMAPHORES] * k + [ANY] * len(after),
        out_specs=(*[SEMAPHORES] * len(sem_out), *[IN_HBM] * n, *[ANY] * m),
        input_output_aliases={j: len(sem_out) + j for j in range(n + m)},
        compiler_params=pltpu.CompilerParams(has_side_effects=IN_FLIGHT),
    )(*[_in_hbm(z) for z in zones], *carried, *sems_in, *after)
    sems = list(outs[:len(sem_out)])
    return sems, list(outs[len(sem_out):len(sem_out) + n]), list(outs[len(sem_out) + n:])


def _gather_start(zones, after, carried, name):
    (send1, recv1), zones, carried = _flight_call(_gather_step1, name, zones, [], 3 * len(zones), after, carried)
    return {"s1": send1, "r1": recv1, "zones": zones}, carried


def _gather_mid(flight, after, carried, name):
    step = lambda zs, recv1, send, recv, pos: _gather_step2(zs, recv1, send, recv, pos)
    (send2, recv2), zones, carried = _flight_call(step, name, flight["zones"], [flight["r1"]], 4 * len(flight["zones"]), after, carried)
    return {**flight, "s2": send2, "r2": recv2, "zones": zones}, carried


def _gather_late(flight, after, name):
    step = lambda zs, recv2, send, recv, pos: _gather_step3(zs, recv2, send, recv, pos)
    (send3, recv3), zones, _ = _flight_call(step, name, flight["zones"], [flight["r2"]], len(flight["zones"]), after, [])
    return {**flight, "s3": send3, "r3": recv3, "zones": zones}


def _gather_end(flight, after, name):
    sems = [flight[k] for k in ("s1", "r1", "s2", "r2", "s3", "r3")]
    _, zones, _ = _flight_call(_gather_step4, name, flight["zones"], sems, 0, after, [])
    return zones


def _scatter_start(parts, name):
    n = len(parts)
    lands = [_in_hbm(lax.empty((3,) + p.shape[1:], p.dtype)) for p in parts]

    def body(*refs):
        ins, zones = refs[:n], refs[n:2 * n]
        send_sems, recv_sems = refs[2 * n:2 * n + 2]
        x, y, c = _position()
        for j in range(n):
            for q, (px, py) in enumerate(_chips(x, y)):
                _remote(ins[j].at[2 * px + py], zones[j].at[q], send_sems.at[3 * j + q], recv_sems.at[3 * j + q],
                        (px, py, c)).start()

    outs = pl.pallas_call(
        body, name=name,
        out_shape=(pltpu.SemaphoreType.DMA((3 * n,)), pltpu.SemaphoreType.DMA((3 * n,)),
                   *[pltpu.HBM(p.shape, p.dtype) for p in parts], *[pltpu.HBM(z.shape, z.dtype) for z in lands]),
        in_specs=[IN_HBM] * (2 * n), out_specs=(SEMAPHORES, SEMAPHORES, *[IN_HBM] * (2 * n)),
        input_output_aliases={j: 2 + j for j in range(2 * n)},
        compiler_params=pltpu.CompilerParams(has_side_effects=IN_FLIGHT),
    )(*[_in_hbm(p) for p in parts], *lands)
    return outs[0], outs[1], list(outs[2:2 + n]), list(outs[2 + n:])


def _scatter_end(parts, zones, send_sems, recv_sems, after, name):
    n = len(parts)

    def body(*refs):
        ins, zs = refs[:n], refs[n:2 * n]
        s, r = refs[2 * n:2 * n + 2]
        me = _position()
        for j in range(n):
            for q in range(3):
                copy = _remote(ins[j].at[0], zs[j].at[q], s.at[3 * j + q], r.at[3 * j + q], me)
                copy.wait_send()
                copy.wait_recv()

    outs = pl.pallas_call(
        body, name=name,
        out_shape=(*[pltpu.HBM(p.shape, p.dtype) for p in parts], *[pltpu.HBM(z.shape, z.dtype) for z in zones]),
        in_specs=[IN_HBM] * (2 * n) + [SEMAPHORES, SEMAPHORES, ANY], out_specs=tuple([IN_HBM] * (2 * n)),
        input_output_aliases={j: j for j in range(2 * n)},
        compiler_params=pltpu.CompilerParams(has_side_effects=IN_FLIGHT),
    )(*parts, *zones, send_sems, recv_sems, after)
    return list(outs[:n]), list(outs[n:])


def _ada_forward(c, ada_w, ada_b, small):
    ns = len(small)

    def body(c_ref, w_ref, b_ref, *rest):
        small_refs, (cact_ref, mod_ref), gathered = rest[:ns], rest[ns:ns + 2], rest[ns + 2:2 * ns + 2]
        gbuf, modrow, send_sems, recv_sems = rest[2 * ns + 2:]
        pos = _position()
        me = _index(pos)

        def to_all(*exchanged):
            copies = []
            for ref, row in exchanged:
                for k in range(1, NDEV):
                    peer, _ = _peer(pos, k)
                    copy = pltpu.make_async_remote_copy(
                        src_ref=ref.at[me], dst_ref=ref.at[me], send_sem=send_sems.at[row, k - 1],
                        recv_sem=recv_sems.at[row, k - 1], device_id=peer, device_id_type=pl.DeviceIdType.MESH)
                    copy.start()
                    copies.append(copy)
            for copy in copies:
                copy.wait()

        cact_ref[me] = _silu(c_ref[...])
        for j in range(ns):
            gathered[j][me] = small_refs[j][...]
        to_all((cact_ref, 0), *[(gathered[j], 2 + j) for j in range(ns)])
        rows = lax.broadcasted_iota(jnp.int32, (NDEV, D), 0)
        cact = jnp.zeros((NDEV, D), F32)
        for e in range(NDEV):
            cact = jnp.where(rows == e, cact_ref[e], cact)
        cact = cact.astype(BF16)
        for l in range(DEPTH):
            gbuf[me, l] = jnp.dot(cact, w_ref[l].astype(BF16), preferred_element_type=F32)
        to_all((gbuf, 1))
        mine = lax.broadcasted_iota(jnp.int32, (NDEV, ADA_NC), 0) == me
        for l in range(DEPTH):
            for d in range(NDEV):
                modrow[:, d * ADA_NC:(d + 1) * ADA_NC] = jnp.sum(jnp.where(mine, gbuf[d, l], 0.0), axis=0, keepdims=True)
            full = modrow[...] + b_ref[l:l + 1, :]
            for w in range(3):
                mod_ref[l, w] = full[:, w * D:(w + 1) * D]

    outs = pl.pallas_call(
        body, name="ada_forward",
        out_shape=[jax.ShapeDtypeStruct((NDEV, 1, D), F32), jax.ShapeDtypeStruct((DEPTH, 3, 1, D), F32)]
        + [jax.ShapeDtypeStruct((NDEV,) + a.shape, a.dtype) for a in small],
        in_specs=[VMEM_FULL] * (3 + ns), out_specs=[VMEM_FULL] * (2 + ns),
        scratch_shapes=[pltpu.VMEM((NDEV, DEPTH, NDEV, ADA_NC), F32), pltpu.VMEM((1, 3 * D), F32),
                        pltpu.SemaphoreType.DMA((2 + ns, NDEV - 1)), pltpu.SemaphoreType.DMA((2 + ns, NDEV - 1))],
        compiler_params=_params(),
    )(c, ada_w, ada_b, *small)
    return outs[0], outs[1], list(outs[2:])


def _mod_spec(layer, which, ngrid):
    index = {1: lambda i: (layer, which, 0, 0), 2: lambda i, j: (layer, which, 0, 0)}[ngrid]
    return pl.BlockSpec((None, None, 1, D), index)


W_BLOCKS = 4


def _norm_proj(x, mod, norm_g3, wg, layer, name):
    nb = wg.shape[-1]
    tm = 1024
    wb = W_BLOCKS

    def body(x_ref, g_ref, shift_ref, scale_ref, w_ref, ht_ref, p_ref, h_ref):
        @pl.when(pl.program_id(1) == 0)
        def _():
            xv = x_ref[...]
            r = lax.rsqrt(jnp.mean(xv * xv, axis=-1, keepdims=True) + EPS)
            hn = xv * r * g_ref[...]
            h = hn * (1.0 + scale_ref[...]) + shift_ref[...]
            h_ref[...] = h.astype(BF16)
            ht_ref[...] = h.T.astype(BF16)

        hv = h_ref[...]
        for b in range(wb):
            p_ref[:, b * nb:(b + 1) * nb] = jnp.dot(hv, w_ref[b], preferred_element_type=F32).astype(BF16)

    return pl.pallas_call(
        body, name=name, grid=(S // tm, NDEV // wb),
        out_shape=[jax.ShapeDtypeStruct((D, S), BF16), jax.ShapeDtypeStruct((S, NDEV * nb), BF16)],
        in_specs=[pl.BlockSpec((tm, D), lambda i, d: (i, 0)),
                  pl.BlockSpec((None, 1, D), lambda i, d: (layer, 0, 0)),
                  _mod_spec(layer, 0, 2), _mod_spec(layer, 1, 2),
                  pl.BlockSpec((wb, D, nb), lambda i, d: (d, 0, 0))],
        out_specs=[pl.BlockSpec((D, tm), lambda i, d: (0, i)), pl.BlockSpec((tm, wb * nb), lambda i, d: (i, d))],
        scratch_shapes=[pltpu.VMEM((tm, D), BF16)],
        compiler_params=_params(("arbitrary", "arbitrary")),
    )(x, norm_g3, mod, mod, wg)


def _out_proj(ycat, w_out, x, mod, layer, name, final=None):
    tm = 512
    e = w_out.shape[0]

    def body(y_ref, w_ref, x_ref, gate_ref, *rest):
        acc = jnp.dot(y_ref[...], w_ref[...], preferred_element_type=F32)
        xv = x_ref[...] + gate_ref[...] * acc
        if final is None:
            xn_ref, o_ref = rest
            o_ref[...] = acc.astype(BF16)
            xn_ref[...] = xv
            return
        t_ref, g_ref, dx_ref, o_ref, loss_ref, dg_ref = rest
        o_ref[...] = acc.astype(BF16)

        @pl.when(pl.program_id(0) == 0)
        def _():
            loss_ref[...] = jnp.zeros_like(loss_ref)
            dg_ref[...] = jnp.zeros_like(dg_ref)

        g = g_ref[...]
        r = lax.rsqrt(jnp.mean(xv * xv, axis=-1, keepdims=True) + EPS)
        xn = xv * r
        err = xn * g - t_ref[...]
        loss_ref[...] += 0.5 * jnp.sum(jnp.mean(err * err, axis=-1, keepdims=True), axis=0, keepdims=True)
        dy = err * (1.0 / D)
        dg_ref[...] += jnp.sum(dy * xn, axis=0, keepdims=True)
        u = dy * g
        dx_ref[...] = r * (u - xn * jnp.mean(xn * u, axis=-1, keepdims=True))

    tile = pl.BlockSpec((tm, D), lambda i: (i, 0))
    row = pl.BlockSpec((1, D), lambda i: (0, 0))
    out_shape = [jax.ShapeDtypeStruct((S, D), F32), jax.ShapeDtypeStruct((S, D), BF16)]
    in_specs = [pl.BlockSpec((tm, e), lambda i: (i, 0)), pl.BlockSpec((e, D), lambda i: (0, 0)), tile, _mod_spec(layer, 2, 1)]
    out_specs, operands = [tile, tile], [ycat, w_out, x, mod]
    if final is not None:
        out_shape += [jax.ShapeDtypeStruct((1, LANE), F32), jax.ShapeDtypeStruct((1, D), F32)]
        in_specs += [tile, row]
        out_specs += [pl.BlockSpec((1, LANE), lambda i: (0, 0)), row]
        operands += list(final)
    return pl.pallas_call(
        body, name=name, grid=(S // tm,), out_shape=out_shape, in_specs=in_specs, out_specs=out_specs,
        compiler_params=_params(("arbitrary",)),
    )(*operands)


def _out_bwd(dx, out, ycat, w_out, mod, layer, carried, name):
    tm = 512
    nsteps = S // tm
    e = ycat.shape[1]
    rb = e // NDEV
    nc = len(carried)

    def body(dx_ref, o_ref, y_ref, w_ref, gate_ref, *rest):
        dy_ref, gw_ref, dgate_ref = rest[nc:nc + 3]
        acc = rest[-1]
        step = pl.program_id(0)

        @pl.when(step == 0)
        def _():
            dgate_ref[...] = jnp.zeros_like(dgate_ref)
            acc[...] = jnp.zeros_like(acc)

        dxv = dx_ref[...]
        d_out = (gate_ref[...] * dxv).astype(BF16)
        dgate_ref[...] += jnp.sum(dxv * o_ref[...].astype(F32), axis=0, keepdims=True)
        dy_ref[...] = lax.dot_general(d_out, w_ref[...], (((1,), (1,)), ((), ())), preferred_element_type=F32).astype(BF16)
        acc[...] += lax.dot_general(y_ref[...], d_out, (((0,), (0,)), ((), ())), preferred_element_type=F32)

        @pl.when(step == nsteps - 1)
        def _():
            for d in range(NDEV):
                gw_ref[d % 2, d // 2] = acc[d * rb:(d + 1) * rb, :].astype(BF16)

    tile = pl.BlockSpec((tm, D), lambda i: (i, 0))
    wide = pl.BlockSpec((tm, e), lambda i: (i, 0))
    outs = pl.pallas_call(
        body, name=name, grid=(nsteps,),
        out_shape=[jax.ShapeDtypeStruct((S, e), BF16), jax.ShapeDtypeStruct((2, NDEV // 2, rb, D), BF16),
                   jax.ShapeDtypeStruct((1, D), F32)] + [jax.ShapeDtypeStruct(a.shape, a.dtype) for a in carried],
        in_specs=[tile, tile, wide, pl.BlockSpec((e, D), lambda i: (0, 0)), _mod_spec(layer, 2, 1)] + [ANY] * nc,
        out_specs=[wide, pl.BlockSpec((2, NDEV // 2, rb, D), lambda i: (0, 0, 0, 0)), pl.BlockSpec((1, D), lambda i: (0, 0))]
        + [ANY] * nc,
        scratch_shapes=[pltpu.VMEM((e, D), F32)],
        input_output_aliases={5 + k: 3 + k for k in range(nc)},
        compiler_params=_params(("arbitrary",)),
    )(dx, out, ycat, w_out, mod, *carried)
    return outs[0], outs[1], outs[2], list(outs[3:])


def _weight_grad(h_t, d_proj, name):
    nb = d_proj.shape[1] // NDEV

    def body(ht_ref, dp_ref, o_ref):
        o_ref[...] = jnp.dot(ht_ref[...], dp_ref[...], preferred_element_type=F32).astype(BF16)

    return pl.pallas_call(
        body, name=name, grid=(NDEV,), out_shape=jax.ShapeDtypeStruct((2, NDEV // 2, D, nb), BF16),
        in_specs=[pl.BlockSpec((D, S), lambda d: (0, 0)), pl.BlockSpec((S, nb), lambda d: (0, d))],
        out_specs=pl.BlockSpec((None, None, D, nb), lambda d: (d % 2, d // 2, 0, 0)),
        compiler_params=_params(("arbitrary",)),
    )(h_t, d_proj)


def _dh_norm_bwd(d_proj, wg, x, dx, mod, norm_g3, layer, carried, name):
    nb = wg.shape[-1]
    tm = 512
    wb = W_BLOCKS
    rc = 128

    def body(dp_ref, w_ref, x_ref, dx_ref, g_ref, scale_ref, carried_ref,
             dxi_ref, dshift_ref, dscale_ref, dg_ref, carried_out, acc):
        i, d = pl.program_id(0), pl.program_id(1)
        nt = (((1,), (1,)), ((), ()))
        part = lax.dot_general(dp_ref[:, :nb], w_ref[0], nt, preferred_element_type=F32)
        for b in range(1, wb):
            part += lax.dot_general(dp_ref[:, b * nb:(b + 1) * nb], w_ref[b], nt, preferred_element_type=F32)

        @pl.when(d == 0)
        def _():
            acc[...] = part

        @pl.when(d != 0)
        def _():
            acc[...] += part

        @pl.when(jnp.logical_and(i == 0, d == 0))
        def _():
            dshift_ref[...] = jnp.zeros_like(dshift_ref)
            dscale_ref[...] = jnp.zeros_like(dscale_ref)
            dg_ref[...] = jnp.zeros_like(dg_ref)

        @pl.when(d == NDEV // wb - 1)
        def _():
            g = g_ref[...]
            scale1 = 1.0 + scale_ref[...]

            def chunk(k, sums):
                rows = pl.ds(pl.multiple_of(k * rc, rc), rc)
                xv, dhv = x_ref[rows, :], acc[rows, :]
                r = lax.rsqrt(jnp.mean(xv * xv, axis=-1, keepdims=True) + EPS)
                xn = xv * r
                dhn = dhv * scale1
                u = dhn * g
                dxi_ref[rows, :] = dx_ref[rows, :] + r * (u - xn * jnp.mean(xn * u, axis=-1, keepdims=True))
                return (sums[0] + jnp.sum(dhv, axis=0, keepdims=True),
                        sums[1] + jnp.sum(dhv * (xn * g), axis=0, keepdims=True),
                        sums[2] + jnp.sum(dhn * xn, axis=0, keepdims=True))

            zero = jnp.zeros((1, D), F32)
            sums = lax.fori_loop(0, tm // rc, chunk, (zero, zero, zero))
            dshift_ref[...] += sums[0]
            dscale_ref[...] += sums[1]
            dg_ref[...] += sums[2]

    tile = pl.BlockSpec((tm, D), lambda i, d: (i, 0))
    row = pl.BlockSpec((1, D), lambda i, d: (0, 0))
    return pl.pallas_call(
        body, name=name, grid=(S // tm, NDEV // wb),
        out_shape=[jax.ShapeDtypeStruct((S, D), F32)] + [jax.ShapeDtypeStruct((1, D), F32)] * 3
        + [jax.ShapeDtypeStruct(carried.shape, carried.dtype)],
        in_specs=[pl.BlockSpec((tm, wb * nb), lambda i, d: (i, d)), pl.BlockSpec((wb, D, nb), lambda i, d: (d, 0, 0)),
                  tile, tile, pl.BlockSpec((None, 1, D), lambda i, d: (layer, 0, 0)), _mod_spec(layer, 1, 2), ANY],
        out_specs=[tile, row, row, row, ANY], scratch_shapes=[pltpu.VMEM((tm, D), F32)],
        input_output_aliases={6: 4}, compiler_params=_params(("arbitrary", "arbitrary")),
    )(d_proj, wg, x, dx, norm_g3, mod, carried)


TS = 256
NCH = TS // CHUNK
HALO_BLOCKS = TS // HALO


def _halo_before(width, col_block):
    return pl.BlockSpec((HALO, width), lambda i: (jnp.maximum(i * HALO_BLOCKS - 1, 0), col_block))


def _halo_after(width, col_block):
    return pl.BlockSpec((HALO, width), lambda i: (jnp.minimum((i + 1) * HALO_BLOCKS, S // HALO - 1), col_block))


def _shift_down(ext, k):
    return pltpu.roll(ext, k, 0)[HALO:]


def _shift_up(ext, k):
    return pltpu.roll(ext, ext.shape[0] - k, 0)[:ext.shape[0] - HALO]


def _layer_norm_head(v, lg, lb):
    mu = jnp.mean(v, axis=-1, keepdims=True)
    vc = v - mu
    rstd = lax.rsqrt(jnp.mean(vc * vc, axis=-1, keepdims=True) + EPS)
    vhat = vc * rstd
    return vhat, rstd, vhat * lg + lb


def _causal_mask():
    return lax.broadcasted_iota(jnp.int32, (CHUNK, CHUNK), 0) >= lax.broadcasted_iota(jnp.int32, (CHUNK, CHUNK), 1)


def _even_mix_fwd(proj, convw, ln_g3, ln_b3, sgu_w, sgu_bcol, wl, after, name):
    def body(pj_ref, hh_ref, hc_ref, cw_ref, lg_ref, lb_ref, sw_ref, sb_ref, *rest):
        y_ref = rest[-1]
        live = (pl.program_id(0) > 0).astype(F32)
        causal = _causal_mask()
        for j in range(E_A // HEAD):
            cols = slice(j * HEAD, (j + 1) * HEAD)
            w0, w1, w2 = cw_ref[0:1, cols], cw_ref[1:2, cols], cw_ref[2:3, cols]
            lg, lb = lg_ref[:, cols], lb_ref[:, cols]
            wm = jnp.where(causal, sw_ref[j], 0.0).astype(BF16)
            bias = sb_ref[j]

            def split(s, rows, cols=cols):
                return pj_ref[rows, s * E_A + cols.start:s * E_A + cols.stop].astype(F32)

            prev_tail = hc_ref[:, cols].astype(F32) * hh_ref[:, cols].astype(F32) * live
            for n in range(NCH):
                rows = slice(n * CHUNK, (n + 1) * CHUNK)
                p = split(2, rows) * split(0, rows)
                ext = jnp.concatenate([prev_tail, p], axis=0)
                prev_tail = p[CHUNK - HALO:]
                cv = w2 * p + w1 * _shift_down(ext, 1) + w0 * _shift_down(ext, 2)
                y_ref[rows, cols] = (split(1, rows) * cv * _silu(split(3, rows))).astype(BF16)
                _, _, vn = _layer_norm_head(split(5, rows), lg, lb)
                mixed = jnp.dot(wm, vn.astype(BF16), preferred_element_type=F32) + bias
                y_ref[rows, E_A + cols.start:E_A + cols.stop] = (split(4, rows) * mixed * _silu(split(6, rows))).astype(BF16)

    const3 = lambda i: (wl, 0, 0)
    const4 = lambda i: (wl, 0, 0, 0)
    return pl.pallas_call(
        body, name=name, grid=(S // TS,), out_shape=jax.ShapeDtypeStruct((S, 2 * E_A), BF16),
        in_specs=[pl.BlockSpec((TS, 7 * E_A), lambda i: (i, 0)), _halo_before(E_A, 0), _halo_before(E_A, 2),
                  pl.BlockSpec((None, 3, E_A), const3), pl.BlockSpec((None, 1, E_A), const3),
                  pl.BlockSpec((None, 1, E_A), const3), pl.BlockSpec((None, NDEV, CHUNK, CHUNK), const4),
                  pl.BlockSpec((None, NDEV, CHUNK, 1), const4)] + [ANY] * len(after),
        out_specs=pl.BlockSpec((TS, 2 * E_A), lambda i: (i, 0)),
        compiler_params=_params(("arbitrary",)),
    )(proj, proj, proj, convw, ln_g3, ln_b3, sgu_w, sgu_bcol, *after)


def _even_mix_bwd(proj, d_ycat, convw, ln_g3, ln_b3, sgu_w, sgu_bcol, wl, name):
    nsteps = S // TS

    def body(pj_ref, hh_ref, hc_ref, hb_ref, hz_ref, dy_ref, hdy_ref, cw_ref, lg_ref, lb_ref, sw_ref, sb_ref,
             dp_ref, dcw_ref, dlg_ref, dlb_ref, dsw_ref, dsb_ref):
        step = pl.program_id(0)

        @pl.when(step == 0)
        def _():
            for ref in (dcw_ref, dlg_ref, dlb_ref, dsw_ref, dsb_ref):
                ref[...] = jnp.zeros_like(ref)

        live_before = (step > 0).astype(F32)
        live_after = (step < nsteps - 1).astype(F32)
        causal = _causal_mask()
        for j in range(E_A // HEAD):
            cols = slice(j * HEAD, (j + 1) * HEAD)
            w0, w1, w2 = cw_ref[0:1, cols], cw_ref[1:2, cols], cw_ref[2:3, cols]
            lg, lb = lg_ref[:, cols], lb_ref[:, cols]
            wmf = jnp.where(causal, sw_ref[j], 0.0)
            wm, wmt = wmf.astype(BF16), wmf.T.astype(BF16)
            bias = sb_ref[j]

            def split(s, rows, cols=cols):
                return pj_ref[rows, s * E_A + cols.start:s * E_A + cols.stop].astype(F32)

            def put(s, rows, val, cols=cols):
                dp_ref[rows, s * E_A + cols.start:s * E_A + cols.stop] = val.astype(BF16)

            ps = [split(2, slice(n * CHUNK, (n + 1) * CHUNK)) * split(0, slice(n * CHUNK, (n + 1) * CHUNK)) for n in range(NCH)]
            next_head = (hdy_ref[:, cols].astype(F32) * hb_ref[:, cols].astype(F32) * _silu(hz_ref[:, cols].astype(F32))
                         * live_after)
            acc_w = [jnp.zeros((1, HEAD), F32) for _ in range(3)]
            for n in reversed(range(NCH)):
                rows = slice(n * CHUNK, (n + 1) * CHUNK)
                p = ps[n]
                tail = ps[n - 1][CHUNK - HALO:] if n > 0 else hc_ref[:, cols].astype(F32) * hh_ref[:, cols].astype(F32) * live_before
                ext = jnp.concatenate([tail, p], axis=0)
                p1, p2 = _shift_down(ext, 1), _shift_down(ext, 2)
                cv = w2 * p + w1 * p1 + w0 * p2
                a_b, a_z = split(1, rows), split(3, rows)
                sz, dsz = _silu_and_grad(a_z)
                dya = dy_ref[rows, cols].astype(F32)
                put(1, rows, dya * cv * sz)
                put(3, rows, dya * a_b * cv * dsz)
                gcv = dya * a_b * sz
                acc_w[0] += jnp.sum(gcv * p2, axis=0, keepdims=True)
                acc_w[1] += jnp.sum(gcv * p1, axis=0, keepdims=True)
                acc_w[2] += jnp.sum(gcv * p, axis=0, keepdims=True)
                gext = jnp.concatenate([gcv, next_head], axis=0)
                next_head = gcv[:HALO]
                dpv = w2 * gcv + w1 * _shift_up(gext, 1) + w0 * _shift_up(gext, 2)
                put(2, rows, dpv * split(0, rows))
                put(0, rows, dpv * split(2, rows))
            for k in range(3):
                dcw_ref[k:k + 1, cols] += acc_w[k]

            acc_lg, acc_lb = jnp.zeros((1, HEAD), F32), jnp.zeros((1, HEAD), F32)
            acc_sw, acc_sb = jnp.zeros((CHUNK, CHUNK), F32), jnp.zeros((CHUNK, 1), F32)
            for n in range(NCH):
                rows = slice(n * CHUNK, (n + 1) * CHUNK)
                u, z = split(4, rows), split(6, rows)
                vhat, rstd, vn = _layer_norm_head(split(5, rows), lg, lb)
                vn16 = vn.astype(BF16)
                mixed = jnp.dot(wm, vn16, preferred_element_type=F32) + bias
                sz, dsz = _silu_and_grad(z)
                dyb = dy_ref[rows, E_A + cols.start:E_A + cols.stop].astype(F32)
                put(4, rows, dyb * mixed * sz)
                put(6, rows, dyb * u * mixed * dsz)
                dmix = dyb * u * sz
                dmix16 = dmix.astype(BF16)
                acc_sb += jnp.sum(dmix, axis=1, keepdims=True)
                acc_sw += lax.dot_general(dmix16, vn16, (((1,), (1,)), ((), ())), preferred_element_type=F32)
                dvn = jnp.dot(wmt, dmix16, preferred_element_type=F32)
                acc_lg += jnp.sum(dvn * vhat, axis=0, keepdims=True)
                acc_lb += jnp.sum(dvn, axis=0, keepdims=True)
                dvh = dvn * lg
                put(5, rows, rstd * (dvh - jnp.mean(dvh, axis=-1, keepdims=True)
                                     - vhat * jnp.mean(dvh * vhat, axis=-1, keepdims=True)))
            dlg_ref[:, cols] += acc_lg
            dlb_ref[:, cols] += acc_lb
            dsw_ref[j] += jnp.where(causal, acc_sw, 0.0)
            dsb_ref[j] += acc_sb

    const3 = lambda i: (wl, 0, 0)
    const4 = lambda i: (wl, 0, 0, 0)
    fixed2 = lambda i: (0, 0)
    fixed3 = lambda i: (0, 0, 0)
    return pl.pallas_call(
        body, name=name, grid=(nsteps,),
        out_shape=[jax.ShapeDtypeStruct((S, 7 * E_A), BF16), jax.ShapeDtypeStruct((3, E_A), F32),
                   jax.ShapeDtypeStruct((1, E_A), F32), jax.ShapeDtypeStruct((1, E_A), F32),
                   jax.ShapeDtypeStruct((NDEV, CHUNK, CHUNK), F32), jax.ShapeDtypeStruct((NDEV, CHUNK, 1), F32)],
        in_specs=[pl.BlockSpec((TS, 7 * E_A), lambda i: (i, 0)), _halo_before(E_A, 0), _halo_before(E_A, 2),
                  _halo_after(E_A, 1), _halo_after(E_A, 3),
                  pl.BlockSpec((TS, 2 * E_A), lambda i: (i, 0)), _halo_after(E_A, 0),
                  pl.BlockSpec((None, 3, E_A), const3), pl.BlockSpec((None, 1, E_A), const3),
                  pl.BlockSpec((None, 1, E_A), const3), pl.BlockSpec((None, NDEV, CHUNK, CHUNK), const4),
                  pl.BlockSpec((None, NDEV, CHUNK, 1), const4)],
        out_specs=[pl.BlockSpec((TS, 7 * E_A), lambda i: (i, 0)), pl.BlockSpec((3, E_A), fixed2),
                   pl.BlockSpec((1, E_A), fixed2), pl.BlockSpec((1, E_A), fixed2),
                   pl.BlockSpec((NDEV, CHUNK, CHUNK), fixed3), pl.BlockSpec((NDEV, CHUNK, 1), fixed3)],
        compiler_params=_params(("arbitrary",)),
    )(proj, proj, proj, proj, proj, d_ycat, d_ycat, convw, ln_g3, ln_b3, sgu_w, sgu_bcol)


def _window_count(step, n, win, ext_before):
    rows = CHUNK if ext_before else CHUNK + HALO
    t = step * TS + n * CHUNK + lax.broadcasted_iota(jnp.int32, (rows, 1), 0)
    return jnp.minimum(t + 1, win).astype(F32)


def _pool_weight(wp_ref, g):
    return jnp.concatenate([wp_ref[d, g] for d in range(NDEV)], axis=0)


def _pooled_chunk(p, tail, win, count):
    sums = jnp.concatenate([tail, p], axis=0)
    shift = 1
    while shift < win:
        sums = sums + pltpu.roll(sums, shift, 0)
        shift *= 2
    return sums[HALO:] / count - p


def _pool_mix_fwd(proj, wpool, pscale4, wl, after, name):
    e_c = 4 * G_C

    def body(pj_ref, hp_ref, wp_ref, ps_ref, *rest):
        y_ref, pooled_scr, yraw_scr = rest[-3:]
        step = pl.program_id(0)
        live = (step > 0).astype(F32)
        for g, win in enumerate(POOL_WINDOWS):
            for q in range(G_C // LANE):
                cols = slice(g * G_C + q * LANE, g * G_C + (q + 1) * LANE)
                tail = hp_ref[:, cols].astype(F32) * live
                for n in range(NCH):
                    rows = slice(n * CHUNK, (n + 1) * CHUNK)
                    p = pj_ref[rows, cols].astype(F32)
                    pooled_scr[rows, q * LANE:(q + 1) * LANE] = _pooled_chunk(
                        p, tail, win, _window_count(step, n, win, True)).astype(BF16)
                    tail = p[CHUNK - HALO:]
            yraw_scr[...] = jnp.dot(pooled_scr[...], _pool_weight(wp_ref, g), preferred_element_type=F32)
            for q in range(G_C // LANE):
                cols = slice(g * G_C + q * LANE, g * G_C + (q + 1) * LANE)
                for n in range(NCH):
                    rows = slice(n * CHUNK, (n + 1) * CHUNK)
                    z = pj_ref[rows, e_c + cols.start:e_c + cols.stop].astype(F32)
                    y_ref[rows, cols] = (yraw_scr[rows, q * LANE:(q + 1) * LANE] * ps_ref[:, cols] * _silu(z)).astype(BF16)

    return pl.pallas_call(
        body, name=name, grid=(S // TS,), out_shape=jax.ShapeDtypeStruct((S, e_c), BF16),
        in_specs=[pl.BlockSpec((TS, 2 * e_c), lambda i: (i, 0)), _halo_before(e_c, 0),
                  pl.BlockSpec((NDEV, 4, G_C // NDEV, G_C), lambda i: (0, 0, 0, 0)),
                  pl.BlockSpec((None, 1, e_c), lambda i: (wl, 0, 0))] + [ANY] * len(after),
        out_specs=pl.BlockSpec((TS, e_c), lambda i: (i, 0)),
        scratch_shapes=[pltpu.VMEM((TS, G_C), BF16), pltpu.VMEM((TS, G_C), F32)],
        compiler_params=_params(("arbitrary",)),
    )(proj, proj, wpool, pscale4, *after)


def _pool_mix_bwd(proj, d_ycat, wpool, pscale4, wl, name):
    e_c = 4 * G_C
    nsteps = S // TS
    rb = G_C // NDEV

    def body(pj_ref, hp_ref, hz_ref, dy_ref, hdy_ref, wp_ref, ps_ref,
             dp_ref, dps_ref, dwp_ref, pooled_scr, yraw_scr, dyraw_scr, dpool_scr, acc_w):
        step = pl.program_id(0)

        @pl.when(step == 0)
        def _():
            dps_ref[...] = jnp.zeros_like(dps_ref)
            acc_w[...] = jnp.zeros_like(acc_w)

        live_before = (step > 0).astype(F32)
        live_after = (step < nsteps - 1).astype(F32)
        for g, win in enumerate(POOL_WINDOWS):
            weight = _pool_weight(wp_ref, g)
            for q in range(G_C // LANE):
                cols = slice(g * G_C + q * LANE, g * G_C + (q + 1) * LANE)
                tail = hp_ref[:, cols].astype(F32) * live_before
                for n in range(NCH):
                    rows = slice(n * CHUNK, (n + 1) * CHUNK)
                    p = pj_ref[rows, cols].astype(F32)
                    pooled_scr[rows, q * LANE:(q + 1) * LANE] = _pooled_chunk(
                        p, tail, win, _window_count(step, n, win, True)).astype(BF16)
                    tail = p[CHUNK - HALO:]
            yraw_scr[...] = jnp.dot(pooled_scr[...], weight, preferred_element_type=F32)
            for q in range(G_C // LANE):
                cols = slice(g * G_C + q * LANE, g * G_C + (q + 1) * LANE)
                local = slice(q * LANE, (q + 1) * LANE)
                scale = ps_ref[:, cols]
                acc_ps = jnp.zeros((1, LANE), F32)
                for n in range(NCH):
                    rows = slice(n * CHUNK, (n + 1) * CHUNK)
                    sz, dsz = _silu_and_grad(pj_ref[rows, e_c + cols.start:e_c + cols.stop].astype(F32))
                    dyv = dy_ref[rows, cols].astype(F32)
                    yraw = yraw_scr[rows, local]
                    dyraw_scr[rows, local] = (dyv * scale * sz).astype(BF16)
                    acc_ps += jnp.sum(dyv * yraw * sz, axis=0, keepdims=True)
                    dp_ref[rows, e_c + cols.start:e_c + cols.stop] = (dyv * yraw * scale * dsz).astype(BF16)
                dps_ref[:, cols] += acc_ps
                dyraw_scr[TS:, local] = (hdy_ref[:, cols].astype(F32) * scale * _silu(hz_ref[:, cols].astype(F32))
                                         * live_after).astype(BF16)
            dpool_scr[...] = lax.dot_general(dyraw_scr[...], weight, (((1,), (1,)), ((), ())), preferred_element_type=F32)
            acc_w[g] += lax.dot_general(pooled_scr[...], dyraw_scr[:TS, :], (((0,), (0,)), ((), ())),
                                        preferred_element_type=F32)
            for q in range(G_C // LANE):
                cols = slice(g * G_C + q * LANE, g * G_C + (q + 1) * LANE)
                local = slice(q * LANE, (q + 1) * LANE)
                for n in range(NCH):
                    rows = slice(n * CHUNK, (n + 1) * CHUNK)
                    ext = dpool_scr[n * CHUNK:(n + 1) * CHUNK + HALO, local]
                    sums = ext / _window_count(step, n, win, False)
                    shift = 1
                    while shift < win:
                        sums = sums + pltpu.roll(sums, CHUNK + HALO - shift, 0)
                        shift *= 2
                    dp_ref[rows, cols] = (sums[:CHUNK] - ext[:CHUNK]).astype(BF16)

        @pl.when(step == nsteps - 1)
        def _():
            for g in range(4):
                for d in range(NDEV):
                    dwp_ref[d % 2, d // 2, g] = acc_w[g, d * rb:(d + 1) * rb, :].astype(BF16)

    in_specs = [pl.BlockSpec((TS, 2 * e_c), lambda i: (i, 0)), _halo_before(e_c, 0), _halo_after(e_c, 1),
                pl.BlockSpec((TS, e_c), lambda i: (i, 0)), _halo_after(e_c, 0),
                pl.BlockSpec((NDEV, 4, rb, G_C), lambda i: (0, 0, 0, 0)),
                pl.BlockSpec((None, 1, e_c), lambda i: (wl, 0, 0))]
    args = [proj, proj, proj, d_ycat, d_ycat, wpool, pscale4]
    return pl.pallas_call(
        body, name=name, grid=(nsteps,),
        out_shape=[jax.ShapeDtypeStruct((S, 2 * e_c), BF16), jax.ShapeDtypeStruct((1, e_c), F32),
                   jax.ShapeDtypeStruct((2, NDEV // 2) + wpool.shape[1:], BF16)],
        in_specs=in_specs,
        out_specs=[pl.BlockSpec((TS, 2 * e_c), lambda i: (i, 0)), pl.BlockSpec((1, e_c), lambda i: (0, 0)),
                   pl.BlockSpec((2, NDEV // 2, 4, rb, G_C), lambda i: (0, 0, 0, 0, 0))],
        scratch_shapes=[pltpu.VMEM((TS, G_C), BF16), pltpu.VMEM((TS, G_C), F32), pltpu.VMEM((TS + HALO, G_C), BF16),
                        pltpu.VMEM((TS + HALO, G_C), F32), pltpu.VMEM((4, G_C, G_C), F32)],
        compiler_params=_params(("arbitrary",)),
    )(*args)


def _adamw(w, g, m, v):
    m = ADAM_B1 * m + (1.0 - ADAM_B1) * g
    v = ADAM_B2 * v + (1.0 - ADAM_B2) * jnp.square(g)
    m_hat = m / (1.0 - ADAM_B1 ** ADAM_STEP)
    v_hat = v / (1.0 - ADAM_B2 ** ADAM_STEP)
    delta = -ADAM_LR * (m_hat / (jnp.sqrt(v_hat) + ADAM_EPS) + ADAM_WD * w)
    return delta, m, v


def _adam_sharded(w, m, v, chip_parts, landed, my_chip, carried, name, first=0, into=()):
    _, nr, ncol = w.shape
    nl = len(chip_parts)
    tr = 128
    steps = nr // tr
    nc, ni = len(carried), len(into)

    def body(chip_ref, w_ref, m_ref, v_ref, *rest):
        parts, zones = rest[:nl], rest[nl:2 * nl]
        g_ref, d_ref, nm_ref, nv_ref = rest[2 * nl + nc + ni:2 * nl + nc + ni + 4]
        layer = pl.program_id(0)
        g = jnp.zeros((tr, ncol), F32)
        for l in range(nl):
            gl = parts[l][...].astype(F32)
            for q in range(3):
                gl = gl + zones[l][q].astype(F32)
            g = jnp.where(layer == l, gl, g)
        g_ref[...] = g
        d_ref[...], nm_ref[...], nv_ref[...] = _adamw(w_ref[...], g, m_ref[...], v_ref[...])

    def rows_of(l):
        return lambda layer, i, chip_ref: jnp.where(layer == l, i, jnp.where(layer < l, 0, steps - 1))

    spec = pl.BlockSpec((None, tr, ncol), lambda layer, i, chip_ref: (first + layer, i, 0))
    part_specs = [pl.BlockSpec((None, tr, ncol), lambda layer, i, chip_ref, l=l: (chip_ref[0], rows_of(l)(layer, i, chip_ref), 0))
                  for l in range(nl)]
    zone_specs = [pl.BlockSpec((3, tr, ncol), lambda layer, i, chip_ref, l=l: (0, rows_of(l)(layer, i, chip_ref), 0))
                  for l in range(nl)]
    grid_spec = pltpu.PrefetchScalarGridSpec(
        num_scalar_prefetch=1, grid=(nl, steps),
        in_specs=[spec, spec, spec] + part_specs + zone_specs + [ANY] * (nc + ni), out_specs=[spec] * 4 + [ANY] * nc)
    aliases = {4 + 2 * nl + k: 4 + k for k in range(nc)}
    aliases.update({4 + 2 * nl + nc + k: k for k in range(ni)})
    return pl.pallas_call(
        body, name=name, grid_spec=grid_spec,
        out_shape=[jax.ShapeDtypeStruct(w.shape, F32)] * 4 + [jax.ShapeDtypeStruct(a.shape, a.dtype) for a in carried],
        input_output_aliases=aliases, compiler_params=_params(("arbitrary", "arbitrary")),
    )(my_chip, w, m, v, *chip_parts, *landed, *carried, *into)


def _adam_small(params, name):
    n = len(params)

    def body(*refs):
        ins, outs = refs[:4 * n], refs[4 * n:]
        for k in range(n):
            w_ref, g_ref, m_ref, v_ref = ins[4 * k:4 * k + 4]
            outs[3 * k][...], outs[3 * k + 1][...], outs[3 * k + 2][...] = _adamw(w_ref[...], g_ref[...], m_ref[...], v_ref[...])

    outs = pl.pallas_call(body, name=name, out_shape=[jax.ShapeDtypeStruct(p[0].shape, F32) for p in params for _ in range(3)],
                          in_specs=[VMEM_FULL] * (4 * n), out_specs=[VMEM_FULL] * (3 * n),
                          compiler_params=_params())(*[a for p in params for a in p])
    return [list(outs[3 * k:3 * k + 3]) for k in range(n)]


def _sum_devices(gathered, name):
    _, nr, ncol = gathered.shape

    def body(g_ref, o_ref):
        acc = g_ref[0].astype(F32)
        for s in range(1, NDEV):
            acc = acc + g_ref[s].astype(F32)
        o_ref[...] = acc

    return pl.pallas_call(body, name=name, grid=(1,), out_shape=jax.ShapeDtypeStruct((nr, ncol), F32),
                          in_specs=[pl.BlockSpec((NDEV, nr, ncol), lambda i: (0, 0, 0))],
                          out_specs=pl.BlockSpec((nr, ncol), lambda i: (0, 0)),
                          compiler_params=_params(("arbitrary",)))(gathered)


def _ada_weight_adam(cact_t, dmod_mine, w, m, v):
    def body(ct_ref, dm_ref, w_ref, m_ref, v_ref, g_ref, d_ref, nm_ref, nv_ref):
        ct, dm = ct_ref[...], dm_ref[...]
        g = ct[:, 0:1] * dm[0:1, :]
        for e in range(1, NDEV):
            g = g + ct[:, e:e + 1] * dm[e:e + 1, :]
        g_ref[...] = g
        d_ref[...], nm_ref[...], nv_ref[...] = _adamw(w_ref[...], g, m_ref[...], v_ref[...])

    spec = pl.BlockSpec((None, D, ADA_NC), lambda l: (l, 0, 0))
    return pl.pallas_call(
        body, name="ada_weight_adam", grid=(DEPTH,), out_shape=[jax.ShapeDtypeStruct(w.shape, F32)] * 4,
        in_specs=[pl.BlockSpec((D, NDEV), lambda l: (0, 0)), pl.BlockSpec((None, NDEV, ADA_NC), lambda l: (l, 0, 0)),
                  spec, spec, spec],
        out_specs=[spec] * 4, compiler_params=_params(("arbitrary",)),
    )(cact_t, dmod_mine, w, m, v)


def _pad_rows(a, rows):
    a = a.reshape(-1, D)
    return jnp.pad(a, ((0, rows - a.shape[0]), (0, 0)))


def kernel(x, c, norm_g, ada_w, ada_b, ab_w_in, ab_conv_w, ab_ln_g, ab_ln_b, ab_sgu_w, ab_sgu_b, ab_w_out, c_w_in, c_pool_w, c_pool_scale, c_w_out, final_g, loss_target, m_norm_g, m_ada_w, m_ada_b, m_ab_w_in, m_ab_conv_w, m_ab_ln_g, m_ab_ln_b, m_ab_sgu_w, m_ab_sgu_b, m_ab_w_out, m_c_w_in, m_c_pool_w, m_c_pool_scale, m_c_w_out, m_final_g, v_norm_g, v_ada_w, v_ada_b, v_ab_w_in, v_ab_conv_w, v_ab_ln_g, v_ab_ln_b, v_ab_sgu_w, v_ab_sgu_b, v_ab_w_out, v_c_w_in, v_c_pool_w, v_c_pool_scale, v_c_w_out, v_final_g):
    x_pos, y_pos, c_pos = _position()
    me = _index((x_pos, y_pos, c_pos))
    core = c_pos.astype(jnp.int32).reshape(1)
    my_chip = (2 * x_pos + y_pos).astype(jnp.int32).reshape(1)
    me1 = me.astype(jnp.int32).reshape(1)
    x0 = x.reshape(S, D)
    target = loss_target.reshape(S, D)
    norm_g3 = norm_g.reshape(DEPTH, 1, D)
    ln_g3, ln_b3 = ab_ln_g.reshape(2, 1, E_A), ab_ln_b.reshape(2, 1, E_A)
    sgu_bcol = ab_sgu_b.reshape(2, NDEV, CHUNK, 1)
    rb = G_C // NDEV
    pool_w3, m_pool_w3, v_pool_w3 = (a.reshape(2, 4 * rb, G_C) for a in (c_pool_w, m_c_pool_w, v_c_pool_w))

    cact_all, mod, (convw_all, pscale_all) = _ada_forward(c, ada_w, ada_b, [ab_conv_w, c_pool_scale])
    convw = jnp.transpose(convw_all, (1, 2, 0, 3)).reshape(2, 3, E_A)
    pscale4 = jnp.transpose(pscale_all, (1, 0, 2)).reshape(2, 1, 4 * G_C)
    zones = []
    for layer in range(DEPTH):
        wl = layer // 2
        if layer % 2 == 0:
            zones.append([_to_zone(ab_w_in, wl, me1, BF16, f"cast_w_in_{layer}"), _to_zone(ab_w_out, wl, me1, BF16, f"cast_w_out_{layer}")])
        else:
            zones.append([_to_zone(c_w_in, wl, me1, BF16, f"cast_w_in_{layer}"), _to_zone(c_w_out, wl, me1, BF16, f"cast_w_out_{layer}"),
                          _to_zone(pool_w3, wl, me1, BF16, f"cast_pool_w_{layer}")])

    def gathered(flight, after, layer):
        wg = _gather_end(flight, [after], f"gather_end_{layer}")
        return [wg[0], wg[1].reshape(-1, D)] + [w.reshape(NDEV, 4, rb, G_C) for w in wg[2:]]

    flight, (mod,) = _gather_start(zones[0], [convw_all], [mod], "gather_start_0")
    flight, (mod,) = _gather_mid(flight, [z for zs in zones[1:] for z in zs], [mod], "gather_mid_0")
    next_flight, (mod,) = _gather_start(zones[1], [], [mod], "gather_start_1")
    flight = _gather_late(flight, [mod], "gather_late_0")
    xs, hts, projs, ycats, outs, gathered_w = [x0], [], [], [], [], [gathered(flight, mod, 0)]
    for layer in range(DEPTH):
        wl = layer // 2
        even = layer % 2 == 0
        wg = gathered_w[layer]
        h_t, proj = _norm_proj(xs[-1], mod, norm_g3, wg[0], layer, f"norm_proj_{layer}")
        if layer + 1 < DEPTH:
            flight, (h_t,) = _gather_mid(next_flight, [], [h_t], f"gather_mid_{layer + 1}")
            if layer + 2 < DEPTH:
                next_flight, (h_t,) = _gather_start(zones[layer + 2], [], [h_t], f"gather_start_{layer + 2}")
        if even:
            ycat = _even_mix_fwd(proj, convw, ln_g3, ln_b3, ab_sgu_w, sgu_bcol, wl, [h_t], f"even_mix_fwd_{layer}")
        else:
            ycat = _pool_mix_fwd(proj, wg[2], pscale4, wl, [h_t], f"pool_mix_fwd_{layer}")
        if layer + 1 < DEPTH:
            flight = _gather_late(flight, [ycat], f"gather_late_{layer + 1}")
        if layer + 1 < DEPTH:
            x_new, out = _out_proj(ycat, wg[1], xs[-1], mod, layer, f"out_proj_{layer}")
            gathered_w.append(gathered(flight, x_new, layer + 1))
            xs.append(x_new)
        else:
            dx, out, loss_part, d_final_g = _out_proj(ycat, wg[1], xs[-1], mod, layer, f"out_proj_{layer}",
                                                      final=(target, final_g.reshape(1, D)))
        hts.append(h_t)
        projs.append(proj)
        ycats.append(ycat)
        outs.append(out)

    d_mod, d_norm_g = [None] * DEPTH, [None] * DEPTH
    small, scatters, landed, res = {}, {}, {}, {}

    def finish_scatter(layer, after):
        send_sems, recv_sems, chip_parts, zones = scatters[layer]
        landed[layer] = _scatter_end(chip_parts, zones, send_sems, recv_sems, after, f"scatter_end_{layer}")

    def flat(a):
        return a.reshape(a.shape[0], -1, a.shape[-1])

    def sharded_adam(k, j, layers, w, m, v, carried, first=0, into=()):
        outs4 = _adam_sharded(w, m, v, [flat(landed[l][0][j]) for l in layers], [flat(landed[l][1][j]) for l in layers],
                              my_chip, carried, f"adam_{k}_{first}" if len(layers) < w.shape[0] else "adam_" + k, first, into)
        res[k] = [o.reshape(c_pool_w.shape) if k == "c_pool_w" else o for o in outs4[:4]]
        return list(outs4[4:])

    previous = None
    for layer in reversed(range(DEPTH)):
        wl = layer // 2
        even = layer % 2 == 0
        wg = gathered_w[layer]
        carried = [] if previous is None else [scatters[previous][2][0]]
        d_ycat, grad_out, d_gate, carried = _out_bwd(dx, outs[layer], ycats[layer], wg[1], mod, layer, carried, f"out_bwd_{layer}")
        if previous is not None:
            scatters[previous][2][0] = carried[0]
        parts = [None, grad_out]
        if even:
            d_proj, d_cw, d_lg, d_lb, d_sw, d_sb = _even_mix_bwd(
                projs[layer], d_ycat, convw, ln_g3, ln_b3, ab_sgu_w, sgu_bcol, wl, f"even_mix_bwd_{layer}")
            small[layer] = (d_cw, d_lg, d_lb, d_sw, d_sb)
        else:
            d_proj, d_ps, d_pool = _pool_mix_bwd(projs[layer], d_ycat, wg[2], pscale4, wl, f"pool_mix_bwd_{layer}")
            small[layer] = (d_ps,)
            parts.append(d_pool)
        parts[0] = _weight_grad(hts[layer], d_proj, f"grad_w_in_{layer}")
        pair_send, pair_recv, parts, from_sibling = _pair_start(parts, f"pair_start_{layer}")
        if layer > 0:
            dx, d_shift, d_scale, d_norm_g[layer], parts[0] = _dh_norm_bwd(
                d_proj, wg[0], xs[layer], dx, mod, norm_g3, layer, parts[0], f"dh_norm_bwd_{layer}")
            pair_after = dx
        else:
            finish_scatter(1, d_proj)
            finish_scatter(3, d_proj)
            parts[0], = sharded_adam("c_w_out", 1, (1, 3), c_w_out, m_c_w_out, v_c_w_out, [parts[0]])
            parts[0], = sharded_adam("c_pool_w", 2, (1, 3), pool_w3, m_pool_w3, v_pool_w3, [parts[0]])
            pair_after = res["c_pool_w"][0]
        parts, from_sibling = _pair_end(parts, from_sibling, pair_send, pair_recv, pair_after, f"pair_end_{layer}")
        chip_parts = [_pair_sum(p, q, core, f"pair_sum_{layer}_{j}") for j, (p, q) in enumerate(zip(parts, from_sibling))]
        send_sems, recv_sems, chip_parts, zones = _scatter_start(chip_parts, f"scatter_start_{layer}")
        if layer == 0:
            chip_parts[0], = sharded_adam("c_w_in", 0, (1, 3), c_w_in, m_c_w_in, v_c_w_in, [chip_parts[0]])
            dx, d_shift, d_scale, d_norm_g[layer], chip_parts[0] = _dh_norm_bwd(
                d_proj, wg[0], xs[layer], dx, mod, norm_g3, layer, chip_parts[0], f"dh_norm_bwd_{layer}")
        scatters[layer] = [send_sems, recv_sems, chip_parts, zones]
        previous = layer
        d_mod[layer] = jnp.concatenate([d_shift, d_scale, d_gate], axis=0)
    grad_x = dx.reshape(x.shape)

    sections = [("norm_g", jnp.concatenate(d_norm_g, axis=0), 8),
                ("d_mod", jnp.concatenate(d_mod, axis=0), 16),
                ("ab_ln_g", jnp.concatenate([small[0][1], small[2][1]], axis=0), 8),
                ("ab_ln_b", jnp.concatenate([small[0][2], small[2][2]], axis=0), 8),
                ("ab_sgu_b", jnp.stack([small[0][4], small[2][4]]), 8),
                ("final_g", d_final_g, 8),
                ("ab_conv_w", jnp.stack([small[0][0], small[2][0]]), 8),
                ("c_pool_scale", jnp.concatenate([small[1][0], small[3][0]], axis=0), 8),
                ("ab_sgu_w", jnp.stack([small[0][3], small[2][3]]), 256)]
    offsets, at = {}, 0
    for name, _, rows in sections:
        offsets[name] = (at, rows)
        at += rows
    packed = jnp.concatenate([_pad_rows(a, rows) for _, a, rows in sections] + [jnp.zeros((-at % 32, D), F32)], axis=0)
    loss_rows = jnp.pad(loss_part, ((0, 15), (0, D - LANE)))
    small_zones = [_to_zone(packed[None], 0, me1, BF16, "place_small_grads"), _to_zone(loss_rows[None], 0, me1, F32, "place_loss")]
    small_flight, (mod,) = _gather_start(small_zones, [], [mod], "gather_small_start")

    finish_scatter(2, mod)
    sharded_adam("ab_w_out", 1, (2,), ab_w_out, m_ab_w_out, v_ab_w_out, [], first=1)
    sharded_adam("ab_w_in", 0, (2,), ab_w_in, m_ab_w_in, v_ab_w_in, [], first=1)
    finish_scatter(0, res["ab_w_in"][0])
    sharded_adam("ab_w_out", 1, (0,), ab_w_out, m_ab_w_out, v_ab_w_out, [], into=res["ab_w_out"])
    small_flight, (mod,) = _gather_mid(small_flight, [res["ab_w_out"][0]], [mod], "gather_small_mid")
    sharded_adam("ab_w_in", 0, (0,), ab_w_in, m_ab_w_in, v_ab_w_in, [mod], into=res["ab_w_in"])

    last = res["ab_w_in"][0]
    small_flight = _gather_late(small_flight, [last], "gather_small_late")
    small_grads, losses = _gather_end(small_flight, [last], "gather_small_end")
    summed = _sum_devices(small_grads, "sum_small_grads")
    loss = _sum_devices(losses, "sum_loss")[0, 0]

    def section(name, nrows, src=summed):
        start = offsets[name][0]
        return src[..., start:start + nrows, :]

    grads = {
        "norm_g": section("norm_g", DEPTH),
        "ada_b": section("d_mod", 3 * DEPTH).reshape(DEPTH, 3 * D),
        "ab_ln_g": section("ab_ln_g", 2), "ab_ln_b": section("ab_ln_b", 2),
        "ab_sgu_b": section("ab_sgu_b", 2).reshape(ab_sgu_b.shape),
        "final_g": section("final_g", 1),
        "ab_sgu_w": section("ab_sgu_w", 256).reshape(ab_sgu_w.shape),
        "ab_conv_w": lax.dynamic_slice_in_dim(section("ab_conv_w", 6).reshape(2, 3, E_A), me * HEAD, HEAD, axis=2),
        "c_pool_scale": lax.dynamic_slice_in_dim(section("c_pool_scale", 4).reshape(2, 4 * G_C), me * 256, 256, axis=1),
    }
    small_w = {"norm_g": (norm_g, m_norm_g, v_norm_g), "ada_b": (ada_b, m_ada_b, v_ada_b),
               "ab_ln_g": (ab_ln_g, m_ab_ln_g, v_ab_ln_g), "ab_ln_b": (ab_ln_b, m_ab_ln_b, v_ab_ln_b),
               "ab_sgu_b": (ab_sgu_b, m_ab_sgu_b, v_ab_sgu_b),
               "final_g": (final_g.reshape(1, D), m_final_g.reshape(1, D), v_final_g.reshape(1, D)),
               "ab_sgu_w": (ab_sgu_w, m_ab_sgu_w, v_ab_sgu_w), "ab_conv_w": (ab_conv_w, m_ab_conv_w, v_ab_conv_w),
               "c_pool_scale": (c_pool_scale, m_c_pool_scale, v_c_pool_scale)}
    updates = _adam_small([(w, grads[k], m, v) for k, (w, m, v) in small_w.items()], "adam_small")
    for k, update in zip(small_w, updates):
        res[k] = [grads[k]] + update
    res["final_g"] = [a.reshape(D) for a in res["final_g"]]

    dmod_all = section("d_mod", 3 * DEPTH, small_grads).reshape(NDEV, DEPTH, 3 * D)
    dmod_mine = jnp.transpose(lax.dynamic_slice_in_dim(dmod_all, me * ADA_NC, ADA_NC, axis=2), (1, 0, 2)).astype(F32)
    res["ada_w"] = _ada_weight_adam(jnp.transpose(cact_all.reshape(NDEV, D)), dmod_mine, ada_w, m_ada_w, v_ada_w)

    order = ["norm_g", "ada_w", "ada_b", "ab_w_in", "ab_conv_w", "ab_ln_g", "ab_ln_b", "ab_sgu_w", "ab_sgu_b",
             "ab_w_out", "c_w_in", "c_pool_w", "c_pool_scale", "c_w_out", "final_g"]
    return (loss, grad_x, *[res[k][0] for k in order], *[res[k][1] for k in order],
            *[res[k][2] for k in order], *[res[k][3] for k in order])
```

```python
import jax
import jax.numpy as jnp
from jax import lax
from jax.experimental import pallas as pl
from jax.experimental.pallas import tpu as pltpu

F32, BF16 = jnp.float32, jnp.bfloat16
S, D = 2048, 1024
NDEV = 8
DEPTH = 4
EPS = 1e-6
E_A = 1024
HEAD = 128
CHUNK = 128
POOL_WINDOWS = (2, 4, 8, 16)
G_C = 512
HALO = 16
ADA_NC = 384
MIB = 1024 * 1024
LANE = 128

ADAM_LR, ADAM_B1, ADAM_B2, ADAM_EPS, ADAM_WD, ADAM_STEP = 0.001, 0.9, 0.999, 1e-08, 0.01, 10

ANY = pl.BlockSpec(memory_space=pl.ANY)
VMEM_FULL = pl.BlockSpec(memory_space=pltpu.VMEM)
IN_HBM = pl.BlockSpec(memory_space=pltpu.HBM)
SEMAPHORES = pl.BlockSpec(memory_space=pltpu.SEMAPHORE)
IN_FLIGHT = pltpu.SideEffectType.DATAFLOW_SIDE_EFFECTING


V7X_VMEM_MIB = 64
VMEM_LIMIT_MIB = V7X_VMEM_MIB - 4


def _params(semantics=None):
    return pltpu.CompilerParams(dimension_semantics=semantics, vmem_limit_bytes=VMEM_LIMIT_MIB * MIB)


def _silu(z):
    return z * jax.nn.sigmoid(z)


def _silu_and_grad(z):
    sig = jax.nn.sigmoid(z)
    return z * sig, sig * (1.0 + z * (1.0 - sig))


def _position():
    return lax.axis_index("x"), lax.axis_index("y"), lax.axis_index("c")


def _index(pos):
    return 4 * pos[0] + 2 * pos[1] + pos[2]


def _peer(pos, k):
    flipped = tuple(1 - p if (k >> (2 - b)) & 1 else p for b, p in enumerate(pos))
    return flipped, _index(flipped)


def _remote(src, dst, send_sem, recv_sem, device):
    return pltpu.make_async_remote_copy(src_ref=src, dst_ref=dst, send_sem=send_sem, recv_sem=recv_sem,
                                        device_id=device, device_id_type=pl.DeviceIdType.MESH)


def _pair_start(parts, name):
    n = len(parts)
    lands = [_in_hbm(lax.empty(p.shape[1:], p.dtype)) for p in parts]

    def body(*refs):
        ins, zones = refs[:n], refs[n:2 * n]
        send_sems, recv_sems = refs[2 * n:2 * n + 2]
        x, y, c = _position()
        for j in range(n):
            _remote(ins[j].at[1 - c], zones[j], send_sems.at[j], recv_sems.at[j], (x, y, 1 - c)).start()

    outs = pl.pallas_call(
        body, name=name,
        out_shape=(pltpu.SemaphoreType.DMA((n,)), pltpu.SemaphoreType.DMA((n,)),
                   *[pltpu.HBM(p.shape, p.dtype) for p in parts], *[pltpu.HBM(z.shape, z.dtype) for z in lands]),
        in_specs=[IN_HBM] * (2 * n), out_specs=(SEMAPHORES, SEMAPHORES, *[IN_HBM] * (2 * n)),
        input_output_aliases={j: 2 + j for j in range(2 * n)},
        compiler_params=pltpu.CompilerParams(has_side_effects=IN_FLIGHT),
    )(*[_in_hbm(p) for p in parts], *lands)
    return outs[0], outs[1], list(outs[2:2 + n]), list(outs[2 + n:])


def _pair_end(parts, zones, send_sems, recv_sems, after, name):
    n = len(parts)

    def body(*refs):
        ins, zs = refs[:n], refs[n:2 * n]
        s, r = refs[2 * n:2 * n + 2]
        me = _position()
        for j in range(n):
            copy = _remote(ins[j].at[0], zs[j], s.at[j], r.at[j], me)
            copy.wait_send()
            copy.wait_recv()

    outs = pl.pallas_call(
        body, name=name,
        out_shape=(*[pltpu.HBM(p.shape, p.dtype) for p in parts], *[pltpu.HBM(z.shape, z.dtype) for z in zones]),
        in_specs=[IN_HBM] * (2 * n) + [SEMAPHORES, SEMAPHORES, ANY], out_specs=tuple([IN_HBM] * (2 * n)),
        input_output_aliases={j: j for j in range(2 * n)},
        compiler_params=pltpu.CompilerParams(has_side_effects=IN_FLIGHT),
    )(*parts, *zones, send_sems, recv_sems, after)
    return list(outs[:n]), list(outs[n:])


def _pair_sum(parts, from_sibling, core, name):
    n = len(parts)
    steps = 8
    p3 = [p.reshape(2, -1, p.shape[-1]) for p in parts]
    q2 = [q.reshape(-1, q.shape[-1]) for q in from_sibling]

    def body(core_ref, *refs):
        for p_ref, q_ref, o_ref in zip(refs[:n], refs[n:2 * n], refs[2 * n:]):
            o_ref[...] = (p_ref[...].astype(F32) + q_ref[...].astype(F32)).astype(BF16)

    tiles = [pl.BlockSpec((q.shape[0] // steps, q.shape[1]), lambda i, core_ref: (i, 0)) for q in q2]
    grid_spec = pltpu.PrefetchScalarGridSpec(
        num_scalar_prefetch=1, grid=(steps,),
        in_specs=[pl.BlockSpec((None, q.shape[0] // steps, q.shape[1]), lambda i, core_ref: (core_ref[0], i, 0)) for q in q2]
        + tiles, out_specs=tiles)
    outs = pl.pallas_call(body, name=name, grid_spec=grid_spec, out_shape=[jax.ShapeDtypeStruct(q.shape, BF16) for q in q2],
                          compiler_params=_params(("arbitrary",)))(core, *p3, *q2)
    return [o.reshape(q.shape) for o, q in zip(outs, from_sibling)]


def _in_hbm(a):
    return pltpu.with_memory_space_constraint(a, pltpu.HBM)


def _chips(x, y):
    return [(1 - x, y), (x, 1 - y), (1 - x, 1 - y)]


def _to_zone(a, wl, me, dtype, name):
    _, rows, cols = a.shape
    tr = 256 if rows % 256 == 0 else rows

    def body(me_ref, a_ref, o_ref):
        o_ref[...] = a_ref[...].astype(dtype)

    grid_spec = pltpu.PrefetchScalarGridSpec(
        num_scalar_prefetch=1, grid=(rows // tr,),
        in_specs=[pl.BlockSpec((None, tr, cols), lambda i, me_ref: (wl, i, 0))],
        out_specs=pl.BlockSpec((None, tr, cols), lambda i, me_ref: (me_ref[0], i, 0)))
    return pl.pallas_call(body, name=name, grid_spec=grid_spec, out_shape=jax.ShapeDtypeStruct((NDEV, rows, cols), dtype),
                          compiler_params=_params(("arbitrary",)))(me, a)


def _halves(block):
    rows = block.shape[0] // 2
    return block.at[pl.ds(0, rows)], block.at[pl.ds(rows, rows)]


def _around(x, y, c):
    return (x, y, 1 - c), (1 - x, y, c), (x, 1 - y, c), (1 - x, 1 - y, c)


def _gather_step1(zs, send, recv, pos):
    sibling, xn, yn, _ = _around(*pos)
    for j, z in enumerate(zs):
        mine = z.at[_index(pos)]
        for k, peer in enumerate((sibling, xn, yn)):
            _remote(mine, mine, send.at[3 * j + k], recv.at[3 * j + k], peer).start()


def _gather_step2(zs, recv1, send, recv, pos):
    sibling, xn, yn, _ = _around(*pos)
    for j, z in enumerate(zs):
        xb, yb = z.at[_index(xn)], z.at[_index(yn)]
        _remote(xb, xb, send.at[4 * j], recv1.at[3 * j + 1], pos).wait_recv()
        _remote(yb, yb, send.at[4 * j], recv1.at[3 * j + 2], pos).wait_recv()
        _remote(xb, xb, send.at[4 * j], recv.at[4 * j], sibling).start()
        _remote(yb, yb, send.at[4 * j + 1], recv.at[4 * j + 1], sibling).start()
        first, second = _halves(xb)[0], _halves(yb)[1]
        _remote(first, first, send.at[4 * j + 2], recv.at[4 * j + 2], yn).start()
        _remote(second, second, send.at[4 * j + 3], recv.at[4 * j + 3], xn).start()


def _gather_step3(zs, recv2, send, recv, pos):
    sibling, _, _, diagonal = _around(*pos)
    for j, z in enumerate(zs):
        db = z.at[_index(diagonal)]
        first, second = _halves(db)
        _remote(first, first, send.at[j], recv2.at[4 * j + 2], pos).wait_recv()
        _remote(second, second, send.at[j], recv2.at[4 * j + 3], pos).wait_recv()
        _remote(db, db, send.at[j], recv.at[j], sibling).start()


def _gather_step4(zs, send1, recv1, send2, recv2, send3, recv3, pos):
    x, y, c = pos
    sibling = (x, y, 1 - c)
    _, sx, sy, sd = _around(*sibling)
    for j, z in enumerate(zs):
        for owner, send, recv, k in ((sibling, send1, recv1, 3 * j), (sx, send2, recv2, 4 * j), (sy, send2, recv2, 4 * j + 1),
                                     (sd, send3, recv3, j)):
            block = z.at[_index(owner)]
            _remote(block, block, send.at[k], recv.at[k], pos).wait_recv()
    for j, z in enumerate(zs):
        block = z.at[0]
        half = _halves(block)[0]
        for ref, send, recv, k in ([(block, send1, recv1, 3 * j + k) for k in range(3)]
                                   + [(block, send2, recv2, 4 * j), (block, send2, recv2, 4 * j + 1),
                                      (half, send2, recv2, 4 * j + 2), (half, send2, recv2, 4 * j + 3), (block, send3, recv3, j)]):
            _remote(ref, ref, send.at[k], recv.at[k], pos).wait_send()


def _flight_call(step, name, zones, sems_in, nsems_out, after, carried):
    n, m, k = len(zones), len(carried), len(sems_in)

    def body(*refs):
        zs = refs[:n]
        given = refs[n + m:n + m + k]
        made = refs[n + m + k + len(after):n + m + k + len(after) + (2 if nsems_out else 0)]
        step(zs, *given, *made, _position())

    sem_out = (pltpu.SemaphoreType.DMA((nsems_out,)),) * 2 if nsems_out else ()
    outs = pl.pallas_call(
        body, name=name,
        out_shape=(*sem_out, *[pltpu.HBM(z.shape, z.dtype) for z in zones], *[jax.ShapeDtypeStruct(a.shape, a.dtype) for a in carried]),
        in_specs=[IN_HBM] * n + [ANY] * m + [SEMAPHORES] * k + [ANY] * len(after),
        out_specs=(*[SEMAPHORES] * len(sem_out), *[IN_HBM] * n, *[ANY] * m),
        input_output_aliases={j: len(sem_out) + j for j in range(n + m)},
        compiler_params=pltpu.CompilerParams(has_side_effects=IN_FLIGHT),
    )(*[_in_hbm(z) for z in zones], *carried, *sems_in, *after)
    sems = list(outs[:len(sem_out)])
    return sems, list(outs[len(sem_out):len(sem_out) + n]), list(outs[len(sem_out) + n:])


def _gather_start(zones, after, carried, name):
    (send1, recv1), zones, carried = _flight_call(_gather_step1, name, zones, [], 3 * len(zones), after, carried)
    return {"s1": send1, "r1": recv1, "zones": zones}, carried


def _gather_mid(flight, after, carried, name):
    step = lambda zs, recv1, send, recv, pos: _gather_step2(zs, recv1, send, recv, pos)
    (send2, recv2), zones, carried = _flight_call(step, name, flight["zones"], [flight["r1"]], 4 * len(flight["zones"]), after, carried)
    return {**flight, "s2": send2, "r2": recv2, "zones": zones}, carried


def _gather_late(flight, after, name):
    step = lambda zs, recv2, send, recv, pos: _gather_step3(zs, recv2, send, recv, pos)
    (send3, recv3), zones, _ = _flight_call(step, name, flight["zones"], [flight["r2"]], len(flight["zones"]), after, [])
    return {**flight, "s3": send3, "r3": recv3, "zones": zones}


def _gather_end(flight, after, name):
    sems = [flight[k] for k in ("s1", "r1", "s2", "r2", "s3", "r3")]
    _, zones, _ = _flight_call(_gather_step4, name, flight["zones"], sems, 0, after, [])
    return zones


def _scatter_start(parts, name):
    n = len(parts)
    lands = [_in_hbm(lax.empty((3,) + p.shape[1:], p.dtype)) for p in parts]

    def body(*refs):
        ins, zones = refs[:n], refs[n:2 * n]
        send_sems, recv_sems = refs[2 * n:2 * n + 2]
        x, y, c = _position()
        for j in range(n):
            for q, (px, py) in enumerate(_chips(x, y)):
                _remote(ins[j].at[2 * px + py], zones[j].at[q], send_sems.at[3 * j + q], recv_sems.at[3 * j + q],
                        (px, py, c)).start()

    outs = pl.pallas_call(
        body, name=name,
        out_shape=(pltpu.SemaphoreType.DMA((3 * n,)), pltpu.SemaphoreType.DMA((3 * n,)),
                   *[pltpu.HBM(p.shape, p.dtype) for p in parts], *[pltpu.HBM(z.shape, z.dtype) for z in lands]),
        in_specs=[IN_HBM] * (2 * n), out_specs=(SEMAPHORES, SEMAPHORES, *[IN_HBM] * (2 * n)),
        input_output_aliases={j: 2 + j for j in range(2 * n)},
        compiler_params=pltpu.CompilerParams(has_side_effects=IN_FLIGHT),
    )(*[_in_hbm(p) for p in parts], *lands)
    return outs[0], outs[1], list(outs[2:2 + n]), list(outs[2 + n:])


def _scatter_end(parts, zones, send_sems, recv_sems, after, name):
    n = len(parts)

    def body(*refs):
        ins, zs = refs[:n], refs[n:2 * n]
        s, r = refs[2 * n:2 * n + 2]
        me = _position()
        for j in range(n):
            for q in range(3):
                copy = _remote(ins[j].at[0], zs[j].at[q], s.at[3 * j + q], r.at[3 * j + q], me)
                copy.wait_send()
                copy.wait_recv()

    outs = pl.pallas_call(
        body, name=name,
        out_shape=(*[pltpu.HBM(p.shape, p.dtype) for p in parts], *[pltpu.HBM(z.shape, z.dtype) for z in zones]),
        in_specs=[IN_HBM] * (2 * n) + [SEMAPHORES, SEMAPHORES, ANY], out_specs=tuple([IN_HBM] * (2 * n)),
        input_output_aliases={j: j for j in range(2 * n)},
        compiler_params=pltpu.CompilerParams(has_side_effects=IN_FLIGHT),
    )(*parts, *zones, send_sems, recv_sems, after)
    return list(outs[:n]), list(outs[n:])


def _ada_forward(c, ada_w, ada_b, small):
    ns = len(small)

    def body(c_ref, w_ref, b_ref, *rest):
        small_refs, (cact_ref, mod_ref), gathered = rest[:ns], rest[ns:ns + 2], rest[ns + 2:2 * ns + 2]
        gbuf, modrow, send_sems, recv_sems = rest[2 * ns + 2:]
        pos = _position()
        me = _index(pos)

        def to_all(*exchanged):
            copies = []
            for ref, row in exchanged:
                for k in range(1, NDEV):
                    peer, _ = _peer(pos, k)
                    copy = pltpu.make_async_remote_copy(
                        src_ref=ref.at[me], dst_ref=ref.at[me], send_sem=send_sems.at[row, k - 1],
                        recv_sem=recv_sems.at[row, k - 1], device_id=peer, device_id_type=pl.DeviceIdType.MESH)
                    copy.start()
                    copies.append(copy)
            for copy in copies:
                copy.wait()

        cact_ref[me] = _silu(c_ref[...])
        for j in range(ns):
            gathered[j][me] = small_refs[j][...]
        to_all((cact_ref, 0), *[(gathered[j], 2 + j) for j in range(ns)])
        rows = lax.broadcasted_iota(jnp.int32, (NDEV, D), 0)
        cact = jnp.zeros((NDEV, D), F32)
        for e in range(NDEV):
            cact = jnp.where(rows == e, cact_ref[e], cact)
        cact = cact.astype(BF16)
        for l in range(DEPTH):
            gbuf[me, l] = jnp.dot(cact, w_ref[l].astype(BF16), preferred_element_type=F32)
        to_all((gbuf, 1))
        mine = lax.broadcasted_iota(jnp.int32, (NDEV, ADA_NC), 0) == me
        for l in range(DEPTH):
            for d in range(NDEV):
                modrow[:, d * ADA_NC:(d + 1) * ADA_NC] = jnp.sum(jnp.where(mine, gbuf[d, l], 0.0), axis=0, keepdims=True)
            full = modrow[...] + b_ref[l:l + 1, :]
            for w in range(3):
                mod_ref[l, w] = full[:, w * D:(w + 1) * D]

    outs = pl.pallas_call(
        body, name="ada_forward",
        out_shape=[jax.ShapeDtypeStruct((NDEV, 1, D), F32), jax.ShapeDtypeStruct((DEPTH, 3, 1, D), F32)]
        + [jax.ShapeDtypeStruct((NDEV,) + a.shape, a.dtype) for a in small],
        in_specs=[VMEM_FULL] * (3 + ns), out_specs=[VMEM_FULL] * (2 + ns),
        scratch_shapes=[pltpu.VMEM((NDEV, DEPTH, NDEV, ADA_NC), F32), pltpu.VMEM((1, 3 * D), F32),
                        pltpu.SemaphoreType.DMA((2 + ns, NDEV - 1)), pltpu.SemaphoreType.DMA((2 + ns, NDEV - 1))],
        compiler_params=_params(),
    )(c, ada_w, ada_b, *small)
    return outs[0], outs[1], list(outs[2:])


def _mod_spec(layer, which, ngrid):
    index = {1: lambda i: (layer, which, 0, 0), 2: lambda i, j: (layer, which, 0, 0)}[ngrid]
    return pl.BlockSpec((None, None, 1, D), index)


W_BLOCKS = 4


def _norm_proj(x, mod, norm_g3, wg, layer, name):
    nb = wg.shape[-1]
    tm = 1024
    wb = W_BLOCKS

    def body(x_ref, g_ref, shift_ref, scale_ref, w_ref, ht_ref, p_ref, h_ref):
        @pl.when(pl.program_id(1) == 0)
        def _():
            xv = x_ref[...]
            r = lax.rsqrt(jnp.mean(xv * xv, axis=-1, keepdims=True) + EPS)
            hn = xv * r * g_ref[...]
            h = hn * (1.0 + scale_ref[...]) + shift_ref[...]
            h_ref[...] = h.astype(BF16)
            ht_ref[...] = h.T.astype(BF16)

        hv = h_ref[...]
        for b in range(wb):
            p_ref[:, b * nb:(b + 1) * nb] = jnp.dot(hv, w_ref[b], preferred_element_type=F32).astype(BF16)

    return pl.pallas_call(
        body, name=name, grid=(S // tm, NDEV // wb),
        out_shape=[jax.ShapeDtypeStruct((D, S), BF16), jax.ShapeDtypeStruct((S, NDEV * nb), BF16)],
        in_specs=[pl.BlockSpec((tm, D), lambda i, d: (i, 0)),
                  pl.BlockSpec((None, 1, D), lambda i, d: (layer, 0, 0)),
                  _mod_spec(layer, 0, 2), _mod_spec(layer, 1, 2),
                  pl.BlockSpec((wb, D, nb), lambda i, d: (d, 0, 0))],
        out_specs=[pl.BlockSpec((D, tm), lambda i, d: (0, i)), pl.BlockSpec((tm, wb * nb), lambda i, d: (i, d))],
        scratch_shapes=[pltpu.VMEM((tm, D), BF16)],
        compiler_params=_params(("arbitrary", "arbitrary")),
    )(x, norm_g3, mod, mod, wg)


def _out_proj(ycat, w_out, x, mod, layer, name, final=None):
    tm = 512
    e = w_out.shape[0]

    def body(y_ref, w_ref, x_ref, gate_ref, *rest):
        acc = jnp.dot(y_ref[...], w_ref[...], preferred_element_type=F32)
        xv = x_ref[...] + gate_ref[...] * acc
        if final is None:
            xn_ref, o_ref = rest
            o_ref[...] = acc.astype(BF16)
            xn_ref[...] = xv
            return
        t_ref, g_ref, dx_ref, o_ref, loss_ref, dg_ref = rest
        o_ref[...] = acc.astype(BF16)

        @pl.when(pl.program_id(0) == 0)
        def _():
            loss_ref[...] = jnp.zeros_like(loss_ref)
            dg_ref[...] = jnp.zeros_like(dg_ref)

        g = g_ref[...]
        r = lax.rsqrt(jnp.mean(xv * xv, axis=-1, keepdims=True) + EPS)
        xn = xv * r
        err = xn * g - t_ref[...]
        loss_ref[...] += 0.5 * jnp.sum(jnp.mean(err * err, axis=-1, keepdims=True), axis=0, keepdims=True)
        dy = err * (1.0 / D)
        dg_ref[...] += jnp.sum(dy * xn, axis=0, keepdims=True)
        u = dy * g
        dx_ref[...] = r * (u - xn * jnp.mean(xn * u, axis=-1, keepdims=True))

    tile = pl.BlockSpec((tm, D), lambda i: (i, 0))
    row = pl.BlockSpec((1, D), lambda i: (0, 0))
    out_shape = [jax.ShapeDtypeStruct((S, D), F32), jax.ShapeDtypeStruct((S, D), BF16)]
    in_specs = [pl.BlockSpec((tm, e), lambda i: (i, 0)), pl.BlockSpec((e, D), lambda i: (0, 0)), tile, _mod_spec(layer, 2, 1)]
    out_specs, operands = [tile, tile], [ycat, w_out, x, mod]
    if final is not None:
        out_shape += [jax.ShapeDtypeStruct((1, LANE), F32), jax.ShapeDtypeStruct((1, D), F32)]
        in_specs += [tile, row]
        out_specs += [pl.BlockSpec((1, LANE), lambda i: (0, 0)), row]
        operands += list(final)
    return pl.pallas_call(
        body, name=name, grid=(S // tm,), out_shape=out_shape, in_specs=in_specs, out_specs=out_specs,
        compiler_params=_params(("arbitrary",)),
    )(*operands)


def _out_bwd(dx, out, ycat, w_out, mod, layer, carried, name):
    tm = 512
    nsteps = S // tm
    e = ycat.shape[1]
    rb = e // NDEV
    nc = len(carried)

    def body(dx_ref, o_ref, y_ref, w_ref, gate_ref, *rest):
        dy_ref, gw_ref, dgate_ref = rest[nc:nc + 3]
        acc = rest[-1]
        step = pl.program_id(0)

        @pl.when(step == 0)
        def _():
            dgate_ref[...] = jnp.zeros_like(dgate_ref)
            acc[...] = jnp.zeros_like(acc)

        dxv = dx_ref[...]
        d_out = (gate_ref[...] * dxv).astype(BF16)
        dgate_ref[...] += jnp.sum(dxv * o_ref[...].astype(F32), axis=0, keepdims=True)
        dy_ref[...] = lax.dot_general(d_out, w_ref[...], (((1,), (1,)), ((), ())), preferred_element_type=F32).astype(BF16)
        acc[...] += lax.dot_general(y_ref[...], d_out, (((0,), (0,)), ((), ())), preferred_element_type=F32)

        @pl.when(step == nsteps - 1)
        def _():
            for d in range(NDEV):
                gw_ref[d % 2, d // 2] = acc[d * rb:(d + 1) * rb, :].astype(BF16)

    tile = pl.BlockSpec((tm, D), lambda i: (i, 0))
    wide = pl.BlockSpec((tm, e), lambda i: (i, 0))
    outs = pl.pallas_call(
        body, name=name, grid=(nsteps,),
        out_shape=[jax.ShapeDtypeStruct((S, e), BF16), jax.ShapeDtypeStruct((2, NDEV // 2, rb, D), BF16),
                   jax.ShapeDtypeStruct((1, D), F32)] + [jax.ShapeDtypeStruct(a.shape, a.dtype) for a in carried],
        in_specs=[tile, tile, wide, pl.BlockSpec((e, D), lambda i: (0, 0)), _mod_spec(layer, 2, 1)] + [ANY] * nc,
        out_specs=[wide, pl.BlockSpec((2, NDEV // 2, rb, D), lambda i: (0, 0, 0, 0)), pl.BlockSpec((1, D), lambda i: (0, 0))]
        + [ANY] * nc,
        scratch_shapes=[pltpu.VMEM((e, D), F32)],
        input_output_aliases={5 + k: 3 + k for k in range(nc)},
        compiler_params=_params(("arbitrary",)),
    )(dx, out, ycat, w_out, mod, *carried)
    return outs[0], outs[1], outs[2], list(outs[3:])


def _weight_grad(h_t, d_proj, name):
    nb = d_proj.shape[1] // NDEV

    def body(ht_ref, dp_ref, o_ref):
        o_ref[...] = jnp.dot(ht_ref[...], dp_ref[...], preferred_element_type=F32).astype(BF16)

    return pl.pallas_call(
        body, name=name, grid=(NDEV,), out_shape=jax.ShapeDtypeStruct((2, NDEV // 2, D, nb), BF16),
        in_specs=[pl.BlockSpec((D, S), lambda d: (0, 0)), pl.BlockSpec((S, nb), lambda d: (0, d))],
        out_specs=pl.BlockSpec((None, None, D, nb), lambda d: (d % 2, d // 2, 0, 0)),
        compiler_params=_params(("arbitrary",)),
    )(h_t, d_proj)


def _dh_norm_bwd(d_proj, wg, x, dx, mod, norm_g3, layer, carried, name):
    nb = wg.shape[-1]
    tm = 512
    wb = W_BLOCKS
    rc = 128

    def body(dp_ref, w_ref, x_ref, dx_ref, g_ref, scale_ref, carried_ref,
             dxi_ref, dshift_ref, dscale_ref, dg_ref, carried_out, acc):
        i, d = pl.program_id(0), pl.program_id(1)
        nt = (((1,), (1,)), ((), ()))
        part = lax.dot_general(dp_ref[:, :nb], w_ref[0], nt, preferred_element_type=F32)
        for b in range(1, wb):
            part += lax.dot_general(dp_ref[:, b * nb:(b + 1) * nb], w_ref[b], nt, preferred_element_type=F32)

        @pl.when(d == 0)
        def _():
            acc[...] = part

        @pl.when(d != 0)
        def _():
            acc[...] += part

        @pl.when(jnp.logical_and(i == 0, d == 0))
        def _():
            dshift_ref[...] = jnp.zeros_like(dshift_ref)
            dscale_ref[...] = jnp.zeros_like(dscale_ref)
            dg_ref[...] = jnp.zeros_like(dg_ref)

        @pl.when(d == NDEV // wb - 1)
        def _():
            g = g_ref[...]
            scale1 = 1.0 + scale_ref[...]

            def chunk(k, sums):
                rows = pl.ds(pl.multiple_of(k * rc, rc), rc)
                xv, dhv = x_ref[rows, :], acc[rows, :]
                r = lax.rsqrt(jnp.mean(xv * xv, axis=-1, keepdims=True) + EPS)
                xn = xv * r
                dhn = dhv * scale1
                u = dhn * g
                dxi_ref[rows, :] = dx_ref[rows, :] + r * (u - xn * jnp.mean(xn * u, axis=-1, keepdims=True))
                return (sums[0] + jnp.sum(dhv, axis=0, keepdims=True),
                        sums[1] + jnp.sum(dhv * (xn * g), axis=0, keepdims=True),
                        sums[2] + jnp.sum(dhn * xn, axis=0, keepdims=True))

            zero = jnp.zeros((1, D), F32)
            sums = lax.fori_loop(0, tm // rc, chunk, (zero, zero, zero))
            dshift_ref[...] += sums[0]
            dscale_ref[...] += sums[1]
            dg_ref[...] += sums[2]

    tile = pl.BlockSpec((tm, D), lambda i, d: (i, 0))
    row = pl.BlockSpec((1, D), lambda i, d: (0, 0))
    return pl.pallas_call(
        body, name=name, grid=(S // tm, NDEV // wb),
        out_shape=[jax.ShapeDtypeStruct((S, D), F32)] + [jax.ShapeDtypeStruct((1, D), F32)] * 3
        + [jax.ShapeDtypeStruct(carried.shape, carried.dtype)],
        in_specs=[pl.BlockSpec((tm, wb * nb), lambda i, d: (i, d)), pl.BlockSpec((wb, D, nb), lambda i, d: (d, 0, 0)),
                  tile, tile, pl.BlockSpec((None, 1, D), lambda i, d: (layer, 0, 0)), _mod_spec(layer, 1, 2), ANY],
        out_specs=[tile, row, row, row, ANY], scratch_shapes=[pltpu.VMEM((tm, D), F32)],
        input_output_aliases={6: 4}, compiler_params=_params(("arbitrary", "arbitrary")),
    )(d_proj, wg, x, dx, norm_g3, mod, carried)


TS = 256
NCH = TS // CHUNK
HALO_BLOCKS = TS // HALO


def _halo_before(width, col_block):
    return pl.BlockSpec((HALO, width), lambda i: (jnp.maximum(i * HALO_BLOCKS - 1, 0), col_block))


def _halo_after(width, col_block):
    return pl.BlockSpec((HALO, width), lambda i: (jnp.minimum((i + 1) * HALO_BLOCKS, S // HALO - 1), col_block))


def _shift_down(ext, k):
    return pltpu.roll(ext, k, 0)[HALO:]


def _shift_up(ext, k):
    return pltpu.roll(ext, ext.shape[0] - k, 0)[:ext.shape[0] - HALO]


def _layer_norm_head(v, lg, lb):
    mu = jnp.mean(v, axis=-1, keepdims=True)
    vc = v - mu
    rstd = lax.rsqrt(jnp.mean(vc * vc, axis=-1, keepdims=True) + EPS)
    vhat = vc * rstd
    return vhat, rstd, vhat * lg + lb


def _causal_mask():
    return lax.broadcasted_iota(jnp.int32, (CHUNK, CHUNK), 0) >= lax.broadcasted_iota(jnp.int32, (CHUNK, CHUNK), 1)


def _even_mix_fwd(proj, convw, ln_g3, ln_b3, sgu_w, sgu_bcol, wl, after, name):
    def body(pj_ref, hh_ref, hc_ref, cw_ref, lg_ref, lb_ref, sw_ref, sb_ref, *rest):
        y_ref = rest[-1]
        live = (pl.program_id(0) > 0).astype(F32)
        causal = _causal_mask()
        for j in range(E_A // HEAD):
            cols = slice(j * HEAD, (j + 1) * HEAD)
            w0, w1, w2 = cw_ref[0:1, cols], cw_ref[1:2, cols], cw_ref[2:3, cols]
            lg, lb = lg_ref[:, cols], lb_ref[:, cols]
            wm = jnp.where(causal, sw_ref[j], 0.0).astype(BF16)
            bias = sb_ref[j]

            def split(s, rows, cols=cols):
                return pj_ref[rows, s * E_A + cols.start:s * E_A + cols.stop].astype(F32)

            prev_tail = hc_ref[:, cols].astype(F32) * hh_ref[:, cols].astype(F32) * live
            for n in range(NCH):
                rows = slice(n * CHUNK, (n + 1) * CHUNK)
                p = split(2, rows) * split(0, rows)
                ext = jnp.concatenate([prev_tail, p], axis=0)
                prev_tail = p[CHUNK - HALO:]
                cv = w2 * p + w1 * _shift_down(ext, 1) + w0 * _shift_down(ext, 2)
                y_ref[rows, cols] = (split(1, rows) * cv * _silu(split(3, rows))).astype(BF16)
                _, _, vn = _layer_norm_head(split(5, rows), lg, lb)
                mixed = jnp.dot(wm, vn.astype(BF16), preferred_element_type=F32) + bias
                y_ref[rows, E_A + cols.start:E_A + cols.stop] = (split(4, rows) * mixed * _silu(split(6, rows))).astype(BF16)

    const3 = lambda i: (wl, 0, 0)
    const4 = lambda i: (wl, 0, 0, 0)
    return pl.pallas_call(
        body, name=name, grid=(S // TS,), out_shape=jax.ShapeDtypeStruct((S, 2 * E_A), BF16),
        in_specs=[pl.BlockSpec((TS, 7 * E_A), lambda i: (i, 0)), _halo_before(E_A, 0), _halo_before(E_A, 2),
                  pl.BlockSpec((None, 3, E_A), const3), pl.BlockSpec((None, 1, E_A), const3),
                  pl.BlockSpec((None, 1, E_A), const3), pl.BlockSpec((None, NDEV, CHUNK, CHUNK), const4),
                  pl.BlockSpec((None, NDEV, CHUNK, 1), const4)] + [ANY] * len(after),
        out_specs=pl.BlockSpec((TS, 2 * E_A), lambda i: (i, 0)),
        compiler_params=_params(("arbitrary",)),
    )(proj, proj, proj, convw, ln_g3, ln_b3, sgu_w, sgu_bcol, *after)


def _even_mix_bwd(proj, d_ycat, convw, ln_g3, ln_b3, sgu_w, sgu_bcol, wl, name):
    nsteps = S // TS

    def body(pj_ref, hh_ref, hc_ref, hb_ref, hz_ref, dy_ref, hdy_ref, cw_ref, lg_ref, lb_ref, sw_ref, sb_ref,
             dp_ref, dcw_ref, dlg_ref, dlb_ref, dsw_ref, dsb_ref):
        step = pl.program_id(0)

        @pl.when(step == 0)
        def _():
            for ref in (dcw_ref, dlg_ref, dlb_ref, dsw_ref, dsb_ref):
                ref[...] = jnp.zeros_like(ref)

        live_before = (step > 0).astype(F32)
        live_after = (step < nsteps - 1).astype(F32)
        causal = _causal_mask()
        for j in range(E_A // HEAD):
            cols = slice(j * HEAD, (j + 1) * HEAD)
            w0, w1, w2 = cw_ref[0:1, cols], cw_ref[1:2, cols], cw_ref[2:3, cols]
            lg, lb = lg_ref[:, cols], lb_ref[:, cols]
            wmf = jnp.where(causal, sw_ref[j], 0.0)
            wm, wmt = wmf.astype(BF16), wmf.T.astype(BF16)
            bias = sb_ref[j]

            def split(s, rows, cols=cols):
                return pj_ref[rows, s * E_A + cols.start:s * E_A + cols.stop].astype(F32)

            def put(s, rows, val, cols=cols):
                dp_ref[rows, s * E_A + cols.start:s * E_A + cols.stop] = val.astype(BF16)

            ps = [split(2, slice(n * CHUNK, (n + 1) * CHUNK)) * split(0, slice(n * CHUNK, (n + 1) * CHUNK)) for n in range(NCH)]
            next_head = (hdy_ref[:, cols].astype(F32) * hb_ref[:, cols].astype(F32) * _silu(hz_ref[:, cols].astype(F32))
                         * live_after)
            acc_w = [jnp.zeros((1, HEAD), F32) for _ in range(3)]
            for n in reversed(range(NCH)):
                rows = slice(n * CHUNK, (n + 1) * CHUNK)
                p = ps[n]
                tail = ps[n - 1][CHUNK - HALO:] if n > 0 else hc_ref[:, cols].astype(F32) * hh_ref[:, cols].astype(F32) * live_before
                ext = jnp.concatenate([tail, p], axis=0)
                p1, p2 = _shift_down(ext, 1), _shift_down(ext, 2)
                cv = w2 * p + w1 * p1 + w0 * p2
                a_b, a_z = split(1, rows), split(3, rows)
                sz, dsz = _silu_and_grad(a_z)
                dya = dy_ref[rows, cols].astype(F32)
                put(1, rows, dya * cv * sz)
                put(3, rows, dya * a_b * cv * dsz)
                gcv = dya * a_b * sz
                acc_w[0] += jnp.sum(gcv * p2, axis=0, keepdims=True)
                acc_w[1] += jnp.sum(gcv * p1, axis=0, keepdims=True)
                acc_w[2] += jnp.sum(gcv * p, axis=0, keepdims=True)
                gext = jnp.concatenate([gcv, next_head], axis=0)
                next_head = gcv[:HALO]
                dpv = w2 * gcv + w1 * _shift_up(gext, 1) + w0 * _shift_up(gext, 2)
                put(2, rows, dpv * split(0, rows))
                put(0, rows, dpv * split(2, rows))
            for k in range(3):
                dcw_ref[k:k + 1, cols] += acc_w[k]

            acc_lg, acc_lb = jnp.zeros((1, HEAD), F32), jnp.zeros((1, HEAD), F32)
            acc_sw, acc_sb = jnp.zeros((CHUNK, CHUNK), F32), jnp.zeros((CHUNK, 1), F32)
            for n in range(NCH):
                rows = slice(n * CHUNK, (n + 1) * CHUNK)
                u, z = split(4, rows), split(6, rows)
                vhat, rstd, vn = _layer_norm_head(split(5, rows), lg, lb)
                vn16 = vn.astype(BF16)
                mixed = jnp.dot(wm, vn16, preferred_element_type=F32) + bias
                sz, dsz = _silu_and_grad(z)
                dyb = dy_ref[rows, E_A + cols.start:E_A + cols.stop].astype(F32)
                put(4, rows, dyb * mixed * sz)
                put(6, rows, dyb * u * mixed * dsz)
                dmix = dyb * u * sz
                dmix16 = dmix.astype(BF16)
                acc_sb += jnp.sum(dmix, axis=1, keepdims=True)
                acc_sw += lax.dot_general(dmix16, vn16, (((1,), (1,)), ((), ())), preferred_element_type=F32)
                dvn = jnp.dot(wmt, dmix16, preferred_element_type=F32)
                acc_lg += jnp.sum(dvn * vhat, axis=0, keepdims=True)
                acc_lb += jnp.sum(dvn, axis=0, keepdims=True)
                dvh = dvn * lg
                put(5, rows, rstd * (dvh - jnp.mean(dvh, axis=-1, keepdims=True)
                                     - vhat * jnp.mean(dvh * vhat, axis=-1, keepdims=True)))
            dlg_ref[:, cols] += acc_lg
            dlb_ref[:, cols] += acc_lb
            dsw_ref[j] += jnp.where(causal, acc_sw, 0.0)
            dsb_ref[j] += acc_sb

    const3 = lambda i: (wl, 0, 0)
    const4 = lambda i: (wl, 0, 0, 0)
    fixed2 = lambda i: (0, 0)
    fixed3 = lambda i: (0, 0, 0)
    return pl.pallas_call(
        body, name=name, grid=(nsteps,),
        out_shape=[jax.ShapeDtypeStruct((S, 7 * E_A), BF16), jax.ShapeDtypeStruct((3, E_A), F32),
                   jax.ShapeDtypeStruct((1, E_A), F32), jax.ShapeDtypeStruct((1, E_A), F32),
                   jax.ShapeDtypeStruct((NDEV, CHUNK, CHUNK), F32), jax.ShapeDtypeStruct((NDEV, CHUNK, 1), F32)],
        in_specs=[pl.BlockSpec((TS, 7 * E_A), lambda i: (i, 0)), _halo_before(E_A, 0), _halo_before(E_A, 2),
                  _halo_after(E_A, 1), _halo_after(E_A, 3),
                  pl.BlockSpec((TS, 2 * E_A), lambda i: (i, 0)), _halo_after(E_A, 0),
                  pl.BlockSpec((None, 3, E_A), const3), pl.BlockSpec((None, 1, E_A), const3),
                  pl.BlockSpec((None, 1, E_A), const3), pl.BlockSpec((None, NDEV, CHUNK, CHUNK), const4),
                  pl.BlockSpec((None, NDEV, CHUNK, 1), const4)],
        out_specs=[pl.BlockSpec((TS, 7 * E_A), lambda i: (i, 0)), pl.BlockSpec((3, E_A), fixed2),
                   pl.BlockSpec((1, E_A), fixed2), pl.BlockSpec((1, E_A), fixed2),
                   pl.BlockSpec((NDEV, CHUNK, CHUNK), fixed3), pl.BlockSpec((NDEV, CHUNK, 1), fixed3)],
        compiler_params=_params(("arbitrary",)),
    )(proj, proj, proj, proj, proj, d_ycat, d_ycat, convw, ln_g3, ln_b3, sgu_w, sgu_bcol)


def _window_count(step, n, win, ext_before):
    rows = CHUNK if ext_before else CHUNK + HALO
    t = step * TS + n * CHUNK + lax.broadcasted_iota(jnp.int32, (rows, 1), 0)
    return jnp.minimum(t + 1, win).astype(F32)


def _pool_weight(wp_ref, g):
    return jnp.concatenate([wp_ref[d, g] for d in range(NDEV)], axis=0)


def _pooled_chunk(p, tail, win, count):
    sums = jnp.concatenate([tail, p], axis=0)
    shift = 1
    while shift < win:
        sums = sums + pltpu.roll(sums, shift, 0)
        shift *= 2
    return sums[HALO:] / count - p


def _pool_mix_fwd(proj, wpool, pscale4, wl, after, name):
    e_c = 4 * G_C

    def body(pj_ref, hp_ref, wp_ref, ps_ref, *rest):
        y_ref, pooled_scr, yraw_scr = rest[-3:]
        step = pl.program_id(0)
        live = (step > 0).astype(F32)
        for g, win in enumerate(POOL_WINDOWS):
            for q in range(G_C // LANE):
                cols = slice(g * G_C + q * LANE, g * G_C + (q + 1) * LANE)
                tail = hp_ref[:, cols].astype(F32) * live
                for n in range(NCH):
                    rows = slice(n * CHUNK, (n + 1) * CHUNK)
                    p = pj_ref[rows, cols].astype(F32)
                    pooled_scr[rows, q * LANE:(q + 1) * LANE] = _pooled_chunk(
                        p, tail, win, _window_count(step, n, win, True)).astype(BF16)
                    tail = p[CHUNK - HALO:]
            yraw_scr[...] = jnp.dot(pooled_scr[...], _pool_weight(wp_ref, g), preferred_element_type=F32)
            for q in range(G_C // LANE):
                cols = slice(g * G_C + q * LANE, g * G_C + (q + 1) * LANE)
                for n in range(NCH):
                    rows = slice(n * CHUNK, (n + 1) * CHUNK)
                    z = pj_ref[rows, e_c + cols.start:e_c + cols.stop].astype(F32)
                    y_ref[rows, cols] = (yraw_scr[rows, q * LANE:(q + 1) * LANE] * ps_ref[:, cols] * _silu(z)).astype(BF16)

    return pl.pallas_call(
        body, name=name, grid=(S // TS,), out_shape=jax.ShapeDtypeStruct((S, e_c), BF16),
        in_specs=[pl.BlockSpec((TS, 2 * e_c), lambda i: (i, 0)), _halo_before(e_c, 0),
                  pl.BlockSpec((NDEV, 4, G_C // NDEV, G_C), lambda i: (0, 0, 0, 0)),
                  pl.BlockSpec((None, 1, e_c), lambda i: (wl, 0, 0))] + [ANY] * len(after),
        out_specs=pl.BlockSpec((TS, e_c), lambda i: (i, 0)),
        scratch_shapes=[pltpu.VMEM((TS, G_C), BF16), pltpu.VMEM((TS, G_C), F32)],
        compiler_params=_params(("arbitrary",)),
    )(proj, proj, wpool, pscale4, *after)


def _pool_mix_bwd(proj, d_ycat, wpool, pscale4, wl, name):
    e_c = 4 * G_C
    nsteps = S // TS
    rb = G_C // NDEV

    def body(pj_ref, hp_ref, hz_ref, dy_ref, hdy_ref, wp_ref, ps_ref,
             dp_ref, dps_ref, dwp_ref, pooled_scr, yraw_scr, dyraw_scr, dpool_scr, acc_w):
        step = pl.program_id(0)

        @pl.when(step == 0)
        def _():
            dps_ref[...] = jnp.zeros_like(dps_ref)
            acc_w[...] = jnp.zeros_like(acc_w)

        live_before = (step > 0).astype(F32)
        live_after = (step < nsteps - 1).astype(F32)
        for g, win in enumerate(POOL_WINDOWS):
            weight = _pool_weight(wp_ref, g)
            for q in range(G_C // LANE):
                cols = slice(g * G_C + q * LANE, g * G_C + (q + 1) * LANE)
                tail = hp_ref[:, cols].astype(F32) * live_before
                for n in range(NCH):
                    rows = slice(n * CHUNK, (n + 1) * CHUNK)
                    p = pj_ref[rows, cols].astype(F32)
                    pooled_scr[rows, q * LANE:(q + 1) * LANE] = _pooled_chunk(
                        p, tail, win, _window_count(step, n, win, True)).astype(BF16)
                    tail = p[CHUNK - HALO:]
            yraw_scr[...] = jnp.dot(pooled_scr[...], weight, preferred_element_type=F32)
            for q in range(G_C // LANE):
                cols = slice(g * G_C + q * LANE, g * G_C + (q + 1) * LANE)
                local = slice(q * LANE, (q + 1) * LANE)
                scale = ps_ref[:, cols]
                acc_ps = jnp.zeros((1, LANE), F32)
                for n in range(NCH):
                    rows = slice(n * CHUNK, (n + 1) * CHUNK)
                    sz, dsz = _silu_and_grad(pj_ref[rows, e_c + cols.start:e_c + cols.stop].astype(F32))
                    dyv = dy_ref[rows, cols].astype(F32)
                    yraw = yraw_scr[rows, local]
                    dyraw_scr[rows, local] = (dyv * scale * sz).astype(BF16)
                    acc_ps += jnp.sum(dyv * yraw * sz, axis=0, keepdims=True)
                    dp_ref[rows, e_c + cols.start:e_c + cols.stop] = (dyv * yraw * scale * dsz).astype(BF16)
                dps_ref[:, cols] += acc_ps
                dyraw_scr[TS:, local] = (hdy_ref[:, cols].astype(F32) * scale * _silu(hz_ref[:, cols].astype(F32))
                                         * live_after).astype(BF16)
            dpool_scr[...] = lax.dot_general(dyraw_scr[...], weight, (((1,), (1,)), ((), ())), preferred_element_type=F32)
            acc_w[g] += lax.dot_general(pooled_scr[...], dyraw_scr[:TS, :], (((0,), (0,)), ((), ())),
                                        preferred_element_type=F32)
            for q in range(G_C // LANE):
                cols = slice(g * G_C + q * LANE, g * G_C + (q + 1) * LANE)
                local = slice(q * LANE, (q + 1) * LANE)
                for n in range(NCH):
                    rows = slice(n * CHUNK, (n + 1) * CHUNK)
                    ext = dpool_scr[n * CHUNK:(n + 1) * CHUNK + HALO, local]
                    sums = ext / _window_count(step, n, win, False)
                    shift = 1
                    while shift < win:
                        sums = sums + pltpu.roll(sums, CHUNK + HALO - shift, 0)
                        shift *= 2
                    dp_ref[rows, cols] = (sums[:CHUNK] - ext[:CHUNK]).astype(BF16)

        @pl.when(step == nsteps - 1)
        def _():
            for g in range(4):
                for d in range(NDEV):
                    dwp_ref[d % 2, d // 2, g] = acc_w[g, d * rb:(d + 1) * rb, :].astype(BF16)

    in_specs = [pl.BlockSpec((TS, 2 * e_c), lambda i: (i, 0)), _halo_before(e_c, 0), _halo_after(e_c, 1),
                pl.BlockSpec((TS, e_c), lambda i: (i, 0)), _halo_after(e_c, 0),
                pl.BlockSpec((NDEV, 4, rb, G_C), lambda i: (0, 0, 0, 0)),
                pl.BlockSpec((None, 1, e_c), lambda i: (wl, 0, 0))]
    args = [proj, proj, proj, d_ycat, d_ycat, wpool, pscale4]
    return pl.pallas_call(
        body, name=name, grid=(nsteps,),
        out_shape=[jax.ShapeDtypeStruct((S, 2 * e_c), BF16), jax.ShapeDtypeStruct((1, e_c), F32),
                   jax.ShapeDtypeStruct((2, NDEV // 2) + wpool.shape[1:], BF16)],
        in_specs=in_specs,
        out_specs=[pl.BlockSpec((TS, 2 * e_c), lambda i: (i, 0)), pl.BlockSpec((1, e_c), lambda i: (0, 0)),
                   pl.BlockSpec((2, NDEV // 2, 4, rb, G_C), lambda i: (0, 0, 0, 0, 0))],
        scratch_shapes=[pltpu.VMEM((TS, G_C), BF16), pltpu.VMEM((TS, G_C), F32), pltpu.VMEM((TS + HALO, G_C), BF16),
                        pltpu.VMEM((TS + HALO, G_C), F32), pltpu.VMEM((4, G_C, G_C), F32)],
        compiler_params=_params(("arbitrary",)),
    )(*args)


def _adamw(w, g, m, v):
    m = ADAM_B1 * m + (1.0 - ADAM_B1) * g
    v = ADAM_B2 * v + (1.0 - ADAM_B2) * jnp.square(g)
    m_hat = m / (1.0 - ADAM_B1 ** ADAM_STEP)
    v_hat = v / (1.0 - ADAM_B2 ** ADAM_STEP)
    delta = -ADAM_LR * (m_hat / (jnp.sqrt(v_hat) + ADAM_EPS) + ADAM_WD * w)
    return delta, m, v


def _adam_sharded(w, m, v, chip_parts, landed, my_chip, carried, name, first=0, into=()):
    _, nr, ncol = w.shape
    nl = len(chip_parts)
    tr = 128
    steps = nr // tr
    nc, ni = len(carried), len(into)

    def body(chip_ref, w_ref, m_ref, v_ref, *rest):
        parts, zones = rest[:nl], rest[nl:2 * nl]
        g_ref, d_ref, nm_ref, nv_ref = rest[2 * nl + nc + ni:2 * nl + nc + ni + 4]
        layer = pl.program_id(0)
        g = jnp.zeros((tr, ncol), F32)
        for l in range(nl):
            gl = parts[l][...].astype(F32)
            for q in range(3):
                gl = gl + zones[l][q].astype(F32)
            g = jnp.where(layer == l, gl, g)
        g_ref[...] = g
        d_ref[...], nm_ref[...], nv_ref[...] = _adamw(w_ref[...], g, m_ref[...], v_ref[...])

    def rows_of(l):
        return lambda layer, i, chip_ref: jnp.where(layer == l, i, jnp.where(layer < l, 0, steps - 1))

    spec = pl.BlockSpec((None, tr, ncol), lambda layer, i, chip_ref: (first + layer, i, 0))
    part_specs = [pl.BlockSpec((None, tr, ncol), lambda layer, i, chip_ref, l=l: (chip_ref[0], rows_of(l)(layer, i, chip_ref), 0))
                  for l in range(nl)]
    zone_specs = [pl.BlockSpec((3, tr, ncol), lambda layer, i, chip_ref, l=l: (0, rows_of(l)(layer, i, chip_ref), 0))
                  for l in range(nl)]
    grid_spec = pltpu.PrefetchScalarGridSpec(
        num_scalar_prefetch=1, grid=(nl, steps),
        in_specs=[spec, spec, spec] + part_specs + zone_specs + [ANY] * (nc + ni), out_specs=[spec] * 4 + [ANY] * nc)
    aliases = {4 + 2 * nl + k: 4 + k for k in range(nc)}
    aliases.update({4 + 2 * nl + nc + k: k for k in range(ni)})
    return pl.pallas_call(
        body, name=name, grid_spec=grid_spec,
        out_shape=[jax.ShapeDtypeStruct(w.shape, F32)] * 4 + [jax.ShapeDtypeStruct(a.shape, a.dtype) for a in carried],
        input_output_aliases=aliases, compiler_params=_params(("arbitrary", "arbitrary")),
    )(my_chip, w, m, v, *chip_parts, *landed, *carried, *into)


def _adam_small(params, name):
    n = len(params)

    def body(*refs):
        ins, outs = refs[:4 * n], refs[4 * n:]
        for k in range(n):
            w_ref, g_ref, m_ref, v_ref = ins[4 * k:4 * k + 4]
            outs[3 * k][...], outs[3 * k + 1][...], outs[3 * k + 2][...] = _adamw(w_ref[...], g_ref[...], m_ref[...], v_ref[...])

    outs = pl.pallas_call(body, name=name, out_shape=[jax.ShapeDtypeStruct(p[0].shape, F32) for p in params for _ in range(3)],
                          in_specs=[VMEM_FULL] * (4 * n), out_specs=[VMEM_FULL] * (3 * n),
                          compiler_params=_params())(*[a for p in params for a in p])
    return [list(outs[3 * k:3 * k + 3]) for k in range(n)]


def _sum_devices(gathered, name):
    _, nr, ncol = gathered.shape

    def body(g_ref, o_ref):
        acc = g_ref[0].astype(F32)
        for s in range(1, NDEV):
            acc = acc + g_ref[s].astype(F32)
        o_ref[...] = acc

    return pl.pallas_call(body, name=name, grid=(1,), out_shape=jax.ShapeDtypeStruct((nr, ncol), F32),
                          in_specs=[pl.BlockSpec((NDEV, nr, ncol), lambda i: (0, 0, 0))],
                          out_specs=pl.BlockSpec((nr, ncol), lambda i: (0, 0)),
                          compiler_params=_params(("arbitrary",)))(gathered)


def _ada_weight_adam(cact_t, dmod_mine, w, m, v):
    def body(ct_ref, dm_ref, w_ref, m_ref, v_ref, g_ref, d_ref, nm_ref, nv_ref):
        ct, dm = ct_ref[...], dm_ref[...]
        g = ct[:, 0:1] * dm[0:1, :]
        for e in range(1, NDEV):
            g = g + ct[:, e:e + 1] * dm[e:e + 1, :]
        g_ref[...] = g
        d_ref[...], nm_ref[...], nv_ref[...] = _adamw(w_ref[...], g, m_ref[...], v_ref[...])

    spec = pl.BlockSpec((None, D, ADA_NC), lambda l: (l, 0, 0))
    return pl.pallas_call(
        body, name="ada_weight_adam", grid=(DEPTH,), out_shape=[jax.ShapeDtypeStruct(w.shape, F32)] * 4,
        in_specs=[pl.BlockSpec((D, NDEV), lambda l: (0, 0)), pl.BlockSpec((None, NDEV, ADA_NC), lambda l: (l, 0, 0)),
                  spec, spec, spec],
        out_specs=[spec] * 4, compiler_params=_params(("arbitrary",)),
    )(cact_t, dmod_mine, w, m, v)


def _pad_rows(a, rows):
    a = a.reshape(-1, D)
    return jnp.pad(a, ((0, rows - a.shape[0]), (0, 0)))


def kernel(x, c, norm_g, ada_w, ada_b, ab_w_in, ab_conv_w, ab_ln_g, ab_ln_b, ab_sgu_w, ab_sgu_b, ab_w_out, c_w_in, c_pool_w, c_pool_scale, c_w_out, final_g, loss_target, m_norm_g, m_ada_w, m_ada_b, m_ab_w_in, m_ab_conv_w, m_ab_ln_g, m_ab_ln_b, m_ab_sgu_w, m_ab_sgu_b, m_ab_w_out, m_c_w_in, m_c_pool_w, m_c_pool_scale, m_c_w_out, m_final_g, v_norm_g, v_ada_w, v_ada_b, v_ab_w_in, v_ab_conv_w, v_ab_ln_g, v_ab_ln_b, v_ab_sgu_w, v_ab_sgu_b, v_ab_w_out, v_c_w_in, v_c_pool_w, v_c_pool_scale, v_c_w_out, v_final_g):
    x_pos, y_pos, c_pos = _position()
    me = _index((x_pos, y_pos, c_pos))
    core = c_pos.astype(jnp.int32).reshape(1)
    my_chip = (2 * x_pos + y_pos).astype(jnp.int32).reshape(1)
    me1 = me.astype(jnp.int32).reshape(1)
    x0 = x.reshape(S, D)
    target = loss_target.reshape(S, D)
    norm_g3 = norm_g.reshape(DEPTH, 1, D)
    ln_g3, ln_b3 = ab_ln_g.reshape(2, 1, E_A), ab_ln_b.reshape(2, 1, E_A)
    sgu_bcol = ab_sgu_b.reshape(2, NDEV, CHUNK, 1)
    rb = G_C // NDEV
    pool_w3, m_pool_w3, v_pool_w3 = (a.reshape(2, 4 * rb, G_C) for a in (c_pool_w, m_c_pool_w, v_c_pool_w))

    cact_all, mod, (convw_all, pscale_all) = _ada_forward(c, ada_w, ada_b, [ab_conv_w, c_pool_scale])
    convw = jnp.transpose(convw_all, (1, 2, 0, 3)).reshape(2, 3, E_A)
    pscale4 = jnp.transpose(pscale_all, (1, 0, 2)).reshape(2, 1, 4 * G_C)
    zones = []
    for layer in range(DEPTH):
        wl = layer // 2
        if layer % 2 == 0:
            zones.append([_to_zone(ab_w_in, wl, me1, BF16, f"cast_w_in_{layer}"), _to_zone(ab_w_out, wl, me1, BF16, f"cast_w_out_{layer}")])
        else:
            zones.append([_to_zone(c_w_in, wl, me1, BF16, f"cast_w_in_{layer}"), _to_zone(c_w_out, wl, me1, BF16, f"cast_w_out_{layer}"),
                          _to_zone(pool_w3, wl, me1, BF16, f"cast_pool_w_{layer}")])

    def gathered(flight, after, layer):
        wg = _gather_end(flight, [after], f"gather_end_{layer}")
        return [wg[0], wg[1].reshape(-1, D)] + [w.reshape(NDEV, 4, rb, G_C) for w in wg[2:]]

    flight, (mod,) = _gather_start(zones[0], [convw_all], [mod], "gather_start_0")
    flight, (mod,) = _gather_mid(flight, [z for zs in zones[1:] for z in zs], [mod], "gather_mid_0")
    next_flight, (mod,) = _gather_start(zones[1], [], [mod], "gather_start_1")
    flight = _gather_late(flight, [mod], "gather_late_0")
    xs, hts, projs, ycats, outs, gathered_w = [x0], [], [], [], [], [gathered(flight, mod, 0)]
    for layer in range(DEPTH):
        wl = layer // 2
        even = layer % 2 == 0
        wg = gathered_w[layer]
        h_t, proj = _norm_proj(xs[-1], mod, norm_g3, wg[0], layer, f"norm_proj_{layer}")
        if layer + 1 < DEPTH:
            flight, (h_t,) = _gather_mid(next_flight, [], [h_t], f"gather_mid_{layer + 1}")
            if layer + 2 < DEPTH:
                next_flight, (h_t,) = _gather_start(zones[layer + 2], [], [h_t], f"gather_start_{layer + 2}")
        if even:
            ycat = _even_mix_fwd(proj, convw, ln_g3, ln_b3, ab_sgu_w, sgu_bcol, wl, [h_t], f"even_mix_fwd_{layer}")
        else:
            ycat = _pool_mix_fwd(proj, wg[2], pscale4, wl, [h_t], f"pool_mix_fwd_{layer}")
        if layer + 1 < DEPTH:
            flight = _gather_late(flight, [ycat], f"gather_late_{layer + 1}")
        if layer + 1 < DEPTH:
            x_new, out = _out_proj(ycat, wg[1], xs[-1], mod, layer, f"out_proj_{layer}")
            gathered_w.append(gathered(flight, x_new, layer + 1))
            xs.append(x_new)
        else:
            dx, out, loss_part, d_final_g = _out_proj(ycat, wg[1], xs[-1], mod, layer, f"out_proj_{layer}",
                                                      final=(target, final_g.reshape(1, D)))
        hts.append(h_t)
        projs.append(proj)
        ycats.append(ycat)
        outs.append(out)

    d_mod, d_norm_g = [None] * DEPTH, [None] * DEPTH
    small, scatters, landed, res = {}, {}, {}, {}

    def finish_scatter(layer, after):
        send_sems, recv_sems, chip_parts, zones = scatters[layer]
        landed[layer] = _scatter_end(chip_parts, zones, send_sems, recv_sems, after, f"scatter_end_{layer}")

    def flat(a):
        return a.reshape(a.shape[0], -1, a.shape[-1])

    def sharded_adam(k, j, layers, w, m, v, carried, first=0, into=()):
        outs4 = _adam_sharded(w, m, v, [flat(landed[l][0][j]) for l in layers], [flat(landed[l][1][j]) for l in layers],
                              my_chip, carried, f"adam_{k}_{first}" if len(layers) < w.shape[0] else "adam_" + k, first, into)
        res[k] = [o.reshape(c_pool_w.shape) if k == "c_pool_w" else o for o in outs4[:4]]
        return list(outs4[4:])

    previous = None
    for layer in reversed(range(DEPTH)):
        wl = layer // 2
        even = layer % 2 == 0
        wg = gathered_w[layer]
        carried = [] if previous is None else [scatters[previous][2][0]]
        d_ycat, grad_out, d_gate, carried = _out_bwd(dx, outs[layer], ycats[layer], wg[1], mod, layer, carried, f"out_bwd_{layer}")
        if previous is not None:
            scatters[previous][2][0] = carried[0]
        parts = [None, grad_out]
        if even:
            d_proj, d_cw, d_lg, d_lb, d_sw, d_sb = _even_mix_bwd(
                projs[layer], d_ycat, convw, ln_g3, ln_b3, ab_sgu_w, sgu_bcol, wl, f"even_mix_bwd_{layer}")
            small[layer] = (d_cw, d_lg, d_lb, d_sw, d_sb)
        else:
            d_proj, d_ps, d_pool = _pool_mix_bwd(projs[layer], d_ycat, wg[2], pscale4, wl, f"pool_mix_bwd_{layer}")
            small[layer] = (d_ps,)
            parts.append(d_pool)
        parts[0] = _weight_grad(hts[layer], d_proj, f"grad_w_in_{layer}")
        pair_send, pair_recv, parts, from_sibling = _pair_start(parts, f"pair_start_{layer}")
        if layer > 0:
            dx, d_shift, d_scale, d_norm_g[layer], parts[0] = _dh_norm_bwd(
                d_proj, wg[0], xs[layer], dx, mod, norm_g3, layer, parts[0], f"dh_norm_bwd_{layer}")
            pair_after = dx
        else:
            finish_scatter(1, d_proj)
            finish_scatter(3, d_proj)
            parts[0], = sharded_adam("c_w_out", 1, (1, 3), c_w_out, m_c_w_out, v_c_w_out, [parts[0]])
            parts[0], = sharded_adam("c_pool_w", 2, (1, 3), pool_w3, m_pool_w3, v_pool_w3, [parts[0]])
            pair_after = res["c_pool_w"][0]
        parts, from_sibling = _pair_end(parts, from_sibling, pair_send, pair_recv, pair_after, f"pair_end_{layer}")
        chip_parts = _pair_sum(parts, from_sibling, core, f"pair_sum_{layer}")
        send_sems, recv_sems, chip_parts, zones = _scatter_start(chip_parts, f"scatter_start_{layer}")
        if layer == 0:
            chip_parts[0], = sharded_adam("c_w_in", 0, (1, 3), c_w_in, m_c_w_in, v_c_w_in, [chip_parts[0]])
            dx, d_shift, d_scale, d_norm_g[layer], chip_parts[0] = _dh_norm_bwd(
                d_proj, wg[0], xs[layer], dx, mod, norm_g3, layer, chip_parts[0], f"dh_norm_bwd_{layer}")
        scatters[layer] = [send_sems, recv_sems, chip_parts, zones]
        previous = layer
        d_mod[layer] = jnp.concatenate([d_shift, d_scale, d_gate], axis=0)
    grad_x = dx.reshape(x.shape)

    sections = [("norm_g", jnp.concatenate(d_norm_g, axis=0), 8),
                ("d_mod", jnp.concatenate(d_mod, axis=0), 16),
                ("ab_ln_g", jnp.concatenate([small[0][1], small[2][1]], axis=0), 8),
                ("ab_ln_b", jnp.concatenate([small[0][2], small[2][2]], axis=0), 8),
                ("ab_sgu_b", jnp.stack([small[0][4], small[2][4]]), 8),
                ("final_g", d_final_g, 8),
                ("ab_conv_w", jnp.stack([small[0][0], small[2][0]]), 8),
                ("c_pool_scale", jnp.concatenate([small[1][0], small[3][0]], axis=0), 8),
                ("ab_sgu_w", jnp.stack([small[0][3], small[2][3]]), 256)]
    offsets, at = {}, 0
    for name, _, rows in sections:
        offsets[name] = (at, rows)
        at += rows
    packed = jnp.concatenate([_pad_rows(a, rows) for _, a, rows in sections] + [jnp.zeros((-at % 32, D), F32)], axis=0)
    loss_rows = jnp.pad(loss_part, ((0, 15), (0, D - LANE)))
    small_zones = [_to_zone(packed[None], 0, me1, BF16, "place_small_grads"), _to_zone(loss_rows[None], 0, me1, F32, "place_loss")]
    small_flight, (mod,) = _gather_start(small_zones, [], [mod], "gather_small_start")

    finish_scatter(2, mod)
    sharded_adam("ab_w_out", 1, (2,), ab_w_out, m_ab_w_out, v_ab_w_out, [], first=1)
    sharded_adam("ab_w_in", 0, (2,), ab_w_in, m_ab_w_in, v_ab_w_in, [], first=1)
    finish_scatter(0, res["ab_w_in"][0])
    sharded_adam("ab_w_out", 1, (0,), ab_w_out, m_ab_w_out, v_ab_w_out, [], into=res["ab_w_out"])
    small_flight, (mod,) = _gather_mid(small_flight, [res["ab_w_out"][0]], [mod], "gather_small_mid")
    sharded_adam("ab_w_in", 0, (0,), ab_w_in, m_ab_w_in, v_ab_w_in, [mod], into=res["ab_w_in"])

    last = res["ab_w_in"][0]
    small_flight = _gather_late(small_flight, [last], "gather_small_late")
    small_grads, losses = _gather_end(small_flight, [last], "gather_small_end")
    summed = _sum_devices(small_grads, "sum_small_grads")
    loss = _sum_devices(losses, "sum_loss")[0, 0]

    def section(name, nrows, src=summed):
        start = offsets[name][0]
        return src[..., start:start + nrows, :]

    grads = {
        "norm_g": section("norm_g", DEPTH),
        "ada_b": section("d_mod", 3 * DEPTH).reshape(DEPTH, 3 * D),
        "ab_ln_g": section("ab_ln_g", 2), "ab_ln_b": section("ab_ln_b", 2),
        "ab_sgu_b": section("ab_sgu_b", 2).reshape(ab_sgu_b.shape),
        "final_g": section("final_g", 1),
        "ab_sgu_w": section("ab_sgu_w", 256).reshape(ab_sgu_w.shape),
        "ab_conv_w": lax.dynamic_slice_in_dim(section("ab_conv_w", 6).reshape(2, 3, E_A), me * HEAD, HEAD, axis=2),
        "c_pool_scale": lax.dynamic_slice_in_dim(section("c_pool_scale", 4).reshape(2, 4 * G_C), me * 256, 256, axis=1),
    }
    small_w = {"norm_g": (norm_g, m_norm_g, v_norm_g), "ada_b": (ada_b, m_ada_b, v_ada_b),
               "ab_ln_g": (ab_ln_g, m_ab_ln_g, v_ab_ln_g), "ab_ln_b": (ab_ln_b, m_ab_ln_b, v_ab_ln_b),
               "ab_sgu_b": (ab_sgu_b, m_ab_sgu_b, v_ab_sgu_b),
               "final_g": (final_g.reshape(1, D), m_final_g.reshape(1, D), v_final_g.reshape(1, D)),
               "ab_sgu_w": (ab_sgu_w, m_ab_sgu_w, v_ab_sgu_w), "ab_conv_w": (ab_conv_w, m_ab_conv_w, v_ab_conv_w),
               "c_pool_scale": (c_pool_scale, m_c_pool_scale, v_c_pool_scale)}
    updates = _adam_small([(w, grads[k], m, v) for k, (w, m, v) in small_w.items()], "adam_small")
    for k, update in zip(small_w, updates):
        res[k] = [grads[k]] + update
    res["final_g"] = [a.reshape(D) for a in res["final_g"]]

    dmod_all = section("d_mod", 3 * DEPTH, small_grads).reshape(NDEV, DEPTH, 3 * D)
    dmod_mine = jnp.transpose(lax.dynamic_slice_in_dim(dmod_all, me * ADA_NC, ADA_NC, axis=2), (1, 0, 2)).astype(F32)
    res["ada_w"] = _ada_weight_adam(jnp.transpose(cact_all.reshape(NDEV, D)), dmod_mine, ada_w, m_ada_w, v_ada_w)

    order = ["norm_g", "ada_w", "ada_b", "ab_w_in", "ab_conv_w", "ab_ln_g", "ab_ln_b", "ab_sgu_w", "ab_sgu_b",
             "ab_w_out", "c_w_in", "c_pool_w", "c_pool_scale", "c_w_out", "final_g"]
    return (loss, grad_x, *[res[k][0] for k in order], *[res[k][1] for k in order],
            *[res[k][2] for k in order], *[res[k][3] for k in order])
```

```python
import jax
import jax.numpy as jnp
from jax import lax
from jax.experimental import pallas as pl
from jax.experimental.pallas import tpu as pltpu

F32, BF16 = jnp.float32, jnp.bfloat16
S, D = 2048, 1024
NDEV = 8
DEPTH = 4
EPS = 1e-6
E_A = 1024
HEAD = 128
CHUNK = 128
POOL_WINDOWS = (2, 4, 8, 16)
G_C = 512
HALO = 16
ADA_NC = 384
MIB = 1024 * 1024
LANE = 128

ADAM_LR, ADAM_B1, ADAM_B2, ADAM_EPS, ADAM_WD, ADAM_STEP = 0.001, 0.9, 0.999, 1e-08, 0.01, 10

ANY = pl.BlockSpec(memory_space=pl.ANY)
VMEM_FULL = pl.BlockSpec(memory_space=pltpu.VMEM)
IN_HBM = pl.BlockSpec(memory_space=pltpu.HBM)
SEMAPHORES = pl.BlockSpec(memory_space=pltpu.SEMAPHORE)
IN_FLIGHT = pltpu.SideEffectType.DATAFLOW_SIDE_EFFECTING


V7X_VMEM_MIB = 64
VMEM_LIMIT_MIB = V7X_VMEM_MIB - 4


def _params(semantics=None):
    return pltpu.CompilerParams(dimension_semantics=semantics, vmem_limit_bytes=VMEM_LIMIT_MIB * MIB)


def _silu(z):
    return z * jax.nn.sigmoid(z)


def _silu_and_grad(z):
    sig = jax.nn.sigmoid(z)
    return z * sig, sig * (1.0 + z * (1.0 - sig))


def _position():
    return lax.axis_index("x"), lax.axis_index("y"), lax.axis_index("c")


def _index(pos):
    return 4 * pos[0] + 2 * pos[1] + pos[2]


def _peer(pos, k):
    flipped = tuple(1 - p if (k >> (2 - b)) & 1 else p for b, p in enumerate(pos))
    return flipped, _index(flipped)


def _remote(src, dst, send_sem, recv_sem, device):
    return pltpu.make_async_remote_copy(src_ref=src, dst_ref=dst, send_sem=send_sem, recv_sem=recv_sem,
                                        device_id=device, device_id_type=pl.DeviceIdType.MESH)


def _pair_start(parts, name):
    n = len(parts)
    lands = [_in_hbm(lax.empty(p.shape[1:], p.dtype)) for p in parts]

    def body(*refs):
        ins, zones = refs[:n], refs[n:2 * n]
        send_sems, recv_sems = refs[2 * n:2 * n + 2]
        x, y, c = _position()
        for j in range(n):
            _remote(ins[j].at[1 - c], zones[j], send_sems.at[j], recv_sems.at[j], (x, y, 1 - c)).start()

    outs = pl.pallas_call(
        body, name=name,
        out_shape=(pltpu.SemaphoreType.DMA((n,)), pltpu.SemaphoreType.DMA((n,)),
                   *[pltpu.HBM(p.shape, p.dtype) for p in parts], *[pltpu.HBM(z.shape, z.dtype) for z in lands]),
        in_specs=[IN_HBM] * (2 * n), out_specs=(SEMAPHORES, SEMAPHORES, *[IN_HBM] * (2 * n)),
        input_output_aliases={j: 2 + j for j in range(2 * n)},
        compiler_params=pltpu.CompilerParams(has_side_effects=IN_FLIGHT),
    )(*[_in_hbm(p) for p in parts], *lands)
    return outs[0], outs[1], list(outs[2:2 + n]), list(outs[2 + n:])


def _pair_end(parts, zones, send_sems, recv_sems, after, name):
    n = len(parts)

    def body(*refs):
        ins, zs = refs[:n], refs[n:2 * n]
        s, r = refs[2 * n:2 * n + 2]
        me = _position()
        for j in range(n):
            copy = _remote(ins[j].at[0], zs[j], s.at[j], r.at[j], me)
            copy.wait_send()
            copy.wait_recv()

    outs = pl.pallas_call(
        body, name=name,
        out_shape=(*[pltpu.HBM(p.shape, p.dtype) for p in parts], *[pltpu.HBM(z.shape, z.dtype) for z in zones]),
        in_specs=[IN_HBM] * (2 * n) + [SEMAPHORES, SEMAPHORES, ANY], out_specs=tuple([IN_HBM] * (2 * n)),
        input_output_aliases={j: j for j in range(2 * n)},
        compiler_params=pltpu.CompilerParams(has_side_effects=IN_FLIGHT),
    )(*parts, *zones, send_sems, recv_sems, after)
    return list(outs[:n]), list(outs[n:])


def _pair_sum(parts, from_sibling, core, name):
    n = len(parts)
    steps = 8
    p3 = [p.reshape(2, -1, p.shape[-1]) for p in parts]
    q2 = [q.reshape(-1, q.shape[-1]) for q in from_sibling]

    def body(core_ref, *refs):
        for p_ref, q_ref, o_ref in zip(refs[:n], refs[n:2 * n], refs[2 * n:]):
            o_ref[...] = (p_ref[...].astype(F32) + q_ref[...].astype(F32)).astype(BF16)

    tiles = [pl.BlockSpec((q.shape[0] // steps, q.shape[1]), lambda i, core_ref: (i, 0)) for q in q2]
    grid_spec = pltpu.PrefetchScalarGridSpec(
        num_scalar_prefetch=1, grid=(steps,),
        in_specs=[pl.BlockSpec((None, q.shape[0] // steps, q.shape[1]), lambda i, core_ref: (core_ref[0], i, 0)) for q in q2]
        + tiles, out_specs=tiles)
    outs = pl.pallas_call(body, name=name, grid_spec=grid_spec, out_shape=[jax.ShapeDtypeStruct(q.shape, BF16) for q in q2],
                          compiler_params=_params(("arbitrary",)))(core, *p3, *q2)
    return [o.reshape(q.shape) for o, q in zip(outs, from_sibling)]


def _in_hbm(a):
    return pltpu.with_memory_space_constraint(a, pltpu.HBM)


def _chips(x, y):
    return [(1 - x, y), (x, 1 - y), (1 - x, 1 - y)]


def _to_zone(a, wl, me, dtype, name):
    _, rows, cols = a.shape
    tr = 256 if rows % 256 == 0 else rows

    def body(me_ref, a_ref, o_ref):
        o_ref[...] = a_ref[...].astype(dtype)

    grid_spec = pltpu.PrefetchScalarGridSpec(
        num_scalar_prefetch=1, grid=(rows // tr,),
        in_specs=[pl.BlockSpec((None, tr, cols), lambda i, me_ref: (wl, i, 0))],
        out_specs=pl.BlockSpec((None, tr, cols), lambda i, me_ref: (me_ref[0], i, 0)))
    return pl.pallas_call(body, name=name, grid_spec=grid_spec, out_shape=jax.ShapeDtypeStruct((NDEV, rows, cols), dtype),
                          compiler_params=_params(("arbitrary",)))(me, a)


def _halves(block):
    rows = block.shape[0] // 2
    return block.at[pl.ds(0, rows)], block.at[pl.ds(rows, rows)]


def _around(x, y, c):
    return (x, y, 1 - c), (1 - x, y, c), (x, 1 - y, c), (1 - x, 1 - y, c)


def _gather_step1(zs, send, recv, pos):
    sibling, xn, yn, _ = _around(*pos)
    for j, z in enumerate(zs):
        mine = z.at[_index(pos)]
        for k, peer in enumerate((sibling, xn, yn)):
            _remote(mine, mine, send.at[3 * j + k], recv.at[3 * j + k], peer).start()


def _gather_step2(zs, recv1, send, recv, pos):
    sibling, xn, yn, _ = _around(*pos)
    for j, z in enumerate(zs):
        xb, yb = z.at[_index(xn)], z.at[_index(yn)]
        _remote(xb, xb, send.at[4 * j], recv1.at[3 * j + 1], pos).wait_recv()
        _remote(yb, yb, send.at[4 * j], recv1.at[3 * j + 2], pos).wait_recv()
        _remote(xb, xb, send.at[4 * j], recv.at[4 * j], sibling).start()
        _remote(yb, yb, send.at[4 * j + 1], recv.at[4 * j + 1], sibling).start()
        first, second = _halves(xb)[0], _halves(yb)[1]
        _remote(first, first, send.at[4 * j + 2], recv.at[4 * j + 2], yn).start()
        _remote(second, second, send.at[4 * j + 3], recv.at[4 * j + 3], xn).start()


def _gather_step3(zs, recv2, send, recv, pos):
    sibling, _, _, diagonal = _around(*pos)
    for j, z in enumerate(zs):
        db = z.at[_index(diagonal)]
        first, second = _halves(db)
        _remote(first, first, send.at[j], recv2.at[4 * j + 2], pos).wait_recv()
        _remote(second, second, send.at[j], recv2.at[4 * j + 3], pos).wait_recv()
        _remote(db, db, send.at[j], recv.at[j], sibling).start()


def _gather_step4(zs, send1, recv1, send2, recv2, send3, recv3, pos):
    x, y, c = pos
    sibling = (x, y, 1 - c)
    _, sx, sy, sd = _around(*sibling)
    for j, z in enumerate(zs):
        for owner, send, recv, k in ((sibling, send1, recv1, 3 * j), (sx, send2, recv2, 4 * j), (sy, send2, recv2, 4 * j + 1),
                                     (sd, send3, recv3, j)):
            block = z.at[_index(owner)]
            _remote(block, block, send.at[k], recv.at[k], pos).wait_recv()
    for j, z in enumerate(zs):
        block = z.at[0]
        half = _halves(block)[0]
        for ref, send, recv, k in ([(block, send1, recv1, 3 * j + k) for k in range(3)]
                                   + [(block, send2, recv2, 4 * j), (block, send2, recv2, 4 * j + 1),
                                      (half, send2, recv2, 4 * j + 2), (half, send2, recv2, 4 * j + 3), (block, send3, recv3, j)]):
            _remote(ref, ref, send.at[k], recv.at[k], pos).wait_send()


def _flight_call(step, name, zones, sems_in, nsems_out, after, carried):
    n, m, k = len(zones), len(carried), len(sems_in)

    def body(*refs):
        zs = refs[:n]
        given = refs[n + m:n + m + k]
        made = refs[n + m + k + len(after):n + m + k + len(after) + (2 if nsems_out else 0)]
        step(zs, *given, *made, _position())

    sem_out = (pltpu.SemaphoreType.DMA((nsems_out,)),) * 2 if nsems_out else ()
    outs = pl.pallas_call(
        body, name=name,
        out_shape=(*sem_out, *[pltpu.HBM(z.shape, z.dtype) for z in zones], *[jax.ShapeDtypeStruct(a.shape, a.dtype) for a in carried]),
        in_specs=[IN_HBM] * n + [ANY] * m + [SEMAPHORES] * k + [ANY] * len(after),
        out_specs=(*[SEMAPHORES] * len(sem_out), *[IN_HBM] * n, *[ANY] * m),
        input_output_aliases={j: len(sem_out) + j for j in range(n + m)},
        compiler_params=pltpu.CompilerParams(has_side_effects=IN_FLIGHT),
    )(*[_in_hbm(z) for z in zones], *carried, *sems_in, *after)
    sems = list(outs[:len(sem_out)])
    return sems, list(outs[len(sem_out):len(sem_out) + n]), list(outs[len(sem_out) + n:])


def _gather_start(zones, after, carried, name):
    (send1, recv1), zones, carried = _flight_call(_gather_step1, name, zones, [], 3 * len(zones), after, carried)
    return {"s1": send1, "r1": recv1, "zones": zones}, carried


def _gather_mid(flight, after, carried, name):
    step = lambda zs, recv1, send, recv, pos: _gather_step2(zs, recv1, send, recv, pos)
    (send2, recv2), zones, carried = _flight_call(step, name, flight["zones"], [flight["r1"]], 4 * len(flight["zones"]), after, carried)
    return {**flight, "s2": send2, "r2": recv2, "zones": zones}, carried


def _gather_late(flight, after, name):
    step = lambda zs, recv2, send, recv, pos: _gather_step3(zs, recv2, send, recv, pos)
    (send3, recv3), zones, _ = _flight_call(step, name, flight["zones"], [flight["r2"]], len(flight["zones"]), after, [])
    return {**flight, "s3": send3, "r3": recv3, "zones": zones}


def _gather_end(flight, after, name):
    sems = [flight[k] for k in ("s1", "r1", "s2", "r2", "s3", "r3")]
    _, zones, _ = _flight_call(_gather_step4, name, flight["zones"], sems, 0, after, [])
    return zones


def _scatter_start(parts, name):
    n = len(parts)
    lands = [_in_hbm(lax.empty((3,) + p.shape[1:], p.dtype)) for p in parts]

    def body(*refs):
        ins, zones = refs[:n], refs[n:2 * n]
        send_sems, recv_sems = refs[2 * n:2 * n + 2]
        x, y, c = _position()
        for j in range(n):
            for q, (px, py) in enumerate(_chips(x, y)):
                _remote(ins[j].at[2 * px + py], zones[j].at[q], send_sems.at[3 * j + q], recv_sems.at[3 * j + q],
                        (px, py, c)).start()

    outs = pl.pallas_call(
        body, name=name,
        out_shape=(pltpu.SemaphoreType.DMA((3 * n,)), pltpu.SemaphoreType.DMA((3 * n,)),
                   *[pltpu.HBM(p.shape, p.dtype) for p in parts], *[pltpu.HBM(z.shape, z.dtype) for z in lands]),
        in_specs=[IN_HBM] * (2 * n), out_specs=(SEMAPHORES, SEMAPHORES, *[IN_HBM] * (2 * n)),
        input_output_aliases={j: 2 + j for j in range(2 * n)},
        compiler_params=pltpu.CompilerParams(has_side_effects=IN_FLIGHT),
    )(*[_in_hbm(p) for p in parts], *lands)
    return outs[0], outs[1], list(outs[2:2 + n]), list(outs[2 + n:])


def _scatter_end(parts, zones, send_sems, recv_sems, after, name):
    n = len(parts)

    def body(*refs):
        ins, zs = refs[:n], refs[n:2 * n]
        s, r = refs[2 * n:2 * n + 2]
        me = _position()
        for j in range(n):
            for q in range(3):
                copy = _remote(ins[j].at[0], zs[j].at[q], s.at[3 * j + q], r.at[3 * j + q], me)
                copy.wait_send()
                copy.wait_recv()

    outs = pl.pallas_call(
        body, name=name,
        out_shape=(*[pltpu.HBM(p.shape, p.dtype) for p in parts], *[pltpu.HBM(z.shape, z.dtype) for z in zones]),
        in_specs=[IN_HBM] * (2 * n) + [SEMAPHORES, SEMAPHORES, ANY], out_specs=tuple([IN_HBM] * (2 * n)),
        input_output_aliases={j: j for j in range(2 * n)},
        compiler_params=pltpu.CompilerParams(has_side_effects=IN_FLIGHT),
    )(*parts, *zones, send_sems, recv_sems, after)
    return list(outs[:n]), list(outs[n:])


def _ada_forward(c, ada_w, ada_b, small):
    ns = len(small)

    def body(c_ref, w_ref, b_ref, *rest):
        small_refs, (cact_ref, mod_ref), gathered = rest[:ns], rest[ns:ns + 2], rest[ns + 2:2 * ns + 2]
        gbuf, modrow, send_sems, recv_sems = rest[2 * ns + 2:]
        pos = _position()
        me = _index(pos)

        def to_all(*exchanged):
            copies = []
            for ref, row in exchanged:
                for k in range(1, NDEV):
                    peer, _ = _peer(pos, k)
                    copy = pltpu.make_async_remote_copy(
                        src_ref=ref.at[me], dst_ref=ref.at[me], send_sem=send_sems.at[row, k - 1],
                        recv_sem=recv_sems.at[row, k - 1], device_id=peer, device_id_type=pl.DeviceIdType.MESH)
                    copy.start()
                    copies.append(copy)
            for copy in copies:
                copy.wait()

        cact_ref[me] = _silu(c_ref[...])
        for j in range(ns):
            gathered[j][me] = small_refs[j][...]
        to_all((cact_ref, 0), *[(gathered[j], 2 + j) for j in range(ns)])
        rows = lax.broadcasted_iota(jnp.int32, (NDEV, D), 0)
        cact = jnp.zeros((NDEV, D), F32)
        for e in range(NDEV):
            cact = jnp.where(rows == e, cact_ref[e], cact)
        cact = cact.astype(BF16)
        for l in range(DEPTH):
            gbuf[me, l] = jnp.dot(cact, w_ref[l].astype(BF16), preferred_element_type=F32)
        to_all((gbuf, 1))
        mine = lax.broadcasted_iota(jnp.int32, (NDEV, ADA_NC), 0) == me
        for l in range(DEPTH):
            for d in range(NDEV):
                modrow[:, d * ADA_NC:(d + 1) * ADA_NC] = jnp.sum(jnp.where(mine, gbuf[d, l], 0.0), axis=0, keepdims=True)
            full = modrow[...] + b_ref[l:l + 1, :]
            for w in range(3):
                mod_ref[l, w] = full[:, w * D:(w + 1) * D]

    outs = pl.pallas_call(
        body, name="ada_forward",
        out_shape=[jax.ShapeDtypeStruct((NDEV, 1, D), F32), jax.ShapeDtypeStruct((DEPTH, 3, 1, D), F32)]
        + [jax.ShapeDtypeStruct((NDEV,) + a.shape, a.dtype) for a in small],
        in_specs=[VMEM_FULL] * (3 + ns), out_specs=[VMEM_FULL] * (2 + ns),
        scratch_shapes=[pltpu.VMEM((NDEV, DEPTH, NDEV, ADA_NC), F32), pltpu.VMEM((1, 3 * D), F32),
                        pltpu.SemaphoreType.DMA((2 + ns, NDEV - 1)), pltpu.SemaphoreType.DMA((2 + ns, NDEV - 1))],
        compiler_params=_params(),
    )(c, ada_w, ada_b, *small)
    return outs[0], outs[1], list(outs[2:])


def _mod_spec(layer, which, ngrid):
    index = {1: lambda i: (layer, which, 0, 0), 2: lambda i, j: (layer, which, 0, 0)}[ngrid]
    return pl.BlockSpec((None, None, 1, D), index)


W_BLOCKS = 4


def _norm_proj(x, mod, norm_g3, wg, layer, name):
    nb = wg.shape[-1]
    tm = 1024
    wb = W_BLOCKS

    def body(x_ref, g_ref, shift_ref, scale_ref, w_ref, ht_ref, p_ref, h_ref):
        @pl.when(pl.program_id(1) == 0)
        def _():
            xv = x_ref[...]
            r = lax.rsqrt(jnp.mean(xv * xv, axis=-1, keepdims=True) + EPS)
            hn = xv * r * g_ref[...]
            h = hn * (1.0 + scale_ref[...]) + shift_ref[...]
            h_ref[...] = h.astype(BF16)
            ht_ref[...] = h.T.astype(BF16)

        hv = h_ref[...]
        for b in range(wb):
            p_ref[:, b * nb:(b + 1) * nb] = jnp.dot(hv, w_ref[b], preferred_element_type=F32).astype(BF16)

    return pl.pallas_call(
        body, name=name, grid=(S // tm, NDEV // wb),
        out_shape=[jax.ShapeDtypeStruct((D, S), BF16), jax.ShapeDtypeStruct((S, NDEV * nb), BF16)],
        in_specs=[pl.BlockSpec((tm, D), lambda i, d: (i, 0)),
                  pl.BlockSpec((None, 1, D), lambda i, d: (layer, 0, 0)),
                  _mod_spec(layer, 0, 2), _mod_spec(layer, 1, 2),
                  pl.BlockSpec((wb, D, nb), lambda i, d: (d, 0, 0))],
        out_specs=[pl.BlockSpec((D, tm), lambda i, d: (0, i)), pl.BlockSpec((tm, wb * nb), lambda i, d: (i, d))],
        scratch_shapes=[pltpu.VMEM((tm, D), BF16)],
        compiler_params=_params(("arbitrary", "arbitrary")),
    )(x, norm_g3, mod, mod, wg)


def _out_proj(ycat, w_out, x, mod, layer, name, final=None):
    tm = 512
    e = w_out.shape[0]

    def body(y_ref, w_ref, x_ref, gate_ref, *rest):
        acc = jnp.dot(y_ref[...], w_ref[...], preferred_element_type=F32)
        xv = x_ref[...] + gate_ref[...] * acc
        if final is None:
            xn_ref, o_ref = rest
            o_ref[...] = acc.astype(BF16)
            xn_ref[...] = xv
            return
        t_ref, g_ref, dx_ref, o_ref, loss_ref, dg_ref = rest
        o_ref[...] = acc.astype(BF16)

        @pl.when(pl.program_id(0) == 0)
        def _():
            loss_ref[...] = jnp.zeros_like(loss_ref)
            dg_ref[...] = jnp.zeros_like(dg_ref)

        g = g_ref[...]
        r = lax.rsqrt(jnp.mean(xv * xv, axis=-1, keepdims=True) + EPS)
        xn = xv * r
        err = xn * g - t_ref[...]
        loss_ref[...] += 0.5 * jnp.sum(jnp.mean(err * err, axis=-1, keepdims=True), axis=0, keepdims=True)
        dy = err * (1.0 / D)
        dg_ref[...] += jnp.sum(dy * xn, axis=0, keepdims=True)
        u = dy * g
        dx_ref[...] = r * (u - xn * jnp.mean(xn * u, axis=-1, keepdims=True))

    tile = pl.BlockSpec((tm, D), lambda i: (i, 0))
    row = pl.BlockSpec((1, D), lambda i: (0, 0))
    out_shape = [jax.ShapeDtypeStruct((S, D), F32), jax.ShapeDtypeStruct((S, D), BF16)]
    in_specs = [pl.BlockSpec((tm, e), lambda i: (i, 0)), pl.BlockSpec((e, D), lambda i: (0, 0)), tile, _mod_spec(layer, 2, 1)]
    out_specs, operands = [tile, tile], [ycat, w_out, x, mod]
    if final is not None:
        out_shape += [jax.ShapeDtypeStruct((1, LANE), F32), jax.ShapeDtypeStruct((1, D), F32)]
        in_specs += [tile, row]
        out_specs += [pl.BlockSpec((1, LANE), lambda i: (0, 0)), row]
        operands += list(final)
    return pl.pallas_call(
        body, name=name, grid=(S // tm,), out_shape=out_shape, in_specs=in_specs, out_specs=out_specs,
        compiler_params=_params(("arbitrary",)),
    )(*operands)


def _out_bwd(dx, out, ycat, w_out, mod, layer, carried, name):
    tm = 512
    nsteps = S // tm
    e = ycat.shape[1]
    rb = e // NDEV
    nc = len(carried)

    def body(dx_ref, o_ref, y_ref, w_ref, gate_ref, *rest):
        dy_ref, gw_ref, dgate_ref = rest[nc:nc + 3]
        acc = rest[-1]
        step = pl.program_id(0)

        @pl.when(step == 0)
        def _():
            dgate_ref[...] = jnp.zeros_like(dgate_ref)
            acc[...] = jnp.zeros_like(acc)

        dxv = dx_ref[...]
        d_out = (gate_ref[...] * dxv).astype(BF16)
        dgate_ref[...] += jnp.sum(dxv * o_ref[...].astype(F32), axis=0, keepdims=True)
        dy_ref[...] = lax.dot_general(d_out, w_ref[...], (((1,), (1,)), ((), ())), preferred_element_type=F32).astype(BF16)
        acc[...] += lax.dot_general(y_ref[...], d_out, (((0,), (0,)), ((), ())), preferred_element_type=F32)

        @pl.when(step == nsteps - 1)
        def _():
            for d in range(NDEV):
                gw_ref[d % 2, d // 2] = acc[d * rb:(d + 1) * rb, :].astype(BF16)

    tile = pl.BlockSpec((tm, D), lambda i: (i, 0))
    wide = pl.BlockSpec((tm, e), lambda i: (i, 0))
    outs = pl.pallas_call(
        body, name=name, grid=(nsteps,),
        out_shape=[jax.ShapeDtypeStruct((S, e), BF16), jax.ShapeDtypeStruct((2, NDEV // 2, rb, D), BF16),
                   jax.ShapeDtypeStruct((1, D), F32)] + [jax.ShapeDtypeStruct(a.shape, a.dtype) for a in carried],
        in_specs=[tile, tile, wide, pl.BlockSpec((e, D), lambda i: (0, 0)), _mod_spec(layer, 2, 1)] + [ANY] * nc,
        out_specs=[wide, pl.BlockSpec((2, NDEV // 2, rb, D), lambda i: (0, 0, 0, 0)), pl.BlockSpec((1, D), lambda i: (0, 0))]
        + [ANY] * nc,
        scratch_shapes=[pltpu.VMEM((e, D), F32)],
        input_output_aliases={5 + k: 3 + k for k in range(nc)},
        compiler_params=_params(("arbitrary",)),
    )(dx, out, ycat, w_out, mod, *carried)
    return outs[0], outs[1], outs[2], list(outs[3:])


def _weight_grad(h_t, d_proj, name):
    nb = d_proj.shape[1] // NDEV

    def body(ht_ref, dp_ref, o_ref):
        o_ref[...] = jnp.dot(ht_ref[...], dp_ref[...], preferred_element_type=F32).astype(BF16)

    return pl.pallas_call(
        body, name=name, grid=(NDEV,), out_shape=jax.ShapeDtypeStruct((2, NDEV // 2, D, nb), BF16),
        in_specs=[pl.BlockSpec((D, S), lambda d: (0, 0)), pl.BlockSpec((S, nb), lambda d: (0, d))],
        out_specs=pl.BlockSpec((None, None, D, nb), lambda d: (d % 2, d // 2, 0, 0)),
        compiler_params=_params(("arbitrary",)),
    )(h_t, d_proj)


def _dh_norm_bwd(d_proj, wg, x, dx, mod, norm_g3, layer, carried, name):
    nb = wg.shape[-1]
    tm = 512
    wb = W_BLOCKS
    rc = 128

    def body(dp_ref, w_ref, x_ref, dx_ref, g_ref, scale_ref, carried_ref,
             dxi_ref, dshift_ref, dscale_ref, dg_ref, carried_out, acc):
        i, d = pl.program_id(0), pl.program_id(1)
        nt = (((1,), (1,)), ((), ()))
        part = lax.dot_general(dp_ref[:, :nb], w_ref[0], nt, preferred_element_type=F32)
        for b in range(1, wb):
            part += lax.dot_general(dp_ref[:, b * nb:(b + 1) * nb], w_ref[b], nt, preferred_element_type=F32)

        @pl.when(d == 0)
        def _():
            acc[...] = part

        @pl.when(d != 0)
        def _():
            acc[...] += part

        @pl.when(jnp.logical_and(i == 0, d == 0))
        def _():
            dshift_ref[...] = jnp.zeros_like(dshift_ref)
            dscale_ref[...] = jnp.zeros_like(dscale_ref)
            dg_ref[...] = jnp.zeros_like(dg_ref)

        @pl.when(d == NDEV // wb - 1)
        def _():
            g = g_ref[...]
            scale1 = 1.0 + scale_ref[...]

            def chunk(k, sums):
                rows = pl.ds(pl.multiple_of(k * rc, rc), rc)
                xv, dhv = x_ref[rows, :], acc[rows, :]
                r = lax.rsqrt(jnp.mean(xv * xv, axis=-1, keepdims=True) + EPS)
                xn = xv * r
                dhn = dhv * scale1
                u = dhn * g
                dxi_ref[rows, :] = dx_ref[rows, :] + r * (u - xn * jnp.mean(xn * u, axis=-1, keepdims=True))
                return (sums[0] + jnp.sum(dhv, axis=0, keepdims=True),
                        sums[1] + jnp.sum(dhv * (xn * g), axis=0, keepdims=True),
                        sums[2] + jnp.sum(dhn * xn, axis=0, keepdims=True))

            zero = jnp.zeros((1, D), F32)
            sums = lax.fori_loop(0, tm // rc, chunk, (zero, zero, zero))
            dshift_ref[...] += sums[0]
            dscale_ref[...] += sums[1]
            dg_ref[...] += sums[2]

    tile = pl.BlockSpec((tm, D), lambda i, d: (i, 0))
    row = pl.BlockSpec((1, D), lambda i, d: (0, 0))
    return pl.pallas_call(
        body, name=name, grid=(S // tm, NDEV // wb),
        out_shape=[jax.ShapeDtypeStruct((S, D), F32)] + [jax.ShapeDtypeStruct((1, D), F32)] * 3
        + [jax.ShapeDtypeStruct(carried.shape, carried.dtype)],
        in_specs=[pl.BlockSpec((tm, wb * nb), lambda i, d: (i, d)), pl.BlockSpec((wb, D, nb), lambda i, d: (d, 0, 0)),
                  tile, tile, pl.BlockSpec((None, 1, D), lambda i, d: (layer, 0, 0)), _mod_spec(layer, 1, 2), ANY],
        out_specs=[tile, row, row, row, ANY], scratch_shapes=[pltpu.VMEM((tm, D), F32)],
        input_output_aliases={6: 4}, compiler_params=_params(("arbitrary", "arbitrary")),
    )(d_proj, wg, x, dx, norm_g3, mod, carried)


TS = 256
NCH = TS // CHUNK
HALO_BLOCKS = TS // HALO


def _halo_before(width, col_block):
    return pl.BlockSpec((HALO, width), lambda i: (jnp.maximum(i * HALO_BLOCKS - 1, 0), col_block))


def _halo_after(width, col_block):
    return pl.BlockSpec((HALO, width), lambda i: (jnp.minimum((i + 1) * HALO_BLOCKS, S // HALO - 1), col_block))


def _shift_down(ext, k):
    return pltpu.roll(ext, k, 0)[HALO:]


def _shift_up(ext, k):
    return pltpu.roll(ext, ext.shape[0] - k, 0)[:ext.shape[0] - HALO]


def _layer_norm_head(v, lg, lb):
    mu = jnp.mean(v, axis=-1, keepdims=True)
    vc = v - mu
    rstd = lax.rsqrt(jnp.mean(vc * vc, axis=-1, keepdims=True) + EPS)
    vhat = vc * rstd
    return vhat, rstd, vhat * lg + lb


def _causal_mask():
    return lax.broadcasted_iota(jnp.int32, (CHUNK, CHUNK), 0) >= lax.broadcasted_iota(jnp.int32, (CHUNK, CHUNK), 1)


def _even_mix_fwd(proj, convw, ln_g3, ln_b3, sgu_w, sgu_bcol, wl, after, name):
    def body(pj_ref, hh_ref, hc_ref, cw_ref, lg_ref, lb_ref, sw_ref, sb_ref, *rest):
        y_ref = rest[-1]
        live = (pl.program_id(0) > 0).astype(F32)
        causal = _causal_mask()
        for j in range(E_A // HEAD):
            cols = slice(j * HEAD, (j + 1) * HEAD)
            w0, w1, w2 = cw_ref[0:1, cols], cw_ref[1:2, cols], cw_ref[2:3, cols]
            lg, lb = lg_ref[:, cols], lb_ref[:, cols]
            wm = jnp.where(causal, sw_ref[j], 0.0).astype(BF16)
            bias = sb_ref[j]

            def split(s, rows, cols=cols):
                return pj_ref[rows, s * E_A + cols.start:s * E_A + cols.stop].astype(F32)

            prev_tail = hc_ref[:, cols].astype(F32) * hh_ref[:, cols].astype(F32) * live
            for n in range(NCH):
                rows = slice(n * CHUNK, (n + 1) * CHUNK)
                p = split(2, rows) * split(0, rows)
                ext = jnp.concatenate([prev_tail, p], axis=0)
                prev_tail = p[CHUNK - HALO:]
                cv = w2 * p + w1 * _shift_down(ext, 1) + w0 * _shift_down(ext, 2)
                y_ref[rows, cols] = (split(1, rows) * cv * _silu(split(3, rows))).astype(BF16)
                _, _, vn = _layer_norm_head(split(5, rows), lg, lb)
                mixed = jnp.dot(wm, vn.astype(BF16), preferred_element_type=F32) + bias
                y_ref[rows, E_A + cols.start:E_A + cols.stop] = (split(4, rows) * mixed * _silu(split(6, rows))).astype(BF16)

    const3 = lambda i: (wl, 0, 0)
    const4 = lambda i: (wl, 0, 0, 0)
    return pl.pallas_call(
        body, name=name, grid=(S // TS,), out_shape=jax.ShapeDtypeStruct((S, 2 * E_A), BF16),
        in_specs=[pl.BlockSpec((TS, 7 * E_A), lambda i: (i, 0)), _halo_before(E_A, 0), _halo_before(E_A, 2),
                  pl.BlockSpec((None, 3, E_A), const3), pl.BlockSpec((None, 1, E_A), const3),
                  pl.BlockSpec((None, 1, E_A), const3), pl.BlockSpec((None, NDEV, CHUNK, CHUNK), const4),
                  pl.BlockSpec((None, NDEV, CHUNK, 1), const4)] + [ANY] * len(after),
        out_specs=pl.BlockSpec((TS, 2 * E_A), lambda i: (i, 0)),
        compiler_params=_params(("arbitrary",)),
    )(proj, proj, proj, convw, ln_g3, ln_b3, sgu_w, sgu_bcol, *after)


def _even_mix_bwd(proj, d_ycat, convw, ln_g3, ln_b3, sgu_w, sgu_bcol, wl, name):
    nsteps = S // TS

    def body(pj_ref, hh_ref, hc_ref, hb_ref, hz_ref, dy_ref, hdy_ref, cw_ref, lg_ref, lb_ref, sw_ref, sb_ref,
             dp_ref, dcw_ref, dlg_ref, dlb_ref, dsw_ref, dsb_ref):
        step = pl.program_id(0)

        @pl.when(step == 0)
        def _():
            for ref in (dcw_ref, dlg_ref, dlb_ref, dsw_ref, dsb_ref):
                ref[...] = jnp.zeros_like(ref)

        live_before = (step > 0).astype(F32)
        live_after = (step < nsteps - 1).astype(F32)
        causal = _causal_mask()
        for j in range(E_A // HEAD):
            cols = slice(j * HEAD, (j + 1) * HEAD)
            w0, w1, w2 = cw_ref[0:1, cols], cw_ref[1:2, cols], cw_ref[2:3, cols]
            lg, lb = lg_ref[:, cols], lb_ref[:, cols]
            wmf = jnp.where(causal, sw_ref[j], 0.0)
            wm, wmt = wmf.astype(BF16), wmf.T.astype(BF16)
            bias = sb_ref[j]

            def split(s, rows, cols=cols):
                return pj_ref[rows, s * E_A + cols.start:s * E_A + cols.stop].astype(F32)

            def put(s, rows, val, cols=cols):
                dp_ref[rows, s * E_A + cols.start:s * E_A + cols.stop] = val.astype(BF16)

            ps = [split(2, slice(n * CHUNK, (n + 1) * CHUNK)) * split(0, slice(n * CHUNK, (n + 1) * CHUNK)) for n in range(NCH)]
            next_head = (hdy_ref[:, cols].astype(F32) * hb_ref[:, cols].astype(F32) * _silu(hz_ref[:, cols].astype(F32))
                         * live_after)
            acc_w = [jnp.zeros((1, HEAD), F32) for _ in range(3)]
            for n in reversed(range(NCH)):
                rows = slice(n * CHUNK, (n + 1) * CHUNK)
                p = ps[n]
                tail = ps[n - 1][CHUNK - HALO:] if n > 0 else hc_ref[:, cols].astype(F32) * hh_ref[:, cols].astype(F32) * live_before
                ext = jnp.concatenate([tail, p], axis=0)
                p1, p2 = _shift_down(ext, 1), _shift_down(ext, 2)
                cv = w2 * p + w1 * p1 + w0 * p2
                a_b, a_z = split(1, rows), split(3, rows)
                sz, dsz = _silu_and_grad(a_z)
                dya = dy_ref[rows, cols].astype(F32)
                put(1, rows, dya * cv * sz)
                put(3, rows, dya * a_b * cv * dsz)
                gcv = dya * a_b * sz
                acc_w[0] += jnp.sum(gcv * p2, axis=0, keepdims=True)
                acc_w[1] += jnp.sum(gcv * p1, axis=0, keepdims=True)
                acc_w[2] += jnp.sum(gcv * p, axis=0, keepdims=True)
                gext = jnp.concatenate([gcv, next_head], axis=0)
                next_head = gcv[:HALO]
                dpv = w2 * gcv + w1 * _shift_up(gext, 1) + w0 * _shift_up(gext, 2)
                put(2, rows, dpv * split(0, rows))
                put(0, rows, dpv * split(2, rows))
            for k in range(3):
                dcw_ref[k:k + 1, cols] += acc_w[k]

            acc_lg, acc_lb = jnp.zeros((1, HEAD), F32), jnp.zeros((1, HEAD), F32)
            acc_sw, acc_sb = jnp.zeros((CHUNK, CHUNK), F32), jnp.zeros((CHUNK, 1), F32)
            for n in range(NCH):
                rows = slice(n * CHUNK, (n + 1) * CHUNK)
                u, z = split(4, rows), split(6, rows)
                vhat, rstd, vn = _layer_norm_head(split(5, rows), lg, lb)
                vn16 = vn.astype(BF16)
                mixed = jnp.dot(wm, vn16, preferred_element_type=F32) + bias
                sz, dsz = _silu_and_grad(z)
                dyb = dy_ref[rows, E_A + cols.start:E_A + cols.stop].astype(F32)
                put(4, rows, dyb * mixed * sz)
                put(6, rows, dyb * u * mixed * dsz)
                dmix = dyb * u * sz
                dmix16 = dmix.astype(BF16)
                acc_sb += jnp.sum(dmix, axis=1, keepdims=True)
                acc_sw += lax.dot_general(dmix16, vn16, (((1,), (1,)), ((), ())), preferred_element_type=F32)
                dvn = jnp.dot(wmt, dmix16, preferred_element_type=F32)
                acc_lg += jnp.sum(dvn * vhat, axis=0, keepdims=True)
                acc_lb += jnp.sum(dvn, axis=0, keepdims=True)
                dvh = dvn * lg
                put(5, rows, rstd * (dvh - jnp.mean(dvh, axis=-1, keepdims=True)
                                     - vhat * jnp.mean(dvh * vhat, axis=-1, keepdims=True)))
            dlg_ref[:, cols] += acc_lg
            dlb_ref[:, cols] += acc_lb
            dsw_ref[j] += jnp.where(causal, acc_sw, 0.0)
            dsb_ref[j] += acc_sb

    const3 = lambda i: (wl, 0, 0)
    const4 = lambda i: (wl, 0, 0, 0)
    fixed2 = lambda i: (0, 0)
    fixed3 = lambda i: (0, 0, 0)
    return pl.pallas_call(
        body, name=name, grid=(nsteps,),
        out_shape=[jax.ShapeDtypeStruct((S, 7 * E_A), BF16), jax.ShapeDtypeStruct((3, E_A), F32),
                   jax.ShapeDtypeStruct((1, E_A), F32), jax.ShapeDtypeStruct((1, E_A), F32),
                   jax.ShapeDtypeStruct((NDEV, CHUNK, CHUNK), F32), jax.ShapeDtypeStruct((NDEV, CHUNK, 1), F32)],
        in_specs=[pl.BlockSpec((TS, 7 * E_A), lambda i: (i, 0)), _halo_before(E_A, 0), _halo_before(E_A, 2),
                  _halo_after(E_A, 1), _halo_after(E_A, 3),
                  pl.BlockSpec((TS, 2 * E_A), lambda i: (i, 0)), _halo_after(E_A, 0),
                  pl.BlockSpec((None, 3, E_A), const3), pl.BlockSpec((None, 1, E_A), const3),
                  pl.BlockSpec((None, 1, E_A), const3), pl.BlockSpec((None, NDEV, CHUNK, CHUNK), const4),
                  pl.BlockSpec((None, NDEV, CHUNK, 1), const4)],
        out_specs=[pl.BlockSpec((TS, 7 * E_A), lambda i: (i, 0)), pl.BlockSpec((3, E_A), fixed2),
                   pl.BlockSpec((1, E_A), fixed2), pl.BlockSpec((1, E_A), fixed2),
                   pl.BlockSpec((NDEV, CHUNK, CHUNK), fixed3), pl.BlockSpec((NDEV, CHUNK, 1), fixed3)],
        compiler_params=_params(("arbitrary",)),
    )(proj, proj, proj, proj, proj, d_ycat, d_ycat, convw, ln_g3, ln_b3, sgu_w, sgu_bcol)


def _window_count(step, n, win, ext_before):
    rows = CHUNK if ext_before else CHUNK + HALO
    t = step * TS + n * CHUNK + lax.broadcasted_iota(jnp.int32, (rows, 1), 0)
    return jnp.minimum(t + 1, win).astype(F32)


def _pool_weight(wp_ref, g):
    return jnp.concatenate([wp_ref[d, g] for d in range(NDEV)], axis=0)


def _pooled_chunk(p, tail, win, count):
    sums = jnp.concatenate([tail, p], axis=0)
    shift = 1
    while shift < win:
        sums = sums + pltpu.roll(sums, shift, 0)
        shift *= 2
    return sums[HALO:] / count - p


def _pool_mix_fwd(proj, wpool, pscale4, wl, after, name):
    e_c = 4 * G_C

    def body(pj_ref, hp_ref, wp_ref, ps_ref, *rest):
        y_ref, pooled_scr, yraw_scr = rest[-3:]
        step = pl.program_id(0)
        live = (step > 0).astype(F32)
        for g, win in enumerate(POOL_WINDOWS):
            for q in range(G_C // LANE):
                cols = slice(g * G_C + q * LANE, g * G_C + (q + 1) * LANE)
                tail = hp_ref[:, cols].astype(F32) * live
                for n in range(NCH):
                    rows = slice(n * CHUNK, (n + 1) * CHUNK)
                    p = pj_ref[rows, cols].astype(F32)
                    pooled_scr[rows, q * LANE:(q + 1) * LANE] = _pooled_chunk(
                        p, tail, win, _window_count(step, n, win, True)).astype(BF16)
                    tail = p[CHUNK - HALO:]
            yraw_scr[...] = jnp.dot(pooled_scr[...], _pool_weight(wp_ref, g), preferred_element_type=F32)
            for q in range(G_C // LANE):
                cols = slice(g * G_C + q * LANE, g * G_C + (q + 1) * LANE)
                for n in range(NCH):
                    rows = slice(n * CHUNK, (n + 1) * CHUNK)
                    z = pj_ref[rows, e_c + cols.start:e_c + cols.stop].astype(F32)
                    y_ref[rows, cols] = (yraw_scr[rows, q * LANE:(q + 1) * LANE] * ps_ref[:, cols] * _silu(z)).astype(BF16)

    return pl.pallas_call(
        body, name=name, grid=(S // TS,), out_shape=jax.ShapeDtypeStruct((S, e_c), BF16),
        in_specs=[pl.BlockSpec((TS, 2 * e_c), lambda i: (i, 0)), _halo_before(e_c, 0),
                  pl.BlockSpec((NDEV, 4, G_C // NDEV, G_C), lambda i: (0, 0, 0, 0)),
                  pl.BlockSpec((None, 1, e_c), lambda i: (wl, 0, 0))] + [ANY] * len(after),
        out_specs=pl.BlockSpec((TS, e_c), lambda i: (i, 0)),
        scratch_shapes=[pltpu.VMEM((TS, G_C), BF16), pltpu.VMEM((TS, G_C), F32)],
        compiler_params=_params(("arbitrary",)),
    )(proj, proj, wpool, pscale4, *after)


def _pool_mix_bwd(proj, d_ycat, wpool, pscale4, wl, name):
    e_c = 4 * G_C
    nsteps = S // TS
    rb = G_C // NDEV

    def body(pj_ref, hp_ref, hz_ref, dy_ref, hdy_ref, wp_ref, ps_ref,
             dp_ref, dps_ref, dwp_ref, pooled_scr, yraw_scr, dyraw_scr, dpool_scr, acc_w):
        step = pl.program_id(0)

        @pl.when(step == 0)
        def _():
            dps_ref[...] = jnp.zeros_like(dps_ref)
            acc_w[...] = jnp.zeros_like(acc_w)

        live_before = (step > 0).astype(F32)
        live_after = (step < nsteps - 1).astype(F32)
        for g, win in enumerate(POOL_WINDOWS):
            weight = _pool_weight(wp_ref, g)
            for q in range(G_C // LANE):
                cols = slice(g * G_C + q * LANE, g * G_C + (q + 1) * LANE)
                tail = hp_ref[:, cols].astype(F32) * live_before
                for n in range(NCH):
                    rows = slice(n * CHUNK, (n + 1) * CHUNK)
                    p = pj_ref[rows, cols].astype(F32)
                    pooled_scr[rows, q * LANE:(q + 1) * LANE] = _pooled_chunk(
                        p, tail, win, _window_count(step, n, win, True)).astype(BF16)
                    tail = p[CHUNK - HALO:]
            yraw_scr[...] = jnp.dot(pooled_scr[...], weight, preferred_element_type=F32)
            for q in range(G_C // LANE):
                cols = slice(g * G_C + q * LANE, g * G_C + (q + 1) * LANE)
                local = slice(q * LANE, (q + 1) * LANE)
                scale = ps_ref[:, cols]
                acc_ps = jnp.zeros((1, LANE), F32)
                for n in range(NCH):
                    rows = slice(n * CHUNK, (n + 1) * CHUNK)
                    sz, dsz = _silu_and_grad(pj_ref[rows, e_c + cols.start:e_c + cols.stop].astype(F32))
                    dyv = dy_ref[rows, cols].astype(F32)
                    yraw = yraw_scr[rows, local]
                    dyraw_scr[rows, local] = (dyv * scale * sz).astype(BF16)
                    acc_ps += jnp.sum(dyv * yraw * sz, axis=0, keepdims=True)
                    dp_ref[rows, e_c + cols.start:e_c + cols.stop] = (dyv * yraw * scale * dsz).astype(BF16)
                dps_ref[:, cols] += acc_ps
                dyraw_scr[TS:, local] = (hdy_ref[:, cols].astype(F32) * scale * _silu(hz_ref[:, cols].astype(F32))
                                         * live_after).astype(BF16)
            dpool_scr[...] = lax.dot_general(dyraw_scr[...], weight, (((1,), (1,)), ((), ())), preferred_element_type=F32)
            acc_w[g] += lax.dot_general(pooled_scr[...], dyraw_scr[:TS, :], (((0,), (0,)), ((), ())),
                                        preferred_element_type=F32)
            for q in range(G_C // LANE):
                cols = slice(g * G_C + q * LANE, g * G_C + (q + 1) * LANE)
                local = slice(q * LANE, (q + 1) * LANE)
                for n in range(NCH):
                    rows = slice(n * CHUNK, (n + 1) * CHUNK)
                    ext = dpool_scr[n * CHUNK:(n + 1) * CHUNK + HALO, local]
                    sums = ext / _window_count(step, n, win, False)
                    shift = 1
                    while shift < win:
                        sums = sums + pltpu.roll(sums, CHUNK + HALO - shift, 0)
                        shift *= 2
                    dp_ref[rows, cols] = (sums[:CHUNK] - ext[:CHUNK]).astype(BF16)

        @pl.when(step == nsteps - 1)
        def _():
            for g in range(4):
                for d in range(NDEV):
                    dwp_ref[d % 2, d // 2, g] = acc_w[g, d * rb:(d + 1) * rb, :].astype(BF16)

    in_specs = [pl.BlockSpec((TS, 2 * e_c), lambda i: (i, 0)), _halo_before(e_c, 0), _halo_after(e_c, 1),
                pl.BlockSpec((TS, e_c), lambda i: (i, 0)), _halo_after(e_c, 0),
                pl.BlockSpec((NDEV, 4, rb, G_C), lambda i: (0, 0, 0, 0)),
                pl.BlockSpec((None, 1, e_c), lambda i: (wl, 0, 0))]
    args = [proj, proj, proj, d_ycat, d_ycat, wpool, pscale4]
    return pl.pallas_call(
        body, name=name, grid=(nsteps,),
        out_shape=[jax.ShapeDtypeStruct((S, 2 * e_c), BF16), jax.ShapeDtypeStruct((1, e_c), F32),
                   jax.ShapeDtypeStruct((2, NDEV // 2) + wpool.shape[1:], BF16)],
        in_specs=in_specs,
        out_specs=[pl.BlockSpec((TS, 2 * e_c), lambda i: (i, 0)), pl.BlockSpec((1, e_c), lambda i: (0, 0)),
                   pl.BlockSpec((2, NDEV // 2, 4, rb, G_C), lambda i: (0, 0, 0, 0, 0))],
        scratch_shapes=[pltpu.VMEM((TS, G_C), BF16), pltpu.VMEM((TS, G_C), F32), pltpu.VMEM((TS + HALO, G_C), BF16),
                        pltpu.VMEM((TS + HALO, G_C), F32), pltpu.VMEM((4, G_C, G_C), F32)],
        compiler_params=_params(("arbitrary",)),
    )(*args)


def _adamw(w, g, m, v):
    m = ADAM_B1 * m + (1.0 - ADAM_B1) * g
    v = ADAM_B2 * v + (1.0 - ADAM_B2) * jnp.square(g)
    m_hat = m / (1.0 - ADAM_B1 ** ADAM_STEP)
    v_hat = v / (1.0 - ADAM_B2 ** ADAM_STEP)
    delta = -ADAM_LR * (m_hat / (jnp.sqrt(v_hat) + ADAM_EPS) + ADAM_WD * w)
    return delta, m, v


def _adam_sharded(w, m, v, chip_parts, landed, my_chip, carried, name, first=0, into=()):
    _, nr, ncol = w.shape
    nl = len(chip_parts)
    tr = 256
    steps = nr // tr
    nc, ni = len(carried), len(into)

    def body(chip_ref, w_ref, m_ref, v_ref, *rest):
        parts, zones = rest[:nl], rest[nl:2 * nl]
        g_ref, d_ref, nm_ref, nv_ref = rest[2 * nl + nc + ni:2 * nl + nc + ni + 4]
        layer = pl.program_id(0)
        g = jnp.zeros((tr, ncol), F32)
        for l in range(nl):
            gl = parts[l][...].astype(F32)
            for q in range(3):
                gl = gl + zones[l][q].astype(F32)
            g = jnp.where(layer == l, gl, g)
        g_ref[...] = g
        d_ref[...], nm_ref[...], nv_ref[...] = _adamw(w_ref[...], g, m_ref[...], v_ref[...])

    def rows_of(l):
        return lambda layer, i, chip_ref: jnp.where(layer == l, i, jnp.where(layer < l, 0, steps - 1))

    spec = pl.BlockSpec((None, tr, ncol), lambda layer, i, chip_ref: (first + layer, i, 0))
    part_specs = [pl.BlockSpec((None, tr, ncol), lambda layer, i, chip_ref, l=l: (chip_ref[0], rows_of(l)(layer, i, chip_ref), 0))
                  for l in range(nl)]
    zone_specs = [pl.BlockSpec((3, tr, ncol), lambda layer, i, chip_ref, l=l: (0, rows_of(l)(layer, i, chip_ref), 0))
                  for l in range(nl)]
    grid_spec = pltpu.PrefetchScalarGridSpec(
        num_scalar_prefetch=1, grid=(nl, steps),
        in_specs=[spec, spec, spec] + part_specs + zone_specs + [ANY] * (nc + ni), out_specs=[spec] * 4 + [ANY] * nc)
    aliases = {4 + 2 * nl + k: 4 + k for k in range(nc)}
    aliases.update({4 + 2 * nl + nc + k: k for k in range(ni)})
    return pl.pallas_call(
        body, name=name, grid_spec=grid_spec,
        out_shape=[jax.ShapeDtypeStruct(w.shape, F32)] * 4 + [jax.ShapeDtypeStruct(a.shape, a.dtype) for a in carried],
        input_output_aliases=aliases, compiler_params=_params(("arbitrary", "arbitrary")),
    )(my_chip, w, m, v, *chip_parts, *landed, *carried, *into)


def _adam_small(params, name):
    n = len(params)

    def body(*refs):
        ins, outs = refs[:4 * n], refs[4 * n:]
        for k in range(n):
            w_ref, g_ref, m_ref, v_ref = ins[4 * k:4 * k + 4]
            outs[3 * k][...], outs[3 * k + 1][...], outs[3 * k + 2][...] = _adamw(w_ref[...], g_ref[...], m_ref[...], v_ref[...])

    outs = pl.pallas_call(body, name=name, out_shape=[jax.ShapeDtypeStruct(p[0].shape, F32) for p in params for _ in range(3)],
                          in_specs=[VMEM_FULL] * (4 * n), out_specs=[VMEM_FULL] * (3 * n),
                          compiler_params=_params())(*[a for p in params for a in p])
    return [list(outs[3 * k:3 * k + 3]) for k in range(n)]


def _sum_devices(gathered, name):
    _, nr, ncol = gathered.shape

    def body(g_ref, o_ref):
        acc = g_ref[0].astype(F32)
        for s in range(1, NDEV):
            acc = acc + g_ref[s].astype(F32)
        o_ref[...] = acc

    return pl.pallas_call(body, name=name, grid=(1,), out_shape=jax.ShapeDtypeStruct((nr, ncol), F32),
                          in_specs=[pl.BlockSpec((NDEV, nr, ncol), lambda i: (0, 0, 0))],
                          out_specs=pl.BlockSpec((nr, ncol), lambda i: (0, 0)),
                          compiler_params=_params(("arbitrary",)))(gathered)


def _ada_weight_adam(cact_t, dmod_mine, w, m, v):
    def body(ct_ref, dm_ref, w_ref, m_ref, v_ref, g_ref, d_ref, nm_ref, nv_ref):
        ct, dm = ct_ref[...], dm_ref[...]
        g = ct[:, 0:1] * dm[0:1, :]
        for e in range(1, NDEV):
            g = g + ct[:, e:e + 1] * dm[e:e + 1, :]
        g_ref[...] = g
        d_ref[...], nm_ref[...], nv_ref[...] = _adamw(w_ref[...], g, m_ref[...], v_ref[...])

    spec = pl.BlockSpec((None, D, ADA_NC), lambda l: (l, 0, 0))
    return pl.pallas_call(
        body, name="ada_weight_adam", grid=(DEPTH,), out_shape=[jax.ShapeDtypeStruct(w.shape, F32)] * 4,
        in_specs=[pl.BlockSpec((D, NDEV), lambda l: (0, 0)), pl.BlockSpec((None, NDEV, ADA_NC), lambda l: (l, 0, 0)),
                  spec, spec, spec],
        out_specs=[spec] * 4, compiler_params=_params(("arbitrary",)),
    )(cact_t, dmod_mine, w, m, v)


def _pad_rows(a, rows):
    a = a.reshape(-1, D)
    return jnp.pad(a, ((0, rows - a.shape[0]), (0, 0)))


def kernel(x, c, norm_g, ada_w, ada_b, ab_w_in, ab_conv_w, ab_ln_g, ab_ln_b, ab_sgu_w, ab_sgu_b, ab_w_out, c_w_in, c_pool_w, c_pool_scale, c_w_out, final_g, loss_target, m_norm_g, m_ada_w, m_ada_b, m_ab_w_in, m_ab_conv_w, m_ab_ln_g, m_ab_ln_b, m_ab_sgu_w, m_ab_sgu_b, m_ab_w_out, m_c_w_in, m_c_pool_w, m_c_pool_scale, m_c_w_out, m_final_g, v_norm_g, v_ada_w, v_ada_b, v_ab_w_in, v_ab_conv_w, v_ab_ln_g, v_ab_ln_b, v_ab_sgu_w, v_ab_sgu_b, v_ab_w_out, v_c_w_in, v_c_pool_w, v_c_pool_scale, v_c_w_out, v_final_g):
    x_pos, y_pos, c_pos = _position()
    me = _index((x_pos, y_pos, c_pos))
    core = c_pos.astype(jnp.int32).reshape(1)
    my_chip = (2 * x_pos + y_pos).astype(jnp.int32).reshape(1)
    me1 = me.astype(jnp.int32).reshape(1)
    x0 = x.reshape(S, D)
    target = loss_target.reshape(S, D)
    norm_g3 = norm_g.reshape(DEPTH, 1, D)
    ln_g3, ln_b3 = ab_ln_g.reshape(2, 1, E_A), ab_ln_b.reshape(2, 1, E_A)
    sgu_bcol = ab_sgu_b.reshape(2, NDEV, CHUNK, 1)
    rb = G_C // NDEV
    pool_w3, m_pool_w3, v_pool_w3 = (a.reshape(2, 4 * rb, G_C) for a in (c_pool_w, m_c_pool_w, v_c_pool_w))

    cact_all, mod, (convw_all, pscale_all) = _ada_forward(c, ada_w, ada_b, [ab_conv_w, c_pool_scale])
    convw = jnp.transpose(convw_all, (1, 2, 0, 3)).reshape(2, 3, E_A)
    pscale4 = jnp.transpose(pscale_all, (1, 0, 2)).reshape(2, 1, 4 * G_C)
    zones = []
    for layer in range(DEPTH):
        wl = layer // 2
        if layer % 2 == 0:
            zones.append([_to_zone(ab_w_in, wl, me1, BF16, f"cast_w_in_{layer}"), _to_zone(ab_w_out, wl, me1, BF16, f"cast_w_out_{layer}")])
        else:
            zones.append([_to_zone(c_w_in, wl, me1, BF16, f"cast_w_in_{layer}"), _to_zone(c_w_out, wl, me1, BF16, f"cast_w_out_{layer}"),
                          _to_zone(pool_w3, wl, me1, BF16, f"cast_pool_w_{layer}")])

    def gathered(flight, after, layer):
        wg = _gather_end(flight, [after], f"gather_end_{layer}")
        return [wg[0], wg[1].reshape(-1, D)] + [w.reshape(NDEV, 4, rb, G_C) for w in wg[2:]]

    flight, (mod,) = _gather_start(zones[0], [convw_all], [mod], "gather_start_0")
    flight, (mod,) = _gather_mid(flight, [z for zs in zones[1:] for z in zs], [mod], "gather_mid_0")
    next_flight, (mod,) = _gather_start(zones[1], [], [mod], "gather_start_1")
    flight = _gather_late(flight, [mod], "gather_late_0")
    xs, hts, projs, ycats, outs, gathered_w = [x0], [], [], [], [], [gathered(flight, mod, 0)]
    for layer in range(DEPTH):
        wl = layer // 2
        even = layer % 2 == 0
        wg = gathered_w[layer]
        h_t, proj = _norm_proj(xs[-1], mod, norm_g3, wg[0], layer, f"norm_proj_{layer}")
        if layer + 1 < DEPTH:
            flight, (h_t,) = _gather_mid(next_flight, [], [h_t], f"gather_mid_{layer + 1}")
            if layer + 2 < DEPTH:
                next_flight, (h_t,) = _gather_start(zones[layer + 2], [], [h_t], f"gather_start_{layer + 2}")
        if even:
            ycat = _even_mix_fwd(proj, convw, ln_g3, ln_b3, ab_sgu_w, sgu_bcol, wl, [h_t], f"even_mix_fwd_{layer}")
        else:
            ycat = _pool_mix_fwd(proj, wg[2], pscale4, wl, [h_t], f"pool_mix_fwd_{layer}")
        if layer + 1 < DEPTH:
            flight = _gather_late(flight, [ycat], f"gather_late_{layer + 1}")
        if layer + 1 < DEPTH:
            x_new, out = _out_proj(ycat, wg[1], xs[-1], mod, layer, f"out_proj_{layer}")
            gathered_w.append(gathered(flight, x_new, layer + 1))
            xs.append(x_new)
        else:
            dx, out, loss_part, d_final_g = _out_proj(ycat, wg[1], xs[-1], mod, layer, f"out_proj_{layer}",
                                                      final=(target, final_g.reshape(1, D)))
        hts.append(h_t)
        projs.append(proj)
        ycats.append(ycat)
        outs.append(out)

    d_mod, d_norm_g = [None] * DEPTH, [None] * DEPTH
    small, scatters, landed, res = {}, {}, {}, {}

    def finish_scatter(layer, after):
        send_sems, recv_sems, chip_parts, zones = scatters[layer]
        landed[layer] = _scatter_end(chip_parts, zones, send_sems, recv_sems, after, f"scatter_end_{layer}")

    def flat(a):
        return a.reshape(a.shape[0], -1, a.shape[-1])

    def sharded_adam(k, j, layers, w, m, v, carried, first=0, into=()):
        outs4 = _adam_sharded(w, m, v, [flat(landed[l][0][j]) for l in layers], [flat(landed[l][1][j]) for l in layers],
                              my_chip, carried, f"adam_{k}_{first}" if len(layers) < w.shape[0] else "adam_" + k, first, into)
        res[k] = [o.reshape(c_pool_w.shape) if k == "c_pool_w" else o for o in outs4[:4]]
        return list(outs4[4:])

    previous = None
    for layer in reversed(range(DEPTH)):
        wl = layer // 2
        even = layer % 2 == 0
        wg = gathered_w[layer]
        carried = [] if previous is None else [scatters[previous][2][0]]
        d_ycat, grad_out, d_gate, carried = _out_bwd(dx, outs[layer], ycats[layer], wg[1], mod, layer, carried, f"out_bwd_{layer}")
        if previous is not None:
            scatters[previous][2][0] = carried[0]
        parts = [None, grad_out]
        if even:
            d_proj, d_cw, d_lg, d_lb, d_sw, d_sb = _even_mix_bwd(
                projs[layer], d_ycat, convw, ln_g3, ln_b3, ab_sgu_w, sgu_bcol, wl, f"even_mix_bwd_{layer}")
            small[layer] = (d_cw, d_lg, d_lb, d_sw, d_sb)
        else:
            d_proj, d_ps, d_pool = _pool_mix_bwd(projs[layer], d_ycat, wg[2], pscale4, wl, f"pool_mix_bwd_{layer}")
            small[layer] = (d_ps,)
            parts.append(d_pool)
        parts[0] = _weight_grad(hts[layer], d_proj, f"grad_w_in_{layer}")
        pair_send, pair_recv, parts, from_sibling = _pair_start(parts, f"pair_start_{layer}")
        if layer > 0:
            dx, d_shift, d_scale, d_norm_g[layer], parts[0] = _dh_norm_bwd(
                d_proj, wg[0], xs[layer], dx, mod, norm_g3, layer, parts[0], f"dh_norm_bwd_{layer}")
            pair_after = dx
        else:
            finish_scatter(1, d_proj)
            finish_scatter(3, d_proj)
            parts[0], = sharded_adam("c_w_out", 1, (1, 3), c_w_out, m_c_w_out, v_c_w_out, [parts[0]])
            parts[0], = sharded_adam("c_pool_w", 2, (1, 3), pool_w3, m_pool_w3, v_pool_w3, [parts[0]])
            pair_after = res["c_pool_w"][0]
        parts, from_sibling = _pair_end(parts, from_sibling, pair_send, pair_recv, pair_after, f"pair_end_{layer}")
        chip_parts = _pair_sum(parts, from_sibling, core, f"pair_sum_{layer}")
        send_sems, recv_sems, chip_parts, zones = _scatter_start(chip_parts, f"scatter_start_{layer}")
        if layer == 0:
            chip_parts[0], = sharded_adam("c_w_in", 0, (1, 3), c_w_in, m_c_w_in, v_c_w_in, [chip_parts[0]])
            dx, d_shift, d_scale, d_norm_g[layer], chip_parts[0] = _dh_norm_bwd(
                d_proj, wg[0], xs[layer], dx, mod, norm_g3, layer, chip_parts[0], f"dh_norm_bwd_{layer}")
        scatters[layer] = [send_sems, recv_sems, chip_parts, zones]
        previous = layer
        d_mod[layer] = jnp.concatenate([d_shift, d_scale, d_gate], axis=0)
    grad_x = dx.reshape(x.shape)

    sections = [("norm_g", jnp.concatenate(d_norm_g, axis=0), 8),
                ("d_mod", jnp.concatenate(d_mod, axis=0), 16),
                ("ab_ln_g", jnp.concatenate([small[0][1], small[2][1]], axis=0), 8),
                ("ab_ln_b", jnp.concatenate([small[0][2], small[2][2]], axis=0), 8),
                ("ab_sgu_b", jnp.stack([small[0][4], small[2][4]]), 8),
                ("final_g", d_final_g, 8),
                ("ab_conv_w", jnp.stack([small[0][0], small[2][0]]), 8),
                ("c_pool_scale", jnp.concatenate([small[1][0], small[3][0]], axis=0), 8),
                ("ab_sgu_w", jnp.stack([small[0][3], small[2][3]]), 256)]
    offsets, at = {}, 0
    for name, _, rows in sections:
        offsets[name] = (at, rows)
        at += rows
    packed = jnp.concatenate([_pad_rows(a, rows) for _, a, rows in sections] + [jnp.zeros((-at % 32, D), F32)], axis=0)
    loss_rows = jnp.pad(loss_part, ((0, 15), (0, D - LANE)))
    small_zones = [_to_zone(packed[None], 0, me1, BF16, "place_small_grads"), _to_zone(loss_rows[None], 0, me1, F32, "place_loss")]
    small_flight, (mod,) = _gather_start(small_zones, [], [mod], "gather_small_start")

    finish_scatter(2, mod)
    sharded_adam("ab_w_out", 1, (2,), ab_w_out, m_ab_w_out, v_ab_w_out, [], first=1)
    sharded_adam("ab_w_in", 0, (2,), ab_w_in, m_ab_w_in, v_ab_w_in, [], first=1)
    finish_scatter(0, res["ab_w_in"][0])
    sharded_adam("ab_w_out", 1, (0,), ab_w_out, m_ab_w_out, v_ab_w_out, [], into=res["ab_w_out"])
    small_flight, (mod,) = _gather_mid(small_flight, [res["ab_w_out"][0]], [mod], "gather_small_mid")
    sharded_adam("ab_w_in", 0, (0,), ab_w_in, m_ab_w_in, v_ab_w_in, [mod], into=res["ab_w_in"])

    last = res["ab_w_in"][0]
    small_flight = _gather_late(small_flight, [last], "gather_small_late")
    small_grads, losses = _gather_end(small_flight, [last], "gather_small_end")
    summed = _sum_devices(small_grads, "sum_small_grads")
    loss = _sum_devices(losses, "sum_loss")[0, 0]

    def section(name, nrows, src=summed):
        start = offsets[name][0]
        return src[..., start:start + nrows, :]

    grads = {
        "norm_g": section("norm_g", DEPTH),
        "ada_b": section("d_mod", 3 * DEPTH).reshape(DEPTH, 3 * D),
        "ab_ln_g": section("ab_ln_g", 2), "ab_ln_b": section("ab_ln_b", 2),
        "ab_sgu_b": section("ab_sgu_b", 2).reshape(ab_sgu_b.shape),
        "final_g": section("final_g", 1),
        "ab_sgu_w": section("ab_sgu_w", 256).reshape(ab_sgu_w.shape),
        "ab_conv_w": lax.dynamic_slice_in_dim(section("ab_conv_w", 6).reshape(2, 3, E_A), me * HEAD, HEAD, axis=2),
        "c_pool_scale": lax.dynamic_slice_in_dim(section("c_pool_scale", 4).reshape(2, 4 * G_C), me * 256, 256, axis=1),
    }
    small_w = {"norm_g": (norm_g, m_norm_g, v_norm_g), "ada_b": (ada_b, m_ada_b, v_ada_b),
               "ab_ln_g": (ab_ln_g, m_ab_ln_g, v_ab_ln_g), "ab_ln_b": (ab_ln_b, m_ab_ln_b, v_ab_ln_b),
               "ab_sgu_b": (ab_sgu_b, m_ab_sgu_b, v_ab_sgu_b),
               "final_g": (final_g.reshape(1, D), m_final_g.reshape(1, D), v_final_g.reshape(1, D)),
               "ab_sgu_w": (ab_sgu_w, m_ab_sgu_w, v_ab_sgu_w), "ab_conv_w": (ab_conv_w, m_ab_conv_w, v_ab_conv_w),
               "c_pool_scale": (c_pool_scale, m_c_pool_scale, v_c_pool_scale)}
    updates = _adam_small([(w, grads[k], m, v) for k, (w, m, v) in small_w.items()], "adam_small")
    for k, update in zip(small_w, updates):
        res[k] = [grads[k]] + update
    res["final_g"] = [a.reshape(D) for a in res["final_g"]]

    dmod_all = section("d_mod", 3 * DEPTH, small_grads).reshape(NDEV, DEPTH, 3 * D)
    dmod_mine = jnp.transpose(lax.dynamic_slice_in_dim(dmod_all, me * ADA_NC, ADA_NC, axis=2), (1, 0, 2)).astype(F32)
    res["ada_w"] = _ada_weight_adam(jnp.transpose(cact_all.reshape(NDEV, D)), dmod_mine, ada_w, m_ada_w, v_ada_w)

    order = ["norm_g", "ada_w", "ada_b", "ab_w_in", "ab_conv_w", "ab_ln_g", "ab_ln_b", "ab_sgu_w", "ab_sgu_b",
             "ab_w_out", "c_w_in", "c_pool_w", "c_pool_scale", "c_w_out", "final_g"]
    return (loss, grad_x, *[res[k][0] for k in order], *[res[k][1] for k in order],
            *[res[k][2] for k in order], *[res[k][3] for k in order])
```

```python
import jax
import jax.numpy as jnp
from jax import lax
from jax.experimental import pallas as pl
from jax.experimental.pallas import tpu as pltpu

F32, BF16 = jnp.float32, jnp.bfloat16
S, D = 2048, 1024
NDEV = 8
DEPTH = 4
EPS = 1e-6
E_A = 1024
HEAD = 128
CHUNK = 128
POOL_WINDOWS = (2, 4, 8, 16)
G_C = 512
HALO = 16
ADA_NC = 384
MIB = 1024 * 1024
LANE = 128

ADAM_LR, ADAM_B1, ADAM_B2, ADAM_EPS, ADAM_WD, ADAM_STEP = 0.001, 0.9, 0.999, 1e-08, 0.01, 10

ANY = pl.BlockSpec(memory_space=pl.ANY)
VMEM_FULL = pl.BlockSpec(memory_space=pltpu.VMEM)
IN_HBM = pl.BlockSpec(memory_space=pltpu.HBM)
SEMAPHORES = pl.BlockSpec(memory_space=pltpu.SEMAPHORE)
IN_FLIGHT = pltpu.SideEffectType.DATAFLOW_SIDE_EFFECTING


V7X_VMEM_MIB = 64
VMEM_LIMIT_MIB = V7X_VMEM_MIB - 4


def _params(semantics=None):
    return pltpu.CompilerParams(dimension_semantics=semantics, vmem_limit_bytes=VMEM_LIMIT_MIB * MIB)


def _silu(z):
    return z * jax.nn.sigmoid(z)


def _silu_and_grad(z):
    sig = jax.nn.sigmoid(z)
    return z * sig, sig * (1.0 + z * (1.0 - sig))


def _position():
    return lax.axis_index("x"), lax.axis_index("y"), lax.axis_index("c")


def _index(pos):
    return 4 * pos[0] + 2 * pos[1] + pos[2]


def _peer(pos, k):
    flipped = tuple(1 - p if (k >> (2 - b)) & 1 else p for b, p in enumerate(pos))
    return flipped, _index(flipped)


def _remote(src, dst, send_sem, recv_sem, device):
    return pltpu.make_async_remote_copy(src_ref=src, dst_ref=dst, send_sem=send_sem, recv_sem=recv_sem,
                                        device_id=device, device_id_type=pl.DeviceIdType.MESH)


def _pair_start(parts, name):
    n = len(parts)
    lands = [_in_hbm(lax.empty(p.shape[1:], p.dtype)) for p in parts]

    def body(*refs):
        ins, zones = refs[:n], refs[n:2 * n]
        send_sems, recv_sems = refs[2 * n:2 * n + 2]
        x, y, c = _position()
        for j in range(n):
            _remote(ins[j].at[1 - c], zones[j], send_sems.at[j], recv_sems.at[j], (x, y, 1 - c)).start()

    outs = pl.pallas_call(
        body, name=name,
        out_shape=(pltpu.SemaphoreType.DMA((n,)), pltpu.SemaphoreType.DMA((n,)),
                   *[pltpu.HBM(p.shape, p.dtype) for p in parts], *[pltpu.HBM(z.shape, z.dtype) for z in lands]),
        in_specs=[IN_HBM] * (2 * n), out_specs=(SEMAPHORES, SEMAPHORES, *[IN_HBM] * (2 * n)),
        input_output_aliases={j: 2 + j for j in range(2 * n)},
        compiler_params=pltpu.CompilerParams(has_side_effects=IN_FLIGHT),
    )(*[_in_hbm(p) for p in parts], *lands)
    return outs[0], outs[1], list(outs[2:2 + n]), list(outs[2 + n:])


def _pair_end(parts, zones, send_sems, recv_sems, after, name):
    n = len(parts)

    def body(*refs):
        ins, zs = refs[:n], refs[n:2 * n]
        s, r = refs[2 * n:2 * n + 2]
        me = _position()
        for j in range(n):
            copy = _remote(ins[j].at[0], zs[j], s.at[j], r.at[j], me)
            copy.wait_send()
            copy.wait_recv()

    outs = pl.pallas_call(
        body, name=name,
        out_shape=(*[pltpu.HBM(p.shape, p.dtype) for p in parts], *[pltpu.HBM(z.shape, z.dtype) for z in zones]),
        in_specs=[IN_HBM] * (2 * n) + [SEMAPHORES, SEMAPHORES, ANY], out_specs=tuple([IN_HBM] * (2 * n)),
        input_output_aliases={j: j for j in range(2 * n)},
        compiler_params=pltpu.CompilerParams(has_side_effects=IN_FLIGHT),
    )(*parts, *zones, send_sems, recv_sems, after)
    return list(outs[:n]), list(outs[n:])


def _pair_sum(parts, from_sibling, core, name):
    n = len(parts)
    steps = 8
    p3 = [p.reshape(2, -1, p.shape[-1]) for p in parts]
    q2 = [q.reshape(-1, q.shape[-1]) for q in from_sibling]

    def body(core_ref, *refs):
        for p_ref, q_ref, o_ref in zip(refs[:n], refs[n:2 * n], refs[2 * n:]):
            o_ref[...] = (p_ref[...].astype(F32) + q_ref[...].astype(F32)).astype(BF16)

    tiles = [pl.BlockSpec((q.shape[0] // steps, q.shape[1]), lambda i, core_ref: (i, 0)) for q in q2]
    grid_spec = pltpu.PrefetchScalarGridSpec(
        num_scalar_prefetch=1, grid=(steps,),
        in_specs=[pl.BlockSpec((None, q.shape[0] // steps, q.shape[1]), lambda i, core_ref: (core_ref[0], i, 0)) for q in q2]
        + tiles, out_specs=tiles)
    outs = pl.pallas_call(body, name=name, grid_spec=grid_spec, out_shape=[jax.ShapeDtypeStruct(q.shape, BF16) for q in q2],
                          compiler_params=_params(("arbitrary",)))(core, *p3, *q2)
    return [o.reshape(q.shape) for o, q in zip(outs, from_sibling)]


def _in_hbm(a):
    return pltpu.with_memory_space_constraint(a, pltpu.HBM)


def _chips(x, y):
    return [(1 - x, y), (x, 1 - y), (1 - x, 1 - y)]


def _to_zone(a, wl, me, dtype, name):
    _, rows, cols = a.shape
    tr = 256 if rows % 256 == 0 else rows

    def body(me_ref, a_ref, o_ref):
        o_ref[...] = a_ref[...].astype(dtype)

    grid_spec = pltpu.PrefetchScalarGridSpec(
        num_scalar_prefetch=1, grid=(rows // tr,),
        in_specs=[pl.BlockSpec((None, tr, cols), lambda i, me_ref: (wl, i, 0))],
        out_specs=pl.BlockSpec((None, tr, cols), lambda i, me_ref: (me_ref[0], i, 0)))
    return pl.pallas_call(body, name=name, grid_spec=grid_spec, out_shape=jax.ShapeDtypeStruct((NDEV, rows, cols), dtype),
                          compiler_params=_params(("arbitrary",)))(me, a)


def _halves(block):
    rows = block.shape[0] // 2
    return block.at[pl.ds(0, rows)], block.at[pl.ds(rows, rows)]


def _around(x, y, c):
    return (x, y, 1 - c), (1 - x, y, c), (x, 1 - y, c), (1 - x, 1 - y, c)


def _gather_step1(zs, send, recv, pos):
    sibling, xn, yn, _ = _around(*pos)
    for j, z in enumerate(zs):
        mine = z.at[_index(pos)]
        for k, peer in enumerate((sibling, xn, yn)):
            _remote(mine, mine, send.at[3 * j + k], recv.at[3 * j + k], peer).start()


def _gather_step2(zs, recv1, send, recv, pos):
    sibling, xn, yn, _ = _around(*pos)
    for j, z in enumerate(zs):
        xb, yb = z.at[_index(xn)], z.at[_index(yn)]
        _remote(xb, xb, send.at[4 * j], recv1.at[3 * j + 1], pos).wait_recv()
        _remote(yb, yb, send.at[4 * j], recv1.at[3 * j + 2], pos).wait_recv()
        _remote(xb, xb, send.at[4 * j], recv.at[4 * j], sibling).start()
        _remote(yb, yb, send.at[4 * j + 1], recv.at[4 * j + 1], sibling).start()
        first, second = _halves(xb)[0], _halves(yb)[1]
        _remote(first, first, send.at[4 * j + 2], recv.at[4 * j + 2], yn).start()
        _remote(second, second, send.at[4 * j + 3], recv.at[4 * j + 3], xn).start()


def _gather_step3(zs, recv2, send, recv, pos):
    sibling, _, _, diagonal = _around(*pos)
    for j, z in enumerate(zs):
        db = z.at[_index(diagonal)]
        first, second = _halves(db)
        _remote(first, first, send.at[j], recv2.at[4 * j + 2], pos).wait_recv()
        _remote(second, second, send.at[j], recv2.at[4 * j + 3], pos).wait_recv()
        _remote(db, db, send.at[j], recv.at[j], sibling).start()


def _gather_step4(zs, send1, recv1, send2, recv2, send3, recv3, pos):
    x, y, c = pos
    sibling = (x, y, 1 - c)
    _, sx, sy, sd = _around(*sibling)
    for j, z in enumerate(zs):
        for owner, send, recv, k in ((sibling, send1, recv1, 3 * j), (sx, send2, recv2, 4 * j), (sy, send2, recv2, 4 * j + 1),
                                     (sd, send3, recv3, j)):
            block = z.at[_index(owner)]
            _remote(block, block, send.at[k], recv.at[k], pos).wait_recv()
    for j, z in enumerate(zs):
        block = z.at[0]
        half = _halves(block)[0]
        for ref, send, recv, k in ([(block, send1, recv1, 3 * j + k) for k in range(3)]
                                   + [(block, send2, recv2, 4 * j), (block, send2, recv2, 4 * j + 1),
                                      (half, send2, recv2, 4 * j + 2), (half, send2, recv2, 4 * j + 3), (block, send3, recv3, j)]):
            _remote(ref, ref, send.at[k], recv.at[k], pos).wait_send()


def _flight_call(step, name, zones, sems_in, nsems_out, after, carried):
    n, m, k = len(zones), len(carried), len(sems_in)

    def body(*refs):
        zs = refs[:n]
        given = refs[n + m:n + m + k]
        made = refs[n + m + k + len(after):n + m + k + len(after) + (2 if nsems_out else 0)]
        step(zs, *given, *made, _position())

    sem_out = (pltpu.SemaphoreType.DMA((nsems_out,)),) * 2 if nsems_out else ()
    outs = pl.pallas_call(
        body, name=name,
        out_shape=(*sem_out, *[pltpu.HBM(z.shape, z.dtype) for z in zones], *[jax.ShapeDtypeStruct(a.shape, a.dtype) for a in carried]),
        in_specs=[IN_HBM] * n + [ANY] * m + [SEMAPHORES] * k + [ANY] * len(after),
        out_specs=(*[SEMAPHORES] * len(sem_out), *[IN_HBM] * n, *[ANY] * m),
        input_output_aliases={j: len(sem_out) + j for j in range(n + m)},
        compiler_params=pltpu.CompilerParams(has_side_effects=IN_FLIGHT),
    )(*[_in_hbm(z) for z in zones], *carried, *sems_in, *after)
    sems = list(outs[:len(sem_out)])
    return sems, list(outs[len(sem_out):len(sem_out) + n]), list(outs[len(sem_out) + n:])


def _gather_start(zones, after, carried, name):
    (send1, recv1), zones, carried = _flight_call(_gather_step1, name, zones, [], 3 * len(zones), after, carried)
    return {"s1": send1, "r1": recv1, "zones": zones}, carried


def _gather_mid(flight, after, carried, name):
    step = lambda zs, recv1, send, recv, pos: _gather_step2(zs, recv1, send, recv, pos)
    (send2, recv2), zones, carried = _flight_call(step, name, flight["zones"], [flight["r1"]], 4 * len(flight["zones"]), after, carried)
    return {**flight, "s2": send2, "r2": recv2, "zones": zones}, carried


def _gather_late(flight, after, name):
    step = lambda zs, recv2, send, recv, pos: _gather_step3(zs, recv2, send, recv, pos)
    (send3, recv3), zones, _ = _flight_call(step, name, flight["zones"], [flight["r2"]], len(flight["zones"]), after, [])
    return {**flight, "s3": send3, "r3": recv3, "zones": zones}


def _gather_end(flight, after, name):
    sems = [flight[k] for k in ("s1", "r1", "s2", "r2", "s3", "r3")]
    _, zones, _ = _flight_call(_gather_step4, name, flight["zones"], sems, 0, after, [])
    return zones


def _scatter_start(parts, name):
    n = len(parts)
    lands = [_in_hbm(lax.empty((3,) + p.shape[1:], p.dtype)) for p in parts]

    def body(*refs):
        ins, zones = refs[:n], refs[n:2 * n]
        send_sems, recv_sems = refs[2 * n:2 * n + 2]
        x, y, c = _position()
        for j in range(n):
            for q, (px, py) in enumerate(_chips(x, y)):
                _remote(ins[j].at[2 * px + py], zones[j].at[q], send_sems.at[3 * j + q], recv_sems.at[3 * j + q],
                        (px, py, c)).start()

    outs = pl.pallas_call(
        body, name=name,
        out_shape=(pltpu.SemaphoreType.DMA((3 * n,)), pltpu.SemaphoreType.DMA((3 * n,)),
                   *[pltpu.HBM(p.shape, p.dtype) for p in parts], *[pltpu.HBM(z.shape, z.dtype) for z in lands]),
        in_specs=[IN_HBM] * (2 * n), out_specs=(SEMAPHORES, SEMAPHORES, *[IN_HBM] * (2 * n)),
        input_output_aliases={j: 2 + j for j in range(2 * n)},
        compiler_params=pltpu.CompilerParams(has_side_effects=IN_FLIGHT),
    )(*[_in_hbm(p) for p in parts], *lands)
    return outs[0], outs[1], list(outs[2:2 + n]), list(outs[2 + n:])


def _scatter_end(parts, zones, send_sems, recv_sems, after, name):
    n = len(parts)

    def body(*refs):
        ins, zs = refs[:n], refs[n:2 * n]
        s, r = refs[2 * n:2 * n + 2]
        me = _position()
        for j in range(n):
            for q in range(3):
                copy = _remote(ins[j].at[0], zs[j].at[q], s.at[3 * j + q], r.at[3 * j + q], me)
                copy.wait_send()
                copy.wait_recv()

    outs = pl.pallas_call(
        body, name=name,
        out_shape=(*[pltpu.HBM(p.shape, p.dtype) for p in parts], *[pltpu.HBM(z.shape, z.dtype) for z in zones]),
        in_specs=[IN_HBM] * (2 * n) + [SEMAPHORES, SEMAPHORES, ANY], out_specs=tuple([IN_HBM] * (2 * n)),
        input_output_aliases={j: j for j in range(2 * n)},
        compiler_params=pltpu.CompilerParams(has_side_effects=IN_FLIGHT),
    )(*parts, *zones, send_sems, recv_sems, after)
    return list(outs[:n]), list(outs[n:])


def _ada_forward(c, ada_w, ada_b, small):
    ns = len(small)

    def body(c_ref, w_ref, b_ref, *rest):
        small_refs, (cact_ref, mod_ref), gathered = rest[:ns], rest[ns:ns + 2], rest[ns + 2:2 * ns + 2]
        gbuf, modrow, send_sems, recv_sems = rest[2 * ns + 2:]
        pos = _position()
        me = _index(pos)

        def to_all(*exchanged):
            copies = []
            for ref, row in exchanged:
                for k in range(1, NDEV):
                    peer, _ = _peer(pos, k)
                    copy = pltpu.make_async_remote_copy(
                        src_ref=ref.at[me], dst_ref=ref.at[me], send_sem=send_sems.at[row, k - 1],
                        recv_sem=recv_sems.at[row, k - 1], device_id=peer, device_id_type=pl.DeviceIdType.MESH)
                    copy.start()
                    copies.append(copy)
            for copy in copies:
                copy.wait()

        cact_ref[me] = _silu(c_ref[...])
        for j in range(ns):
            gathered[j][me] = small_refs[j][...]
        to_all((cact_ref, 0), *[(gathered[j], 2 + j) for j in range(ns)])
        rows = lax.broadcasted_iota(jnp.int32, (NDEV, D), 0)
        cact = jnp.zeros((NDEV, D), F32)
        for e in range(NDEV):
            cact = jnp.where(rows == e, cact_ref[e], cact)
        cact = cact.astype(BF16)
        for l in range(DEPTH):
            gbuf[me, l] = jnp.dot(cact, w_ref[l].astype(BF16), preferred_element_type=F32)
        to_all((gbuf, 1))
        mine = lax.broadcasted_iota(jnp.int32, (NDEV, ADA_NC), 0) == me
        for l in range(DEPTH):
            for d in range(NDEV):
                modrow[:, d * ADA_NC:(d + 1) * ADA_NC] = jnp.sum(jnp.where(mine, gbuf[d, l], 0.0), axis=0, keepdims=True)
            full = modrow[...] + b_ref[l:l + 1, :]
            for w in range(3):
                mod_ref[l, w] = full[:, w * D:(w + 1) * D]

    outs = pl.pallas_call(
        body, name="ada_forward",
        out_shape=[jax.ShapeDtypeStruct((NDEV, 1, D), F32), jax.ShapeDtypeStruct((DEPTH, 3, 1, D), F32)]
        + [jax.ShapeDtypeStruct((NDEV,) + a.shape, a.dtype) for a in small],
        in_specs=[VMEM_FULL] * (3 + ns), out_specs=[VMEM_FULL] * (2 + ns),
        scratch_shapes=[pltpu.VMEM((NDEV, DEPTH, NDEV, ADA_NC), F32), pltpu.VMEM((1, 3 * D), F32),
                        pltpu.SemaphoreType.DMA((2 + ns, NDEV - 1)), pltpu.SemaphoreType.DMA((2 + ns, NDEV - 1))],
        compiler_params=_params(),
    )(c, ada_w, ada_b, *small)
    return outs[0], outs[1], list(outs[2:])


def _mod_spec(layer, which, ngrid):
    index = {1: lambda i: (layer, which, 0, 0), 2: lambda i, j: (layer, which, 0, 0)}[ngrid]
    return pl.BlockSpec((None, None, 1, D), index)


W_BLOCKS = 4


def _norm_proj(x, mod, norm_g3, wg, layer, name):
    nb = wg.shape[-1]
    tm = 1024
    wb = W_BLOCKS

    def body(x_ref, g_ref, shift_ref, scale_ref, w_ref, ht_ref, p_ref, h_ref):
        @pl.when(pl.program_id(1) == 0)
        def _():
            xv = x_ref[...]
            r = lax.rsqrt(jnp.mean(xv * xv, axis=-1, keepdims=True) + EPS)
            hn = xv * r * g_ref[...]
            h = hn * (1.0 + scale_ref[...]) + shift_ref[...]
            h_ref[...] = h.astype(BF16)
            ht_ref[...] = h.T.astype(BF16)

        hv = h_ref[...]
        for b in range(wb):
            p_ref[:, b * nb:(b + 1) * nb] = jnp.dot(hv, w_ref[b], preferred_element_type=F32).astype(BF16)

    return pl.pallas_call(
        body, name=name, grid=(S // tm, NDEV // wb),
        out_shape=[jax.ShapeDtypeStruct((D, S), BF16), jax.ShapeDtypeStruct((S, NDEV * nb), BF16)],
        in_specs=[pl.BlockSpec((tm, D), lambda i, d: (i, 0)),
                  pl.BlockSpec((None, 1, D), lambda i, d: (layer, 0, 0)),
                  _mod_spec(layer, 0, 2), _mod_spec(layer, 1, 2),
                  pl.BlockSpec((wb, D, nb), lambda i, d: (d, 0, 0))],
        out_specs=[pl.BlockSpec((D, tm), lambda i, d: (0, i)), pl.BlockSpec((tm, wb * nb), lambda i, d: (i, d))],
        scratch_shapes=[pltpu.VMEM((tm, D), BF16)],
        compiler_params=_params(("arbitrary", "arbitrary")),
    )(x, norm_g3, mod, mod, wg)


def _out_proj(ycat, w_out, x, mod, layer, name, final=None):
    tm = 512
    e = w_out.shape[0]

    def body(y_ref, w_ref, x_ref, gate_ref, *rest):
        acc = jnp.dot(y_ref[...], w_ref[...], preferred_element_type=F32)
        xv = x_ref[...] + gate_ref[...] * acc
        if final is None:
            xn_ref, o_ref = rest
            o_ref[...] = acc.astype(BF16)
            xn_ref[...] = xv
            return
        t_ref, g_ref, dx_ref, o_ref, loss_ref, dg_ref = rest
        o_ref[...] = acc.astype(BF16)

        @pl.when(pl.program_id(0) == 0)
        def _():
            loss_ref[...] = jnp.zeros_like(loss_ref)
            dg_ref[...] = jnp.zeros_like(dg_ref)

        g = g_ref[...]
        r = lax.rsqrt(jnp.mean(xv * xv, axis=-1, keepdims=True) + EPS)
        xn = xv * r
        err = xn * g - t_ref[...]
        loss_ref[...] += 0.5 * jnp.sum(jnp.mean(err * err, axis=-1, keepdims=True), axis=0, keepdims=True)
        dy = err * (1.0 / D)
        dg_ref[...] += jnp.sum(dy * xn, axis=0, keepdims=True)
        u = dy * g
        dx_ref[...] = r * (u - xn * jnp.mean(xn * u, axis=-1, keepdims=True))

    tile = pl.BlockSpec((tm, D), lambda i: (i, 0))
    row = pl.BlockSpec((1, D), lambda i: (0, 0))
    out_shape = [jax.ShapeDtypeStruct((S, D), F32), jax.ShapeDtypeStruct((S, D), BF16)]
    in_specs = [pl.BlockSpec((tm, e), lambda i: (i, 0)), pl.BlockSpec((e, D), lambda i: (0, 0)), tile, _mod_spec(layer, 2, 1)]
    out_specs, operands = [tile, tile], [ycat, w_out, x, mod]
    if final is not None:
        out_shape += [jax.ShapeDtypeStruct((1, LANE), F32), jax.ShapeDtypeStruct((1, D), F32)]
        in_specs += [tile, row]
        out_specs += [pl.BlockSpec((1, LANE), lambda i: (0, 0)), row]
        operands += list(final)
    return pl.pallas_call(
        body, name=name, grid=(S // tm,), out_shape=out_shape, in_specs=in_specs, out_specs=out_specs,
        compiler_params=_params(("arbitrary",)),
    )(*operands)


def _out_bwd(dx, out, ycat, w_out, mod, layer, carried, name):
    tm = 512
    nsteps = S // tm
    e = ycat.shape[1]
    rb = e // NDEV
    nc = len(carried)

    def body(dx_ref, o_ref, y_ref, w_ref, gate_ref, *rest):
        dy_ref, gw_ref, dgate_ref = rest[nc:nc + 3]
        acc = rest[-1]
        step = pl.program_id(0)

        @pl.when(step == 0)
        def _():
            dgate_ref[...] = jnp.zeros_like(dgate_ref)
            acc[...] = jnp.zeros_like(acc)

        dxv = dx_ref[...]
        d_out = (gate_ref[...] * dxv).astype(BF16)
        dgate_ref[...] += jnp.sum(dxv * o_ref[...].astype(F32), axis=0, keepdims=True)
        dy_ref[...] = lax.dot_general(d_out, w_ref[...], (((1,), (1,)), ((), ())), preferred_element_type=F32).astype(BF16)
        acc[...] += lax.dot_general(y_ref[...], d_out, (((0,), (0,)), ((), ())), preferred_element_type=F32)

        @pl.when(step == nsteps - 1)
        def _():
            for d in range(NDEV):
                gw_ref[d % 2, d // 2] = acc[d * rb:(d + 1) * rb, :].astype(BF16)

    tile = pl.BlockSpec((tm, D), lambda i: (i, 0))
    wide = pl.BlockSpec((tm, e), lambda i: (i, 0))
    outs = pl.pallas_call(
        body, name=name, grid=(nsteps,),
        out_shape=[jax.ShapeDtypeStruct((S, e), BF16), jax.ShapeDtypeStruct((2, NDEV // 2, rb, D), BF16),
                   jax.ShapeDtypeStruct((1, D), F32)] + [jax.ShapeDtypeStruct(a.shape, a.dtype) for a in carried],
        in_specs=[tile, tile, wide, pl.BlockSpec((e, D), lambda i: (0, 0)), _mod_spec(layer, 2, 1)] + [ANY] * nc,
        out_specs=[wide, pl.BlockSpec((2, NDEV // 2, rb, D), lambda i: (0, 0, 0, 0)), pl.BlockSpec((1, D), lambda i: (0, 0))]
        + [ANY] * nc,
        scratch_shapes=[pltpu.VMEM((e, D), F32)],
        input_output_aliases={5 + k: 3 + k for k in range(nc)},
        compiler_params=_params(("arbitrary",)),
    )(dx, out, ycat, w_out, mod, *carried)
    return outs[0], outs[1], outs[2], list(outs[3:])


def _weight_grad(h_t, d_proj, name):
    nb = d_proj.shape[1] // NDEV

    def body(ht_ref, dp_ref, o_ref):
        o_ref[...] = jnp.dot(ht_ref[...], dp_ref[...], preferred_element_type=F32).astype(BF16)

    return pl.pallas_call(
        body, name=name, grid=(NDEV,), out_shape=jax.ShapeDtypeStruct((2, NDEV // 2, D, nb), BF16),
        in_specs=[pl.BlockSpec((D, S), lambda d: (0, 0)), pl.BlockSpec((S, nb), lambda d: (0, d))],
        out_specs=pl.BlockSpec((None, None, D, nb), lambda d: (d % 2, d // 2, 0, 0)),
        compiler_params=_params(("arbitrary",)),
    )(h_t, d_proj)


def _dh_norm_bwd(d_proj, wg, x, dx, mod, norm_g3, layer, carried, name):
    nb = wg.shape[-1]
    tm = 512
    wb = W_BLOCKS
    rc = 128

    def body(dp_ref, w_ref, x_ref, dx_ref, g_ref, scale_ref, carried_ref,
             dxi_ref, dshift_ref, dscale_ref, dg_ref, carried_out, acc):
        i, d = pl.program_id(0), pl.program_id(1)
        nt = (((1,), (1,)), ((), ()))
        part = lax.dot_general(dp_ref[:, :nb], w_ref[0], nt, preferred_element_type=F32)
        for b in range(1, wb):
            part += lax.dot_general(dp_ref[:, b * nb:(b + 1) * nb], w_ref[b], nt, preferred_element_type=F32)

        @pl.when(d == 0)
        def _():
            acc[...] = part

        @pl.when(d != 0)
        def _():
            acc[...] += part

        @pl.when(jnp.logical_and(i == 0, d == 0))
        def _():
            dshift_ref[...] = jnp.zeros_like(dshift_ref)
            dscale_ref[...] = jnp.zeros_like(dscale_ref)
            dg_ref[...] = jnp.zeros_like(dg_ref)

        @pl.when(d == NDEV // wb - 1)
        def _():
            g = g_ref[...]
            scale1 = 1.0 + scale_ref[...]

            def chunk(k, sums):
                rows = pl.ds(pl.multiple_of(k * rc, rc), rc)
                xv, dhv = x_ref[rows, :], acc[rows, :]
                r = lax.rsqrt(jnp.mean(xv * xv, axis=-1, keepdims=True) + EPS)
                xn = xv * r
                dhn = dhv * scale1
                u = dhn * g
                dxi_ref[rows, :] = dx_ref[rows, :] + r * (u - xn * jnp.mean(xn * u, axis=-1, keepdims=True))
                return (sums[0] + jnp.sum(dhv, axis=0, keepdims=True),
                        sums[1] + jnp.sum(dhv * (xn * g), axis=0, keepdims=True),
                        sums[2] + jnp.sum(dhn * xn, axis=0, keepdims=True))

            zero = jnp.zeros((1, D), F32)
            sums = lax.fori_loop(0, tm // rc, chunk, (zero, zero, zero))
            dshift_ref[...] += sums[0]
            dscale_ref[...] += sums[1]
            dg_ref[...] += sums[2]

    tile = pl.BlockSpec((tm, D), lambda i, d: (i, 0))
    row = pl.BlockSpec((1, D), lambda i, d: (0, 0))
    return pl.pallas_call(
        body, name=name, grid=(S // tm, NDEV // wb),
        out_shape=[jax.ShapeDtypeStruct((S, D), F32)] + [jax.ShapeDtypeStruct((1, D), F32)] * 3
        + [jax.ShapeDtypeStruct(carried.shape, carried.dtype)],
        in_specs=[pl.BlockSpec((tm, wb * nb), lambda i, d: (i, d)), pl.BlockSpec((wb, D, nb), lambda i, d: (d, 0, 0)),
                  tile, tile, pl.BlockSpec((None, 1, D), lambda i, d: (layer, 0, 0)), _mod_spec(layer, 1, 2), ANY],
        out_specs=[tile, row, row, row, ANY], scratch_shapes=[pltpu.VMEM((tm, D), F32)],
        input_output_aliases={6: 4}, compiler_params=_params(("arbitrary", "arbitrary")),
    )(d_proj, wg, x, dx, norm_g3, mod, carried)


TS = 256
NCH = TS // CHUNK
HALO_BLOCKS = TS // HALO


def _halo_before(width, col_block):
    return pl.BlockSpec((HALO, width), lambda i: (jnp.maximum(i * HALO_BLOCKS - 1, 0), col_block))


def _halo_after(width, col_block):
    return pl.BlockSpec((HALO, width), lambda i: (jnp.minimum((i + 1) * HALO_BLOCKS, S // HALO - 1), col_block))


def _shift_down(ext, k):
    return pltpu.roll(ext, k, 0)[HALO:]


def _shift_up(ext, k):
    return pltpu.roll(ext, ext.shape[0] - k, 0)[:ext.shape[0] - HALO]


def _layer_norm_head(v, lg, lb):
    mu = jnp.mean(v, axis=-1, keepdims=True)
    vc = v - mu
    rstd = lax.rsqrt(jnp.mean(vc * vc, axis=-1, keepdims=True) + EPS)
    vhat = vc * rstd
    return vhat, rstd, vhat * lg + lb


def _causal_mask():
    return lax.broadcasted_iota(jnp.int32, (CHUNK, CHUNK), 0) >= lax.broadcasted_iota(jnp.int32, (CHUNK, CHUNK), 1)


def _even_mix_fwd(proj, convw, ln_g3, ln_b3, sgu_w, sgu_bcol, wl, after, name):
    def body(pj_ref, hh_ref, hc_ref, cw_ref, lg_ref, lb_ref, sw_ref, sb_ref, *rest):
        y_ref = rest[-1]
        live = (pl.program_id(0) > 0).astype(F32)
        causal = _causal_mask()
        for j in range(E_A // HEAD):
            cols = slice(j * HEAD, (j + 1) * HEAD)
            w0, w1, w2 = cw_ref[0:1, cols], cw_ref[1:2, cols], cw_ref[2:3, cols]
            lg, lb = lg_ref[:, cols], lb_ref[:, cols]
            wm = jnp.where(causal, sw_ref[j], 0.0).astype(BF16)
            bias = sb_ref[j]

            def split(s, rows, cols=cols):
                return pj_ref[rows, s * E_A + cols.start:s * E_A + cols.stop].astype(F32)

            prev_tail = hc_ref[:, cols].astype(F32) * hh_ref[:, cols].astype(F32) * live
            for n in range(NCH):
                rows = slice(n * CHUNK, (n + 1) * CHUNK)
                p = split(2, rows) * split(0, rows)
                ext = jnp.concatenate([prev_tail, p], axis=0)
                prev_tail = p[CHUNK - HALO:]
                cv = w2 * p + w1 * _shift_down(ext, 1) + w0 * _shift_down(ext, 2)
                y_ref[rows, cols] = (split(1, rows) * cv * _silu(split(3, rows))).astype(BF16)
                _, _, vn = _layer_norm_head(split(5, rows), lg, lb)
                mixed = jnp.dot(wm, vn.astype(BF16), preferred_element_type=F32) + bias
                y_ref[rows, E_A + cols.start:E_A + cols.stop] = (split(4, rows) * mixed * _silu(split(6, rows))).astype(BF16)

    const3 = lambda i: (wl, 0, 0)
    const4 = lambda i: (wl, 0, 0, 0)
    return pl.pallas_call(
        body, name=name, grid=(S // TS,), out_shape=jax.ShapeDtypeStruct((S, 2 * E_A), BF16),
        in_specs=[pl.BlockSpec((TS, 7 * E_A), lambda i: (i, 0)), _halo_before(E_A, 0), _halo_before(E_A, 2),
                  pl.BlockSpec((None, 3, E_A), const3), pl.BlockSpec((None, 1, E_A), const3),
                  pl.BlockSpec((None, 1, E_A), const3), pl.BlockSpec((None, NDEV, CHUNK, CHUNK), const4),
                  pl.BlockSpec((None, NDEV, CHUNK, 1), const4)] + [ANY] * len(after),
        out_specs=pl.BlockSpec((TS, 2 * E_A), lambda i: (i, 0)),
        compiler_params=_params(("arbitrary",)),
    )(proj, proj, proj, convw, ln_g3, ln_b3, sgu_w, sgu_bcol, *after)


def _even_mix_bwd(proj, d_ycat, convw, ln_g3, ln_b3, sgu_w, sgu_bcol, wl, name):
    nsteps = S // TS

    def body(pj_ref, hh_ref, hc_ref, hb_ref, hz_ref, dy_ref, hdy_ref, cw_ref, lg_ref, lb_ref, sw_ref, sb_ref,
             dp_ref, dcw_ref, dlg_ref, dlb_ref, dsw_ref, dsb_ref):
        step = pl.program_id(0)

        @pl.when(step == 0)
        def _():
            for ref in (dcw_ref, dlg_ref, dlb_ref, dsw_ref, dsb_ref):
                ref[...] = jnp.zeros_like(ref)

        live_before = (step > 0).astype(F32)
        live_after = (step < nsteps - 1).astype(F32)
        causal = _causal_mask()
        for j in range(E_A // HEAD):
            cols = slice(j * HEAD, (j + 1) * HEAD)
            w0, w1, w2 = cw_ref[0:1, cols], cw_ref[1:2, cols], cw_ref[2:3, cols]
            lg, lb = lg_ref[:, cols], lb_ref[:, cols]
            wmf = jnp.where(causal, sw_ref[j], 0.0)
            wm, wmt = wmf.astype(BF16), wmf.T.astype(BF16)
            bias = sb_ref[j]

            def split(s, rows, cols=cols):
                return pj_ref[rows, s * E_A + cols.start:s * E_A + cols.stop].astype(F32)

            def put(s, rows, val, cols=cols):
                dp_ref[rows, s * E_A + cols.start:s * E_A + cols.stop] = val.astype(BF16)

            ps = [split(2, slice(n * CHUNK, (n + 1) * CHUNK)) * split(0, slice(n * CHUNK, (n + 1) * CHUNK)) for n in range(NCH)]
            next_head = (hdy_ref[:, cols].astype(F32) * hb_ref[:, cols].astype(F32) * _silu(hz_ref[:, cols].astype(F32))
                         * live_after)
            acc_w = [jnp.zeros((1, HEAD), F32) for _ in range(3)]
            for n in reversed(range(NCH)):
                rows = slice(n * CHUNK, (n + 1) * CHUNK)
                p = ps[n]
                tail = ps[n - 1][CHUNK - HALO:] if n > 0 else hc_ref[:, cols].astype(F32) * hh_ref[:, cols].astype(F32) * live_before
                ext = jnp.concatenate([tail, p], axis=0)
                p1, p2 = _shift_down(ext, 1), _shift_down(ext, 2)
                cv = w2 * p + w1 * p1 + w0 * p2
                a_b, a_z = split(1, rows), split(3, rows)
                sz, dsz = _silu_and_grad(a_z)
                dya = dy_ref[rows, cols].astype(F32)
                put(1, rows, dya * cv * sz)
                put(3, rows, dya * a_b * cv * dsz)
                gcv = dya * a_b * sz
                acc_w[0] += jnp.sum(gcv * p2, axis=0, keepdims=True)
                acc_w[1] += jnp.sum(gcv * p1, axis=0, keepdims=True)
                acc_w[2] += jnp.sum(gcv * p, axis=0, keepdims=True)
                gext = jnp.concatenate([gcv, next_head], axis=0)
                next_head = gcv[:HALO]
                dpv = w2 * gcv + w1 * _shift_up(gext, 1) + w0 * _shift_up(gext, 2)
                put(2, rows, dpv * split(0, rows))
                put(0, rows, dpv * split(2, rows))
            for k in range(3):
                dcw_ref[k:k + 1, cols] += acc_w[k]

            acc_lg, acc_lb = jnp.zeros((1, HEAD), F32), jnp.zeros((1, HEAD), F32)
            acc_sw, acc_sb = jnp.zeros((CHUNK, CHUNK), F32), jnp.zeros((CHUNK, 1), F32)
            for n in range(NCH):
                rows = slice(n * CHUNK, (n + 1) * CHUNK)
                u, z = split(4, rows), split(6, rows)
                vhat, rstd, vn = _layer_norm_head(split(5, rows), lg, lb)
                vn16 = vn.astype(BF16)
                mixed = jnp.dot(wm, vn16, preferred_element_type=F32) + bias
                sz, dsz = _silu_and_grad(z)
                dyb = dy_ref[rows, E_A + cols.start:E_A + cols.stop].astype(F32)
                put(4, rows, dyb * mixed * sz)
                put(6, rows, dyb * u * mixed * dsz)
                dmix = dyb * u * sz
                dmix16 = dmix.astype(BF16)
                acc_sb += jnp.sum(dmix, axis=1, keepdims=True)
                acc_sw += lax.dot_general(dmix16, vn16, (((1,), (1,)), ((), ())), preferred_element_type=F32)
                dvn = jnp.dot(wmt, dmix16, preferred_element_type=F32)
                acc_lg += jnp.sum(dvn * vhat, axis=0, keepdims=True)
                acc_lb += jnp.sum(dvn, axis=0, keepdims=True)
                dvh = dvn * lg
                put(5, rows, rstd * (dvh - jnp.mean(dvh, axis=-1, keepdims=True)
                                     - vhat * jnp.mean(dvh * vhat, axis=-1, keepdims=True)))
            dlg_ref[:, cols] += acc_lg
            dlb_ref[:, cols] += acc_lb
            dsw_ref[j] += jnp.where(causal, acc_sw, 0.0)
            dsb_ref[j] += acc_sb

    const3 = lambda i: (wl, 0, 0)
    const4 = lambda i: (wl, 0, 0, 0)
    fixed2 = lambda i: (0, 0)
    fixed3 = lambda i: (0, 0, 0)
    return pl.pallas_call(
        body, name=name, grid=(nsteps,),
        out_shape=[jax.ShapeDtypeStruct((S, 7 * E_A), BF16), jax.ShapeDtypeStruct((3, E_A), F32),
                   jax.ShapeDtypeStruct((1, E_A), F32), jax.ShapeDtypeStruct((1, E_A), F32),
                   jax.ShapeDtypeStruct((NDEV, CHUNK, CHUNK), F32), jax.ShapeDtypeStruct((NDEV, CHUNK, 1), F32)],
        in_specs=[pl.BlockSpec((TS, 7 * E_A), lambda i: (i, 0)), _halo_before(E_A, 0), _halo_before(E_A, 2),
                  _halo_after(E_A, 1), _halo_after(E_A, 3),
                  pl.BlockSpec((TS, 2 * E_A), lambda i: (i, 0)), _halo_after(E_A, 0),
                  pl.BlockSpec((None, 3, E_A), const3), pl.BlockSpec((None, 1, E_A), const3),
                  pl.BlockSpec((None, 1, E_A), const3), pl.BlockSpec((None, NDEV, CHUNK, CHUNK), const4),
                  pl.BlockSpec((None, NDEV, CHUNK, 1), const4)],
        out_specs=[pl.BlockSpec((TS, 7 * E_A), lambda i: (i, 0)), pl.BlockSpec((3, E_A), fixed2),
                   pl.BlockSpec((1, E_A), fixed2), pl.BlockSpec((1, E_A), fixed2),
                   pl.BlockSpec((NDEV, CHUNK, CHUNK), fixed3), pl.BlockSpec((NDEV, CHUNK, 1), fixed3)],
        compiler_params=_params(("arbitrary",)),
    )(proj, proj, proj, proj, proj, d_ycat, d_ycat, convw, ln_g3, ln_b3, sgu_w, sgu_bcol)


def _window_count(step, n, win, ext_before):
    rows = CHUNK if ext_before else CHUNK + HALO
    t = step * TS + n * CHUNK + lax.broadcasted_iota(jnp.int32, (rows, 1), 0)
    return jnp.minimum(t + 1, win).astype(F32)


def _pool_weight(wp_ref, g):
    return jnp.concatenate([wp_ref[d, g] for d in range(NDEV)], axis=0)


def _pooled_chunk(p, tail, win, count):
    sums = jnp.concatenate([tail, p], axis=0)
    shift = 1
    while shift < win:
        sums = sums + pltpu.roll(sums, shift, 0)
        shift *= 2
    return sums[HALO:] / count - p


def _pool_mix_fwd(proj, wpool, pscale4, wl, after, name):
    e_c = 4 * G_C

    def body(pj_ref, hp_ref, wp_ref, ps_ref, *rest):
        y_ref, pooled_scr, yraw_scr = rest[-3:]
        step = pl.program_id(0)
        live = (step > 0).astype(F32)
        for g, win in enumerate(POOL_WINDOWS):
            for q in range(G_C // LANE):
                cols = slice(g * G_C + q * LANE, g * G_C + (q + 1) * LANE)
                tail = hp_ref[:, cols].astype(F32) * live
                for n in range(NCH):
                    rows = slice(n * CHUNK, (n + 1) * CHUNK)
                    p = pj_ref[rows, cols].astype(F32)
                    pooled_scr[rows, q * LANE:(q + 1) * LANE] = _pooled_chunk(
                        p, tail, win, _window_count(step, n, win, True)).astype(BF16)
                    tail = p[CHUNK - HALO:]
            yraw_scr[...] = jnp.dot(pooled_scr[...], _pool_weight(wp_ref, g), preferred_element_type=F32)
            for q in range(G_C // LANE):
                cols = slice(g * G_C + q * LANE, g * G_C + (q + 1) * LANE)
                for n in range(NCH):
                    rows = slice(n * CHUNK, (n + 1) * CHUNK)
                    z = pj_ref[rows, e_c + cols.start:e_c + cols.stop].astype(F32)
                    y_ref[rows, cols] = (yraw_scr[rows, q * LANE:(q + 1) * LANE] * ps_ref[:, cols] * _silu(z)).astype(BF16)

    return pl.pallas_call(
        body, name=name, grid=(S // TS,), out_shape=jax.ShapeDtypeStruct((S, e_c), BF16),
        in_specs=[pl.BlockSpec((TS, 2 * e_c), lambda i: (i, 0)), _halo_before(e_c, 0),
                  pl.BlockSpec((NDEV, 4, G_C // NDEV, G_C), lambda i: (0, 0, 0, 0)),
                  pl.BlockSpec((None, 1, e_c), lambda i: (wl, 0, 0))] + [ANY] * len(after),
        out_specs=pl.BlockSpec((TS, e_c), lambda i: (i, 0)),
        scratch_shapes=[pltpu.VMEM((TS, G_C), BF16), pltpu.VMEM((TS, G_C), F32)],
        compiler_params=_params(("arbitrary",)),
    )(proj, proj, wpool, pscale4, *after)


def _pool_mix_bwd(proj, d_ycat, wpool, pscale4, wl, name):
    e_c = 4 * G_C
    nsteps = S // TS
    rb = G_C // NDEV

    def body(pj_ref, hp_ref, hz_ref, dy_ref, hdy_ref, wp_ref, ps_ref,
             dp_ref, dps_ref, dwp_ref, pooled_scr, yraw_scr, dyraw_scr, dpool_scr, acc_w):
        step = pl.program_id(0)

        @pl.when(step == 0)
        def _():
            dps_ref[...] = jnp.zeros_like(dps_ref)
            acc_w[...] = jnp.zeros_like(acc_w)

        live_before = (step > 0).astype(F32)
        live_after = (step < nsteps - 1).astype(F32)
        for g, win in enumerate(POOL_WINDOWS):
            weight = _pool_weight(wp_ref, g)
            for q in range(G_C // LANE):
                cols = slice(g * G_C + q * LANE, g * G_C + (q + 1) * LANE)
                tail = hp_ref[:, cols].astype(F32) * live_before
                for n in range(NCH):
                    rows = slice(n * CHUNK, (n + 1) * CHUNK)
                    p = pj_ref[rows, cols].astype(F32)
                    pooled_scr[rows, q * LANE:(q + 1) * LANE] = _pooled_chunk(
                        p, tail, win, _window_count(step, n, win, True)).astype(BF16)
                    tail = p[CHUNK - HALO:]
            yraw_scr[...] = jnp.dot(pooled_scr[...], weight, preferred_element_type=F32)
            for q in range(G_C // LANE):
                cols = slice(g * G_C + q * LANE, g * G_C + (q + 1) * LANE)
                local = slice(q * LANE, (q + 1) * LANE)
                scale = ps_ref[:, cols]
                acc_ps = jnp.zeros((1, LANE), F32)
                for n in range(NCH):
                    rows = slice(n * CHUNK, (n + 1) * CHUNK)
                    sz, dsz = _silu_and_grad(pj_ref[rows, e_c + cols.start:e_c + cols.stop].astype(F32))
                    dyv = dy_ref[rows, cols].astype(F32)
                    yraw = yraw_scr[rows, local]
                    dyraw_scr[rows, local] = (dyv * scale * sz).astype(BF16)
                    acc_ps += jnp.sum(dyv * yraw * sz, axis=0, keepdims=True)
                    dp_ref[rows, e_c + cols.start:e_c + cols.stop] = (dyv * yraw * scale * dsz).astype(BF16)
                dps_ref[:, cols] += acc_ps
                dyraw_scr[TS:, local] = (hdy_ref[:, cols].astype(F32) * scale * _silu(hz_ref[:, cols].astype(F32))
                                         * live_after).astype(BF16)
            dpool_scr[...] = lax.dot_general(dyraw_scr[...], weight, (((1,), (1,)), ((), ())), preferred_element_type=F32)
            acc_w[g] += lax.dot_general(pooled_scr[...], dyraw_scr[:TS, :], (((0,), (0,)), ((), ())),
                                        preferred_element_type=F32)
            for q in range(G_C // LANE):
                cols = slice(g * G_C + q * LANE, g * G_C + (q + 1) * LANE)
                local = slice(q * LANE, (q + 1) * LANE)
                for n in range(NCH):
                    rows = slice(n * CHUNK, (n + 1) * CHUNK)
                    ext = dpool_scr[n * CHUNK:(n + 1) * CHUNK + HALO, local]
                    sums = ext / _window_count(step, n, win, False)
                    shift = 1
                    while shift < win:
                        sums = sums + pltpu.roll(sums, CHUNK + HALO - shift, 0)
                        shift *= 2
                    dp_ref[rows, cols] = (sums[:CHUNK] - ext[:CHUNK]).astype(BF16)

        @pl.when(step == nsteps - 1)
        def _():
            for g in range(4):
                for d in range(NDEV):
                    dwp_ref[d % 2, d // 2, g] = acc_w[g, d * rb:(d + 1) * rb, :].astype(BF16)

    in_specs = [pl.BlockSpec((TS, 2 * e_c), lambda i: (i, 0)), _halo_before(e_c, 0), _halo_after(e_c, 1),
                pl.BlockSpec((TS, e_c), lambda i: (i, 0)), _halo_after(e_c, 0),
                pl.BlockSpec((NDEV, 4, rb, G_C), lambda i: (0, 0, 0, 0)),
                pl.BlockSpec((None, 1, e_c), lambda i: (wl, 0, 0))]
    args = [proj, proj, proj, d_ycat, d_ycat, wpool, pscale4]
    return pl.pallas_call(
        body, name=name, grid=(nsteps,),
        out_shape=[jax.ShapeDtypeStruct((S, 2 * e_c), BF16), jax.ShapeDtypeStruct((1, e_c), F32),
                   jax.ShapeDtypeStruct((2, NDEV // 2) + wpool.shape[1:], BF16)],
        in_specs=in_specs,
        out_specs=[pl.BlockSpec((TS, 2 * e_c), lambda i: (i, 0)), pl.BlockSpec((1, e_c), lambda i: (0, 0)),
                   pl.BlockSpec((2, NDEV // 2, 4, rb, G_C), lambda i: (0, 0, 0, 0, 0))],
        scratch_shapes=[pltpu.VMEM((TS, G_C), BF16), pltpu.VMEM((TS, G_C), F32), pltpu.VMEM((TS + HALO, G_C), BF16),
                        pltpu.VMEM((TS + HALO, G_C), F32), pltpu.VMEM((4, G_C, G_C), F32)],
        compiler_params=_params(("arbitrary",)),
    )(*args)


def _adamw(w, g, m, v):
    m = ADAM_B1 * m + (1.0 - ADAM_B1) * g
    v = ADAM_B2 * v + (1.0 - ADAM_B2) * jnp.square(g)
    m_hat = m / (1.0 - ADAM_B1 ** ADAM_STEP)
    v_hat = v / (1.0 - ADAM_B2 ** ADAM_STEP)
    delta = -ADAM_LR * (m_hat / (jnp.sqrt(v_hat) + ADAM_EPS) + ADAM_WD * w)
    return delta, m, v


def _adam_sharded(w, m, v, chip_parts, landed, my_chip, carried, name, first=0, into=()):
    _, nr, ncol = w.shape
    nl = len(chip_parts)
    steps = 2
    tr = nr // steps
    nc, ni = len(carried), len(into)

    def body(chip_ref, w_ref, m_ref, v_ref, *rest):
        parts, zones = rest[:nl], rest[nl:2 * nl]
        g_ref, d_ref, nm_ref, nv_ref = rest[2 * nl + nc + ni:2 * nl + nc + ni + 4]
        layer = pl.program_id(0)
        g = jnp.zeros((tr, ncol), F32)
        for l in range(nl):
            gl = parts[l][...].astype(F32)
            for q in range(3):
                gl = gl + zones[l][q].astype(F32)
            g = jnp.where(layer == l, gl, g)
        g_ref[...] = g
        d_ref[...], nm_ref[...], nv_ref[...] = _adamw(w_ref[...], g, m_ref[...], v_ref[...])

    def rows_of(l):
        return lambda layer, i, chip_ref: jnp.where(layer == l, i, jnp.where(layer < l, 0, steps - 1))

    spec = pl.BlockSpec((None, tr, ncol), lambda layer, i, chip_ref: (first + layer, i, 0))
    part_specs = [pl.BlockSpec((None, tr, ncol), lambda layer, i, chip_ref, l=l: (chip_ref[0], rows_of(l)(layer, i, chip_ref), 0))
                  for l in range(nl)]
    zone_specs = [pl.BlockSpec((3, tr, ncol), lambda layer, i, chip_ref, l=l: (0, rows_of(l)(layer, i, chip_ref), 0))
                  for l in range(nl)]
    grid_spec = pltpu.PrefetchScalarGridSpec(
        num_scalar_prefetch=1, grid=(nl, steps),
        in_specs=[spec, spec, spec] + part_specs + zone_specs + [ANY] * (nc + ni), out_specs=[spec] * 4 + [ANY] * nc)
    aliases = {4 + 2 * nl + k: 4 + k for k in range(nc)}
    aliases.update({4 + 2 * nl + nc + k: k for k in range(ni)})
    return pl.pallas_call(
        body, name=name, grid_spec=grid_spec,
        out_shape=[jax.ShapeDtypeStruct(w.shape, F32)] * 4 + [jax.ShapeDtypeStruct(a.shape, a.dtype) for a in carried],
        input_output_aliases=aliases, compiler_params=_params(("arbitrary", "arbitrary")),
    )(my_chip, w, m, v, *chip_parts, *landed, *carried, *into)


def _adam_small(params, name):
    n = len(params)

    def body(*refs):
        ins, outs = refs[:4 * n], refs[4 * n:]
        for k in range(n):
            w_ref, g_ref, m_ref, v_ref = ins[4 * k:4 * k + 4]
            outs[3 * k][...], outs[3 * k + 1][...], outs[3 * k + 2][...] = _adamw(w_ref[...], g_ref[...], m_ref[...], v_ref[...])

    outs = pl.pallas_call(body, name=name, out_shape=[jax.ShapeDtypeStruct(p[0].shape, F32) for p in params for _ in range(3)],
                          in_specs=[VMEM_FULL] * (4 * n), out_specs=[VMEM_FULL] * (3 * n),
                          compiler_params=_params())(*[a for p in params for a in p])
    return [list(outs[3 * k:3 * k + 3]) for k in range(n)]


def _sum_devices(gathered, name):
    _, nr, ncol = gathered.shape

    def body(g_ref, o_ref):
        acc = g_ref[0].astype(F32)
        for s in range(1, NDEV):
            acc = acc + g_ref[s].astype(F32)
        o_ref[...] = acc

    return pl.pallas_call(body, name=name, grid=(1,), out_shape=jax.ShapeDtypeStruct((nr, ncol), F32),
                          in_specs=[pl.BlockSpec((NDEV, nr, ncol), lambda i: (0, 0, 0))],
                          out_specs=pl.BlockSpec((nr, ncol), lambda i: (0, 0)),
                          compiler_params=_params(("arbitrary",)))(gathered)


def _ada_weight_adam(cact_t, dmod_mine, w, m, v):
    def body(ct_ref, dm_ref, w_ref, m_ref, v_ref, g_ref, d_ref, nm_ref, nv_ref):
        ct, dm = ct_ref[...], dm_ref[...]
        g = ct[:, 0:1] * dm[0:1, :]
        for e in range(1, NDEV):
            g = g + ct[:, e:e + 1] * dm[e:e + 1, :]
        g_ref[...] = g
        d_ref[...], nm_ref[...], nv_ref[...] = _adamw(w_ref[...], g, m_ref[...], v_ref[...])

    spec = pl.BlockSpec((None, D, ADA_NC), lambda l: (l, 0, 0))
    return pl.pallas_call(
        body, name="ada_weight_adam", grid=(DEPTH,), out_shape=[jax.ShapeDtypeStruct(w.shape, F32)] * 4,
        in_specs=[pl.BlockSpec((D, NDEV), lambda l: (0, 0)), pl.BlockSpec((None, NDEV, ADA_NC), lambda l: (l, 0, 0)),
                  spec, spec, spec],
        out_specs=[spec] * 4, compiler_params=_params(("arbitrary",)),
    )(cact_t, dmod_mine, w, m, v)


def _pad_rows(a, rows):
    a = a.reshape(-1, D)
    return jnp.pad(a, ((0, rows - a.shape[0]), (0, 0)))


def kernel(x, c, norm_g, ada_w, ada_b, ab_w_in, ab_conv_w, ab_ln_g, ab_ln_b, ab_sgu_w, ab_sgu_b, ab_w_out, c_w_in, c_pool_w, c_pool_scale, c_w_out, final_g, loss_target, m_norm_g, m_ada_w, m_ada_b, m_ab_w_in, m_ab_conv_w, m_ab_ln_g, m_ab_ln_b, m_ab_sgu_w, m_ab_sgu_b, m_ab_w_out, m_c_w_in, m_c_pool_w, m_c_pool_scale, m_c_w_out, m_final_g, v_norm_g, v_ada_w, v_ada_b, v_ab_w_in, v_ab_conv_w, v_ab_ln_g, v_ab_ln_b, v_ab_sgu_w, v_ab_sgu_b, v_ab_w_out, v_c_w_in, v_c_pool_w, v_c_pool_scale, v_c_w_out, v_final_g):
    x_pos, y_pos, c_pos = _position()
    me = _index((x_pos, y_pos, c_pos))
    core = c_pos.astype(jnp.int32).reshape(1)
    my_chip = (2 * x_pos + y_pos).astype(jnp.int32).reshape(1)
    me1 = me.astype(jnp.int32).reshape(1)
    x0 = x.reshape(S, D)
    target = loss_target.reshape(S, D)
    norm_g3 = norm_g.reshape(DEPTH, 1, D)
    ln_g3, ln_b3 = ab_ln_g.reshape(2, 1, E_A), ab_ln_b.reshape(2, 1, E_A)
    sgu_bcol = ab_sgu_b.reshape(2, NDEV, CHUNK, 1)
    rb = G_C // NDEV
    pool_w3, m_pool_w3, v_pool_w3 = (a.reshape(2, 4 * rb, G_C) for a in (c_pool_w, m_c_pool_w, v_c_pool_w))

    cact_all, mod, (convw_all, pscale_all) = _ada_forward(c, ada_w, ada_b, [ab_conv_w, c_pool_scale])
    convw = jnp.transpose(convw_all, (1, 2, 0, 3)).reshape(2, 3, E_A)
    pscale4 = jnp.transpose(pscale_all, (1, 0, 2)).reshape(2, 1, 4 * G_C)
    zones = []
    for layer in range(DEPTH):
        wl = layer // 2
        if layer % 2 == 0:
            zones.append([_to_zone(ab_w_in, wl, me1, BF16, f"cast_w_in_{layer}"), _to_zone(ab_w_out, wl, me1, BF16, f"cast_w_out_{layer}")])
        else:
            zones.append([_to_zone(c_w_in, wl, me1, BF16, f"cast_w_in_{layer}"), _to_zone(c_w_out, wl, me1, BF16, f"cast_w_out_{layer}"),
                          _to_zone(pool_w3, wl, me1, BF16, f"cast_pool_w_{layer}")])

    def gathered(flight, after, layer):
        wg = _gather_end(flight, [after], f"gather_end_{layer}")
        return [wg[0], wg[1].reshape(-1, D)] + [w.reshape(NDEV, 4, rb, G_C) for w in wg[2:]]

    flight, (mod,) = _gather_start(zones[0], [convw_all], [mod], "gather_start_0")
    flight, (mod,) = _gather_mid(flight, [z for zs in zones[1:] for z in zs], [mod], "gather_mid_0")
    next_flight, (mod,) = _gather_start(zones[1], [], [mod], "gather_start_1")
    flight = _gather_late(flight, [mod], "gather_late_0")
    xs, hts, projs, ycats, outs, gathered_w = [x0], [], [], [], [], [gathered(flight, mod, 0)]
    for layer in range(DEPTH):
        wl = layer // 2
        even = layer % 2 == 0
        wg = gathered_w[layer]
        h_t, proj = _norm_proj(xs[-1], mod, norm_g3, wg[0], layer, f"norm_proj_{layer}")
        if layer + 1 < DEPTH:
            flight, (h_t,) = _gather_mid(next_flight, [], [h_t], f"gather_mid_{layer + 1}")
            if layer + 2 < DEPTH:
                next_flight, (h_t,) = _gather_start(zones[layer + 2], [], [h_t], f"gather_start_{layer + 2}")
        if even:
            ycat = _even_mix_fwd(proj, convw, ln_g3, ln_b3, ab_sgu_w, sgu_bcol, wl, [h_t], f"even_mix_fwd_{layer}")
        else:
            ycat = _pool_mix_fwd(proj, wg[2], pscale4, wl, [h_t], f"pool_mix_fwd_{layer}")
        if layer + 1 < DEPTH:
            flight = _gather_late(flight, [ycat], f"gather_late_{layer + 1}")
        if layer + 1 < DEPTH:
            x_new, out = _out_proj(ycat, wg[1], xs[-1], mod, layer, f"out_proj_{layer}")
            gathered_w.append(gathered(flight, x_new, layer + 1))
            xs.append(x_new)
        else:
            dx, out, loss_part, d_final_g = _out_proj(ycat, wg[1], xs[-1], mod, layer, f"out_proj_{layer}",
                                                      final=(target, final_g.reshape(1, D)))
        hts.append(h_t)
        projs.append(proj)
        ycats.append(ycat)
        outs.append(out)

    d_mod, d_norm_g = [None] * DEPTH, [None] * DEPTH
    small, scatters, landed, res = {}, {}, {}, {}

    def finish_scatter(layer, after):
        send_sems, recv_sems, chip_parts, zones = scatters[layer]
        landed[layer] = _scatter_end(chip_parts, zones, send_sems, recv_sems, after, f"scatter_end_{layer}")

    def flat(a):
        return a.reshape(a.shape[0], -1, a.shape[-1])

    def sharded_adam(k, j, layers, w, m, v, carried, first=0, into=()):
        outs4 = _adam_sharded(w, m, v, [flat(landed[l][0][j]) for l in layers], [flat(landed[l][1][j]) for l in layers],
                              my_chip, carried, f"adam_{k}_{first}" if len(layers) < w.shape[0] else "adam_" + k, first, into)
        res[k] = [o.reshape(c_pool_w.shape) if k == "c_pool_w" else o for o in outs4[:4]]
        return list(outs4[4:])

    previous = None
    for layer in reversed(range(DEPTH)):
        wl = layer // 2
        even = layer % 2 == 0
        wg = gathered_w[layer]
        carried = [] if previous is None else [scatters[previous][2][0]]
        d_ycat, grad_out, d_gate, carried = _out_bwd(dx, outs[layer], ycats[layer], wg[1], mod, layer, carried, f"out_bwd_{layer}")
        if previous is not None:
            scatters[previous][2][0] = carried[0]
        parts = [None, grad_out]
        if even:
            d_proj, d_cw, d_lg, d_lb, d_sw, d_sb = _even_mix_bwd(
                projs[layer], d_ycat, convw, ln_g3, ln_b3, ab_sgu_w, sgu_bcol, wl, f"even_mix_bwd_{layer}")
            small[layer] = (d_cw, d_lg, d_lb, d_sw, d_sb)
        else:
            d_proj, d_ps, d_pool = _pool_mix_bwd(projs[layer], d_ycat, wg[2], pscale4, wl, f"pool_mix_bwd_{layer}")
            small[layer] = (d_ps,)
            parts.append(d_pool)
        parts[0] = _weight_grad(hts[layer], d_proj, f"grad_w_in_{layer}")
        pair_send, pair_recv, parts, from_sibling = _pair_start(parts, f"pair_start_{layer}")
        if layer > 0:
            dx, d_shift, d_scale, d_norm_g[layer], parts[0] = _dh_norm_bwd(
                d_proj, wg[0], xs[layer], dx, mod, norm_g3, layer, parts[0], f"dh_norm_bwd_{layer}")
            pair_after = dx
        else:
            finish_scatter(1, d_proj)
            finish_scatter(3, d_proj)
            parts[0], = sharded_adam("c_w_out", 1, (1, 3), c_w_out, m_c_w_out, v_c_w_out, [parts[0]])
            parts[0], = sharded_adam("c_pool_w", 2, (1, 3), pool_w3, m_pool_w3, v_pool_w3, [parts[0]])
            pair_after = res["c_pool_w"][0]
        parts, from_sibling = _pair_end(parts, from_sibling, pair_send, pair_recv, pair_after, f"pair_end_{layer}")
        chip_parts = _pair_sum(parts, from_sibling, core, f"pair_sum_{layer}")
        send_sems, recv_sems, chip_parts, zones = _scatter_start(chip_parts, f"scatter_start_{layer}")
        if layer == 0:
            chip_parts[0], = sharded_adam("c_w_in", 0, (1, 3), c_w_in, m_c_w_in, v_c_w_in, [chip_parts[0]])
            dx, d_shift, d_scale, d_norm_g[layer], chip_parts[0] = _dh_norm_bwd(
                d_proj, wg[0], xs[layer], dx, mod, norm_g3, layer, chip_parts[0], f"dh_norm_bwd_{layer}")
        scatters[layer] = [send_sems, recv_sems, chip_parts, zones]
        previous = layer
        d_mod[layer] = jnp.concatenate([d_shift, d_scale, d_gate], axis=0)
    grad_x = dx.reshape(x.shape)

    sections = [("norm_g", jnp.concatenate(d_norm_g, axis=0), 8),
                ("d_mod", jnp.concatenate(d_mod, axis=0), 16),
                ("ab_ln_g", jnp.concatenate([small[0][1], small[2][1]], axis=0), 8),
                ("ab_ln_b", jnp.concatenate([small[0][2], small[2][2]], axis=0), 8),
                ("ab_sgu_b", jnp.stack([small[0][4], small[2][4]]), 8),
                ("final_g", d_final_g, 8),
                ("ab_conv_w", jnp.stack([small[0][0], small[2][0]]), 8),
                ("c_pool_scale", jnp.concatenate([small[1][0], small[3][0]], axis=0), 8),
                ("ab_sgu_w", jnp.stack([small[0][3], small[2][3]]), 256)]
    offsets, at = {}, 0
    for name, _, rows in sections:
        offsets[name] = (at, rows)
        at += rows
    packed = jnp.concatenate([_pad_rows(a, rows) for _, a, rows in sections] + [jnp.zeros((-at % 32, D), F32)], axis=0)
    loss_rows = jnp.pad(loss_part, ((0, 15), (0, D - LANE)))
    small_zones = [_to_zone(packed[None], 0, me1, BF16, "place_small_grads"), _to_zone(loss_rows[None], 0, me1, F32, "place_loss")]
    small_flight, (mod,) = _gather_start(small_zones, [], [mod], "gather_small_start")

    finish_scatter(2, mod)
    sharded_adam("ab_w_out", 1, (2,), ab_w_out, m_ab_w_out, v_ab_w_out, [], first=1)
    sharded_adam("ab_w_in", 0, (2,), ab_w_in, m_ab_w_in, v_ab_w_in, [], first=1)
    finish_scatter(0, res["ab_w_in"][0])
    sharded_adam("ab_w_out", 1, (0,), ab_w_out, m_ab_w_out, v_ab_w_out, [], into=res["ab_w_out"])
    small_flight, (mod,) = _gather_mid(small_flight, [res["ab_w_out"][0]], [mod], "gather_small_mid")
    sharded_adam("ab_w_in", 0, (0,), ab_w_in, m_ab_w_in, v_ab_w_in, [mod], into=res["ab_w_in"])

    last = res["ab_w_in"][0]
    small_flight = _gather_late(small_flight, [last], "gather_small_late")
    small_grads, losses = _gather_end(small_flight, [last], "gather_small_end")
    summed = _sum_devices(small_grads, "sum_small_grads")
    loss = _sum_devices(losses, "sum_loss")[0, 0]

    def section(name, nrows, src=summed):
        start = offsets[name][0]
        return src[..., start:start + nrows, :]

    grads = {
        "norm_g": section("norm_g", DEPTH),
        "ada_b": section("d_mod", 3 * DEPTH).reshape(DEPTH, 3 * D),
        "ab_ln_g": section("ab_ln_g", 2), "ab_ln_b": section("ab_ln_b", 2),
        "ab_sgu_b": section("ab_sgu_b", 2).reshape(ab_sgu_b.shape),
        "final_g": section("final_g", 1),
        "ab_sgu_w": section("ab_sgu_w", 256).reshape(ab_sgu_w.shape),
        "ab_conv_w": lax.dynamic_slice_in_dim(section("ab_conv_w", 6).reshape(2, 3, E_A), me * HEAD, HEAD, axis=2),
        "c_pool_scale": lax.dynamic_slice_in_dim(section("c_pool_scale", 4).reshape(2, 4 * G_C), me * 256, 256, axis=1),
    }
    small_w = {"norm_g": (norm_g, m_norm_g, v_norm_g), "ada_b": (ada_b, m_ada_b, v_ada_b),
               "ab_ln_g": (ab_ln_g, m_ab_ln_g, v_ab_ln_g), "ab_ln_b": (ab_ln_b, m_ab_ln_b, v_ab_ln_b),
               "ab_sgu_b": (ab_sgu_b, m_ab_sgu_b, v_ab_sgu_b),
               "final_g": (final_g.reshape(1, D), m_final_g.reshape(1, D), v_final_g.reshape(1, D)),
               "ab_sgu_w": (ab_sgu_w, m_ab_sgu_w, v_ab_sgu_w), "ab_conv_w": (ab_conv_w, m_ab_conv_w, v_ab_conv_w),
               "c_pool_scale": (c_pool_scale, m_c_pool_scale, v_c_pool_scale)}
    updates = _adam_small([(w, grads[k], m, v) for k, (w, m, v) in small_w.items()], "adam_small")
    for k, update in zip(small_w, updates):
        res[k] = [grads[k]] + update
    res["final_g"] = [a.reshape(D) for a in res["final_g"]]

    dmod_all = section("d_mod", 3 * DEPTH, small_grads).reshape(NDEV, DEPTH, 3 * D)
    dmod_mine = jnp.transpose(lax.dynamic_slice_in_dim(dmod_all, me * ADA_NC, ADA_NC, axis=2), (1, 0, 2)).astype(F32)
    res["ada_w"] = _ada_weight_adam(jnp.transpose(cact_all.reshape(NDEV, D)), dmod_mine, ada_w, m_ada_w, v_ada_w)

    order = ["norm_g", "ada_w", "ada_b", "ab_w_in", "ab_conv_w", "ab_ln_g", "ab_ln_b", "ab_sgu_w", "ab_sgu_b",
             "ab_w_out", "c_w_in", "c_pool_w", "c_pool_scale", "c_w_out", "final_g"]
    return (loss, grad_x, *[res[k][0] for k in order], *[res[k][1] for k in order],
            *[res[k][2] for k in order], *[res[k][3] for k in order])
```

```python
import functools

import jax
import jax.numpy as jnp
from jax import lax
from jax.experimental import pallas as pl
from jax.experimental.pallas import tpu as pltpu

F32, BF16 = jnp.float32, jnp.bfloat16
S, D = 2048, 1024
NDEV = 8
DEPTH = 4
EPS = 1e-6
E_A = 1024
HEAD = 128
CHUNK = 128
POOL_WINDOWS = (2, 4, 8, 16)
G_C = 512
HALO = 16
ADA_NC = 384
MIB = 1024 * 1024
LANE = 128

ADAM_LR, ADAM_B1, ADAM_B2, ADAM_EPS, ADAM_WD, ADAM_STEP = 0.001, 0.9, 0.999, 1e-08, 0.01, 10

ANY = pl.BlockSpec(memory_space=pl.ANY)
VMEM_FULL = pl.BlockSpec(memory_space=pltpu.VMEM)
IN_HBM = pl.BlockSpec(memory_space=pltpu.HBM)
SEMAPHORES = pl.BlockSpec(memory_space=pltpu.SEMAPHORE)
IN_FLIGHT = pltpu.SideEffectType.DATAFLOW_SIDE_EFFECTING


V7X_VMEM_MIB = 64
VMEM_LIMIT_MIB = V7X_VMEM_MIB - 4


def _params(semantics=None):
    return pltpu.CompilerParams(dimension_semantics=semantics, vmem_limit_bytes=VMEM_LIMIT_MIB * MIB)


def _silu(z):
    return z * jax.nn.sigmoid(z)


def _silu_and_grad(z):
    sig = jax.nn.sigmoid(z)
    return z * sig, sig * (1.0 + z * (1.0 - sig))


def _position():
    return lax.axis_index("x"), lax.axis_index("y"), lax.axis_index("c")


def _index(pos):
    return 4 * pos[0] + 2 * pos[1] + pos[2]


def _peer(pos, k):
    flipped = tuple(1 - p if (k >> (2 - b)) & 1 else p for b, p in enumerate(pos))
    return flipped, _index(flipped)


def _remote(src, dst, send_sem, recv_sem, device):
    return pltpu.make_async_remote_copy(src_ref=src, dst_ref=dst, send_sem=send_sem, recv_sem=recv_sem,
                                        device_id=device, device_id_type=pl.DeviceIdType.MESH)


def _pair_start(parts, name):
    n = len(parts)
    lands = [_in_hbm(lax.empty(p.shape[1:], p.dtype)) for p in parts]

    def body(*refs):
        ins, zones = refs[:n], refs[n:2 * n]
        send_sems, recv_sems = refs[2 * n:2 * n + 2]
        x, y, c = _position()
        for j in range(n):
            _remote(ins[j].at[1 - c], zones[j], send_sems.at[j], recv_sems.at[j], (x, y, 1 - c)).start()

    outs = pl.pallas_call(
        body, name=name,
        out_shape=(pltpu.SemaphoreType.DMA((n,)), pltpu.SemaphoreType.DMA((n,)),
                   *[pltpu.HBM(p.shape, p.dtype) for p in parts], *[pltpu.HBM(z.shape, z.dtype) for z in lands]),
        in_specs=[IN_HBM] * (2 * n), out_specs=(SEMAPHORES, SEMAPHORES, *[IN_HBM] * (2 * n)),
        input_output_aliases={j: 2 + j for j in range(2 * n)},
        compiler_params=pltpu.CompilerParams(has_side_effects=IN_FLIGHT),
    )(*[_in_hbm(p) for p in parts], *lands)
    return outs[0], outs[1], list(outs[2:2 + n]), list(outs[2 + n:])


def _pair_end(parts, zones, send_sems, recv_sems, after, name):
    n = len(parts)

    def body(*refs):
        ins, zs = refs[:n], refs[n:2 * n]
        s, r = refs[2 * n:2 * n + 2]
        me = _position()
        for j in range(n):
            copy = _remote(ins[j].at[0], zs[j], s.at[j], r.at[j], me)
            copy.wait_send()
            copy.wait_recv()

    outs = pl.pallas_call(
        body, name=name,
        out_shape=(*[pltpu.HBM(p.shape, p.dtype) for p in parts], *[pltpu.HBM(z.shape, z.dtype) for z in zones]),
        in_specs=[IN_HBM] * (2 * n) + [SEMAPHORES, SEMAPHORES, ANY], out_specs=tuple([IN_HBM] * (2 * n)),
        input_output_aliases={j: j for j in range(2 * n)},
        compiler_params=pltpu.CompilerParams(has_side_effects=IN_FLIGHT),
    )(*parts, *zones, send_sems, recv_sems, after)
    return list(outs[:n]), list(outs[n:])


def _pair_sum(parts, from_sibling, core, name):
    n = len(parts)
    steps = 8
    p3 = [p.reshape(2, -1, p.shape[-1]) for p in parts]
    q2 = [q.reshape(-1, q.shape[-1]) for q in from_sibling]

    def body(core_ref, *refs):
        for p_ref, q_ref, o_ref in zip(refs[:n], refs[n:2 * n], refs[2 * n:]):
            o_ref[...] = (p_ref[...].astype(F32) + q_ref[...].astype(F32)).astype(BF16)

    tiles = [pl.BlockSpec((q.shape[0] // steps, q.shape[1]), lambda i, core_ref: (i, 0)) for q in q2]
    grid_spec = pltpu.PrefetchScalarGridSpec(
        num_scalar_prefetch=1, grid=(steps,),
        in_specs=[pl.BlockSpec((None, q.shape[0] // steps, q.shape[1]), lambda i, core_ref: (core_ref[0], i, 0)) for q in q2]
        + tiles, out_specs=tiles)
    outs = pl.pallas_call(body, name=name, grid_spec=grid_spec, out_shape=[jax.ShapeDtypeStruct(q.shape, BF16) for q in q2],
                          compiler_params=_params(("arbitrary",)))(core, *p3, *q2)
    return [o.reshape(q.shape) for o, q in zip(outs, from_sibling)]


def _in_hbm(a):
    return pltpu.with_memory_space_constraint(a, pltpu.HBM)


def _chips(x, y):
    return [(1 - x, y), (x, 1 - y), (1 - x, 1 - y)]


def _to_zone(a, wl, me, dtype, name):
    _, rows, cols = a.shape
    tr = 256 if rows % 256 == 0 else rows

    def body(me_ref, a_ref, o_ref):
        o_ref[...] = a_ref[...].astype(dtype)

    grid_spec = pltpu.PrefetchScalarGridSpec(
        num_scalar_prefetch=1, grid=(rows // tr,),
        in_specs=[pl.BlockSpec((None, tr, cols), lambda i, me_ref: (wl, i, 0))],
        out_specs=pl.BlockSpec((None, tr, cols), lambda i, me_ref: (me_ref[0], i, 0)))
    return pl.pallas_call(body, name=name, grid_spec=grid_spec, out_shape=jax.ShapeDtypeStruct((NDEV, rows, cols), dtype),
                          compiler_params=_params(("arbitrary",)))(me, a)


def _halves(block):
    rows = block.shape[0] // 2
    return block.at[pl.ds(0, rows)], block.at[pl.ds(rows, rows)]


def _around(x, y, c):
    return (x, y, 1 - c), (1 - x, y, c), (x, 1 - y, c), (1 - x, 1 - y, c)


def _gather_step1(zs, send, recv, pos):
    sibling, xn, yn, _ = _around(*pos)
    for j, z in enumerate(zs):
        mine = z.at[_index(pos)]
        for k, peer in enumerate((sibling, xn, yn)):
            _remote(mine, mine, send.at[3 * j + k], recv.at[3 * j + k], peer).start()


def _gather_step2(zs, recv1, send, recv, pos):
    sibling, xn, yn, _ = _around(*pos)
    for j, z in enumerate(zs):
        xb, yb = z.at[_index(xn)], z.at[_index(yn)]
        _remote(xb, xb, send.at[4 * j], recv1.at[3 * j + 1], pos).wait_recv()
        _remote(yb, yb, send.at[4 * j], recv1.at[3 * j + 2], pos).wait_recv()
        _remote(xb, xb, send.at[4 * j], recv.at[4 * j], sibling).start()
        _remote(yb, yb, send.at[4 * j + 1], recv.at[4 * j + 1], sibling).start()
        first, second = _halves(xb)[0], _halves(yb)[1]
        _remote(first, first, send.at[4 * j + 2], recv.at[4 * j + 2], yn).start()
        _remote(second, second, send.at[4 * j + 3], recv.at[4 * j + 3], xn).start()


def _gather_step3(zs, recv2, send, recv, pos):
    sibling, _, _, diagonal = _around(*pos)
    for j, z in enumerate(zs):
        db = z.at[_index(diagonal)]
        first, second = _halves(db)
        _remote(first, first, send.at[j], recv2.at[4 * j + 2], pos).wait_recv()
        _remote(second, second, send.at[j], recv2.at[4 * j + 3], pos).wait_recv()
        _remote(db, db, send.at[j], recv.at[j], sibling).start()


def _gather_step4(zs, send1, recv1, send2, recv2, send3, recv3, pos):
    x, y, c = pos
    sibling = (x, y, 1 - c)
    _, sx, sy, sd = _around(*sibling)
    for j, z in enumerate(zs):
        for owner, send, recv, k in ((sibling, send1, recv1, 3 * j), (sx, send2, recv2, 4 * j), (sy, send2, recv2, 4 * j + 1),
                                     (sd, send3, recv3, j)):
            block = z.at[_index(owner)]
            _remote(block, block, send.at[k], recv.at[k], pos).wait_recv()
    for j, z in enumerate(zs):
        block = z.at[0]
        half = _halves(block)[0]
        for ref, send, recv, k in ([(block, send1, recv1, 3 * j + k) for k in range(3)]
                                   + [(block, send2, recv2, 4 * j), (block, send2, recv2, 4 * j + 1),
                                      (half, send2, recv2, 4 * j + 2), (half, send2, recv2, 4 * j + 3), (block, send3, recv3, j)]):
            _remote(ref, ref, send.at[k], recv.at[k], pos).wait_send()


def _flight_call(step, name, zones, sems_in, nsems_out, after, carried):
    n, m, k = len(zones), len(carried), len(sems_in)

    def body(*refs):
        zs = refs[:n]
        given = refs[n + m:n + m + k]
        made = refs[n + m + k + len(after):n + m + k + len(after) + (2 if nsems_out else 0)]
        step(zs, *given, *made, _position())

    sem_out = (pltpu.SemaphoreType.DMA((nsems_out,)),) * 2 if nsems_out else ()
    outs = pl.pallas_call(
        body, name=name,
        out_shape=(*sem_out, *[pltpu.HBM(z.shape, z.dtype) for z in zones], *[jax.ShapeDtypeStruct(a.shape, a.dtype) for a in carried]),
        in_specs=[IN_HBM] * n + [ANY] * m + [SEMAPHORES] * k + [ANY] * len(after),
        out_specs=(*[SEMAPHORES] * len(sem_out), *[IN_HBM] * n, *[ANY] * m),
        input_output_aliases={j: len(sem_out) + j for j in range(n + m)},
        compiler_params=pltpu.CompilerParams(has_side_effects=IN_FLIGHT),
    )(*[_in_hbm(z) for z in zones], *carried, *sems_in, *after)
    sems = list(outs[:len(sem_out)])
    return sems, list(outs[len(sem_out):len(sem_out) + n]), list(outs[len(sem_out) + n:])


def _gather_start(zones, after, carried, name):
    (send1, recv1), zones, carried = _flight_call(_gather_step1, name, zones, [], 3 * len(zones), after, carried)
    return {"s1": send1, "r1": recv1, "zones": zones}, carried


def _gather_mid(flight, after, carried, name):
    step = lambda zs, recv1, send, recv, pos: _gather_step2(zs, recv1, send, recv, pos)
    (send2, recv2), zones, carried = _flight_call(step, name, flight["zones"], [flight["r1"]], 4 * len(flight["zones"]), after, carried)
    return {**flight, "s2": send2, "r2": recv2, "zones": zones}, carried


def _gather_late(flight, after, name):
    step = lambda zs, recv2, send, recv, pos: _gather_step3(zs, recv2, send, recv, pos)
    (send3, recv3), zones, _ = _flight_call(step, name, flight["zones"], [flight["r2"]], len(flight["zones"]), after, [])
    return {**flight, "s3": send3, "r3": recv3, "zones": zones}


def _gather_end(flight, after, name):
    sems = [flight[k] for k in ("s1", "r1", "s2", "r2", "s3", "r3")]
    _, zones, _ = _flight_call(_gather_step4, name, flight["zones"], sems, 0, after, [])
    return zones


def _scatter_start(parts, name):
    n = len(parts)
    lands = [_in_hbm(lax.empty((3,) + p.shape[1:], p.dtype)) for p in parts]

    def body(*refs):
        ins, zones = refs[:n], refs[n:2 * n]
        send_sems, recv_sems = refs[2 * n:2 * n + 2]
        x, y, c = _position()
        for j in range(n):
            for q, (px, py) in enumerate(_chips(x, y)):
                _remote(ins[j].at[2 * px + py], zones[j].at[q], send_sems.at[3 * j + q], recv_sems.at[3 * j + q],
                        (px, py, c)).start()

    outs = pl.pallas_call(
        body, name=name,
        out_shape=(pltpu.SemaphoreType.DMA((3 * n,)), pltpu.SemaphoreType.DMA((3 * n,)),
                   *[pltpu.HBM(p.shape, p.dtype) for p in parts], *[pltpu.HBM(z.shape, z.dtype) for z in lands]),
        in_specs=[IN_HBM] * (2 * n), out_specs=(SEMAPHORES, SEMAPHORES, *[IN_HBM] * (2 * n)),
        input_output_aliases={j: 2 + j for j in range(2 * n)},
        compiler_params=pltpu.CompilerParams(has_side_effects=IN_FLIGHT),
    )(*[_in_hbm(p) for p in parts], *lands)
    return outs[0], outs[1], list(outs[2:2 + n]), list(outs[2 + n:])


def _scatter_end(parts, zones, send_sems, recv_sems, after, name):
    n = len(parts)

    def body(*refs):
        ins, zs = refs[:n], refs[n:2 * n]
        s, r = refs[2 * n:2 * n + 2]
        me = _position()
        for j in range(n):
            for q in range(3):
                copy = _remote(ins[j].at[0], zs[j].at[q], s.at[3 * j + q], r.at[3 * j + q], me)
                copy.wait_send()
                copy.wait_recv()

    outs = pl.pallas_call(
        body, name=name,
        out_shape=(*[pltpu.HBM(p.shape, p.dtype) for p in parts], *[pltpu.HBM(z.shape, z.dtype) for z in zones]),
        in_specs=[IN_HBM] * (2 * n) + [SEMAPHORES, SEMAPHORES, ANY], out_specs=tuple([IN_HBM] * (2 * n)),
        input_output_aliases={j: j for j in range(2 * n)},
        compiler_params=pltpu.CompilerParams(has_side_effects=IN_FLIGHT),
    )(*parts, *zones, send_sems, recv_sems, after)
    return list(outs[:n]), list(outs[n:])


def _ada_forward(c, ada_w, ada_b, small):
    ns = len(small)

    def body(c_ref, w_ref, b_ref, *rest):
        small_refs, (cact_ref, mod_ref), gathered = rest[:ns], rest[ns:ns + 2], rest[ns + 2:2 * ns + 2]
        gbuf, modrow, send_sems, recv_sems = rest[2 * ns + 2:]
        pos = _position()
        me = _index(pos)

        def to_all(*exchanged):
            copies = []
            for ref, row in exchanged:
                for k in range(1, NDEV):
                    peer, _ = _peer(pos, k)
                    copy = pltpu.make_async_remote_copy(
                        src_ref=ref.at[me], dst_ref=ref.at[me], send_sem=send_sems.at[row, k - 1],
                        recv_sem=recv_sems.at[row, k - 1], device_id=peer, device_id_type=pl.DeviceIdType.MESH)
                    copy.start()
                    copies.append(copy)
            for copy in copies:
                copy.wait()

        cact_ref[me] = _silu(c_ref[...])
        for j in range(ns):
            gathered[j][me] = small_refs[j][...]
        to_all((cact_ref, 0), *[(gathered[j], 2 + j) for j in range(ns)])
        rows = lax.broadcasted_iota(jnp.int32, (NDEV, D), 0)
        cact = jnp.zeros((NDEV, D), F32)
        for e in range(NDEV):
            cact = jnp.where(rows == e, cact_ref[e], cact)
        cact = cact.astype(BF16)
        for l in range(DEPTH):
            gbuf[me, l] = jnp.dot(cact, w_ref[l].astype(BF16), preferred_element_type=F32)
        to_all((gbuf, 1))
        mine = lax.broadcasted_iota(jnp.int32, (NDEV, ADA_NC), 0) == me
        for l in range(DEPTH):
            for d in range(NDEV):
                modrow[:, d * ADA_NC:(d + 1) * ADA_NC] = jnp.sum(jnp.where(mine, gbuf[d, l], 0.0), axis=0, keepdims=True)
            full = modrow[...] + b_ref[l:l + 1, :]
            for w in range(3):
                mod_ref[l, w] = full[:, w * D:(w + 1) * D]

    outs = pl.pallas_call(
        body, name="ada_forward",
        out_shape=[jax.ShapeDtypeStruct((NDEV, 1, D), F32), jax.ShapeDtypeStruct((DEPTH, 3, 1, D), F32)]
        + [jax.ShapeDtypeStruct((NDEV,) + a.shape, a.dtype) for a in small],
        in_specs=[VMEM_FULL] * (3 + ns), out_specs=[VMEM_FULL] * (2 + ns),
        scratch_shapes=[pltpu.VMEM((NDEV, DEPTH, NDEV, ADA_NC), F32), pltpu.VMEM((1, 3 * D), F32),
                        pltpu.SemaphoreType.DMA((2 + ns, NDEV - 1)), pltpu.SemaphoreType.DMA((2 + ns, NDEV - 1))],
        compiler_params=_params(),
    )(c, ada_w, ada_b, *small)
    return outs[0], outs[1], list(outs[2:])


def _mod_spec(layer, which, ngrid):
    index = {1: lambda i: (layer, which, 0, 0), 2: lambda i, j: (layer, which, 0, 0)}[ngrid]
    return pl.BlockSpec((None, None, 1, D), index)


W_BLOCKS = 4


def _norm_proj(x, mod, norm_g3, wg, layer, name):
    nb = wg.shape[-1]
    tm = 1024
    wb = W_BLOCKS

    def body(x_ref, g_ref, shift_ref, scale_ref, w_ref, ht_ref, p_ref, h_ref):
        @pl.when(pl.program_id(1) == 0)
        def _():
            xv = x_ref[...]
            r = lax.rsqrt(jnp.mean(xv * xv, axis=-1, keepdims=True) + EPS)
            hn = xv * r * g_ref[...]
            h = hn * (1.0 + scale_ref[...]) + shift_ref[...]
            h_ref[...] = h.astype(BF16)
            ht_ref[...] = h.T.astype(BF16)

        hv = h_ref[...]
        for b in range(wb):
            p_ref[:, b * nb:(b + 1) * nb] = jnp.dot(hv, w_ref[b], preferred_element_type=F32).astype(BF16)

    return pl.pallas_call(
        body, name=name, grid=(S // tm, NDEV // wb),
        out_shape=[jax.ShapeDtypeStruct((D, S), BF16), jax.ShapeDtypeStruct((S, NDEV * nb), BF16)],
        in_specs=[pl.BlockSpec((tm, D), lambda i, d: (i, 0)),
                  pl.BlockSpec((None, 1, D), lambda i, d: (layer, 0, 0)),
                  _mod_spec(layer, 0, 2), _mod_spec(layer, 1, 2),
                  pl.BlockSpec((wb, D, nb), lambda i, d: (d, 0, 0))],
        out_specs=[pl.BlockSpec((D, tm), lambda i, d: (0, i)), pl.BlockSpec((tm, wb * nb), lambda i, d: (i, d))],
        scratch_shapes=[pltpu.VMEM((tm, D), BF16)],
        compiler_params=_params(("arbitrary", "arbitrary")),
    )(x, norm_g3, mod, mod, wg)


def _out_proj(ycat, w_out, x, mod, layer, name, final=None):
    tm = 512
    e = w_out.shape[0]

    def body(y_ref, w_ref, x_ref, gate_ref, *rest):
        acc = jnp.dot(y_ref[...], w_ref[...], preferred_element_type=F32)
        xv = x_ref[...] + gate_ref[...] * acc
        if final is None:
            xn_ref, o_ref = rest
            o_ref[...] = acc.astype(BF16)
            xn_ref[...] = xv
            return
        t_ref, g_ref, dx_ref, o_ref, loss_ref, dg_ref = rest
        o_ref[...] = acc.astype(BF16)

        @pl.when(pl.program_id(0) == 0)
        def _():
            loss_ref[...] = jnp.zeros_like(loss_ref)
            dg_ref[...] = jnp.zeros_like(dg_ref)

        g = g_ref[...]
        r = lax.rsqrt(jnp.mean(xv * xv, axis=-1, keepdims=True) + EPS)
        xn = xv * r
        err = xn * g - t_ref[...]
        loss_ref[...] += 0.5 * jnp.sum(jnp.mean(err * err, axis=-1, keepdims=True), axis=0, keepdims=True)
        dy = err * (1.0 / D)
        dg_ref[...] += jnp.sum(dy * xn, axis=0, keepdims=True)
        u = dy * g
        dx_ref[...] = r * (u - xn * jnp.mean(xn * u, axis=-1, keepdims=True))

    tile = pl.BlockSpec((tm, D), lambda i: (i, 0))
    row = pl.BlockSpec((1, D), lambda i: (0, 0))
    out_shape = [jax.ShapeDtypeStruct((S, D), F32), jax.ShapeDtypeStruct((S, D), BF16)]
    in_specs = [pl.BlockSpec((tm, e), lambda i: (i, 0)), pl.BlockSpec((e, D), lambda i: (0, 0)), tile, _mod_spec(layer, 2, 1)]
    out_specs, operands = [tile, tile], [ycat, w_out, x, mod]
    if final is not None:
        out_shape += [jax.ShapeDtypeStruct((1, LANE), F32), jax.ShapeDtypeStruct((1, D), F32)]
        in_specs += [tile, row]
        out_specs += [pl.BlockSpec((1, LANE), lambda i: (0, 0)), row]
        operands += list(final)
    return pl.pallas_call(
        body, name=name, grid=(S // tm,), out_shape=out_shape, in_specs=in_specs, out_specs=out_specs,
        compiler_params=_params(("arbitrary",)),
    )(*operands)


def _out_bwd(dx, out, ycat, w_out, mod, layer, carried, name):
    tm = 512
    nsteps = S // tm
    e = ycat.shape[1]
    rb = e // NDEV
    nc = len(carried)

    def body(dx_ref, o_ref, y_ref, w_ref, gate_ref, *rest):
        dy_ref, gw_ref, dgate_ref = rest[nc:nc + 3]
        acc = rest[-1]
        step = pl.program_id(0)

        @pl.when(step == 0)
        def _():
            dgate_ref[...] = jnp.zeros_like(dgate_ref)
            acc[...] = jnp.zeros_like(acc)

        dxv = dx_ref[...]
        d_out = (gate_ref[...] * dxv).astype(BF16)
        dgate_ref[...] += jnp.sum(dxv * o_ref[...].astype(F32), axis=0, keepdims=True)
        dy_ref[...] = lax.dot_general(d_out, w_ref[...], (((1,), (1,)), ((), ())), preferred_element_type=F32).astype(BF16)
        acc[...] += lax.dot_general(y_ref[...], d_out, (((0,), (0,)), ((), ())), preferred_element_type=F32)

        @pl.when(step == nsteps - 1)
        def _():
            for d in range(NDEV):
                gw_ref[d % 2, d // 2] = acc[d * rb:(d + 1) * rb, :].astype(BF16)

    tile = pl.BlockSpec((tm, D), lambda i: (i, 0))
    wide = pl.BlockSpec((tm, e), lambda i: (i, 0))
    outs = pl.pallas_call(
        body, name=name, grid=(nsteps,),
        out_shape=[jax.ShapeDtypeStruct((S, e), BF16), jax.ShapeDtypeStruct((2, NDEV // 2, rb, D), BF16),
                   jax.ShapeDtypeStruct((1, D), F32)] + [jax.ShapeDtypeStruct(a.shape, a.dtype) for a in carried],
        in_specs=[tile, tile, wide, pl.BlockSpec((e, D), lambda i: (0, 0)), _mod_spec(layer, 2, 1)] + [ANY] * nc,
        out_specs=[wide, pl.BlockSpec((2, NDEV // 2, rb, D), lambda i: (0, 0, 0, 0)), pl.BlockSpec((1, D), lambda i: (0, 0))]
        + [ANY] * nc,
        scratch_shapes=[pltpu.VMEM((e, D), F32)],
        input_output_aliases={5 + k: 3 + k for k in range(nc)},
        compiler_params=_params(("arbitrary",)),
    )(dx, out, ycat, w_out, mod, *carried)
    return outs[0], outs[1], outs[2], list(outs[3:])


def _weight_grad(h_t, d_proj, name):
    nb = d_proj.shape[1] // NDEV

    def body(ht_ref, dp_ref, o_ref):
        o_ref[...] = jnp.dot(ht_ref[...], dp_ref[...], preferred_element_type=F32).astype(BF16)

    return pl.pallas_call(
        body, name=name, grid=(NDEV,), out_shape=jax.ShapeDtypeStruct((2, NDEV // 2, D, nb), BF16),
        in_specs=[pl.BlockSpec((D, S), lambda d: (0, 0)), pl.BlockSpec((S, nb), lambda d: (0, d))],
        out_specs=pl.BlockSpec((None, None, D, nb), lambda d: (d % 2, d // 2, 0, 0)),
        compiler_params=_params(("arbitrary",)),
    )(h_t, d_proj)


def _dh_norm_bwd(d_proj, wg, x, dx, mod, norm_g3, layer, carried, name):
    nb = wg.shape[-1]
    tm = 512
    wb = W_BLOCKS
    rc = 128

    def body(dp_ref, w_ref, x_ref, dx_ref, g_ref, scale_ref, carried_ref,
             dxi_ref, dshift_ref, dscale_ref, dg_ref, carried_out, acc):
        i, d = pl.program_id(0), pl.program_id(1)
        nt = (((1,), (1,)), ((), ()))
        part = lax.dot_general(dp_ref[:, :nb], w_ref[0], nt, preferred_element_type=F32)
        for b in range(1, wb):
            part += lax.dot_general(dp_ref[:, b * nb:(b + 1) * nb], w_ref[b], nt, preferred_element_type=F32)

        @pl.when(d == 0)
        def _():
            acc[...] = part

        @pl.when(d != 0)
        def _():
            acc[...] += part

        @pl.when(jnp.logical_and(i == 0, d == 0))
        def _():
            dshift_ref[...] = jnp.zeros_like(dshift_ref)
            dscale_ref[...] = jnp.zeros_like(dscale_ref)
            dg_ref[...] = jnp.zeros_like(dg_ref)

        @pl.when(d == NDEV // wb - 1)
        def _():
            g = g_ref[...]
            scale1 = 1.0 + scale_ref[...]

            def chunk(k, sums):
                rows = pl.ds(pl.multiple_of(k * rc, rc), rc)
                xv, dhv = x_ref[rows, :], acc[rows, :]
                r = lax.rsqrt(jnp.mean(xv * xv, axis=-1, keepdims=True) + EPS)
                xn = xv * r
                dhn = dhv * scale1
                u = dhn * g
                dxi_ref[rows, :] = dx_ref[rows, :] + r * (u - xn * jnp.mean(xn * u, axis=-1, keepdims=True))
                return (sums[0] + jnp.sum(dhv, axis=0, keepdims=True),
                        sums[1] + jnp.sum(dhv * (xn * g), axis=0, keepdims=True),
                        sums[2] + jnp.sum(dhn * xn, axis=0, keepdims=True))

            zero = jnp.zeros((1, D), F32)
            sums = lax.fori_loop(0, tm // rc, chunk, (zero, zero, zero))
            dshift_ref[...] += sums[0]
            dscale_ref[...] += sums[1]
            dg_ref[...] += sums[2]

    tile = pl.BlockSpec((tm, D), lambda i, d: (i, 0))
    row = pl.BlockSpec((1, D), lambda i, d: (0, 0))
    return pl.pallas_call(
        body, name=name, grid=(S // tm, NDEV // wb),
        out_shape=[jax.ShapeDtypeStruct((S, D), F32)] + [jax.ShapeDtypeStruct((1, D), F32)] * 3
        + [jax.ShapeDtypeStruct(carried.shape, carried.dtype)],
        in_specs=[pl.BlockSpec((tm, wb * nb), lambda i, d: (i, d)), pl.BlockSpec((wb, D, nb), lambda i, d: (d, 0, 0)),
                  tile, tile, pl.BlockSpec((None, 1, D), lambda i, d: (layer, 0, 0)), _mod_spec(layer, 1, 2), ANY],
        out_specs=[tile, row, row, row, ANY], scratch_shapes=[pltpu.VMEM((tm, D), F32)],
        input_output_aliases={6: 4}, compiler_params=_params(("arbitrary", "arbitrary")),
    )(d_proj, wg, x, dx, norm_g3, mod, carried)


TS = 256
NCH = TS // CHUNK
HALO_BLOCKS = TS // HALO


def _halo_before(width, col_block):
    return pl.BlockSpec((HALO, width), lambda i: (jnp.maximum(i * HALO_BLOCKS - 1, 0), col_block))


def _halo_after(width, col_block):
    return pl.BlockSpec((HALO, width), lambda i: (jnp.minimum((i + 1) * HALO_BLOCKS, S // HALO - 1), col_block))


def _shift_down(ext, k):
    return pltpu.roll(ext, k, 0)[HALO:]


def _shift_up(ext, k):
    return pltpu.roll(ext, ext.shape[0] - k, 0)[:ext.shape[0] - HALO]


def _layer_norm_head(v, lg, lb):
    mu = jnp.mean(v, axis=-1, keepdims=True)
    vc = v - mu
    rstd = lax.rsqrt(jnp.mean(vc * vc, axis=-1, keepdims=True) + EPS)
    vhat = vc * rstd
    return vhat, rstd, vhat * lg + lb


def _causal_mask():
    return lax.broadcasted_iota(jnp.int32, (CHUNK, CHUNK), 0) >= lax.broadcasted_iota(jnp.int32, (CHUNK, CHUNK), 1)


def _region_per_head(head, step):
    for j in range(E_A // HEAD):
        pl.when(step >= 0)(functools.partial(head, j))


def _even_mix_fwd(proj, convw, ln_g3, ln_b3, sgu_w, sgu_bcol, wl, after, name):
    def body(pj_ref, hh_ref, hc_ref, cw_ref, lg_ref, lb_ref, sw_ref, sb_ref, *rest):
        y_ref = rest[-1]
        step = pl.program_id(0)
        live = (step > 0).astype(F32)
        causal = _causal_mask()

        def head(j):
            cols = slice(j * HEAD, (j + 1) * HEAD)
            w0, w1, w2 = cw_ref[0:1, cols], cw_ref[1:2, cols], cw_ref[2:3, cols]
            lg, lb = lg_ref[:, cols], lb_ref[:, cols]
            wm = jnp.where(causal, sw_ref[j], 0.0).astype(BF16)
            bias = sb_ref[j]

            def split(s, rows, cols=cols):
                return pj_ref[rows, s * E_A + cols.start:s * E_A + cols.stop].astype(F32)

            prev_tail = hc_ref[:, cols].astype(F32) * hh_ref[:, cols].astype(F32) * live
            for n in range(NCH):
                rows = slice(n * CHUNK, (n + 1) * CHUNK)
                p = split(2, rows) * split(0, rows)
                ext = jnp.concatenate([prev_tail, p], axis=0)
                prev_tail = p[CHUNK - HALO:]
                cv = w2 * p + w1 * _shift_down(ext, 1) + w0 * _shift_down(ext, 2)
                y_ref[rows, cols] = (split(1, rows) * cv * _silu(split(3, rows))).astype(BF16)
                _, _, vn = _layer_norm_head(split(5, rows), lg, lb)
                mixed = jnp.dot(wm, vn.astype(BF16), preferred_element_type=F32) + bias
                y_ref[rows, E_A + cols.start:E_A + cols.stop] = (split(4, rows) * mixed * _silu(split(6, rows))).astype(BF16)

        _region_per_head(head, step)

    const3 = lambda i: (wl, 0, 0)
    const4 = lambda i: (wl, 0, 0, 0)
    return pl.pallas_call(
        body, name=name, grid=(S // TS,), out_shape=jax.ShapeDtypeStruct((S, 2 * E_A), BF16),
        in_specs=[pl.BlockSpec((TS, 7 * E_A), lambda i: (i, 0)), _halo_before(E_A, 0), _halo_before(E_A, 2),
                  pl.BlockSpec((None, 3, E_A), const3), pl.BlockSpec((None, 1, E_A), const3),
                  pl.BlockSpec((None, 1, E_A), const3), pl.BlockSpec((None, NDEV, CHUNK, CHUNK), const4),
                  pl.BlockSpec((None, NDEV, CHUNK, 1), const4)] + [ANY] * len(after),
        out_specs=pl.BlockSpec((TS, 2 * E_A), lambda i: (i, 0)),
        compiler_params=_params(("arbitrary",)),
    )(proj, proj, proj, convw, ln_g3, ln_b3, sgu_w, sgu_bcol, *after)


def _even_mix_bwd(proj, d_ycat, convw, ln_g3, ln_b3, sgu_w, sgu_bcol, wl, name):
    nsteps = S // TS

    def body(pj_ref, hh_ref, hc_ref, hb_ref, hz_ref, dy_ref, hdy_ref, cw_ref, lg_ref, lb_ref, sw_ref, sb_ref,
             dp_ref, dcw_ref, dlg_ref, dlb_ref, dsw_ref, dsb_ref):
        step = pl.program_id(0)

        @pl.when(step == 0)
        def _():
            for ref in (dcw_ref, dlg_ref, dlb_ref, dsw_ref, dsb_ref):
                ref[...] = jnp.zeros_like(ref)

        live_before = (step > 0).astype(F32)
        live_after = (step < nsteps - 1).astype(F32)
        causal = _causal_mask()

        def head(j):
            cols = slice(j * HEAD, (j + 1) * HEAD)
            w0, w1, w2 = cw_ref[0:1, cols], cw_ref[1:2, cols], cw_ref[2:3, cols]
            lg, lb = lg_ref[:, cols], lb_ref[:, cols]
            wmf = jnp.where(causal, sw_ref[j], 0.0)
            wm, wmt = wmf.astype(BF16), wmf.T.astype(BF16)
            bias = sb_ref[j]

            def split(s, rows, cols=cols):
                return pj_ref[rows, s * E_A + cols.start:s * E_A + cols.stop].astype(F32)

            def put(s, rows, val, cols=cols):
                dp_ref[rows, s * E_A + cols.start:s * E_A + cols.stop] = val.astype(BF16)

            ps = [split(2, slice(n * CHUNK, (n + 1) * CHUNK)) * split(0, slice(n * CHUNK, (n + 1) * CHUNK)) for n in range(NCH)]
            next_head = (hdy_ref[:, cols].astype(F32) * hb_ref[:, cols].astype(F32) * _silu(hz_ref[:, cols].astype(F32))
                         * live_after)
            acc_w = [jnp.zeros((1, HEAD), F32) for _ in range(3)]
            for n in reversed(range(NCH)):
                rows = slice(n * CHUNK, (n + 1) * CHUNK)
                p = ps[n]
                tail = ps[n - 1][CHUNK - HALO:] if n > 0 else hc_ref[:, cols].astype(F32) * hh_ref[:, cols].astype(F32) * live_before
                ext = jnp.concatenate([tail, p], axis=0)
                p1, p2 = _shift_down(ext, 1), _shift_down(ext, 2)
                cv = w2 * p + w1 * p1 + w0 * p2
                a_b, a_z = split(1, rows), split(3, rows)
                sz, dsz = _silu_and_grad(a_z)
                dya = dy_ref[rows, cols].astype(F32)
                put(1, rows, dya * cv * sz)
                put(3, rows, dya * a_b * cv * dsz)
                gcv = dya * a_b * sz
                acc_w[0] += jnp.sum(gcv * p2, axis=0, keepdims=True)
                acc_w[1] += jnp.sum(gcv * p1, axis=0, keepdims=True)
                acc_w[2] += jnp.sum(gcv * p, axis=0, keepdims=True)
                gext = jnp.concatenate([gcv, next_head], axis=0)
                next_head = gcv[:HALO]
                dpv = w2 * gcv + w1 * _shift_up(gext, 1) + w0 * _shift_up(gext, 2)
                put(2, rows, dpv * split(0, rows))
                put(0, rows, dpv * split(2, rows))
            for k in range(3):
                dcw_ref[k:k + 1, cols] += acc_w[k]

            acc_lg, acc_lb = jnp.zeros((1, HEAD), F32), jnp.zeros((1, HEAD), F32)
            acc_sw, acc_sb = jnp.zeros((CHUNK, CHUNK), F32), jnp.zeros((CHUNK, 1), F32)
            for n in range(NCH):
                rows = slice(n * CHUNK, (n + 1) * CHUNK)
                u, z = split(4, rows), split(6, rows)
                vhat, rstd, vn = _layer_norm_head(split(5, rows), lg, lb)
                vn16 = vn.astype(BF16)
                mixed = jnp.dot(wm, vn16, preferred_element_type=F32) + bias
                sz, dsz = _silu_and_grad(z)
                dyb = dy_ref[rows, E_A + cols.start:E_A + cols.stop].astype(F32)
                put(4, rows, dyb * mixed * sz)
                put(6, rows, dyb * u * mixed * dsz)
                dmix = dyb * u * sz
                dmix16 = dmix.astype(BF16)
                acc_sb += jnp.sum(dmix, axis=1, keepdims=True)
                acc_sw += lax.dot_general(dmix16, vn16, (((1,), (1,)), ((), ())), preferred_element_type=F32)
                dvn = jnp.dot(wmt, dmix16, preferred_element_type=F32)
                acc_lg += jnp.sum(dvn * vhat, axis=0, keepdims=True)
                acc_lb += jnp.sum(dvn, axis=0, keepdims=True)
                dvh = dvn * lg
                put(5, rows, rstd * (dvh - jnp.mean(dvh, axis=-1, keepdims=True)
                                     - vhat * jnp.mean(dvh * vhat, axis=-1, keepdims=True)))
            dlg_ref[:, cols] += acc_lg
            dlb_ref[:, cols] += acc_lb
            dsw_ref[j] += jnp.where(causal, acc_sw, 0.0)
            dsb_ref[j] += acc_sb

        _region_per_head(head, step)

    const3 = lambda i: (wl, 0, 0)
    const4 = lambda i: (wl, 0, 0, 0)
    fixed2 = lambda i: (0, 0)
    fixed3 = lambda i: (0, 0, 0)
    return pl.pallas_call(
        body, name=name, grid=(nsteps,),
        out_shape=[jax.ShapeDtypeStruct((S, 7 * E_A), BF16), jax.ShapeDtypeStruct((3, E_A), F32),
                   jax.ShapeDtypeStruct((1, E_A), F32), jax.ShapeDtypeStruct((1, E_A), F32),
                   jax.ShapeDtypeStruct((NDEV, CHUNK, CHUNK), F32), jax.ShapeDtypeStruct((NDEV, CHUNK, 1), F32)],
        in_specs=[pl.BlockSpec((TS, 7 * E_A), lambda i: (i, 0)), _halo_before(E_A, 0), _halo_before(E_A, 2),
                  _halo_after(E_A, 1), _halo_after(E_A, 3),
                  pl.BlockSpec((TS, 2 * E_A), lambda i: (i, 0)), _halo_after(E_A, 0),
                  pl.BlockSpec((None, 3, E_A), const3), pl.BlockSpec((None, 1, E_A), const3),
                  pl.BlockSpec((None, 1, E_A), const3), pl.BlockSpec((None, NDEV, CHUNK, CHUNK), const4),
                  pl.BlockSpec((None, NDEV, CHUNK, 1), const4)],
        out_specs=[pl.BlockSpec((TS, 7 * E_A), lambda i: (i, 0)), pl.BlockSpec((3, E_A), fixed2),
                   pl.BlockSpec((1, E_A), fixed2), pl.BlockSpec((1, E_A), fixed2),
                   pl.BlockSpec((NDEV, CHUNK, CHUNK), fixed3), pl.BlockSpec((NDEV, CHUNK, 1), fixed3)],
        compiler_params=_params(("arbitrary",)),
    )(proj, proj, proj, proj, proj, d_ycat, d_ycat, convw, ln_g3, ln_b3, sgu_w, sgu_bcol)


def _window_count(step, n, win, ext_before):
    rows = CHUNK if ext_before else CHUNK + HALO
    t = step * TS + n * CHUNK + lax.broadcasted_iota(jnp.int32, (rows, 1), 0)
    return jnp.minimum(t + 1, win).astype(F32)


def _pool_weight(wp_ref, g):
    return jnp.concatenate([wp_ref[d, g] for d in range(NDEV)], axis=0)


def _pooled_chunk(p, tail, win, count):
    sums = jnp.concatenate([tail, p], axis=0)
    shift = 1
    while shift < win:
        sums = sums + pltpu.roll(sums, shift, 0)
        shift *= 2
    return sums[HALO:] / count - p


def _pool_mix_fwd(proj, wpool, pscale4, wl, after, name):
    e_c = 4 * G_C

    def body(pj_ref, hp_ref, wp_ref, ps_ref, *rest):
        y_ref, pooled_scr, yraw_scr = rest[-3:]
        step = pl.program_id(0)
        live = (step > 0).astype(F32)
        for g, win in enumerate(POOL_WINDOWS):
            for q in range(G_C // LANE):
                cols = slice(g * G_C + q * LANE, g * G_C + (q + 1) * LANE)
                tail = hp_ref[:, cols].astype(F32) * live
                for n in range(NCH):
                    rows = slice(n * CHUNK, (n + 1) * CHUNK)
                    p = pj_ref[rows, cols].astype(F32)
                    pooled_scr[rows, q * LANE:(q + 1) * LANE] = _pooled_chunk(
                        p, tail, win, _window_count(step, n, win, True)).astype(BF16)
                    tail = p[CHUNK - HALO:]
            yraw_scr[...] = jnp.dot(pooled_scr[...], _pool_weight(wp_ref, g), preferred_element_type=F32)
            for q in range(G_C // LANE):
                cols = slice(g * G_C + q * LANE, g * G_C + (q + 1) * LANE)
                for n in range(NCH):
                    rows = slice(n * CHUNK, (n + 1) * CHUNK)
                    z = pj_ref[rows, e_c + cols.start:e_c + cols.stop].astype(F32)
                    y_ref[rows, cols] = (yraw_scr[rows, q * LANE:(q + 1) * LANE] * ps_ref[:, cols] * _silu(z)).astype(BF16)

    return pl.pallas_call(
        body, name=name, grid=(S // TS,), out_shape=jax.ShapeDtypeStruct((S, e_c), BF16),
        in_specs=[pl.BlockSpec((TS, 2 * e_c), lambda i: (i, 0)), _halo_before(e_c, 0),
                  pl.BlockSpec((NDEV, 4, G_C // NDEV, G_C), lambda i: (0, 0, 0, 0)),
                  pl.BlockSpec((None, 1, e_c), lambda i: (wl, 0, 0))] + [ANY] * len(after),
        out_specs=pl.BlockSpec((TS, e_c), lambda i: (i, 0)),
        scratch_shapes=[pltpu.VMEM((TS, G_C), BF16), pltpu.VMEM((TS, G_C), F32)],
        compiler_params=_params(("arbitrary",)),
    )(proj, proj, wpool, pscale4, *after)


def _pool_mix_bwd(proj, d_ycat, wpool, pscale4, wl, name):
    e_c = 4 * G_C
    nsteps = S // TS
    rb = G_C // NDEV

    def body(pj_ref, hp_ref, hz_ref, dy_ref, hdy_ref, wp_ref, ps_ref,
             dp_ref, dps_ref, dwp_ref, pooled_scr, yraw_scr, dyraw_scr, dpool_scr, acc_w):
        step = pl.program_id(0)

        @pl.when(step == 0)
        def _():
            dps_ref[...] = jnp.zeros_like(dps_ref)
            acc_w[...] = jnp.zeros_like(acc_w)

        live_before = (step > 0).astype(F32)
        live_after = (step < nsteps - 1).astype(F32)
        for g, win in enumerate(POOL_WINDOWS):
            weight = _pool_weight(wp_ref, g)
            for q in range(G_C // LANE):
                cols = slice(g * G_C + q * LANE, g * G_C + (q + 1) * LANE)
                tail = hp_ref[:, cols].astype(F32) * live_before
                for n in range(NCH):
                    rows = slice(n * CHUNK, (n + 1) * CHUNK)
                    p = pj_ref[rows, cols].astype(F32)
                    pooled_scr[rows, q * LANE:(q + 1) * LANE] = _pooled_chunk(
                        p, tail, win, _window_count(step, n, win, True)).astype(BF16)
                    tail = p[CHUNK - HALO:]
            yraw_scr[...] = jnp.dot(pooled_scr[...], weight, preferred_element_type=F32)
            for q in range(G_C // LANE):
                cols = slice(g * G_C + q * LANE, g * G_C + (q + 1) * LANE)
                local = slice(q * LANE, (q + 1) * LANE)
                scale = ps_ref[:, cols]
                acc_ps = jnp.zeros((1, LANE), F32)
                for n in range(NCH):
                    rows = slice(n * CHUNK, (n + 1) * CHUNK)
                    sz, dsz = _silu_and_grad(pj_ref[rows, e_c + cols.start:e_c + cols.stop].astype(F32))
                    dyv = dy_ref[rows, cols].astype(F32)
                    yraw = yraw_scr[rows, local]
                    dyraw_scr[rows, local] = (dyv * scale * sz).astype(BF16)
                    acc_ps += jnp.sum(dyv * yraw * sz, axis=0, keepdims=True)
                    dp_ref[rows, e_c + cols.start:e_c + cols.stop] = (dyv * yraw * scale * dsz).astype(BF16)
                dps_ref[:, cols] += acc_ps
                dyraw_scr[TS:, local] = (hdy_ref[:, cols].astype(F32) * scale * _silu(hz_ref[:, cols].astype(F32))
                                         * live_after).astype(BF16)
            dpool_scr[...] = lax.dot_general(dyraw_scr[...], weight, (((1,), (1,)), ((), ())), preferred_element_type=F32)
            acc_w[g] += lax.dot_general(pooled_scr[...], dyraw_scr[:TS, :], (((0,), (0,)), ((), ())),
                                        preferred_element_type=F32)
            for q in range(G_C // LANE):
                cols = slice(g * G_C + q * LANE, g * G_C + (q + 1) * LANE)
                local = slice(q * LANE, (q + 1) * LANE)
                for n in range(NCH):
                    rows = slice(n * CHUNK, (n + 1) * CHUNK)
                    ext = dpool_scr[n * CHUNK:(n + 1) * CHUNK + HALO, local]
                    sums = ext / _window_count(step, n, win, False)
                    shift = 1
                    while shift < win:
                        sums = sums + pltpu.roll(sums, CHUNK + HALO - shift, 0)
                        shift *= 2
                    dp_ref[rows, cols] = (sums[:CHUNK] - ext[:CHUNK]).astype(BF16)

        @pl.when(step == nsteps - 1)
        def _():
            for g in range(4):
                for d in range(NDEV):
                    dwp_ref[d % 2, d // 2, g] = acc_w[g, d * rb:(d + 1) * rb, :].astype(BF16)

    in_specs = [pl.BlockSpec((TS, 2 * e_c), lambda i: (i, 0)), _halo_before(e_c, 0), _halo_after(e_c, 1),
                pl.BlockSpec((TS, e_c), lambda i: (i, 0)), _halo_after(e_c, 0),
                pl.BlockSpec((NDEV, 4, rb, G_C), lambda i: (0, 0, 0, 0)),
                pl.BlockSpec((None, 1, e_c), lambda i: (wl, 0, 0))]
    args = [proj, proj, proj, d_ycat, d_ycat, wpool, pscale4]
    return pl.pallas_call(
        body, name=name, grid=(nsteps,),
        out_shape=[jax.ShapeDtypeStruct((S, 2 * e_c), BF16), jax.ShapeDtypeStruct((1, e_c), F32),
                   jax.ShapeDtypeStruct((2, NDEV // 2) + wpool.shape[1:], BF16)],
        in_specs=in_specs,
        out_specs=[pl.BlockSpec((TS, 2 * e_c), lambda i: (i, 0)), pl.BlockSpec((1, e_c), lambda i: (0, 0)),
                   pl.BlockSpec((2, NDEV // 2, 4, rb, G_C), lambda i: (0, 0, 0, 0, 0))],
        scratch_shapes=[pltpu.VMEM((TS, G_C), BF16), pltpu.VMEM((TS, G_C), F32), pltpu.VMEM((TS + HALO, G_C), BF16),
                        pltpu.VMEM((TS + HALO, G_C), F32), pltpu.VMEM((4, G_C, G_C), F32)],
        compiler_params=_params(("arbitrary",)),
    )(*args)


def _adamw(w, g, m, v):
    m = ADAM_B1 * m + (1.0 - ADAM_B1) * g
    v = ADAM_B2 * v + (1.0 - ADAM_B2) * jnp.square(g)
    m_hat = m / (1.0 - ADAM_B1 ** ADAM_STEP)
    v_hat = v / (1.0 - ADAM_B2 ** ADAM_STEP)
    delta = -ADAM_LR * (m_hat / (jnp.sqrt(v_hat) + ADAM_EPS) + ADAM_WD * w)
    return delta, m, v


def _adam_sharded(w, m, v, chip_parts, landed, my_chip, carried, name, first=0, into=()):
    _, nr, ncol = w.shape
    nl = len(chip_parts)
    tr = 256
    steps = nr // tr
    nc, ni = len(carried), len(into)

    def body(chip_ref, w_ref, m_ref, v_ref, *rest):
        parts, zones = rest[:nl], rest[nl:2 * nl]
        g_ref, d_ref, nm_ref, nv_ref = rest[2 * nl + nc + ni:2 * nl + nc + ni + 4]
        layer = pl.program_id(0)
        g = jnp.zeros((tr, ncol), F32)
        for l in range(nl):
            gl = parts[l][...].astype(F32)
            for q in range(3):
                gl = gl + zones[l][q].astype(F32)
            g = jnp.where(layer == l, gl, g)
        g_ref[...] = g
        d_ref[...], nm_ref[...], nv_ref[...] = _adamw(w_ref[...], g, m_ref[...], v_ref[...])

    def rows_of(l):
        return lambda layer, i, chip_ref: jnp.where(layer == l, i, jnp.where(layer < l, 0, steps - 1))

    spec = pl.BlockSpec((None, tr, ncol), lambda layer, i, chip_ref: (first + layer, i, 0))
    part_specs = [pl.BlockSpec((None, tr, ncol), lambda layer, i, chip_ref, l=l: (chip_ref[0], rows_of(l)(layer, i, chip_ref), 0))
                  for l in range(nl)]
    zone_specs = [pl.BlockSpec((3, tr, ncol), lambda layer, i, chip_ref, l=l: (0, rows_of(l)(layer, i, chip_ref), 0))
                  for l in range(nl)]
    grid_spec = pltpu.PrefetchScalarGridSpec(
        num_scalar_prefetch=1, grid=(nl, steps),
        in_specs=[spec, spec, spec] + part_specs + zone_specs + [ANY] * (nc + ni), out_specs=[spec] * 4 + [ANY] * nc)
    aliases = {4 + 2 * nl + k: 4 + k for k in range(nc)}
    aliases.update({4 + 2 * nl + nc + k: k for k in range(ni)})
    return pl.pallas_call(
        body, name=name, grid_spec=grid_spec,
        out_shape=[jax.ShapeDtypeStruct(w.shape, F32)] * 4 + [jax.ShapeDtypeStruct(a.shape, a.dtype) for a in carried],
        input_output_aliases=aliases, compiler_params=_params(("arbitrary", "arbitrary")),
    )(my_chip, w, m, v, *chip_parts, *landed, *carried, *into)


def _adam_small(params, name):
    n = len(params)

    def body(*refs):
        ins, outs = refs[:4 * n], refs[4 * n:]
        for k in range(n):
            w_ref, g_ref, m_ref, v_ref = ins[4 * k:4 * k + 4]
            outs[3 * k][...], outs[3 * k + 1][...], outs[3 * k + 2][...] = _adamw(w_ref[...], g_ref[...], m_ref[...], v_ref[...])

    outs = pl.pallas_call(body, name=name, out_shape=[jax.ShapeDtypeStruct(p[0].shape, F32) for p in params for _ in range(3)],
                          in_specs=[VMEM_FULL] * (4 * n), out_specs=[VMEM_FULL] * (3 * n),
                          compiler_params=_params())(*[a for p in params for a in p])
    return [list(outs[3 * k:3 * k + 3]) for k in range(n)]


def _sum_devices(gathered, name):
    _, nr, ncol = gathered.shape

    def body(g_ref, o_ref):
        acc = g_ref[0].astype(F32)
        for s in range(1, NDEV):
            acc = acc + g_ref[s].astype(F32)
        o_ref[...] = acc

    return pl.pallas_call(body, name=name, grid=(1,), out_shape=jax.ShapeDtypeStruct((nr, ncol), F32),
                          in_specs=[pl.BlockSpec((NDEV, nr, ncol), lambda i: (0, 0, 0))],
                          out_specs=pl.BlockSpec((nr, ncol), lambda i: (0, 0)),
                          compiler_params=_params(("arbitrary",)))(gathered)


def _ada_weight_adam(cact_t, dmod_mine, w, m, v):
    def body(ct_ref, dm_ref, w_ref, m_ref, v_ref, g_ref, d_ref, nm_ref, nv_ref):
        ct, dm = ct_ref[...], dm_ref[...]
        g = ct[:, 0:1] * dm[0:1, :]
        for e in range(1, NDEV):
            g = g + ct[:, e:e + 1] * dm[e:e + 1, :]
        g_ref[...] = g
        d_ref[...], nm_ref[...], nv_ref[...] = _adamw(w_ref[...], g, m_ref[...], v_ref[...])

    spec = pl.BlockSpec((None, D, ADA_NC), lambda l: (l, 0, 0))
    return pl.pallas_call(
        body, name="ada_weight_adam", grid=(DEPTH,), out_shape=[jax.ShapeDtypeStruct(w.shape, F32)] * 4,
        in_specs=[pl.BlockSpec((D, NDEV), lambda l: (0, 0)), pl.BlockSpec((None, NDEV, ADA_NC), lambda l: (l, 0, 0)),
                  spec, spec, spec],
        out_specs=[spec] * 4, compiler_params=_params(("arbitrary",)),
    )(cact_t, dmod_mine, w, m, v)


def _pad_rows(a, rows):
    a = a.reshape(-1, D)
    return jnp.pad(a, ((0, rows - a.shape[0]), (0, 0)))


def kernel(x, c, norm_g, ada_w, ada_b, ab_w_in, ab_conv_w, ab_ln_g, ab_ln_b, ab_sgu_w, ab_sgu_b, ab_w_out, c_w_in, c_pool_w, c_pool_scale, c_w_out, final_g, loss_target, m_norm_g, m_ada_w, m_ada_b, m_ab_w_in, m_ab_conv_w, m_ab_ln_g, m_ab_ln_b, m_ab_sgu_w, m_ab_sgu_b, m_ab_w_out, m_c_w_in, m_c_pool_w, m_c_pool_scale, m_c_w_out, m_final_g, v_norm_g, v_ada_w, v_ada_b, v_ab_w_in, v_ab_conv_w, v_ab_ln_g, v_ab_ln_b, v_ab_sgu_w, v_ab_sgu_b, v_ab_w_out, v_c_w_in, v_c_pool_w, v_c_pool_scale, v_c_w_out, v_final_g):
    x_pos, y_pos, c_pos = _position()
    me = _index((x_pos, y_pos, c_pos))
    core = c_pos.astype(jnp.int32).reshape(1)
    my_chip = (2 * x_pos + y_pos).astype(jnp.int32).reshape(1)
    me1 = me.astype(jnp.int32).reshape(1)
    x0 = x.reshape(S, D)
    target = loss_target.reshape(S, D)
    norm_g3 = norm_g.reshape(DEPTH, 1, D)
    ln_g3, ln_b3 = ab_ln_g.reshape(2, 1, E_A), ab_ln_b.reshape(2, 1, E_A)
    sgu_bcol = ab_sgu_b.reshape(2, NDEV, CHUNK, 1)
    rb = G_C // NDEV
    pool_w3, m_pool_w3, v_pool_w3 = (a.reshape(2, 4 * rb, G_C) for a in (c_pool_w, m_c_pool_w, v_c_pool_w))

    cact_all, mod, (convw_all, pscale_all) = _ada_forward(c, ada_w, ada_b, [ab_conv_w, c_pool_scale])
    convw = jnp.transpose(convw_all, (1, 2, 0, 3)).reshape(2, 3, E_A)
    pscale4 = jnp.transpose(pscale_all, (1, 0, 2)).reshape(2, 1, 4 * G_C)
    zones = []
    for layer in range(DEPTH):
        wl = layer // 2
        if layer % 2 == 0:
            zones.append([_to_zone(ab_w_in, wl, me1, BF16, f"cast_w_in_{layer}"), _to_zone(ab_w_out, wl, me1, BF16, f"cast_w_out_{layer}")])
        else:
            zones.append([_to_zone(c_w_in, wl, me1, BF16, f"cast_w_in_{layer}"), _to_zone(c_w_out, wl, me1, BF16, f"cast_w_out_{layer}"),
                          _to_zone(pool_w3, wl, me1, BF16, f"cast_pool_w_{layer}")])

    def gathered(flight, after, layer):
        wg = _gather_end(flight, [after], f"gather_end_{layer}")
        return [wg[0], wg[1].reshape(-1, D)] + [w.reshape(NDEV, 4, rb, G_C) for w in wg[2:]]

    flight, (mod,) = _gather_start(zones[0], [convw_all], [mod], "gather_start_0")
    flight, (mod,) = _gather_mid(flight, [z for zs in zones[1:] for z in zs], [mod], "gather_mid_0")
    next_flight, (mod,) = _gather_start(zones[1], [], [mod], "gather_start_1")
    flight = _gather_late(flight, [mod], "gather_late_0")
    xs, hts, projs, ycats, outs, gathered_w = [x0], [], [], [], [], [gathered(flight, mod, 0)]
    for layer in range(DEPTH):
        wl = layer // 2
        even = layer % 2 == 0
        wg = gathered_w[layer]
        h_t, proj = _norm_proj(xs[-1], mod, norm_g3, wg[0], layer, f"norm_proj_{layer}")
        if layer + 1 < DEPTH:
            flight, (h_t,) = _gather_mid(next_flight, [], [h_t], f"gather_mid_{layer + 1}")
            if layer + 2 < DEPTH:
                next_flight, (h_t,) = _gather_start(zones[layer + 2], [], [h_t], f"gather_start_{layer + 2}")
        if even:
            ycat = _even_mix_fwd(proj, convw, ln_g3, ln_b3, ab_sgu_w, sgu_bcol, wl, [h_t], f"even_mix_fwd_{layer}")
        else:
            ycat = _pool_mix_fwd(proj, wg[2], pscale4, wl, [h_t], f"pool_mix_fwd_{layer}")
        if layer + 1 < DEPTH:
            flight = _gather_late(flight, [ycat], f"gather_late_{layer + 1}")
        if layer + 1 < DEPTH:
            x_new, out = _out_proj(ycat, wg[1], xs[-1], mod, layer, f"out_proj_{layer}")
            gathered_w.append(gathered(flight, x_new, layer + 1))
            xs.append(x_new)
        else:
            dx, out, loss_part, d_final_g = _out_proj(ycat, wg[1], xs[-1], mod, layer, f"out_proj_{layer}",
                                                      final=(target, final_g.reshape(1, D)))
        hts.append(h_t)
        projs.append(proj)
        ycats.append(ycat)
        outs.append(out)

    d_mod, d_norm_g = [None] * DEPTH, [None] * DEPTH
    small, scatters, landed, res = {}, {}, {}, {}

    def finish_scatter(layer, after):
        send_sems, recv_sems, chip_parts, zones = scatters[layer]
        landed[layer] = _scatter_end(chip_parts, zones, send_sems, recv_sems, after, f"scatter_end_{layer}")

    def flat(a):
        return a.reshape(a.shape[0], -1, a.shape[-1])

    def sharded_adam(k, j, layers, w, m, v, carried, first=0, into=()):
        outs4 = _adam_sharded(w, m, v, [flat(landed[l][0][j]) for l in layers], [flat(landed[l][1][j]) for l in layers],
                              my_chip, carried, f"adam_{k}_{first}" if len(layers) < w.shape[0] else "adam_" + k, first, into)
        res[k] = [o.reshape(c_pool_w.shape) if k == "c_pool_w" else o for o in outs4[:4]]
        return list(outs4[4:])

    previous = None
    for layer in reversed(range(DEPTH)):
        wl = layer // 2
        even = layer % 2 == 0
        wg = gathered_w[layer]
        carried = [] if previous is None else [scatters[previous][2][0]]
        d_ycat, grad_out, d_gate, carried = _out_bwd(dx, outs[layer], ycats[layer], wg[1], mod, layer, carried, f"out_bwd_{layer}")
        if previous is not None:
            scatters[previous][2][0] = carried[0]
        parts = [None, grad_out]
        if even:
            d_proj, d_cw, d_lg, d_lb, d_sw, d_sb = _even_mix_bwd(
                projs[layer], d_ycat, convw, ln_g3, ln_b3, ab_sgu_w, sgu_bcol, wl, f"even_mix_bwd_{layer}")
            small[layer] = (d_cw, d_lg, d_lb, d_sw, d_sb)
        else:
            d_proj, d_ps, d_pool = _pool_mix_bwd(projs[layer], d_ycat, wg[2], pscale4, wl, f"pool_mix_bwd_{layer}")
            small[layer] = (d_ps,)
            parts.append(d_pool)
        parts[0] = _weight_grad(hts[layer], d_proj, f"grad_w_in_{layer}")
        pair_send, pair_recv, parts, from_sibling = _pair_start(parts, f"pair_start_{layer}")
        if layer > 0:
            dx, d_shift, d_scale, d_norm_g[layer], parts[0] = _dh_norm_bwd(
                d_proj, wg[0], xs[layer], dx, mod, norm_g3, layer, parts[0], f"dh_norm_bwd_{layer}")
            pair_after = dx
        else:
            finish_scatter(1, d_proj)
            finish_scatter(3, d_proj)
            parts[0], = sharded_adam("c_w_out", 1, (1, 3), c_w_out, m_c_w_out, v_c_w_out, [parts[0]])
            parts[0], = sharded_adam("c_pool_w", 2, (1, 3), pool_w3, m_pool_w3, v_pool_w3, [parts[0]])
            pair_after = res["c_pool_w"][0]
        parts, from_sibling = _pair_end(parts, from_sibling, pair_send, pair_recv, pair_after, f"pair_end_{layer}")
        chip_parts = _pair_sum(parts, from_sibling, core, f"pair_sum_{layer}")
        send_sems, recv_sems, chip_parts, zones = _scatter_start(chip_parts, f"scatter_start_{layer}")
        if layer == 0:
            chip_parts[0], = sharded_adam("c_w_in", 0, (1, 3), c_w_in, m_c_w_in, v_c_w_in, [chip_parts[0]])
            dx, d_shift, d_scale, d_norm_g[layer], chip_parts[0] = _dh_norm_bwd(
                d_proj, wg[0], xs[layer], dx, mod, norm_g3, layer, chip_parts[0], f"dh_norm_bwd_{layer}")
        scatters[layer] = [send_sems, recv_sems, chip_parts, zones]
        previous = layer
        d_mod[layer] = jnp.concatenate([d_shift, d_scale, d_gate], axis=0)
    grad_x = dx.reshape(x.shape)

    sections = [("norm_g", jnp.concatenate(d_norm_g, axis=0), 8),
                ("d_mod", jnp.concatenate(d_mod, axis=0), 16),
                ("ab_ln_g", jnp.concatenate([small[0][1], small[2][1]], axis=0), 8),
                ("ab_ln_b", jnp.concatenate([small[0][2], small[2][2]], axis=0), 8),
                ("ab_sgu_b", jnp.stack([small[0][4], small[2][4]]), 8),
                ("final_g", d_final_g, 8),
                ("ab_conv_w", jnp.stack([small[0][0], small[2][0]]), 8),
                ("c_pool_scale", jnp.concatenate([small[1][0], small[3][0]], axis=0), 8),
                ("ab_sgu_w", jnp.stack([small[0][3], small[2][3]]), 256)]
    offsets, at = {}, 0
    for name, _, rows in sections:
        offsets[name] = (at, rows)
        at += rows
    packed = jnp.concatenate([_pad_rows(a, rows) for _, a, rows in sections] + [jnp.zeros((-at % 32, D), F32)], axis=0)
    loss_rows = jnp.pad(loss_part, ((0, 15), (0, D - LANE)))
    small_zones = [_to_zone(packed[None], 0, me1, BF16, "place_small_grads"), _to_zone(loss_rows[None], 0, me1, F32, "place_loss")]
    small_flight, (mod,) = _gather_start(small_zones, [], [mod], "gather_small_start")

    finish_scatter(2, mod)
    sharded_adam("ab_w_out", 1, (2,), ab_w_out, m_ab_w_out, v_ab_w_out, [], first=1)
    sharded_adam("ab_w_in", 0, (2,), ab_w_in, m_ab_w_in, v_ab_w_in, [], first=1)
    finish_scatter(0, res["ab_w_in"][0])
    sharded_adam("ab_w_out", 1, (0,), ab_w_out, m_ab_w_out, v_ab_w_out, [], into=res["ab_w_out"])
    small_flight, (mod,) = _gather_mid(small_flight, [res["ab_w_out"][0]], [mod], "gather_small_mid")
    sharded_adam("ab_w_in", 0, (0,), ab_w_in, m_ab_w_in, v_ab_w_in, [mod], into=res["ab_w_in"])

    last = res["ab_w_in"][0]
    small_flight = _gather_late(small_flight, [last], "gather_small_late")
    small_grads, losses = _gather_end(small_flight, [last], "gather_small_end")
    summed = _sum_devices(small_grads, "sum_small_grads")
    loss = _sum_devices(losses, "sum_loss")[0, 0]

    def section(name, nrows, src=summed):
        start = offsets[name][0]
        return src[..., start:start + nrows, :]

    grads = {
        "norm_g": section("norm_g", DEPTH),
        "ada_b": section("d_mod", 3 * DEPTH).reshape(DEPTH, 3 * D),
        "ab_ln_g": section("ab_ln_g", 2), "ab_ln_b": section("ab_ln_b", 2),
        "ab_sgu_b": section("ab_sgu_b", 2).reshape(ab_sgu_b.shape),
        "final_g": section("final_g", 1),
        "ab_sgu_w": section("ab_sgu_w", 256).reshape(ab_sgu_w.shape),
        "ab_conv_w": lax.dynamic_slice_in_dim(section("ab_conv_w", 6).reshape(2, 3, E_A), me * HEAD, HEAD, axis=2),
        "c_pool_scale": lax.dynamic_slice_in_dim(section("c_pool_scale", 4).reshape(2, 4 * G_C), me * 256, 256, axis=1),
    }
    small_w = {"norm_g": (norm_g, m_norm_g, v_norm_g), "ada_b": (ada_b, m_ada_b, v_ada_b),
               "ab_ln_g": (ab_ln_g, m_ab_ln_g, v_ab_ln_g), "ab_ln_b": (ab_ln_b, m_ab_ln_b, v_ab_ln_b),
               "ab_sgu_b": (ab_sgu_b, m_ab_sgu_b, v_ab_sgu_b),
               "final_g": (final_g.reshape(1, D), m_final_g.reshape(1, D), v_final_g.reshape(1, D)),
               "ab_sgu_w": (ab_sgu_w, m_ab_sgu_w, v_ab_sgu_w), "ab_conv_w": (ab_conv_w, m_ab_conv_w, v_ab_conv_w),
               "c_pool_scale": (c_pool_scale, m_c_pool_scale, v_c_pool_scale)}
    updates = _adam_small([(w, grads[k], m, v) for k, (w, m, v) in small_w.items()], "adam_small")
    for k, update in zip(small_w, updates):
        res[k] = [grads[k]] + update
    res["final_g"] = [a.reshape(D) for a in res["final_g"]]

    dmod_all = section("d_mod", 3 * DEPTH, small_grads).reshape(NDEV, DEPTH, 3 * D)
    dmod_mine = jnp.transpose(lax.dynamic_slice_in_dim(dmod_all, me * ADA_NC, ADA_NC, axis=2), (1, 0, 2)).astype(F32)
    res["ada_w"] = _ada_weight_adam(jnp.transpose(cact_all.reshape(NDEV, D)), dmod_mine, ada_w, m_ada_w, v_ada_w)

    order = ["norm_g", "ada_w", "ada_b", "ab_w_in", "ab_conv_w", "ab_ln_g", "ab_ln_b", "ab_sgu_w", "ab_sgu_b",
             "ab_w_out", "c_w_in", "c_pool_w", "c_pool_scale", "c_w_out", "final_g"]
    return (loss, grad_x, *[res[k][0] for k in order], *[res[k][1] for k in order],
            *[res[k][2] for k in order], *[res[k][3] for k in order])
```

```python
import functools

import jax
import jax.numpy as jnp
from jax import lax
from jax.experimental import pallas as pl
from jax.experimental.pallas import tpu as pltpu

F32, BF16 = jnp.float32, jnp.bfloat16
S, D = 2048, 1024
NDEV = 8
DEPTH = 4
EPS = 1e-6
E_A = 1024
HEAD = 128
CHUNK = 128
POOL_WINDOWS = (2, 4, 8, 16)
G_C = 512
HALO = 16
ADA_NC = 384
MIB = 1024 * 1024
LANE = 128

ADAM_LR, ADAM_B1, ADAM_B2, ADAM_EPS, ADAM_WD, ADAM_STEP = 0.001, 0.9, 0.999, 1e-08, 0.01, 10

ANY = pl.BlockSpec(memory_space=pl.ANY)
VMEM_FULL = pl.BlockSpec(memory_space=pltpu.VMEM)
IN_HBM = pl.BlockSpec(memory_space=pltpu.HBM)
SEMAPHORES = pl.BlockSpec(memory_space=pltpu.SEMAPHORE)
IN_FLIGHT = pltpu.SideEffectType.DATAFLOW_SIDE_EFFECTING


V7X_VMEM_MIB = 64
VMEM_LIMIT_MIB = V7X_VMEM_MIB - 4


def _params(semantics=None):
    return pltpu.CompilerParams(dimension_semantics=semantics, vmem_limit_bytes=VMEM_LIMIT_MIB * MIB)


def _silu(z):
    return z * jax.nn.sigmoid(z)


def _silu_and_grad(z):
    sig = jax.nn.sigmoid(z)
    return z * sig, sig * (1.0 + z * (1.0 - sig))


def _position():
    return lax.axis_index("x"), lax.axis_index("y"), lax.axis_index("c")


def _index(pos):
    return 4 * pos[0] + 2 * pos[1] + pos[2]


def _peer(pos, k):
    flipped = tuple(1 - p if (k >> (2 - b)) & 1 else p for b, p in enumerate(pos))
    return flipped, _index(flipped)


def _remote(src, dst, send_sem, recv_sem, device):
    return pltpu.make_async_remote_copy(src_ref=src, dst_ref=dst, send_sem=send_sem, recv_sem=recv_sem,
                                        device_id=device, device_id_type=pl.DeviceIdType.MESH)


def _pair_start(parts, name):
    n = len(parts)
    lands = [_in_hbm(lax.empty(p.shape[1:], p.dtype)) for p in parts]

    def body(*refs):
        ins, zones = refs[:n], refs[n:2 * n]
        send_sems, recv_sems = refs[2 * n:2 * n + 2]
        x, y, c = _position()
        for j in range(n):
            _remote(ins[j].at[1 - c], zones[j], send_sems.at[j], recv_sems.at[j], (x, y, 1 - c)).start()

    outs = pl.pallas_call(
        body, name=name,
        out_shape=(pltpu.SemaphoreType.DMA((n,)), pltpu.SemaphoreType.DMA((n,)),
                   *[pltpu.HBM(p.shape, p.dtype) for p in parts], *[pltpu.HBM(z.shape, z.dtype) for z in lands]),
        in_specs=[IN_HBM] * (2 * n), out_specs=(SEMAPHORES, SEMAPHORES, *[IN_HBM] * (2 * n)),
        input_output_aliases={j: 2 + j for j in range(2 * n)},
        compiler_params=pltpu.CompilerParams(has_side_effects=IN_FLIGHT),
    )(*[_in_hbm(p) for p in parts], *lands)
    return outs[0], outs[1], list(outs[2:2 + n]), list(outs[2 + n:])


def _pair_end(parts, zones, send_sems, recv_sems, after, name):
    n = len(parts)

    def body(*refs):
        ins, zs = refs[:n], refs[n:2 * n]
        s, r = refs[2 * n:2 * n + 2]
        me = _position()
        for j in range(n):
            copy = _remote(ins[j].at[0], zs[j], s.at[j], r.at[j], me)
            copy.wait_send()
            copy.wait_recv()

    outs = pl.pallas_call(
        body, name=name,
        out_shape=(*[pltpu.HBM(p.shape, p.dtype) for p in parts], *[pltpu.HBM(z.shape, z.dtype) for z in zones]),
        in_specs=[IN_HBM] * (2 * n) + [SEMAPHORES, SEMAPHORES, ANY], out_specs=tuple([IN_HBM] * (2 * n)),
        input_output_aliases={j: j for j in range(2 * n)},
        compiler_params=pltpu.CompilerParams(has_side_effects=IN_FLIGHT),
    )(*parts, *zones, send_sems, recv_sems, after)
    return list(outs[:n]), list(outs[n:])


def _pair_sum(parts, from_sibling, core, name):
    n = len(parts)
    steps = 8
    p3 = [p.reshape(2, -1, p.shape[-1]) for p in parts]
    q2 = [q.reshape(-1, q.shape[-1]) for q in from_sibling]

    def body(core_ref, *refs):
        for p_ref, q_ref, o_ref in zip(refs[:n], refs[n:2 * n], refs[2 * n:]):
            o_ref[...] = (p_ref[...].astype(F32) + q_ref[...].astype(F32)).astype(BF16)

    tiles = [pl.BlockSpec((q.shape[0] // steps, q.shape[1]), lambda i, core_ref: (i, 0)) for q in q2]
    grid_spec = pltpu.PrefetchScalarGridSpec(
        num_scalar_prefetch=1, grid=(steps,),
        in_specs=[pl.BlockSpec((None, q.shape[0] // steps, q.shape[1]), lambda i, core_ref: (core_ref[0], i, 0)) for q in q2]
        + tiles, out_specs=tiles)
    outs = pl.pallas_call(body, name=name, grid_spec=grid_spec, out_shape=[jax.ShapeDtypeStruct(q.shape, BF16) for q in q2],
                          compiler_params=_params(("arbitrary",)))(core, *p3, *q2)
    return [o.reshape(q.shape) for o, q in zip(outs, from_sibling)]


def _in_hbm(a):
    return pltpu.with_memory_space_constraint(a, pltpu.HBM)


def _chips(x, y):
    return [(1 - x, y), (x, 1 - y), (1 - x, 1 - y)]


def _to_zone(a, wl, me, dtype, name):
    _, rows, cols = a.shape
    tr = 256 if rows % 256 == 0 else rows

    def body(me_ref, a_ref, o_ref):
        o_ref[...] = a_ref[...].astype(dtype)

    grid_spec = pltpu.PrefetchScalarGridSpec(
        num_scalar_prefetch=1, grid=(rows // tr,),
        in_specs=[pl.BlockSpec((None, tr, cols), lambda i, me_ref: (wl, i, 0))],
        out_specs=pl.BlockSpec((None, tr, cols), lambda i, me_ref: (me_ref[0], i, 0)))
    return pl.pallas_call(body, name=name, grid_spec=grid_spec, out_shape=jax.ShapeDtypeStruct((NDEV, rows, cols), dtype),
                          compiler_params=_params(("arbitrary",)))(me, a)


def _halves(block):
    rows = block.shape[0] // 2
    return block.at[pl.ds(0, rows)], block.at[pl.ds(rows, rows)]


def _around(x, y, c):
    return (x, y, 1 - c), (1 - x, y, c), (x, 1 - y, c), (1 - x, 1 - y, c)


def _gather_step1(zs, send, recv, pos):
    sibling, xn, yn, _ = _around(*pos)
    for j, z in enumerate(zs):
        mine = z.at[_index(pos)]
        for k, peer in enumerate((sibling, xn, yn)):
            _remote(mine, mine, send.at[3 * j + k], recv.at[3 * j + k], peer).start()


def _gather_step2(zs, recv1, send, recv, pos):
    sibling, xn, yn, _ = _around(*pos)
    for j, z in enumerate(zs):
        xb, yb = z.at[_index(xn)], z.at[_index(yn)]
        _remote(xb, xb, send.at[4 * j], recv1.at[3 * j + 1], pos).wait_recv()
        _remote(yb, yb, send.at[4 * j], recv1.at[3 * j + 2], pos).wait_recv()
        _remote(xb, xb, send.at[4 * j], recv.at[4 * j], sibling).start()
        _remote(yb, yb, send.at[4 * j + 1], recv.at[4 * j + 1], sibling).start()
        first, second = _halves(xb)[0], _halves(yb)[1]
        _remote(first, first, send.at[4 * j + 2], recv.at[4 * j + 2], yn).start()
        _remote(second, second, send.at[4 * j + 3], recv.at[4 * j + 3], xn).start()


def _gather_step3(zs, recv2, send, recv, pos):
    sibling, _, _, diagonal = _around(*pos)
    for j, z in enumerate(zs):
        db = z.at[_index(diagonal)]
        first, second = _halves(db)
        _remote(first, first, send.at[j], recv2.at[4 * j + 2], pos).wait_recv()
        _remote(second, second, send.at[j], recv2.at[4 * j + 3], pos).wait_recv()
        _remote(db, db, send.at[j], recv.at[j], sibling).start()


def _gather_step4(zs, send1, recv1, send2, recv2, send3, recv3, pos):
    x, y, c = pos
    sibling = (x, y, 1 - c)
    _, sx, sy, sd = _around(*sibling)
    for j, z in enumerate(zs):
        for owner, send, recv, k in ((sibling, send1, recv1, 3 * j), (sx, send2, recv2, 4 * j), (sy, send2, recv2, 4 * j + 1),
                                     (sd, send3, recv3, j)):
            block = z.at[_index(owner)]
            _remote(block, block, send.at[k], recv.at[k], pos).wait_recv()
    for j, z in enumerate(zs):
        block = z.at[0]
        half = _halves(block)[0]
        for ref, send, recv, k in ([(block, send1, recv1, 3 * j + k) for k in range(3)]
                                   + [(block, send2, recv2, 4 * j), (block, send2, recv2, 4 * j + 1),
                                      (half, send2, recv2, 4 * j + 2), (half, send2, recv2, 4 * j + 3), (block, send3, recv3, j)]):
            _remote(ref, ref, send.at[k], recv.at[k], pos).wait_send()


def _flight_call(step, name, zones, sems_in, nsems_out, after, carried):
    n, m, k = len(zones), len(carried), len(sems_in)

    def body(*refs):
        zs = refs[:n]
        given = refs[n + m:n + m + k]
        made = refs[n + m + k + len(after):n + m + k + len(after) + (2 if nsems_out else 0)]
        step(zs, *given, *made, _position())

    sem_out = (pltpu.SemaphoreType.DMA((nsems_out,)),) * 2 if nsems_out else ()
    outs = pl.pallas_call(
        body, name=name,
        out_shape=(*sem_out, *[pltpu.HBM(z.shape, z.dtype) for z in zones], *[jax.ShapeDtypeStruct(a.shape, a.dtype) for a in carried]),
        in_specs=[IN_HBM] * n + [ANY] * m + [SEMAPHORES] * k + [ANY] * len(after),
        out_specs=(*[SEMAPHORES] * len(sem_out), *[IN_HBM] * n, *[ANY] * m),
        input_output_aliases={j: len(sem_out) + j for j in range(n + m)},
        compiler_params=pltpu.CompilerParams(has_side_effects=IN_FLIGHT),
    )(*[_in_hbm(z) for z in zones], *carried, *sems_in, *after)
    sems = list(outs[:len(sem_out)])
    return sems, list(outs[len(sem_out):len(sem_out) + n]), list(outs[len(sem_out) + n:])


def _gather_start(zones, after, carried, name):
    (send1, recv1), zones, carried = _flight_call(_gather_step1, name, zones, [], 3 * len(zones), after, carried)
    return {"s1": send1, "r1": recv1, "zones": zones}, carried


def _gather_mid(flight, after, carried, name):
    step = lambda zs, recv1, send, recv, pos: _gather_step2(zs, recv1, send, recv, pos)
    (send2, recv2), zones, carried = _flight_call(step, name, flight["zones"], [flight["r1"]], 4 * len(flight["zones"]), after, carried)
    return {**flight, "s2": send2, "r2": recv2, "zones": zones}, carried


def _gather_late(flight, after, name):
    step = lambda zs, recv2, send, recv, pos: _gather_step3(zs, recv2, send, recv, pos)
    (send3, recv3), zones, _ = _flight_call(step, name, flight["zones"], [flight["r2"]], len(flight["zones"]), after, [])
    return {**flight, "s3": send3, "r3": recv3, "zones": zones}


def _gather_end(flight, after, name):
    sems = [flight[k] for k in ("s1", "r1", "s2", "r2", "s3", "r3")]
    _, zones, _ = _flight_call(_gather_step4, name, flight["zones"], sems, 0, after, [])
    return zones


def _scatter_start(parts, name):
    n = len(parts)
    lands = [_in_hbm(lax.empty((3,) + p.shape[1:], p.dtype)) for p in parts]

    def body(*refs):
        ins, zones = refs[:n], refs[n:2 * n]
        send_sems, recv_sems = refs[2 * n:2 * n + 2]
        x, y, c = _position()
        for j in range(n):
            for q, (px, py) in enumerate(_chips(x, y)):
                _remote(ins[j].at[2 * px + py], zones[j].at[q], send_sems.at[3 * j + q], recv_sems.at[3 * j + q],
                        (px, py, c)).start()

    outs = pl.pallas_call(
        body, name=name,
        out_shape=(pltpu.SemaphoreType.DMA((3 * n,)), pltpu.SemaphoreType.DMA((3 * n,)),
                   *[pltpu.HBM(p.shape, p.dtype) for p in parts], *[pltpu.HBM(z.shape, z.dtype) for z in lands]),
        in_specs=[IN_HBM] * (2 * n), out_specs=(SEMAPHORES, SEMAPHORES, *[IN_HBM] * (2 * n)),
        input_output_aliases={j: 2 + j for j in range(2 * n)},
        compiler_params=pltpu.CompilerParams(has_side_effects=IN_FLIGHT),
    )(*[_in_hbm(p) for p in parts], *lands)
    return outs[0], outs[1], list(outs[2:2 + n]), list(outs[2 + n:])


def _scatter_end(parts, zones, send_sems, recv_sems, after, name):
    n = len(parts)

    def body(*refs):
        ins, zs = refs[:n], refs[n:2 * n]
        s, r = refs[2 * n:2 * n + 2]
        me = _position()
        for j in range(n):
            for q in range(3):
                copy = _remote(ins[j].at[0], zs[j].at[q], s.at[3 * j + q], r.at[3 * j + q], me)
                copy.wait_send()
                copy.wait_recv()

    outs = pl.pallas_call(
        body, name=name,
        out_shape=(*[pltpu.HBM(p.shape, p.dtype) for p in parts], *[pltpu.HBM(z.shape, z.dtype) for z in zones]),
        in_specs=[IN_HBM] * (2 * n) + [SEMAPHORES, SEMAPHORES, ANY], out_specs=tuple([IN_HBM] * (2 * n)),
        input_output_aliases={j: j for j in range(2 * n)},
        compiler_params=pltpu.CompilerParams(has_side_effects=IN_FLIGHT),
    )(*parts, *zones, send_sems, recv_sems, after)
    return list(outs[:n]), list(outs[n:])


def _ada_forward(c, ada_w, ada_b, small):
    ns = len(small)

    def body(c_ref, w_ref, b_ref, *rest):
        small_refs, (cact_ref, mod_ref), gathered = rest[:ns], rest[ns:ns + 2], rest[ns + 2:2 * ns + 2]
        gbuf, modrow, send_sems, recv_sems = rest[2 * ns + 2:]
        pos = _position()
        me = _index(pos)

        def to_all(*exchanged):
            copies = []
            for ref, row in exchanged:
                for k in range(1, NDEV):
                    peer, _ = _peer(pos, k)
                    copy = pltpu.make_async_remote_copy(
                        src_ref=ref.at[me], dst_ref=ref.at[me], send_sem=send_sems.at[row, k - 1],
                        recv_sem=recv_sems.at[row, k - 1], device_id=peer, device_id_type=pl.DeviceIdType.MESH)
                    copy.start()
                    copies.append(copy)
            for copy in copies:
                copy.wait()

        cact_ref[me] = _silu(c_ref[...])
        for j in range(ns):
            gathered[j][me] = small_refs[j][...]
        to_all((cact_ref, 0), *[(gathered[j], 2 + j) for j in range(ns)])
        rows = lax.broadcasted_iota(jnp.int32, (NDEV, D), 0)
        cact = jnp.zeros((NDEV, D), F32)
        for e in range(NDEV):
            cact = jnp.where(rows == e, cact_ref[e], cact)
        cact = cact.astype(BF16)
        for l in range(DEPTH):
            gbuf[me, l] = jnp.dot(cact, w_ref[l].astype(BF16), preferred_element_type=F32)
        to_all((gbuf, 1))
        mine = lax.broadcasted_iota(jnp.int32, (NDEV, ADA_NC), 0) == me
        for l in range(DEPTH):
            for d in range(NDEV):
                modrow[:, d * ADA_NC:(d + 1) * ADA_NC] = jnp.sum(jnp.where(mine, gbuf[d, l], 0.0), axis=0, keepdims=True)
            full = modrow[...] + b_ref[l:l + 1, :]
            for w in range(3):
                mod_ref[l, w] = full[:, w * D:(w + 1) * D]

    outs = pl.pallas_call(
        body, name="ada_forward",
        out_shape=[jax.ShapeDtypeStruct((NDEV, 1, D), F32), jax.ShapeDtypeStruct((DEPTH, 3, 1, D), F32)]
        + [jax.ShapeDtypeStruct((NDEV,) + a.shape, a.dtype) for a in small],
        in_specs=[VMEM_FULL] * (3 + ns), out_specs=[VMEM_FULL] * (2 + ns),
        scratch_shapes=[pltpu.VMEM((NDEV, DEPTH, NDEV, ADA_NC), F32), pltpu.VMEM((1, 3 * D), F32),
                        pltpu.SemaphoreType.DMA((2 + ns, NDEV - 1)), pltpu.SemaphoreType.DMA((2 + ns, NDEV - 1))],
        compiler_params=_params(),
    )(c, ada_w, ada_b, *small)
    return outs[0], outs[1], list(outs[2:])


def _mod_spec(layer, which, ngrid):
    index = {1: lambda i: (layer, which, 0, 0), 2: lambda i, j: (layer, which, 0, 0)}[ngrid]
    return pl.BlockSpec((None, None, 1, D), index)


W_BLOCKS = 4


def _norm_proj(x, mod, norm_g3, wg, layer, name):
    nb = wg.shape[-1]
    tm = 1024
    wb = W_BLOCKS

    def body(x_ref, g_ref, shift_ref, scale_ref, w_ref, ht_ref, p_ref, h_ref):
        @pl.when(pl.program_id(1) == 0)
        def _():
            xv = x_ref[...]
            r = lax.rsqrt(jnp.mean(xv * xv, axis=-1, keepdims=True) + EPS)
            hn = xv * r * g_ref[...]
            h = hn * (1.0 + scale_ref[...]) + shift_ref[...]
            h_ref[...] = h.astype(BF16)
            ht_ref[...] = h.T.astype(BF16)

        hv = h_ref[...]
        for b in range(wb):
            p_ref[:, b * nb:(b + 1) * nb] = jnp.dot(hv, w_ref[b], preferred_element_type=F32).astype(BF16)

    return pl.pallas_call(
        body, name=name, grid=(S // tm, NDEV // wb),
        out_shape=[jax.ShapeDtypeStruct((D, S), BF16), jax.ShapeDtypeStruct((S, NDEV * nb), BF16)],
        in_specs=[pl.BlockSpec((tm, D), lambda i, d: (i, 0)),
                  pl.BlockSpec((None, 1, D), lambda i, d: (layer, 0, 0)),
                  _mod_spec(layer, 0, 2), _mod_spec(layer, 1, 2),
                  pl.BlockSpec((wb, D, nb), lambda i, d: (d, 0, 0))],
        out_specs=[pl.BlockSpec((D, tm), lambda i, d: (0, i)), pl.BlockSpec((tm, wb * nb), lambda i, d: (i, d))],
        scratch_shapes=[pltpu.VMEM((tm, D), BF16)],
        compiler_params=_params(("arbitrary", "arbitrary")),
    )(x, norm_g3, mod, mod, wg)


def _out_proj(ycat, w_out, x, mod, layer, name, final=None):
    tm = 512
    e = w_out.shape[0]

    def body(y_ref, w_ref, x_ref, gate_ref, *rest):
        acc = jnp.dot(y_ref[...], w_ref[...], preferred_element_type=F32)
        xv = x_ref[...] + gate_ref[...] * acc
        if final is None:
            xn_ref, o_ref = rest
            o_ref[...] = acc.astype(BF16)
            xn_ref[...] = xv
            return
        t_ref, g_ref, dx_ref, o_ref, loss_ref, dg_ref = rest
        o_ref[...] = acc.astype(BF16)

        @pl.when(pl.program_id(0) == 0)
        def _():
            loss_ref[...] = jnp.zeros_like(loss_ref)
            dg_ref[...] = jnp.zeros_like(dg_ref)

        g = g_ref[...]
        r = lax.rsqrt(jnp.mean(xv * xv, axis=-1, keepdims=True) + EPS)
        xn = xv * r
        err = xn * g - t_ref[...]
        loss_ref[...] += 0.5 * jnp.sum(jnp.mean(err * err, axis=-1, keepdims=True), axis=0, keepdims=True)
        dy = err * (1.0 / D)
        dg_ref[...] += jnp.sum(dy * xn, axis=0, keepdims=True)
        u = dy * g
        dx_ref[...] = r * (u - xn * jnp.mean(xn * u, axis=-1, keepdims=True))

    tile = pl.BlockSpec((tm, D), lambda i: (i, 0))
    row = pl.BlockSpec((1, D), lambda i: (0, 0))
    out_shape = [jax.ShapeDtypeStruct((S, D), F32), jax.ShapeDtypeStruct((S, D), BF16)]
    in_specs = [pl.BlockSpec((tm, e), lambda i: (i, 0)), pl.BlockSpec((e, D), lambda i: (0, 0)), tile, _mod_spec(layer, 2, 1)]
    out_specs, operands = [tile, tile], [ycat, w_out, x, mod]
    if final is not None:
        out_shape += [jax.ShapeDtypeStruct((1, LANE), F32), jax.ShapeDtypeStruct((1, D), F32)]
        in_specs += [tile, row]
        out_specs += [pl.BlockSpec((1, LANE), lambda i: (0, 0)), row]
        operands += list(final)
    return pl.pallas_call(
        body, name=name, grid=(S // tm,), out_shape=out_shape, in_specs=in_specs, out_specs=out_specs,
        compiler_params=_params(("arbitrary",)),
    )(*operands)


def _out_bwd(dx, out, ycat, w_out, mod, layer, carried, name):
    tm = 512
    nsteps = S // tm
    e = ycat.shape[1]
    rb = e // NDEV
    nc = len(carried)

    def body(dx_ref, o_ref, y_ref, w_ref, gate_ref, *rest):
        dy_ref, gw_ref, dgate_ref = rest[nc:nc + 3]
        acc = rest[-1]
        step = pl.program_id(0)

        @pl.when(step == 0)
        def _():
            dgate_ref[...] = jnp.zeros_like(dgate_ref)
            acc[...] = jnp.zeros_like(acc)

        dxv = dx_ref[...]
        d_out = (gate_ref[...] * dxv).astype(BF16)
        dgate_ref[...] += jnp.sum(dxv * o_ref[...].astype(F32), axis=0, keepdims=True)
        dy_ref[...] = lax.dot_general(d_out, w_ref[...], (((1,), (1,)), ((), ())), preferred_element_type=F32).astype(BF16)
        acc[...] += lax.dot_general(y_ref[...], d_out, (((0,), (0,)), ((), ())), preferred_element_type=F32)

        @pl.when(step == nsteps - 1)
        def _():
            for d in range(NDEV):
                gw_ref[d % 2, d // 2] = acc[d * rb:(d + 1) * rb, :].astype(BF16)

    tile = pl.BlockSpec((tm, D), lambda i: (i, 0))
    wide = pl.BlockSpec((tm, e), lambda i: (i, 0))
    outs = pl.pallas_call(
        body, name=name, grid=(nsteps,),
        out_shape=[jax.ShapeDtypeStruct((S, e), BF16), jax.ShapeDtypeStruct((2, NDEV // 2, rb, D), BF16),
                   jax.ShapeDtypeStruct((1, D), F32)] + [jax.ShapeDtypeStruct(a.shape, a.dtype) for a in carried],
        in_specs=[tile, tile, wide, pl.BlockSpec((e, D), lambda i: (0, 0)), _mod_spec(layer, 2, 1)] + [ANY] * nc,
        out_specs=[wide, pl.BlockSpec((2, NDEV // 2, rb, D), lambda i: (0, 0, 0, 0)), pl.BlockSpec((1, D), lambda i: (0, 0))]
        + [ANY] * nc,
        scratch_shapes=[pltpu.VMEM((e, D), F32)],
        input_output_aliases={5 + k: 3 + k for k in range(nc)},
        compiler_params=_params(("arbitrary",)),
    )(dx, out, ycat, w_out, mod, *carried)
    return outs[0], outs[1], outs[2], list(outs[3:])


def _weight_grad(h_t, d_proj, name):
    nb = d_proj.shape[1] // NDEV

    def body(ht_ref, dp_ref, o_ref):
        o_ref[...] = jnp.dot(ht_ref[...], dp_ref[...], preferred_element_type=F32).astype(BF16)

    return pl.pallas_call(
        body, name=name, grid=(NDEV,), out_shape=jax.ShapeDtypeStruct((2, NDEV // 2, D, nb), BF16),
        in_specs=[pl.BlockSpec((D, S), lambda d: (0, 0)), pl.BlockSpec((S, nb), lambda d: (0, d))],
        out_specs=pl.BlockSpec((None, None, D, nb), lambda d: (d % 2, d // 2, 0, 0)),
        compiler_params=_params(("arbitrary",)),
    )(h_t, d_proj)


def _dh_norm_bwd(d_proj, wg, x, dx, mod, norm_g3, layer, carried, name):
    nb = wg.shape[-1]
    tm = 512
    wb = W_BLOCKS
    rc = 128

    def body(dp_ref, w_ref, x_ref, dx_ref, g_ref, scale_ref, carried_ref,
             dxi_ref, dshift_ref, dscale_ref, dg_ref, carried_out, acc):
        i, d = pl.program_id(0), pl.program_id(1)
        nt = (((1,), (1,)), ((), ()))
        part = lax.dot_general(dp_ref[:, :nb], w_ref[0], nt, preferred_element_type=F32)
        for b in range(1, wb):
            part += lax.dot_general(dp_ref[:, b * nb:(b + 1) * nb], w_ref[b], nt, preferred_element_type=F32)

        @pl.when(d == 0)
        def _():
            acc[...] = jnp.zeros_like(acc)

        acc[...] += part

        @pl.when(jnp.logical_and(i == 0, d == 0))
        def _():
            dshift_ref[...] = jnp.zeros_like(dshift_ref)
            dscale_ref[...] = jnp.zeros_like(dscale_ref)
            dg_ref[...] = jnp.zeros_like(dg_ref)

        @pl.when(d == NDEV // wb - 1)
        def _():
            g = g_ref[...]
            scale1 = 1.0 + scale_ref[...]

            def chunk(k, sums):
                rows = pl.ds(pl.multiple_of(k * rc, rc), rc)
                xv, dhv = x_ref[rows, :], acc[rows, :]
                r = lax.rsqrt(jnp.mean(xv * xv, axis=-1, keepdims=True) + EPS)
                xn = xv * r
                dhn = dhv * scale1
                u = dhn * g
                dxi_ref[rows, :] = dx_ref[rows, :] + r * (u - xn * jnp.mean(xn * u, axis=-1, keepdims=True))
                return (sums[0] + jnp.sum(dhv, axis=0, keepdims=True),
                        sums[1] + jnp.sum(dhv * (xn * g), axis=0, keepdims=True),
                        sums[2] + jnp.sum(dhn * xn, axis=0, keepdims=True))

            zero = jnp.zeros((1, D), F32)
            sums = lax.fori_loop(0, tm // rc, chunk, (zero, zero, zero))
            dshift_ref[...] += sums[0]
            dscale_ref[...] += sums[1]
            dg_ref[...] += sums[2]

    tile = pl.BlockSpec((tm, D), lambda i, d: (i, 0))
    row = pl.BlockSpec((1, D), lambda i, d: (0, 0))
    return pl.pallas_call(
        body, name=name, grid=(S // tm, NDEV // wb),
        out_shape=[jax.ShapeDtypeStruct((S, D), F32)] + [jax.ShapeDtypeStruct((1, D), F32)] * 3
        + [jax.ShapeDtypeStruct(carried.shape, carried.dtype)],
        in_specs=[pl.BlockSpec((tm, wb * nb), lambda i, d: (i, d)), pl.BlockSpec((wb, D, nb), lambda i, d: (d, 0, 0)),
                  tile, tile, pl.BlockSpec((None, 1, D), lambda i, d: (layer, 0, 0)), _mod_spec(layer, 1, 2), ANY],
        out_specs=[tile, row, row, row, ANY], scratch_shapes=[pltpu.VMEM((tm, D), F32)],
        input_output_aliases={6: 4}, compiler_params=_params(("arbitrary", "arbitrary")),
    )(d_proj, wg, x, dx, norm_g3, mod, carried)


TS = 256
NCH = TS // CHUNK
HALO_BLOCKS = TS // HALO


def _halo_before(width, col_block):
    return pl.BlockSpec((HALO, width), lambda i: (jnp.maximum(i * HALO_BLOCKS - 1, 0), col_block))


def _halo_after(width, col_block):
    return pl.BlockSpec((HALO, width), lambda i: (jnp.minimum((i + 1) * HALO_BLOCKS, S // HALO - 1), col_block))


def _shift_down(ext, k):
    return pltpu.roll(ext, k, 0)[HALO:]


def _shift_up(ext, k):
    return pltpu.roll(ext, ext.shape[0] - k, 0)[:ext.shape[0] - HALO]


def _layer_norm_head(v, lg, lb):
    mu = jnp.mean(v, axis=-1, keepdims=True)
    vc = v - mu
    rstd = lax.rsqrt(jnp.mean(vc * vc, axis=-1, keepdims=True) + EPS)
    vhat = vc * rstd
    return vhat, rstd, vhat * lg + lb


def _causal_mask():
    return lax.broadcasted_iota(jnp.int32, (CHUNK, CHUNK), 0) >= lax.broadcasted_iota(jnp.int32, (CHUNK, CHUNK), 1)


def _region_per_head(head, step):
    for j in range(E_A // HEAD):
        pl.when(step >= 0)(functools.partial(head, j))


def _even_mix_fwd(proj, convw, ln_g3, ln_b3, sgu_w, sgu_bcol, wl, after, name):
    def body(pj_ref, hh_ref, hc_ref, cw_ref, lg_ref, lb_ref, sw_ref, sb_ref, *rest):
        y_ref = rest[-1]
        step = pl.program_id(0)
        live = (step > 0).astype(F32)
        causal = _causal_mask()

        def head(j):
            cols = slice(j * HEAD, (j + 1) * HEAD)
            w0, w1, w2 = cw_ref[0:1, cols], cw_ref[1:2, cols], cw_ref[2:3, cols]
            lg, lb = lg_ref[:, cols], lb_ref[:, cols]
            wm = jnp.where(causal, sw_ref[j], 0.0).astype(BF16)
            bias = sb_ref[j]

            def split(s, rows, cols=cols):
                return pj_ref[rows, s * E_A + cols.start:s * E_A + cols.stop].astype(F32)

            prev_tail = hc_ref[:, cols].astype(F32) * hh_ref[:, cols].astype(F32) * live
            for n in range(NCH):
                rows = slice(n * CHUNK, (n + 1) * CHUNK)
                p = split(2, rows) * split(0, rows)
                ext = jnp.concatenate([prev_tail, p], axis=0)
                prev_tail = p[CHUNK - HALO:]
                cv = w2 * p + w1 * _shift_down(ext, 1) + w0 * _shift_down(ext, 2)
                y_ref[rows, cols] = (split(1, rows) * cv * _silu(split(3, rows))).astype(BF16)
                _, _, vn = _layer_norm_head(split(5, rows), lg, lb)
                mixed = jnp.dot(wm, vn.astype(BF16), preferred_element_type=F32) + bias
                y_ref[rows, E_A + cols.start:E_A + cols.stop] = (split(4, rows) * mixed * _silu(split(6, rows))).astype(BF16)

        _region_per_head(head, step)

    const3 = lambda i: (wl, 0, 0)
    const4 = lambda i: (wl, 0, 0, 0)
    return pl.pallas_call(
        body, name=name, grid=(S // TS,), out_shape=jax.ShapeDtypeStruct((S, 2 * E_A), BF16),
        in_specs=[pl.BlockSpec((TS, 7 * E_A), lambda i: (i, 0)), _halo_before(E_A, 0), _halo_before(E_A, 2),
                  pl.BlockSpec((None, 3, E_A), const3), pl.BlockSpec((None, 1, E_A), const3),
                  pl.BlockSpec((None, 1, E_A), const3), pl.BlockSpec((None, NDEV, CHUNK, CHUNK), const4),
                  pl.BlockSpec((None, NDEV, CHUNK, 1), const4)] + [ANY] * len(after),
        out_specs=pl.BlockSpec((TS, 2 * E_A), lambda i: (i, 0)),
        compiler_params=_params(("arbitrary",)),
    )(proj, proj, proj, convw, ln_g3, ln_b3, sgu_w, sgu_bcol, *after)


def _even_mix_bwd(proj, d_ycat, convw, ln_g3, ln_b3, sgu_w, sgu_bcol, wl, name):
    nsteps = S // TS

    def body(pj_ref, hh_ref, hc_ref, hb_ref, hz_ref, dy_ref, hdy_ref, cw_ref, lg_ref, lb_ref, sw_ref, sb_ref,
             dp_ref, dcw_ref, dlg_ref, dlb_ref, dsw_ref, dsb_ref):
        step = pl.program_id(0)

        @pl.when(step == 0)
        def _():
            for ref in (dcw_ref, dlg_ref, dlb_ref, dsw_ref, dsb_ref):
                ref[...] = jnp.zeros_like(ref)

        live_before = (step > 0).astype(F32)
        live_after = (step < nsteps - 1).astype(F32)
        causal = _causal_mask()

        def head(j):
            cols = slice(j * HEAD, (j + 1) * HEAD)
            w0, w1, w2 = cw_ref[0:1, cols], cw_ref[1:2, cols], cw_ref[2:3, cols]
            lg, lb = lg_ref[:, cols], lb_ref[:, cols]
            wmf = jnp.where(causal, sw_ref[j], 0.0)
            wm, wmt = wmf.astype(BF16), wmf.T.astype(BF16)
            bias = sb_ref[j]

            def split(s, rows, cols=cols):
                return pj_ref[rows, s * E_A + cols.start:s * E_A + cols.stop].astype(F32)

            def put(s, rows, val, cols=cols):
                dp_ref[rows, s * E_A + cols.start:s * E_A + cols.stop] = val.astype(BF16)

            ps = [split(2, slice(n * CHUNK, (n + 1) * CHUNK)) * split(0, slice(n * CHUNK, (n + 1) * CHUNK)) for n in range(NCH)]
            next_head = (hdy_ref[:, cols].astype(F32) * hb_ref[:, cols].astype(F32) * _silu(hz_ref[:, cols].astype(F32))
                         * live_after)
            acc_w = [jnp.zeros((1, HEAD), F32) for _ in range(3)]
            for n in reversed(range(NCH)):
                rows = slice(n * CHUNK, (n + 1) * CHUNK)
                p = ps[n]
                tail = ps[n - 1][CHUNK - HALO:] if n > 0 else hc_ref[:, cols].astype(F32) * hh_ref[:, cols].astype(F32) * live_before
                ext = jnp.concatenate([tail, p], axis=0)
                p1, p2 = _shift_down(ext, 1), _shift_down(ext, 2)
                cv = w2 * p + w1 * p1 + w0 * p2
                a_b, a_z = split(1, rows), split(3, rows)
                sz, dsz = _silu_and_grad(a_z)
                dya = dy_ref[rows, cols].astype(F32)
                put(1, rows, dya * cv * sz)
                put(3, rows, dya * a_b * cv * dsz)
                gcv = dya * a_b * sz
                acc_w[0] += jnp.sum(gcv * p2, axis=0, keepdims=True)
                acc_w[1] += jnp.sum(gcv * p1, axis=0, keepdims=True)
                acc_w[2] += jnp.sum(gcv * p, axis=0, keepdims=True)
                gext = jnp.concatenate([gcv, next_head], axis=0)
                next_head = gcv[:HALO]
                dpv = w2 * gcv + w1 * _shift_up(gext, 1) + w0 * _shift_up(gext, 2)
                put(2, rows, dpv * split(0, rows))
                put(0, rows, dpv * split(2, rows))
            for k in range(3):
                dcw_ref[k:k + 1, cols] += acc_w[k]

            acc_lg, acc_lb = jnp.zeros((1, HEAD), F32), jnp.zeros((1, HEAD), F32)
            acc_sw, acc_sb = jnp.zeros((CHUNK, CHUNK), F32), jnp.zeros((CHUNK, 1), F32)
            for n in range(NCH):
                rows = slice(n * CHUNK, (n + 1) * CHUNK)
                u, z = split(4, rows), split(6, rows)
                vhat, rstd, vn = _layer_norm_head(split(5, rows), lg, lb)
                vn16 = vn.astype(BF16)
                mixed = jnp.dot(wm, vn16, preferred_element_type=F32) + bias
                sz, dsz = _silu_and_grad(z)
                dyb = dy_ref[rows, E_A + cols.start:E_A + cols.stop].astype(F32)
                put(4, rows, dyb * mixed * sz)
                put(6, rows, dyb * u * mixed * dsz)
                dmix = dyb * u * sz
                dmix16 = dmix.astype(BF16)
                acc_sb += jnp.sum(dmix, axis=1, keepdims=True)
                acc_sw += lax.dot_general(dmix16, vn16, (((1,), (1,)), ((), ())), preferred_element_type=F32)
                dvn = jnp.dot(wmt, dmix16, preferred_element_type=F32)
                acc_lg += jnp.sum(dvn * vhat, axis=0, keepdims=True)
                acc_lb += jnp.sum(dvn, axis=0, keepdims=True)
                dvh = dvn * lg
                put(5, rows, rstd * (dvh - jnp.mean(dvh, axis=-1, keepdims=True)
                                     - vhat * jnp.mean(dvh * vhat, axis=-1, keepdims=True)))
            dlg_ref[:, cols] += acc_lg
            dlb_ref[:, cols] += acc_lb
            dsw_ref[j] += jnp.where(causal, acc_sw, 0.0)
            dsb_ref[j] += acc_sb

        _region_per_head(head, step)

    const3 = lambda i: (wl, 0, 0)
    const4 = lambda i: (wl, 0, 0, 0)
    fixed2 = lambda i: (0, 0)
    fixed3 = lambda i: (0, 0, 0)
    return pl.pallas_call(
        body, name=name, grid=(nsteps,),
        out_shape=[jax.ShapeDtypeStruct((S, 7 * E_A), BF16), jax.ShapeDtypeStruct((3, E_A), F32),
                   jax.ShapeDtypeStruct((1, E_A), F32), jax.ShapeDtypeStruct((1, E_A), F32),
                   jax.ShapeDtypeStruct((NDEV, CHUNK, CHUNK), F32), jax.ShapeDtypeStruct((NDEV, CHUNK, 1), F32)],
        in_specs=[pl.BlockSpec((TS, 7 * E_A), lambda i: (i, 0)), _halo_before(E_A, 0), _halo_before(E_A, 2),
                  _halo_after(E_A, 1), _halo_after(E_A, 3),
                  pl.BlockSpec((TS, 2 * E_A), lambda i: (i, 0)), _halo_after(E_A, 0),
                  pl.BlockSpec((None, 3, E_A), const3), pl.BlockSpec((None, 1, E_A), const3),
                  pl.BlockSpec((None, 1, E_A), const3), pl.BlockSpec((None, NDEV, CHUNK, CHUNK), const4),
                  pl.BlockSpec((None, NDEV, CHUNK, 1), const4)],
        out_specs=[pl.BlockSpec((TS, 7 * E_A), lambda i: (i, 0)), pl.BlockSpec((3, E_A), fixed2),
                   pl.BlockSpec((1, E_A), fixed2), pl.BlockSpec((1, E_A), fixed2),
                   pl.BlockSpec((NDEV, CHUNK, CHUNK), fixed3), pl.BlockSpec((NDEV, CHUNK, 1), fixed3)],
        compiler_params=_params(("arbitrary",)),
    )(proj, proj, proj, proj, proj, d_ycat, d_ycat, convw, ln_g3, ln_b3, sgu_w, sgu_bcol)


def _window_count(step, n, win, ext_before):
    rows = CHUNK if ext_before else CHUNK + HALO
    t = step * TS + n * CHUNK + lax.broadcasted_iota(jnp.int32, (rows, 1), 0)
    return jnp.minimum(t + 1, win).astype(F32)


def _pool_weight(wp_ref, g):
    return jnp.concatenate([wp_ref[d, g] for d in range(NDEV)], axis=0)


def _pooled_chunk(p, tail, win, count):
    sums = jnp.concatenate([tail, p], axis=0)
    shift = 1
    while shift < win:
        sums = sums + pltpu.roll(sums, shift, 0)
        shift *= 2
    return sums[HALO:] / count - p


def _pool_mix_fwd(proj, wpool, pscale4, wl, after, name):
    e_c = 4 * G_C

    def body(pj_ref, hp_ref, wp_ref, ps_ref, *rest):
        y_ref, pooled_scr, yraw_scr = rest[-3:]
        step = pl.program_id(0)
        live = (step > 0).astype(F32)
        for g, win in enumerate(POOL_WINDOWS):
            for q in range(G_C // LANE):
                cols = slice(g * G_C + q * LANE, g * G_C + (q + 1) * LANE)
                tail = hp_ref[:, cols].astype(F32) * live
                for n in range(NCH):
                    rows = slice(n * CHUNK, (n + 1) * CHUNK)
                    p = pj_ref[rows, cols].astype(F32)
                    pooled_scr[rows, q * LANE:(q + 1) * LANE] = _pooled_chunk(
                        p, tail, win, _window_count(step, n, win, True)).astype(BF16)
                    tail = p[CHUNK - HALO:]
            yraw_scr[...] = jnp.dot(pooled_scr[...], _pool_weight(wp_ref, g), preferred_element_type=F32)
            for q in range(G_C // LANE):
                cols = slice(g * G_C + q * LANE, g * G_C + (q + 1) * LANE)
                for n in range(NCH):
                    rows = slice(n * CHUNK, (n + 1) * CHUNK)
                    z = pj_ref[rows, e_c + cols.start:e_c + cols.stop].astype(F32)
                    y_ref[rows, cols] = (yraw_scr[rows, q * LANE:(q + 1) * LANE] * ps_ref[:, cols] * _silu(z)).astype(BF16)

    return pl.pallas_call(
        body, name=name, grid=(S // TS,), out_shape=jax.ShapeDtypeStruct((S, e_c), BF16),
        in_specs=[pl.BlockSpec((TS, 2 * e_c), lambda i: (i, 0)), _halo_before(e_c, 0),
                  pl.BlockSpec((NDEV, 4, G_C // NDEV, G_C), lambda i: (0, 0, 0, 0)),
                  pl.BlockSpec((None, 1, e_c), lambda i: (wl, 0, 0))] + [ANY] * len(after),
        out_specs=pl.BlockSpec((TS, e_c), lambda i: (i, 0)),
        scratch_shapes=[pltpu.VMEM((TS, G_C), BF16), pltpu.VMEM((TS, G_C), F32)],
        compiler_params=_params(("arbitrary",)),
    )(proj, proj, wpool, pscale4, *after)


def _pool_mix_bwd(proj, d_ycat, wpool, pscale4, wl, name):
    e_c = 4 * G_C
    nsteps = S // TS
    rb = G_C // NDEV

    def body(pj_ref, hp_ref, hz_ref, dy_ref, hdy_ref, wp_ref, ps_ref,
             dp_ref, dps_ref, dwp_ref, pooled_scr, yraw_scr, dyraw_scr, dpool_scr, acc_w):
        step = pl.program_id(0)

        @pl.when(step == 0)
        def _():
            dps_ref[...] = jnp.zeros_like(dps_ref)
            acc_w[...] = jnp.zeros_like(acc_w)

        live_before = (step > 0).astype(F32)
        live_after = (step < nsteps - 1).astype(F32)
        for g, win in enumerate(POOL_WINDOWS):
            weight = _pool_weight(wp_ref, g)
            for q in range(G_C // LANE):
                cols = slice(g * G_C + q * LANE, g * G_C + (q + 1) * LANE)
                tail = hp_ref[:, cols].astype(F32) * live_before
                for n in range(NCH):
                    rows = slice(n * CHUNK, (n + 1) * CHUNK)
                    p = pj_ref[rows, cols].astype(F32)
                    pooled_scr[rows, q * LANE:(q + 1) * LANE] = _pooled_chunk(
                        p, tail, win, _window_count(step, n, win, True)).astype(BF16)
                    tail = p[CHUNK - HALO:]
            yraw_scr[...] = jnp.dot(pooled_scr[...], weight, preferred_element_type=F32)
            for q in range(G_C // LANE):
                cols = slice(g * G_C + q * LANE, g * G_C + (q + 1) * LANE)
                local = slice(q * LANE, (q + 1) * LANE)
                scale = ps_ref[:, cols]
                acc_ps = jnp.zeros((1, LANE), F32)
                for n in range(NCH):
                    rows = slice(n * CHUNK, (n + 1) * CHUNK)
                    sz, dsz = _silu_and_grad(pj_ref[rows, e_c + cols.start:e_c + cols.stop].astype(F32))
                    dyv = dy_ref[rows, cols].astype(F32)
                    yraw = yraw_scr[rows, local]
                    dyraw_scr[rows, local] = (dyv * scale * sz).astype(BF16)
                    acc_ps += jnp.sum(dyv * yraw * sz, axis=0, keepdims=True)
                    dp_ref[rows, e_c + cols.start:e_c + cols.stop] = (dyv * yraw * scale * dsz).astype(BF16)
                dps_ref[:, cols] += acc_ps
                dyraw_scr[TS:, local] = (hdy_ref[:, cols].astype(F32) * scale * _silu(hz_ref[:, cols].astype(F32))
                                         * live_after).astype(BF16)
            dpool_scr[...] = lax.dot_general(dyraw_scr[...], weight, (((1,), (1,)), ((), ())), preferred_element_type=F32)
            acc_w[g] += lax.dot_general(pooled_scr[...], dyraw_scr[:TS, :], (((0,), (0,)), ((), ())),
                                        preferred_element_type=F32)
            for q in range(G_C // LANE):
                cols = slice(g * G_C + q * LANE, g * G_C + (q + 1) * LANE)
                local = slice(q * LANE, (q + 1) * LANE)
                for n in range(NCH):
                    rows = slice(n * CHUNK, (n + 1) * CHUNK)
                    ext = dpool_scr[n * CHUNK:(n + 1) * CHUNK + HALO, local]
                    sums = ext / _window_count(step, n, win, False)
                    shift = 1
                    while shift < win:
                        sums = sums + pltpu.roll(sums, CHUNK + HALO - shift, 0)
                        shift *= 2
                    dp_ref[rows, cols] = (sums[:CHUNK] - ext[:CHUNK]).astype(BF16)

        @pl.when(step == nsteps - 1)
        def _():
            for g in range(4):
                for d in range(NDEV):
                    dwp_ref[d % 2, d // 2, g] = acc_w[g, d * rb:(d + 1) * rb, :].astype(BF16)

    in_specs = [pl.BlockSpec((TS, 2 * e_c), lambda i: (i, 0)), _halo_before(e_c, 0), _halo_after(e_c, 1),
                pl.BlockSpec((TS, e_c), lambda i: (i, 0)), _halo_after(e_c, 0),
                pl.BlockSpec((NDEV, 4, rb, G_C), lambda i: (0, 0, 0, 0)),
                pl.BlockSpec((None, 1, e_c), lambda i: (wl, 0, 0))]
    args = [proj, proj, proj, d_ycat, d_ycat, wpool, pscale4]
    return pl.pallas_call(
        body, name=name, grid=(nsteps,),
        out_shape=[jax.ShapeDtypeStruct((S, 2 * e_c), BF16), jax.ShapeDtypeStruct((1, e_c), F32),
                   jax.ShapeDtypeStruct((2, NDEV // 2) + wpool.shape[1:], BF16)],
        in_specs=in_specs,
        out_specs=[pl.BlockSpec((TS, 2 * e_c), lambda i: (i, 0)), pl.BlockSpec((1, e_c), lambda i: (0, 0)),
                   pl.BlockSpec((2, NDEV // 2, 4, rb, G_C), lambda i: (0, 0, 0, 0, 0))],
        scratch_shapes=[pltpu.VMEM((TS, G_C), BF16), pltpu.VMEM((TS, G_C), F32), pltpu.VMEM((TS + HALO, G_C), BF16),
                        pltpu.VMEM((TS + HALO, G_C), F32), pltpu.VMEM((4, G_C, G_C), F32)],
        compiler_params=_params(("arbitrary",)),
    )(*args)


def _adamw(w, g, m, v):
    m = ADAM_B1 * m + (1.0 - ADAM_B1) * g
    v = ADAM_B2 * v + (1.0 - ADAM_B2) * jnp.square(g)
    m_hat = m / (1.0 - ADAM_B1 ** ADAM_STEP)
    v_hat = v / (1.0 - ADAM_B2 ** ADAM_STEP)
    delta = -ADAM_LR * (m_hat / (jnp.sqrt(v_hat) + ADAM_EPS) + ADAM_WD * w)
    return delta, m, v


def _adam_sharded(w, m, v, chip_parts, landed, my_chip, carried, name, first=0, into=()):
    _, nr, ncol = w.shape
    nl = len(chip_parts)
    tr = 256
    steps = nr // tr
    nc, ni = len(carried), len(into)

    def body(chip_ref, w_ref, m_ref, v_ref, *rest):
        parts, zones = rest[:nl], rest[nl:2 * nl]
        g_ref, d_ref, nm_ref, nv_ref = rest[2 * nl + nc + ni:2 * nl + nc + ni + 4]
        layer = pl.program_id(0)
        g = jnp.zeros((tr, ncol), F32)
        for l in range(nl):
            gl = parts[l][...].astype(F32)
            for q in range(3):
                gl = gl + zones[l][q].astype(F32)
            g = jnp.where(layer == l, gl, g)
        g_ref[...] = g
        d_ref[...], nm_ref[...], nv_ref[...] = _adamw(w_ref[...], g, m_ref[...], v_ref[...])

    def rows_of(l):
        return lambda layer, i, chip_ref: jnp.where(layer == l, i, jnp.where(layer < l, 0, steps - 1))

    spec = pl.BlockSpec((None, tr, ncol), lambda layer, i, chip_ref: (first + layer, i, 0))
    part_specs = [pl.BlockSpec((None, tr, ncol), lambda layer, i, chip_ref, l=l: (chip_ref[0], rows_of(l)(layer, i, chip_ref), 0))
                  for l in range(nl)]
    zone_specs = [pl.BlockSpec((3, tr, ncol), lambda layer, i, chip_ref, l=l: (0, rows_of(l)(layer, i, chip_ref), 0))
                  for l in range(nl)]
    grid_spec = pltpu.PrefetchScalarGridSpec(
        num_scalar_prefetch=1, grid=(nl, steps),
        in_specs=[spec, spec, spec] + part_specs + zone_specs + [ANY] * (nc + ni), out_specs=[spec] * 4 + [ANY] * nc)
    aliases = {4 + 2 * nl + k: 4 + k for k in range(nc)}
    aliases.update({4 + 2 * nl + nc + k: k for k in range(ni)})
    return pl.pallas_call(
        body, name=name, grid_spec=grid_spec,
        out_shape=[jax.ShapeDtypeStruct(w.shape, F32)] * 4 + [jax.ShapeDtypeStruct(a.shape, a.dtype) for a in carried],
        input_output_aliases=aliases, compiler_params=_params(("arbitrary", "arbitrary")),
    )(my_chip, w, m, v, *chip_parts, *landed, *carried, *into)


def _adam_small(params, name):
    n = len(params)

    def body(*refs):
        ins, outs = refs[:4 * n], refs[4 * n:]
        for k in range(n):
            w_ref, g_ref, m_ref, v_ref = ins[4 * k:4 * k + 4]
            outs[3 * k][...], outs[3 * k + 1][...], outs[3 * k + 2][...] = _adamw(w_ref[...], g_ref[...], m_ref[...], v_ref[...])

    outs = pl.pallas_call(body, name=name, out_shape=[jax.ShapeDtypeStruct(p[0].shape, F32) for p in params for _ in range(3)],
                          in_specs=[VMEM_FULL] * (4 * n), out_specs=[VMEM_FULL] * (3 * n),
                          compiler_params=_params())(*[a for p in params for a in p])
    return [list(outs[3 * k:3 * k + 3]) for k in range(n)]


def _sum_devices(gathered, name):
    _, nr, ncol = gathered.shape

    def body(g_ref, o_ref):
        acc = g_ref[0].astype(F32)
        for s in range(1, NDEV):
            acc = acc + g_ref[s].astype(F32)
        o_ref[...] = acc

    return pl.pallas_call(body, name=name, grid=(1,), out_shape=jax.ShapeDtypeStruct((nr, ncol), F32),
                          in_specs=[pl.BlockSpec((NDEV, nr, ncol), lambda i: (0, 0, 0))],
                          out_specs=pl.BlockSpec((nr, ncol), lambda i: (0, 0)),
                          compiler_params=_params(("arbitrary",)))(gathered)


def _ada_weight_adam(cact_t, dmod_mine, w, m, v):
    def body(ct_ref, dm_ref, w_ref, m_ref, v_ref, g_ref, d_ref, nm_ref, nv_ref):
        ct, dm = ct_ref[...], dm_ref[...]
        g = ct[:, 0:1] * dm[0:1, :]
        for e in range(1, NDEV):
            g = g + ct[:, e:e + 1] * dm[e:e + 1, :]
        g_ref[...] = g
        d_ref[...], nm_ref[...], nv_ref[...] = _adamw(w_ref[...], g, m_ref[...], v_ref[...])

    spec = pl.BlockSpec((None, D, ADA_NC), lambda l: (l, 0, 0))
    return pl.pallas_call(
        body, name="ada_weight_adam", grid=(DEPTH,), out_shape=[jax.ShapeDtypeStruct(w.shape, F32)] * 4,
        in_specs=[pl.BlockSpec((D, NDEV), lambda l: (0, 0)), pl.BlockSpec((None, NDEV, ADA_NC), lambda l: (l, 0, 0)),
                  spec, spec, spec],
        out_specs=[spec] * 4, compiler_params=_params(("arbitrary",)),
    )(cact_t, dmod_mine, w, m, v)


def _pad_rows(a, rows):
    a = a.reshape(-1, D)
    return jnp.pad(a, ((0, rows - a.shape[0]), (0, 0)))


def kernel(x, c, norm_g, ada_w, ada_b, ab_w_in, ab_conv_w, ab_ln_g, ab_ln_b, ab_sgu_w, ab_sgu_b, ab_w_out, c_w_in, c_pool_w, c_pool_scale, c_w_out, final_g, loss_target, m_norm_g, m_ada_w, m_ada_b, m_ab_w_in, m_ab_conv_w, m_ab_ln_g, m_ab_ln_b, m_ab_sgu_w, m_ab_sgu_b, m_ab_w_out, m_c_w_in, m_c_pool_w, m_c_pool_scale, m_c_w_out, m_final_g, v_norm_g, v_ada_w, v_ada_b, v_ab_w_in, v_ab_conv_w, v_ab_ln_g, v_ab_ln_b, v_ab_sgu_w, v_ab_sgu_b, v_ab_w_out, v_c_w_in, v_c_pool_w, v_c_pool_scale, v_c_w_out, v_final_g):
    x_pos, y_pos, c_pos = _position()
    me = _index((x_pos, y_pos, c_pos))
    core = c_pos.astype(jnp.int32).reshape(1)
    my_chip = (2 * x_pos + y_pos).astype(jnp.int32).reshape(1)
    me1 = me.astype(jnp.int32).reshape(1)
    x0 = x.reshape(S, D)
    target = loss_target.reshape(S, D)
    norm_g3 = norm_g.reshape(DEPTH, 1, D)
    ln_g3, ln_b3 = ab_ln_g.reshape(2, 1, E_A), ab_ln_b.reshape(2, 1, E_A)
    sgu_bcol = ab_sgu_b.reshape(2, NDEV, CHUNK, 1)
    rb = G_C // NDEV
    pool_w3, m_pool_w3, v_pool_w3 = (a.reshape(2, 4 * rb, G_C) for a in (c_pool_w, m_c_pool_w, v_c_pool_w))

    cact_all, mod, (convw_all, pscale_all) = _ada_forward(c, ada_w, ada_b, [ab_conv_w, c_pool_scale])
    convw = jnp.transpose(convw_all, (1, 2, 0, 3)).reshape(2, 3, E_A)
    pscale4 = jnp.transpose(pscale_all, (1, 0, 2)).reshape(2, 1, 4 * G_C)
    zones = []
    for layer in range(DEPTH):
        wl = layer // 2
        if layer % 2 == 0:
            zones.append([_to_zone(ab_w_in, wl, me1, BF16, f"cast_w_in_{layer}"), _to_zone(ab_w_out, wl, me1, BF16, f"cast_w_out_{layer}")])
        else:
            zones.append([_to_zone(c_w_in, wl, me1, BF16, f"cast_w_in_{layer}"), _to_zone(c_w_out, wl, me1, BF16, f"cast_w_out_{layer}"),
                          _to_zone(pool_w3, wl, me1, BF16, f"cast_pool_w_{layer}")])

    def gathered(flight, after, layer):
        wg = _gather_end(flight, [after], f"gather_end_{layer}")
        return [wg[0], wg[1].reshape(-1, D)] + [w.reshape(NDEV, 4, rb, G_C) for w in wg[2:]]

    flight, (mod,) = _gather_start(zones[0], [convw_all], [mod], "gather_start_0")
    flight, (mod,) = _gather_mid(flight, [z for zs in zones[1:] for z in zs], [mod], "gather_mid_0")
    next_flight, (mod,) = _gather_start(zones[1], [], [mod], "gather_start_1")
    flight = _gather_late(flight, [mod], "gather_late_0")
    xs, hts, projs, ycats, outs, gathered_w = [x0], [], [], [], [], [gathered(flight, mod, 0)]
    for layer in range(DEPTH):
        wl = layer // 2
        even = layer % 2 == 0
        wg = gathered_w[layer]
        h_t, proj = _norm_proj(xs[-1], mod, norm_g3, wg[0], layer, f"norm_proj_{layer}")
        if layer + 1 < DEPTH:
            flight, (h_t,) = _gather_mid(next_flight, [], [h_t], f"gather_mid_{layer + 1}")
            if layer + 2 < DEPTH:
                next_flight, (h_t,) = _gather_start(zones[layer + 2], [], [h_t], f"gather_start_{layer + 2}")
        if even:
            ycat = _even_mix_fwd(proj, convw, ln_g3, ln_b3, ab_sgu_w, sgu_bcol, wl, [h_t], f"even_mix_fwd_{layer}")
        else:
            ycat = _pool_mix_fwd(proj, wg[2], pscale4, wl, [h_t], f"pool_mix_fwd_{layer}")
        if layer + 1 < DEPTH:
            flight = _gather_late(flight, [ycat], f"gather_late_{layer + 1}")
        if layer + 1 < DEPTH:
            x_new, out = _out_proj(ycat, wg[1], xs[-1], mod, layer, f"out_proj_{layer}")
            gathered_w.append(gathered(flight, x_new, layer + 1))
            xs.append(x_new)
        else:
            dx, out, loss_part, d_final_g = _out_proj(ycat, wg[1], xs[-1], mod, layer, f"out_proj_{layer}",
                                                      final=(target, final_g.reshape(1, D)))
        hts.append(h_t)
        projs.append(proj)
        ycats.append(ycat)
        outs.append(out)

    d_mod, d_norm_g = [None] * DEPTH, [None] * DEPTH
    small, scatters, landed, res = {}, {}, {}, {}

    def finish_scatter(layer, after):
        send_sems, recv_sems, chip_parts, zones = scatters[layer]
        landed[layer] = _scatter_end(chip_parts, zones, send_sems, recv_sems, after, f"scatter_end_{layer}")

    def flat(a):
        return a.reshape(a.shape[0], -1, a.shape[-1])

    def sharded_adam(k, j, layers, w, m, v, carried, first=0, into=()):
        outs4 = _adam_sharded(w, m, v, [flat(landed[l][0][j]) for l in layers], [flat(landed[l][1][j]) for l in layers],
                              my_chip, carried, f"adam_{k}_{first}" if len(layers) < w.shape[0] else "adam_" + k, first, into)
        res[k] = [o.reshape(c_pool_w.shape) if k == "c_pool_w" else o for o in outs4[:4]]
        return list(outs4[4:])

    previous = None
    for layer in reversed(range(DEPTH)):
        wl = layer // 2
        even = layer % 2 == 0
        wg = gathered_w[layer]
        carried = [] if previous is None else [scatters[previous][2][0]]
        d_ycat, grad_out, d_gate, carried = _out_bwd(dx, outs[layer], ycats[layer], wg[1], mod, layer, carried, f"out_bwd_{layer}")
        if previous is not None:
            scatters[previous][2][0] = carried[0]
        parts = [None, grad_out]
        if even:
            d_proj, d_cw, d_lg, d_lb, d_sw, d_sb = _even_mix_bwd(
                projs[layer], d_ycat, convw, ln_g3, ln_b3, ab_sgu_w, sgu_bcol, wl, f"even_mix_bwd_{layer}")
            small[layer] = (d_cw, d_lg, d_lb, d_sw, d_sb)
        else:
            d_proj, d_ps, d_pool = _pool_mix_bwd(projs[layer], d_ycat, wg[2], pscale4, wl, f"pool_mix_bwd_{layer}")
            small[layer] = (d_ps,)
            parts.append(d_pool)
        parts[0] = _weight_grad(hts[layer], d_proj, f"grad_w_in_{layer}")
        pair_send, pair_recv, parts, from_sibling = _pair_start(parts, f"pair_start_{layer}")
        if layer > 0:
            dx, d_shift, d_scale, d_norm_g[layer], parts[0] = _dh_norm_bwd(
                d_proj, wg[0], xs[layer], dx, mod, norm_g3, layer, parts[0], f"dh_norm_bwd_{layer}")
            pair_after = dx
        else:
            finish_scatter(1, d_proj)
            finish_scatter(3, d_proj)
            parts[0], = sharded_adam("c_w_out", 1, (1, 3), c_w_out, m_c_w_out, v_c_w_out, [parts[0]])
            parts[0], = sharded_adam("c_pool_w", 2, (1, 3), pool_w3, m_pool_w3, v_pool_w3, [parts[0]])
            pair_after = res["c_pool_w"][0]
        parts, from_sibling = _pair_end(parts, from_sibling, pair_send, pair_recv, pair_after, f"pair_end_{layer}")
        chip_parts = _pair_sum(parts, from_sibling, core, f"pair_sum_{layer}")
        send_sems, recv_sems, chip_parts, zones = _scatter_start(chip_parts, f"scatter_start_{layer}")
        if layer == 0:
            chip_parts[0], = sharded_adam("c_w_in", 0, (1, 3), c_w_in, m_c_w_in, v_c_w_in, [chip_parts[0]])
            dx, d_shift, d_scale, d_norm_g[layer], chip_parts[0] = _dh_norm_bwd(
                d_proj, wg[0], xs[layer], dx, mod, norm_g3, layer, chip_parts[0], f"dh_norm_bwd_{layer}")
        scatters[layer] = [send_sems, recv_sems, chip_parts, zones]
        previous = layer
        d_mod[layer] = jnp.concatenate([d_shift, d_scale, d_gate], axis=0)
    grad_x = dx.reshape(x.shape)

    sections = [("norm_g", jnp.concatenate(d_norm_g, axis=0), 8),
                ("d_mod", jnp.concatenate(d_mod, axis=0), 16),
                ("ab_ln_g", jnp.concatenate([small[0][1], small[2][1]], axis=0), 8),
                ("ab_ln_b", jnp.concatenate([small[0][2], small[2][2]], axis=0), 8),
                ("ab_sgu_b", jnp.stack([small[0][4], small[2][4]]), 8),
                ("final_g", d_final_g, 8),
                ("ab_conv_w", jnp.stack([small[0][0], small[2][0]]), 8),
                ("c_pool_scale", jnp.concatenate([small[1][0], small[3][0]], axis=0), 8),
                ("ab_sgu_w", jnp.stack([small[0][3], small[2][3]]), 256)]
    offsets, at = {}, 0
    for name, _, rows in sections:
        offsets[name] = (at, rows)
        at += rows
    packed = jnp.concatenate([_pad_rows(a, rows) for _, a, rows in sections] + [jnp.zeros((-at % 32, D), F32)], axis=0)
    loss_rows = jnp.pad(loss_part, ((0, 15), (0, D - LANE)))
    small_zones = [_to_zone(packed[None], 0, me1, BF16, "place_small_grads"), _to_zone(loss_rows[None], 0, me1, F32, "place_loss")]
    small_flight, (mod,) = _gather_start(small_zones, [], [mod], "gather_small_start")

    finish_scatter(2, mod)
    sharded_adam("ab_w_out", 1, (2,), ab_w_out, m_ab_w_out, v_ab_w_out, [], first=1)
    sharded_adam("ab_w_in", 0, (2,), ab_w_in, m_ab_w_in, v_ab_w_in, [], first=1)
    finish_scatter(0, res["ab_w_in"][0])
    sharded_adam("ab_w_out", 1, (0,), ab_w_out, m_ab_w_out, v_ab_w_out, [], into=res["ab_w_out"])
    small_flight, (mod,) = _gather_mid(small_flight, [res["ab_w_out"][0]], [mod], "gather_small_mid")
    sharded_adam("ab_w_in", 0, (0,), ab_w_in, m_ab_w_in, v_ab_w_in, [mod], into=res["ab_w_in"])

    last = res["ab_w_in"][0]
    small_flight = _gather_late(small_flight, [last], "gather_small_late")
    small_grads, losses = _gather_end(small_flight, [last], "gather_small_end")
    summed = _sum_devices(small_grads, "sum_small_grads")
    loss = _sum_devices(losses, "sum_loss")[0, 0]

    def section(name, nrows, src=summed):
        start = offsets[name][0]
        return src[..., start:start + nrows, :]

    grads = {
        "norm_g": section("norm_g", DEPTH),
        "ada_b": section("d_mod", 3 * DEPTH).reshape(DEPTH, 3 * D),
        "ab_ln_g": section("ab_ln_g", 2), "ab_ln_b": section("ab_ln_b", 2),
        "ab_sgu_b": section("ab_sgu_b", 2).reshape(ab_sgu_b.shape),
        "final_g": section("final_g", 1),
        "ab_sgu_w": section("ab_sgu_w", 256).reshape(ab_sgu_w.shape),
        "ab_conv_w": lax.dynamic_slice_in_dim(section("ab_conv_w", 6).reshape(2, 3, E_A), me * HEAD, HEAD, axis=2),
        "c_pool_scale": lax.dynamic_slice_in_dim(section("c_pool_scale", 4).reshape(2, 4 * G_C), me * 256, 256, axis=1),
    }
    small_w = {"norm_g": (norm_g, m_norm_g, v_norm_g), "ada_b": (ada_b, m_ada_b, v_ada_b),
               "ab_ln_g": (ab_ln_g, m_ab_ln_g, v_ab_ln_g), "ab_ln_b": (ab_ln_b, m_ab_ln_b, v_ab_ln_b),
               "ab_sgu_b": (ab_sgu_b, m_ab_sgu_b, v_ab_sgu_b),
               "final_g": (final_g.reshape(1, D), m_final_g.reshape(1, D), v_final_g.reshape(1, D)),
               "ab_sgu_w": (ab_sgu_w, m_ab_sgu_w, v_ab_sgu_w), "ab_conv_w": (ab_conv_w, m_ab_conv_w, v_ab_conv_w),
               "c_pool_scale": (c_pool_scale, m_c_pool_scale, v_c_pool_scale)}
    updates = _adam_small([(w, grads[k], m, v) for k, (w, m, v) in small_w.items()], "adam_small")
    for k, update in zip(small_w, updates):
        res[k] = [grads[k]] + update
    res["final_g"] = [a.reshape(D) for a in res["final_g"]]

    dmod_all = section("d_mod", 3 * DEPTH, small_grads).reshape(NDEV, DEPTH, 3 * D)
    dmod_mine = jnp.transpose(lax.dynamic_slice_in_dim(dmod_all, me * ADA_NC, ADA_NC, axis=2), (1, 0, 2)).astype(F32)
    res["ada_w"] = _ada_weight_adam(jnp.transpose(cact_all.reshape(NDEV, D)), dmod_mine, ada_w, m_ada_w, v_ada_w)

    order = ["norm_g", "ada_w", "ada_b", "ab_w_in", "ab_conv_w", "ab_ln_g", "ab_ln_b", "ab_sgu_w", "ab_sgu_b",
             "ab_w_out", "c_w_in", "c_pool_w", "c_pool_scale", "c_w_out", "final_g"]
    return (loss, grad_x, *[res[k][0] for k in order], *[res[k][1] for k in order],
            *[res[k][2] for k in order], *[res[k][3] for k in order])
```
